```python
import jax, jax.numpy as jnp
from jax import lax
import numpy as np

D_MODEL = 1024
BATCH = 8
SEQ = 8192
DEPTH = 2

N_META = 16
POOL_WINDOWS = (2, 4, 8, 16)
N_POOL_GROUPS = len(POOL_WINDOWS)
POOL_GROUP_DIM = D_MODEL // N_POOL_GROUPS
HEAD_DIM = 64
N_HEADS = D_MODEL // HEAD_DIM
D_FF = ((8 * D_MODEL // 3 + 127) // 128) * 128
CONV_WIDTH = 3
Q_BLOCK = 128
N_A_LAYERS = DEPTH // 2
N_B_LAYERS = DEPTH - N_A_LAYERS
ALPHA = (2.0 * DEPTH) ** 0.25
BETA = (8.0 * DEPTH) ** -0.25
LN_EPS = 1e-5
NEG_INF = -1e30

kernel_name = "yoco_pool_fox_convffn_deepnorm_meta"


def layer_norm(x, g, b):
    xf = x.astype(jnp.float32)
    mu = jnp.mean(xf, axis=-1, keepdims=True)
    xc = xf - mu
    var = jnp.mean(xc * xc, axis=-1, keepdims=True)
    y = xc * lax.rsqrt(var + LN_EPS) * g.astype(jnp.float32) + b.astype(jnp.float32)
    return y.astype(x.dtype)


def multiscale_pool_mixer(h, w_group, scale):
    b_, L, D = h.shape
    G = POOL_GROUP_DIM
    hf = h.astype(jnp.float32)
    cs0 = jnp.pad(jnp.cumsum(hf, axis=1), ((0, 0), (1, 0), (0, 0)))
    t = jnp.arange(1, L + 1, dtype=jnp.float32)
    outs = []
    for g, w in enumerate(POOL_WINDOWS):
        sl = slice(g * G, (g + 1) * G)
        upper = cs0[:, 1:, sl]
        lower = jnp.pad(cs0[:, :L + 1 - w, sl], ((0, 0), (w - 1, 0), (0, 0)))
        count = jnp.minimum(t, float(w))[None, :, None]
        outs.append((upper - lower) / count)
    pooled = jnp.concatenate(outs, axis=-1)
    diff = (pooled - hf).astype(h.dtype).reshape(b_, L, N_POOL_GROUPS, G)
    mixed = jnp.einsum('blgc,gcd->blgd', diff, w_group).reshape(b_, L, D)
    return mixed * scale


def conv_glu_ffn(h, w_in, conv_w, conv_b, w_out):
    L = h.shape[1]
    u = h @ w_in
    up = jnp.pad(u, ((0, 0), (CONV_WIDTH - 1, 0), (0, 0)))
    c = conv_b + sum(conv_w[k] * up[:, k:k + L] for k in range(CONV_WIDTH))
    a, g = jnp.split(c, 2, axis=-1)
    return (jax.nn.silu(a) * g) @ w_out


def padded_layout(L):
    front = (-N_META) % Q_BLOCK
    total = ((front + L + Q_BLOCK - 1) // Q_BLOCK) * Q_BLOCK
    return front, total


def shared_kv(h, w_kv, w_f, b_f):
    b_, L, D = h.shape
    front, Lp = padded_layout(L)
    pad = ((0, 0), (front, Lp - front - L), (0, 0), (0, 0))
    kv = h @ w_kv
    k = jnp.pad(kv[..., :D].reshape(b_, L, N_HEADS, HEAD_DIM), pad).transpose(0, 2, 1, 3)
    v = jnp.pad(kv[..., D:].reshape(b_, L, N_HEADS, HEAD_DIM), pad).transpose(0, 2, 1, 3)
    logf = jax.nn.log_sigmoid((h @ w_f).astype(jnp.float32) + b_f.astype(jnp.float32))
    logf = jnp.pad(logf, ((0, 0), (front, Lp - front - L), (0, 0)))
    c = jnp.cumsum(logf, axis=1).transpose(0, 2, 1)
    return k, v, c


def forgetting_attention(h, w_q, w_o, k, v, c):
    b_, L, D = h.shape
    front, Lp = padded_layout(L)
    nb = Lp // Q_BLOCK
    q = (h @ w_q).reshape(b_, L, N_HEADS, HEAD_DIM)
    q = jnp.pad(q, ((0, 0), (front, Lp - front - L), (0, 0), (0, 0)))
    qb = q.reshape(b_, nb, Q_BLOCK, N_HEADS, HEAD_DIM).transpose(1, 0, 3, 2, 4)
    cq = c.reshape(b_, N_HEADS, nb, Q_BLOCK).transpose(2, 0, 1, 3)
    kpos = jnp.arange(Lp)
    scale = HEAD_DIM ** -0.5

    def block(args):
        i, q_i, cq_i = args
        qpos = i * Q_BLOCK + jnp.arange(Q_BLOCK)
        s = jnp.einsum('bhqd,bhkd->bhqk', q_i, k, preferred_element_type=jnp.float32) * scale
        s = s + cq_i[..., None] - c[:, :, None, :]
        mask = (kpos[None, :] <= qpos[:, None]) & (kpos[None, :] >= front)
        p = jax.nn.softmax(jnp.where(mask, s, NEG_INF), axis=-1)
        return jnp.einsum('bhqk,bhkd->bhqd', p.astype(v.dtype), v)

    o = lax.map(block, (jnp.arange(nb), qb, cq))
    o = o.transpose(1, 0, 3, 2, 4).reshape(b_, Lp, D)[:, front:front + L]
    return o @ w_o


def _fwd_setup_inputs(seed: int = 0) -> dict:
    key = jax.random.key(seed)
    ks = jax.random.split(key, 16)
    D, F, G, H = D_MODEL, D_FF, POOL_GROUP_DIM, N_HEADS
    nrm = jax.random.normal
    return {
        "x": nrm(ks[0], (BATCH, SEQ, D), jnp.float32),
        "meta": nrm(ks[1], (N_META, D), jnp.float32),
        "pool_w": nrm(ks[2], (N_A_LAYERS, N_POOL_GROUPS, G, G), jnp.float32) * (G ** -0.5) * BETA,
        "pool_scale": 1.0 + 0.02 * nrm(ks[3], (N_A_LAYERS, D), jnp.float32),
        "w_kv": nrm(ks[4], (D, 2 * D), jnp.float32) * (D ** -0.5),
        "w_f": nrm(ks[5], (D, H), jnp.float32) * (D ** -0.5),
        "b_f": jax.random.uniform(ks[6], (H,), jnp.float32, 1.0, 6.0),
        "w_q": nrm(ks[7], (N_B_LAYERS, D, D), jnp.float32) * (D ** -0.5),
        "w_o": nrm(ks[8], (N_B_LAYERS, D, D), jnp.float32) * (D ** -0.5) * BETA,
        "ffn_w_in": nrm(ks[9], (DEPTH, D, 2 * F), jnp.float32) * (D ** -0.5),
        "ffn_conv_w": nrm(ks[10], (DEPTH, CONV_WIDTH, 2 * F), jnp.float32) * (CONV_WIDTH ** -0.5),
        "ffn_conv_b": 0.02 * nrm(ks[11], (DEPTH, 2 * F), jnp.float32),
        "ffn_w_out": nrm(ks[12], (DEPTH, F, D), jnp.float32) * (F ** -0.5) * BETA,
        "ln_g": 1.0 + 0.02 * nrm(ks[13], (DEPTH, 2, D), jnp.float32),
        "ln_b": 0.02 * nrm(ks[14], (DEPTH, 2, D), jnp.float32),
    }


def _fwd_reference(x, meta, pool_w, pool_scale, w_kv, w_f, b_f, w_q, w_o, ffn_w_in, ffn_conv_w,
              ffn_conv_b, ffn_w_out, ln_g, ln_b):
    b_ = x.shape[0]
    h = jnp.concatenate(
        [jnp.broadcast_to(meta[None].astype(x.dtype), (b_, N_META, D_MODEL)), x], axis=1)
    shared = None
    for i in range(DEPTH):
        if i < N_A_LAYERS:
            mix = multiscale_pool_mixer(h, pool_w[i], pool_scale[i])
        else:
            if i == N_A_LAYERS:
                shared = shared_kv(h, w_kv, w_f, b_f)
            j = i - N_A_LAYERS
            mix = forgetting_attention(h, w_q[j], w_o[j], shared[0], shared[1], shared[2])
        h = layer_norm(ALPHA * h + mix, ln_g[i, 0], ln_b[i, 0])
        ffn = conv_glu_ffn(h, ffn_w_in[i], ffn_conv_w[i], ffn_conv_b[i], ffn_w_out[i])
        h = layer_norm(ALPHA * h + ffn, ln_g[i, 1], ln_b[i, 1])
    return h[:, N_META:]


import jax as _jax
import jax.numpy as _jnp

TWIN_FORMAT = 'train_step'
FWD_PARAMS = ['x', 'meta', 'pool_w', 'pool_scale', 'w_kv', 'w_f', 'b_f', 'w_q', 'w_o', 'ffn_w_in', 'ffn_conv_w', 'ffn_conv_b', 'ffn_w_out', 'ln_g', 'ln_b']
TWIN_WEIGHTS = ['meta', 'pool_w', 'pool_scale', 'w_kv', 'w_f', 'b_f', 'w_q', 'w_o', 'ffn_w_in', 'ffn_conv_w', 'ffn_conv_b', 'ffn_w_out', 'ln_g', 'ln_b']
TWIN_DIFF_INPUT = 'x'
TWIN_INPUTS = ['x', 'meta', 'pool_w', 'pool_scale', 'w_kv', 'w_f', 'b_f', 'w_q', 'w_o', 'ffn_w_in', 'ffn_conv_w', 'ffn_conv_b', 'ffn_w_out', 'ln_g', 'ln_b', 'loss_target', 'm_meta', 'm_pool_w', 'm_pool_scale', 'm_w_kv', 'm_w_f', 'm_b_f', 'm_w_q', 'm_w_o', 'm_ffn_w_in', 'm_ffn_conv_w', 'm_ffn_conv_b', 'm_ffn_w_out', 'm_ln_g', 'm_ln_b', 'v_meta', 'v_pool_w', 'v_pool_scale', 'v_w_kv', 'v_w_f', 'v_b_f', 'v_w_q', 'v_w_o', 'v_ffn_w_in', 'v_ffn_conv_w', 'v_ffn_conv_b', 'v_ffn_w_out', 'v_ln_g', 'v_ln_b']
TWIN_OUTPUTS = ['loss', 'grad_x', 'grad_meta', 'grad_pool_w', 'grad_pool_scale', 'grad_w_kv', 'grad_w_f', 'grad_b_f', 'grad_w_q', 'grad_w_o', 'grad_ffn_w_in', 'grad_ffn_conv_w', 'grad_ffn_conv_b', 'grad_ffn_w_out', 'grad_ln_g', 'grad_ln_b', 'delta_meta', 'delta_pool_w', 'delta_pool_scale', 'delta_w_kv', 'delta_w_f', 'delta_b_f', 'delta_w_q', 'delta_w_o', 'delta_ffn_w_in', 'delta_ffn_conv_w', 'delta_ffn_conv_b', 'delta_ffn_w_out', 'delta_ln_g', 'delta_ln_b', 'new_m_meta', 'new_m_pool_w', 'new_m_pool_scale', 'new_m_w_kv', 'new_m_w_f', 'new_m_b_f', 'new_m_w_q', 'new_m_w_o', 'new_m_ffn_w_in', 'new_m_ffn_conv_w', 'new_m_ffn_conv_b', 'new_m_ffn_w_out', 'new_m_ln_g', 'new_m_ln_b', 'new_v_meta', 'new_v_pool_w', 'new_v_pool_scale', 'new_v_w_kv', 'new_v_w_f', 'new_v_b_f', 'new_v_w_q', 'new_v_w_o', 'new_v_ffn_w_in', 'new_v_ffn_conv_w', 'new_v_ffn_conv_b', 'new_v_ffn_w_out', 'new_v_ln_g', 'new_v_ln_b']
TWIN_LEAF_KINDS = {'loss': 'loss', 'grad_x': 'grad_x', 'grad_meta': 'grad_w', 'grad_pool_w': 'grad_w', 'grad_pool_scale': 'grad_w', 'grad_w_kv': 'grad_w', 'grad_w_f': 'grad_w', 'grad_b_f': 'grad_w', 'grad_w_q': 'grad_w', 'grad_w_o': 'grad_w', 'grad_ffn_w_in': 'grad_w', 'grad_ffn_conv_w': 'grad_w', 'grad_ffn_conv_b': 'grad_w', 'grad_ffn_w_out': 'grad_w', 'grad_ln_g': 'grad_w', 'grad_ln_b': 'grad_w', 'delta_meta': 'delta_w', 'delta_pool_w': 'delta_w', 'delta_pool_scale': 'delta_w', 'delta_w_kv': 'delta_w', 'delta_w_f': 'delta_w', 'delta_b_f': 'delta_w', 'delta_w_q': 'delta_w', 'delta_w_o': 'delta_w', 'delta_ffn_w_in': 'delta_w', 'delta_ffn_conv_w': 'delta_w', 'delta_ffn_conv_b': 'delta_w', 'delta_ffn_w_out': 'delta_w', 'delta_ln_g': 'delta_w', 'delta_ln_b': 'delta_w', 'new_m_meta': 'new_m', 'new_m_pool_w': 'new_m', 'new_m_pool_scale': 'new_m', 'new_m_w_kv': 'new_m', 'new_m_w_f': 'new_m', 'new_m_b_f': 'new_m', 'new_m_w_q': 'new_m', 'new_m_w_o': 'new_m', 'new_m_ffn_w_in': 'new_m', 'new_m_ffn_conv_w': 'new_m', 'new_m_ffn_conv_b': 'new_m', 'new_m_ffn_w_out': 'new_m', 'new_m_ln_g': 'new_m', 'new_m_ln_b': 'new_m', 'new_v_meta': 'new_v', 'new_v_pool_w': 'new_v', 'new_v_pool_scale': 'new_v', 'new_v_w_kv': 'new_v', 'new_v_w_f': 'new_v', 'new_v_b_f': 'new_v', 'new_v_w_q': 'new_v', 'new_v_w_o': 'new_v', 'new_v_ffn_w_in': 'new_v', 'new_v_ffn_conv_w': 'new_v', 'new_v_ffn_conv_b': 'new_v', 'new_v_ffn_w_out': 'new_v', 'new_v_ln_g': 'new_v', 'new_v_ln_b': 'new_v'}


def _forward(args):
    return _fwd_reference(*[args[k] for k in FWD_PARAMS])


def _output_shape():
    def fwd():
        inp = _fwd_setup_inputs(0)
        return _fwd_reference(*[inp[k] for k in FWD_PARAMS])
    out = _jax.eval_shape(fwd)
    return out.shape, out.dtype

N_MICROBATCH = 1
ADAM_LR = 0.001
ADAM_B1 = 0.9
ADAM_B2 = 0.999
ADAM_EPS = 1e-08
ADAM_WD = 0.01
ADAM_STEP = 10
PER_EXAMPLE_BATCH_AXIS = {'x': 0, 'loss_target': 0}
SHARED_INPUTS = []
_WEIGHT_DTYPES = {'meta': _jnp.float32, 'pool_w': _jnp.float32, 'pool_scale': _jnp.float32, 'w_kv': _jnp.float32, 'w_f': _jnp.float32, 'b_f': _jnp.float32, 'w_q': _jnp.float32, 'w_o': _jnp.float32, 'ffn_w_in': _jnp.float32, 'ffn_conv_w': _jnp.float32, 'ffn_conv_b': _jnp.float32, 'ffn_w_out': _jnp.float32, 'ln_g': _jnp.float32, 'ln_b': _jnp.float32}
MOMENT_SCALE = {'meta': 2.918412e-03, 'pool_w': 1.611187e-01, 'pool_scale': 1.417895e-01, 'w_kv': 2.278054e-02, 'w_f': 4.151921e-02, 'b_f': 1.038300e-01, 'w_q': 2.052813e-02, 'w_o': 4.955011e-02, 'ffn_w_in': 3.311107e-02, 'ffn_conv_w': 3.317172e-02, 'ffn_conv_b': 4.217842e-02, 'ffn_w_out': 1.079982e-01, 'ln_g': 3.210657e+01, 'ln_b': 1.410562e+00}


def _to_microbatches(a, axis):
    t = _jnp.moveaxis(a, axis, 0)
    t = t.reshape((N_MICROBATCH, t.shape[0] // N_MICROBATCH) + t.shape[1:])
    return _jnp.moveaxis(t, 1, axis + 1)


def setup_inputs(seed: int = 0) -> dict:
    inp = _fwd_setup_inputs(seed)
    key = _jax.random.fold_in(_jax.random.key(seed), 7919)
    shape, _ = _output_shape()
    out = dict(inp)
    out["loss_target"] = _jax.random.normal(_jax.random.fold_in(key, 0), shape, _jnp.float32)
    for i, name in enumerate(TWIN_WEIGHTS):
        w = inp[name].astype(_jnp.float32)
        if MOMENT_SCALE is None:
            s = _jnp.sqrt(_jnp.mean(_jnp.square(w)) + 1e-30)
        else:
            s = MOMENT_SCALE[name]
        km, kv = _jax.random.split(_jax.random.fold_in(key, i + 1))
        out[name] = w
        out["m_" + name] = s * _jax.random.normal(km, w.shape, _jnp.float32)
        out["v_" + name] = (s * s) * _jax.random.uniform(kv, w.shape, _jnp.float32, 0.5, 1.5)
    if N_MICROBATCH > 1:
        for name, axis in PER_EXAMPLE_BATCH_AXIS.items():
            out[name] = _to_microbatches(out[name], axis)
    return {'x': out['x'], 'meta': out['meta'], 'pool_w': out['pool_w'], 'pool_scale': out['pool_scale'], 'w_kv': out['w_kv'], 'w_f': out['w_f'], 'b_f': out['b_f'], 'w_q': out['w_q'], 'w_o': out['w_o'], 'ffn_w_in': out['ffn_w_in'], 'ffn_conv_w': out['ffn_conv_w'], 'ffn_conv_b': out['ffn_conv_b'], 'ffn_w_out': out['ffn_w_out'], 'ln_g': out['ln_g'], 'ln_b': out['ln_b'], 'loss_target': out['loss_target'], 'm_meta': out['m_meta'], 'm_pool_w': out['m_pool_w'], 'm_pool_scale': out['m_pool_scale'], 'm_w_kv': out['m_w_kv'], 'm_w_f': out['m_w_f'], 'm_b_f': out['m_b_f'], 'm_w_q': out['m_w_q'], 'm_w_o': out['m_w_o'], 'm_ffn_w_in': out['m_ffn_w_in'], 'm_ffn_conv_w': out['m_ffn_conv_w'], 'm_ffn_conv_b': out['m_ffn_conv_b'], 'm_ffn_w_out': out['m_ffn_w_out'], 'm_ln_g': out['m_ln_g'], 'm_ln_b': out['m_ln_b'], 'v_meta': out['v_meta'], 'v_pool_w': out['v_pool_w'], 'v_pool_scale': out['v_pool_scale'], 'v_w_kv': out['v_w_kv'], 'v_w_f': out['v_w_f'], 'v_b_f': out['v_b_f'], 'v_w_q': out['v_w_q'], 'v_w_o': out['v_w_o'], 'v_ffn_w_in': out['v_ffn_w_in'], 'v_ffn_conv_w': out['v_ffn_conv_w'], 'v_ffn_conv_b': out['v_ffn_conv_b'], 'v_ffn_w_out': out['v_ffn_w_out'], 'v_ln_g': out['v_ln_g'], 'v_ln_b': out['v_ln_b']}


def _loss(weights, diff, rest, loss_target):
    with _jax.named_scope("forward"):
        args = {**rest, TWIN_DIFF_INPUT: diff, **{k: w.astype(_WEIGHT_DTYPES[k]) for k, w in weights.items()}}
        y = _forward(args)
    with _jax.named_scope("loss_head"):
        err = _jnp.square(y.astype(_jnp.float32) - loss_target)
        return 0.5 * _jnp.sum(_jnp.mean(err, axis=-1)) if err.ndim else 0.5 * err


def _adamw(w, g, m, v):
    m = ADAM_B1 * m + (1.0 - ADAM_B1) * g
    v = ADAM_B2 * v + (1.0 - ADAM_B2) * _jnp.square(g)
    m_hat = m / (1.0 - ADAM_B1 ** ADAM_STEP)
    v_hat = v / (1.0 - ADAM_B2 ** ADAM_STEP)
    delta = -ADAM_LR * (m_hat / (_jnp.sqrt(v_hat) + ADAM_EPS) + ADAM_WD * w)
    return delta, m, v


def reference(x, meta, pool_w, pool_scale, w_kv, w_f, b_f, w_q, w_o, ffn_w_in, ffn_conv_w, ffn_conv_b, ffn_w_out, ln_g, ln_b, loss_target, m_meta, m_pool_w, m_pool_scale, m_w_kv, m_w_f, m_b_f, m_w_q, m_w_o, m_ffn_w_in, m_ffn_conv_w, m_ffn_conv_b, m_ffn_w_out, m_ln_g, m_ln_b, v_meta, v_pool_w, v_pool_scale, v_w_kv, v_w_f, v_b_f, v_w_q, v_w_o, v_ffn_w_in, v_ffn_conv_w, v_ffn_conv_b, v_ffn_w_out, v_ln_g, v_ln_b):
    given = dict(x=x, meta=meta, pool_w=pool_w, pool_scale=pool_scale, w_kv=w_kv, w_f=w_f, b_f=b_f, w_q=w_q, w_o=w_o, ffn_w_in=ffn_w_in, ffn_conv_w=ffn_conv_w, ffn_conv_b=ffn_conv_b, ffn_w_out=ffn_w_out, ln_g=ln_g, ln_b=ln_b, loss_target=loss_target, m_meta=m_meta, m_pool_w=m_pool_w, m_pool_scale=m_pool_scale, m_w_kv=m_w_kv, m_w_f=m_w_f, m_b_f=m_b_f, m_w_q=m_w_q, m_w_o=m_w_o, m_ffn_w_in=m_ffn_w_in, m_ffn_conv_w=m_ffn_conv_w, m_ffn_conv_b=m_ffn_conv_b, m_ffn_w_out=m_ffn_w_out, m_ln_g=m_ln_g, m_ln_b=m_ln_b, v_meta=v_meta, v_pool_w=v_pool_w, v_pool_scale=v_pool_scale, v_w_kv=v_w_kv, v_w_f=v_w_f, v_b_f=v_b_f, v_w_q=v_w_q, v_w_o=v_w_o, v_ffn_w_in=v_ffn_w_in, v_ffn_conv_w=v_ffn_conv_w, v_ffn_conv_b=v_ffn_conv_b, v_ffn_w_out=v_ffn_w_out, v_ln_g=v_ln_g, v_ln_b=v_ln_b)
    weights = {n: given[n] for n in TWIN_WEIGHTS}
    shared = {n: given[n] for n in SHARED_INPUTS}
    per_example = {n: given[n] for n in ['x']}
    grad_fn = _jax.value_and_grad(_loss, argnums=(0, 1))

    def one_microbatch(ex, loss_target):
        ex = dict(ex)
        diff = ex.pop(TWIN_DIFF_INPUT)
        return grad_fn(weights, diff, {**shared, **ex}, loss_target)

    if N_MICROBATCH == 1:
        loss, (grad_w, grad_x) = one_microbatch(per_example, given["loss_target"])
    else:
        def body(carry, xs):
            loss_sum, grad_sum = carry
            l_k, (gw_k, gx_k) = one_microbatch(xs[0], xs[1])
            with _jax.named_scope("update"):
                return (loss_sum + l_k, _jax.tree.map(_jnp.add, grad_sum, gw_k)), gx_k

        init = (_jnp.zeros((), _jnp.float32), _jax.tree.map(_jnp.zeros_like, weights))
        (loss, grad_w), grad_x = _jax.lax.scan(body, init, (per_example, given["loss_target"]))
    with _jax.named_scope("update"):
        delta_w, new_m, new_v = {}, {}, {}
        for n in TWIN_WEIGHTS:
            delta_w[n], new_m[n], new_v[n] = _adamw(weights[n], grad_w[n], given["m_" + n], given["v_" + n])
    return (loss, grad_x, *[grad_w[n] for n in TWIN_WEIGHTS], *[delta_w[n] for n in TWIN_WEIGHTS],
            *[new_m[n] for n in TWIN_WEIGHTS], *[new_v[n] for n in TWIN_WEIGHTS])
```

```python
import jax
import jax.numpy as jnp
from jax import lax
from jax.experimental import pallas as pl
from jax.experimental.pallas import tpu as pltpu

F32, BF16 = jnp.float32, jnp.bfloat16
MESH = pl.DeviceIdType.MESH

N_DEV = 8
N_META = 16
POOL_WINDOWS = (2, 4, 8, 16)
POOL_HALO = 16
CONV_HALO = 8
ALPHA = 4.0 ** 0.25
LN_EPS = 1e-5
NEG_INF = -1e30
ADAM_LR, ADAM_B1, ADAM_B2, ADAM_EPS, ADAM_WD, ADAM_STEP = 0.001, 0.9, 0.999, 1e-08, 0.01, 10

LANES = 128
PACK_COLS = 1024
ADAM_ROWS = 128
GLU_CHUNK = 256
VMEM_LIMIT = 56 * 1024 * 1024

PARAMS = (("meta", 1), ("pool_w", 2), ("pool_scale", 1), ("w_kv", 1), ("w_f", 0), ("b_f", None),
          ("w_q", 1), ("w_o", 1), ("ffn_w_in", 2), ("ffn_conv_w", 2), ("ffn_conv_b", None),
          ("ffn_w_out", 1), ("ln_g", 2), ("ln_b", 2))
GATHER_BF16 = ("pool_w", "w_kv", "w_f", "w_q", "w_o", "ffn_w_in", "ffn_w_out")
GATHER_F32 = ("meta", "pool_scale", "ffn_conv_w", "ln_g", "ln_b")


def _cparams(**kw):
    return pltpu.CompilerParams(vmem_limit_bytes=VMEM_LIMIT, **kw)


def _pick(n, cands):
    for c in cands:
        if n % c == 0:
            return c
    return n


def _round_up(n, m):
    return (n + m - 1) // m * m


def _pack(pieces, dtype, row_multiple):
    flat = []
    for p in pieces:
        v = p.astype(dtype).reshape(-1)
        flat.append(jnp.pad(v, (0, _round_up(v.size, PACK_COLS) - v.size)))
    v = jnp.concatenate(flat)
    rows = _round_up(v.size // PACK_COLS, row_multiple)
    v = jnp.pad(v, (0, rows * PACK_COLS - v.size))
    return v.reshape(rows, PACK_COLS)


def _unpack(buf, shapes, lead=()):
    flat = buf.reshape(lead + (-1,))
    out, off = [], 0
    for s in shapes:
        n = 1
        for d in s:
            n *= d
        out.append(flat[..., off:off + n].reshape(lead + tuple(s)))
        off += _round_up(n, PACK_COLS)
    return out


def _to_blocks(full, axis):
    if axis is None:
        return jnp.broadcast_to(full.reshape(1, -1), (N_DEV, full.size))
    s = full.shape
    x = full.reshape(s[:axis] + (N_DEV, s[axis] // N_DEV) + s[axis + 1:])
    return jnp.moveaxis(x, axis, 0).reshape(N_DEV, -1)


def _from_blocks(blocks, axis):
    x = jnp.moveaxis(blocks, 0, axis)
    s = x.shape
    return x.reshape(s[:axis] + (s[axis] * s[axis + 1],) + s[axis + 2:])


def _coords():
    return lax.axis_index("x"), lax.axis_index("y"), lax.axis_index("c")


def _flip(pos, mask):
    x, y, c = pos
    return (1 - x if mask & 4 else x, 1 - y if mask & 2 else y, 1 - c if mask & 1 else c)


def _index(pos):
    x, y, c = pos
    return 4 * x + 2 * y + c


def _all_gather(block, name):
    chip_masks = (4, 2, 6)

    def body(x_ref, out_ref, send_sems, recv_sems, local_sem):
        me = _coords()
        sibling = _flip(me, 1)

        def copy(k, owner, to, src=None):
            slot = out_ref.at[_index(owner)]
            return pltpu.make_async_remote_copy(
                src_ref=slot if src is None else src, dst_ref=slot,
                send_sem=send_sems.at[k], recv_sem=recv_sems.at[k], device_id=to, device_id_type=MESH)

        mine = pltpu.make_async_copy(x_ref, out_ref.at[_index(me)], local_sem)
        mine.start()
        first = [copy(0, me, sibling, src=x_ref)]
        first += [copy(1 + j, me, _flip(me, m), src=x_ref) for j, m in enumerate(chip_masks)]
        for cp in first:
            cp.start()
        passed = [copy(4 + j, _flip(me, m), sibling) for j, m in enumerate(chip_masks)]
        for j, m in enumerate(chip_masks):
            copy(1 + j, _flip(me, m), me).wait_recv()
            passed[j].start()
        copy(0, sibling, me).wait_recv()
        for j, m in enumerate(chip_masks):
            copy(4 + j, _flip(sibling, m), me).wait_recv()
        for cp in first + passed:
            cp.wait_send()
        mine.wait()

    return pl.pallas_call(
        body, name=name,
        out_shape=jax.ShapeDtypeStruct((N_DEV,) + block.shape, block.dtype),
        in_specs=[pl.BlockSpec(memory_space=pl.ANY)],
        out_specs=pl.BlockSpec(memory_space=pl.ANY),
        scratch_shapes=[pltpu.SemaphoreType.DMA((7,)), pltpu.SemaphoreType.DMA((7,)), pltpu.SemaphoreType.DMA],
    )(block)


def _exchange(seg, name):
    def body(seg_ref, out_ref, send_sems, recv_sems, local_sem):
        me = _coords()
        mine = pltpu.make_async_copy(seg_ref.at[_index(me)], out_ref.at[_index(me)], local_sem)
        mine.start()
        sends = []
        for mask in range(1, N_DEV):
            peer = _flip(me, mask)
            sends.append(pltpu.make_async_remote_copy(
                src_ref=seg_ref.at[_index(peer)], dst_ref=out_ref.at[_index(me)],
                send_sem=send_sems.at[mask - 1], recv_sem=recv_sems.at[mask - 1],
                device_id=peer, device_id_type=MESH))
        for cp in sends:
            cp.start()
        for mask in range(1, N_DEV):
            peer = _flip(me, mask)
            pltpu.make_async_remote_copy(
                src_ref=seg_ref.at[_index(me)], dst_ref=out_ref.at[_index(peer)],
                send_sem=send_sems.at[mask - 1], recv_sem=recv_sems.at[mask - 1],
                device_id=peer, device_id_type=MESH).wait_recv()
        for cp in sends:
            cp.wait_send()
        mine.wait()

    return pl.pallas_call(
        body, name=name,
        out_shape=jax.ShapeDtypeStruct(seg.shape, seg.dtype),
        in_specs=[pl.BlockSpec(memory_space=pl.ANY)],
        out_specs=pl.BlockSpec(memory_space=pl.ANY),
        scratch_shapes=[pltpu.SemaphoreType.DMA((7,)), pltpu.SemaphoreType.DMA((7,)), pltpu.SemaphoreType.DMA],
    )(seg)


def _adamw(recv, w, m, v, name):
    rows = w.shape[0]
    c1 = 1.0 - ADAM_B1 ** ADAM_STEP
    c2 = 1.0 - ADAM_B2 ** ADAM_STEP

    def body(r_ref, w_ref, m_ref, v_ref, g_out, d_out, m_out, v_out):
        g = r_ref[0]
        for s in range(1, N_DEV):
            g = g + r_ref[s]
        m_new = ADAM_B1 * m_ref[...] + (1.0 - ADAM_B1) * g
        v_new = ADAM_B2 * v_ref[...] + (1.0 - ADAM_B2) * (g * g)
        m_hat = m_new / c1
        v_hat = v_new / c2
        g_out[...] = g
        d_out[...] = -ADAM_LR * (m_hat / (jnp.sqrt(v_hat) + ADAM_EPS) + ADAM_WD * w_ref[...])
        m_out[...] = m_new
        v_out[...] = v_new

    tile = pl.BlockSpec((ADAM_ROWS, PACK_COLS), lambda i: (i, 0))
    return pl.pallas_call(
        body, name=name, grid=(rows // ADAM_ROWS,),
        in_specs=[pl.BlockSpec((N_DEV, ADAM_ROWS, PACK_COLS), lambda i: (0, i, 0)), tile, tile, tile],
        out_specs=[tile] * 4,
        out_shape=[jax.ShapeDtypeStruct(w.shape, F32)] * 4,
        compiler_params=_cparams(),
    )(recv, w, m, v)


def _mm(a, b, *, name, out_dtype, res=None, res_scale=1.0):
    M, K = a.shape
    N = b.shape[1]
    tm = _pick(M, (640, 128))
    tn = _pick(N, (1408, 1024, 640, 512, 256, 128))
    tk = K if K <= 1024 else _pick(K, (1024, 640, 512, 256, 128))
    nk = K // tk

    def body(*refs):
        a_ref, b_ref = refs[0], refs[1]
        r_ref = refs[2] if res is not None else None
        o_ref = refs[3] if res is not None else refs[2]
        acc_ref = refs[-1] if nk > 1 else None

        def finish(acc):
            if r_ref is not None:
                acc = acc + res_scale * r_ref[...]
            o_ref[...] = acc.astype(out_dtype)

        prod = jnp.dot(a_ref[...], b_ref[...], preferred_element_type=F32)
        if nk == 1:
            finish(prod)
        else:
            k = pl.program_id(2)

            @pl.when(k == 0)
            def _():
                acc_ref[...] = prod

            @pl.when(k > 0)
            def _():
                acc_ref[...] += prod

            @pl.when(k == nk - 1)
            def _():
                finish(acc_ref[...])

    in_specs = [pl.BlockSpec((tm, tk), lambda i, j, k: (i, k)), pl.BlockSpec((tk, tn), lambda i, j, k: (k, j))]
    args = [a, b]
    if res is not None:
        in_specs.append(pl.BlockSpec((tm, tn), lambda i, j, k: (i, j)))
        args.append(res)
    return pl.pallas_call(
        body, name=name, grid=(M // tm, N // tn, nk),
        in_specs=in_specs, out_specs=pl.BlockSpec((tm, tn), lambda i, j, k: (i, j)),
        out_shape=jax.ShapeDtypeStruct((M, N), out_dtype),
        scratch_shapes=[pltpu.VMEM((tm, tn), F32)] if nk > 1 else [],
        compiler_params=_cparams(),
    )(*args)


def _mm_tn(a, b, *, name):
    T, M = a.shape
    N = b.shape[1]
    tt = _pick(T, (1664, 640, 128))
    tm = _pick(M, (1408, 1024, 512, 256, 128))
    tn = _pick(N, (1024, 640, 512, 256, 128))

    def body(a_ref, b_ref, o_ref):
        prod = lax.dot_general(a_ref[...], b_ref[...], (((0,), (0,)), ((), ())), preferred_element_type=F32)

        @pl.when(pl.program_id(2) == 0)
        def _():
            o_ref[...] = prod

        @pl.when(pl.program_id(2) > 0)
        def _():
            o_ref[...] += prod

    return pl.pallas_call(
        body, name=name, grid=(M // tm, N // tn, T // tt),
        in_specs=[pl.BlockSpec((tt, tm), lambda i, j, k: (k, i)), pl.BlockSpec((tt, tn), lambda i, j, k: (k, j))],
        out_specs=pl.BlockSpec((tm, tn), lambda i, j, k: (i, j)),
        out_shape=jax.ShapeDtypeStruct((M, N), F32),
        compiler_params=_cparams(),
    )(a, b)


def _layer_norm(z, g, b):
    mu = jnp.mean(z, axis=-1, keepdims=True)
    xc = z - mu
    var = jnp.mean(xc * xc, axis=-1, keepdims=True)
    return xc * lax.rsqrt(var + LN_EPS) * g + b


def _mm_ln(a, w, res, g, b, *, name, target=None, n_tok=None):
    M, K = a.shape
    D = w.shape[1]
    tm = _pick(M, (640, 128))
    final = target is not None

    def body(*refs):
        if final:
            a_ref, w_ref, r_ref, g_ref, b_ref, t_ref, z_ref, dy_ref, loss_ref = refs
        else:
            a_ref, w_ref, r_ref, g_ref, b_ref, z_ref, h_ref, hb_ref = refs
        z = ALPHA * r_ref[...] + jnp.dot(a_ref[...], w_ref[...], preferred_element_type=F32)
        z_ref[...] = z
        h = _layer_norm(z, g_ref[...], b_ref[...])
        if not final:
            h_ref[...] = h
            hb_ref[...] = h.astype(BF16)
            return
        i = pl.program_id(0)
        row = i * tm + lax.broadcasted_iota(jnp.int32, (tm, 1), 0)
        valid = (row >= N_META) & (row < n_tok)
        err = jnp.where(valid, h - t_ref[...], 0.0)
        dy_ref[...] = err / D

        @pl.when(i == 0)
        def _():
            loss_ref[...] = jnp.zeros_like(loss_ref)

        loss_ref[...] += 0.5 * jnp.sum(jnp.sum(err * err, axis=1, keepdims=True) / D, axis=0, keepdims=True)

    row_blk = lambda cols: pl.BlockSpec((tm, cols), lambda i: (i, 0))
    vec = pl.BlockSpec((1, D), lambda i: (0, 0))
    in_specs = [row_blk(K), pl.BlockSpec((K, D), lambda i: (0, 0)), row_blk(D), vec, vec]
    args = [a, w, res, g, b]
    if final:
        in_specs.append(row_blk(D))
        args.append(target)
        out_specs = [row_blk(D), row_blk(D), pl.BlockSpec((8, LANES), lambda i: (0, 0))]
        out_shape = [jax.ShapeDtypeStruct((M, D), F32)] * 2 + [jax.ShapeDtypeStruct((8, LANES), F32)]
    else:
        out_specs = [row_blk(D)] * 3
        out_shape = [jax.ShapeDtypeStruct((M, D), F32)] * 2 + [jax.ShapeDtypeStruct((M, D), BF16)]
    return pl.pallas_call(
        body, name=name, grid=(M // tm,), in_specs=in_specs, out_specs=out_specs, out_shape=out_shape,
        compiler_params=_cparams(),
    )(*args)


def _ln_bwd(dh, z, g, *, name, dz_next=None):
    M, D = z.shape
    tm = _pick(M, (640, 128))
    has_next = dz_next is not None

    def body(*refs):
        if has_next:
            dh_ref, nx_ref, z_ref, g_ref, dz_ref, dzb_ref, dg_ref, db_ref = refs
            dh_v = dh_ref[...] + ALPHA * nx_ref[...]
        else:
            dh_ref, z_ref, g_ref, dz_ref, dzb_ref, dg_ref, db_ref = refs
            dh_v = dh_ref[...]
        z_v = z_ref[...]
        mu = jnp.mean(z_v, axis=-1, keepdims=True)
        xc = z_v - mu
        rstd = lax.rsqrt(jnp.mean(xc * xc, axis=-1, keepdims=True) + LN_EPS)
        xhat = xc * rstd
        dxhat = dh_v * g_ref[...]
        dz = rstd * (dxhat - jnp.mean(dxhat, axis=-1, keepdims=True)
                     - xhat * jnp.mean(dxhat * xhat, axis=-1, keepdims=True))
        dz_ref[...] = dz
        dzb_ref[...] = dz.astype(BF16)

        @pl.when(pl.program_id(0) == 0)
        def _():
            dg_ref[...] = jnp.zeros_like(dg_ref)
            db_ref[...] = jnp.zeros_like(db_ref)

        dg_ref[...] += jnp.sum(dh_v * xhat, axis=0, keepdims=True)
        db_ref[...] += jnp.sum(dh_v, axis=0, keepdims=True)

    row_blk = pl.BlockSpec((tm, D), lambda i: (i, 0))
    vec = pl.BlockSpec((1, D), lambda i: (0, 0))
    args = [dh] + ([dz_next] if has_next else []) + [z, g]
    in_specs = [row_blk] * (len(args) - 1) + [vec]
    return pl.pallas_call(
        body, name=name, grid=(M // tm,), in_specs=in_specs,
        out_specs=[row_blk, row_blk, vec, vec],
        out_shape=[jax.ShapeDtypeStruct((M, D), F32), jax.ShapeDtypeStruct((M, D), BF16),
                   jax.ShapeDtypeStruct((1, D), F32), jax.ShapeDtypeStruct((1, D), F32)],
        compiler_params=_cparams(),
    )(*args)


def _pool_fwd(h0, pw, scale, g, b, *, name):
    M, D = h0.shape
    n_groups, G, _ = pw.shape
    tm = _pick(M, (640, 128))

    def body(x_ref, halo_ref, pw_ref, sc_ref, g_ref, b_ref, z_ref, h_ref, hb_ref, diff_ref, ext_ref, mix_ref):
        i = pl.program_id(0)
        x = x_ref[...]
        ext_ref[0:POOL_HALO, :] = jnp.where(i == 0, 0.0, halo_ref[...])
        ext_ref[POOL_HALO:, :] = x
        tok = i * tm + lax.broadcasted_iota(jnp.int32, (tm, 1), 0)
        for gi, win in enumerate(POOL_WINDOWS):
            cols = slice(gi * G, (gi + 1) * G)
            xs = x[:, cols]
            s = xs
            for k in range(1, win):
                s = s + ext_ref[pl.ds(POOL_HALO - k, tm), cols]
            count = jnp.minimum(tok + 1, win).astype(F32)
            d = (s / count - xs).astype(BF16)
            diff_ref[:, cols] = d
            mix_ref[:, cols] = jnp.dot(d, pw_ref[gi], preferred_element_type=F32)
        z = ALPHA * x + mix_ref[...] * sc_ref[...]
        z_ref[...] = z
        h = _layer_norm(z, g_ref[...], b_ref[...])
        h_ref[...] = h
        hb_ref[...] = h.astype(BF16)

    row_blk = pl.BlockSpec((tm, D), lambda i: (i, 0))
    vec = pl.BlockSpec((1, D), lambda i: (0, 0))
    halo = pl.BlockSpec((POOL_HALO, D), lambda i: (jnp.maximum(i * (tm // POOL_HALO) - 1, 0), 0))
    return pl.pallas_call(
        body, name=name, grid=(M // tm,),
        in_specs=[row_blk, halo, pl.BlockSpec((n_groups, G, G), lambda i: (0, 0, 0)), vec, vec, vec],
        out_specs=[row_blk] * 4,
        out_shape=[jax.ShapeDtypeStruct((M, D), F32)] * 2 + [jax.ShapeDtypeStruct((M, D), BF16)] * 2,
        scratch_shapes=[pltpu.VMEM((tm + POOL_HALO, D), F32), pltpu.VMEM((tm, D), F32)],
        compiler_params=_cparams(),
    )(h0, h0, pw, scale, g, b)


def _pool_bwd(dz, diff, pw, pw_t, scale, *, name):
    M, D = dz.shape
    n_groups, G, _ = pw.shape
    tm = _pick(M, (640, 128))
    nt = M // tm

    def body(dz_ref, halo_ref, diff_ref, pw_ref, pwt_ref, sc_ref, dh_ref, dpw_ref, dsc_ref, ext_ref, q_ref):
        i = pl.program_id(0)
        dz_v = dz_ref[...]
        ext_ref[0:tm, :] = dz_v
        ext_ref[tm:, :] = jnp.where(i == nt - 1, 0.0, halo_ref[...])
        tok = i * tm + lax.broadcasted_iota(jnp.int32, (tm + POOL_HALO, 1), 0)

        @pl.when(i == 0)
        def _():
            dpw_ref[...] = jnp.zeros_like(dpw_ref)
            dsc_ref[...] = jnp.zeros_like(dsc_ref)

        for gi, win in enumerate(POOL_WINDOWS):
            cols = slice(gi * G, (gi + 1) * G)
            dmix = (ext_ref[:, cols] * sc_ref[:, cols]).astype(BF16)
            ddiff = jnp.dot(dmix, pwt_ref[gi], preferred_element_type=F32)
            count = jnp.minimum(tok + 1, win).astype(F32)
            q_ref[:, cols] = ddiff / count
            acc = -ddiff[0:tm]
            for k in range(win):
                acc = acc + q_ref[pl.ds(k, tm), cols]
            dh_ref[:, cols] = ALPHA * dz_v[:, cols] + acc
            d = diff_ref[:, cols]
            dpw_ref[gi] += lax.dot_general(d, dmix[0:tm], (((0,), (0,)), ((), ())), preferred_element_type=F32)
            mixed = jnp.dot(d, pw_ref[gi], preferred_element_type=F32)
            dsc_ref[:, cols] += jnp.sum(dz_v[:, cols] * mixed, axis=0, keepdims=True)

    row_blk = pl.BlockSpec((tm, D), lambda i: (i, 0))
    vec = pl.BlockSpec((1, D), lambda i: (0, 0))
    per_tile = tm // POOL_HALO
    halo = pl.BlockSpec((POOL_HALO, D), lambda i: (jnp.minimum((i + 1) * per_tile, nt * per_tile - 1), 0))
    wblk = pl.BlockSpec((n_groups, G, G), lambda i: (0, 0, 0))
    return pl.pallas_call(
        body, name=name, grid=(nt,),
        in_specs=[row_blk, halo, row_blk, wblk, wblk, vec],
        out_specs=[row_blk, wblk, vec],
        out_shape=[jax.ShapeDtypeStruct((M, D), F32), jax.ShapeDtypeStruct((n_groups, G, G), F32),
                   jax.ShapeDtypeStruct((1, D), F32)],
        scratch_shapes=[pltpu.VMEM((tm + POOL_HALO, D), F32), pltpu.VMEM((tm + POOL_HALO, D), F32)],
        compiler_params=_cparams(),
    )(dz, dz, diff, pw, pw_t, scale)


def _glu_interleave(x):
    s = x.shape
    n = s[-1] // (2 * GLU_CHUNK)
    return jnp.swapaxes(x.reshape(s[:-1] + (2, n, GLU_CHUNK)), -3, -2).reshape(s)


def _glu_deinterleave(x):
    s = x.shape
    n = s[-1] // (2 * GLU_CHUNK)
    return jnp.swapaxes(x.reshape(s[:-1] + (n, 2, GLU_CHUNK)), -3, -2).reshape(s)


def _conv_taps(ext_ref, u, tm):
    return ext_ref[pl.ds(CONV_HALO - 2, tm), :], ext_ref[pl.ds(CONV_HALO - 1, tm), :], u


def _conv_glu_fwd(u, cw, cb, *, name):
    M, F2 = u.shape
    tm = _pick(M, (640, 128))
    tc = 2 * GLU_CHUNK

    def body(u_ref, halo_ref, w_ref, b_ref, o_ref, ext_ref):
        i = pl.program_id(0)
        u_v = u_ref[...]
        ext_ref[0:CONV_HALO, :] = jnp.where(i == 0, 0.0, halo_ref[...])
        ext_ref[CONV_HALO:, :] = u_v
        u2, u1, u0 = _conv_taps(ext_ref, u_v, tm)
        c = b_ref[...] + w_ref[0:1, :] * u2 + w_ref[1:2, :] * u1 + w_ref[2:3, :] * u0
        a, g = c[:, :GLU_CHUNK], c[:, GLU_CHUNK:]
        o_ref[...] = (a * jax.nn.sigmoid(a) * g).astype(BF16)

    per_tile = tm // CONV_HALO
    return pl.pallas_call(
        body, name=name, grid=(M // tm, F2 // tc),
        in_specs=[pl.BlockSpec((tm, tc), lambda i, j: (i, j)),
                  pl.BlockSpec((CONV_HALO, tc), lambda i, j: (jnp.maximum(i * per_tile - 1, 0), j)),
                  pl.BlockSpec((3, tc), lambda i, j: (0, j)), pl.BlockSpec((1, tc), lambda i, j: (0, j))],
        out_specs=pl.BlockSpec((tm, GLU_CHUNK), lambda i, j: (i, j)),
        out_shape=jax.ShapeDtypeStruct((M, F2 // 2), BF16),
        scratch_shapes=[pltpu.VMEM((tm + CONV_HALO, tc), F32)],
        compiler_params=_cparams(),
    )(u, u, cw, cb)


def _conv_glu_bwd(u, dact, cw, cb, *, name):
    M, F2 = u.shape
    tm = _pick(M, (640, 128))
    nt = M // tm
    tc = 2 * GLU_CHUNK

    def body(u_ref, halo_ref, da_ref, w_ref, b_ref, du_ref, dw_ref, db_ref, ext_ref, dcx_ref, carry_ref):
        i = pl.program_id(1)
        u_v = u_ref[...]
        ext_ref[0:CONV_HALO, :] = jnp.where(i == nt - 1, 0.0, halo_ref[...])
        ext_ref[CONV_HALO:, :] = u_v
        u2, u1, u0 = _conv_taps(ext_ref, u_v, tm)
        c = b_ref[...] + w_ref[0:1, :] * u2 + w_ref[1:2, :] * u1 + w_ref[2:3, :] * u0
        a, g = c[:, :GLU_CHUNK], c[:, GLU_CHUNK:]
        sig = jax.nn.sigmoid(a)
        dact_v = da_ref[...]
        d_a = dact_v * g * (sig * (1.0 + a * (1.0 - sig)))
        d_g = dact_v * (a * sig)
        dc = jnp.concatenate([d_a, d_g], axis=1)

        @pl.when(i == 0)
        def _():
            dw_ref[...] = jnp.zeros_like(dw_ref)
            db_ref[...] = jnp.zeros_like(db_ref)
            carry_ref[...] = jnp.zeros_like(carry_ref)

        db_ref[...] += jnp.sum(dc, axis=0, keepdims=True)
        dw_ref[0:1, :] += jnp.sum(dc * u2, axis=0, keepdims=True)
        dw_ref[1:2, :] += jnp.sum(dc * u1, axis=0, keepdims=True)
        dw_ref[2:3, :] += jnp.sum(dc * u0, axis=0, keepdims=True)
        dcx_ref[0:tm, :] = dc
        dcx_ref[tm:, :] = carry_ref[...]
        du = w_ref[2:3, :] * dc + w_ref[1:2, :] * dcx_ref[pl.ds(1, tm), :] + w_ref[0:1, :] * dcx_ref[pl.ds(2, tm), :]
        du_ref[...] = du.astype(BF16)
        carry_ref[...] = dcx_ref[0:CONV_HALO, :]

    per_tile = tm // CONV_HALO
    rev = lambda i: nt - 1 - i
    return pl.pallas_call(
        body, name=name, grid=(F2 // tc, nt),
        in_specs=[pl.BlockSpec((tm, tc), lambda j, i: (rev(i), j)),
                  pl.BlockSpec((CONV_HALO, tc), lambda j, i: (jnp.maximum(rev(i) * per_tile - 1, 0), j)),
                  pl.BlockSpec((tm, GLU_CHUNK), lambda j, i: (rev(i), j)),
                  pl.BlockSpec((3, tc), lambda j, i: (0, j)), pl.BlockSpec((1, tc), lambda j, i: (0, j))],
        out_specs=[pl.BlockSpec((tm, tc), lambda j, i: (rev(i), j)),
                   pl.BlockSpec((3, tc), lambda j, i: (0, j)), pl.BlockSpec((1, tc), lambda j, i: (0, j))],
        out_shape=[jax.ShapeDtypeStruct((M, F2), BF16), jax.ShapeDtypeStruct((3, F2), F32),
                   jax.ShapeDtypeStruct((1, F2), F32)],
        scratch_shapes=[pltpu.VMEM((tm + CONV_HALO, tc), F32), pltpu.VMEM((tm + CONV_HALO, tc), F32),
                        pltpu.VMEM((CONV_HALO, tc), F32)],
        compiler_params=_cparams(),
    )(u, u, dact, cw, cb)


def _split3(x):
    hi = x.astype(BF16)
    r = x - hi.astype(F32)
    mid = r.astype(BF16)
    lo = (r - mid.astype(F32)).astype(BF16)
    return hi, mid, lo


def _tri_sum(tri, x):
    return sum(jnp.dot(tri, part, preferred_element_type=F32) for part in _split3(x))


def _log_sigmoid(x):
    return jnp.minimum(x, 0.0) - jnp.log1p(jnp.exp(-jnp.abs(x)))


def _forget_cumsum(pre, bias, *, name):
    M, C = pre.shape
    tm = _pick(M, (640, 128))

    def body(p_ref, b_ref, c_ref, carry_ref):
        i = pl.program_id(0)

        @pl.when(i == 0)
        def _():
            carry_ref[...] = jnp.zeros_like(carry_ref)

        logf = _log_sigmoid(p_ref[...] + b_ref[...])
        r = lax.broadcasted_iota(jnp.int32, (tm, tm), 0)
        s = lax.broadcasted_iota(jnp.int32, (tm, tm), 1)
        c_ref[...] = _tri_sum((s <= r).astype(BF16), logf) + carry_ref[...]
        carry_ref[...] = c_ref[pl.ds(tm - 1, 1), :]

    return pl.pallas_call(
        body, name=name, grid=(M // tm,),
        in_specs=[pl.BlockSpec((tm, C), lambda i: (i, 0)), pl.BlockSpec((1, C), lambda i: (0, 0))],
        out_specs=pl.BlockSpec((tm, C), lambda i: (i, 0)),
        out_shape=jax.ShapeDtypeStruct((M, C), F32),
        scratch_shapes=[pltpu.VMEM((1, C), F32)],
        compiler_params=_cparams(),
    )(pre, bias)


def _forget_cumsum_bwd(dc, pre, bias, *, name):
    M, C = pre.shape
    tm = _pick(M, (640, 128))
    nt = M // tm

    def body(dc_ref, p_ref, b_ref, dp_ref, db_ref, carry_ref, run_ref):
        i = pl.program_id(0)

        @pl.when(i == 0)
        def _():
            carry_ref[...] = jnp.zeros_like(carry_ref)
            db_ref[...] = jnp.zeros_like(db_ref)

        r = lax.broadcasted_iota(jnp.int32, (tm, tm), 0)
        s = lax.broadcasted_iota(jnp.int32, (tm, tm), 1)
        run_ref[...] = _tri_sum((s >= r).astype(BF16), dc_ref[...]) + carry_ref[...]
        carry_ref[...] = run_ref[pl.ds(0, 1), :]
        dpre = run_ref[...] * jax.nn.sigmoid(-(p_ref[...] + b_ref[...]))
        dp_ref[...] = dpre.astype(BF16)
        db_ref[...] += jnp.sum(dpre, axis=0, keepdims=True)

    rev_blk = pl.BlockSpec((tm, C), lambda i: (nt - 1 - i, 0))
    vec = pl.BlockSpec((1, C), lambda i: (0, 0))
    return pl.pallas_call(
        body, name=name, grid=(nt,),
        in_specs=[rev_blk, rev_blk, vec], out_specs=[rev_blk, vec],
        out_shape=[jax.ShapeDtypeStruct((M, C), BF16), jax.ShapeDtypeStruct((1, C), F32)],
        scratch_shapes=[pltpu.VMEM((1, C), F32), pltpu.VMEM((tm, C), F32)],
        compiler_params=_cparams(),
    )(dc, pre, bias)


def _causal_mask(tm):
    row = lax.broadcasted_iota(jnp.int32, (tm, tm), 0)
    col = lax.broadcasted_iota(jnp.int32, (tm, tm), 1)
    return col <= row


def _dot_nt(a, b):
    return lax.dot_general(a, b, (((1,), (1,)), ((), ())), preferred_element_type=F32)


def _dot_tn(a, b):
    return lax.dot_general(a, b, (((0,), (0,)), ((), ())), preferred_element_type=F32)


def _attn_specs(H, M, dh, tm):
    nt = M // tm
    q_blk = pl.BlockSpec((1, tm, dh), lambda h, i: (h, i, 0))
    col_blk = pl.BlockSpec((1, tm, 1), lambda h, i: (h, i, 0))
    kv_blk = pl.BlockSpec((1, M, dh), lambda h, i: (h, 0, 0))
    ck_blk = pl.BlockSpec((1, nt, 1, tm), lambda h, i: (h, 0, 0, 0))
    return q_blk, col_blk, kv_blk, ck_blk


def _attn_fwd(q, k, v, cq, ck, *, name):
    H, M, dh = q.shape
    tm = ck.shape[-1]
    q_blk, col_blk, kv_blk, ck_blk = _attn_specs(H, M, dh, tm)

    def body(q_ref, k_ref, v_ref, cq_ref, ck_ref, o_ref, lse_ref):
        i = pl.program_id(1)
        qs = q_ref[0] * jnp.asarray(dh ** -0.5, BF16)
        cq_v = cq_ref[0]

        def step(j, carry, masked):
            m, l, acc = carry
            off = pl.multiple_of(j * tm, tm)
            s = _dot_nt(qs, k_ref[0, pl.ds(off, tm), :]) + (cq_v - ck_ref[0, j])
            if masked:
                s = jnp.where(_causal_mask(tm), s, NEG_INF)
            m_new = jnp.maximum(m, jnp.max(s, axis=1, keepdims=True))
            a = jnp.exp(m - m_new)
            p = jnp.exp(s - m_new)
            l = a * l + jnp.sum(p, axis=1, keepdims=True)
            acc = a * acc + jnp.dot(p.astype(BF16), v_ref[0, pl.ds(off, tm), :], preferred_element_type=F32)
            return m_new, l, acc

        init = (jnp.full((tm, 1), NEG_INF, F32), jnp.zeros((tm, 1), F32), jnp.zeros((tm, dh), F32))
        carry = lax.fori_loop(0, i, lambda j, c: step(j, c, False), init)
        m, l, acc = step(i, carry, True)
        o_ref[0] = (acc / l).astype(BF16)
        lse_ref[0] = m + jnp.log(l)

    return pl.pallas_call(
        body, name=name, grid=(H, M // tm),
        in_specs=[q_blk, kv_blk, kv_blk, col_blk, ck_blk], out_specs=[q_blk, col_blk],
        out_shape=[jax.ShapeDtypeStruct((H, M, dh), BF16), jax.ShapeDtypeStruct((H, M, 1), F32)],
        compiler_params=_cparams(),
    )(q, k, v, cq, ck)


def _attn_bwd(q, k, v, o, do, lse, cq, ck, *, name):
    H, M, dh = q.shape
    tm = ck.shape[-1]
    q_blk, col_blk, kv_blk, ck_blk = _attn_specs(H, M, dh, tm)

    def body(q_ref, o_ref, do_ref, lse_ref, cq_ref, k_ref, v_ref, ck_ref, dq_ref, dk_ref, dv_ref, dcq_ref, dck_ref):
        i = pl.program_id(1)

        @pl.when(i == 0)
        def _():
            dk_ref[...] = jnp.zeros_like(dk_ref)
            dv_ref[...] = jnp.zeros_like(dv_ref)
            dck_ref[...] = jnp.zeros_like(dck_ref)

        scale = jnp.asarray(dh ** -0.5, BF16)
        qs = q_ref[0] * scale
        do_v = do_ref[0]
        delta = jnp.sum(do_v.astype(F32) * o_ref[0].astype(F32), axis=1, keepdims=True)
        lse_v = lse_ref[0]
        cq_v = cq_ref[0]

        def step(j, carry, masked):
            dq, dcq = carry
            off = pl.multiple_of(j * tm, tm)
            rows = pl.ds(off, tm)
            k_j = k_ref[0, rows, :]
            s = _dot_nt(qs, k_j) + (cq_v - ck_ref[0, j])
            p = jnp.exp(s - lse_v)
            if masked:
                p = jnp.where(_causal_mask(tm), p, 0.0)
            dv_ref[0, rows, :] += _dot_tn(p.astype(BF16), do_v)
            dp = _dot_nt(do_v, v_ref[0, rows, :])
            ds = p * (dp - delta)
            ds_b = ds.astype(BF16)
            dk_ref[0, rows, :] += _dot_tn(ds_b, qs)
            dck_ref[0, j] += -jnp.sum(ds, axis=0, keepdims=True)
            return dq + jnp.dot(ds_b, k_j, preferred_element_type=F32), dcq + jnp.sum(ds, axis=1, keepdims=True)

        init = (jnp.zeros((tm, dh), F32), jnp.zeros((tm, 1), F32))
        dq, dcq = step(i, lax.fori_loop(0, i, lambda j, c: step(j, c, False), init), True)
        dq_ref[0] = (dq * dh ** -0.5).astype(BF16)
        dcq_ref[0] = dcq

    return pl.pallas_call(
        body, name=name, grid=(H, M // tm),
        in_specs=[q_blk, q_blk, q_blk, col_blk, col_blk, kv_blk, kv_blk, ck_blk],
        out_specs=[q_blk, kv_blk, kv_blk, col_blk, ck_blk],
        out_shape=[jax.ShapeDtypeStruct((H, M, dh), BF16), jax.ShapeDtypeStruct((H, M, dh), F32),
                   jax.ShapeDtypeStruct((H, M, dh), F32), jax.ShapeDtypeStruct((H, M, 1), F32),
                   jax.ShapeDtypeStruct(ck.shape, F32)],
        compiler_params=_cparams(),
    )(q, o, do, lse, cq, k, v, ck)


def kernel(x, meta, pool_w, pool_scale, w_kv, w_f, b_f, w_q, w_o, ffn_w_in, ffn_conv_w, ffn_conv_b, ffn_w_out, ln_g, ln_b, loss_target, m_meta, m_pool_w, m_pool_scale, m_w_kv, m_w_f, m_b_f, m_w_q, m_w_o, m_ffn_w_in, m_ffn_conv_w, m_ffn_conv_b, m_ffn_w_out, m_ln_g, m_ln_b, v_meta, v_pool_w, v_pool_scale, v_w_kv, v_w_f, v_b_f, v_w_q, v_w_o, v_ffn_w_in, v_ffn_conv_w, v_ffn_conv_b, v_ffn_w_out, v_ln_g, v_ln_b):
    local = dict(meta=meta, pool_w=pool_w, pool_scale=pool_scale, w_kv=w_kv, w_f=w_f, b_f=b_f, w_q=w_q, w_o=w_o,
                 ffn_w_in=ffn_w_in, ffn_conv_w=ffn_conv_w, ffn_conv_b=ffn_conv_b, ffn_w_out=ffn_w_out,
                 ln_g=ln_g, ln_b=ln_b)
    mom1 = dict(meta=m_meta, pool_w=m_pool_w, pool_scale=m_pool_scale, w_kv=m_w_kv, w_f=m_w_f, b_f=m_b_f,
                w_q=m_w_q, w_o=m_w_o, ffn_w_in=m_ffn_w_in, ffn_conv_w=m_ffn_conv_w, ffn_conv_b=m_ffn_conv_b,
                ffn_w_out=m_ffn_w_out, ln_g=m_ln_g, ln_b=m_ln_b)
    mom2 = dict(meta=v_meta, pool_w=v_pool_w, pool_scale=v_pool_scale, w_kv=v_w_kv, w_f=v_w_f, b_f=v_b_f,
                w_q=v_w_q, w_o=v_w_o, ffn_w_in=v_ffn_w_in, ffn_conv_w=v_ffn_conv_w, ffn_conv_b=v_ffn_conv_b,
                ffn_w_out=v_ffn_w_out, ln_g=v_ln_g, ln_b=v_ln_b)
    axis_of = dict(PARAMS)

    S, D = x.shape[1], x.shape[2]
    H = b_f.shape[0]
    dh = D // H
    n_tok = N_META + S
    M = _round_up(n_tok, LANES)
    tm = _pick(M, (640, 128))
    nt = M // tm
    F2 = ffn_conv_b.shape[1]
    F = F2 // 2
    depth = ffn_conv_b.shape[0]

    def gathered(names, dtype, gname):
        got = _all_gather(_pack([local[n] for n in names], dtype, 8), gname)
        blocks = _unpack(got, [local[n].shape for n in names], lead=(N_DEV,))
        return {n: _from_blocks(blk, axis_of[n]) for n, blk in zip(names, blocks)}

    wb = gathered(GATHER_BF16, BF16, "gather_matmul_weights")
    wf32 = gathered(GATHER_F32, F32, "gather_vector_params")

    pw = wb["pool_w"][0]
    pw_t = jnp.swapaxes(pw, 1, 2)
    wf_pad = jnp.pad(wb["w_f"], ((0, 0), (0, LANES - H)))
    w_qkv = jnp.concatenate([wb["w_kv"], wb["w_q"][0]], axis=1)
    w_att_t = jnp.concatenate([w_qkv, wf_pad], axis=1).T
    wo = wb["w_o"][0]
    wo_t = wo.T
    w_in = _glu_interleave(wb["ffn_w_in"])
    w_in_t = jnp.swapaxes(w_in, 1, 2)
    w_out = wb["ffn_w_out"]
    w_out_t = jnp.swapaxes(w_out, 1, 2)
    conv_w = _glu_interleave(wf32["ffn_conv_w"])
    conv_b = _glu_interleave(ffn_conv_b)[:, None, :]
    scale = wf32["pool_scale"]
    g_ln, b_ln = wf32["ln_g"], wf32["ln_b"]
    ln = lambda i, j: (g_ln[i, j][None, :], b_ln[i, j][None, :])
    bias_f = jnp.pad(b_f, (0, LANES - H))[None, :]

    pad_rows = M - n_tok
    h0 = jnp.concatenate([wf32["meta"], x[0], jnp.zeros((pad_rows, D), F32)], axis=0)
    target = jnp.concatenate([jnp.zeros((N_META, D), F32), loss_target[0], jnp.zeros((pad_rows, D), F32)], axis=0)

    z1, h1, h1b, diff = _pool_fwd(h0, pw, scale, *ln(0, 0), name="pool_fwd")
    u0 = _mm(h1b, w_in[0], name="ffn0_up", out_dtype=F32)
    act0 = _conv_glu_fwd(u0, conv_w[0], conv_b[0], name="ffn0_conv")
    z2, h2, h2b = _mm_ln(act0, w_out[0], h1, *ln(0, 1), name="ffn0_down_ln")

    qkv = _mm(h2b, w_qkv, name="attn_qkv", out_dtype=BF16)
    pre = _mm(h2b, wf_pad, name="attn_gate", out_dtype=F32)
    c = _forget_cumsum(pre, bias_f, name="attn_cumsum")
    kvq = jnp.transpose(qkv.reshape(M, 3, H, dh), (1, 2, 0, 3))
    k_h, v_h, q_h = kvq[0], kvq[1], kvq[2]
    c_t = c[:, :H].T
    cq = c_t[:, :, None]
    ck = c_t.reshape(H, nt, 1, tm)
    o_h, lse = _attn_fwd(q_h, k_h, v_h, cq, ck, name="attn_fwd")
    o = jnp.transpose(o_h, (1, 0, 2)).reshape(M, D)
    z3, h3, h3b = _mm_ln(o, wo, h2, *ln(1, 0), name="attn_out_ln")

    u1 = _mm(h3b, w_in[1], name="ffn1_up", out_dtype=F32)
    act1 = _conv_glu_fwd(u1, conv_w[1], conv_b[1], name="ffn1_conv")
    z4, dy, loss_part = _mm_ln(act1, w_out[1], h3, *ln(1, 1), name="ffn1_down_loss", target=target, n_tok=n_tok)
    loss = lax.psum(loss_part[0, 0], ("x", "y", "c"))

    grads = {}

    def ffn_bwd(layer, dz_b, u, act, h_in_b, tag):
        dact = _mm(dz_b, w_out_t[layer], name=tag + "_dact", out_dtype=F32)
        du, dcw, dcb = _conv_glu_bwd(u, dact, conv_w[layer], conv_b[layer], name=tag + "_conv_bwd")
        dh = _mm(du, w_in_t[layer], name=tag + "_dh", out_dtype=F32)
        d_w_out = _mm_tn(act, dz_b, name=tag + "_dw_out")
        d_w_in = _glu_deinterleave(_mm_tn(h_in_b, du, name=tag + "_dw_in"))
        return dh, d_w_in, d_w_out, _glu_deinterleave(dcw), _glu_deinterleave(dcb)[0]

    dz4, dz4b, dg11, db11 = _ln_bwd(dy, z4, g_ln[1, 1][None, :], name="ln4_bwd")
    dh3, dwin1, dwout1, dcw1, dcb1 = ffn_bwd(1, dz4b, u1, act1, h3b, "ffn1")
    dz3, dz3b, dg10, db10 = _ln_bwd(dh3, z3, g_ln[1, 0][None, :], name="ln3_bwd", dz_next=dz4)

    do = _mm(dz3b, wo_t, name="attn_do", out_dtype=BF16)
    grads["w_o"] = _mm_tn(o, dz3b, name="attn_dw_o")[None]
    do_h = jnp.transpose(do.reshape(M, H, dh), (1, 0, 2))
    dq_h, dk_h, dv_h, dcq, dck = _attn_bwd(q_h, k_h, v_h, o_h, do_h, lse, cq, ck, name="attn_bwd")
    dc = jnp.pad((dcq[:, :, 0] + dck.reshape(H, M)).T, ((0, 0), (0, LANES - H)))
    dpre, dbias = _forget_cumsum_bwd(dc, pre, bias_f, name="attn_cumsum_bwd")
    flat = lambda t: jnp.transpose(t, (1, 0, 2)).reshape(M, D).astype(BF16)
    d_att = jnp.concatenate([flat(dk_h), flat(dv_h), flat(dq_h), dpre], axis=1)
    dh2 = _mm(d_att, w_att_t, name="attn_dh", out_dtype=F32)
    d_w_att = _mm_tn(h2b, d_att, name="attn_dw_qkv")
    grads["w_kv"] = d_w_att[:, :2 * D]
    grads["w_q"] = d_w_att[:, 2 * D:3 * D][None]
    grads["w_f"] = d_w_att[:, 3 * D:3 * D + H]
    grads["b_f"] = dbias[0, :H]
    dz2, dz2b, dg01, db01 = _ln_bwd(dh2, z2, g_ln[0, 1][None, :], name="ln2_bwd", dz_next=dz3)

    dh1, dwin0, dwout0, dcw0, dcb0 = ffn_bwd(0, dz2b, u0, act0, h1b, "ffn0")
    dz1, _, dg00, db00 = _ln_bwd(dh1, z1, g_ln[0, 0][None, :], name="ln1_bwd", dz_next=dz2)
    dh0, dpw, dscale = _pool_bwd(dz1, diff, pw, pw_t, scale, name="pool_bwd")

    grads["meta"] = dh0[:N_META]
    grads["pool_w"] = dpw[None]
    grads["pool_scale"] = dscale
    grads["ffn_w_in"] = jnp.stack([dwin0, dwin1])
    grads["ffn_w_out"] = jnp.stack([dwout0, dwout1])
    grads["ffn_conv_w"] = jnp.stack([dcw0, dcw1])
    grads["ffn_conv_b"] = jnp.stack([dcb0, dcb1])
    grads["ln_g"] = jnp.stack([jnp.stack([dg00[0], dg01[0]]), jnp.stack([dg10[0], dg11[0]])])
    grads["ln_b"] = jnp.stack([jnp.stack([db00[0], db01[0]]), jnp.stack([db10[0], db11[0]])])
    grad_x = dh0[N_META:n_tok][None]

    names = [n for n, _ in PARAMS]
    pieces = [_to_blocks(grads[n], axis_of[n]) for n in names]
    seg = jnp.concatenate([jnp.pad(p, ((0, 0), (0, _round_up(p.shape[1], PACK_COLS) - p.shape[1]))) for p in pieces], axis=1)
    rows = _round_up(seg.shape[1] // PACK_COLS, ADAM_ROWS)
    seg = jnp.pad(seg, ((0, 0), (0, rows * PACK_COLS - seg.shape[1]))).reshape(N_DEV, rows, PACK_COLS)
    recv = _exchange(seg, "grad_exchange")
    packed = [_pack([d[n] for n in names], F32, ADAM_ROWS) for d in (local, mom1, mom2)]
    outs = _adamw(recv, *packed, name="adamw")
    shapes = [local[n].shape for n in names]
    g_out, d_out, m_out, v_out = [_unpack(o_, shapes) for o_ in outs]
    return (loss, grad_x, *g_out, *d_out, *m_out, *v_out)
```

```python
import jax
import jax.numpy as jnp
from jax import lax
from jax.experimental import pallas as pl
from jax.experimental.pallas import tpu as pltpu

F32, BF16 = jnp.float32, jnp.bfloat16
MESH = pl.DeviceIdType.MESH

N_DEV = 8
N_META = 16
POOL_WINDOWS = (2, 4, 8, 16)
POOL_HALO = 16
CONV_HALO = 8
ALPHA = 4.0 ** 0.25
LN_EPS = 1e-5
NEG_INF = -1e30
ADAM_LR, ADAM_B1, ADAM_B2, ADAM_EPS, ADAM_WD, ADAM_STEP = 0.001, 0.9, 0.999, 1e-08, 0.01, 10

LANES = 128
PACK_COLS = 1024
ADAM_TILE_BYTES = 4 << 20
GLU_CHUNK = 256
VMEM_LIMIT = 56 * 1024 * 1024

PARAMS = (("meta", 1), ("pool_w", 2), ("pool_scale", 1), ("w_kv", 1), ("w_f", 0), ("b_f", None),
          ("w_q", 1), ("w_o", 1), ("ffn_w_in", 2), ("ffn_conv_w", 2), ("ffn_conv_b", None),
          ("ffn_w_out", 1), ("ln_g", 2), ("ln_b", 2))
BIG = ("pool_w", "w_kv", "w_q", "w_o", "ffn_w_in", "ffn_w_out")
SMALL = ("meta", "pool_scale", "w_f", "b_f", "ffn_conv_w", "ffn_conv_b", "ln_g", "ln_b")


def _cparams(**kw):
    return pltpu.CompilerParams(vmem_limit_bytes=VMEM_LIMIT, **kw)


def _pick(n, cands):
    for c in cands:
        if n % c == 0:
            return c
    return n


def _round_up(n, m):
    return (n + m - 1) // m * m


def _pack(pieces, lead=()):
    flat = []
    for p in pieces:
        v = p.reshape(lead + (-1,))
        flat.append(jnp.pad(v, [(0, 0)] * len(lead) + [(0, _round_up(v.shape[-1], PACK_COLS) - v.shape[-1])]))
    v = jnp.concatenate(flat, axis=-1)
    rows = _round_up(v.shape[-1] // PACK_COLS, 8)
    v = jnp.pad(v, [(0, 0)] * len(lead) + [(0, rows * PACK_COLS - v.shape[-1])])
    return v.reshape(lead + (rows, PACK_COLS))


def _unpack(buf, shapes, lead=()):
    flat = buf.reshape(lead + (-1,))
    out, off = [], 0
    for s in shapes:
        n = 1
        for d in s:
            n *= d
        out.append(flat[..., off:off + n].reshape(lead + tuple(s)))
        off += _round_up(n, PACK_COLS)
    return out


def _to_blocks(full, axis):
    if axis is None:
        return jnp.broadcast_to(full[None], (N_DEV,) + full.shape)
    s = full.shape
    x = full.reshape(s[:axis] + (N_DEV, s[axis] // N_DEV) + s[axis + 1:])
    return jnp.moveaxis(x, axis, 0)


def _from_blocks(blocks, axis):
    x = jnp.moveaxis(blocks, 0, axis)
    s = x.shape
    return x.reshape(s[:axis] + (s[axis] * s[axis + 1],) + s[axis + 2:])


def _coords():
    return lax.axis_index("x"), lax.axis_index("y"), lax.axis_index("c")


def _flip(pos, mask):
    x, y, c = pos
    return (1 - x if mask & 4 else x, 1 - y if mask & 2 else y, 1 - c if mask & 1 else c)


def _index(pos):
    x, y, c = pos
    return 4 * x + 2 * y + c


def _comm_call(body, name, arrays, out_shapes):
    n = len(arrays)
    hbm = pl.BlockSpec(memory_space=pl.ANY)
    return pl.pallas_call(
        body, name=name, out_shape=out_shapes, in_specs=[hbm] * n, out_specs=[hbm] * n,
        scratch_shapes=[pltpu.SemaphoreType.DMA((7 * n,)), pltpu.SemaphoreType.DMA((7 * n,)),
                        pltpu.SemaphoreType.DMA((n,))],
    )(*arrays)


def _all_gather(blocks, name):
    chip_masks = (4, 2, 6)
    n = len(blocks)

    def body(*refs):
        x_refs, out_refs = refs[:n], refs[n:2 * n]
        send_sems, recv_sems, local_sems = refs[2 * n:]
        me = _coords()
        sibling = _flip(me, 1)

        def copy(a, k, owner, to, from_input=False):
            slot = out_refs[a].at[_index(owner)]
            return pltpu.make_async_remote_copy(
                src_ref=x_refs[a] if from_input else slot, dst_ref=slot,
                send_sem=send_sems.at[7 * a + k], recv_sem=recv_sems.at[7 * a + k],
                device_id=to, device_id_type=MESH)

        mine = [pltpu.make_async_copy(x_refs[a], out_refs[a].at[_index(me)], local_sems.at[a]) for a in range(n)]
        first = [copy(a, 0, me, sibling, True) for a in range(n)]
        first += [copy(a, 1 + j, me, _flip(me, m), True) for j, m in enumerate(chip_masks) for a in range(n)]
        for cp in mine + first:
            cp.start()
        passed = []
        for j, m in enumerate(chip_masks):
            for a in range(n):
                copy(a, 1 + j, _flip(me, m), me).wait_recv()
                passed.append(copy(a, 4 + j, _flip(me, m), sibling))
                passed[-1].start()
        for a in range(n):
            copy(a, 0, sibling, me).wait_recv()
            for j, m in enumerate(chip_masks):
                copy(a, 4 + j, _flip(sibling, m), me).wait_recv()
        for cp in first + passed:
            cp.wait_send()
        for cp in mine:
            cp.wait()

    return _comm_call(body, name, blocks, [jax.ShapeDtypeStruct((N_DEV,) + b.shape, b.dtype) for b in blocks])


def _exchange(segs, name):
    n = len(segs)

    def body(*refs):
        seg_refs, out_refs = refs[:n], refs[n:2 * n]
        send_sems, recv_sems, local_sems = refs[2 * n:]
        me = _coords()

        def copy(a, mask):
            peer = _flip(me, mask)
            return pltpu.make_async_remote_copy(
                src_ref=seg_refs[a].at[_index(peer)], dst_ref=out_refs[a].at[_index(me)],
                send_sem=send_sems.at[7 * a + mask - 1], recv_sem=recv_sems.at[7 * a + mask - 1],
                device_id=peer, device_id_type=MESH)

        def arrival(a, mask):
            peer = _flip(me, mask)
            return pltpu.make_async_remote_copy(
                src_ref=seg_refs[a].at[_index(me)], dst_ref=out_refs[a].at[_index(peer)],
                send_sem=send_sems.at[7 * a + mask - 1], recv_sem=recv_sems.at[7 * a + mask - 1],
                device_id=peer, device_id_type=MESH)

        mine = [pltpu.make_async_copy(seg_refs[a].at[_index(me)], out_refs[a].at[_index(me)], local_sems.at[a])
                for a in range(n)]
        sends = [copy(a, mask) for mask in range(1, N_DEV) for a in range(n)]
        for cp in mine + sends:
            cp.start()
        for mask in range(1, N_DEV):
            for a in range(n):
                arrival(a, mask).wait_recv()
        for cp in sends:
            cp.wait_send()
        for cp in mine:
            cp.wait()

    return _comm_call(body, name, segs, [jax.ShapeDtypeStruct(s.shape, s.dtype) for s in segs])


def _adamw(recv, w, m, v, name):
    rows, cols = w.shape
    tr = max(t for t in range(8, rows + 1, 8) if rows % t == 0 and N_DEV * t * cols * 4 <= ADAM_TILE_BYTES)
    c1 = 1.0 - ADAM_B1 ** ADAM_STEP
    c2 = 1.0 - ADAM_B2 ** ADAM_STEP

    def body(r_ref, w_ref, m_ref, v_ref, g_out, d_out, m_out, v_out):
        g = r_ref[0]
        for s in range(1, N_DEV):
            g = g + r_ref[s]
        m_new = ADAM_B1 * m_ref[...] + (1.0 - ADAM_B1) * g
        v_new = ADAM_B2 * v_ref[...] + (1.0 - ADAM_B2) * (g * g)
        m_hat = m_new / c1
        v_hat = v_new / c2
        g_out[...] = g
        d_out[...] = -ADAM_LR * (m_hat / (jnp.sqrt(v_hat) + ADAM_EPS) + ADAM_WD * w_ref[...])
        m_out[...] = m_new
        v_out[...] = v_new

    tile = pl.BlockSpec((tr, cols), lambda i: (i, 0))
    return pl.pallas_call(
        body, name=name, grid=(rows // tr,),
        in_specs=[pl.BlockSpec((N_DEV, tr, cols), lambda i: (0, i, 0)), tile, tile, tile],
        out_specs=[tile] * 4,
        out_shape=[jax.ShapeDtypeStruct(w.shape, F32)] * 4,
        compiler_params=_cparams(),
    )(recv, w, m, v)


def _mm(a, b, *, name, out_dtype, res=None, res_scale=1.0):
    M, K = a.shape
    N = b.shape[1]
    tm = _pick(M, (640, 128))
    tn = _pick(N, (1408, 1024, 640, 512, 256, 128))
    tk = K if K <= 1024 else _pick(K, (1024, 640, 512, 256, 128))
    nk = K // tk

    def body(*refs):
        a_ref, b_ref = refs[0], refs[1]
        r_ref = refs[2] if res is not None else None
        o_ref = refs[3] if res is not None else refs[2]
        acc_ref = refs[-1] if nk > 1 else None

        def finish(acc):
            if r_ref is not None:
                acc = acc + res_scale * r_ref[...]
            o_ref[...] = acc.astype(out_dtype)

        prod = jnp.dot(a_ref[...], b_ref[...], preferred_element_type=F32)
        if nk == 1:
            finish(prod)
        else:
            k = pl.program_id(2)

            @pl.when(k == 0)
            def _():
                acc_ref[...] = prod

            @pl.when(k > 0)
            def _():
                acc_ref[...] += prod

            @pl.when(k == nk - 1)
            def _():
                finish(acc_ref[...])

    in_specs = [pl.BlockSpec((tm, tk), lambda i, j, k: (i, k)), pl.BlockSpec((tk, tn), lambda i, j, k: (k, j))]
    args = [a, b]
    if res is not None:
        in_specs.append(pl.BlockSpec((tm, tn), lambda i, j, k: (i, j)))
        args.append(res)
    return pl.pallas_call(
        body, name=name, grid=(M // tm, N // tn, nk),
        in_specs=in_specs, out_specs=pl.BlockSpec((tm, tn), lambda i, j, k: (i, j)),
        out_shape=jax.ShapeDtypeStruct((M, N), out_dtype),
        scratch_shapes=[pltpu.VMEM((tm, tn), F32)] if nk > 1 else [],
        compiler_params=_cparams(),
    )(*args)


def _mm_tn(a, b, *, name):
    T, M = a.shape
    N = b.shape[1]
    tt = _pick(T, (1664, 640, 128))
    tm = _pick(M, (1408, 1024, 512, 256, 128))
    tn = _pick(N, (1024, 640, 512, 256, 128))

    def body(a_ref, b_ref, o_ref):
        prod = lax.dot_general(a_ref[...], b_ref[...], (((0,), (0,)), ((), ())), preferred_element_type=F32)

        @pl.when(pl.program_id(2) == 0)
        def _():
            o_ref[...] = prod

        @pl.when(pl.program_id(2) > 0)
        def _():
            o_ref[...] += prod

    return pl.pallas_call(
        body, name=name, grid=(M // tm, N // tn, T // tt),
        in_specs=[pl.BlockSpec((tt, tm), lambda i, j, k: (k, i)), pl.BlockSpec((tt, tn), lambda i, j, k: (k, j))],
        out_specs=pl.BlockSpec((tm, tn), lambda i, j, k: (i, j)),
        out_shape=jax.ShapeDtypeStruct((M, N), F32),
        compiler_params=_cparams(),
    )(a, b)


def _layer_norm(z, g, b):
    mu = jnp.mean(z, axis=-1, keepdims=True)
    xc = z - mu
    var = jnp.mean(xc * xc, axis=-1, keepdims=True)
    return xc * lax.rsqrt(var + LN_EPS) * g + b


def _mm_ln(a, w, res, g, b, *, name, target=None, n_tok=None):
    M, K = a.shape
    D = w.shape[1]
    tm = _pick(M, (640, 128))
    final = target is not None

    def body(*refs):
        if final:
            a_ref, w_ref, r_ref, g_ref, b_ref, t_ref, z_ref, dy_ref, loss_ref = refs
        else:
            a_ref, w_ref, r_ref, g_ref, b_ref, z_ref, h_ref, hb_ref = refs
        z = ALPHA * r_ref[...] + jnp.dot(a_ref[...], w_ref[...], preferred_element_type=F32)
        z_ref[...] = z
        h = _layer_norm(z, g_ref[...], b_ref[...])
        if not final:
            h_ref[...] = h
            hb_ref[...] = h.astype(BF16)
            return
        i = pl.program_id(0)
        row = i * tm + lax.broadcasted_iota(jnp.int32, (tm, 1), 0)
        valid = (row >= N_META) & (row < n_tok)
        err = jnp.where(valid, h - t_ref[...], 0.0)
        dy_ref[...] = err / D

        @pl.when(i == 0)
        def _():
            loss_ref[...] = jnp.zeros_like(loss_ref)

        loss_ref[...] += 0.5 * jnp.sum(jnp.sum(err * err, axis=1, keepdims=True) / D, axis=0, keepdims=True)

    row_blk = lambda cols: pl.BlockSpec((tm, cols), lambda i: (i, 0))
    vec = pl.BlockSpec((1, D), lambda i: (0, 0))
    in_specs = [row_blk(K), pl.BlockSpec((K, D), lambda i: (0, 0)), row_blk(D), vec, vec]
    args = [a, w, res, g, b]
    if final:
        in_specs.append(row_blk(D))
        args.append(target)
        out_specs = [row_blk(D), row_blk(D), pl.BlockSpec((8, LANES), lambda i: (0, 0))]
        out_shape = [jax.ShapeDtypeStruct((M, D), F32)] * 2 + [jax.ShapeDtypeStruct((8, LANES), F32)]
    else:
        out_specs = [row_blk(D)] * 3
        out_shape = [jax.ShapeDtypeStruct((M, D), F32)] * 2 + [jax.ShapeDtypeStruct((M, D), BF16)]
    return pl.pallas_call(
        body, name=name, grid=(M // tm,), in_specs=in_specs, out_specs=out_specs, out_shape=out_shape,
        compiler_params=_cparams(),
    )(*args)


def _ln_bwd(dh, z, g, *, name, dz_next=None):
    M, D = z.shape
    tm = _pick(M, (640, 128))
    has_next = dz_next is not None

    def body(*refs):
        if has_next:
            dh_ref, nx_ref, z_ref, g_ref, dz_ref, dzb_ref, dg_ref, db_ref = refs
            dh_v = dh_ref[...] + ALPHA * nx_ref[...]
        else:
            dh_ref, z_ref, g_ref, dz_ref, dzb_ref, dg_ref, db_ref = refs
            dh_v = dh_ref[...]
        z_v = z_ref[...]
        mu = jnp.mean(z_v, axis=-1, keepdims=True)
        xc = z_v - mu
        rstd = lax.rsqrt(jnp.mean(xc * xc, axis=-1, keepdims=True) + LN_EPS)
        xhat = xc * rstd
        dxhat = dh_v * g_ref[...]
        dz = rstd * (dxhat - jnp.mean(dxhat, axis=-1, keepdims=True)
                     - xhat * jnp.mean(dxhat * xhat, axis=-1, keepdims=True))
        dz_ref[...] = dz
        dzb_ref[...] = dz.astype(BF16)

        @pl.when(pl.program_id(0) == 0)
        def _():
            dg_ref[...] = jnp.zeros_like(dg_ref)
            db_ref[...] = jnp.zeros_like(db_ref)

        dg_ref[...] += jnp.sum(dh_v * xhat, axis=0, keepdims=True)
        db_ref[...] += jnp.sum(dh_v, axis=0, keepdims=True)

    row_blk = pl.BlockSpec((tm, D), lambda i: (i, 0))
    vec = pl.BlockSpec((1, D), lambda i: (0, 0))
    args = [dh] + ([dz_next] if has_next else []) + [z, g]
    in_specs = [row_blk] * (len(args) - 1) + [vec]
    return pl.pallas_call(
        body, name=name, grid=(M // tm,), in_specs=in_specs,
        out_specs=[row_blk, row_blk, vec, vec],
        out_shape=[jax.ShapeDtypeStruct((M, D), F32), jax.ShapeDtypeStruct((M, D), BF16),
                   jax.ShapeDtypeStruct((1, D), F32), jax.ShapeDtypeStruct((1, D), F32)],
        compiler_params=_cparams(),
    )(*args)


def _pool_fwd(h0, pw, scale, g, b, *, name):
    M, D = h0.shape
    n_groups, G, _ = pw.shape
    tm = _pick(M, (640, 128))

    def body(x_ref, halo_ref, pw_ref, sc_ref, g_ref, b_ref, z_ref, h_ref, hb_ref, diff_ref, ext_ref, mix_ref):
        i = pl.program_id(0)
        x = x_ref[...]
        ext_ref[0:POOL_HALO, :] = jnp.where(i == 0, 0.0, halo_ref[...])
        ext_ref[POOL_HALO:, :] = x
        tok = i * tm + lax.broadcasted_iota(jnp.int32, (tm, 1), 0)
        for gi, win in enumerate(POOL_WINDOWS):
            cols = slice(gi * G, (gi + 1) * G)
            xs = x[:, cols]
            s = xs
            for k in range(1, win):
                s = s + ext_ref[pl.ds(POOL_HALO - k, tm), cols]
            count = jnp.minimum(tok + 1, win).astype(F32)
            d = (s / count - xs).astype(BF16)
            diff_ref[:, cols] = d
            mix_ref[:, cols] = jnp.dot(d, pw_ref[gi], preferred_element_type=F32)
        z = ALPHA * x + mix_ref[...] * sc_ref[...]
        z_ref[...] = z
        h = _layer_norm(z, g_ref[...], b_ref[...])
        h_ref[...] = h
        hb_ref[...] = h.astype(BF16)

    row_blk = pl.BlockSpec((tm, D), lambda i: (i, 0))
    vec = pl.BlockSpec((1, D), lambda i: (0, 0))
    halo = pl.BlockSpec((POOL_HALO, D), lambda i: (jnp.maximum(i * (tm // POOL_HALO) - 1, 0), 0))
    return pl.pallas_call(
        body, name=name, grid=(M // tm,),
        in_specs=[row_blk, halo, pl.BlockSpec((n_groups, G, G), lambda i: (0, 0, 0)), vec, vec, vec],
        out_specs=[row_blk] * 4,
        out_shape=[jax.ShapeDtypeStruct((M, D), F32)] * 2 + [jax.ShapeDtypeStruct((M, D), BF16)] * 2,
        scratch_shapes=[pltpu.VMEM((tm + POOL_HALO, D), F32), pltpu.VMEM((tm, D), F32)],
        compiler_params=_cparams(),
    )(h0, h0, pw, scale, g, b)


def _pool_bwd(dz, diff, pw, pw_t, scale, *, name):
    M, D = dz.shape
    n_groups, G, _ = pw.shape
    tm = _pick(M, (640, 128))
    nt = M // tm

    def body(dz_ref, halo_ref, diff_ref, pw_ref, pwt_ref, sc_ref, dh_ref, dpw_ref, dsc_ref, ext_ref, q_ref):
        i = pl.program_id(0)
        dz_v = dz_ref[...]
        ext_ref[0:tm, :] = dz_v
        ext_ref[tm:, :] = jnp.where(i == nt - 1, 0.0, halo_ref[...])
        tok = i * tm + lax.broadcasted_iota(jnp.int32, (tm + POOL_HALO, 1), 0)

        @pl.when(i == 0)
        def _():
            dpw_ref[...] = jnp.zeros_like(dpw_ref)
            dsc_ref[...] = jnp.zeros_like(dsc_ref)

        for gi, win in enumerate(POOL_WINDOWS):
            cols = slice(gi * G, (gi + 1) * G)
            dmix = (ext_ref[:, cols] * sc_ref[:, cols]).astype(BF16)
            ddiff = jnp.dot(dmix, pwt_ref[gi], preferred_element_type=F32)
            count = jnp.minimum(tok + 1, win).astype(F32)
            q_ref[:, cols] = ddiff / count
            acc = -ddiff[0:tm]
            for k in range(win):
                acc = acc + q_ref[pl.ds(k, tm), cols]
            dh_ref[:, cols] = ALPHA * dz_v[:, cols] + acc
            d = diff_ref[:, cols]
            dpw_ref[gi] += lax.dot_general(d, dmix[0:tm], (((0,), (0,)), ((), ())), preferred_element_type=F32)
            mixed = jnp.dot(d, pw_ref[gi], preferred_element_type=F32)
            dsc_ref[:, cols] += jnp.sum(dz_v[:, cols] * mixed, axis=0, keepdims=True)

    row_blk = pl.BlockSpec((tm, D), lambda i: (i, 0))
    vec = pl.BlockSpec((1, D), lambda i: (0, 0))
    per_tile = tm // POOL_HALO
    halo = pl.BlockSpec((POOL_HALO, D), lambda i: (jnp.minimum((i + 1) * per_tile, nt * per_tile - 1), 0))
    wblk = pl.BlockSpec((n_groups, G, G), lambda i: (0, 0, 0))
    return pl.pallas_call(
        body, name=name, grid=(nt,),
        in_specs=[row_blk, halo, row_blk, wblk, wblk, vec],
        out_specs=[row_blk, wblk, vec],
        out_shape=[jax.ShapeDtypeStruct((M, D), F32), jax.ShapeDtypeStruct((n_groups, G, G), F32),
                   jax.ShapeDtypeStruct((1, D), F32)],
        scratch_shapes=[pltpu.VMEM((tm + POOL_HALO, D), F32), pltpu.VMEM((tm + POOL_HALO, D), F32)],
        compiler_params=_cparams(),
    )(dz, dz, diff, pw, pw_t, scale)


def _glu_interleave(x):
    s = x.shape
    n = s[-1] // (2 * GLU_CHUNK)
    return jnp.swapaxes(x.reshape(s[:-1] + (2, n, GLU_CHUNK)), -3, -2).reshape(s)


def _glu_deinterleave(x):
    s = x.shape
    n = s[-1] // (2 * GLU_CHUNK)
    return jnp.swapaxes(x.reshape(s[:-1] + (n, 2, GLU_CHUNK)), -3, -2).reshape(s)


def _conv_taps(ext_ref, u, tm):
    return ext_ref[pl.ds(CONV_HALO - 2, tm), :], ext_ref[pl.ds(CONV_HALO - 1, tm), :], u


def _conv_glu_fwd(u, cw, cb, *, name):
    M, F2 = u.shape
    tm = _pick(M, (640, 128))
    tc = 2 * GLU_CHUNK

    def body(u_ref, halo_ref, w_ref, b_ref, o_ref, ext_ref):
        i = pl.program_id(0)
        u_v = u_ref[...]
        ext_ref[0:CONV_HALO, :] = jnp.where(i == 0, 0.0, halo_ref[...])
        ext_ref[CONV_HALO:, :] = u_v
        u2, u1, u0 = _conv_taps(ext_ref, u_v, tm)
        c = b_ref[...] + w_ref[0:1, :] * u2 + w_ref[1:2, :] * u1 + w_ref[2:3, :] * u0
        a, g = c[:, :GLU_CHUNK], c[:, GLU_CHUNK:]
        o_ref[...] = (a * jax.nn.sigmoid(a) * g).astype(BF16)

    per_tile = tm // CONV_HALO
    return pl.pallas_call(
        body, name=name, grid=(M // tm, F2 // tc),
        in_specs=[pl.BlockSpec((tm, tc), lambda i, j: (i, j)),
                  pl.BlockSpec((CONV_HALO, tc), lambda i, j: (jnp.maximum(i * per_tile - 1, 0), j)),
                  pl.BlockSpec((3, tc), lambda i, j: (0, j)), pl.BlockSpec((1, tc), lambda i, j: (0, j))],
        out_specs=pl.BlockSpec((tm, GLU_CHUNK), lambda i, j: (i, j)),
        out_shape=jax.ShapeDtypeStruct((M, F2 // 2), BF16),
        scratch_shapes=[pltpu.VMEM((tm + CONV_HALO, tc), F32)],
        compiler_params=_cparams(),
    )(u, u, cw, cb)


def _conv_glu_bwd(u, dact, cw, cb, *, name):
    M, F2 = u.shape
    tm = _pick(M, (640, 128))
    nt = M // tm
    tc = 2 * GLU_CHUNK

    def body(u_ref, halo_ref, da_ref, w_ref, b_ref, du_ref, dw_ref, db_ref, ext_ref, dcx_ref, carry_ref):
        i = pl.program_id(1)
        u_v = u_ref[...]
        ext_ref[0:CONV_HALO, :] = jnp.where(i == nt - 1, 0.0, halo_ref[...])
        ext_ref[CONV_HALO:, :] = u_v
        u2, u1, u0 = _conv_taps(ext_ref, u_v, tm)
        c = b_ref[...] + w_ref[0:1, :] * u2 + w_ref[1:2, :] * u1 + w_ref[2:3, :] * u0
        a, g = c[:, :GLU_CHUNK], c[:, GLU_CHUNK:]
        sig = jax.nn.sigmoid(a)
        dact_v = da_ref[...]
        d_a = dact_v * g * (sig * (1.0 + a * (1.0 - sig)))
        d_g = dact_v * (a * sig)
        dc = jnp.concatenate([d_a, d_g], axis=1)

        @pl.when(i == 0)
        def _():
            dw_ref[...] = jnp.zeros_like(dw_ref)
            db_ref[...] = jnp.zeros_like(db_ref)
            carry_ref[...] = jnp.zeros_like(carry_ref)

        db_ref[...] += jnp.sum(dc, axis=0, keepdims=True)
        dw_ref[0:1, :] += jnp.sum(dc * u2, axis=0, keepdims=True)
        dw_ref[1:2, :] += jnp.sum(dc * u1, axis=0, keepdims=True)
        dw_ref[2:3, :] += jnp.sum(dc * u0, axis=0, keepdims=True)
        dcx_ref[0:tm, :] = dc
        dcx_ref[tm:, :] = carry_ref[...]
        du = w_ref[2:3, :] * dc + w_ref[1:2, :] * dcx_ref[pl.ds(1, tm), :] + w_ref[0:1, :] * dcx_ref[pl.ds(2, tm), :]
        du_ref[...] = du.astype(BF16)
        carry_ref[...] = dcx_ref[0:CONV_HALO, :]

    per_tile = tm // CONV_HALO
    rev = lambda i: nt - 1 - i
    return pl.pallas_call(
        body, name=name, grid=(F2 // tc, nt),
        in_specs=[pl.BlockSpec((tm, tc), lambda j, i: (rev(i), j)),
                  pl.BlockSpec((CONV_HALO, tc), lambda j, i: (jnp.maximum(rev(i) * per_tile - 1, 0), j)),
                  pl.BlockSpec((tm, GLU_CHUNK), lambda j, i: (rev(i), j)),
                  pl.BlockSpec((3, tc), lambda j, i: (0, j)), pl.BlockSpec((1, tc), lambda j, i: (0, j))],
        out_specs=[pl.BlockSpec((tm, tc), lambda j, i: (rev(i), j)),
                   pl.BlockSpec((3, tc), lambda j, i: (0, j)), pl.BlockSpec((1, tc), lambda j, i: (0, j))],
        out_shape=[jax.ShapeDtypeStruct((M, F2), BF16), jax.ShapeDtypeStruct((3, F2), F32),
                   jax.ShapeDtypeStruct((1, F2), F32)],
        scratch_shapes=[pltpu.VMEM((tm + CONV_HALO, tc), F32), pltpu.VMEM((tm + CONV_HALO, tc), F32),
                        pltpu.VMEM((CONV_HALO, tc), F32)],
        compiler_params=_cparams(),
    )(u, u, dact, cw, cb)


def _split3(x):
    hi = x.astype(BF16)
    r = x - hi.astype(F32)
    mid = r.astype(BF16)
    lo = (r - mid.astype(F32)).astype(BF16)
    return hi, mid, lo


def _tri_sum(tri, x):
    return sum(jnp.dot(tri, part, preferred_element_type=F32) for part in _split3(x))


def _log_sigmoid(x):
    return jnp.minimum(x, 0.0) - jnp.log1p(jnp.exp(-jnp.abs(x)))


def _forget_cumsum(pre, bias, *, name):
    M, C = pre.shape
    tm = _pick(M, (640, 128))

    def body(p_ref, b_ref, c_ref, carry_ref):
        i = pl.program_id(0)

        @pl.when(i == 0)
        def _():
            carry_ref[...] = jnp.zeros_like(carry_ref)

        logf = _log_sigmoid(p_ref[...] + b_ref[...])
        r = lax.broadcasted_iota(jnp.int32, (tm, tm), 0)
        s = lax.broadcasted_iota(jnp.int32, (tm, tm), 1)
        c_ref[...] = _tri_sum((s <= r).astype(BF16), logf) + carry_ref[...]
        carry_ref[...] = c_ref[pl.ds(tm - 1, 1), :]

    return pl.pallas_call(
        body, name=name, grid=(M // tm,),
        in_specs=[pl.BlockSpec((tm, C), lambda i: (i, 0)), pl.BlockSpec((1, C), lambda i: (0, 0))],
        out_specs=pl.BlockSpec((tm, C), lambda i: (i, 0)),
        out_shape=jax.ShapeDtypeStruct((M, C), F32),
        scratch_shapes=[pltpu.VMEM((1, C), F32)],
        compiler_params=_cparams(),
    )(pre, bias)


def _forget_cumsum_bwd(dc, pre, bias, *, name):
    M, C = pre.shape
    tm = _pick(M, (640, 128))
    nt = M // tm

    def body(dc_ref, p_ref, b_ref, dp_ref, db_ref, carry_ref, run_ref):
        i = pl.program_id(0)

        @pl.when(i == 0)
        def _():
            carry_ref[...] = jnp.zeros_like(carry_ref)
            db_ref[...] = jnp.zeros_like(db_ref)

        r = lax.broadcasted_iota(jnp.int32, (tm, tm), 0)
        s = lax.broadcasted_iota(jnp.int32, (tm, tm), 1)
        run_ref[...] = _tri_sum((s >= r).astype(BF16), dc_ref[...]) + carry_ref[...]
        carry_ref[...] = run_ref[pl.ds(0, 1), :]
        dpre = run_ref[...] * jax.nn.sigmoid(-(p_ref[...] + b_ref[...]))
        dp_ref[...] = dpre.astype(BF16)
        db_ref[...] += jnp.sum(dpre, axis=0, keepdims=True)

    rev_blk = pl.BlockSpec((tm, C), lambda i: (nt - 1 - i, 0))
    vec = pl.BlockSpec((1, C), lambda i: (0, 0))
    return pl.pallas_call(
        body, name=name, grid=(nt,),
        in_specs=[rev_blk, rev_blk, vec], out_specs=[rev_blk, vec],
        out_shape=[jax.ShapeDtypeStruct((M, C), BF16), jax.ShapeDtypeStruct((1, C), F32)],
        scratch_shapes=[pltpu.VMEM((1, C), F32), pltpu.VMEM((tm, C), F32)],
        compiler_params=_cparams(),
    )(dc, pre, bias)


def _causal_mask(tm):
    row = lax.broadcasted_iota(jnp.int32, (tm, tm), 0)
    col = lax.broadcasted_iota(jnp.int32, (tm, tm), 1)
    return col <= row


def _dot_nt(a, b):
    return lax.dot_general(a, b, (((1,), (1,)), ((), ())), preferred_element_type=F32)


def _dot_tn(a, b):
    return lax.dot_general(a, b, (((0,), (0,)), ((), ())), preferred_element_type=F32)


def _attn_specs(H, M, dh, tm):
    nt = M // tm
    q_blk = pl.BlockSpec((1, tm, dh), lambda h, i: (h, i, 0))
    col_blk = pl.BlockSpec((1, tm, 1), lambda h, i: (h, i, 0))
    kv_blk = pl.BlockSpec((1, M, dh), lambda h, i: (h, 0, 0))
    ck_blk = pl.BlockSpec((1, nt, 1, tm), lambda h, i: (h, 0, 0, 0))
    return q_blk, col_blk, kv_blk, ck_blk


def _attn_fwd(q, k, v, cq, ck, *, name):
    H, M, dh = q.shape
    tm = ck.shape[-1]
    q_blk, col_blk, kv_blk, ck_blk = _attn_specs(H, M, dh, tm)

    def body(q_ref, k_ref, v_ref, cq_ref, ck_ref, o_ref, lse_ref):
        i = pl.program_id(1)
        qs = q_ref[0] * jnp.asarray(dh ** -0.5, BF16)
        cq_v = cq_ref[0]

        def step(j, carry, masked):
            m, l, acc = carry
            off = pl.multiple_of(j * tm, tm)
            s = _dot_nt(qs, k_ref[0, pl.ds(off, tm), :]) + (cq_v - ck_ref[0, j])
            if masked:
                s = jnp.where(_causal_mask(tm), s, NEG_INF)
            m_new = jnp.maximum(m, jnp.max(s, axis=1, keepdims=True))
            a = jnp.exp(m - m_new)
            p = jnp.exp(s - m_new)
            l = a * l + jnp.sum(p, axis=1, keepdims=True)
            acc = a * acc + jnp.dot(p.astype(BF16), v_ref[0, pl.ds(off, tm), :], preferred_element_type=F32)
            return m_new, l, acc

        init = (jnp.full((tm, 1), NEG_INF, F32), jnp.zeros((tm, 1), F32), jnp.zeros((tm, dh), F32))
        carry = lax.fori_loop(0, i, lambda j, c: step(j, c, False), init)
        m, l, acc = step(i, carry, True)
        o_ref[0] = (acc / l).astype(BF16)
        lse_ref[0] = m + jnp.log(l)

    return pl.pallas_call(
        body, name=name, grid=(H, M // tm),
        in_specs=[q_blk, kv_blk, kv_blk, col_blk, ck_blk], out_specs=[q_blk, col_blk],
        out_shape=[jax.ShapeDtypeStruct((H, M, dh), BF16), jax.ShapeDtypeStruct((H, M, 1), F32)],
        compiler_params=_cparams(),
    )(q, k, v, cq, ck)


def _attn_bwd(q, k, v, o, do, lse, cq, ck, *, name):
    H, M, dh = q.shape
    tm = ck.shape[-1]
    q_blk, col_blk, kv_blk, ck_blk = _attn_specs(H, M, dh, tm)

    def body(q_ref, o_ref, do_ref, lse_ref, cq_ref, k_ref, v_ref, ck_ref, dq_ref, dk_ref, dv_ref, dcq_ref, dck_ref):
        i = pl.program_id(1)

        @pl.when(i == 0)
        def _():
            dk_ref[...] = jnp.zeros_like(dk_ref)
            dv_ref[...] = jnp.zeros_like(dv_ref)
            dck_ref[...] = jnp.zeros_like(dck_ref)

        scale = jnp.asarray(dh ** -0.5, BF16)
        qs = q_ref[0] * scale
        do_v = do_ref[0]
        delta = jnp.sum(do_v.astype(F32) * o_ref[0].astype(F32), axis=1, keepdims=True)
        lse_v = lse_ref[0]
        cq_v = cq_ref[0]

        def step(j, carry, masked):
            dq, dcq = carry
            off = pl.multiple_of(j * tm, tm)
            rows = pl.ds(off, tm)
            k_j = k_ref[0, rows, :]
            s = _dot_nt(qs, k_j) + (cq_v - ck_ref[0, j])
            p = jnp.exp(s - lse_v)
            if masked:
                p = jnp.where(_causal_mask(tm), p, 0.0)
            dv_ref[0, rows, :] += _dot_tn(p.astype(BF16), do_v)
            dp = _dot_nt(do_v, v_ref[0, rows, :])
            ds = p * (dp - delta)
            ds_b = ds.astype(BF16)
            dk_ref[0, rows, :] += _dot_tn(ds_b, qs)
            dck_ref[0, j] += -jnp.sum(ds, axis=0, keepdims=True)
            return dq + jnp.dot(ds_b, k_j, preferred_element_type=F32), dcq + jnp.sum(ds, axis=1, keepdims=True)

        init = (jnp.zeros((tm, dh), F32), jnp.zeros((tm, 1), F32))
        dq, dcq = step(i, lax.fori_loop(0, i, lambda j, c: step(j, c, False), init), True)
        dq_ref[0] = (dq * dh ** -0.5).astype(BF16)
        dcq_ref[0] = dcq

    return pl.pallas_call(
        body, name=name, grid=(H, M // tm),
        in_specs=[q_blk, q_blk, q_blk, col_blk, col_blk, kv_blk, kv_blk, ck_blk],
        out_specs=[q_blk, kv_blk, kv_blk, col_blk, ck_blk],
        out_shape=[jax.ShapeDtypeStruct((H, M, dh), BF16), jax.ShapeDtypeStruct((H, M, dh), F32),
                   jax.ShapeDtypeStruct((H, M, dh), F32), jax.ShapeDtypeStruct((H, M, 1), F32),
                   jax.ShapeDtypeStruct(ck.shape, F32)],
        compiler_params=_cparams(),
    )(q, o, do, lse, cq, k, v, ck)


def kernel(x, meta, pool_w, pool_scale, w_kv, w_f, b_f, w_q, w_o, ffn_w_in, ffn_conv_w, ffn_conv_b, ffn_w_out, ln_g, ln_b, loss_target, m_meta, m_pool_w, m_pool_scale, m_w_kv, m_w_f, m_b_f, m_w_q, m_w_o, m_ffn_w_in, m_ffn_conv_w, m_ffn_conv_b, m_ffn_w_out, m_ln_g, m_ln_b, v_meta, v_pool_w, v_pool_scale, v_w_kv, v_w_f, v_b_f, v_w_q, v_w_o, v_ffn_w_in, v_ffn_conv_w, v_ffn_conv_b, v_ffn_w_out, v_ln_g, v_ln_b):
    local = dict(meta=meta, pool_w=pool_w, pool_scale=pool_scale, w_kv=w_kv, w_f=w_f, b_f=b_f, w_q=w_q, w_o=w_o,
                 ffn_w_in=ffn_w_in, ffn_conv_w=ffn_conv_w, ffn_conv_b=ffn_conv_b, ffn_w_out=ffn_w_out,
                 ln_g=ln_g, ln_b=ln_b)
    mom1 = dict(meta=m_meta, pool_w=m_pool_w, pool_scale=m_pool_scale, w_kv=m_w_kv, w_f=m_w_f, b_f=m_b_f,
                w_q=m_w_q, w_o=m_w_o, ffn_w_in=m_ffn_w_in, ffn_conv_w=m_ffn_conv_w, ffn_conv_b=m_ffn_conv_b,
                ffn_w_out=m_ffn_w_out, ln_g=m_ln_g, ln_b=m_ln_b)
    mom2 = dict(meta=v_meta, pool_w=v_pool_w, pool_scale=v_pool_scale, w_kv=v_w_kv, w_f=v_w_f, b_f=v_b_f,
                w_q=v_w_q, w_o=v_w_o, ffn_w_in=v_ffn_w_in, ffn_conv_w=v_ffn_conv_w, ffn_conv_b=v_ffn_conv_b,
                ffn_w_out=v_ffn_w_out, ln_g=v_ln_g, ln_b=v_ln_b)
    axis_of = dict(PARAMS)

    S, D = x.shape[1], x.shape[2]
    H = b_f.shape[0]
    dh = D // H
    n_tok = N_META + S
    M = _round_up(n_tok, LANES)
    tm = _pick(M, (640, 128))
    nt = M // tm
    F2 = ffn_conv_b.shape[1]
    F = F2 // 2
    depth = ffn_conv_b.shape[0]

    small_sharded = [n for n in SMALL if axis_of[n] is not None]
    got = _all_gather([local[n].astype(BF16) for n in BIG] + [_pack([local[n] for n in small_sharded])],
                      "gather_weights")
    wb = {n: _from_blocks(blk, axis_of[n]) for n, blk in zip(BIG, got)}
    small_blocks = _unpack(got[-1], [local[n].shape for n in small_sharded], lead=(N_DEV,))
    wf32 = {n: _from_blocks(blk, axis_of[n]) for n, blk in zip(small_sharded, small_blocks)}

    pw = wb["pool_w"][0]
    pw_t = jnp.swapaxes(pw, 1, 2)
    wf_pad = jnp.pad(wf32["w_f"].astype(BF16), ((0, 0), (0, LANES - H)))
    w_qkv = jnp.concatenate([wb["w_kv"], wb["w_q"][0]], axis=1)
    w_att_t = jnp.concatenate([w_qkv, wf_pad], axis=1).T
    wo = wb["w_o"][0]
    wo_t = wo.T
    w_in = _glu_interleave(wb["ffn_w_in"])
    w_in_t = jnp.swapaxes(w_in, 1, 2)
    w_out = wb["ffn_w_out"]
    w_out_t = jnp.swapaxes(w_out, 1, 2)
    conv_w = _glu_interleave(wf32["ffn_conv_w"])
    conv_b = _glu_interleave(ffn_conv_b)[:, None, :]
    scale = wf32["pool_scale"]
    g_ln, b_ln = wf32["ln_g"], wf32["ln_b"]
    ln = lambda i, j: (g_ln[i, j][None, :], b_ln[i, j][None, :])
    bias_f = jnp.pad(b_f, (0, LANES - H))[None, :]

    pad_rows = M - n_tok
    h0 = jnp.concatenate([wf32["meta"], x[0], jnp.zeros((pad_rows, D), F32)], axis=0)
    target = jnp.concatenate([jnp.zeros((N_META, D), F32), loss_target[0], jnp.zeros((pad_rows, D), F32)], axis=0)

    z1, h1, h1b, diff = _pool_fwd(h0, pw, scale, *ln(0, 0), name="pool_fwd")
    u0 = _mm(h1b, w_in[0], name="ffn0_up", out_dtype=F32)
    act0 = _conv_glu_fwd(u0, conv_w[0], conv_b[0], name="ffn0_conv")
    z2, h2, h2b = _mm_ln(act0, w_out[0], h1, *ln(0, 1), name="ffn0_down_ln")

    qkv = _mm(h2b, w_qkv, name="attn_qkv", out_dtype=BF16)
    pre = _mm(h2b, wf_pad, name="attn_gate", out_dtype=F32)
    c = _forget_cumsum(pre, bias_f, name="attn_cumsum")
    kvq = jnp.transpose(qkv.reshape(M, 3, H, dh), (1, 2, 0, 3))
    k_h, v_h, q_h = kvq[0], kvq[1], kvq[2]
    c_t = c[:, :H].T
    cq = c_t[:, :, None]
    ck = c_t.reshape(H, nt, 1, tm)
    o_h, lse = _attn_fwd(q_h, k_h, v_h, cq, ck, name="attn_fwd")
    o = jnp.transpose(o_h, (1, 0, 2)).reshape(M, D)
    z3, h3, h3b = _mm_ln(o, wo, h2, *ln(1, 0), name="attn_out_ln")

    u1 = _mm(h3b, w_in[1], name="ffn1_up", out_dtype=F32)
    act1 = _conv_glu_fwd(u1, conv_w[1], conv_b[1], name="ffn1_conv")
    z4, dy, loss_part = _mm_ln(act1, w_out[1], h3, *ln(1, 1), name="ffn1_down_loss", target=target, n_tok=n_tok)
    loss = lax.psum(loss_part[0, 0], ("x", "y", "c"))

    grads = {}

    def ffn_bwd(layer, dz_b, u, act, h_in_b, tag):
        dact = _mm(dz_b, w_out_t[layer], name=tag + "_dact", out_dtype=F32)
        du, dcw, dcb = _conv_glu_bwd(u, dact, conv_w[layer], conv_b[layer], name=tag + "_conv_bwd")
        dh = _mm(du, w_in_t[layer], name=tag + "_dh", out_dtype=F32)
        d_w_out = _mm_tn(act, dz_b, name=tag + "_dw_out")
        d_w_in = _glu_deinterleave(_mm_tn(h_in_b, du, name=tag + "_dw_in"))
        return dh, d_w_in, d_w_out, _glu_deinterleave(dcw), _glu_deinterleave(dcb)[0]

    dz4, dz4b, dg11, db11 = _ln_bwd(dy, z4, g_ln[1, 1][None, :], name="ln4_bwd")
    dh3, dwin1, dwout1, dcw1, dcb1 = ffn_bwd(1, dz4b, u1, act1, h3b, "ffn1")
    dz3, dz3b, dg10, db10 = _ln_bwd(dh3, z3, g_ln[1, 0][None, :], name="ln3_bwd", dz_next=dz4)

    do = _mm(dz3b, wo_t, name="attn_do", out_dtype=BF16)
    grads["w_o"] = _mm_tn(o, dz3b, name="attn_dw_o")[None]
    do_h = jnp.transpose(do.reshape(M, H, dh), (1, 0, 2))
    dq_h, dk_h, dv_h, dcq, dck = _attn_bwd(q_h, k_h, v_h, o_h, do_h, lse, cq, ck, name="attn_bwd")
    dc = jnp.pad((dcq[:, :, 0] + dck.reshape(H, M)).T, ((0, 0), (0, LANES - H)))
    dpre, dbias = _forget_cumsum_bwd(dc, pre, bias_f, name="attn_cumsum_bwd")
    flat = lambda t: jnp.transpose(t, (1, 0, 2)).reshape(M, D).astype(BF16)
    d_att = jnp.concatenate([flat(dk_h), flat(dv_h), flat(dq_h), dpre], axis=1)
    dh2 = _mm(d_att, w_att_t, name="attn_dh", out_dtype=F32)
    d_w_att = _mm_tn(h2b, d_att, name="attn_dw_qkv")
    grads["w_kv"] = d_w_att[:, :2 * D]
    grads["w_q"] = d_w_att[:, 2 * D:3 * D][None]
    grads["w_f"] = d_w_att[:, 3 * D:3 * D + H]
    grads["b_f"] = dbias[0, :H]
    dz2, dz2b, dg01, db01 = _ln_bwd(dh2, z2, g_ln[0, 1][None, :], name="ln2_bwd", dz_next=dz3)

    dh1, dwin0, dwout0, dcw0, dcb0 = ffn_bwd(0, dz2b, u0, act0, h1b, "ffn0")
    dz1, _, dg00, db00 = _ln_bwd(dh1, z1, g_ln[0, 0][None, :], name="ln1_bwd", dz_next=dz2)
    dh0, dpw, dscale = _pool_bwd(dz1, diff, pw, pw_t, scale, name="pool_bwd")

    grads["meta"] = dh0[:N_META]
    grads["pool_w"] = dpw[None]
    grads["pool_scale"] = dscale
    grads["ffn_w_in"] = jnp.stack([dwin0, dwin1])
    grads["ffn_w_out"] = jnp.stack([dwout0, dwout1])
    grads["ffn_conv_w"] = jnp.stack([dcw0, dcw1])
    grads["ffn_conv_b"] = jnp.stack([dcb0, dcb1])
    grads["ln_g"] = jnp.stack([jnp.stack([dg00[0], dg01[0]]), jnp.stack([dg10[0], dg11[0]])])
    grads["ln_b"] = jnp.stack([jnp.stack([db00[0], db01[0]]), jnp.stack([db10[0], db11[0]])])
    grad_x = dh0[N_META:n_tok][None]

    blocks = {n: _to_blocks(grads[n], axis_of[n]) for n, _ in PARAMS}
    recv = _exchange([blocks[n] for n in BIG] + [_pack([blocks[n] for n in SMALL], lead=(N_DEV,))], "grad_exchange")
    results = {}
    for n, r in zip(BIG, recv):
        shape = local[n].shape
        as2d = lambda t: t.reshape(-1, shape[-1])
        outs = _adamw(r.reshape(N_DEV, -1, shape[-1]), as2d(local[n]), as2d(mom1[n]), as2d(mom2[n]), name="adamw_" + n)
        results[n] = [o_.reshape(shape) for o_ in outs]
    outs = _adamw(recv[-1], *[_pack([d[n] for n in SMALL]) for d in (local, mom1, mom2)], name="adamw_small")
    small_out = [_unpack(o_, [local[n].shape for n in SMALL]) for o_ in outs]
    for i, n in enumerate(SMALL):
        results[n] = [small_out[k][i] for k in range(4)]
    return (loss, grad_x, *[results[n][k] for k in range(4) for n, _ in PARAMS])
```

```python
import jax
import jax.numpy as jnp
from jax import lax
from jax.experimental import pallas as pl
from jax.experimental.pallas import tpu as pltpu

F32, BF16 = jnp.float32, jnp.bfloat16
MESH = pl.DeviceIdType.MESH

N_DEV = 8
N_META = 16
POOL_WINDOWS = (2, 4, 8, 16)
POOL_HALO = 16
CONV_HALO = 8
ALPHA = 4.0 ** 0.25
LN_EPS = 1e-5
NEG_INF = -1e30
ADAM_LR, ADAM_B1, ADAM_B2, ADAM_EPS, ADAM_WD, ADAM_STEP = 0.001, 0.9, 0.999, 1e-08, 0.01, 10

LANES = 128
PACK_COLS = 1024
ADAM_TILE_BYTES = 4 << 20
GLU_CHUNK = 256
VMEM_LIMIT = 56 * 1024 * 1024

PARAMS = (("meta", 1), ("pool_w", 2), ("pool_scale", 1), ("w_kv", 1), ("w_f", 0), ("b_f", None),
          ("w_q", 1), ("w_o", 1), ("ffn_w_in", 2), ("ffn_conv_w", 2), ("ffn_conv_b", None),
          ("ffn_w_out", 1), ("ln_g", 2), ("ln_b", 2))
BIG = ("pool_w", "w_kv", "w_q", "w_o", "ffn_w_in", "ffn_w_out")
SMALL = ("meta", "pool_scale", "w_f", "b_f", "ffn_conv_w", "ffn_conv_b", "ln_g", "ln_b")


def _cparams(**kw):
    return pltpu.CompilerParams(vmem_limit_bytes=VMEM_LIMIT, **kw)


def _pick(n, cands):
    for c in cands:
        if n % c == 0:
            return c
    return n


def _round_up(n, m):
    return (n + m - 1) // m * m


def _pack(pieces, lead=()):
    flat = []
    for p in pieces:
        v = p.reshape(lead + (-1,))
        flat.append(jnp.pad(v, [(0, 0)] * len(lead) + [(0, _round_up(v.shape[-1], PACK_COLS) - v.shape[-1])]))
    v = jnp.concatenate(flat, axis=-1)
    rows = _round_up(v.shape[-1] // PACK_COLS, 8)
    v = jnp.pad(v, [(0, 0)] * len(lead) + [(0, rows * PACK_COLS - v.shape[-1])])
    return v.reshape(lead + (rows, PACK_COLS))


def _unpack(buf, shapes, lead=()):
    flat = buf.reshape(lead + (-1,))
    out, off = [], 0
    for s in shapes:
        n = 1
        for d in s:
            n *= d
        out.append(flat[..., off:off + n].reshape(lead + tuple(s)))
        off += _round_up(n, PACK_COLS)
    return out


def _to_blocks(full, axis):
    if axis is None:
        return jnp.broadcast_to(full[None], (N_DEV,) + full.shape)
    s = full.shape
    x = full.reshape(s[:axis] + (N_DEV, s[axis] // N_DEV) + s[axis + 1:])
    return jnp.moveaxis(x, axis, 0)


def _from_blocks(blocks, axis):
    x = jnp.moveaxis(blocks, 0, axis)
    s = x.shape
    return x.reshape(s[:axis] + (s[axis] * s[axis + 1],) + s[axis + 2:])


def _coords():
    return lax.axis_index("x"), lax.axis_index("y"), lax.axis_index("c")


def _flip(pos, mask):
    x, y, c = pos
    return (1 - x if mask & 4 else x, 1 - y if mask & 2 else y, 1 - c if mask & 1 else c)


def _index(pos):
    x, y, c = pos
    return 4 * x + 2 * y + c


def _comm_call(body, name, arrays, out_shapes):
    n = len(arrays)
    hbm = pl.BlockSpec(memory_space=pl.ANY)
    return pl.pallas_call(
        body, name=name, out_shape=out_shapes, in_specs=[hbm] * n, out_specs=[hbm] * n,
        scratch_shapes=[pltpu.SemaphoreType.DMA((7 * n,)), pltpu.SemaphoreType.DMA((7 * n,)),
                        pltpu.SemaphoreType.DMA((n,))],
    )(*arrays)


def _all_gather(blocks, name):
    chip_masks = (4, 2, 6)
    n = len(blocks)

    def body(*refs):
        x_refs, out_refs = refs[:n], refs[n:2 * n]
        send_sems, recv_sems, local_sems = refs[2 * n:]
        me = _coords()
        sibling = _flip(me, 1)

        def copy(a, k, owner, to, from_input=False):
            slot = out_refs[a].at[_index(owner)]
            return pltpu.make_async_remote_copy(
                src_ref=x_refs[a] if from_input else slot, dst_ref=slot,
                send_sem=send_sems.at[7 * a + k], recv_sem=recv_sems.at[7 * a + k],
                device_id=to, device_id_type=MESH)

        mine = [pltpu.make_async_copy(x_refs[a], out_refs[a].at[_index(me)], local_sems.at[a]) for a in range(n)]
        first = [copy(a, 0, me, sibling, True) for a in range(n)]
        first += [copy(a, 1 + j, me, _flip(me, m), True) for j, m in enumerate(chip_masks) for a in range(n)]
        for cp in mine + first:
            cp.start()
        passed = []
        for j, m in enumerate(chip_masks):
            for a in range(n):
                copy(a, 1 + j, _flip(me, m), me).wait_recv()
                passed.append(copy(a, 4 + j, _flip(me, m), sibling))
                passed[-1].start()
        for a in range(n):
            copy(a, 0, sibling, me).wait_recv()
            for j, m in enumerate(chip_masks):
                copy(a, 4 + j, _flip(sibling, m), me).wait_recv()
        for cp in first + passed:
            cp.wait_send()
        for cp in mine:
            cp.wait()

    return _comm_call(body, name, blocks, [jax.ShapeDtypeStruct((N_DEV,) + b.shape, b.dtype) for b in blocks])


def _exchange(segs, name):
    n = len(segs)

    def body(*refs):
        seg_refs, out_refs = refs[:n], refs[n:2 * n]
        send_sems, recv_sems, local_sems = refs[2 * n:]
        me = _coords()

        def copy(a, mask):
            peer = _flip(me, mask)
            return pltpu.make_async_remote_copy(
                src_ref=seg_refs[a].at[_index(peer)], dst_ref=out_refs[a].at[_index(me)],
                send_sem=send_sems.at[7 * a + mask - 1], recv_sem=recv_sems.at[7 * a + mask - 1],
                device_id=peer, device_id_type=MESH)

        def arrival(a, mask):
            peer = _flip(me, mask)
            return pltpu.make_async_remote_copy(
                src_ref=seg_refs[a].at[_index(me)], dst_ref=out_refs[a].at[_index(peer)],
                send_sem=send_sems.at[7 * a + mask - 1], recv_sem=recv_sems.at[7 * a + mask - 1],
                device_id=peer, device_id_type=MESH)

        mine = [pltpu.make_async_copy(seg_refs[a].at[_index(me)], out_refs[a].at[_index(me)], local_sems.at[a])
                for a in range(n)]
        sends = [copy(a, mask) for mask in range(1, N_DEV) for a in range(n)]
        for cp in mine + sends:
            cp.start()
        for mask in range(1, N_DEV):
            for a in range(n):
                arrival(a, mask).wait_recv()
        for cp in sends:
            cp.wait_send()
        for cp in mine:
            cp.wait()

    return _comm_call(body, name, segs, [jax.ShapeDtypeStruct(s.shape, s.dtype) for s in segs])


def _adamw(recv, w, m, v, name):
    rows, cols = w.shape
    tr = max(t for t in range(8, rows + 1, 8) if rows % t == 0 and N_DEV * t * cols * 4 <= ADAM_TILE_BYTES)
    c1 = 1.0 - ADAM_B1 ** ADAM_STEP
    c2 = 1.0 - ADAM_B2 ** ADAM_STEP

    def body(r_ref, w_ref, m_ref, v_ref, g_out, d_out, m_out, v_out):
        g = r_ref[0]
        for s in range(1, N_DEV):
            g = g + r_ref[s]
        m_new = ADAM_B1 * m_ref[...] + (1.0 - ADAM_B1) * g
        v_new = ADAM_B2 * v_ref[...] + (1.0 - ADAM_B2) * (g * g)
        m_hat = m_new / c1
        v_hat = v_new / c2
        g_out[...] = g
        d_out[...] = -ADAM_LR * (m_hat / (jnp.sqrt(v_hat) + ADAM_EPS) + ADAM_WD * w_ref[...])
        m_out[...] = m_new
        v_out[...] = v_new

    tile = pl.BlockSpec((tr, cols), lambda i: (i, 0))
    return pl.pallas_call(
        body, name=name, grid=(rows // tr,),
        in_specs=[pl.BlockSpec((N_DEV, tr, cols), lambda i: (0, i, 0)), tile, tile, tile],
        out_specs=[tile] * 4,
        out_shape=[jax.ShapeDtypeStruct(w.shape, F32)] * 4,
        compiler_params=_cparams(),
    )(recv, w, m, v)


def _mm(a, b, *, name, out_dtype, res=None, res_scale=1.0):
    M, K = a.shape
    N = b.shape[1]
    tm = _pick(M, (640, 128))
    tn = _pick(N, (1408, 1024, 640, 512, 256, 128))
    tk = K if K <= 1024 else _pick(K, (1024, 640, 512, 256, 128))
    nk = K // tk

    def body(*refs):
        a_ref, b_ref = refs[0], refs[1]
        r_ref = refs[2] if res is not None else None
        o_ref = refs[3] if res is not None else refs[2]
        acc_ref = refs[-1] if nk > 1 else None

        def finish(acc):
            if r_ref is not None:
                acc = acc + res_scale * r_ref[...]
            o_ref[...] = acc.astype(out_dtype)

        prod = jnp.dot(a_ref[...], b_ref[...], preferred_element_type=F32)
        if nk == 1:
            finish(prod)
        else:
            k = pl.program_id(2)

            @pl.when(k == 0)
            def _():
                acc_ref[...] = prod

            @pl.when(k > 0)
            def _():
                acc_ref[...] += prod

            @pl.when(k == nk - 1)
            def _():
                finish(acc_ref[...])

    in_specs = [pl.BlockSpec((tm, tk), lambda i, j, k: (i, k)), pl.BlockSpec((tk, tn), lambda i, j, k: (k, j))]
    args = [a, b]
    if res is not None:
        in_specs.append(pl.BlockSpec((tm, tn), lambda i, j, k: (i, j)))
        args.append(res)
    return pl.pallas_call(
        body, name=name, grid=(M // tm, N // tn, nk),
        in_specs=in_specs, out_specs=pl.BlockSpec((tm, tn), lambda i, j, k: (i, j)),
        out_shape=jax.ShapeDtypeStruct((M, N), out_dtype),
        scratch_shapes=[pltpu.VMEM((tm, tn), F32)] if nk > 1 else [],
        compiler_params=_cparams(),
    )(*args)


def _mm_tn(a, b, *, name):
    T, M = a.shape
    N = b.shape[1]
    tt = _pick(T, (1664, 640, 128))
    tm = _pick(M, (1408, 1024, 512, 256, 128))
    tn = _pick(N, (1024, 640, 512, 256, 128))

    def body(a_ref, b_ref, o_ref):
        prod = lax.dot_general(a_ref[...], b_ref[...], (((0,), (0,)), ((), ())), preferred_element_type=F32)

        @pl.when(pl.program_id(2) == 0)
        def _():
            o_ref[...] = prod

        @pl.when(pl.program_id(2) > 0)
        def _():
            o_ref[...] += prod

    return pl.pallas_call(
        body, name=name, grid=(M // tm, N // tn, T // tt),
        in_specs=[pl.BlockSpec((tt, tm), lambda i, j, k: (k, i)), pl.BlockSpec((tt, tn), lambda i, j, k: (k, j))],
        out_specs=pl.BlockSpec((tm, tn), lambda i, j, k: (i, j)),
        out_shape=jax.ShapeDtypeStruct((M, N), F32),
        compiler_params=_cparams(),
    )(a, b)


def _layer_norm(z, g, b):
    mu = jnp.mean(z, axis=-1, keepdims=True)
    xc = z - mu
    var = jnp.mean(xc * xc, axis=-1, keepdims=True)
    return xc * lax.rsqrt(var + LN_EPS) * g + b


def _mm_ln(a, w, res, g, b, *, name, target=None, n_tok=None):
    M, K = a.shape
    D = w.shape[1]
    tm = _pick(M, (640, 128))
    final = target is not None

    def body(*refs):
        if final:
            a_ref, w_ref, r_ref, g_ref, b_ref, t_ref, z_ref, dy_ref, loss_ref = refs
        else:
            a_ref, w_ref, r_ref, g_ref, b_ref, z_ref, h_ref, hb_ref = refs
        z = ALPHA * r_ref[...] + jnp.dot(a_ref[...], w_ref[...], preferred_element_type=F32)
        z_ref[...] = z
        h = _layer_norm(z, g_ref[...], b_ref[...])
        if not final:
            h_ref[...] = h
            hb_ref[...] = h.astype(BF16)
            return
        i = pl.program_id(0)
        row = i * tm + lax.broadcasted_iota(jnp.int32, (tm, 1), 0)
        valid = (row >= N_META) & (row < n_tok)
        err = jnp.where(valid, h - t_ref[...], 0.0)
        dy_ref[...] = err / D

        @pl.when(i == 0)
        def _():
            loss_ref[...] = jnp.zeros_like(loss_ref)

        loss_ref[...] += 0.5 * jnp.sum(jnp.sum(err * err, axis=1, keepdims=True) / D, axis=0, keepdims=True)

    row_blk = lambda cols: pl.BlockSpec((tm, cols), lambda i: (i, 0))
    vec = pl.BlockSpec((1, D), lambda i: (0, 0))
    in_specs = [row_blk(K), pl.BlockSpec((K, D), lambda i: (0, 0)), row_blk(D), vec, vec]
    args = [a, w, res, g, b]
    if final:
        in_specs.append(row_blk(D))
        args.append(target)
        out_specs = [row_blk(D), row_blk(D), pl.BlockSpec((8, LANES), lambda i: (0, 0))]
        out_shape = [jax.ShapeDtypeStruct((M, D), F32)] * 2 + [jax.ShapeDtypeStruct((8, LANES), F32)]
    else:
        out_specs = [row_blk(D)] * 3
        out_shape = [jax.ShapeDtypeStruct((M, D), F32)] * 2 + [jax.ShapeDtypeStruct((M, D), BF16)]
    return pl.pallas_call(
        body, name=name, grid=(M // tm,), in_specs=in_specs, out_specs=out_specs, out_shape=out_shape,
        compiler_params=_cparams(),
    )(*args)


def _ln_bwd(dh, z, g, *, name, dz_next=None):
    M, D = z.shape
    tm = _pick(M, (640, 128))
    has_next = dz_next is not None

    def body(*refs):
        if has_next:
            dh_ref, nx_ref, z_ref, g_ref, dz_ref, dzb_ref, dg_ref, db_ref = refs
            dh_v = dh_ref[...] + ALPHA * nx_ref[...]
        else:
            dh_ref, z_ref, g_ref, dz_ref, dzb_ref, dg_ref, db_ref = refs
            dh_v = dh_ref[...]
        z_v = z_ref[...]
        mu = jnp.mean(z_v, axis=-1, keepdims=True)
        xc = z_v - mu
        rstd = lax.rsqrt(jnp.mean(xc * xc, axis=-1, keepdims=True) + LN_EPS)
        xhat = xc * rstd
        dxhat = dh_v * g_ref[...]
        dz = rstd * (dxhat - jnp.mean(dxhat, axis=-1, keepdims=True)
                     - xhat * jnp.mean(dxhat * xhat, axis=-1, keepdims=True))
        dz_ref[...] = dz
        dzb_ref[...] = dz.astype(BF16)

        @pl.when(pl.program_id(0) == 0)
        def _():
            dg_ref[...] = jnp.zeros_like(dg_ref)
            db_ref[...] = jnp.zeros_like(db_ref)

        dg_ref[...] += jnp.sum(dh_v * xhat, axis=0, keepdims=True)
        db_ref[...] += jnp.sum(dh_v, axis=0, keepdims=True)

    row_blk = pl.BlockSpec((tm, D), lambda i: (i, 0))
    vec = pl.BlockSpec((1, D), lambda i: (0, 0))
    args = [dh] + ([dz_next] if has_next else []) + [z, g]
    in_specs = [row_blk] * (len(args) - 1) + [vec]
    return pl.pallas_call(
        body, name=name, grid=(M // tm,), in_specs=in_specs,
        out_specs=[row_blk, row_blk, vec, vec],
        out_shape=[jax.ShapeDtypeStruct((M, D), F32), jax.ShapeDtypeStruct((M, D), BF16),
                   jax.ShapeDtypeStruct((1, D), F32), jax.ShapeDtypeStruct((1, D), F32)],
        compiler_params=_cparams(),
    )(*args)


def _pool_fwd(h0, pw, scale, g, b, *, name):
    M, D = h0.shape
    n_groups, G, _ = pw.shape
    tm = _pick(M, (640, 128))

    def body(x_ref, halo_ref, pw_ref, sc_ref, g_ref, b_ref, z_ref, h_ref, hb_ref, diff_ref, ext_ref, mix_ref):
        i = pl.program_id(0)
        x = x_ref[...]
        ext_ref[0:POOL_HALO, :] = jnp.where(i == 0, 0.0, halo_ref[...])
        ext_ref[POOL_HALO:, :] = x
        tok = i * tm + lax.broadcasted_iota(jnp.int32, (tm, 1), 0)
        for gi, win in enumerate(POOL_WINDOWS):
            cols = slice(gi * G, (gi + 1) * G)
            xs = x[:, cols]
            s = xs
            for k in range(1, win):
                s = s + ext_ref[pl.ds(POOL_HALO - k, tm), cols]
            count = jnp.minimum(tok + 1, win).astype(F32)
            d = (s / count - xs).astype(BF16)
            diff_ref[:, cols] = d
            mix_ref[:, cols] = jnp.dot(d, pw_ref[gi], preferred_element_type=F32)
        z = ALPHA * x + mix_ref[...] * sc_ref[...]
        z_ref[...] = z
        h = _layer_norm(z, g_ref[...], b_ref[...])
        h_ref[...] = h
        hb_ref[...] = h.astype(BF16)

    row_blk = pl.BlockSpec((tm, D), lambda i: (i, 0))
    vec = pl.BlockSpec((1, D), lambda i: (0, 0))
    halo = pl.BlockSpec((POOL_HALO, D), lambda i: (jnp.maximum(i * (tm // POOL_HALO) - 1, 0), 0))
    return pl.pallas_call(
        body, name=name, grid=(M // tm,),
        in_specs=[row_blk, halo, pl.BlockSpec((n_groups, G, G), lambda i: (0, 0, 0)), vec, vec, vec],
        out_specs=[row_blk] * 4,
        out_shape=[jax.ShapeDtypeStruct((M, D), F32)] * 2 + [jax.ShapeDtypeStruct((M, D), BF16)] * 2,
        scratch_shapes=[pltpu.VMEM((tm + POOL_HALO, D), F32), pltpu.VMEM((tm, D), F32)],
        compiler_params=_cparams(),
    )(h0, h0, pw, scale, g, b)


def _pool_bwd(dz, diff, pw, pw_t, scale, *, name):
    M, D = dz.shape
    n_groups, G, _ = pw.shape
    tm = _pick(M, (640, 128))
    nt = M // tm

    def body(dz_ref, halo_ref, diff_ref, pw_ref, pwt_ref, sc_ref, dh_ref, dpw_ref, dsc_ref, ext_ref, q_ref):
        i = pl.program_id(0)
        dz_v = dz_ref[...]
        ext_ref[0:tm, :] = dz_v
        ext_ref[tm:, :] = jnp.where(i == nt - 1, 0.0, halo_ref[...])
        tok = i * tm + lax.broadcasted_iota(jnp.int32, (tm + POOL_HALO, 1), 0)

        @pl.when(i == 0)
        def _():
            dpw_ref[...] = jnp.zeros_like(dpw_ref)
            dsc_ref[...] = jnp.zeros_like(dsc_ref)

        for gi, win in enumerate(POOL_WINDOWS):
            cols = slice(gi * G, (gi + 1) * G)
            dmix = (ext_ref[:, cols] * sc_ref[:, cols]).astype(BF16)
            ddiff = jnp.dot(dmix, pwt_ref[gi], preferred_element_type=F32)
            count = jnp.minimum(tok + 1, win).astype(F32)
            q_ref[:, cols] = ddiff / count
            acc = -ddiff[0:tm]
            for k in range(win):
                acc = acc + q_ref[pl.ds(k, tm), cols]
            dh_ref[:, cols] = ALPHA * dz_v[:, cols] + acc
            d = diff_ref[:, cols]
            dpw_ref[gi] += lax.dot_general(d, dmix[0:tm], (((0,), (0,)), ((), ())), preferred_element_type=F32)
            mixed = jnp.dot(d, pw_ref[gi], preferred_element_type=F32)
            dsc_ref[:, cols] += jnp.sum(dz_v[:, cols] * mixed, axis=0, keepdims=True)

    row_blk = pl.BlockSpec((tm, D), lambda i: (i, 0))
    vec = pl.BlockSpec((1, D), lambda i: (0, 0))
    per_tile = tm // POOL_HALO
    halo = pl.BlockSpec((POOL_HALO, D), lambda i: (jnp.minimum((i + 1) * per_tile, nt * per_tile - 1), 0))
    wblk = pl.BlockSpec((n_groups, G, G), lambda i: (0, 0, 0))
    return pl.pallas_call(
        body, name=name, grid=(nt,),
        in_specs=[row_blk, halo, row_blk, wblk, wblk, vec],
        out_specs=[row_blk, wblk, vec],
        out_shape=[jax.ShapeDtypeStruct((M, D), F32), jax.ShapeDtypeStruct((n_groups, G, G), F32),
                   jax.ShapeDtypeStruct((1, D), F32)],
        scratch_shapes=[pltpu.VMEM((tm + POOL_HALO, D), F32), pltpu.VMEM((tm + POOL_HALO, D), F32)],
        compiler_params=_cparams(),
    )(dz, dz, diff, pw, pw_t, scale)


def _glu_interleave(x):
    s = x.shape
    n = s[-1] // (2 * GLU_CHUNK)
    return jnp.swapaxes(x.reshape(s[:-1] + (2, n, GLU_CHUNK)), -3, -2).reshape(s)


def _glu_deinterleave(x):
    s = x.shape
    n = s[-1] // (2 * GLU_CHUNK)
    return jnp.swapaxes(x.reshape(s[:-1] + (n, 2, GLU_CHUNK)), -3, -2).reshape(s)


def _conv_taps(ext_ref, u, tm):
    return ext_ref[pl.ds(CONV_HALO - 2, tm), :], ext_ref[pl.ds(CONV_HALO - 1, tm), :], u


def _conv_glu_fwd(u, cw, cb, *, name):
    M, F2 = u.shape
    tm = _pick(M, (640, 128))
    tc = 2 * GLU_CHUNK

    def body(u_ref, halo_ref, w_ref, b_ref, o_ref, ext_ref):
        i = pl.program_id(0)
        u_v = u_ref[...]
        ext_ref[0:CONV_HALO, :] = jnp.where(i == 0, 0.0, halo_ref[...])
        ext_ref[CONV_HALO:, :] = u_v
        u2, u1, u0 = _conv_taps(ext_ref, u_v, tm)
        c = b_ref[...] + w_ref[0:1, :] * u2 + w_ref[1:2, :] * u1 + w_ref[2:3, :] * u0
        a, g = c[:, :GLU_CHUNK], c[:, GLU_CHUNK:]
        o_ref[...] = (a * jax.nn.sigmoid(a) * g).astype(BF16)

    per_tile = tm // CONV_HALO
    return pl.pallas_call(
        body, name=name, grid=(M // tm, F2 // tc),
        in_specs=[pl.BlockSpec((tm, tc), lambda i, j: (i, j)),
                  pl.BlockSpec((CONV_HALO, tc), lambda i, j: (jnp.maximum(i * per_tile - 1, 0), j)),
                  pl.BlockSpec((3, tc), lambda i, j: (0, j)), pl.BlockSpec((1, tc), lambda i, j: (0, j))],
        out_specs=pl.BlockSpec((tm, GLU_CHUNK), lambda i, j: (i, j)),
        out_shape=jax.ShapeDtypeStruct((M, F2 // 2), BF16),
        scratch_shapes=[pltpu.VMEM((tm + CONV_HALO, tc), F32)],
        compiler_params=_cparams(),
    )(u, u, cw, cb)


def _conv_glu_bwd(u, dact, cw, cb, *, name):
    M, F2 = u.shape
    tm = _pick(M, (640, 128))
    nt = M // tm
    tc = 2 * GLU_CHUNK

    def body(u_ref, halo_ref, da_ref, w_ref, b_ref, du_ref, dw_ref, db_ref, ext_ref, dcx_ref, carry_ref):
        i = pl.program_id(1)
        u_v = u_ref[...]
        ext_ref[0:CONV_HALO, :] = jnp.where(i == nt - 1, 0.0, halo_ref[...])
        ext_ref[CONV_HALO:, :] = u_v
        u2, u1, u0 = _conv_taps(ext_ref, u_v, tm)
        c = b_ref[...] + w_ref[0:1, :] * u2 + w_ref[1:2, :] * u1 + w_ref[2:3, :] * u0
        a, g = c[:, :GLU_CHUNK], c[:, GLU_CHUNK:]
        sig = jax.nn.sigmoid(a)
        dact_v = da_ref[...]
        d_a = dact_v * g * (sig * (1.0 + a * (1.0 - sig)))
        d_g = dact_v * (a * sig)
        dc = jnp.concatenate([d_a, d_g], axis=1)

        @pl.when(i == 0)
        def _():
            dw_ref[...] = jnp.zeros_like(dw_ref)
            db_ref[...] = jnp.zeros_like(db_ref)
            carry_ref[...] = jnp.zeros_like(carry_ref)

        db_ref[...] += jnp.sum(dc, axis=0, keepdims=True)
        dw_ref[0:1, :] += jnp.sum(dc * u2, axis=0, keepdims=True)
        dw_ref[1:2, :] += jnp.sum(dc * u1, axis=0, keepdims=True)
        dw_ref[2:3, :] += jnp.sum(dc * u0, axis=0, keepdims=True)
        dcx_ref[0:tm, :] = dc
        dcx_ref[tm:, :] = carry_ref[...]
        du = w_ref[2:3, :] * dc + w_ref[1:2, :] * dcx_ref[pl.ds(1, tm), :] + w_ref[0:1, :] * dcx_ref[pl.ds(2, tm), :]
        du_ref[...] = du.astype(BF16)
        carry_ref[...] = dcx_ref[0:CONV_HALO, :]

    per_tile = tm // CONV_HALO
    rev = lambda i: nt - 1 - i
    return pl.pallas_call(
        body, name=name, grid=(F2 // tc, nt),
        in_specs=[pl.BlockSpec((tm, tc), lambda j, i: (rev(i), j)),
                  pl.BlockSpec((CONV_HALO, tc), lambda j, i: (jnp.maximum(rev(i) * per_tile - 1, 0), j)),
                  pl.BlockSpec((tm, GLU_CHUNK), lambda j, i: (rev(i), j)),
                  pl.BlockSpec((3, tc), lambda j, i: (0, j)), pl.BlockSpec((1, tc), lambda j, i: (0, j))],
        out_specs=[pl.BlockSpec((tm, tc), lambda j, i: (rev(i), j)),
                   pl.BlockSpec((3, tc), lambda j, i: (0, j)), pl.BlockSpec((1, tc), lambda j, i: (0, j))],
        out_shape=[jax.ShapeDtypeStruct((M, F2), BF16), jax.ShapeDtypeStruct((3, F2), F32),
                   jax.ShapeDtypeStruct((1, F2), F32)],
        scratch_shapes=[pltpu.VMEM((tm + CONV_HALO, tc), F32), pltpu.VMEM((tm + CONV_HALO, tc), F32),
                        pltpu.VMEM((CONV_HALO, tc), F32)],
        compiler_params=_cparams(),
    )(u, u, dact, cw, cb)


def _split3(x):
    hi = x.astype(BF16)
    r = x - hi.astype(F32)
    mid = r.astype(BF16)
    lo = (r - mid.astype(F32)).astype(BF16)
    return hi, mid, lo


def _tri_sum(tri, x):
    return sum(jnp.dot(tri, part, preferred_element_type=F32) for part in _split3(x))


def _log_sigmoid(x):
    return jnp.minimum(x, 0.0) - jnp.log1p(jnp.exp(-jnp.abs(x)))


def _forget_cumsum(pre, bias, *, name):
    M, C = pre.shape
    tm = _pick(M, (640, 128))

    def body(p_ref, b_ref, c_ref, carry_ref):
        i = pl.program_id(0)

        @pl.when(i == 0)
        def _():
            carry_ref[...] = jnp.zeros_like(carry_ref)

        logf = _log_sigmoid(p_ref[...] + b_ref[...])
        r = lax.broadcasted_iota(jnp.int32, (tm, tm), 0)
        s = lax.broadcasted_iota(jnp.int32, (tm, tm), 1)
        c_ref[...] = _tri_sum((s <= r).astype(BF16), logf) + carry_ref[...]
        carry_ref[...] = c_ref[pl.ds(tm - 1, 1), :]

    return pl.pallas_call(
        body, name=name, grid=(M // tm,),
        in_specs=[pl.BlockSpec((tm, C), lambda i: (i, 0)), pl.BlockSpec((1, C), lambda i: (0, 0))],
        out_specs=pl.BlockSpec((tm, C), lambda i: (i, 0)),
        out_shape=jax.ShapeDtypeStruct((M, C), F32),
        scratch_shapes=[pltpu.VMEM((1, C), F32)],
        compiler_params=_cparams(),
    )(pre, bias)


def _forget_cumsum_bwd(dc, pre, bias, *, name):
    M, C = pre.shape
    tm = _pick(M, (640, 128))
    nt = M // tm

    def body(dc_ref, p_ref, b_ref, dp_ref, db_ref, carry_ref, run_ref):
        i = pl.program_id(0)

        @pl.when(i == 0)
        def _():
            carry_ref[...] = jnp.zeros_like(carry_ref)
            db_ref[...] = jnp.zeros_like(db_ref)

        r = lax.broadcasted_iota(jnp.int32, (tm, tm), 0)
        s = lax.broadcasted_iota(jnp.int32, (tm, tm), 1)
        run_ref[...] = _tri_sum((s >= r).astype(BF16), dc_ref[...]) + carry_ref[...]
        carry_ref[...] = run_ref[pl.ds(0, 1), :]
        dpre = run_ref[...] * jax.nn.sigmoid(-(p_ref[...] + b_ref[...]))
        dp_ref[...] = dpre.astype(BF16)
        db_ref[...] += jnp.sum(dpre, axis=0, keepdims=True)

    rev_blk = pl.BlockSpec((tm, C), lambda i: (nt - 1 - i, 0))
    vec = pl.BlockSpec((1, C), lambda i: (0, 0))
    return pl.pallas_call(
        body, name=name, grid=(nt,),
        in_specs=[rev_blk, rev_blk, vec], out_specs=[rev_blk, vec],
        out_shape=[jax.ShapeDtypeStruct((M, C), BF16), jax.ShapeDtypeStruct((1, C), F32)],
        scratch_shapes=[pltpu.VMEM((1, C), F32), pltpu.VMEM((tm, C), F32)],
        compiler_params=_cparams(),
    )(dc, pre, bias)


def _causal_mask(tm):
    key = lax.broadcasted_iota(jnp.int32, (tm, tm), 0)
    query = lax.broadcasted_iota(jnp.int32, (tm, tm), 1)
    return key <= query


def _dot_nt(a, b):
    return lax.dot_general(a, b, (((1,), (1,)), ((), ())), preferred_element_type=F32)


def _attn_specs(H, M, dh, tm):
    nt = M // tm
    qT_blk = pl.BlockSpec((1, dh, tm), lambda h, i: (h, 0, i))
    row_blk = pl.BlockSpec((1, 1, tm), lambda h, i: (h, 0, i))
    kv_blk = pl.BlockSpec((1, M, dh), lambda h, i: (h, 0, 0))
    kvT_blk = pl.BlockSpec((1, nt, dh, tm), lambda h, i: (h, 0, 0, 0))
    ck_blk = pl.BlockSpec((1, M, 1), lambda h, i: (h, 0, 0))
    return qT_blk, row_blk, kv_blk, kvT_blk, ck_blk


def _attn_fwd(qT, k, vT, cq, ck, *, name):
    H, M, dh = k.shape
    tm = vT.shape[-1]
    qT_blk, row_blk, kv_blk, kvT_blk, ck_blk = _attn_specs(H, M, dh, tm)

    def body(qT_ref, k_ref, vT_ref, cq_ref, ck_ref, oT_ref, lse_ref):
        i = pl.program_id(1)
        qsT = qT_ref[0] * jnp.asarray(dh ** -0.5, BF16)
        cq_v = cq_ref[0]

        def block(j, carry, masked):
            m, l, acc = carry
            keys = pl.ds(pl.multiple_of(j * tm, tm), tm)
            sT = jnp.dot(k_ref[0, keys, :], qsT, preferred_element_type=F32) + (cq_v - ck_ref[0, keys, :])
            if masked:
                sT = jnp.where(_causal_mask(tm), sT, NEG_INF)
            m_new = jnp.maximum(m, jnp.max(sT, axis=0, keepdims=True))
            a = jnp.exp(m - m_new)
            pT = jnp.exp(sT - m_new)
            l = a * l + jnp.sum(pT, axis=0, keepdims=True)
            acc = a * acc + jnp.dot(vT_ref[0, j], pT.astype(BF16), preferred_element_type=F32)
            return m_new, l, acc

        init = (jnp.full((1, tm), NEG_INF, F32), jnp.zeros((1, tm), F32), jnp.zeros((dh, tm), F32))
        carry = lax.fori_loop(0, i, lambda j, c: block(j, c, False), init)
        m, l, acc = block(i, carry, True)
        oT_ref[0] = (acc / l).astype(BF16)
        lse_ref[0] = m + jnp.log(l)

    return pl.pallas_call(
        body, name=name, grid=(H, M // tm),
        in_specs=[qT_blk, kv_blk, kvT_blk, row_blk, ck_blk], out_specs=[qT_blk, row_blk],
        out_shape=[jax.ShapeDtypeStruct((H, dh, M), BF16), jax.ShapeDtypeStruct((H, 1, M), F32)],
        compiler_params=_cparams(),
    )(qT, k, vT, cq, ck)


def _attn_bwd(qT, k, v, kT, oT, doT, lse, cq, ck, *, name):
    H, M, dh = k.shape
    tm = kT.shape[-1]
    qT_blk, row_blk, kv_blk, kvT_blk, ck_blk = _attn_specs(H, M, dh, tm)

    def body(qT_ref, oT_ref, doT_ref, lse_ref, cq_ref, k_ref, v_ref, kT_ref, ck_ref,
             dqT_ref, dkT_ref, dvT_ref, dcq_ref, dck_ref):
        i = pl.program_id(1)

        @pl.when(i == 0)
        def _():
            dkT_ref[...] = jnp.zeros_like(dkT_ref)
            dvT_ref[...] = jnp.zeros_like(dvT_ref)
            dck_ref[...] = jnp.zeros_like(dck_ref)

        qsT = qT_ref[0] * jnp.asarray(dh ** -0.5, BF16)
        doT = doT_ref[0]
        delta = jnp.sum(doT.astype(F32) * oT_ref[0].astype(F32), axis=0, keepdims=True)
        shift = cq_ref[0] - lse_ref[0]

        def block(j, carry, masked):
            dq, dcq = carry
            keys = pl.ds(pl.multiple_of(j * tm, tm), tm)
            sT = jnp.dot(k_ref[0, keys, :], qsT, preferred_element_type=F32) + (shift - ck_ref[0, keys, :])
            pT = jnp.exp(sT)
            if masked:
                pT = jnp.where(_causal_mask(tm), pT, 0.0)
            dpT = jnp.dot(v_ref[0, keys, :], doT, preferred_element_type=F32)
            dsT = pT * (dpT - delta)
            ds_b = dsT.astype(BF16)
            dvT_ref[0, j] += _dot_nt(doT, pT.astype(BF16))
            dkT_ref[0, j] += _dot_nt(qsT, ds_b)
            dck_ref[0, keys, :] += -jnp.sum(dsT, axis=1, keepdims=True)
            return (dq + jnp.dot(kT_ref[0, j], ds_b, preferred_element_type=F32),
                    dcq + jnp.sum(dsT, axis=0, keepdims=True))

        init = (jnp.zeros((dh, tm), F32), jnp.zeros((1, tm), F32))
        dq, dcq = block(i, lax.fori_loop(0, i, lambda j, c: block(j, c, False), init), True)
        dqT_ref[0] = (dq * dh ** -0.5).astype(BF16)
        dcq_ref[0] = dcq

    blocked = jax.ShapeDtypeStruct(kT.shape, F32)
    return pl.pallas_call(
        body, name=name, grid=(H, M // tm),
        in_specs=[qT_blk, qT_blk, qT_blk, row_blk, row_blk, kv_blk, kv_blk, kvT_blk, ck_blk],
        out_specs=[qT_blk, kvT_blk, kvT_blk, row_blk, ck_blk],
        out_shape=[jax.ShapeDtypeStruct((H, dh, M), BF16), blocked, blocked,
                   jax.ShapeDtypeStruct((H, 1, M), F32), jax.ShapeDtypeStruct((H, M, 1), F32)],
        compiler_params=_cparams(),
    )(qT, oT, doT, lse, cq, k, v, kT, ck)


def kernel(x, meta, pool_w, pool_scale, w_kv, w_f, b_f, w_q, w_o, ffn_w_in, ffn_conv_w, ffn_conv_b, ffn_w_out, ln_g, ln_b, loss_target, m_meta, m_pool_w, m_pool_scale, m_w_kv, m_w_f, m_b_f, m_w_q, m_w_o, m_ffn_w_in, m_ffn_conv_w, m_ffn_conv_b, m_ffn_w_out, m_ln_g, m_ln_b, v_meta, v_pool_w, v_pool_scale, v_w_kv, v_w_f, v_b_f, v_w_q, v_w_o, v_ffn_w_in, v_ffn_conv_w, v_ffn_conv_b, v_ffn_w_out, v_ln_g, v_ln_b):
    local = dict(meta=meta, pool_w=pool_w, pool_scale=pool_scale, w_kv=w_kv, w_f=w_f, b_f=b_f, w_q=w_q, w_o=w_o,
                 ffn_w_in=ffn_w_in, ffn_conv_w=ffn_conv_w, ffn_conv_b=ffn_conv_b, ffn_w_out=ffn_w_out,
                 ln_g=ln_g, ln_b=ln_b)
    mom1 = dict(meta=m_meta, pool_w=m_pool_w, pool_scale=m_pool_scale, w_kv=m_w_kv, w_f=m_w_f, b_f=m_b_f,
                w_q=m_w_q, w_o=m_w_o, ffn_w_in=m_ffn_w_in, ffn_conv_w=m_ffn_conv_w, ffn_conv_b=m_ffn_conv_b,
                ffn_w_out=m_ffn_w_out, ln_g=m_ln_g, ln_b=m_ln_b)
    mom2 = dict(meta=v_meta, pool_w=v_pool_w, pool_scale=v_pool_scale, w_kv=v_w_kv, w_f=v_w_f, b_f=v_b_f,
                w_q=v_w_q, w_o=v_w_o, ffn_w_in=v_ffn_w_in, ffn_conv_w=v_ffn_conv_w, ffn_conv_b=v_ffn_conv_b,
                ffn_w_out=v_ffn_w_out, ln_g=v_ln_g, ln_b=v_ln_b)
    axis_of = dict(PARAMS)

    S, D = x.shape[1], x.shape[2]
    H = b_f.shape[0]
    dh = D // H
    n_tok = N_META + S
    M = _round_up(n_tok, LANES)
    tm = _pick(M, (640, 128))
    nt = M // tm
    F2 = ffn_conv_b.shape[1]
    F = F2 // 2
    depth = ffn_conv_b.shape[0]

    small_sharded = [n for n in SMALL if axis_of[n] is not None]
    got = _all_gather([local[n].astype(BF16) for n in BIG] + [_pack([local[n] for n in small_sharded])],
                      "gather_weights")
    wb = {n: _from_blocks(blk, axis_of[n]) for n, blk in zip(BIG, got)}
    small_blocks = _unpack(got[-1], [local[n].shape for n in small_sharded], lead=(N_DEV,))
    wf32 = {n: _from_blocks(blk, axis_of[n]) for n, blk in zip(small_sharded, small_blocks)}

    pw = wb["pool_w"][0]
    pw_t = jnp.swapaxes(pw, 1, 2)
    wf_pad = jnp.pad(wf32["w_f"].astype(BF16), ((0, 0), (0, LANES - H)))
    w_qkv = jnp.concatenate([wb["w_kv"], wb["w_q"][0]], axis=1)
    w_att_t = jnp.concatenate([w_qkv, wf_pad], axis=1).T
    wo = wb["w_o"][0]
    wo_t = wo.T
    w_in = _glu_interleave(wb["ffn_w_in"])
    w_in_t = jnp.swapaxes(w_in, 1, 2)
    w_out = wb["ffn_w_out"]
    w_out_t = jnp.swapaxes(w_out, 1, 2)
    conv_w = _glu_interleave(wf32["ffn_conv_w"])
    conv_b = _glu_interleave(ffn_conv_b)[:, None, :]
    scale = wf32["pool_scale"]
    g_ln, b_ln = wf32["ln_g"], wf32["ln_b"]
    ln = lambda i, j: (g_ln[i, j][None, :], b_ln[i, j][None, :])
    bias_f = jnp.pad(b_f, (0, LANES - H))[None, :]

    pad_rows = M - n_tok
    h0 = jnp.concatenate([wf32["meta"], x[0], jnp.zeros((pad_rows, D), F32)], axis=0)
    target = jnp.concatenate([jnp.zeros((N_META, D), F32), loss_target[0], jnp.zeros((pad_rows, D), F32)], axis=0)

    z1, h1, h1b, diff = _pool_fwd(h0, pw, scale, *ln(0, 0), name="pool_fwd")
    u0 = _mm(h1b, w_in[0], name="ffn0_up", out_dtype=F32)
    act0 = _conv_glu_fwd(u0, conv_w[0], conv_b[0], name="ffn0_conv")
    z2, h2, h2b = _mm_ln(act0, w_out[0], h1, *ln(0, 1), name="ffn0_down_ln")

    qkv = _mm(h2b, w_qkv, name="attn_qkv", out_dtype=BF16)
    pre = _mm(h2b, wf_pad, name="attn_gate", out_dtype=F32)
    c = _forget_cumsum(pre, bias_f, name="attn_cumsum")
    qkv4 = qkv.reshape(M, 3, H, dh)
    rows_first = lambda t: jnp.transpose(t, (1, 0, 2))
    cols_first = lambda t: jnp.transpose(t, (1, 2, 0))
    key_blocks = lambda t: jnp.transpose(t.reshape(nt, tm, H, dh), (2, 0, 3, 1))
    k_h, v_h = rows_first(qkv4[:, 0]), rows_first(qkv4[:, 1])
    kT_h, vT_h, qT_h = key_blocks(qkv4[:, 0]), key_blocks(qkv4[:, 1]), cols_first(qkv4[:, 2])
    c_t = c[:, :H].T
    cq = c_t[:, None, :]
    ck = c_t[:, :, None]
    oT_h, lse = _attn_fwd(qT_h, k_h, vT_h, cq, ck, name="attn_fwd")
    o = jnp.transpose(oT_h, (2, 0, 1)).reshape(M, D)
    z3, h3, h3b = _mm_ln(o, wo, h2, *ln(1, 0), name="attn_out_ln")

    u1 = _mm(h3b, w_in[1], name="ffn1_up", out_dtype=F32)
    act1 = _conv_glu_fwd(u1, conv_w[1], conv_b[1], name="ffn1_conv")
    z4, dy, loss_part = _mm_ln(act1, w_out[1], h3, *ln(1, 1), name="ffn1_down_loss", target=target, n_tok=n_tok)
    loss = lax.psum(loss_part[0, 0], ("x", "y", "c"))

    grads = {}

    def ffn_bwd(layer, dz_b, u, act, h_in_b, tag):
        dact = _mm(dz_b, w_out_t[layer], name=tag + "_dact", out_dtype=F32)
        du, dcw, dcb = _conv_glu_bwd(u, dact, conv_w[layer], conv_b[layer], name=tag + "_conv_bwd")
        dh = _mm(du, w_in_t[layer], name=tag + "_dh", out_dtype=F32)
        d_w_out = _mm_tn(act, dz_b, name=tag + "_dw_out")
        d_w_in = _glu_deinterleave(_mm_tn(h_in_b, du, name=tag + "_dw_in"))
        return dh, d_w_in, d_w_out, _glu_deinterleave(dcw), _glu_deinterleave(dcb)[0]

    dz4, dz4b, dg11, db11 = _ln_bwd(dy, z4, g_ln[1, 1][None, :], name="ln4_bwd")
    dh3, dwin1, dwout1, dcw1, dcb1 = ffn_bwd(1, dz4b, u1, act1, h3b, "ffn1")
    dz3, dz3b, dg10, db10 = _ln_bwd(dh3, z3, g_ln[1, 0][None, :], name="ln3_bwd", dz_next=dz4)

    do = _mm(dz3b, wo_t, name="attn_do", out_dtype=BF16)
    grads["w_o"] = _mm_tn(o, dz3b, name="attn_dw_o")[None]
    doT_h = cols_first(do.reshape(M, H, dh))
    dqT_h, dkT_h, dvT_h, dcq, dck = _attn_bwd(qT_h, k_h, v_h, kT_h, oT_h, doT_h, lse, cq, ck, name="attn_bwd")
    dc = jnp.pad((dcq[:, 0, :] + dck[:, :, 0]).T, ((0, 0), (0, LANES - H)))
    dpre, dbias = _forget_cumsum_bwd(dc, pre, bias_f, name="attn_cumsum_bwd")
    from_blocks = lambda t: jnp.transpose(t, (1, 3, 0, 2)).reshape(M, D).astype(BF16)
    d_att = jnp.concatenate([from_blocks(dkT_h), from_blocks(dvT_h),
                             jnp.transpose(dqT_h, (2, 0, 1)).reshape(M, D), dpre], axis=1)
    dh2 = _mm(d_att, w_att_t, name="attn_dh", out_dtype=F32)
    d_w_att = _mm_tn(h2b, d_att, name="attn_dw_qkv")
    grads["w_kv"] = d_w_att[:, :2 * D]
    grads["w_q"] = d_w_att[:, 2 * D:3 * D][None]
    grads["w_f"] = d_w_att[:, 3 * D:3 * D + H]
    grads["b_f"] = dbias[0, :H]
    dz2, dz2b, dg01, db01 = _ln_bwd(dh2, z2, g_ln[0, 1][None, :], name="ln2_bwd", dz_next=dz3)

    dh1, dwin0, dwout0, dcw0, dcb0 = ffn_bwd(0, dz2b, u0, act0, h1b, "ffn0")
    dz1, _, dg00, db00 = _ln_bwd(dh1, z1, g_ln[0, 0][None, :], name="ln1_bwd", dz_next=dz2)
    dh0, dpw, dscale = _pool_bwd(dz1, diff, pw, pw_t, scale, name="pool_bwd")

    grads["meta"] = dh0[:N_META]
    grads["pool_w"] = dpw[None]
    grads["pool_scale"] = dscale
    grads["ffn_w_in"] = jnp.stack([dwin0, dwin1])
    grads["ffn_w_out"] = jnp.stack([dwout0, dwout1])
    grads["ffn_conv_w"] = jnp.stack([dcw0, dcw1])
    grads["ffn_conv_b"] = jnp.stack([dcb0, dcb1])
    grads["ln_g"] = jnp.stack([jnp.stack([dg00[0], dg01[0]]), jnp.stack([dg10[0], dg11[0]])])
    grads["ln_b"] = jnp.stack([jnp.stack([db00[0], db01[0]]), jnp.stack([db10[0], db11[0]])])
    grad_x = dh0[N_META:n_tok][None]

    blocks = {n: _to_blocks(grads[n], axis_of[n]) for n, _ in PARAMS}
    recv = _exchange([blocks[n] for n in BIG] + [_pack([blocks[n] for n in SMALL], lead=(N_DEV,))], "grad_exchange")
    results = {}
    for n, r in zip(BIG, recv):
        shape = local[n].shape
        as2d = lambda t: t.reshape(-1, shape[-1])
        outs = _adamw(r.reshape(N_DEV, -1, shape[-1]), as2d(local[n]), as2d(mom1[n]), as2d(mom2[n]), name="adamw_" + n)
        results[n] = [o_.reshape(shape) for o_ in outs]
    outs = _adamw(recv[-1], *[_pack([d[n] for n in SMALL]) for d in (local, mom1, mom2)], name="adamw_small")
    small_out = [_unpack(o_, [local[n].shape for n in SMALL]) for o_ in outs]
    for i, n in enumerate(SMALL):
        results[n] = [small_out[k][i] for k in range(4)]
    return (loss, grad_x, *[results[n][k] for k in range(4) for n, _ in PARAMS])
```

```python
import jax
import jax.numpy as jnp
from jax import lax
from jax.experimental import pallas as pl
from jax.experimental.pallas import tpu as pltpu

F32, BF16 = jnp.float32, jnp.bfloat16
MESH = pl.DeviceIdType.MESH

N_DEV = 8
N_CHIPS = 4
N_META = 16
POOL_WINDOWS = (2, 4, 8, 16)
POOL_HALO = 16
CONV_HALO = 8
CONV_STRIP = 16
ALPHA = 4.0 ** 0.25
LN_EPS = 1e-5
NEG_INF = -1e30
ADAM_LR, ADAM_B1, ADAM_B2, ADAM_EPS, ADAM_WD, ADAM_STEP = 0.001, 0.9, 0.999, 1e-08, 0.01, 10

LANES = 128
PACK_COLS = 1024
ADAM_TILE_BYTES = 4 << 20
GLU_CHUNK = 256
VMEM_LIMIT = 56 * 1024 * 1024

PARAMS = (("meta", 1), ("pool_w", 2), ("pool_scale", 1), ("w_kv", 1), ("w_f", 0), ("b_f", None),
          ("w_q", 1), ("w_o", 1), ("ffn_w_in", 2), ("ffn_conv_w", 2), ("ffn_conv_b", None),
          ("ffn_w_out", 1), ("ln_g", 2), ("ln_b", 2))
BIG = ("pool_w", "w_kv", "w_q", "w_o", "ffn_w_in", "ffn_w_out")
SMALL = ("meta", "pool_scale", "w_f", "b_f", "ffn_conv_w", "ffn_conv_b", "ln_g", "ln_b")


def _cparams(**kw):
    return pltpu.CompilerParams(vmem_limit_bytes=VMEM_LIMIT, **kw)


def _pick(n, cands):
    for c in cands:
        if n % c == 0:
            return c
    return n


def _round_up(n, m):
    return (n + m - 1) // m * m


def _pack(pieces, lead=()):
    flat = []
    for p in pieces:
        v = p.reshape(lead + (-1,))
        flat.append(jnp.pad(v, [(0, 0)] * len(lead) + [(0, _round_up(v.shape[-1], PACK_COLS) - v.shape[-1])]))
    v = jnp.concatenate(flat, axis=-1)
    rows = _round_up(v.shape[-1] // PACK_COLS, 8)
    v = jnp.pad(v, [(0, 0)] * len(lead) + [(0, rows * PACK_COLS - v.shape[-1])])
    return v.reshape(lead + (rows, PACK_COLS))


def _unpack(buf, shapes, lead=()):
    flat = buf.reshape(lead + (-1,))
    out, off = [], 0
    for s in shapes:
        n = 1
        for d in s:
            n *= d
        out.append(flat[..., off:off + n].reshape(lead + tuple(s)))
        off += _round_up(n, PACK_COLS)
    return out


def _to_blocks(full, axis):
    if axis is None:
        return jnp.broadcast_to(full[None], (N_DEV,) + full.shape)
    s = full.shape
    x = full.reshape(s[:axis] + (N_DEV, s[axis] // N_DEV) + s[axis + 1:])
    return jnp.moveaxis(x, axis, 0)


def _from_blocks(blocks, axis):
    x = jnp.moveaxis(blocks, 0, axis)
    s = x.shape
    return x.reshape(s[:axis] + (s[axis] * s[axis + 1],) + s[axis + 2:])


def _coords():
    return lax.axis_index("x"), lax.axis_index("y"), lax.axis_index("c")


def _flip(pos, mask):
    x, y, c = pos
    return (1 - x if mask & 4 else x, 1 - y if mask & 2 else y, 1 - c if mask & 1 else c)


def _index(pos):
    x, y, c = pos
    return 4 * x + 2 * y + c


def _comm_call(body, name, arrays, out_shapes):
    n = len(arrays)
    hbm = pl.BlockSpec(memory_space=pl.ANY)
    return pl.pallas_call(
        body, name=name, out_shape=out_shapes, in_specs=[hbm] * n, out_specs=[hbm] * n,
        scratch_shapes=[pltpu.SemaphoreType.DMA((7 * n,)), pltpu.SemaphoreType.DMA((7 * n,)),
                        pltpu.SemaphoreType.DMA((n,))],
    )(*arrays)


def _all_gather(blocks, name):
    chip_masks = (4, 2, 6)
    n = len(blocks)

    def body(*refs):
        x_refs, out_refs = refs[:n], refs[n:2 * n]
        send_sems, recv_sems, local_sems = refs[2 * n:]
        me = _coords()
        sibling = _flip(me, 1)

        def copy(a, k, owner, to, from_input=False):
            slot = out_refs[a].at[_index(owner)]
            return pltpu.make_async_remote_copy(
                src_ref=x_refs[a] if from_input else slot, dst_ref=slot,
                send_sem=send_sems.at[7 * a + k], recv_sem=recv_sems.at[7 * a + k],
                device_id=to, device_id_type=MESH)

        mine = [pltpu.make_async_copy(x_refs[a], out_refs[a].at[_index(me)], local_sems.at[a]) for a in range(n)]
        first = [copy(a, 0, me, sibling, True) for a in range(n)]
        first += [copy(a, 1 + j, me, _flip(me, m), True) for j, m in enumerate(chip_masks) for a in range(n)]
        for cp in mine + first:
            cp.start()
        passed = []
        for j, m in enumerate(chip_masks):
            for a in range(n):
                copy(a, 1 + j, _flip(me, m), me).wait_recv()
                passed.append(copy(a, 4 + j, _flip(me, m), sibling))
                passed[-1].start()
        for a in range(n):
            copy(a, 0, sibling, me).wait_recv()
            for j, m in enumerate(chip_masks):
                copy(a, 4 + j, _flip(sibling, m), me).wait_recv()
        for cp in first + passed:
            cp.wait_send()
        for cp in mine:
            cp.wait()

    return _comm_call(body, name, blocks, [jax.ShapeDtypeStruct((N_DEV,) + b.shape, b.dtype) for b in blocks])


def _exchange(segs, name, masks, slot_of):
    n = len(segs)

    def body(*refs):
        seg_refs, out_refs = refs[:n], refs[n:2 * n]
        send_sems, recv_sems, local_sems = refs[2 * n:]
        me = _coords()

        def copy(a, k, mask):
            peer = _flip(me, mask)
            return pltpu.make_async_remote_copy(
                src_ref=seg_refs[a].at[slot_of(peer)], dst_ref=out_refs[a].at[slot_of(me)],
                send_sem=send_sems.at[7 * a + k], recv_sem=recv_sems.at[7 * a + k],
                device_id=peer, device_id_type=MESH)

        def arrival(a, k, mask):
            peer = _flip(me, mask)
            return pltpu.make_async_remote_copy(
                src_ref=seg_refs[a].at[slot_of(me)], dst_ref=out_refs[a].at[slot_of(peer)],
                send_sem=send_sems.at[7 * a + k], recv_sem=recv_sems.at[7 * a + k],
                device_id=peer, device_id_type=MESH)

        mine = [pltpu.make_async_copy(seg_refs[a].at[slot_of(me)], out_refs[a].at[slot_of(me)], local_sems.at[a])
                for a in range(n)]
        sends = [copy(a, k, mask) for k, mask in enumerate(masks) for a in range(n)]
        for cp in mine + sends:
            cp.start()
        for k, mask in enumerate(masks):
            for a in range(n):
                arrival(a, k, mask).wait_recv()
        for cp in sends:
            cp.wait_send()
        for cp in mine:
            cp.wait()

    return _comm_call(body, name, segs, [jax.ShapeDtypeStruct(s.shape, s.dtype) for s in segs])


def _pair_sum(pair, name):
    _, rows, cols = pair.shape
    tr = max(t for t in range(16, rows + 1, 16) if rows % t == 0 and 2 * t * cols * 2 <= ADAM_TILE_BYTES)

    def body(p_ref, o_ref):
        o_ref[...] = (p_ref[0].astype(F32) + p_ref[1].astype(F32)).astype(BF16)

    return pl.pallas_call(
        body, name=name, grid=(rows // tr,),
        in_specs=[pl.BlockSpec((2, tr, cols), lambda i: (0, i, 0))],
        out_specs=pl.BlockSpec((tr, cols), lambda i: (i, 0)),
        out_shape=jax.ShapeDtypeStruct((rows, cols), BF16),
        compiler_params=_cparams(),
    )(pair)


def _adamw(recv, w, m, v, name):
    rows, cols = w.shape
    n_slots = recv.shape[0]
    sublanes = 8 * (4 // recv.dtype.itemsize)
    tr = max(t for t in range(sublanes, rows + 1, sublanes)
             if rows % t == 0 and N_DEV * t * cols * 4 <= ADAM_TILE_BYTES)
    c1 = 1.0 - ADAM_B1 ** ADAM_STEP
    c2 = 1.0 - ADAM_B2 ** ADAM_STEP

    def body(r_ref, w_ref, m_ref, v_ref, g_out, d_out, m_out, v_out):
        g = r_ref[0].astype(F32)
        for s in range(1, n_slots):
            g = g + r_ref[s].astype(F32)
        m_new = ADAM_B1 * m_ref[...] + (1.0 - ADAM_B1) * g
        v_new = ADAM_B2 * v_ref[...] + (1.0 - ADAM_B2) * (g * g)
        m_hat = m_new / c1
        v_hat = v_new / c2
        g_out[...] = g
        d_out[...] = -ADAM_LR * (m_hat / (jnp.sqrt(v_hat) + ADAM_EPS) + ADAM_WD * w_ref[...])
        m_out[...] = m_new
        v_out[...] = v_new

    tile = pl.BlockSpec((tr, cols), lambda i: (i, 0))
    return pl.pallas_call(
        body, name=name, grid=(rows // tr,),
        in_specs=[pl.BlockSpec((n_slots, tr, cols), lambda i: (0, i, 0)), tile, tile, tile],
        out_specs=[tile] * 4,
        out_shape=[jax.ShapeDtypeStruct(w.shape, F32)] * 4,
        compiler_params=_cparams(),
    )(recv, w, m, v)


def _mm(a, b, *, name, out_dtype, res=None, res_scale=1.0):
    M, K = a.shape
    N = b.shape[1]
    tm = _pick(M, (640, 128))
    tn = _pick(N, (1408, 1024, 640, 512, 256, 128))
    tk = K if K <= 1024 else _pick(K, (1024, 640, 512, 256, 128))
    nk = K // tk

    def body(*refs):
        a_ref, b_ref = refs[0], refs[1]
        r_ref = refs[2] if res is not None else None
        o_ref = refs[3] if res is not None else refs[2]
        acc_ref = refs[-1] if nk > 1 else None

        def finish(acc):
            if r_ref is not None:
                acc = acc + res_scale * r_ref[...]
            o_ref[...] = acc.astype(out_dtype)

        prod = jnp.dot(a_ref[...], b_ref[...], preferred_element_type=F32)
        if nk == 1:
            finish(prod)
        else:
            k = pl.program_id(2)

            @pl.when(k == 0)
            def _():
                acc_ref[...] = prod

            @pl.when(k > 0)
            def _():
                acc_ref[...] += prod

            @pl.when(k == nk - 1)
            def _():
                finish(acc_ref[...])

    in_specs = [pl.BlockSpec((tm, tk), lambda i, j, k: (i, k)), pl.BlockSpec((tk, tn), lambda i, j, k: (k, j))]
    args = [a, b]
    if res is not None:
        in_specs.append(pl.BlockSpec((tm, tn), lambda i, j, k: (i, j)))
        args.append(res)
    return pl.pallas_call(
        body, name=name, grid=(M // tm, N // tn, nk),
        in_specs=in_specs, out_specs=pl.BlockSpec((tm, tn), lambda i, j, k: (i, j)),
        out_shape=jax.ShapeDtypeStruct((M, N), out_dtype),
        scratch_shapes=[pltpu.VMEM((tm, tn), F32)] if nk > 1 else [],
        compiler_params=_cparams(),
    )(*args)


def _mm_tn(a, b, *, name):
    T, M = a.shape
    N = b.shape[1]
    tt = _pick(T, (1664, 640, 128))
    tm = _pick(M, (1408, 1024, 512, 256, 128))
    tn = _pick(N, (1024, 640, 512, 256, 128))

    def body(a_ref, b_ref, o_ref):
        prod = lax.dot_general(a_ref[...], b_ref[...], (((0,), (0,)), ((), ())), preferred_element_type=F32)

        @pl.when(pl.program_id(2) == 0)
        def _():
            o_ref[...] = prod

        @pl.when(pl.program_id(2) > 0)
        def _():
            o_ref[...] += prod

    return pl.pallas_call(
        body, name=name, grid=(M // tm, N // tn, T // tt),
        in_specs=[pl.BlockSpec((tt, tm), lambda i, j, k: (k, i)), pl.BlockSpec((tt, tn), lambda i, j, k: (k, j))],
        out_specs=pl.BlockSpec((tm, tn), lambda i, j, k: (i, j)),
        out_shape=jax.ShapeDtypeStruct((M, N), F32),
        compiler_params=_cparams(),
    )(a, b)


def _layer_norm(z, g, b):
    mu = jnp.mean(z, axis=-1, keepdims=True)
    xc = z - mu
    var = jnp.mean(xc * xc, axis=-1, keepdims=True)
    return xc * lax.rsqrt(var + LN_EPS) * g + b


def _mm_ln(a, w, res, g, b, *, name, target=None, n_tok=None):
    M, K = a.shape
    D = w.shape[1]
    tm = _pick(M, (640, 128))
    final = target is not None

    def body(*refs):
        if final:
            a_ref, w_ref, r_ref, g_ref, b_ref, t_ref, z_ref, dy_ref, loss_ref = refs
        else:
            a_ref, w_ref, r_ref, g_ref, b_ref, z_ref, h_ref, hb_ref = refs
        z = ALPHA * r_ref[...] + jnp.dot(a_ref[...], w_ref[...], preferred_element_type=F32)
        z_ref[...] = z
        h = _layer_norm(z, g_ref[...], b_ref[...])
        if not final:
            h_ref[...] = h
            hb_ref[...] = h.astype(BF16)
            return
        i = pl.program_id(0)
        row = i * tm + lax.broadcasted_iota(jnp.int32, (tm, 1), 0)
        valid = (row >= N_META) & (row < n_tok)
        err = jnp.where(valid, h - t_ref[...], 0.0)
        dy_ref[...] = err / D

        @pl.when(i == 0)
        def _():
            loss_ref[...] = jnp.zeros_like(loss_ref)

        loss_ref[...] += 0.5 * jnp.sum(jnp.sum(err * err, axis=1, keepdims=True) / D, axis=0, keepdims=True)

    row_blk = lambda cols: pl.BlockSpec((tm, cols), lambda i: (i, 0))
    vec = pl.BlockSpec((1, D), lambda i: (0, 0))
    in_specs = [row_blk(K), pl.BlockSpec((K, D), lambda i: (0, 0)), row_blk(D), vec, vec]
    args = [a, w, res, g, b]
    if final:
        in_specs.append(row_blk(D))
        args.append(target)
        out_specs = [row_blk(D), row_blk(D), pl.BlockSpec((8, LANES), lambda i: (0, 0))]
        out_shape = [jax.ShapeDtypeStruct((M, D), F32)] * 2 + [jax.ShapeDtypeStruct((8, LANES), F32)]
    else:
        out_specs = [row_blk(D)] * 3
        out_shape = [jax.ShapeDtypeStruct((M, D), F32)] * 2 + [jax.ShapeDtypeStruct((M, D), BF16)]
    return pl.pallas_call(
        body, name=name, grid=(M // tm,), in_specs=in_specs, out_specs=out_specs, out_shape=out_shape,
        compiler_params=_cparams(),
    )(*args)


def _ln_bwd(dh, z, g, *, name, dz_next=None):
    M, D = z.shape
    tm = _pick(M, (640, 128))
    has_next = dz_next is not None

    def body(*refs):
        if has_next:
            dh_ref, nx_ref, z_ref, g_ref, dz_ref, dzb_ref, dg_ref, db_ref = refs
            dh_v = dh_ref[...] + ALPHA * nx_ref[...]
        else:
            dh_ref, z_ref, g_ref, dz_ref, dzb_ref, dg_ref, db_ref = refs
            dh_v = dh_ref[...]
        z_v = z_ref[...]
        mu = jnp.mean(z_v, axis=-1, keepdims=True)
        xc = z_v - mu
        rstd = lax.rsqrt(jnp.mean(xc * xc, axis=-1, keepdims=True) + LN_EPS)
        xhat = xc * rstd
        dxhat = dh_v * g_ref[...]
        dz = rstd * (dxhat - jnp.mean(dxhat, axis=-1, keepdims=True)
                     - xhat * jnp.mean(dxhat * xhat, axis=-1, keepdims=True))
        dz_ref[...] = dz
        dzb_ref[...] = dz.astype(BF16)

        @pl.when(pl.program_id(0) == 0)
        def _():
            dg_ref[...] = jnp.zeros_like(dg_ref)
            db_ref[...] = jnp.zeros_like(db_ref)

        dg_ref[...] += jnp.sum(dh_v * xhat, axis=0, keepdims=True)
        db_ref[...] += jnp.sum(dh_v, axis=0, keepdims=True)

    row_blk = pl.BlockSpec((tm, D), lambda i: (i, 0))
    vec = pl.BlockSpec((1, D), lambda i: (0, 0))
    args = [dh] + ([dz_next] if has_next else []) + [z, g]
    in_specs = [row_blk] * (len(args) - 1) + [vec]
    return pl.pallas_call(
        body, name=name, grid=(M // tm,), in_specs=in_specs,
        out_specs=[row_blk, row_blk, vec, vec],
        out_shape=[jax.ShapeDtypeStruct((M, D), F32), jax.ShapeDtypeStruct((M, D), BF16),
                   jax.ShapeDtypeStruct((1, D), F32), jax.ShapeDtypeStruct((1, D), F32)],
        compiler_params=_cparams(),
    )(*args)


def _pool_fwd(h0, pw, scale, g, b, *, name):
    M, D = h0.shape
    n_groups, G, _ = pw.shape
    tm = _pick(M, (640, 128))

    def body(x_ref, halo_ref, pw_ref, sc_ref, g_ref, b_ref, z_ref, h_ref, hb_ref, diff_ref, ext_ref, mix_ref):
        i = pl.program_id(0)
        x = x_ref[...]
        ext_ref[0:POOL_HALO, :] = jnp.where(i == 0, 0.0, halo_ref[...])
        ext_ref[POOL_HALO:, :] = x
        tok = i * tm + lax.broadcasted_iota(jnp.int32, (tm, 1), 0)
        for gi, win in enumerate(POOL_WINDOWS):
            cols = slice(gi * G, (gi + 1) * G)
            xs = x[:, cols]
            s = xs
            for k in range(1, win):
                s = s + ext_ref[pl.ds(POOL_HALO - k, tm), cols]
            count = jnp.minimum(tok + 1, win).astype(F32)
            d = (s / count - xs).astype(BF16)
            diff_ref[:, cols] = d
            mix_ref[:, cols] = jnp.dot(d, pw_ref[gi], preferred_element_type=F32)
        z = ALPHA * x + mix_ref[...] * sc_ref[...]
        z_ref[...] = z
        h = _layer_norm(z, g_ref[...], b_ref[...])
        h_ref[...] = h
        hb_ref[...] = h.astype(BF16)

    row_blk = pl.BlockSpec((tm, D), lambda i: (i, 0))
    vec = pl.BlockSpec((1, D), lambda i: (0, 0))
    halo = pl.BlockSpec((POOL_HALO, D), lambda i: (jnp.maximum(i * (tm // POOL_HALO) - 1, 0), 0))
    return pl.pallas_call(
        body, name=name, grid=(M // tm,),
        in_specs=[row_blk, halo, pl.BlockSpec((n_groups, G, G), lambda i: (0, 0, 0)), vec, vec, vec],
        out_specs=[row_blk] * 4,
        out_shape=[jax.ShapeDtypeStruct((M, D), F32)] * 2 + [jax.ShapeDtypeStruct((M, D), BF16)] * 2,
        scratch_shapes=[pltpu.VMEM((tm + POOL_HALO, D), F32), pltpu.VMEM((tm, D), F32)],
        compiler_params=_cparams(),
    )(h0, h0, pw, scale, g, b)


def _pool_bwd(dz, diff, pw, pw_t, scale, *, name):
    M, D = dz.shape
    n_groups, G, _ = pw.shape
    tm = _pick(M, (640, 128))
    nt = M // tm

    def body(dz_ref, halo_ref, diff_ref, pw_ref, pwt_ref, sc_ref, dh_ref, dpw_ref, dsc_ref, ext_ref, q_ref):
        i = pl.program_id(0)
        dz_v = dz_ref[...]
        ext_ref[0:tm, :] = dz_v
        ext_ref[tm:, :] = jnp.where(i == nt - 1, 0.0, halo_ref[...])
        tok = i * tm + lax.broadcasted_iota(jnp.int32, (tm + POOL_HALO, 1), 0)

        @pl.when(i == 0)
        def _():
            dpw_ref[...] = jnp.zeros_like(dpw_ref)
            dsc_ref[...] = jnp.zeros_like(dsc_ref)

        for gi, win in enumerate(POOL_WINDOWS):
            cols = slice(gi * G, (gi + 1) * G)
            dmix = (ext_ref[:, cols] * sc_ref[:, cols]).astype(BF16)
            ddiff = jnp.dot(dmix, pwt_ref[gi], preferred_element_type=F32)
            count = jnp.minimum(tok + 1, win).astype(F32)
            q_ref[:, cols] = ddiff / count
            acc = -ddiff[0:tm]
            for k in range(win):
                acc = acc + q_ref[pl.ds(k, tm), cols]
            dh_ref[:, cols] = ALPHA * dz_v[:, cols] + acc
            d = diff_ref[:, cols]
            dpw_ref[gi] += lax.dot_general(d, dmix[0:tm], (((0,), (0,)), ((), ())), preferred_element_type=F32)
            mixed = jnp.dot(d, pw_ref[gi], preferred_element_type=F32)
            dsc_ref[:, cols] += jnp.sum(dz_v[:, cols] * mixed, axis=0, keepdims=True)

    row_blk = pl.BlockSpec((tm, D), lambda i: (i, 0))
    vec = pl.BlockSpec((1, D), lambda i: (0, 0))
    per_tile = tm // POOL_HALO
    halo = pl.BlockSpec((POOL_HALO, D), lambda i: (jnp.minimum((i + 1) * per_tile, nt * per_tile - 1), 0))
    wblk = pl.BlockSpec((n_groups, G, G), lambda i: (0, 0, 0))
    return pl.pallas_call(
        body, name=name, grid=(nt,),
        in_specs=[row_blk, halo, row_blk, wblk, wblk, vec],
        out_specs=[row_blk, wblk, vec],
        out_shape=[jax.ShapeDtypeStruct((M, D), F32), jax.ShapeDtypeStruct((n_groups, G, G), F32),
                   jax.ShapeDtypeStruct((1, D), F32)],
        scratch_shapes=[pltpu.VMEM((tm + POOL_HALO, D), F32), pltpu.VMEM((tm + POOL_HALO, D), F32)],
        compiler_params=_cparams(),
    )(dz, dz, diff, pw, pw_t, scale)


def _glu_interleave(x):
    s = x.shape
    n = s[-1] // (2 * GLU_CHUNK)
    return jnp.swapaxes(x.reshape(s[:-1] + (2, n, GLU_CHUNK)), -3, -2).reshape(s)


def _glu_deinterleave(x):
    s = x.shape
    n = s[-1] // (2 * GLU_CHUNK)
    return jnp.swapaxes(x.reshape(s[:-1] + (n, 2, GLU_CHUNK)), -3, -2).reshape(s)


def _taps(u_ref, head_ref, r0):
    src, base = (head_ref, CONV_HALO) if r0 == 0 else (u_ref, r0)
    return tuple(src[pl.ds(base - k, CONV_STRIP), :] for k in range(3))


def _conv_glu_fwd(u, cw, cb, *, name):
    M, F2 = u.shape
    tm = _pick(M, (640, 128))
    tc = 2 * GLU_CHUNK

    def body(u_ref, halo_ref, w_ref, b_ref, o_ref, head_ref):
        i = pl.program_id(0)
        head_ref[0:CONV_HALO, :] = jnp.where(i == 0, 0.0, halo_ref[...])
        head_ref[CONV_HALO:, :] = u_ref[0:CONV_STRIP, :]
        w0, w1, w2, b = w_ref[0:1, :], w_ref[1:2, :], w_ref[2:3, :], b_ref[...]
        for r0 in range(0, tm, CONV_STRIP):
            u0, u1, u2 = _taps(u_ref, head_ref, r0)
            c = b + w0 * u2 + w1 * u1 + w2 * u0
            a, g = c[:, :GLU_CHUNK], c[:, GLU_CHUNK:]
            o_ref[pl.ds(r0, CONV_STRIP), :] = (a * jax.nn.sigmoid(a) * g).astype(BF16)

    per_tile = tm // CONV_HALO
    return pl.pallas_call(
        body, name=name, grid=(M // tm, F2 // tc),
        in_specs=[pl.BlockSpec((tm, tc), lambda i, j: (i, j)),
                  pl.BlockSpec((CONV_HALO, tc), lambda i, j: (jnp.maximum(i * per_tile - 1, 0), j)),
                  pl.BlockSpec((3, tc), lambda i, j: (0, j)), pl.BlockSpec((1, tc), lambda i, j: (0, j))],
        out_specs=pl.BlockSpec((tm, GLU_CHUNK), lambda i, j: (i, j)),
        out_shape=jax.ShapeDtypeStruct((M, F2 // 2), BF16),
        scratch_shapes=[pltpu.VMEM((CONV_HALO + CONV_STRIP, tc), F32)],
        compiler_params=_cparams(),
    )(u, u, cw, cb)


def _conv_glu_bwd(u, dact, cw, cb, *, name):
    M, F2 = u.shape
    tm = _pick(M, (640, 128))
    nt = M // tm
    tc = 2 * GLU_CHUNK

    def body(u_ref, halo_ref, da_ref, w_ref, b_ref, du_ref, dw_ref, db_ref, head_ref, dcx_ref, carry_ref):
        i = pl.program_id(1)
        head_ref[0:CONV_HALO, :] = jnp.where(i == nt - 1, 0.0, halo_ref[...])
        head_ref[CONV_HALO:, :] = u_ref[0:CONV_STRIP, :]
        w0, w1, w2, b = w_ref[0:1, :], w_ref[1:2, :], w_ref[2:3, :], b_ref[...]

        @pl.when(i == 0)
        def _():
            dw_ref[...] = jnp.zeros_like(dw_ref)
            db_ref[...] = jnp.zeros_like(db_ref)
            carry_ref[...] = jnp.zeros_like(carry_ref)

        def fold(t):
            return sum(t[r:r + 8] for r in range(0, CONV_STRIP, 8))

        dcx_ref[tm:, :] = carry_ref[...]
        s_b = s_0 = s_1 = s_2 = jnp.zeros((8, tc), F32)
        for r0 in reversed(range(0, tm, CONV_STRIP)):
            rows = pl.ds(r0, CONV_STRIP)
            u0, u1, u2 = _taps(u_ref, head_ref, r0)
            c = b + w0 * u2 + w1 * u1 + w2 * u0
            a, g = c[:, :GLU_CHUNK], c[:, GLU_CHUNK:]
            sig = jax.nn.sigmoid(a)
            dact_v = da_ref[rows, :]
            d_a = dact_v * g * (sig * (1.0 + a * (1.0 - sig)))
            d_g = dact_v * (a * sig)
            dc = jnp.concatenate([d_a, d_g], axis=1)
            dcx_ref[rows, :] = dc
            du = w2 * dc + w1 * dcx_ref[pl.ds(r0 + 1, CONV_STRIP), :] + w0 * dcx_ref[pl.ds(r0 + 2, CONV_STRIP), :]
            du_ref[rows, :] = du.astype(BF16)
            s_b, s_0, s_1, s_2 = s_b + fold(dc), s_0 + fold(dc * u2), s_1 + fold(dc * u1), s_2 + fold(dc * u0)
        carry_ref[...] = dcx_ref[0:CONV_HALO, :]
        db_ref[...] += jnp.sum(s_b, axis=0, keepdims=True)
        dw_ref[0:1, :] += jnp.sum(s_0, axis=0, keepdims=True)
        dw_ref[1:2, :] += jnp.sum(s_1, axis=0, keepdims=True)
        dw_ref[2:3, :] += jnp.sum(s_2, axis=0, keepdims=True)

    per_tile = tm // CONV_HALO
    rev = lambda i: nt - 1 - i
    return pl.pallas_call(
        body, name=name, grid=(F2 // tc, nt),
        in_specs=[pl.BlockSpec((tm, tc), lambda j, i: (rev(i), j)),
                  pl.BlockSpec((CONV_HALO, tc), lambda j, i: (jnp.maximum(rev(i) * per_tile - 1, 0), j)),
                  pl.BlockSpec((tm, GLU_CHUNK), lambda j, i: (rev(i), j)),
                  pl.BlockSpec((3, tc), lambda j, i: (0, j)), pl.BlockSpec((1, tc), lambda j, i: (0, j))],
        out_specs=[pl.BlockSpec((tm, tc), lambda j, i: (rev(i), j)),
                   pl.BlockSpec((3, tc), lambda j, i: (0, j)), pl.BlockSpec((1, tc), lambda j, i: (0, j))],
        out_shape=[jax.ShapeDtypeStruct((M, F2), BF16), jax.ShapeDtypeStruct((3, F2), F32),
                   jax.ShapeDtypeStruct((1, F2), F32)],
        scratch_shapes=[pltpu.VMEM((CONV_HALO + CONV_STRIP, tc), F32), pltpu.VMEM((tm + CONV_HALO, tc), F32),
                        pltpu.VMEM((CONV_HALO, tc), F32)],
        compiler_params=_cparams(),
    )(u, u, dact, cw, cb)


def _split3(x):
    hi = x.astype(BF16)
    r = x - hi.astype(F32)
    mid = r.astype(BF16)
    lo = (r - mid.astype(F32)).astype(BF16)
    return hi, mid, lo


def _tri_sum(tri, x):
    return sum(jnp.dot(tri, part, preferred_element_type=F32) for part in _split3(x))


def _log_sigmoid(x):
    return jnp.minimum(x, 0.0) - jnp.log1p(jnp.exp(-jnp.abs(x)))


def _forget_cumsum(pre, bias, *, name):
    M, C = pre.shape
    tm = _pick(M, (640, 128))

    def body(p_ref, b_ref, c_ref, carry_ref):
        i = pl.program_id(0)

        @pl.when(i == 0)
        def _():
            carry_ref[...] = jnp.zeros_like(carry_ref)

        logf = _log_sigmoid(p_ref[...] + b_ref[...])
        r = lax.broadcasted_iota(jnp.int32, (tm, tm), 0)
        s = lax.broadcasted_iota(jnp.int32, (tm, tm), 1)
        c_ref[...] = _tri_sum((s <= r).astype(BF16), logf) + carry_ref[...]
        carry_ref[...] = c_ref[pl.ds(tm - 1, 1), :]

    return pl.pallas_call(
        body, name=name, grid=(M // tm,),
        in_specs=[pl.BlockSpec((tm, C), lambda i: (i, 0)), pl.BlockSpec((1, C), lambda i: (0, 0))],
        out_specs=pl.BlockSpec((tm, C), lambda i: (i, 0)),
        out_shape=jax.ShapeDtypeStruct((M, C), F32),
        scratch_shapes=[pltpu.VMEM((1, C), F32)],
        compiler_params=_cparams(),
    )(pre, bias)


def _forget_cumsum_bwd(dc, pre, bias, *, name):
    M, C = pre.shape
    tm = _pick(M, (640, 128))
    nt = M // tm

    def body(dc_ref, p_ref, b_ref, dp_ref, db_ref, carry_ref, run_ref):
        i = pl.program_id(0)

        @pl.when(i == 0)
        def _():
            carry_ref[...] = jnp.zeros_like(carry_ref)
            db_ref[...] = jnp.zeros_like(db_ref)

        r = lax.broadcasted_iota(jnp.int32, (tm, tm), 0)
        s = lax.broadcasted_iota(jnp.int32, (tm, tm), 1)
        run_ref[...] = _tri_sum((s >= r).astype(BF16), dc_ref[...]) + carry_ref[...]
        carry_ref[...] = run_ref[pl.ds(0, 1), :]
        dpre = run_ref[...] * jax.nn.sigmoid(-(p_ref[...] + b_ref[...]))
        dp_ref[...] = dpre.astype(BF16)
        db_ref[...] += jnp.sum(dpre, axis=0, keepdims=True)

    rev_blk = pl.BlockSpec((tm, C), lambda i: (nt - 1 - i, 0))
    vec = pl.BlockSpec((1, C), lambda i: (0, 0))
    return pl.pallas_call(
        body, name=name, grid=(nt,),
        in_specs=[rev_blk, rev_blk, vec], out_specs=[rev_blk, vec],
        out_shape=[jax.ShapeDtypeStruct((M, C), BF16), jax.ShapeDtypeStruct((1, C), F32)],
        scratch_shapes=[pltpu.VMEM((1, C), F32), pltpu.VMEM((tm, C), F32)],
        compiler_params=_cparams(),
    )(dc, pre, bias)


def _causal_mask(tm):
    key = lax.broadcasted_iota(jnp.int32, (tm, tm), 0)
    query = lax.broadcasted_iota(jnp.int32, (tm, tm), 1)
    return key <= query


def _dot_nt(a, b):
    return lax.dot_general(a, b, (((1,), (1,)), ((), ())), preferred_element_type=F32)


def _attn_specs(H, M, dh, tm):
    nt = M // tm
    qT_blk = pl.BlockSpec((1, dh, tm), lambda h, i: (h, 0, i))
    row_blk = pl.BlockSpec((1, 1, tm), lambda h, i: (h, 0, i))
    kv_blk = pl.BlockSpec((1, M, dh), lambda h, i: (h, 0, 0))
    kvT_blk = pl.BlockSpec((1, nt, dh, tm), lambda h, i: (h, 0, 0, 0))
    ck_blk = pl.BlockSpec((1, M, 1), lambda h, i: (h, 0, 0))
    return qT_blk, row_blk, kv_blk, kvT_blk, ck_blk


def _attn_fwd(qT, k, vT, cq, ck, *, name):
    H, M, dh = k.shape
    tm = vT.shape[-1]
    qT_blk, row_blk, kv_blk, kvT_blk, ck_blk = _attn_specs(H, M, dh, tm)

    def body(qT_ref, k_ref, vT_ref, cq_ref, ck_ref, oT_ref, lse_ref):
        i = pl.program_id(1)
        qsT = qT_ref[0] * jnp.asarray(dh ** -0.5, BF16)
        cq_v = cq_ref[0]

        def block(j, carry, masked):
            m, l, acc = carry
            keys = pl.ds(pl.multiple_of(j * tm, tm), tm)
            sT = jnp.dot(k_ref[0, keys, :], qsT, preferred_element_type=F32) + (cq_v - ck_ref[0, keys, :])
            if masked:
                sT = jnp.where(_causal_mask(tm), sT, NEG_INF)
            m_new = jnp.maximum(m, jnp.max(sT, axis=0, keepdims=True))
            a = jnp.exp(m - m_new)
            pT = jnp.exp(sT - m_new)
            l = a * l + jnp.sum(pT, axis=0, keepdims=True)
            acc = a * acc + jnp.dot(vT_ref[0, j], pT.astype(BF16), preferred_element_type=F32)
            return m_new, l, acc

        init = (jnp.full((1, tm), NEG_INF, F32), jnp.zeros((1, tm), F32), jnp.zeros((dh, tm), F32))
        carry = lax.fori_loop(0, i, lambda j, c: block(j, c, False), init)
        m, l, acc = block(i, carry, True)
        oT_ref[0] = (acc / l).astype(BF16)
        lse_ref[0] = m + jnp.log(l)

    return pl.pallas_call(
        body, name=name, grid=(H, M // tm),
        in_specs=[qT_blk, kv_blk, kvT_blk, row_blk, ck_blk], out_specs=[qT_blk, row_blk],
        out_shape=[jax.ShapeDtypeStruct((H, dh, M), BF16), jax.ShapeDtypeStruct((H, 1, M), F32)],
        compiler_params=_cparams(),
    )(qT, k, vT, cq, ck)


def _attn_bwd(qT, k, v, kT, oT, doT, lse, cq, ck, *, name):
    H, M, dh = k.shape
    tm = kT.shape[-1]
    qT_blk, row_blk, kv_blk, kvT_blk, ck_blk = _attn_specs(H, M, dh, tm)

    def body(qT_ref, oT_ref, doT_ref, lse_ref, cq_ref, k_ref, v_ref, kT_ref, ck_ref,
             dqT_ref, dkT_ref, dvT_ref, dcq_ref, dck_ref):
        i = pl.program_id(1)

        @pl.when(i == 0)
        def _():
            dkT_ref[...] = jnp.zeros_like(dkT_ref)
            dvT_ref[...] = jnp.zeros_like(dvT_ref)
            dck_ref[...] = jnp.zeros_like(dck_ref)

        qsT = qT_ref[0] * jnp.asarray(dh ** -0.5, BF16)
        doT = doT_ref[0]
        delta = jnp.sum(doT.astype(F32) * oT_ref[0].astype(F32), axis=0, keepdims=True)
        shift = cq_ref[0] - lse_ref[0]

        def block(j, carry, masked):
            dq, dcq = carry
            keys = pl.ds(pl.multiple_of(j * tm, tm), tm)
            sT = jnp.dot(k_ref[0, keys, :], qsT, preferred_element_type=F32) + (shift - ck_ref[0, keys, :])
            pT = jnp.exp(sT)
            if masked:
                pT = jnp.where(_causal_mask(tm), pT, 0.0)
            dpT = jnp.dot(v_ref[0, keys, :], doT, preferred_element_type=F32)
            dsT = pT * (dpT - delta)
            ds_b = dsT.astype(BF16)
            dvT_ref[0, j] += _dot_nt(doT, pT.astype(BF16))
            dkT_ref[0, j] += _dot_nt(qsT, ds_b)
            dck_ref[0, keys, :] += -jnp.sum(dsT, axis=1, keepdims=True)
            return (dq + jnp.dot(kT_ref[0, j], ds_b, preferred_element_type=F32),
                    dcq + jnp.sum(dsT, axis=0, keepdims=True))

        init = (jnp.zeros((dh, tm), F32), jnp.zeros((1, tm), F32))
        dq, dcq = block(i, lax.fori_loop(0, i, lambda j, c: block(j, c, False), init), True)
        dqT_ref[0] = (dq * dh ** -0.5).astype(BF16)
        dcq_ref[0] = dcq

    blocked = jax.ShapeDtypeStruct(kT.shape, F32)
    return pl.pallas_call(
        body, name=name, grid=(H, M // tm),
        in_specs=[qT_blk, qT_blk, qT_blk, row_blk, row_blk, kv_blk, kv_blk, kvT_blk, ck_blk],
        out_specs=[qT_blk, kvT_blk, kvT_blk, row_blk, ck_blk],
        out_shape=[jax.ShapeDtypeStruct((H, dh, M), BF16), blocked, blocked,
                   jax.ShapeDtypeStruct((H, 1, M), F32), jax.ShapeDtypeStruct((H, M, 1), F32)],
        compiler_params=_cparams(),
    )(qT, oT, doT, lse, cq, k, v, kT, ck)


def kernel(x, meta, pool_w, pool_scale, w_kv, w_f, b_f, w_q, w_o, ffn_w_in, ffn_conv_w, ffn_conv_b, ffn_w_out, ln_g, ln_b, loss_target, m_meta, m_pool_w, m_pool_scale, m_w_kv, m_w_f, m_b_f, m_w_q, m_w_o, m_ffn_w_in, m_ffn_conv_w, m_ffn_conv_b, m_ffn_w_out, m_ln_g, m_ln_b, v_meta, v_pool_w, v_pool_scale, v_w_kv, v_w_f, v_b_f, v_w_q, v_w_o, v_ffn_w_in, v_ffn_conv_w, v_ffn_conv_b, v_ffn_w_out, v_ln_g, v_ln_b):
    local = dict(meta=meta, pool_w=pool_w, pool_scale=pool_scale, w_kv=w_kv, w_f=w_f, b_f=b_f, w_q=w_q, w_o=w_o,
                 ffn_w_in=ffn_w_in, ffn_conv_w=ffn_conv_w, ffn_conv_b=ffn_conv_b, ffn_w_out=ffn_w_out,
                 ln_g=ln_g, ln_b=ln_b)
    mom1 = dict(meta=m_meta, pool_w=m_pool_w, pool_scale=m_pool_scale, w_kv=m_w_kv, w_f=m_w_f, b_f=m_b_f,
                w_q=m_w_q, w_o=m_w_o, ffn_w_in=m_ffn_w_in, ffn_conv_w=m_ffn_conv_w, ffn_conv_b=m_ffn_conv_b,
                ffn_w_out=m_ffn_w_out, ln_g=m_ln_g, ln_b=m_ln_b)
    mom2 = dict(meta=v_meta, pool_w=v_pool_w, pool_scale=v_pool_scale, w_kv=v_w_kv, w_f=v_w_f, b_f=v_b_f,
                w_q=v_w_q, w_o=v_w_o, ffn_w_in=v_ffn_w_in, ffn_conv_w=v_ffn_conv_w, ffn_conv_b=v_ffn_conv_b,
                ffn_w_out=v_ffn_w_out, ln_g=v_ln_g, ln_b=v_ln_b)
    axis_of = dict(PARAMS)

    S, D = x.shape[1], x.shape[2]
    H = b_f.shape[0]
    dh = D // H
    n_tok = N_META + S
    M = _round_up(n_tok, LANES)
    tm = _pick(M, (640, 128))
    nt = M // tm
    F2 = ffn_conv_b.shape[1]
    F = F2 // 2
    depth = ffn_conv_b.shape[0]

    small_sharded = [n for n in SMALL if axis_of[n] is not None]
    got = _all_gather([local[n].astype(BF16) for n in BIG] + [_pack([local[n] for n in small_sharded])],
                      "gather_weights")
    wb = {n: _from_blocks(blk, axis_of[n]) for n, blk in zip(BIG, got)}
    small_blocks = _unpack(got[-1], [local[n].shape for n in small_sharded], lead=(N_DEV,))
    wf32 = {n: _from_blocks(blk, axis_of[n]) for n, blk in zip(small_sharded, small_blocks)}

    pw = wb["pool_w"][0]
    pw_t = jnp.swapaxes(pw, 1, 2)
    wf_pad = jnp.pad(wf32["w_f"].astype(BF16), ((0, 0), (0, LANES - H)))
    w_qkv = jnp.concatenate([wb["w_kv"], wb["w_q"][0]], axis=1)
    w_att_t = jnp.concatenate([w_qkv, wf_pad], axis=1).T
    wo = wb["w_o"][0]
    wo_t = wo.T
    w_in = _glu_interleave(wb["ffn_w_in"])
    w_in_t = jnp.swapaxes(w_in, 1, 2)
    w_out = wb["ffn_w_out"]
    w_out_t = jnp.swapaxes(w_out, 1, 2)
    conv_w = _glu_interleave(wf32["ffn_conv_w"])
    conv_b = _glu_interleave(ffn_conv_b)[:, None, :]
    scale = wf32["pool_scale"]
    g_ln, b_ln = wf32["ln_g"], wf32["ln_b"]
    ln = lambda i, j: (g_ln[i, j][None, :], b_ln[i, j][None, :])
    bias_f = jnp.pad(b_f, (0, LANES - H))[None, :]

    pad_rows = M - n_tok
    h0 = jnp.concatenate([wf32["meta"], x[0], jnp.zeros((pad_rows, D), F32)], axis=0)
    target = jnp.concatenate([jnp.zeros((N_META, D), F32), loss_target[0], jnp.zeros((pad_rows, D), F32)], axis=0)

    z1, h1, h1b, diff = _pool_fwd(h0, pw, scale, *ln(0, 0), name="pool_fwd")
    u0 = _mm(h1b, w_in[0], name="ffn0_up", out_dtype=F32)
    act0 = _conv_glu_fwd(u0, conv_w[0], conv_b[0], name="ffn0_conv")
    z2, h2, h2b = _mm_ln(act0, w_out[0], h1, *ln(0, 1), name="ffn0_down_ln")

    qkv = _mm(h2b, w_qkv, name="attn_qkv", out_dtype=BF16)
    pre = _mm(h2b, wf_pad, name="attn_gate", out_dtype=F32)
    c = _forget_cumsum(pre, bias_f, name="attn_cumsum")
    qkv4 = qkv.reshape(M, 3, H, dh)
    rows_first = lambda t: jnp.transpose(t, (1, 0, 2))
    cols_first = lambda t: jnp.transpose(t, (1, 2, 0))
    key_blocks = lambda t: jnp.transpose(t.reshape(nt, tm, H, dh), (2, 0, 3, 1))
    k_h, v_h = rows_first(qkv4[:, 0]), rows_first(qkv4[:, 1])
    kT_h, vT_h, qT_h = key_blocks(qkv4[:, 0]), key_blocks(qkv4[:, 1]), cols_first(qkv4[:, 2])
    c_t = c[:, :H].T
    cq = c_t[:, None, :]
    ck = c_t[:, :, None]
    oT_h, lse = _attn_fwd(qT_h, k_h, vT_h, cq, ck, name="attn_fwd")
    o = jnp.transpose(oT_h, (2, 0, 1)).reshape(M, D)
    z3, h3, h3b = _mm_ln(o, wo, h2, *ln(1, 0), name="attn_out_ln")

    u1 = _mm(h3b, w_in[1], name="ffn1_up", out_dtype=F32)
    act1 = _conv_glu_fwd(u1, conv_w[1], conv_b[1], name="ffn1_conv")
    z4, dy, loss_part = _mm_ln(act1, w_out[1], h3, *ln(1, 1), name="ffn1_down_loss", target=target, n_tok=n_tok)
    loss = lax.psum(loss_part[0, 0], ("x", "y", "c"))

    grads = {}

    def ffn_bwd(layer, dz_b, u, act, h_in_b, tag):
        dact = _mm(dz_b, w_out_t[layer], name=tag + "_dact", out_dtype=F32)
        du, dcw, dcb = _conv_glu_bwd(u, dact, conv_w[layer], conv_b[layer], name=tag + "_conv_bwd")
        dh = _mm(du, w_in_t[layer], name=tag + "_dh", out_dtype=F32)
        d_w_out = _mm_tn(act, dz_b, name=tag + "_dw_out")
        d_w_in = _glu_deinterleave(_mm_tn(h_in_b, du, name=tag + "_dw_in"))
        return dh, d_w_in, d_w_out, _glu_deinterleave(dcw), _glu_deinterleave(dcb)[0]

    dz4, dz4b, dg11, db11 = _ln_bwd(dy, z4, g_ln[1, 1][None, :], name="ln4_bwd")
    dh3, dwin1, dwout1, dcw1, dcb1 = ffn_bwd(1, dz4b, u1, act1, h3b, "ffn1")
    dz3, dz3b, dg10, db10 = _ln_bwd(dh3, z3, g_ln[1, 0][None, :], name="ln3_bwd", dz_next=dz4)

    do = _mm(dz3b, wo_t, name="attn_do", out_dtype=BF16)
    grads["w_o"] = _mm_tn(o, dz3b, name="attn_dw_o")[None]
    doT_h = cols_first(do.reshape(M, H, dh))
    dqT_h, dkT_h, dvT_h, dcq, dck = _attn_bwd(qT_h, k_h, v_h, kT_h, oT_h, doT_h, lse, cq, ck, name="attn_bwd")
    dc = jnp.pad((dcq[:, 0, :] + dck[:, :, 0]).T, ((0, 0), (0, LANES - H)))
    dpre, dbias = _forget_cumsum_bwd(dc, pre, bias_f, name="attn_cumsum_bwd")
    from_blocks = lambda t: jnp.transpose(t, (1, 3, 0, 2)).reshape(M, D).astype(BF16)
    d_att = jnp.concatenate([from_blocks(dkT_h), from_blocks(dvT_h),
                             jnp.transpose(dqT_h, (2, 0, 1)).reshape(M, D), dpre], axis=1)
    dh2 = _mm(d_att, w_att_t, name="attn_dh", out_dtype=F32)
    d_w_att = _mm_tn(h2b, d_att, name="attn_dw_qkv")
    grads["w_kv"] = d_w_att[:, :2 * D]
    grads["w_q"] = d_w_att[:, 2 * D:3 * D][None]
    grads["w_f"] = d_w_att[:, 3 * D:3 * D + H]
    grads["b_f"] = dbias[0, :H]
    dz2, dz2b, dg01, db01 = _ln_bwd(dh2, z2, g_ln[0, 1][None, :], name="ln2_bwd", dz_next=dz3)

    dh1, dwin0, dwout0, dcw0, dcb0 = ffn_bwd(0, dz2b, u0, act0, h1b, "ffn0")
    dz1, _, dg00, db00 = _ln_bwd(dh1, z1, g_ln[0, 0][None, :], name="ln1_bwd", dz_next=dz2)
    dh0, dpw, dscale = _pool_bwd(dz1, diff, pw, pw_t, scale, name="pool_bwd")

    grads["meta"] = dh0[:N_META]
    grads["pool_w"] = dpw[None]
    grads["pool_scale"] = dscale
    grads["ffn_w_in"] = jnp.stack([dwin0, dwin1])
    grads["ffn_w_out"] = jnp.stack([dwout0, dwout1])
    grads["ffn_conv_w"] = jnp.stack([dcw0, dcw1])
    grads["ffn_conv_b"] = jnp.stack([dcb0, dcb1])
    grads["ln_g"] = jnp.stack([jnp.stack([dg00[0], dg01[0]]), jnp.stack([dg10[0], dg11[0]])])
    grads["ln_b"] = jnp.stack([jnp.stack([db00[0], db01[0]]), jnp.stack([db10[0], db11[0]])])
    grad_x = dh0[N_META:n_tok][None]

    blocks = {n: _to_blocks(grads[n], axis_of[n]) for n, _ in PARAMS}
    by_core = lambda t: jnp.swapaxes(t.reshape((N_CHIPS, 2) + t.shape[1:]), 0, 1).astype(BF16)
    pair = _exchange([by_core(blocks[n]) for n in BIG], "grad_pair_exchange", (1,), lambda pos: pos[2])
    partial = [_pair_sum(p.reshape(2, -1, p.shape[-1]), name="grad_pair_sum_" + n).reshape(p.shape[1:])
               for n, p in zip(BIG, pair)]
    chip_of = lambda pos: 2 * pos[0] + pos[1]
    recv = _exchange(partial, "grad_chip_exchange", (4, 2, 6), chip_of)
    recv_small = _exchange([_pack([blocks[n] for n in SMALL], lead=(N_DEV,))], "grad_small_exchange",
                           tuple(range(1, N_DEV)), _index)[0]
    results = {}
    for n, r in zip(BIG, recv):
        shape = local[n].shape
        as2d = lambda t: t.reshape(-1, shape[-1])
        outs = _adamw(r.reshape(N_CHIPS, -1, shape[-1]), as2d(local[n]), as2d(mom1[n]), as2d(mom2[n]), name="adamw_" + n)
        results[n] = [o_.reshape(shape) for o_ in outs]
    outs = _adamw(recv_small, *[_pack([d[n] for n in SMALL]) for d in (local, mom1, mom2)], name="adamw_small")
    small_out = [_unpack(o_, [local[n].shape for n in SMALL]) for o_ in outs]
    for i, n in enumerate(SMALL):
        results[n] = [small_out[k][i] for k in range(4)]
    return (loss, grad_x, *[results[n][k] for k in range(4) for n, _ in PARAMS])
```

```python
import jax
import jax.numpy as jnp
from jax import lax
from jax.experimental import pallas as pl
from jax.experimental.pallas import tpu as pltpu

F32, BF16 = jnp.float32, jnp.bfloat16
MESH = pl.DeviceIdType.MESH

N_DEV = 8
N_CHIPS = 4
N_META = 16
POOL_WINDOWS = (2, 4, 8, 16)
POOL_HALO = 16
CONV_HALO = 8
CONV_STRIP = 16
ALPHA = 4.0 ** 0.25
LN_EPS = 1e-5
NEG_INF = -1e30
ADAM_LR, ADAM_B1, ADAM_B2, ADAM_EPS, ADAM_WD, ADAM_STEP = 0.001, 0.9, 0.999, 1e-08, 0.01, 10

LANES = 128
PACK_COLS = 1024
ADAM_TILE_BYTES = 4 << 20
PAIR_PIECE_AXIS_MAX = 4
GLU_CHUNK = 256
VMEM_LIMIT = 56 * 1024 * 1024

PARAMS = (("meta", 1), ("pool_w", 2), ("pool_scale", 1), ("w_kv", 1), ("w_f", 0), ("b_f", None),
          ("w_q", 1), ("w_o", 1), ("ffn_w_in", 2), ("ffn_conv_w", 2), ("ffn_conv_b", None),
          ("ffn_w_out", 1), ("ln_g", 2), ("ln_b", 2))
BIG = ("pool_w", "w_kv", "w_q", "w_o", "ffn_w_in", "ffn_w_out")
SMALL = ("meta", "pool_scale", "w_f", "b_f", "ffn_conv_w", "ffn_conv_b", "ln_g", "ln_b")


def _cparams(**kw):
    return pltpu.CompilerParams(vmem_limit_bytes=VMEM_LIMIT, **kw)


def _pick(n, cands):
    for c in cands:
        if n % c == 0:
            return c
    return n


def _round_up(n, m):
    return (n + m - 1) // m * m


def _pack(pieces, lead=()):
    flat = []
    for p in pieces:
        v = p.reshape(lead + (-1,))
        flat.append(jnp.pad(v, [(0, 0)] * len(lead) + [(0, _round_up(v.shape[-1], PACK_COLS) - v.shape[-1])]))
    v = jnp.concatenate(flat, axis=-1)
    rows = _round_up(v.shape[-1] // PACK_COLS, 8)
    v = jnp.pad(v, [(0, 0)] * len(lead) + [(0, rows * PACK_COLS - v.shape[-1])])
    return v.reshape(lead + (rows, PACK_COLS))


def _unpack(buf, shapes, lead=()):
    flat = buf.reshape(lead + (-1,))
    out, off = [], 0
    for s in shapes:
        n = 1
        for d in s:
            n *= d
        out.append(flat[..., off:off + n].reshape(lead + tuple(s)))
        off += _round_up(n, PACK_COLS)
    return out


def _to_blocks(full, axis):
    if axis is None:
        return jnp.broadcast_to(full[None], (N_DEV,) + full.shape)
    s = full.shape
    x = full.reshape(s[:axis] + (N_DEV, s[axis] // N_DEV) + s[axis + 1:])
    return jnp.moveaxis(x, axis, 0)


def _from_blocks(blocks, axis):
    x = jnp.moveaxis(blocks, 0, axis)
    s = x.shape
    return x.reshape(s[:axis] + (s[axis] * s[axis + 1],) + s[axis + 2:])


def _coords():
    return lax.axis_index("x"), lax.axis_index("y"), lax.axis_index("c")


def _flip(pos, mask):
    x, y, c = pos
    return (1 - x if mask & 4 else x, 1 - y if mask & 2 else y, 1 - c if mask & 1 else c)


def _index(pos):
    x, y, c = pos
    return 4 * x + 2 * y + c


def _comm_call(body, name, arrays, out_shapes):
    n = len(arrays)
    hbm = pl.BlockSpec(memory_space=pl.ANY)
    return pl.pallas_call(
        body, name=name, out_shape=out_shapes, in_specs=[hbm] * n, out_specs=[hbm] * n,
        scratch_shapes=[pltpu.SemaphoreType.DMA((7 * n,)), pltpu.SemaphoreType.DMA((7 * n,)),
                        pltpu.SemaphoreType.DMA((n,))],
    )(*arrays)


def _all_gather(blocks, name):
    chip_masks = (4, 2, 6)
    n = len(blocks)

    def body(*refs):
        x_refs, out_refs = refs[:n], refs[n:2 * n]
        send_sems, recv_sems, local_sems = refs[2 * n:]
        me = _coords()
        sibling = _flip(me, 1)

        def copy(a, k, owner, to, from_input=False):
            slot = out_refs[a].at[_index(owner)]
            return pltpu.make_async_remote_copy(
                src_ref=x_refs[a] if from_input else slot, dst_ref=slot,
                send_sem=send_sems.at[7 * a + k], recv_sem=recv_sems.at[7 * a + k],
                device_id=to, device_id_type=MESH)

        mine = [pltpu.make_async_copy(x_refs[a], out_refs[a].at[_index(me)], local_sems.at[a]) for a in range(n)]
        first = [copy(a, 0, me, sibling, True) for a in range(n)]
        first += [copy(a, 1 + j, me, _flip(me, m), True) for j, m in enumerate(chip_masks) for a in range(n)]
        for cp in mine + first:
            cp.start()
        passed = []
        for j, m in enumerate(chip_masks):
            for a in range(n):
                copy(a, 1 + j, _flip(me, m), me).wait_recv()
                passed.append(copy(a, 4 + j, _flip(me, m), sibling))
                passed[-1].start()
        for a in range(n):
            copy(a, 0, sibling, me).wait_recv()
            for j, m in enumerate(chip_masks):
                copy(a, 4 + j, _flip(sibling, m), me).wait_recv()
        for cp in first + passed:
            cp.wait_send()
        for cp in mine:
            cp.wait()

    return _comm_call(body, name, blocks, [jax.ShapeDtypeStruct((N_DEV,) + b.shape, b.dtype) for b in blocks])


def _exchange(segs, name, masks, slot_of, pieces=None):
    n = len(segs)
    pieces = pieces or [1] * n

    def body(*refs):
        seg_refs, out_refs = refs[:n], refs[n:2 * n]
        send_sems, recv_sems, local_sems = refs[2 * n:]
        me = _coords()

        def parts(a, slot_ref):
            rows = slot_ref.shape[0] // pieces[a]
            return [slot_ref.at[pl.ds(p * rows, rows)] for p in range(pieces[a])] if pieces[a] > 1 else [slot_ref]

        def copy(a, k, mask, src, dst):
            return pltpu.make_async_remote_copy(
                src_ref=src, dst_ref=dst, send_sem=send_sems.at[7 * a + k], recv_sem=recv_sems.at[7 * a + k],
                device_id=_flip(me, mask), device_id_type=MESH)

        def slot_copies(a, k, mask):
            peer = _flip(me, mask)
            return [copy(a, k, mask, s, d) for s, d in zip(parts(a, seg_refs[a].at[slot_of(peer)]),
                                                           parts(a, out_refs[a].at[slot_of(me)]))]

        def whole_slot(a, k, mask):
            peer = _flip(me, mask)
            return copy(a, k, mask, seg_refs[a].at[slot_of(peer)], out_refs[a].at[slot_of(peer)])

        mine = [pltpu.make_async_copy(s, d, local_sems.at[a]) for a in range(n)
                for s, d in zip(parts(a, seg_refs[a].at[slot_of(me)]), parts(a, out_refs[a].at[slot_of(me)]))]
        sends = [cp for k, mask in enumerate(masks) for a in range(n) for cp in slot_copies(a, k, mask)]
        for cp in mine + sends:
            cp.start()
        for k, mask in enumerate(masks):
            for a in range(n):
                whole_slot(a, k, mask).wait_recv()
        for k, mask in enumerate(masks):
            for a in range(n):
                whole_slot(a, k, mask).wait_send()
        for a in range(n):
            pltpu.make_async_copy(seg_refs[a].at[slot_of(me)], out_refs[a].at[slot_of(me)], local_sems.at[a]).wait()

    return _comm_call(body, name, segs, [jax.ShapeDtypeStruct(s.shape, s.dtype) for s in segs])


def _pair_sum(pair, name):
    _, rows, cols = pair.shape
    tr = max(t for t in range(16, rows + 1, 16) if rows % t == 0 and 2 * t * cols * 2 <= ADAM_TILE_BYTES)

    def body(p_ref, o_ref):
        o_ref[...] = (p_ref[0].astype(F32) + p_ref[1].astype(F32)).astype(BF16)

    return pl.pallas_call(
        body, name=name, grid=(rows // tr,),
        in_specs=[pl.BlockSpec((2, tr, cols), lambda i: (0, i, 0))],
        out_specs=pl.BlockSpec((tr, cols), lambda i: (i, 0)),
        out_shape=jax.ShapeDtypeStruct((rows, cols), BF16),
        compiler_params=_cparams(),
    )(pair)


def _adamw(recv, w, m, v, name):
    rows, cols = w.shape
    n_slots = recv.shape[0]
    sublanes = 8 * (4 // recv.dtype.itemsize)
    tr = max(t for t in range(sublanes, rows + 1, sublanes)
             if rows % t == 0 and N_DEV * t * cols * 4 <= ADAM_TILE_BYTES)
    c1 = 1.0 - ADAM_B1 ** ADAM_STEP
    c2 = 1.0 - ADAM_B2 ** ADAM_STEP

    def body(r_ref, w_ref, m_ref, v_ref, g_out, d_out, m_out, v_out):
        g = r_ref[0].astype(F32)
        for s in range(1, n_slots):
            g = g + r_ref[s].astype(F32)
        m_new = ADAM_B1 * m_ref[...] + (1.0 - ADAM_B1) * g
        v_new = ADAM_B2 * v_ref[...] + (1.0 - ADAM_B2) * (g * g)
        m_hat = m_new / c1
        v_hat = v_new / c2
        g_out[...] = g
        d_out[...] = -ADAM_LR * (m_hat / (jnp.sqrt(v_hat) + ADAM_EPS) + ADAM_WD * w_ref[...])
        m_out[...] = m_new
        v_out[...] = v_new

    tile = pl.BlockSpec((tr, cols), lambda i: (i, 0))
    return pl.pallas_call(
        body, name=name, grid=(rows // tr,),
        in_specs=[pl.BlockSpec((n_slots, tr, cols), lambda i: (0, i, 0)), tile, tile, tile],
        out_specs=[tile] * 4,
        out_shape=[jax.ShapeDtypeStruct(w.shape, F32)] * 4,
        compiler_params=_cparams(),
    )(recv, w, m, v)


def _mm(a, b, *, name, out_dtype, b_transposed=False):
    M, K = a.shape
    N = b.shape[0] if b_transposed else b.shape[1]
    tm = _pick(M, (640, 128))
    tn = _pick(N, (1408, 1024, 640, 512, 256, 128))
    tk = K if K <= 1024 else _pick(K, (1024, 640, 512, 256, 128))
    nk = K // tk

    def body(*refs):
        a_ref, b_ref, o_ref = refs[:3]
        acc_ref = refs[-1] if nk > 1 else None
        if b_transposed:
            prod = _dot_nt(a_ref[...], b_ref[...])
        else:
            prod = jnp.dot(a_ref[...], b_ref[...], preferred_element_type=F32)
        if nk == 1:
            o_ref[...] = prod.astype(out_dtype)
        else:
            k = pl.program_id(2)

            @pl.when(k == 0)
            def _():
                acc_ref[...] = prod

            @pl.when(k > 0)
            def _():
                acc_ref[...] += prod

            @pl.when(k == nk - 1)
            def _():
                o_ref[...] = acc_ref[...].astype(out_dtype)

    b_spec = (pl.BlockSpec((tn, tk), lambda i, j, k: (j, k)) if b_transposed
              else pl.BlockSpec((tk, tn), lambda i, j, k: (k, j)))
    return pl.pallas_call(
        body, name=name, grid=(M // tm, N // tn, nk),
        in_specs=[pl.BlockSpec((tm, tk), lambda i, j, k: (i, k)), b_spec],
        out_specs=pl.BlockSpec((tm, tn), lambda i, j, k: (i, j)),
        out_shape=jax.ShapeDtypeStruct((M, N), out_dtype),
        scratch_shapes=[pltpu.VMEM((tm, tn), F32)] if nk > 1 else [],
        compiler_params=_cparams(),
    )(a, b)


def _mm_tn(a, b, *, name):
    T, M = a.shape
    N = b.shape[1]
    tt = _pick(T, (1664, 640, 128))
    tm = _pick(M, (1408, 1024, 512, 256, 128))
    tn = _pick(N, (1024, 640, 512, 256, 128))

    def body(a_ref, b_ref, o_ref):
        prod = lax.dot_general(a_ref[...], b_ref[...], (((0,), (0,)), ((), ())), preferred_element_type=F32)

        @pl.when(pl.program_id(2) == 0)
        def _():
            o_ref[...] = prod

        @pl.when(pl.program_id(2) > 0)
        def _():
            o_ref[...] += prod

    return pl.pallas_call(
        body, name=name, grid=(M // tm, N // tn, T // tt),
        in_specs=[pl.BlockSpec((tt, tm), lambda i, j, k: (k, i)), pl.BlockSpec((tt, tn), lambda i, j, k: (k, j))],
        out_specs=pl.BlockSpec((tm, tn), lambda i, j, k: (i, j)),
        out_shape=jax.ShapeDtypeStruct((M, N), F32),
        compiler_params=_cparams(),
    )(a, b)


def _layer_norm(z, g, b):
    mu = jnp.mean(z, axis=-1, keepdims=True)
    xc = z - mu
    var = jnp.mean(xc * xc, axis=-1, keepdims=True)
    return xc * lax.rsqrt(var + LN_EPS) * g + b


def _mm_ln(a, w, res, g, b, *, name, target=None, n_tok=None):
    M, K = a.shape
    D = w.shape[1]
    tm = _pick(M, (640, 128))
    final = target is not None

    def body(*refs):
        if final:
            a_ref, w_ref, r_ref, g_ref, b_ref, t_ref, z_ref, dy_ref, loss_ref = refs
        else:
            a_ref, w_ref, r_ref, g_ref, b_ref, z_ref, h_ref, hb_ref = refs
        z = ALPHA * r_ref[...] + jnp.dot(a_ref[...], w_ref[...], preferred_element_type=F32)
        z_ref[...] = z
        h = _layer_norm(z, g_ref[...], b_ref[...])
        if not final:
            h_ref[...] = h
            hb_ref[...] = h.astype(BF16)
            return
        i = pl.program_id(0)
        row = i * tm + lax.broadcasted_iota(jnp.int32, (tm, 1), 0)
        valid = (row >= N_META) & (row < n_tok)
        err = jnp.where(valid, h - t_ref[...], 0.0)
        dy_ref[...] = err / D

        @pl.when(i == 0)
        def _():
            loss_ref[...] = jnp.zeros_like(loss_ref)

        loss_ref[...] += 0.5 * jnp.sum(jnp.sum(err * err, axis=1, keepdims=True) / D, axis=0, keepdims=True)

    row_blk = lambda cols: pl.BlockSpec((tm, cols), lambda i: (i, 0))
    vec = pl.BlockSpec((1, D), lambda i: (0, 0))
    in_specs = [row_blk(K), pl.BlockSpec((K, D), lambda i: (0, 0)), row_blk(D), vec, vec]
    args = [a, w, res, g, b]
    if final:
        in_specs.append(row_blk(D))
        args.append(target)
        out_specs = [row_blk(D), row_blk(D), pl.BlockSpec((8, LANES), lambda i: (0, 0))]
        out_shape = [jax.ShapeDtypeStruct((M, D), F32)] * 2 + [jax.ShapeDtypeStruct((8, LANES), F32)]
    else:
        out_specs = [row_blk(D)] * 3
        out_shape = [jax.ShapeDtypeStruct((M, D), F32)] * 2 + [jax.ShapeDtypeStruct((M, D), BF16)]
    return pl.pallas_call(
        body, name=name, grid=(M // tm,), in_specs=in_specs, out_specs=out_specs, out_shape=out_shape,
        compiler_params=_cparams(),
    )(*args)


def _ln_bwd(dh, z, g, *, name, dz_next=None):
    M, D = z.shape
    tm = _pick(M, (640, 128))
    has_next = dz_next is not None

    def body(*refs):
        if has_next:
            dh_ref, nx_ref, z_ref, g_ref, dz_ref, dzb_ref, dg_ref, db_ref = refs
            dh_v = dh_ref[...] + ALPHA * nx_ref[...]
        else:
            dh_ref, z_ref, g_ref, dz_ref, dzb_ref, dg_ref, db_ref = refs
            dh_v = dh_ref[...]
        z_v = z_ref[...]
        mu = jnp.mean(z_v, axis=-1, keepdims=True)
        xc = z_v - mu
        rstd = lax.rsqrt(jnp.mean(xc * xc, axis=-1, keepdims=True) + LN_EPS)
        xhat = xc * rstd
        dxhat = dh_v * g_ref[...]
        dz = rstd * (dxhat - jnp.mean(dxhat, axis=-1, keepdims=True)
                     - xhat * jnp.mean(dxhat * xhat, axis=-1, keepdims=True))
        dz_ref[...] = dz
        dzb_ref[...] = dz.astype(BF16)

        @pl.when(pl.program_id(0) == 0)
        def _():
            dg_ref[...] = jnp.zeros_like(dg_ref)
            db_ref[...] = jnp.zeros_like(db_ref)

        dg_ref[...] += jnp.sum(dh_v * xhat, axis=0, keepdims=True)
        db_ref[...] += jnp.sum(dh_v, axis=0, keepdims=True)

    row_blk = pl.BlockSpec((tm, D), lambda i: (i, 0))
    vec = pl.BlockSpec((1, D), lambda i: (0, 0))
    args = [dh] + ([dz_next] if has_next else []) + [z, g]
    in_specs = [row_blk] * (len(args) - 1) + [vec]
    return pl.pallas_call(
        body, name=name, grid=(M // tm,), in_specs=in_specs,
        out_specs=[row_blk, row_blk, vec, vec],
        out_shape=[jax.ShapeDtypeStruct((M, D), F32), jax.ShapeDtypeStruct((M, D), BF16),
                   jax.ShapeDtypeStruct((1, D), F32), jax.ShapeDtypeStruct((1, D), F32)],
        compiler_params=_cparams(),
    )(*args)


def _pool_fwd(h0, pw, scale, g, b, *, name):
    M, D = h0.shape
    n_groups, G, _ = pw.shape
    tm = _pick(M, (640, 128))

    def body(x_ref, halo_ref, pw_ref, sc_ref, g_ref, b_ref, z_ref, h_ref, hb_ref, diff_ref, ext_ref, mix_ref):
        i = pl.program_id(0)
        x = x_ref[...]
        ext_ref[0:POOL_HALO, :] = jnp.where(i == 0, 0.0, halo_ref[...])
        ext_ref[POOL_HALO:, :] = x
        tok = i * tm + lax.broadcasted_iota(jnp.int32, (tm, 1), 0)
        for gi, win in enumerate(POOL_WINDOWS):
            cols = slice(gi * G, (gi + 1) * G)
            xs = x[:, cols]
            s = xs
            for k in range(1, win):
                s = s + ext_ref[pl.ds(POOL_HALO - k, tm), cols]
            count = jnp.minimum(tok + 1, win).astype(F32)
            d = (s / count - xs).astype(BF16)
            diff_ref[:, cols] = d
            mix_ref[:, cols] = jnp.dot(d, pw_ref[gi], preferred_element_type=F32)
        z = ALPHA * x + mix_ref[...] * sc_ref[...]
        z_ref[...] = z
        h = _layer_norm(z, g_ref[...], b_ref[...])
        h_ref[...] = h
        hb_ref[...] = h.astype(BF16)

    row_blk = pl.BlockSpec((tm, D), lambda i: (i, 0))
    vec = pl.BlockSpec((1, D), lambda i: (0, 0))
    halo = pl.BlockSpec((POOL_HALO, D), lambda i: (jnp.maximum(i * (tm // POOL_HALO) - 1, 0), 0))
    return pl.pallas_call(
        body, name=name, grid=(M // tm,),
        in_specs=[row_blk, halo, pl.BlockSpec((n_groups, G, G), lambda i: (0, 0, 0)), vec, vec, vec],
        out_specs=[row_blk] * 4,
        out_shape=[jax.ShapeDtypeStruct((M, D), F32)] * 2 + [jax.ShapeDtypeStruct((M, D), BF16)] * 2,
        scratch_shapes=[pltpu.VMEM((tm + POOL_HALO, D), F32), pltpu.VMEM((tm, D), F32)],
        compiler_params=_cparams(),
    )(h0, h0, pw, scale, g, b)


def _pool_bwd(dz, diff, pw, scale, *, name):
    M, D = dz.shape
    n_groups, G, _ = pw.shape
    tm = _pick(M, (640, 128))
    nt = M // tm

    def body(dz_ref, halo_ref, diff_ref, pw_ref, sc_ref, dh_ref, dpw_ref, dsc_ref, ext_ref, q_ref):
        i = pl.program_id(0)
        dz_v = dz_ref[...]
        ext_ref[0:tm, :] = dz_v
        ext_ref[tm:, :] = jnp.where(i == nt - 1, 0.0, halo_ref[...])
        tok = i * tm + lax.broadcasted_iota(jnp.int32, (tm + POOL_HALO, 1), 0)

        @pl.when(i == 0)
        def _():
            dpw_ref[...] = jnp.zeros_like(dpw_ref)
            dsc_ref[...] = jnp.zeros_like(dsc_ref)

        for gi, win in enumerate(POOL_WINDOWS):
            cols = slice(gi * G, (gi + 1) * G)
            dmix = (ext_ref[:, cols] * sc_ref[:, cols]).astype(BF16)
            ddiff = _dot_nt(dmix, pw_ref[gi])
            count = jnp.minimum(tok + 1, win).astype(F32)
            q_ref[:, cols] = ddiff / count
            acc = -ddiff[0:tm]
            for k in range(win):
                acc = acc + q_ref[pl.ds(k, tm), cols]
            dh_ref[:, cols] = ALPHA * dz_v[:, cols] + acc
            d = diff_ref[:, cols]
            dpw_ref[gi] += lax.dot_general(d, dmix[0:tm], (((0,), (0,)), ((), ())), preferred_element_type=F32)
            mixed = jnp.dot(d, pw_ref[gi], preferred_element_type=F32)
            dsc_ref[:, cols] += jnp.sum(dz_v[:, cols] * mixed, axis=0, keepdims=True)

    row_blk = pl.BlockSpec((tm, D), lambda i: (i, 0))
    vec = pl.BlockSpec((1, D), lambda i: (0, 0))
    per_tile = tm // POOL_HALO
    halo = pl.BlockSpec((POOL_HALO, D), lambda i: (jnp.minimum((i + 1) * per_tile, nt * per_tile - 1), 0))
    wblk = pl.BlockSpec((n_groups, G, G), lambda i: (0, 0, 0))
    return pl.pallas_call(
        body, name=name, grid=(nt,),
        in_specs=[row_blk, halo, row_blk, wblk, vec],
        out_specs=[row_blk, wblk, vec],
        out_shape=[jax.ShapeDtypeStruct((M, D), F32), jax.ShapeDtypeStruct((n_groups, G, G), F32),
                   jax.ShapeDtypeStruct((1, D), F32)],
        scratch_shapes=[pltpu.VMEM((tm + POOL_HALO, D), F32), pltpu.VMEM((tm + POOL_HALO, D), F32)],
        compiler_params=_cparams(),
    )(dz, dz, diff, pw, scale)


def _glu_interleave(x):
    s = x.shape
    n = s[-1] // (2 * GLU_CHUNK)
    return jnp.swapaxes(x.reshape(s[:-1] + (2, n, GLU_CHUNK)), -3, -2).reshape(s)


def _glu_deinterleave(x):
    s = x.shape
    n = s[-1] // (2 * GLU_CHUNK)
    return jnp.swapaxes(x.reshape(s[:-1] + (n, 2, GLU_CHUNK)), -3, -2).reshape(s)


def _taps(u_ref, head_ref, r0):
    src, base = (head_ref, CONV_HALO) if r0 == 0 else (u_ref, r0)
    return tuple(src[pl.ds(base - k, CONV_STRIP), :] for k in range(3))


def _conv_glu_fwd(u, cw, cb, *, name):
    M, F2 = u.shape
    tm = _pick(M, (640, 128))
    tc = 2 * GLU_CHUNK

    def body(u_ref, halo_ref, w_ref, b_ref, o_ref, head_ref):
        i = pl.program_id(0)
        head_ref[0:CONV_HALO, :] = jnp.where(i == 0, 0.0, halo_ref[...])
        head_ref[CONV_HALO:, :] = u_ref[0:CONV_STRIP, :]
        w0, w1, w2, b = w_ref[0:1, :], w_ref[1:2, :], w_ref[2:3, :], b_ref[...]
        for r0 in range(0, tm, CONV_STRIP):
            u0, u1, u2 = _taps(u_ref, head_ref, r0)
            c = b + w0 * u2 + w1 * u1 + w2 * u0
            a, g = c[:, :GLU_CHUNK], c[:, GLU_CHUNK:]
            o_ref[pl.ds(r0, CONV_STRIP), :] = (a * jax.nn.sigmoid(a) * g).astype(BF16)

    per_tile = tm // CONV_HALO
    return pl.pallas_call(
        body, name=name, grid=(M // tm, F2 // tc),
        in_specs=[pl.BlockSpec((tm, tc), lambda i, j: (i, j)),
                  pl.BlockSpec((CONV_HALO, tc), lambda i, j: (jnp.maximum(i * per_tile - 1, 0), j)),
                  pl.BlockSpec((3, tc), lambda i, j: (0, j)), pl.BlockSpec((1, tc), lambda i, j: (0, j))],
        out_specs=pl.BlockSpec((tm, GLU_CHUNK), lambda i, j: (i, j)),
        out_shape=jax.ShapeDtypeStruct((M, F2 // 2), BF16),
        scratch_shapes=[pltpu.VMEM((CONV_HALO + CONV_STRIP, tc), F32)],
        compiler_params=_cparams(),
    )(u, u, cw, cb)


def _conv_glu_bwd(u, dact, cw, cb, *, name):
    M, F2 = u.shape
    tm = _pick(M, (640, 128))
    nt = M // tm
    tc = 2 * GLU_CHUNK

    def body(u_ref, halo_ref, da_ref, w_ref, b_ref, du_ref, dw_ref, db_ref, head_ref, dcx_ref, carry_ref):
        i = pl.program_id(1)
        head_ref[0:CONV_HALO, :] = jnp.where(i == nt - 1, 0.0, halo_ref[...])
        head_ref[CONV_HALO:, :] = u_ref[0:CONV_STRIP, :]
        w0, w1, w2, b = w_ref[0:1, :], w_ref[1:2, :], w_ref[2:3, :], b_ref[...]

        @pl.when(i == 0)
        def _():
            dw_ref[...] = jnp.zeros_like(dw_ref)
            db_ref[...] = jnp.zeros_like(db_ref)
            carry_ref[...] = jnp.zeros_like(carry_ref)

        def fold(t):
            return sum(t[r:r + 8] for r in range(0, CONV_STRIP, 8))

        dcx_ref[tm:, :] = carry_ref[...]
        s_b = s_0 = s_1 = s_2 = jnp.zeros((8, tc), F32)
        for r0 in reversed(range(0, tm, CONV_STRIP)):
            rows = pl.ds(r0, CONV_STRIP)
            u0, u1, u2 = _taps(u_ref, head_ref, r0)
            c = b + w0 * u2 + w1 * u1 + w2 * u0
            a, g = c[:, :GLU_CHUNK], c[:, GLU_CHUNK:]
            sig = jax.nn.sigmoid(a)
            dact_v = da_ref[rows, :]
            d_a = dact_v * g * (sig * (1.0 + a * (1.0 - sig)))
            d_g = dact_v * (a * sig)
            dc = jnp.concatenate([d_a, d_g], axis=1)
            dcx_ref[rows, :] = dc
            du = w2 * dc + w1 * dcx_ref[pl.ds(r0 + 1, CONV_STRIP), :] + w0 * dcx_ref[pl.ds(r0 + 2, CONV_STRIP), :]
            du_ref[rows, :] = du.astype(BF16)
            s_b, s_0, s_1, s_2 = s_b + fold(dc), s_0 + fold(dc * u2), s_1 + fold(dc * u1), s_2 + fold(dc * u0)
        carry_ref[...] = dcx_ref[0:CONV_HALO, :]
        db_ref[...] += jnp.sum(s_b, axis=0, keepdims=True)
        dw_ref[0:1, :] += jnp.sum(s_0, axis=0, keepdims=True)
        dw_ref[1:2, :] += jnp.sum(s_1, axis=0, keepdims=True)
        dw_ref[2:3, :] += jnp.sum(s_2, axis=0, keepdims=True)

    per_tile = tm // CONV_HALO
    rev = lambda i: nt - 1 - i
    return pl.pallas_call(
        body, name=name, grid=(F2 // tc, nt),
        in_specs=[pl.BlockSpec((tm, tc), lambda j, i: (rev(i), j)),
                  pl.BlockSpec((CONV_HALO, tc), lambda j, i: (jnp.maximum(rev(i) * per_tile - 1, 0), j)),
                  pl.BlockSpec((tm, GLU_CHUNK), lambda j, i: (rev(i), j)),
                  pl.BlockSpec((3, tc), lambda j, i: (0, j)), pl.BlockSpec((1, tc), lambda j, i: (0, j))],
        out_specs=[pl.BlockSpec((tm, tc), lambda j, i: (rev(i), j)),
                   pl.BlockSpec((3, tc), lambda j, i: (0, j)), pl.BlockSpec((1, tc), lambda j, i: (0, j))],
        out_shape=[jax.ShapeDtypeStruct((M, F2), BF16), jax.ShapeDtypeStruct((3, F2), F32),
                   jax.ShapeDtypeStruct((1, F2), F32)],
        scratch_shapes=[pltpu.VMEM((CONV_HALO + CONV_STRIP, tc), F32), pltpu.VMEM((tm + CONV_HALO, tc), F32),
                        pltpu.VMEM((CONV_HALO, tc), F32)],
        compiler_params=_cparams(),
    )(u, u, dact, cw, cb)


def _split3(x):
    hi = x.astype(BF16)
    r = x - hi.astype(F32)
    mid = r.astype(BF16)
    lo = (r - mid.astype(F32)).astype(BF16)
    return hi, mid, lo


def _tri_sum(tri, x):
    return sum(jnp.dot(tri, part, preferred_element_type=F32) for part in _split3(x))


def _log_sigmoid(x):
    return jnp.minimum(x, 0.0) - jnp.log1p(jnp.exp(-jnp.abs(x)))


def _forget_cumsum(pre, bias, *, name):
    M, C = pre.shape
    tm = _pick(M, (640, 128))

    def body(p_ref, b_ref, c_ref, carry_ref):
        i = pl.program_id(0)

        @pl.when(i == 0)
        def _():
            carry_ref[...] = jnp.zeros_like(carry_ref)

        logf = _log_sigmoid(p_ref[...] + b_ref[...])
        r = lax.broadcasted_iota(jnp.int32, (tm, tm), 0)
        s = lax.broadcasted_iota(jnp.int32, (tm, tm), 1)
        c_ref[...] = _tri_sum((s <= r).astype(BF16), logf) + carry_ref[...]
        carry_ref[...] = c_ref[pl.ds(tm - 1, 1), :]

    return pl.pallas_call(
        body, name=name, grid=(M // tm,),
        in_specs=[pl.BlockSpec((tm, C), lambda i: (i, 0)), pl.BlockSpec((1, C), lambda i: (0, 0))],
        out_specs=pl.BlockSpec((tm, C), lambda i: (i, 0)),
        out_shape=jax.ShapeDtypeStruct((M, C), F32),
        scratch_shapes=[pltpu.VMEM((1, C), F32)],
        compiler_params=_cparams(),
    )(pre, bias)


def _forget_cumsum_bwd(dc, pre, bias, *, name):
    M, C = pre.shape
    tm = _pick(M, (640, 128))
    nt = M // tm

    def body(dc_ref, p_ref, b_ref, dp_ref, db_ref, carry_ref, run_ref):
        i = pl.program_id(0)

        @pl.when(i == 0)
        def _():
            carry_ref[...] = jnp.zeros_like(carry_ref)
            db_ref[...] = jnp.zeros_like(db_ref)

        r = lax.broadcasted_iota(jnp.int32, (tm, tm), 0)
        s = lax.broadcasted_iota(jnp.int32, (tm, tm), 1)
        run_ref[...] = _tri_sum((s >= r).astype(BF16), dc_ref[...]) + carry_ref[...]
        carry_ref[...] = run_ref[pl.ds(0, 1), :]
        dpre = run_ref[...] * jax.nn.sigmoid(-(p_ref[...] + b_ref[...]))
        dp_ref[...] = dpre.astype(BF16)
        db_ref[...] += jnp.sum(dpre, axis=0, keepdims=True)

    rev_blk = pl.BlockSpec((tm, C), lambda i: (nt - 1 - i, 0))
    vec = pl.BlockSpec((1, C), lambda i: (0, 0))
    return pl.pallas_call(
        body, name=name, grid=(nt,),
        in_specs=[rev_blk, rev_blk, vec], out_specs=[rev_blk, vec],
        out_shape=[jax.ShapeDtypeStruct((M, C), BF16), jax.ShapeDtypeStruct((1, C), F32)],
        scratch_shapes=[pltpu.VMEM((1, C), F32), pltpu.VMEM((tm, C), F32)],
        compiler_params=_cparams(),
    )(dc, pre, bias)


def _causal_mask(tm):
    key = lax.broadcasted_iota(jnp.int32, (tm, tm), 0)
    query = lax.broadcasted_iota(jnp.int32, (tm, tm), 1)
    return key <= query


def _dot_nt(a, b):
    return lax.dot_general(a, b, (((1,), (1,)), ((), ())), preferred_element_type=F32)


def _loop_in_pairs(n, step, init):
    carry = lax.fori_loop(0, n // 2, lambda p, c: step(2 * p + 1, step(2 * p, c)), init)
    return lax.fori_loop(2 * (n // 2), n, step, carry)


def _attn_specs(H, M, dh, tm):
    nt = M // tm
    qT_blk = pl.BlockSpec((1, dh, tm), lambda h, i: (h, 0, i))
    row_blk = pl.BlockSpec((1, 1, tm), lambda h, i: (h, 0, i))
    kv_blk = pl.BlockSpec((1, M, dh), lambda h, i: (h, 0, 0))
    kvT_blk = pl.BlockSpec((1, nt, dh, tm), lambda h, i: (h, 0, 0, 0))
    ck_blk = pl.BlockSpec((1, M, 1), lambda h, i: (h, 0, 0))
    return qT_blk, row_blk, kv_blk, kvT_blk, ck_blk


def _attn_fwd(qT, k, vT, cq, ck, *, name):
    H, M, dh = k.shape
    tm = vT.shape[-1]
    qT_blk, row_blk, kv_blk, kvT_blk, ck_blk = _attn_specs(H, M, dh, tm)

    def body(qT_ref, k_ref, vT_ref, cq_ref, ck_ref, oT_ref, lse_ref):
        i = pl.program_id(1)
        qsT = qT_ref[0] * jnp.asarray(dh ** -0.5, BF16)
        cq_v = cq_ref[0]

        def block(j, carry, masked):
            m, l, acc = carry
            keys = pl.ds(pl.multiple_of(j * tm, tm), tm)
            sT = jnp.dot(k_ref[0, keys, :], qsT, preferred_element_type=F32) + (cq_v - ck_ref[0, keys, :])
            if masked:
                sT = jnp.where(_causal_mask(tm), sT, NEG_INF)
            m_new = jnp.maximum(m, jnp.max(sT, axis=0, keepdims=True))
            a = jnp.exp(m - m_new)
            pT = jnp.exp(sT - m_new)
            l = a * l + jnp.sum(pT, axis=0, keepdims=True)
            acc = a * acc + jnp.dot(vT_ref[0, j], pT.astype(BF16), preferred_element_type=F32)
            return m_new, l, acc

        init = (jnp.full((1, tm), NEG_INF, F32), jnp.zeros((1, tm), F32), jnp.zeros((dh, tm), F32))
        carry = _loop_in_pairs(i, lambda j, c: block(j, c, False), init)
        m, l, acc = block(i, carry, True)
        oT_ref[0] = (acc / l).astype(BF16)
        lse_ref[0] = m + jnp.log(l)

    return pl.pallas_call(
        body, name=name, grid=(H, M // tm),
        in_specs=[qT_blk, kv_blk, kvT_blk, row_blk, ck_blk], out_specs=[qT_blk, row_blk],
        out_shape=[jax.ShapeDtypeStruct((H, dh, M), BF16), jax.ShapeDtypeStruct((H, 1, M), F32)],
        compiler_params=_cparams(),
    )(qT, k, vT, cq, ck)


def _attn_bwd(qT, k, v, kT, oT, doT, lse, cq, ck, *, name):
    H, M, dh = k.shape
    tm = kT.shape[-1]
    qT_blk, row_blk, kv_blk, kvT_blk, ck_blk = _attn_specs(H, M, dh, tm)

    def body(qT_ref, oT_ref, doT_ref, lse_ref, cq_ref, k_ref, v_ref, kT_ref, ck_ref,
             dqT_ref, dkT_ref, dvT_ref, dcq_ref, dck_ref):
        i = pl.program_id(1)

        @pl.when(i == 0)
        def _():
            dkT_ref[...] = jnp.zeros_like(dkT_ref)
            dvT_ref[...] = jnp.zeros_like(dvT_ref)
            dck_ref[...] = jnp.zeros_like(dck_ref)

        qsT = qT_ref[0] * jnp.asarray(dh ** -0.5, BF16)
        doT = doT_ref[0]
        delta = jnp.sum(doT.astype(F32) * oT_ref[0].astype(F32), axis=0, keepdims=True)
        shift = cq_ref[0] - lse_ref[0]

        def block(j, carry, masked):
            dq, dcq = carry
            keys = pl.ds(pl.multiple_of(j * tm, tm), tm)
            sT = jnp.dot(k_ref[0, keys, :], qsT, preferred_element_type=F32) + (shift - ck_ref[0, keys, :])
            pT = jnp.exp(sT)
            if masked:
                pT = jnp.where(_causal_mask(tm), pT, 0.0)
            dpT = jnp.dot(v_ref[0, keys, :], doT, preferred_element_type=F32)
            dsT = pT * (dpT - delta)
            ds_b = dsT.astype(BF16)
            dvT_ref[0, j] += _dot_nt(doT, pT.astype(BF16))
            dkT_ref[0, j] += _dot_nt(qsT, ds_b)
            dck_ref[0, keys, :] += -jnp.sum(dsT, axis=1, keepdims=True)
            return (dq + jnp.dot(kT_ref[0, j], ds_b, preferred_element_type=F32),
                    dcq + jnp.sum(dsT, axis=0, keepdims=True))

        init = (jnp.zeros((dh, tm), F32), jnp.zeros((1, tm), F32))
        dq, dcq = block(i, _loop_in_pairs(i, lambda j, c: block(j, c, False), init), True)
        dqT_ref[0] = (dq * dh ** -0.5).astype(BF16)
        dcq_ref[0] = dcq

    blocked = jax.ShapeDtypeStruct(kT.shape, F32)
    return pl.pallas_call(
        body, name=name, grid=(H, M // tm),
        in_specs=[qT_blk, qT_blk, qT_blk, row_blk, row_blk, kv_blk, kv_blk, kvT_blk, ck_blk],
        out_specs=[qT_blk, kvT_blk, kvT_blk, row_blk, ck_blk],
        out_shape=[jax.ShapeDtypeStruct((H, dh, M), BF16), blocked, blocked,
                   jax.ShapeDtypeStruct((H, 1, M), F32), jax.ShapeDtypeStruct((H, M, 1), F32)],
        compiler_params=_cparams(),
    )(qT, oT, doT, lse, cq, k, v, kT, ck)


def kernel(x, meta, pool_w, pool_scale, w_kv, w_f, b_f, w_q, w_o, ffn_w_in, ffn_conv_w, ffn_conv_b, ffn_w_out, ln_g, ln_b, loss_target, m_meta, m_pool_w, m_pool_scale, m_w_kv, m_w_f, m_b_f, m_w_q, m_w_o, m_ffn_w_in, m_ffn_conv_w, m_ffn_conv_b, m_ffn_w_out, m_ln_g, m_ln_b, v_meta, v_pool_w, v_pool_scale, v_w_kv, v_w_f, v_b_f, v_w_q, v_w_o, v_ffn_w_in, v_ffn_conv_w, v_ffn_conv_b, v_ffn_w_out, v_ln_g, v_ln_b):
    local = dict(meta=meta, pool_w=pool_w, pool_scale=pool_scale, w_kv=w_kv, w_f=w_f, b_f=b_f, w_q=w_q, w_o=w_o,
                 ffn_w_in=ffn_w_in, ffn_conv_w=ffn_conv_w, ffn_conv_b=ffn_conv_b, ffn_w_out=ffn_w_out,
                 ln_g=ln_g, ln_b=ln_b)
    mom1 = dict(meta=m_meta, pool_w=m_pool_w, pool_scale=m_pool_scale, w_kv=m_w_kv, w_f=m_w_f, b_f=m_b_f,
                w_q=m_w_q, w_o=m_w_o, ffn_w_in=m_ffn_w_in, ffn_conv_w=m_ffn_conv_w, ffn_conv_b=m_ffn_conv_b,
                ffn_w_out=m_ffn_w_out, ln_g=m_ln_g, ln_b=m_ln_b)
    mom2 = dict(meta=v_meta, pool_w=v_pool_w, pool_scale=v_pool_scale, w_kv=v_w_kv, w_f=v_w_f, b_f=v_b_f,
                w_q=v_w_q, w_o=v_w_o, ffn_w_in=v_ffn_w_in, ffn_conv_w=v_ffn_conv_w, ffn_conv_b=v_ffn_conv_b,
                ffn_w_out=v_ffn_w_out, ln_g=v_ln_g, ln_b=v_ln_b)
    axis_of = dict(PARAMS)

    S, D = x.shape[1], x.shape[2]
    H = b_f.shape[0]
    dh = D // H
    n_tok = N_META + S
    M = _round_up(n_tok, LANES)
    tm = _pick(M, (640, 128))
    nt = M // tm
    F2 = ffn_conv_b.shape[1]
    F = F2 // 2
    depth = ffn_conv_b.shape[0]

    small_sharded = [n for n in SMALL if axis_of[n] is not None]
    got = _all_gather([local[n].astype(BF16) for n in BIG] + [_pack([local[n] for n in small_sharded])],
                      "gather_weights")
    wb = {n: _from_blocks(blk, axis_of[n]) for n, blk in zip(BIG, got)}
    small_blocks = _unpack(got[-1], [local[n].shape for n in small_sharded], lead=(N_DEV,))
    wf32 = {n: _from_blocks(blk, axis_of[n]) for n, blk in zip(small_sharded, small_blocks)}

    pw = wb["pool_w"][0]
    wf_pad = jnp.pad(wf32["w_f"].astype(BF16), ((0, 0), (0, LANES - H)))
    w_qkv = jnp.concatenate([wb["w_kv"], wb["w_q"][0]], axis=1)
    w_att = jnp.concatenate([w_qkv, wf_pad], axis=1)
    wo = wb["w_o"][0]
    w_in = _glu_interleave(wb["ffn_w_in"])
    w_out = wb["ffn_w_out"]
    conv_w = _glu_interleave(wf32["ffn_conv_w"])
    conv_b = _glu_interleave(ffn_conv_b)[:, None, :]
    scale = wf32["pool_scale"]
    g_ln, b_ln = wf32["ln_g"], wf32["ln_b"]
    ln = lambda i, j: (g_ln[i, j][None, :], b_ln[i, j][None, :])
    bias_f = jnp.pad(b_f, (0, LANES - H))[None, :]

    pad_rows = M - n_tok
    h0 = jnp.concatenate([wf32["meta"], x[0], jnp.zeros((pad_rows, D), F32)], axis=0)
    target = jnp.concatenate([jnp.zeros((N_META, D), F32), loss_target[0], jnp.zeros((pad_rows, D), F32)], axis=0)

    z1, h1, h1b, diff = _pool_fwd(h0, pw, scale, *ln(0, 0), name="pool_fwd")
    u0 = _mm(h1b, w_in[0], name="ffn0_up", out_dtype=F32)
    act0 = _conv_glu_fwd(u0, conv_w[0], conv_b[0], name="ffn0_conv")
    z2, h2, h2b = _mm_ln(act0, w_out[0], h1, *ln(0, 1), name="ffn0_down_ln")

    qkv = _mm(h2b, w_qkv, name="attn_qkv", out_dtype=BF16)
    pre = _mm(h2b, wf_pad, name="attn_gate", out_dtype=F32)
    c = _forget_cumsum(pre, bias_f, name="attn_cumsum")
    qkv4 = qkv.reshape(M, 3, H, dh)
    rows_first = lambda t: jnp.transpose(t, (1, 0, 2))
    cols_first = lambda t: jnp.transpose(t, (1, 2, 0))
    key_blocks = lambda t: jnp.transpose(t.reshape(nt, tm, H, dh), (2, 0, 3, 1))
    k_h, v_h = rows_first(qkv4[:, 0]), rows_first(qkv4[:, 1])
    kT_h, vT_h, qT_h = key_blocks(qkv4[:, 0]), key_blocks(qkv4[:, 1]), cols_first(qkv4[:, 2])
    c_t = c[:, :H].T
    cq = c_t[:, None, :]
    ck = c_t[:, :, None]
    oT_h, lse = _attn_fwd(qT_h, k_h, vT_h, cq, ck, name="attn_fwd")
    o = jnp.transpose(oT_h, (2, 0, 1)).reshape(M, D)
    z3, h3, h3b = _mm_ln(o, wo, h2, *ln(1, 0), name="attn_out_ln")

    u1 = _mm(h3b, w_in[1], name="ffn1_up", out_dtype=F32)
    act1 = _conv_glu_fwd(u1, conv_w[1], conv_b[1], name="ffn1_conv")
    z4, dy, loss_part = _mm_ln(act1, w_out[1], h3, *ln(1, 1), name="ffn1_down_loss", target=target, n_tok=n_tok)
    loss = lax.psum(loss_part[0, 0], ("x", "y", "c"))

    grads = {}

    def ffn_bwd(layer, dz_b, u, act, h_in_b, tag):
        dact = _mm(dz_b, w_out[layer], name=tag + "_dact", out_dtype=F32, b_transposed=True)
        du, dcw, dcb = _conv_glu_bwd(u, dact, conv_w[layer], conv_b[layer], name=tag + "_conv_bwd")
        dh = _mm(du, w_in[layer], name=tag + "_dh", out_dtype=F32, b_transposed=True)
        d_w_out = _mm_tn(act, dz_b, name=tag + "_dw_out")
        d_w_in = _glu_deinterleave(_mm_tn(h_in_b, du, name=tag + "_dw_in"))
        return dh, d_w_in, d_w_out, _glu_deinterleave(dcw), _glu_deinterleave(dcb)[0]

    dz4, dz4b, dg11, db11 = _ln_bwd(dy, z4, g_ln[1, 1][None, :], name="ln4_bwd")
    dh3, dwin1, dwout1, dcw1, dcb1 = ffn_bwd(1, dz4b, u1, act1, h3b, "ffn1")
    dz3, dz3b, dg10, db10 = _ln_bwd(dh3, z3, g_ln[1, 0][None, :], name="ln3_bwd", dz_next=dz4)

    do = _mm(dz3b, wo, name="attn_do", out_dtype=BF16, b_transposed=True)
    grads["w_o"] = _mm_tn(o, dz3b, name="attn_dw_o")[None]
    doT_h = cols_first(do.reshape(M, H, dh))
    dqT_h, dkT_h, dvT_h, dcq, dck = _attn_bwd(qT_h, k_h, v_h, kT_h, oT_h, doT_h, lse, cq, ck, name="attn_bwd")
    dc = jnp.pad((dcq[:, 0, :] + dck[:, :, 0]).T, ((0, 0), (0, LANES - H)))
    dpre, dbias = _forget_cumsum_bwd(dc, pre, bias_f, name="attn_cumsum_bwd")
    from_blocks = lambda t: jnp.transpose(t, (1, 3, 0, 2)).reshape(M, D).astype(BF16)
    d_att = jnp.concatenate([from_blocks(dkT_h), from_blocks(dvT_h),
                             jnp.transpose(dqT_h, (2, 0, 1)).reshape(M, D), dpre], axis=1)
    dh2 = _mm(d_att, w_att, name="attn_dh", out_dtype=F32, b_transposed=True)
    d_w_att = _mm_tn(h2b, d_att, name="attn_dw_qkv")
    grads["w_kv"] = d_w_att[:, :2 * D]
    grads["w_q"] = d_w_att[:, 2 * D:3 * D][None]
    grads["w_f"] = d_w_att[:, 3 * D:3 * D + H]
    grads["b_f"] = dbias[0, :H]
    dz2, dz2b, dg01, db01 = _ln_bwd(dh2, z2, g_ln[0, 1][None, :], name="ln2_bwd", dz_next=dz3)

    dh1, dwin0, dwout0, dcw0, dcb0 = ffn_bwd(0, dz2b, u0, act0, h1b, "ffn0")
    dz1, _, dg00, db00 = _ln_bwd(dh1, z1, g_ln[0, 0][None, :], name="ln1_bwd", dz_next=dz2)
    dh0, dpw, dscale = _pool_bwd(dz1, diff, pw, scale, name="pool_bwd")

    grads["meta"] = dh0[:N_META]
    grads["pool_w"] = dpw[None]
    grads["pool_scale"] = dscale
    grads["ffn_w_in"] = jnp.stack([dwin0, dwin1])
    grads["ffn_w_out"] = jnp.stack([dwout0, dwout1])
    grads["ffn_conv_w"] = jnp.stack([dcw0, dcw1])
    grads["ffn_conv_b"] = jnp.stack([dcb0, dcb1])
    grads["ln_g"] = jnp.stack([jnp.stack([dg00[0], dg01[0]]), jnp.stack([dg10[0], dg11[0]])])
    grads["ln_b"] = jnp.stack([jnp.stack([db00[0], db01[0]]), jnp.stack([db10[0], db11[0]])])
    grad_x = dh0[N_META:n_tok][None]

    blocks = {n: _to_blocks(grads[n], axis_of[n]) for n, _ in PARAMS}
    def by_core(t):
        t = jnp.swapaxes(t.reshape((N_CHIPS, 2) + t.shape[1:]), 0, 1).astype(BF16)
        lead = t.shape[2] if t.shape[2] <= PAIR_PIECE_AXIS_MAX else 1
        return t.reshape((2, N_CHIPS * lead) + t.shape[2 + (lead > 1):])

    segs = [by_core(blocks[n]) for n in BIG]
    pair = _exchange(segs, "grad_pair_exchange", (1,), lambda pos: pos[2], pieces=[s.shape[1] for s in segs])
    partial = [_pair_sum(p.reshape(2, -1, p.shape[-1]), name="grad_pair_sum_" + n).reshape((N_CHIPS,) + local[n].shape)
               for n, p in zip(BIG, pair)]
    chip_of = lambda pos: 2 * pos[0] + pos[1]
    recv = _exchange(partial, "grad_chip_exchange", (4, 2, 6), chip_of)
    recv_small = _exchange([_pack([blocks[n] for n in SMALL], lead=(N_DEV,))], "grad_small_exchange",
                           tuple(range(1, N_DEV)), _index)[0]
    results = {}
    for n, r in zip(BIG, recv):
        shape = local[n].shape
        as2d = lambda t: t.reshape(-1, shape[-1])
        outs = _adamw(r.reshape(N_CHIPS, -1, shape[-1]), as2d(local[n]), as2d(mom1[n]), as2d(mom2[n]), name="adamw_" + n)
        results[n] = [o_.reshape(shape) for o_ in outs]
    outs = _adamw(recv_small, *[_pack([d[n] for n in SMALL]) for d in (local, mom1, mom2)], name="adamw_small")
    small_out = [_unpack(o_, [local[n].shape for n in SMALL]) for o_ in outs]
    for i, n in enumerate(SMALL):
        results[n] = [small_out[k][i] for k in range(4)]
    return (loss, grad_x, *[results[n][k] for k in range(4) for n, _ in PARAMS])
```

```python
import jax
import jax.numpy as jnp
from jax import lax
from jax.experimental import pallas as pl
from jax.experimental.pallas import tpu as pltpu

F32, BF16 = jnp.float32, jnp.bfloat16
MESH = pl.DeviceIdType.MESH

N_DEV = 8
N_CHIPS = 4
N_META = 16
POOL_WINDOWS = (2, 4, 8, 16)
POOL_HALO = 16
CONV_HALO = 8
CONV_STRIP = 16
ALPHA = 4.0 ** 0.25
LN_EPS = 1e-5
NEG_INF = -1e30
ADAM_LR, ADAM_B1, ADAM_B2, ADAM_EPS, ADAM_WD, ADAM_STEP = 0.001, 0.9, 0.999, 1e-08, 0.01, 10

LANES = 128
PACK_COLS = 1024
ADAM_TILE_BYTES = 4 << 20
GLU_CHUNK = 256
VMEM_LIMIT = 56 * 1024 * 1024

PARAMS = (("meta", 1), ("pool_w", 2), ("pool_scale", 1), ("w_kv", 1), ("w_f", 0), ("b_f", None),
          ("w_q", 1), ("w_o", 1), ("ffn_w_in", 2), ("ffn_conv_w", 2), ("ffn_conv_b", None),
          ("ffn_w_out", 1), ("ln_g", 2), ("ln_b", 2))
BIG = ("pool_w", "w_kv", "w_q", "w_o", "ffn_w_in", "ffn_w_out")
SMALL = ("meta", "pool_scale", "w_f", "b_f", "ffn_conv_w", "ffn_conv_b", "ln_g", "ln_b")


def _cparams(**kw):
    return pltpu.CompilerParams(vmem_limit_bytes=VMEM_LIMIT, **kw)


def _pick(n, cands):
    for c in cands:
        if n % c == 0:
            return c
    return n


def _round_up(n, m):
    return (n + m - 1) // m * m


def _pack(pieces, lead=()):
    flat = []
    for p in pieces:
        v = p.reshape(lead + (-1,))
        flat.append(jnp.pad(v, [(0, 0)] * len(lead) + [(0, _round_up(v.shape[-1], PACK_COLS) - v.shape[-1])]))
    v = jnp.concatenate(flat, axis=-1)
    rows = _round_up(v.shape[-1] // PACK_COLS, 8)
    v = jnp.pad(v, [(0, 0)] * len(lead) + [(0, rows * PACK_COLS - v.shape[-1])])
    return v.reshape(lead + (rows, PACK_COLS))


def _unpack(buf, shapes, lead=()):
    flat = buf.reshape(lead + (-1,))
    out, off = [], 0
    for s in shapes:
        n = 1
        for d in s:
            n *= d
        out.append(flat[..., off:off + n].reshape(lead + tuple(s)))
        off += _round_up(n, PACK_COLS)
    return out


def _to_blocks(full, axis):
    if axis is None:
        return jnp.broadcast_to(full[None], (N_DEV,) + full.shape)
    s = full.shape
    x = full.reshape(s[:axis] + (N_DEV, s[axis] // N_DEV) + s[axis + 1:])
    return jnp.moveaxis(x, axis, 0)


def _from_blocks(blocks, axis):
    x = jnp.moveaxis(blocks, 0, axis)
    s = x.shape
    return x.reshape(s[:axis] + (s[axis] * s[axis + 1],) + s[axis + 2:])


def _coords():
    return lax.axis_index("x"), lax.axis_index("y"), lax.axis_index("c")


def _flip(pos, mask):
    x, y, c = pos
    return (1 - x if mask & 4 else x, 1 - y if mask & 2 else y, 1 - c if mask & 1 else c)


def _index(pos):
    x, y, c = pos
    return 4 * x + 2 * y + c


def _comm_call(body, name, arrays, out_shapes):
    n = len(arrays)
    hbm = pl.BlockSpec(memory_space=pl.ANY)
    return pl.pallas_call(
        body, name=name, out_shape=out_shapes, in_specs=[hbm] * n, out_specs=[hbm] * n,
        scratch_shapes=[pltpu.SemaphoreType.DMA((7 * n,)), pltpu.SemaphoreType.DMA((7 * n,)),
                        pltpu.SemaphoreType.DMA((n,))],
    )(*arrays)


def _all_gather(blocks, name):
    chip_masks = (4, 2, 6)
    n = len(blocks)

    def body(*refs):
        x_refs, out_refs = refs[:n], refs[n:2 * n]
        send_sems, recv_sems, local_sems = refs[2 * n:]
        me = _coords()
        sibling = _flip(me, 1)

        def copy(a, k, owner, to, from_input=False):
            slot = out_refs[a].at[_index(owner)]
            return pltpu.make_async_remote_copy(
                src_ref=x_refs[a] if from_input else slot, dst_ref=slot,
                send_sem=send_sems.at[7 * a + k], recv_sem=recv_sems.at[7 * a + k],
                device_id=to, device_id_type=MESH)

        mine = [pltpu.make_async_copy(x_refs[a], out_refs[a].at[_index(me)], local_sems.at[a]) for a in range(n)]
        first = [copy(a, 0, me, sibling, True) for a in range(n)]
        first += [copy(a, 1 + j, me, _flip(me, m), True) for j, m in enumerate(chip_masks) for a in range(n)]
        for cp in mine + first:
            cp.start()
        passed = []
        for j, m in enumerate(chip_masks):
            for a in range(n):
                copy(a, 1 + j, _flip(me, m), me).wait_recv()
                passed.append(copy(a, 4 + j, _flip(me, m), sibling))
                passed[-1].start()
        for a in range(n):
            copy(a, 0, sibling, me).wait_recv()
            for j, m in enumerate(chip_masks):
                copy(a, 4 + j, _flip(sibling, m), me).wait_recv()
        for cp in first + passed:
            cp.wait_send()
        for cp in mine:
            cp.wait()

    return _comm_call(body, name, blocks, [jax.ShapeDtypeStruct((N_DEV,) + b.shape, b.dtype) for b in blocks])


def _exchange(segs, name, masks, slot_of, by_sender=False):
    n = len(segs)

    def body(*refs):
        seg_refs, out_refs = refs[:n], refs[n:2 * n]
        send_sems, recv_sems, local_sems = refs[2 * n:]
        me = _coords()

        def copy(a, k, sender):
            to = _flip(sender, masks[k])
            return pltpu.make_async_remote_copy(
                src_ref=seg_refs[a].at[slot_of(to)], dst_ref=out_refs[a].at[slot_of(sender) if by_sender else k],
                send_sem=send_sems.at[7 * a + k], recv_sem=recv_sems.at[7 * a + k],
                device_id=to, device_id_type=MESH)

        mine = [pltpu.make_async_copy(seg_refs[a].at[slot_of(me)], out_refs[a].at[slot_of(me)], local_sems.at[a])
                for a in range(n)] if by_sender else []
        sends = [copy(a, k, me) for k in range(len(masks)) for a in range(n)]
        for cp in mine + sends:
            cp.start()
        for k, mask in enumerate(masks):
            for a in range(n):
                copy(a, k, _flip(me, mask)).wait_recv()
        for cp in sends:
            cp.wait_send()
        for cp in mine:
            cp.wait()

    slots = lambda s: s.shape[0] if by_sender else len(masks)
    return _comm_call(body, name, segs, [jax.ShapeDtypeStruct((slots(s),) + s.shape[1:], s.dtype) for s in segs])


def _own_slot(tr, cols):
    return pl.BlockSpec((1, tr, cols), lambda i, slot: (slot[0], i, 0))


def _pair_sum(seg, slot, recv, name):
    _, rows, cols = seg.shape
    tr = max(t for t in range(16, rows + 1, 16) if rows % t == 0 and 2 * t * cols * 2 <= ADAM_TILE_BYTES)

    def body(slot_ref, own_ref, recv_ref, o_ref):
        o_ref[...] = (own_ref[0].astype(F32) + recv_ref[...].astype(F32)).astype(BF16)

    tile = pl.BlockSpec((tr, cols), lambda i, slot: (i, 0))
    return pl.pallas_call(
        body, name=name,
        grid_spec=pltpu.PrefetchScalarGridSpec(
            num_scalar_prefetch=1, grid=(rows // tr,), in_specs=[_own_slot(tr, cols), tile], out_specs=tile),
        out_shape=jax.ShapeDtypeStruct((rows, cols), BF16),
        compiler_params=_cparams(),
    )(slot, seg, recv)


def _adamw(seg, slot, recv, w, m, v, name):
    rows, cols = w.shape
    n_slots = recv.shape[0]
    sublanes = 8 * (4 // recv.dtype.itemsize)
    tr = max(t for t in range(sublanes, rows + 1, sublanes)
             if rows % t == 0 and N_DEV * t * cols * 4 <= ADAM_TILE_BYTES)
    c1 = 1.0 - ADAM_B1 ** ADAM_STEP
    c2 = 1.0 - ADAM_B2 ** ADAM_STEP

    def body(slot_ref, own_ref, r_ref, w_ref, m_ref, v_ref, g_out, d_out, m_out, v_out):
        g = own_ref[0].astype(F32)
        for s in range(n_slots):
            g = g + r_ref[s].astype(F32)
        m_new = ADAM_B1 * m_ref[...] + (1.0 - ADAM_B1) * g
        v_new = ADAM_B2 * v_ref[...] + (1.0 - ADAM_B2) * (g * g)
        m_hat = m_new / c1
        v_hat = v_new / c2
        g_out[...] = g
        d_out[...] = -ADAM_LR * (m_hat / (jnp.sqrt(v_hat) + ADAM_EPS) + ADAM_WD * w_ref[...])
        m_out[...] = m_new
        v_out[...] = v_new

    tile = pl.BlockSpec((tr, cols), lambda i, slot: (i, 0))
    return pl.pallas_call(
        body, name=name,
        grid_spec=pltpu.PrefetchScalarGridSpec(
            num_scalar_prefetch=1, grid=(rows // tr,),
            in_specs=[_own_slot(tr, cols), pl.BlockSpec((n_slots, tr, cols), lambda i, slot: (0, i, 0)),
                      tile, tile, tile],
            out_specs=[tile] * 4),
        out_shape=[jax.ShapeDtypeStruct(w.shape, F32)] * 4,
        compiler_params=_cparams(),
    )(slot, seg, recv, w, m, v)


def _mm(a, b, *, name, out_dtype, b_transposed=False):
    M, K = a.shape
    N = b.shape[0] if b_transposed else b.shape[1]
    tm = _pick(M, (640, 128))
    tn = _pick(N, (1408, 1024, 640, 512, 256, 128))
    tk = K if K <= 1024 else _pick(K, (1408, 1024, 640, 512, 256, 128))
    nk = K // tk

    def body(*refs):
        a_ref, b_ref, o_ref = refs[:3]
        acc_ref = refs[-1] if nk > 1 else None
        if b_transposed:
            prod = _dot_nt(a_ref[...], b_ref[...])
        else:
            prod = jnp.dot(a_ref[...], b_ref[...], preferred_element_type=F32)
        if nk == 1:
            o_ref[...] = prod.astype(out_dtype)
        else:
            k = pl.program_id(2)

            @pl.when(k == 0)
            def _():
                acc_ref[...] = prod

            @pl.when(k > 0)
            def _():
                acc_ref[...] += prod

            @pl.when(k == nk - 1)
            def _():
                o_ref[...] = acc_ref[...].astype(out_dtype)

    b_spec = (pl.BlockSpec((tn, tk), lambda i, j, k: (j, k)) if b_transposed
              else pl.BlockSpec((tk, tn), lambda i, j, k: (k, j)))
    return pl.pallas_call(
        body, name=name, grid=(M // tm, N // tn, nk),
        in_specs=[pl.BlockSpec((tm, tk), lambda i, j, k: (i, k)), b_spec],
        out_specs=pl.BlockSpec((tm, tn), lambda i, j, k: (i, j)),
        out_shape=jax.ShapeDtypeStruct((M, N), out_dtype),
        scratch_shapes=[pltpu.VMEM((tm, tn), F32)] if nk > 1 else [],
        compiler_params=_cparams(),
    )(a, b)


def _mm_tn(a, b, *, name):
    T, M = a.shape
    N = b.shape[1]
    tt = _pick(T, (1664, 640, 128))
    tm = _pick(M, (1408, 1024, 512, 256, 128))
    tn = _pick(N, (1024, 640, 512, 256, 128))

    def body(a_ref, b_ref, o_ref):
        prod = lax.dot_general(a_ref[...], b_ref[...], (((0,), (0,)), ((), ())), preferred_element_type=F32)

        @pl.when(pl.program_id(2) == 0)
        def _():
            o_ref[...] = prod

        @pl.when(pl.program_id(2) > 0)
        def _():
            o_ref[...] += prod

    return pl.pallas_call(
        body, name=name, grid=(M // tm, N // tn, T // tt),
        in_specs=[pl.BlockSpec((tt, tm), lambda i, j, k: (k, i)), pl.BlockSpec((tt, tn), lambda i, j, k: (k, j))],
        out_specs=pl.BlockSpec((tm, tn), lambda i, j, k: (i, j)),
        out_shape=jax.ShapeDtypeStruct((M, N), F32),
        compiler_params=_cparams(),
    )(a, b)


def _layer_norm(z, g, b):
    mu = jnp.mean(z, axis=-1, keepdims=True)
    xc = z - mu
    var = jnp.mean(xc * xc, axis=-1, keepdims=True)
    return xc * lax.rsqrt(var + LN_EPS) * g + b


def _mm_ln(a, w, res, g, b, *, name, target=None, n_tok=None):
    M, K = a.shape
    D = w.shape[1]
    tm = _pick(M, (640, 128))
    final = target is not None

    def body(*refs):
        if final:
            a_ref, w_ref, r_ref, g_ref, b_ref, t_ref, z_ref, dy_ref, loss_ref = refs
        else:
            a_ref, w_ref, r_ref, g_ref, b_ref, z_ref, h_ref, hb_ref = refs
        z = ALPHA * r_ref[...] + jnp.dot(a_ref[...], w_ref[...], preferred_element_type=F32)
        z_ref[...] = z
        h = _layer_norm(z, g_ref[...], b_ref[...])
        if not final:
            h_ref[...] = h
            hb_ref[...] = h.astype(BF16)
            return
        i = pl.program_id(0)
        row = i * tm + lax.broadcasted_iota(jnp.int32, (tm, 1), 0)
        valid = (row >= N_META) & (row < n_tok)
        err = jnp.where(valid, h - t_ref[...], 0.0)
        dy_ref[...] = err / D

        @pl.when(i == 0)
        def _():
            loss_ref[...] = jnp.zeros_like(loss_ref)

        loss_ref[...] += 0.5 * jnp.sum(jnp.sum(err * err, axis=1, keepdims=True) / D, axis=0, keepdims=True)

    row_blk = lambda cols: pl.BlockSpec((tm, cols), lambda i: (i, 0))
    vec = pl.BlockSpec((1, D), lambda i: (0, 0))
    in_specs = [row_blk(K), pl.BlockSpec((K, D), lambda i: (0, 0)), row_blk(D), vec, vec]
    args = [a, w, res, g, b]
    if final:
        in_specs.append(row_blk(D))
        args.append(target)
        out_specs = [row_blk(D), row_blk(D), pl.BlockSpec((8, LANES), lambda i: (0, 0))]
        out_shape = [jax.ShapeDtypeStruct((M, D), F32)] * 2 + [jax.ShapeDtypeStruct((8, LANES), F32)]
    else:
        out_specs = [row_blk(D)] * 3
        out_shape = [jax.ShapeDtypeStruct((M, D), F32)] * 2 + [jax.ShapeDtypeStruct((M, D), BF16)]
    return pl.pallas_call(
        body, name=name, grid=(M // tm,), in_specs=in_specs, out_specs=out_specs, out_shape=out_shape,
        compiler_params=_cparams(),
    )(*args)


def _ln_bwd(dh, z, g, *, name, dz_next=None):
    M, D = z.shape
    tm = _pick(M, (640, 128))
    has_next = dz_next is not None

    def body(*refs):
        if has_next:
            dh_ref, nx_ref, z_ref, g_ref, dz_ref, dzb_ref, dg_ref, db_ref = refs
            dh_v = dh_ref[...] + ALPHA * nx_ref[...]
        else:
            dh_ref, z_ref, g_ref, dz_ref, dzb_ref, dg_ref, db_ref = refs
            dh_v = dh_ref[...]
        z_v = z_ref[...]
        mu = jnp.mean(z_v, axis=-1, keepdims=True)
        xc = z_v - mu
        rstd = lax.rsqrt(jnp.mean(xc * xc, axis=-1, keepdims=True) + LN_EPS)
        xhat = xc * rstd
        dxhat = dh_v * g_ref[...]
        dz = rstd * (dxhat - jnp.mean(dxhat, axis=-1, keepdims=True)
                     - xhat * jnp.mean(dxhat * xhat, axis=-1, keepdims=True))
        dz_ref[...] = dz
        dzb_ref[...] = dz.astype(BF16)

        @pl.when(pl.program_id(0) == 0)
        def _():
            dg_ref[...] = jnp.zeros_like(dg_ref)
            db_ref[...] = jnp.zeros_like(db_ref)

        dg_ref[...] += jnp.sum(dh_v * xhat, axis=0, keepdims=True)
        db_ref[...] += jnp.sum(dh_v, axis=0, keepdims=True)

    row_blk = pl.BlockSpec((tm, D), lambda i: (i, 0))
    vec = pl.BlockSpec((1, D), lambda i: (0, 0))
    args = [dh] + ([dz_next] if has_next else []) + [z, g]
    in_specs = [row_blk] * (len(args) - 1) + [vec]
    return pl.pallas_call(
        body, name=name, grid=(M // tm,), in_specs=in_specs,
        out_specs=[row_blk, row_blk, vec, vec],
        out_shape=[jax.ShapeDtypeStruct((M, D), F32), jax.ShapeDtypeStruct((M, D), BF16),
                   jax.ShapeDtypeStruct((1, D), F32), jax.ShapeDtypeStruct((1, D), F32)],
        compiler_params=_cparams(),
    )(*args)


def _pool_fwd(h0, pw, scale, g, b, *, name):
    M, D = h0.shape
    n_groups, G, _ = pw.shape
    tm = _pick(M, (640, 128))

    def body(x_ref, halo_ref, pw_ref, sc_ref, g_ref, b_ref, z_ref, h_ref, hb_ref, diff_ref, ext_ref, mix_ref):
        i = pl.program_id(0)
        x = x_ref[...]
        ext_ref[0:POOL_HALO, :] = jnp.where(i == 0, 0.0, halo_ref[...])
        ext_ref[POOL_HALO:, :] = x
        tok = i * tm + lax.broadcasted_iota(jnp.int32, (tm, 1), 0)
        for gi, win in enumerate(POOL_WINDOWS):
            cols = slice(gi * G, (gi + 1) * G)
            xs = x[:, cols]
            s = xs
            for k in range(1, win):
                s = s + ext_ref[pl.ds(POOL_HALO - k, tm), cols]
            count = jnp.minimum(tok + 1, win).astype(F32)
            d = (s / count - xs).astype(BF16)
            diff_ref[:, cols] = d
            mix_ref[:, cols] = jnp.dot(d, pw_ref[gi], preferred_element_type=F32)
        z = ALPHA * x + mix_ref[...] * sc_ref[...]
        z_ref[...] = z
        h = _layer_norm(z, g_ref[...], b_ref[...])
        h_ref[...] = h
        hb_ref[...] = h.astype(BF16)

    row_blk = pl.BlockSpec((tm, D), lambda i: (i, 0))
    vec = pl.BlockSpec((1, D), lambda i: (0, 0))
    halo = pl.BlockSpec((POOL_HALO, D), lambda i: (jnp.maximum(i * (tm // POOL_HALO) - 1, 0), 0))
    return pl.pallas_call(
        body, name=name, grid=(M // tm,),
        in_specs=[row_blk, halo, pl.BlockSpec((n_groups, G, G), lambda i: (0, 0, 0)), vec, vec, vec],
        out_specs=[row_blk] * 4,
        out_shape=[jax.ShapeDtypeStruct((M, D), F32)] * 2 + [jax.ShapeDtypeStruct((M, D), BF16)] * 2,
        scratch_shapes=[pltpu.VMEM((tm + POOL_HALO, D), F32), pltpu.VMEM((tm, D), F32)],
        compiler_params=_cparams(),
    )(h0, h0, pw, scale, g, b)


def _pool_bwd(dz, diff, pw, scale, *, name):
    M, D = dz.shape
    n_groups, G, _ = pw.shape
    tm = _pick(M, (640, 128))
    nt = M // tm

    def body(dz_ref, halo_ref, diff_ref, pw_ref, sc_ref, dh_ref, dpw_ref, dsc_ref, ext_ref, q_ref):
        i = pl.program_id(0)
        dz_v = dz_ref[...]
        ext_ref[0:tm, :] = dz_v
        ext_ref[tm:, :] = jnp.where(i == nt - 1, 0.0, halo_ref[...])
        tok = i * tm + lax.broadcasted_iota(jnp.int32, (tm + POOL_HALO, 1), 0)

        @pl.when(i == 0)
        def _():
            dpw_ref[...] = jnp.zeros_like(dpw_ref)
            dsc_ref[...] = jnp.zeros_like(dsc_ref)

        for gi, win in enumerate(POOL_WINDOWS):
            cols = slice(gi * G, (gi + 1) * G)
            dmix = (ext_ref[:, cols] * sc_ref[:, cols]).astype(BF16)
            ddiff = _dot_nt(dmix, pw_ref[gi])
            count = jnp.minimum(tok + 1, win).astype(F32)
            q_ref[:, cols] = ddiff / count
            acc = -ddiff[0:tm]
            for k in range(win):
                acc = acc + q_ref[pl.ds(k, tm), cols]
            dh_ref[:, cols] = ALPHA * dz_v[:, cols] + acc
            d = diff_ref[:, cols]
            dpw_ref[gi] += lax.dot_general(d, dmix[0:tm], (((0,), (0,)), ((), ())), preferred_element_type=F32)
            mixed = jnp.dot(d, pw_ref[gi], preferred_element_type=F32)
            dsc_ref[:, cols] += jnp.sum(dz_v[:, cols] * mixed, axis=0, keepdims=True)

    row_blk = pl.BlockSpec((tm, D), lambda i: (i, 0))
    vec = pl.BlockSpec((1, D), lambda i: (0, 0))
    per_tile = tm // POOL_HALO
    halo = pl.BlockSpec((POOL_HALO, D), lambda i: (jnp.minimum((i + 1) * per_tile, nt * per_tile - 1), 0))
    wblk = pl.BlockSpec((n_groups, G, G), lambda i: (0, 0, 0))
    return pl.pallas_call(
        body, name=name, grid=(nt,),
        in_specs=[row_blk, halo, row_blk, wblk, vec],
        out_specs=[row_blk, wblk, vec],
        out_shape=[jax.ShapeDtypeStruct((M, D), F32), jax.ShapeDtypeStruct((n_groups, G, G), F32),
                   jax.ShapeDtypeStruct((1, D), F32)],
        scratch_shapes=[pltpu.VMEM((tm + POOL_HALO, D), F32), pltpu.VMEM((tm + POOL_HALO, D), F32)],
        compiler_params=_cparams(),
    )(dz, dz, diff, pw, scale)


def _glu_interleave(x):
    s = x.shape
    n = s[-1] // (2 * GLU_CHUNK)
    return jnp.swapaxes(x.reshape(s[:-1] + (2, n, GLU_CHUNK)), -3, -2).reshape(s)


def _glu_deinterleave(x):
    s = x.shape
    n = s[-1] // (2 * GLU_CHUNK)
    return jnp.swapaxes(x.reshape(s[:-1] + (n, 2, GLU_CHUNK)), -3, -2).reshape(s)


def _taps(u_ref, head_ref, r0):
    src, base = (head_ref, CONV_HALO) if r0 == 0 else (u_ref, r0)
    return tuple(src[pl.ds(base - k, CONV_STRIP), :] for k in range(3))


def _conv_glu_fwd(u, cw, cb, *, name):
    M, F2 = u.shape
    tm = _pick(M, (640, 128))
    tc = 2 * GLU_CHUNK

    def body(u_ref, halo_ref, w_ref, b_ref, o_ref, head_ref):
        i = pl.program_id(0)
        head_ref[0:CONV_HALO, :] = jnp.where(i == 0, 0.0, halo_ref[...])
        head_ref[CONV_HALO:, :] = u_ref[0:CONV_STRIP, :]
        w0, w1, w2, b = w_ref[0:1, :], w_ref[1:2, :], w_ref[2:3, :], b_ref[...]
        for r0 in range(0, tm, CONV_STRIP):
            u0, u1, u2 = _taps(u_ref, head_ref, r0)
            c = b + w0 * u2 + w1 * u1 + w2 * u0
            a, g = c[:, :GLU_CHUNK], c[:, GLU_CHUNK:]
            o_ref[pl.ds(r0, CONV_STRIP), :] = (a * jax.nn.sigmoid(a) * g).astype(BF16)

    per_tile = tm // CONV_HALO
    return pl.pallas_call(
        body, name=name, grid=(M // tm, F2 // tc),
        in_specs=[pl.BlockSpec((tm, tc), lambda i, j: (i, j)),
                  pl.BlockSpec((CONV_HALO, tc), lambda i, j: (jnp.maximum(i * per_tile - 1, 0), j)),
                  pl.BlockSpec((3, tc), lambda i, j: (0, j)), pl.BlockSpec((1, tc), lambda i, j: (0, j))],
        out_specs=pl.BlockSpec((tm, GLU_CHUNK), lambda i, j: (i, j)),
        out_shape=jax.ShapeDtypeStruct((M, F2 // 2), BF16),
        scratch_shapes=[pltpu.VMEM((CONV_HALO + CONV_STRIP, tc), F32)],
        compiler_params=_cparams(),
    )(u, u, cw, cb)


def _conv_glu_bwd(u, dact, cw, cb, *, name):
    M, F2 = u.shape
    tm = _pick(M, (640, 128))
    nt = M // tm
    tc = 2 * GLU_CHUNK

    def body(u_ref, halo_ref, da_ref, w_ref, b_ref, du_ref, dw_ref, db_ref, head_ref, dcx_ref, carry_ref):
        i = pl.program_id(1)
        head_ref[0:CONV_HALO, :] = jnp.where(i == nt - 1, 0.0, halo_ref[...])
        head_ref[CONV_HALO:, :] = u_ref[0:CONV_STRIP, :]
        w0, w1, w2, b = w_ref[0:1, :], w_ref[1:2, :], w_ref[2:3, :], b_ref[...]

        @pl.when(i == 0)
        def _():
            dw_ref[...] = jnp.zeros_like(dw_ref)
            db_ref[...] = jnp.zeros_like(db_ref)
            carry_ref[...] = jnp.zeros_like(carry_ref)

        def fold(t):
            return sum(t[r:r + 8] for r in range(0, CONV_STRIP, 8))

        dcx_ref[tm:, :] = carry_ref[...]
        s_b = s_0 = s_1 = s_2 = jnp.zeros((8, tc), F32)
        for r0 in reversed(range(0, tm, CONV_STRIP)):
            rows = pl.ds(r0, CONV_STRIP)
            u0, u1, u2 = _taps(u_ref, head_ref, r0)
            c = b + w0 * u2 + w1 * u1 + w2 * u0
            a, g = c[:, :GLU_CHUNK], c[:, GLU_CHUNK:]
            sig = jax.nn.sigmoid(a)
            dact_v = da_ref[rows, :]
            d_a = dact_v * g * (sig * (1.0 + a * (1.0 - sig)))
            d_g = dact_v * (a * sig)
            dc = jnp.concatenate([d_a, d_g], axis=1)
            dcx_ref[rows, :] = dc
            du = w2 * dc + w1 * dcx_ref[pl.ds(r0 + 1, CONV_STRIP), :] + w0 * dcx_ref[pl.ds(r0 + 2, CONV_STRIP), :]
            du_ref[rows, :] = du.astype(BF16)
            s_b, s_0, s_1, s_2 = s_b + fold(dc), s_0 + fold(dc * u2), s_1 + fold(dc * u1), s_2 + fold(dc * u0)
        carry_ref[...] = dcx_ref[0:CONV_HALO, :]
        db_ref[...] += jnp.sum(s_b, axis=0, keepdims=True)
        dw_ref[0:1, :] += jnp.sum(s_0, axis=0, keepdims=True)
        dw_ref[1:2, :] += jnp.sum(s_1, axis=0, keepdims=True)
        dw_ref[2:3, :] += jnp.sum(s_2, axis=0, keepdims=True)

    per_tile = tm // CONV_HALO
    rev = lambda i: nt - 1 - i
    return pl.pallas_call(
        body, name=name, grid=(F2 // tc, nt),
        in_specs=[pl.BlockSpec((tm, tc), lambda j, i: (rev(i), j)),
                  pl.BlockSpec((CONV_HALO, tc), lambda j, i: (jnp.maximum(rev(i) * per_tile - 1, 0), j)),
                  pl.BlockSpec((tm, GLU_CHUNK), lambda j, i: (rev(i), j)),
                  pl.BlockSpec((3, tc), lambda j, i: (0, j)), pl.BlockSpec((1, tc), lambda j, i: (0, j))],
        out_specs=[pl.BlockSpec((tm, tc), lambda j, i: (rev(i), j)),
                   pl.BlockSpec((3, tc), lambda j, i: (0, j)), pl.BlockSpec((1, tc), lambda j, i: (0, j))],
        out_shape=[jax.ShapeDtypeStruct((M, F2), BF16), jax.ShapeDtypeStruct((3, F2), F32),
                   jax.ShapeDtypeStruct((1, F2), F32)],
        scratch_shapes=[pltpu.VMEM((CONV_HALO + CONV_STRIP, tc), F32), pltpu.VMEM((tm + CONV_HALO, tc), F32),
                        pltpu.VMEM((CONV_HALO, tc), F32)],
        compiler_params=_cparams(),
    )(u, u, dact, cw, cb)


def _split3(x):
    hi = x.astype(BF16)
    r = x - hi.astype(F32)
    mid = r.astype(BF16)
    lo = (r - mid.astype(F32)).astype(BF16)
    return hi, mid, lo


def _tri_sum(tri, x):
    return sum(jnp.dot(tri, part, preferred_element_type=F32) for part in _split3(x))


def _log_sigmoid(x):
    return jnp.minimum(x, 0.0) - jnp.log1p(jnp.exp(-jnp.abs(x)))


def _forget_cumsum(pre, bias, *, name):
    M, C = pre.shape
    tm = _pick(M, (640, 128))

    def body(p_ref, b_ref, c_ref, carry_ref):
        i = pl.program_id(0)

        @pl.when(i == 0)
        def _():
            carry_ref[...] = jnp.zeros_like(carry_ref)

        logf = _log_sigmoid(p_ref[...] + b_ref[...])
        r = lax.broadcasted_iota(jnp.int32, (tm, tm), 0)
        s = lax.broadcasted_iota(jnp.int32, (tm, tm), 1)
        c_ref[...] = _tri_sum((s <= r).astype(BF16), logf) + carry_ref[...]
        carry_ref[...] = c_ref[pl.ds(tm - 1, 1), :]

    return pl.pallas_call(
        body, name=name, grid=(M // tm,),
        in_specs=[pl.BlockSpec((tm, C), lambda i: (i, 0)), pl.BlockSpec((1, C), lambda i: (0, 0))],
        out_specs=pl.BlockSpec((tm, C), lambda i: (i, 0)),
        out_shape=jax.ShapeDtypeStruct((M, C), F32),
        scratch_shapes=[pltpu.VMEM((1, C), F32)],
        compiler_params=_cparams(),
    )(pre, bias)


def _forget_cumsum_bwd(dc, pre, bias, *, name):
    M, C = pre.shape
    tm = _pick(M, (640, 128))
    nt = M // tm

    def body(dc_ref, p_ref, b_ref, dp_ref, db_ref, carry_ref, run_ref):
        i = pl.program_id(0)

        @pl.when(i == 0)
        def _():
            carry_ref[...] = jnp.zeros_like(carry_ref)
            db_ref[...] = jnp.zeros_like(db_ref)

        r = lax.broadcasted_iota(jnp.int32, (tm, tm), 0)
        s = lax.broadcasted_iota(jnp.int32, (tm, tm), 1)
        run_ref[...] = _tri_sum((s >= r).astype(BF16), dc_ref[...]) + carry_ref[...]
        carry_ref[...] = run_ref[pl.ds(0, 1), :]
        dpre = run_ref[...] * jax.nn.sigmoid(-(p_ref[...] + b_ref[...]))
        dp_ref[...] = dpre.astype(BF16)
        db_ref[...] += jnp.sum(dpre, axis=0, keepdims=True)

    rev_blk = pl.BlockSpec((tm, C), lambda i: (nt - 1 - i, 0))
    vec = pl.BlockSpec((1, C), lambda i: (0, 0))
    return pl.pallas_call(
        body, name=name, grid=(nt,),
        in_specs=[rev_blk, rev_blk, vec], out_specs=[rev_blk, vec],
        out_shape=[jax.ShapeDtypeStruct((M, C), BF16), jax.ShapeDtypeStruct((1, C), F32)],
        scratch_shapes=[pltpu.VMEM((1, C), F32), pltpu.VMEM((tm, C), F32)],
        compiler_params=_cparams(),
    )(dc, pre, bias)


def _causal_mask(tm):
    key = lax.broadcasted_iota(jnp.int32, (tm, tm), 0)
    query = lax.broadcasted_iota(jnp.int32, (tm, tm), 1)
    return key <= query


def _dot_nt(a, b):
    return lax.dot_general(a, b, (((1,), (1,)), ((), ())), preferred_element_type=F32)


def _loop_in_pairs(n, step, init):
    carry = lax.fori_loop(0, n // 2, lambda p, c: step(2 * p + 1, step(2 * p, c)), init)
    return lax.fori_loop(2 * (n // 2), n, step, carry)


def _attn_specs(H, M, dh, tm):
    nt = M // tm
    qT_blk = pl.BlockSpec((1, dh, tm), lambda h, i: (h, 0, i))
    row_blk = pl.BlockSpec((1, 1, tm), lambda h, i: (h, 0, i))
    kv_blk = pl.BlockSpec((1, M, dh), lambda h, i: (h, 0, 0))
    kvT_blk = pl.BlockSpec((1, nt, dh, tm), lambda h, i: (h, 0, 0, 0))
    ck_blk = pl.BlockSpec((1, M, 1), lambda h, i: (h, 0, 0))
    return qT_blk, row_blk, kv_blk, kvT_blk, ck_blk


def _attn_fwd(qT, k, vT, cq, ck, *, name):
    H, M, dh = k.shape
    tm = vT.shape[-1]
    qT_blk, row_blk, kv_blk, kvT_blk, ck_blk = _attn_specs(H, M, dh, tm)

    def body(qT_ref, k_ref, vT_ref, cq_ref, ck_ref, oT_ref, lse_ref):
        i = pl.program_id(1)
        qsT = qT_ref[0] * jnp.asarray(dh ** -0.5, BF16)
        cq_v = cq_ref[0]

        def block(j, carry, masked):
            m, l, acc = carry
            keys = pl.ds(pl.multiple_of(j * tm, tm), tm)
            sT = jnp.dot(k_ref[0, keys, :], qsT, preferred_element_type=F32) + (cq_v - ck_ref[0, keys, :])
            if masked:
                sT = jnp.where(_causal_mask(tm), sT, NEG_INF)
            m_new = jnp.maximum(m, jnp.max(sT, axis=0, keepdims=True))
            a = jnp.exp(m - m_new)
            pT = jnp.exp(sT - m_new)
            l = a * l + jnp.sum(pT, axis=0, keepdims=True)
            acc = a * acc + jnp.dot(vT_ref[0, j], pT.astype(BF16), preferred_element_type=F32)
            return m_new, l, acc

        init = (jnp.full((1, tm), NEG_INF, F32), jnp.zeros((1, tm), F32), jnp.zeros((dh, tm), F32))
        carry = _loop_in_pairs(i, lambda j, c: block(j, c, False), init)
        m, l, acc = block(i, carry, True)
        oT_ref[0] = (acc / l).astype(BF16)
        lse_ref[0] = m + jnp.log(l)

    return pl.pallas_call(
        body, name=name, grid=(H, M // tm),
        in_specs=[qT_blk, kv_blk, kvT_blk, row_blk, ck_blk], out_specs=[qT_blk, row_blk],
        out_shape=[jax.ShapeDtypeStruct((H, dh, M), BF16), jax.ShapeDtypeStruct((H, 1, M), F32)],
        compiler_params=_cparams(),
    )(qT, k, vT, cq, ck)


def _attn_bwd(qT, k, v, kT, oT, doT, lse, cq, ck, *, name):
    H, M, dh = k.shape
    tm = kT.shape[-1]
    qT_blk, row_blk, kv_blk, kvT_blk, ck_blk = _attn_specs(H, M, dh, tm)

    def body(qT_ref, oT_ref, doT_ref, lse_ref, cq_ref, k_ref, v_ref, kT_ref, ck_ref,
             dqT_ref, dkT_ref, dvT_ref, dcq_ref, dck_ref):
        i = pl.program_id(1)

        @pl.when(i == 0)
        def _():
            dkT_ref[...] = jnp.zeros_like(dkT_ref)
            dvT_ref[...] = jnp.zeros_like(dvT_ref)
            dck_ref[...] = jnp.zeros_like(dck_ref)

        qsT = qT_ref[0] * jnp.asarray(dh ** -0.5, BF16)
        doT = doT_ref[0]
        delta = jnp.sum(doT.astype(F32) * oT_ref[0].astype(F32), axis=0, keepdims=True)
        shift = cq_ref[0] - lse_ref[0]

        def block(j, carry, masked):
            dq, dcq = carry
            keys = pl.ds(pl.multiple_of(j * tm, tm), tm)
            sT = jnp.dot(k_ref[0, keys, :], qsT, preferred_element_type=F32) + (shift - ck_ref[0, keys, :])
            pT = jnp.exp(sT)
            if masked:
                pT = jnp.where(_causal_mask(tm), pT, 0.0)
            dpT = jnp.dot(v_ref[0, keys, :], doT, preferred_element_type=F32)
            dsT = pT * (dpT - delta)
            ds_b = dsT.astype(BF16)
            dvT_ref[0, j] += _dot_nt(doT, pT.astype(BF16))
            dkT_ref[0, j] += _dot_nt(qsT, ds_b)
            dck_ref[0, keys, :] += -jnp.sum(dsT, axis=1, keepdims=True)
            return (dq + jnp.dot(kT_ref[0, j], ds_b, preferred_element_type=F32),
                    dcq + jnp.sum(dsT, axis=0, keepdims=True))

        init = (jnp.zeros((dh, tm), F32), jnp.zeros((1, tm), F32))
        dq, dcq = block(i, _loop_in_pairs(i, lambda j, c: block(j, c, False), init), True)
        dqT_ref[0] = (dq * dh ** -0.5).astype(BF16)
        dcq_ref[0] = dcq

    blocked = jax.ShapeDtypeStruct(kT.shape, F32)
    return pl.pallas_call(
        body, name=name, grid=(H, M // tm),
        in_specs=[qT_blk, qT_blk, qT_blk, row_blk, row_blk, kv_blk, kv_blk, kvT_blk, ck_blk],
        out_specs=[qT_blk, kvT_blk, kvT_blk, row_blk, ck_blk],
        out_shape=[jax.ShapeDtypeStruct((H, dh, M), BF16), blocked, blocked,
                   jax.ShapeDtypeStruct((H, 1, M), F32), jax.ShapeDtypeStruct((H, M, 1), F32)],
        compiler_params=_cparams(),
    )(qT, oT, doT, lse, cq, k, v, kT, ck)


def kernel(x, meta, pool_w, pool_scale, w_kv, w_f, b_f, w_q, w_o, ffn_w_in, ffn_conv_w, ffn_conv_b, ffn_w_out, ln_g, ln_b, loss_target, m_meta, m_pool_w, m_pool_scale, m_w_kv, m_w_f, m_b_f, m_w_q, m_w_o, m_ffn_w_in, m_ffn_conv_w, m_ffn_conv_b, m_ffn_w_out, m_ln_g, m_ln_b, v_meta, v_pool_w, v_pool_scale, v_w_kv, v_w_f, v_b_f, v_w_q, v_w_o, v_ffn_w_in, v_ffn_conv_w, v_ffn_conv_b, v_ffn_w_out, v_ln_g, v_ln_b):
    local = dict(meta=meta, pool_w=pool_w, pool_scale=pool_scale, w_kv=w_kv, w_f=w_f, b_f=b_f, w_q=w_q, w_o=w_o,
                 ffn_w_in=ffn_w_in, ffn_conv_w=ffn_conv_w, ffn_conv_b=ffn_conv_b, ffn_w_out=ffn_w_out,
                 ln_g=ln_g, ln_b=ln_b)
    mom1 = dict(meta=m_meta, pool_w=m_pool_w, pool_scale=m_pool_scale, w_kv=m_w_kv, w_f=m_w_f, b_f=m_b_f,
                w_q=m_w_q, w_o=m_w_o, ffn_w_in=m_ffn_w_in, ffn_conv_w=m_ffn_conv_w, ffn_conv_b=m_ffn_conv_b,
                ffn_w_out=m_ffn_w_out, ln_g=m_ln_g, ln_b=m_ln_b)
    mom2 = dict(meta=v_meta, pool_w=v_pool_w, pool_scale=v_pool_scale, w_kv=v_w_kv, w_f=v_w_f, b_f=v_b_f,
                w_q=v_w_q, w_o=v_w_o, ffn_w_in=v_ffn_w_in, ffn_conv_w=v_ffn_conv_w, ffn_conv_b=v_ffn_conv_b,
                ffn_w_out=v_ffn_w_out, ln_g=v_ln_g, ln_b=v_ln_b)
    axis_of = dict(PARAMS)

    S, D = x.shape[1], x.shape[2]
    H = b_f.shape[0]
    dh = D // H
    n_tok = N_META + S
    M = _round_up(n_tok, LANES)
    tm = _pick(M, (640, 128))
    nt = M // tm
    F2 = ffn_conv_b.shape[1]
    F = F2 // 2
    depth = ffn_conv_b.shape[0]

    small_sharded = [n for n in SMALL if axis_of[n] is not None]
    got = _all_gather([local[n].astype(BF16) for n in BIG] + [_pack([local[n] for n in small_sharded])],
                      "gather_weights")
    wb = {n: _from_blocks(blk, axis_of[n]) for n, blk in zip(BIG, got)}
    small_blocks = _unpack(got[-1], [local[n].shape for n in small_sharded], lead=(N_DEV,))
    wf32 = {n: _from_blocks(blk, axis_of[n]) for n, blk in zip(small_sharded, small_blocks)}

    pw = wb["pool_w"][0]
    wf_pad = jnp.pad(wf32["w_f"].astype(BF16), ((0, 0), (0, LANES - H)))
    w_qkv = jnp.concatenate([wb["w_kv"], wb["w_q"][0]], axis=1)
    w_att = jnp.concatenate([w_qkv, wf_pad], axis=1)
    wo = wb["w_o"][0]
    w_in = _glu_interleave(wb["ffn_w_in"])
    w_out = wb["ffn_w_out"]
    conv_w = _glu_interleave(wf32["ffn_conv_w"])
    conv_b = _glu_interleave(ffn_conv_b)[:, None, :]
    scale = wf32["pool_scale"]
    g_ln, b_ln = wf32["ln_g"], wf32["ln_b"]
    ln = lambda i, j: (g_ln[i, j][None, :], b_ln[i, j][None, :])
    bias_f = jnp.pad(b_f, (0, LANES - H))[None, :]

    pad_rows = M - n_tok
    h0 = jnp.concatenate([wf32["meta"], x[0], jnp.zeros((pad_rows, D), F32)], axis=0)
    target = jnp.concatenate([jnp.zeros((N_META, D), F32), loss_target[0], jnp.zeros((pad_rows, D), F32)], axis=0)

    z1, h1, h1b, diff = _pool_fwd(h0, pw, scale, *ln(0, 0), name="pool_fwd")
    u0 = _mm(h1b, w_in[0], name="ffn0_up", out_dtype=F32)
    act0 = _conv_glu_fwd(u0, conv_w[0], conv_b[0], name="ffn0_conv")
    z2, h2, h2b = _mm_ln(act0, w_out[0], h1, *ln(0, 1), name="ffn0_down_ln")

    qkv = _mm(h2b, w_qkv, name="attn_qkv", out_dtype=BF16)
    pre = _mm(h2b, wf_pad, name="attn_gate", out_dtype=F32)
    c = _forget_cumsum(pre, bias_f, name="attn_cumsum")
    qkv4 = qkv.reshape(M, 3, H, dh)
    rows_first = lambda t: jnp.transpose(t, (1, 0, 2))
    cols_first = lambda t: jnp.transpose(t, (1, 2, 0))
    key_blocks = lambda t: jnp.transpose(t.reshape(nt, tm, H, dh), (2, 0, 3, 1))
    k_h, v_h = rows_first(qkv4[:, 0]), rows_first(qkv4[:, 1])
    kT_h, vT_h, qT_h = key_blocks(qkv4[:, 0]), key_blocks(qkv4[:, 1]), cols_first(qkv4[:, 2])
    c_t = c[:, :H].T
    cq = c_t[:, None, :]
    ck = c_t[:, :, None]
    oT_h, lse = _attn_fwd(qT_h, k_h, vT_h, cq, ck, name="attn_fwd")
    o = jnp.transpose(oT_h, (2, 0, 1)).reshape(M, D)
    z3, h3, h3b = _mm_ln(o, wo, h2, *ln(1, 0), name="attn_out_ln")

    u1 = _mm(h3b, w_in[1], name="ffn1_up", out_dtype=F32)
    act1 = _conv_glu_fwd(u1, conv_w[1], conv_b[1], name="ffn1_conv")
    z4, dy, loss_part = _mm_ln(act1, w_out[1], h3, *ln(1, 1), name="ffn1_down_loss", target=target, n_tok=n_tok)
    loss = lax.psum(loss_part[0, 0], ("x", "y", "c"))

    grads = {}

    def ffn_bwd(layer, dz_b, u, act, h_in_b, tag):
        dact = _mm(dz_b, w_out[layer], name=tag + "_dact", out_dtype=F32, b_transposed=True)
        du, dcw, dcb = _conv_glu_bwd(u, dact, conv_w[layer], conv_b[layer], name=tag + "_conv_bwd")
        dh = _mm(du, w_in[layer], name=tag + "_dh", out_dtype=F32, b_transposed=True)
        d_w_out = _mm_tn(act, dz_b, name=tag + "_dw_out")
        d_w_in = _glu_deinterleave(_mm_tn(h_in_b, du, name=tag + "_dw_in"))
        return dh, d_w_in, d_w_out, _glu_deinterleave(dcw), _glu_deinterleave(dcb)[0]

    dz4, dz4b, dg11, db11 = _ln_bwd(dy, z4, g_ln[1, 1][None, :], name="ln4_bwd")
    dh3, dwin1, dwout1, dcw1, dcb1 = ffn_bwd(1, dz4b, u1, act1, h3b, "ffn1")
    dz3, dz3b, dg10, db10 = _ln_bwd(dh3, z3, g_ln[1, 0][None, :], name="ln3_bwd", dz_next=dz4)

    do = _mm(dz3b, wo, name="attn_do", out_dtype=BF16, b_transposed=True)
    grads["w_o"] = _mm_tn(o, dz3b, name="attn_dw_o")[None]
    doT_h = cols_first(do.reshape(M, H, dh))
    dqT_h, dkT_h, dvT_h, dcq, dck = _attn_bwd(qT_h, k_h, v_h, kT_h, oT_h, doT_h, lse, cq, ck, name="attn_bwd")
    dc = jnp.pad((dcq[:, 0, :] + dck[:, :, 0]).T, ((0, 0), (0, LANES - H)))
    dpre, dbias = _forget_cumsum_bwd(dc, pre, bias_f, name="attn_cumsum_bwd")
    from_blocks = lambda t: jnp.transpose(t, (1, 3, 0, 2)).reshape(M, D).astype(BF16)
    d_att = jnp.concatenate([from_blocks(dkT_h), from_blocks(dvT_h),
                             jnp.transpose(dqT_h, (2, 0, 1)).reshape(M, D), dpre], axis=1)
    dh2 = _mm(d_att, w_att, name="attn_dh", out_dtype=F32, b_transposed=True)
    d_w_att = _mm_tn(h2b, d_att, name="attn_dw_qkv")
    grads["w_kv"] = d_w_att[:, :2 * D]
    grads["w_q"] = d_w_att[:, 2 * D:3 * D][None]
    grads["w_f"] = d_w_att[:, 3 * D:3 * D + H]
    grads["b_f"] = dbias[0, :H]
    dz2, dz2b, dg01, db01 = _ln_bwd(dh2, z2, g_ln[0, 1][None, :], name="ln2_bwd", dz_next=dz3)

    dh1, dwin0, dwout0, dcw0, dcb0 = ffn_bwd(0, dz2b, u0, act0, h1b, "ffn0")
    dz1, _, dg00, db00 = _ln_bwd(dh1, z1, g_ln[0, 0][None, :], name="ln1_bwd", dz_next=dz2)
    dh0, dpw, dscale = _pool_bwd(dz1, diff, pw, scale, name="pool_bwd")

    grads["meta"] = dh0[:N_META]
    grads["pool_w"] = dpw[None]
    grads["pool_scale"] = dscale
    grads["ffn_w_in"] = jnp.stack([dwin0, dwin1])
    grads["ffn_w_out"] = jnp.stack([dwout0, dwout1])
    grads["ffn_conv_w"] = jnp.stack([dcw0, dcw1])
    grads["ffn_conv_b"] = jnp.stack([dcb0, dcb1])
    grads["ln_g"] = jnp.stack([jnp.stack([dg00[0], dg01[0]]), jnp.stack([dg10[0], dg11[0]])])
    grads["ln_b"] = jnp.stack([jnp.stack([db00[0], db01[0]]), jnp.stack([db10[0], db11[0]])])
    grad_x = dh0[N_META:n_tok][None]

    blocks = {n: _to_blocks(grads[n], axis_of[n]) for n, _ in PARAMS}
    me = _coords()
    core, chip = (jnp.reshape(v, (1,)).astype(jnp.int32) for v in (me[2], 2 * me[0] + me[1]))
    by_core = lambda t: jnp.swapaxes(t.reshape((N_CHIPS, 2) + t.shape[1:]), 0, 1).astype(BF16)
    segs = [by_core(blocks[n]) for n in BIG]
    pair = _exchange(segs, "grad_pair_exchange", (1,), lambda pos: pos[2])
    rows_of = lambda t, lead: t.reshape(lead + (-1, t.shape[-1]))
    partial = [_pair_sum(rows_of(s, (2,)), core, rows_of(p[0], ()), name="grad_pair_sum_" + n).reshape(s.shape[1:])
               for n, s, p in zip(BIG, segs, pair)]
    recv = _exchange(partial, "grad_chip_exchange", (4, 2, 6), lambda pos: 2 * pos[0] + pos[1])
    recv_small = _exchange([_pack([blocks[n] for n in SMALL], lead=(N_DEV,))], "grad_small_exchange",
                           tuple(range(1, N_DEV)), _index, by_sender=True)[0]
    results = {}
    for n, own, r in zip(BIG, partial, recv):
        shape = local[n].shape
        as2d = lambda t: t.reshape(-1, shape[-1])
        outs = _adamw(rows_of(own, (N_CHIPS,)), chip, rows_of(r, (N_CHIPS - 1,)), as2d(local[n]), as2d(mom1[n]),
                      as2d(mom2[n]), name="adamw_" + n)
        results[n] = [o_.reshape(shape) for o_ in outs]
    outs = _adamw(recv_small, jnp.zeros((1,), jnp.int32), recv_small[1:],
                  *[_pack([d[n] for n in SMALL]) for d in (local, mom1, mom2)], name="adamw_small")
    small_out = [_unpack(o_, [local[n].shape for n in SMALL]) for o_ in outs]
    for i, n in enumerate(SMALL):
        results[n] = [small_out[k][i] for k in range(4)]
    return (loss, grad_x, *[results[n][k] for k in range(4) for n, _ in PARAMS])
```

```python
import jax
import jax.numpy as jnp
from jax import lax
from jax.experimental import pallas as pl
from jax.experimental.pallas import tpu as pltpu

F32, BF16 = jnp.float32, jnp.bfloat16
MESH = pl.DeviceIdType.MESH

N_DEV = 8
N_CHIPS = 4
N_META = 16
POOL_WINDOWS = (2, 4, 8, 16)
POOL_HALO = 16
CONV_HALO = 8
CONV_STRIP = 16
ALPHA = 4.0 ** 0.25
LN_EPS = 1e-5
NEG_INF = -1e30
ADAM_LR, ADAM_B1, ADAM_B2, ADAM_EPS, ADAM_WD, ADAM_STEP = 0.001, 0.9, 0.999, 1e-08, 0.01, 10

LANES = 128
PACK_COLS = 1024
ADAM_TILE_BYTES = 4 << 20
MM_TN_WINDOW_BYTES = 28 << 20
GLU_CHUNK = 256
ATTN_EXTRA = 16
VMEM_LIMIT = 56 * 1024 * 1024

PARAMS = (("meta", 1), ("pool_w", 2), ("pool_scale", 1), ("w_kv", 1), ("w_f", 0), ("b_f", None),
          ("w_q", 1), ("w_o", 1), ("ffn_w_in", 2), ("ffn_conv_w", 2), ("ffn_conv_b", None),
          ("ffn_w_out", 1), ("ln_g", 2), ("ln_b", 2))
BIG = ("pool_w", "w_kv", "w_q", "w_o", "ffn_w_in", "ffn_w_out")
SMALL = ("meta", "pool_scale", "w_f", "b_f", "ffn_conv_w", "ffn_conv_b", "ln_g", "ln_b")


def _cparams(**kw):
    return pltpu.CompilerParams(vmem_limit_bytes=VMEM_LIMIT, **kw)


def _pick(n, cands):
    for c in cands:
        if n % c == 0:
            return c
    return n


def _round_up(n, m):
    return (n + m - 1) // m * m


def _pack(pieces, lead=()):
    flat = []
    for p in pieces:
        v = p.reshape(lead + (-1,))
        flat.append(jnp.pad(v, [(0, 0)] * len(lead) + [(0, _round_up(v.shape[-1], PACK_COLS) - v.shape[-1])]))
    v = jnp.concatenate(flat, axis=-1)
    rows = _round_up(v.shape[-1] // PACK_COLS, 8)
    v = jnp.pad(v, [(0, 0)] * len(lead) + [(0, rows * PACK_COLS - v.shape[-1])])
    return v.reshape(lead + (rows, PACK_COLS))


def _unpack(buf, shapes, lead=()):
    flat = buf.reshape(lead + (-1,))
    out, off = [], 0
    for s in shapes:
        n = 1
        for d in s:
            n *= d
        out.append(flat[..., off:off + n].reshape(lead + tuple(s)))
        off += _round_up(n, PACK_COLS)
    return out


def _to_blocks(full, axis):
    if axis is None:
        return jnp.broadcast_to(full[None], (N_DEV,) + full.shape)
    s = full.shape
    x = full.reshape(s[:axis] + (N_DEV, s[axis] // N_DEV) + s[axis + 1:])
    return jnp.moveaxis(x, axis, 0)


def _from_blocks(blocks, axis):
    x = jnp.moveaxis(blocks, 0, axis)
    s = x.shape
    return x.reshape(s[:axis] + (s[axis] * s[axis + 1],) + s[axis + 2:])


def _coords():
    return lax.axis_index("x"), lax.axis_index("y"), lax.axis_index("c")


def _flip(pos, mask):
    x, y, c = pos
    return (1 - x if mask & 4 else x, 1 - y if mask & 2 else y, 1 - c if mask & 1 else c)


def _index(pos):
    x, y, c = pos
    return 4 * x + 2 * y + c


def _comm_call(body, name, arrays, out_shapes):
    n = len(arrays)
    hbm = pl.BlockSpec(memory_space=pl.ANY)
    return pl.pallas_call(
        body, name=name, out_shape=out_shapes, in_specs=[hbm] * n, out_specs=[hbm] * n,
        scratch_shapes=[pltpu.SemaphoreType.DMA((7 * n,)), pltpu.SemaphoreType.DMA((7 * n,)),
                        pltpu.SemaphoreType.DMA((n,))],
    )(*arrays)


def _all_gather(blocks, name):
    chip_masks = (4, 2, 6)
    n = len(blocks)

    def body(*refs):
        x_refs, out_refs = refs[:n], refs[n:2 * n]
        send_sems, recv_sems, local_sems = refs[2 * n:]
        me = _coords()
        sibling = _flip(me, 1)

        def copy(a, k, owner, to, from_input=False):
            slot = out_refs[a].at[_index(owner)]
            return pltpu.make_async_remote_copy(
                src_ref=x_refs[a] if from_input else slot, dst_ref=slot,
                send_sem=send_sems.at[7 * a + k], recv_sem=recv_sems.at[7 * a + k],
                device_id=to, device_id_type=MESH)

        mine = [pltpu.make_async_copy(x_refs[a], out_refs[a].at[_index(me)], local_sems.at[a]) for a in range(n)]
        first = [copy(a, 0, me, sibling, True) for a in range(n)]
        first += [copy(a, 1 + j, me, _flip(me, m), True) for j, m in enumerate(chip_masks) for a in range(n)]
        for cp in mine + first:
            cp.start()
        passed = []
        for j, m in enumerate(chip_masks):
            for a in range(n):
                copy(a, 1 + j, _flip(me, m), me).wait_recv()
                passed.append(copy(a, 4 + j, _flip(me, m), sibling))
                passed[-1].start()
        for a in range(n):
            copy(a, 0, sibling, me).wait_recv()
            for j, m in enumerate(chip_masks):
                copy(a, 4 + j, _flip(sibling, m), me).wait_recv()
        for cp in first + passed:
            cp.wait_send()
        for cp in mine:
            cp.wait()

    return _comm_call(body, name, blocks, [jax.ShapeDtypeStruct((N_DEV,) + b.shape, b.dtype) for b in blocks])


def _exchange(segs, name, masks, slot_of, by_sender=False):
    n = len(segs)

    def body(*refs):
        seg_refs, out_refs = refs[:n], refs[n:2 * n]
        send_sems, recv_sems, local_sems = refs[2 * n:]
        me = _coords()

        def copy(a, k, sender):
            to = _flip(sender, masks[k])
            return pltpu.make_async_remote_copy(
                src_ref=seg_refs[a].at[slot_of(to)], dst_ref=out_refs[a].at[slot_of(sender) if by_sender else k],
                send_sem=send_sems.at[7 * a + k], recv_sem=recv_sems.at[7 * a + k],
                device_id=to, device_id_type=MESH)

        mine = [pltpu.make_async_copy(seg_refs[a].at[slot_of(me)], out_refs[a].at[slot_of(me)], local_sems.at[a])
                for a in range(n)] if by_sender else []
        sends = [copy(a, k, me) for k in range(len(masks)) for a in range(n)]
        for cp in mine + sends:
            cp.start()
        for k, mask in enumerate(masks):
            for a in range(n):
                copy(a, k, _flip(me, mask)).wait_recv()
        for cp in sends:
            cp.wait_send()
        for cp in mine:
            cp.wait()

    slots = lambda s: s.shape[0] if by_sender else len(masks)
    return _comm_call(body, name, segs, [jax.ShapeDtypeStruct((slots(s),) + s.shape[1:], s.dtype) for s in segs])


def _own_slot(tr, cols):
    return pl.BlockSpec((1, tr, cols), lambda i, slot: (slot[0], i, 0))


def _pair_sum(seg, slot, recv, name):
    _, rows, cols = seg.shape
    tr = max(t for t in range(16, rows + 1, 16) if rows % t == 0 and 2 * t * cols * 2 <= ADAM_TILE_BYTES)

    def body(slot_ref, own_ref, recv_ref, o_ref):
        o_ref[...] = (own_ref[0].astype(F32) + recv_ref[...].astype(F32)).astype(BF16)

    tile = pl.BlockSpec((tr, cols), lambda i, slot: (i, 0))
    return pl.pallas_call(
        body, name=name,
        grid_spec=pltpu.PrefetchScalarGridSpec(
            num_scalar_prefetch=1, grid=(rows // tr,), in_specs=[_own_slot(tr, cols), tile], out_specs=tile),
        out_shape=jax.ShapeDtypeStruct((rows, cols), BF16),
        compiler_params=_cparams(),
    )(slot, seg, recv)


def _adamw(seg, slot, recv, w, m, v, name):
    rows, cols = w.shape
    n_slots = recv.shape[0]
    sublanes = 8 * (4 // recv.dtype.itemsize)
    tr = max(t for t in range(sublanes, rows + 1, sublanes)
             if rows % t == 0 and N_DEV * t * cols * 4 <= ADAM_TILE_BYTES)
    c1 = 1.0 - ADAM_B1 ** ADAM_STEP
    c2 = 1.0 - ADAM_B2 ** ADAM_STEP

    def body(slot_ref, own_ref, r_ref, w_ref, m_ref, v_ref, g_out, d_out, m_out, v_out):
        g = own_ref[0].astype(F32)
        for s in range(n_slots):
            g = g + r_ref[s].astype(F32)
        m_new = ADAM_B1 * m_ref[...] + (1.0 - ADAM_B1) * g
        v_new = ADAM_B2 * v_ref[...] + (1.0 - ADAM_B2) * (g * g)
        m_hat = m_new / c1
        v_hat = v_new / c2
        g_out[...] = g
        d_out[...] = -ADAM_LR * (m_hat / (jnp.sqrt(v_hat) + ADAM_EPS) + ADAM_WD * w_ref[...])
        m_out[...] = m_new
        v_out[...] = v_new

    tile = pl.BlockSpec((tr, cols), lambda i, slot: (i, 0))
    return pl.pallas_call(
        body, name=name,
        grid_spec=pltpu.PrefetchScalarGridSpec(
            num_scalar_prefetch=1, grid=(rows // tr,),
            in_specs=[_own_slot(tr, cols), pl.BlockSpec((n_slots, tr, cols), lambda i, slot: (0, i, 0)),
                      tile, tile, tile],
            out_specs=[tile] * 4),
        out_shape=[jax.ShapeDtypeStruct(w.shape, F32)] * 4,
        compiler_params=_cparams(),
    )(slot, seg, recv, w, m, v)


def _mm(a, b, *, name, out_dtype, b_transposed=False):
    M, K = a.shape
    N = b.shape[0] if b_transposed else b.shape[1]
    tm = _pick(M, (640, 128))
    tn = _pick(N, (2816, 1024, 640, 512, 256, 128))
    tk = K if K <= 3200 else _pick(K, (2816, 1024, 640, 512, 256, 128))
    nk = K // tk

    def body(*refs):
        a_ref, b_ref, o_ref = refs[:3]
        acc_ref = refs[-1] if nk > 1 else None
        if b_transposed:
            prod = _dot_nt(a_ref[...], b_ref[...])
        else:
            prod = jnp.dot(a_ref[...], b_ref[...], preferred_element_type=F32)
        if nk == 1:
            o_ref[...] = prod.astype(out_dtype)
        else:
            k = pl.program_id(2)

            @pl.when(k == 0)
            def _():
                acc_ref[...] = prod

            @pl.when(k > 0)
            def _():
                acc_ref[...] += prod

            @pl.when(k == nk - 1)
            def _():
                o_ref[...] = acc_ref[...].astype(out_dtype)

    b_spec = (pl.BlockSpec((tn, tk), lambda i, j, k: (j, k)) if b_transposed
              else pl.BlockSpec((tk, tn), lambda i, j, k: (k, j)))
    return pl.pallas_call(
        body, name=name, grid=(M // tm, N // tn, nk),
        in_specs=[pl.BlockSpec((tm, tk), lambda i, j, k: (i, k)), b_spec],
        out_specs=pl.BlockSpec((tm, tn), lambda i, j, k: (i, j)),
        out_shape=jax.ShapeDtypeStruct((M, N), out_dtype),
        scratch_shapes=[pltpu.VMEM((tm, tn), F32)] if nk > 1 else [],
        compiler_params=_cparams(),
    )(a, b)


def _mm_tn(a, b, *, name):
    T, M = a.shape
    N = b.shape[1]
    tm = _pick(M, (1408, 1024, 512, 256, 128))
    tn = _pick(N, (1024, 640, 512, 256, 128))
    tt = _pick(T, [t for t in (4160, 1664, 640) if 2 * 2 * t * (tm + tn) <= MM_TN_WINDOW_BYTES] + [128])

    def body(a_ref, b_ref, o_ref):
        prod = lax.dot_general(a_ref[...], b_ref[...], (((0,), (0,)), ((), ())), preferred_element_type=F32)

        @pl.when(pl.program_id(2) == 0)
        def _():
            o_ref[...] = prod

        @pl.when(pl.program_id(2) > 0)
        def _():
            o_ref[...] += prod

    return pl.pallas_call(
        body, name=name, grid=(M // tm, N // tn, T // tt),
        in_specs=[pl.BlockSpec((tt, tm), lambda i, j, k: (k, i)), pl.BlockSpec((tt, tn), lambda i, j, k: (k, j))],
        out_specs=pl.BlockSpec((tm, tn), lambda i, j, k: (i, j)),
        out_shape=jax.ShapeDtypeStruct((M, N), F32),
        compiler_params=_cparams(),
    )(a, b)


def _layer_norm(z, g, b):
    mu = jnp.mean(z, axis=-1, keepdims=True)
    xc = z - mu
    var = jnp.mean(xc * xc, axis=-1, keepdims=True)
    return xc * lax.rsqrt(var + LN_EPS) * g + b


def _mm_ln(a, w, res, g, b, *, name, target=None, n_tok=None):
    M, K = a.shape
    D = w.shape[1]
    tm = _pick(M, (640, 128))
    final = target is not None

    def body(*refs):
        if final:
            a_ref, w_ref, r_ref, g_ref, b_ref, t_ref, z_ref, dy_ref, loss_ref = refs
        else:
            a_ref, w_ref, r_ref, g_ref, b_ref, z_ref, h_ref, hb_ref = refs
        z = ALPHA * r_ref[...] + jnp.dot(a_ref[...], w_ref[...], preferred_element_type=F32)
        z_ref[...] = z
        h = _layer_norm(z, g_ref[...], b_ref[...])
        if not final:
            h_ref[...] = h
            hb_ref[...] = h.astype(BF16)
            return
        i = pl.program_id(0)
        row = i * tm + lax.broadcasted_iota(jnp.int32, (tm, 1), 0)
        valid = (row >= N_META) & (row < n_tok)
        err = jnp.where(valid, h - t_ref[...], 0.0)
        dy_ref[...] = err / D

        @pl.when(i == 0)
        def _():
            loss_ref[...] = jnp.zeros_like(loss_ref)

        loss_ref[...] += 0.5 * jnp.sum(jnp.sum(err * err, axis=1, keepdims=True) / D, axis=0, keepdims=True)

    row_blk = lambda cols: pl.BlockSpec((tm, cols), lambda i: (i, 0))
    vec = pl.BlockSpec((1, D), lambda i: (0, 0))
    in_specs = [row_blk(K), pl.BlockSpec((K, D), lambda i: (0, 0)), row_blk(D), vec, vec]
    args = [a, w, res, g, b]
    if final:
        in_specs.append(row_blk(D))
        args.append(target)
        out_specs = [row_blk(D), row_blk(D), pl.BlockSpec((8, LANES), lambda i: (0, 0))]
        out_shape = [jax.ShapeDtypeStruct((M, D), F32)] * 2 + [jax.ShapeDtypeStruct((8, LANES), F32)]
    else:
        out_specs = [row_blk(D)] * 3
        out_shape = [jax.ShapeDtypeStruct((M, D), F32)] * 2 + [jax.ShapeDtypeStruct((M, D), BF16)]
    return pl.pallas_call(
        body, name=name, grid=(M // tm,), in_specs=in_specs, out_specs=out_specs, out_shape=out_shape,
        compiler_params=_cparams(),
    )(*args)


def _ln_bwd(dh, z, g, *, name, dz_next=None):
    M, D = z.shape
    tm = _pick(M, (640, 128))
    has_next = dz_next is not None

    def body(*refs):
        if has_next:
            dh_ref, nx_ref, z_ref, g_ref, dz_ref, dzb_ref, dg_ref, db_ref = refs
            dh_v = dh_ref[...] + ALPHA * nx_ref[...]
        else:
            dh_ref, z_ref, g_ref, dz_ref, dzb_ref, dg_ref, db_ref = refs
            dh_v = dh_ref[...]
        z_v = z_ref[...]
        mu = jnp.mean(z_v, axis=-1, keepdims=True)
        xc = z_v - mu
        rstd = lax.rsqrt(jnp.mean(xc * xc, axis=-1, keepdims=True) + LN_EPS)
        xhat = xc * rstd
        dxhat = dh_v * g_ref[...]
        dz = rstd * (dxhat - jnp.mean(dxhat, axis=-1, keepdims=True)
                     - xhat * jnp.mean(dxhat * xhat, axis=-1, keepdims=True))
        dz_ref[...] = dz
        dzb_ref[...] = dz.astype(BF16)

        @pl.when(pl.program_id(0) == 0)
        def _():
            dg_ref[...] = jnp.zeros_like(dg_ref)
            db_ref[...] = jnp.zeros_like(db_ref)

        dg_ref[...] += jnp.sum(dh_v * xhat, axis=0, keepdims=True)
        db_ref[...] += jnp.sum(dh_v, axis=0, keepdims=True)

    row_blk = pl.BlockSpec((tm, D), lambda i: (i, 0))
    vec = pl.BlockSpec((1, D), lambda i: (0, 0))
    args = [dh] + ([dz_next] if has_next else []) + [z, g]
    in_specs = [row_blk] * (len(args) - 1) + [vec]
    return pl.pallas_call(
        body, name=name, grid=(M // tm,), in_specs=in_specs,
        out_specs=[row_blk, row_blk, vec, vec],
        out_shape=[jax.ShapeDtypeStruct((M, D), F32), jax.ShapeDtypeStruct((M, D), BF16),
                   jax.ShapeDtypeStruct((1, D), F32), jax.ShapeDtypeStruct((1, D), F32)],
        compiler_params=_cparams(),
    )(*args)


def _pool_fwd(h0, pw, scale, g, b, *, name):
    M, D = h0.shape
    n_groups, G, _ = pw.shape
    tm = _pick(M, (640, 128))

    def body(x_ref, halo_ref, pw_ref, sc_ref, g_ref, b_ref, z_ref, h_ref, hb_ref, diff_ref, ext_ref, mix_ref):
        i = pl.program_id(0)
        x = x_ref[...]
        ext_ref[0:POOL_HALO, :] = jnp.where(i == 0, 0.0, halo_ref[...])
        ext_ref[POOL_HALO:, :] = x
        tok = i * tm + lax.broadcasted_iota(jnp.int32, (tm, 1), 0)
        for gi, win in enumerate(POOL_WINDOWS):
            cols = slice(gi * G, (gi + 1) * G)
            xs = x[:, cols]
            s = xs
            for k in range(1, win):
                s = s + ext_ref[pl.ds(POOL_HALO - k, tm), cols]
            count = jnp.minimum(tok + 1, win).astype(F32)
            d = (s / count - xs).astype(BF16)
            diff_ref[:, cols] = d
            mix_ref[:, cols] = jnp.dot(d, pw_ref[gi], preferred_element_type=F32)
        z = ALPHA * x + mix_ref[...] * sc_ref[...]
        z_ref[...] = z
        h = _layer_norm(z, g_ref[...], b_ref[...])
        h_ref[...] = h
        hb_ref[...] = h.astype(BF16)

    row_blk = pl.BlockSpec((tm, D), lambda i: (i, 0))
    vec = pl.BlockSpec((1, D), lambda i: (0, 0))
    halo = pl.BlockSpec((POOL_HALO, D), lambda i: (jnp.maximum(i * (tm // POOL_HALO) - 1, 0), 0))
    return pl.pallas_call(
        body, name=name, grid=(M // tm,),
        in_specs=[row_blk, halo, pl.BlockSpec((n_groups, G, G), lambda i: (0, 0, 0)), vec, vec, vec],
        out_specs=[row_blk] * 4,
        out_shape=[jax.ShapeDtypeStruct((M, D), F32)] * 2 + [jax.ShapeDtypeStruct((M, D), BF16)] * 2,
        scratch_shapes=[pltpu.VMEM((tm + POOL_HALO, D), F32), pltpu.VMEM((tm, D), F32)],
        compiler_params=_cparams(),
    )(h0, h0, pw, scale, g, b)


def _pool_bwd(dz, diff, pw, scale, *, name):
    M, D = dz.shape
    n_groups, G, _ = pw.shape
    tm = _pick(M, (640, 128))
    nt = M // tm

    def body(dz_ref, halo_ref, diff_ref, pw_ref, sc_ref, dh_ref, dpw_ref, dsc_ref, ext_ref, q_ref):
        i = pl.program_id(0)
        dz_v = dz_ref[...]
        ext_ref[0:tm, :] = dz_v
        ext_ref[tm:, :] = jnp.where(i == nt - 1, 0.0, halo_ref[...])
        tok = i * tm + lax.broadcasted_iota(jnp.int32, (tm + POOL_HALO, 1), 0)

        @pl.when(i == 0)
        def _():
            dpw_ref[...] = jnp.zeros_like(dpw_ref)
            dsc_ref[...] = jnp.zeros_like(dsc_ref)

        for gi, win in enumerate(POOL_WINDOWS):
            cols = slice(gi * G, (gi + 1) * G)
            dmix = (ext_ref[:, cols] * sc_ref[:, cols]).astype(BF16)
            ddiff = _dot_nt(dmix, pw_ref[gi])
            count = jnp.minimum(tok + 1, win).astype(F32)
            q_ref[:, cols] = ddiff / count
            acc = -ddiff[0:tm]
            for k in range(win):
                acc = acc + q_ref[pl.ds(k, tm), cols]
            dh_ref[:, cols] = ALPHA * dz_v[:, cols] + acc
            d = diff_ref[:, cols]
            dpw_ref[gi] += lax.dot_general(d, dmix[0:tm], (((0,), (0,)), ((), ())), preferred_element_type=F32)
            mixed = jnp.dot(d, pw_ref[gi], preferred_element_type=F32)
            dsc_ref[:, cols] += jnp.sum(dz_v[:, cols] * mixed, axis=0, keepdims=True)

    row_blk = pl.BlockSpec((tm, D), lambda i: (i, 0))
    vec = pl.BlockSpec((1, D), lambda i: (0, 0))
    per_tile = tm // POOL_HALO
    halo = pl.BlockSpec((POOL_HALO, D), lambda i: (jnp.minimum((i + 1) * per_tile, nt * per_tile - 1), 0))
    wblk = pl.BlockSpec((n_groups, G, G), lambda i: (0, 0, 0))
    return pl.pallas_call(
        body, name=name, grid=(nt,),
        in_specs=[row_blk, halo, row_blk, wblk, vec],
        out_specs=[row_blk, wblk, vec],
        out_shape=[jax.ShapeDtypeStruct((M, D), F32), jax.ShapeDtypeStruct((n_groups, G, G), F32),
                   jax.ShapeDtypeStruct((1, D), F32)],
        scratch_shapes=[pltpu.VMEM((tm + POOL_HALO, D), F32), pltpu.VMEM((tm + POOL_HALO, D), F32)],
        compiler_params=_cparams(),
    )(dz, dz, diff, pw, scale)


def _glu_interleave(x):
    s = x.shape
    n = s[-1] // (2 * GLU_CHUNK)
    return jnp.swapaxes(x.reshape(s[:-1] + (2, n, GLU_CHUNK)), -3, -2).reshape(s)


def _glu_deinterleave(x):
    s = x.shape
    n = s[-1] // (2 * GLU_CHUNK)
    return jnp.swapaxes(x.reshape(s[:-1] + (n, 2, GLU_CHUNK)), -3, -2).reshape(s)


def _taps(u_ref, head_ref, r0):
    src, base = (head_ref, CONV_HALO) if r0 == 0 else (u_ref, r0)
    return tuple(src[pl.ds(base - k, CONV_STRIP), :] for k in range(3))


def _conv_glu_fwd(u, cw, cb, *, name):
    M, F2 = u.shape
    tm = _pick(M, (640, 128))
    tc = 2 * GLU_CHUNK

    def body(u_ref, halo_ref, w_ref, b_ref, o_ref, head_ref):
        i = pl.program_id(0)
        head_ref[0:CONV_HALO, :] = jnp.where(i == 0, 0.0, halo_ref[...])
        head_ref[CONV_HALO:, :] = u_ref[0:CONV_STRIP, :]
        w0, w1, w2, b = w_ref[0:1, :], w_ref[1:2, :], w_ref[2:3, :], b_ref[...]
        for r0 in range(0, tm, CONV_STRIP):
            u0, u1, u2 = _taps(u_ref, head_ref, r0)
            c = b + w0 * u2 + w1 * u1 + w2 * u0
            a, g = c[:, :GLU_CHUNK], c[:, GLU_CHUNK:]
            o_ref[pl.ds(r0, CONV_STRIP), :] = (a * jax.nn.sigmoid(a) * g).astype(BF16)

    per_tile = tm // CONV_HALO
    return pl.pallas_call(
        body, name=name, grid=(M // tm, F2 // tc),
        in_specs=[pl.BlockSpec((tm, tc), lambda i, j: (i, j)),
                  pl.BlockSpec((CONV_HALO, tc), lambda i, j: (jnp.maximum(i * per_tile - 1, 0), j)),
                  pl.BlockSpec((3, tc), lambda i, j: (0, j)), pl.BlockSpec((1, tc), lambda i, j: (0, j))],
        out_specs=pl.BlockSpec((tm, GLU_CHUNK), lambda i, j: (i, j)),
        out_shape=jax.ShapeDtypeStruct((M, F2 // 2), BF16),
        scratch_shapes=[pltpu.VMEM((CONV_HALO + CONV_STRIP, tc), F32)],
        compiler_params=_cparams(),
    )(u, u, cw, cb)


def _conv_glu_bwd(u, dact, cw, cb, *, name):
    M, F2 = u.shape
    tm = _pick(M, (640, 128))
    nt = M // tm
    tc = 2 * GLU_CHUNK

    def body(u_ref, halo_ref, da_ref, w_ref, b_ref, du_ref, dw_ref, db_ref, head_ref, dcx_ref, carry_ref):
        i = pl.program_id(1)
        head_ref[0:CONV_HALO, :] = jnp.where(i == nt - 1, 0.0, halo_ref[...])
        head_ref[CONV_HALO:, :] = u_ref[0:CONV_STRIP, :]
        w0, w1, w2, b = w_ref[0:1, :], w_ref[1:2, :], w_ref[2:3, :], b_ref[...]

        @pl.when(i == 0)
        def _():
            dw_ref[...] = jnp.zeros_like(dw_ref)
            db_ref[...] = jnp.zeros_like(db_ref)
            carry_ref[...] = jnp.zeros_like(carry_ref)

        def fold(t):
            return sum(t[r:r + 8] for r in range(0, CONV_STRIP, 8))

        dcx_ref[tm:, :] = carry_ref[...]
        s_b = s_0 = s_1 = s_2 = jnp.zeros((8, tc), F32)
        for r0 in reversed(range(0, tm, CONV_STRIP)):
            rows = pl.ds(r0, CONV_STRIP)
            u0, u1, u2 = _taps(u_ref, head_ref, r0)
            c = b + w0 * u2 + w1 * u1 + w2 * u0
            a, g = c[:, :GLU_CHUNK], c[:, GLU_CHUNK:]
            sig = jax.nn.sigmoid(a)
            dact_v = da_ref[rows, :]
            d_a = dact_v * g * (sig * (1.0 + a * (1.0 - sig)))
            d_g = dact_v * (a * sig)
            dc = jnp.concatenate([d_a, d_g], axis=1)
            dcx_ref[rows, :] = dc
            du = w2 * dc + w1 * dcx_ref[pl.ds(r0 + 1, CONV_STRIP), :] + w0 * dcx_ref[pl.ds(r0 + 2, CONV_STRIP), :]
            du_ref[rows, :] = du.astype(BF16)
            s_b, s_0, s_1, s_2 = s_b + fold(dc), s_0 + fold(dc * u2), s_1 + fold(dc * u1), s_2 + fold(dc * u0)
        carry_ref[...] = dcx_ref[0:CONV_HALO, :]
        db_ref[...] += jnp.sum(s_b, axis=0, keepdims=True)
        dw_ref[0:1, :] += jnp.sum(s_0, axis=0, keepdims=True)
        dw_ref[1:2, :] += jnp.sum(s_1, axis=0, keepdims=True)
        dw_ref[2:3, :] += jnp.sum(s_2, axis=0, keepdims=True)

    per_tile = tm // CONV_HALO
    rev = lambda i: nt - 1 - i
    return pl.pallas_call(
        body, name=name, grid=(F2 // tc, nt),
        in_specs=[pl.BlockSpec((tm, tc), lambda j, i: (rev(i), j)),
                  pl.BlockSpec((CONV_HALO, tc), lambda j, i: (jnp.maximum(rev(i) * per_tile - 1, 0), j)),
                  pl.BlockSpec((tm, GLU_CHUNK), lambda j, i: (rev(i), j)),
                  pl.BlockSpec((3, tc), lambda j, i: (0, j)), pl.BlockSpec((1, tc), lambda j, i: (0, j))],
        out_specs=[pl.BlockSpec((tm, tc), lambda j, i: (rev(i), j)),
                   pl.BlockSpec((3, tc), lambda j, i: (0, j)), pl.BlockSpec((1, tc), lambda j, i: (0, j))],
        out_shape=[jax.ShapeDtypeStruct((M, F2), BF16), jax.ShapeDtypeStruct((3, F2), F32),
                   jax.ShapeDtypeStruct((1, F2), F32)],
        scratch_shapes=[pltpu.VMEM((CONV_HALO + CONV_STRIP, tc), F32), pltpu.VMEM((tm + CONV_HALO, tc), F32),
                        pltpu.VMEM((CONV_HALO, tc), F32)],
        compiler_params=_cparams(),
    )(u, u, dact, cw, cb)


def _split3(x):
    hi = x.astype(BF16)
    r = x - hi.astype(F32)
    mid = r.astype(BF16)
    lo = (r - mid.astype(F32)).astype(BF16)
    return hi, mid, lo


def _tri_sum(tri, x):
    return sum(jnp.dot(tri, part, preferred_element_type=F32) for part in _split3(x))


def _log_sigmoid(x):
    return jnp.minimum(x, 0.0) - jnp.log1p(jnp.exp(-jnp.abs(x)))


def _forget_cumsum(pre, bias, *, name):
    M, C = pre.shape
    tm = _pick(M, (640, 128))

    def body(p_ref, b_ref, c_ref, carry_ref):
        i = pl.program_id(0)

        @pl.when(i == 0)
        def _():
            carry_ref[...] = jnp.zeros_like(carry_ref)

        logf = _log_sigmoid(p_ref[...] + b_ref[...])
        r = lax.broadcasted_iota(jnp.int32, (tm, tm), 0)
        s = lax.broadcasted_iota(jnp.int32, (tm, tm), 1)
        c_ref[...] = _tri_sum((s <= r).astype(BF16), logf) + carry_ref[...]
        carry_ref[...] = c_ref[pl.ds(tm - 1, 1), :]

    return pl.pallas_call(
        body, name=name, grid=(M // tm,),
        in_specs=[pl.BlockSpec((tm, C), lambda i: (i, 0)), pl.BlockSpec((1, C), lambda i: (0, 0))],
        out_specs=pl.BlockSpec((tm, C), lambda i: (i, 0)),
        out_shape=jax.ShapeDtypeStruct((M, C), F32),
        scratch_shapes=[pltpu.VMEM((1, C), F32)],
        compiler_params=_cparams(),
    )(pre, bias)


def _forget_cumsum_bwd(dc, pre, bias, *, name):
    M, C = pre.shape
    tm = _pick(M, (640, 128))
    nt = M // tm

    def body(dc_ref, p_ref, b_ref, dp_ref, db_ref, carry_ref, run_ref):
        i = pl.program_id(0)

        @pl.when(i == 0)
        def _():
            carry_ref[...] = jnp.zeros_like(carry_ref)
            db_ref[...] = jnp.zeros_like(db_ref)

        r = lax.broadcasted_iota(jnp.int32, (tm, tm), 0)
        s = lax.broadcasted_iota(jnp.int32, (tm, tm), 1)
        run_ref[...] = _tri_sum((s >= r).astype(BF16), dc_ref[...]) + carry_ref[...]
        carry_ref[...] = run_ref[pl.ds(0, 1), :]
        dpre = run_ref[...] * jax.nn.sigmoid(-(p_ref[...] + b_ref[...]))
        dp_ref[...] = dpre.astype(BF16)
        db_ref[...] += jnp.sum(dpre, axis=0, keepdims=True)

    rev_blk = pl.BlockSpec((tm, C), lambda i: (nt - 1 - i, 0))
    vec = pl.BlockSpec((1, C), lambda i: (0, 0))
    return pl.pallas_call(
        body, name=name, grid=(nt,),
        in_specs=[rev_blk, rev_blk, vec], out_specs=[rev_blk, vec],
        out_shape=[jax.ShapeDtypeStruct((M, C), BF16), jax.ShapeDtypeStruct((1, C), F32)],
        scratch_shapes=[pltpu.VMEM((1, C), F32), pltpu.VMEM((tm, C), F32)],
        compiler_params=_cparams(),
    )(dc, pre, bias)


def _causal_mask(tm):
    key = lax.broadcasted_iota(jnp.int32, (tm, tm), 0)
    query = lax.broadcasted_iota(jnp.int32, (tm, tm), 1)
    return key <= query


def _dot_nt(a, b):
    return lax.dot_general(a, b, (((1,), (1,)), ((), ())), preferred_element_type=F32)


def _loop_unrolled(n, step, init, unroll):
    def trip(p, carry):
        for r in range(unroll):
            carry = step(unroll * p + r, carry)
        return carry
    carry = lax.fori_loop(0, n // unroll, trip, init)
    return lax.fori_loop(unroll * (n // unroll), n, step, carry)


def _rows(main, extras, total):
    tm = main.shape[1]
    used = sum(e.shape[0] for e in extras)
    tile = jnp.concatenate([e.astype(BF16) for e in extras] + [jnp.zeros((ATTN_EXTRA - used, tm), BF16)], axis=0)
    rest = total - main.shape[0] - ATTN_EXTRA
    return jnp.concatenate([main, tile] + ([jnp.zeros((rest, tm), BF16)] if rest else []), axis=0)


def _attn_specs(H, M, dh, tm):
    nt = M // tm
    qT_blk = pl.BlockSpec((1, dh, tm), lambda h, i: (h, 0, i))
    row_blk = pl.BlockSpec((1, 1, tm), lambda h, i: (h, 0, i))
    key_blk = pl.BlockSpec((1, M, LANES), lambda h, i: (h, 0, 0))
    keyT_blk = pl.BlockSpec((1, nt, dh + ATTN_EXTRA, tm), lambda h, i: (h, 0, 0, 0))
    return qT_blk, row_blk, key_blk, keyT_blk


def _attn_fwd(qT, cq, k_aug, vT_aug, *, name):
    H, dh, M = qT.shape
    tm = vT_aug.shape[-1]
    qT_blk, row_blk, key_blk, keyT_blk = _attn_specs(H, M, dh, tm)

    def body(qT_ref, cq_ref, k_ref, vT_ref, oT_ref, lse_ref):
        i = pl.program_id(1)
        ones = jnp.ones((3, tm), BF16)
        qa = _rows(qT_ref[0] * jnp.asarray(dh ** -0.5, BF16), [*_split3(cq_ref[0]), ones], LANES)

        def block(j, carry, masked):
            m, acc = carry
            keys = pl.ds(pl.multiple_of(j * tm, tm), tm)
            sT = jnp.dot(k_ref[0, keys, :], qa, preferred_element_type=F32)
            if masked:
                sT = jnp.where(_causal_mask(tm), sT, NEG_INF)
            m_new = jnp.maximum(m, jnp.max(sT, axis=0, keepdims=True))
            pT = jnp.exp(sT - m_new).astype(BF16)
            acc = jnp.exp(m - m_new) * acc + jnp.dot(vT_ref[0, j], pT, preferred_element_type=F32)
            return m_new, acc

        init = (jnp.full((1, tm), NEG_INF, F32), jnp.zeros((dh + ATTN_EXTRA, tm), F32))
        m, acc = block(i, _loop_unrolled(i, lambda j, c: block(j, c, False), init, 4), True)
        l = acc[dh:dh + 1, :]
        oT_ref[0] = (acc[0:dh, :] / l).astype(BF16)
        lse_ref[0] = m + jnp.log(l)

    return pl.pallas_call(
        body, name=name, grid=(H, M // tm),
        in_specs=[qT_blk, row_blk, key_blk, keyT_blk], out_specs=[qT_blk, row_blk],
        out_shape=[jax.ShapeDtypeStruct((H, dh, M), BF16), jax.ShapeDtypeStruct((H, 1, M), F32)],
        compiler_params=_cparams(),
    )(qT, cq, k_aug, vT_aug)


def _attn_bwd(qT, oT, doT, lse, cq, k_aug, v_aug, kT_aug, *, name):
    H, dh, M = qT.shape
    tm = kT_aug.shape[-1]
    qT_blk, row_blk, key_blk, keyT_blk = _attn_specs(H, M, dh, tm)
    dvT_blk = pl.BlockSpec((1, M // tm, dh, tm), lambda h, i: (h, 0, 0, 0))

    def body(qT_ref, oT_ref, doT_ref, lse_ref, cq_ref, k_ref, v_ref, kT_ref, dqT_ref, dcq_ref, dkT_ref, dvT_ref):
        i = pl.program_id(1)

        @pl.when(i == 0)
        def _():
            dkT_ref[...] = jnp.zeros_like(dkT_ref)
            dvT_ref[...] = jnp.zeros_like(dvT_ref)

        qsT = qT_ref[0] * jnp.asarray(dh ** -0.5, BF16)
        doT = doT_ref[0]
        delta = jnp.sum(doT.astype(F32) * oT_ref[0].astype(F32), axis=0, keepdims=True)
        ones = jnp.ones((3, tm), BF16)
        qa = _rows(qsT, [*_split3(cq_ref[0] - lse_ref[0]), ones], LANES)
        da = _rows(doT, _split3(-delta), LANES)
        q1 = _rows(qsT, [ones[0:1]], dh + ATTN_EXTRA)

        def block(j, dq, masked):
            keys = pl.ds(pl.multiple_of(j * tm, tm), tm)
            pT = jnp.exp(jnp.dot(k_ref[0, keys, :], qa, preferred_element_type=F32))
            if masked:
                pT = jnp.where(_causal_mask(tm), pT, 0.0)
            ds_b = (pT * jnp.dot(v_ref[0, keys, :], da, preferred_element_type=F32)).astype(BF16)
            dvT_ref[0, j] += _dot_nt(doT, pT.astype(BF16))
            dkT_ref[0, j] += _dot_nt(q1, ds_b)
            return dq + jnp.dot(kT_ref[0, j], ds_b, preferred_element_type=F32)

        init = jnp.zeros((dh + ATTN_EXTRA, tm), F32)
        dq = block(i, _loop_unrolled(i, lambda j, c: block(j, c, False), init, 2), True)
        dqT_ref[0] = (dq[0:dh, :] * dh ** -0.5).astype(BF16)
        dcq_ref[0] = dq[dh:dh + 1, :]

    return pl.pallas_call(
        body, name=name, grid=(H, M // tm),
        in_specs=[qT_blk, qT_blk, qT_blk, row_blk, row_blk, key_blk, key_blk, keyT_blk],
        out_specs=[qT_blk, row_blk, keyT_blk, dvT_blk],
        out_shape=[jax.ShapeDtypeStruct((H, dh, M), BF16), jax.ShapeDtypeStruct((H, 1, M), F32),
                   jax.ShapeDtypeStruct(kT_aug.shape, F32), jax.ShapeDtypeStruct((H, M // tm, dh, tm), F32)],
        compiler_params=_cparams(),
    )(qT, oT, doT, lse, cq, k_aug, v_aug, kT_aug)


def kernel(x, meta, pool_w, pool_scale, w_kv, w_f, b_f, w_q, w_o, ffn_w_in, ffn_conv_w, ffn_conv_b, ffn_w_out, ln_g, ln_b, loss_target, m_meta, m_pool_w, m_pool_scale, m_w_kv, m_w_f, m_b_f, m_w_q, m_w_o, m_ffn_w_in, m_ffn_conv_w, m_ffn_conv_b, m_ffn_w_out, m_ln_g, m_ln_b, v_meta, v_pool_w, v_pool_scale, v_w_kv, v_w_f, v_b_f, v_w_q, v_w_o, v_ffn_w_in, v_ffn_conv_w, v_ffn_conv_b, v_ffn_w_out, v_ln_g, v_ln_b):
    local = dict(meta=meta, pool_w=pool_w, pool_scale=pool_scale, w_kv=w_kv, w_f=w_f, b_f=b_f, w_q=w_q, w_o=w_o,
                 ffn_w_in=ffn_w_in, ffn_conv_w=ffn_conv_w, ffn_conv_b=ffn_conv_b, ffn_w_out=ffn_w_out,
                 ln_g=ln_g, ln_b=ln_b)
    mom1 = dict(meta=m_meta, pool_w=m_pool_w, pool_scale=m_pool_scale, w_kv=m_w_kv, w_f=m_w_f, b_f=m_b_f,
                w_q=m_w_q, w_o=m_w_o, ffn_w_in=m_ffn_w_in, ffn_conv_w=m_ffn_conv_w, ffn_conv_b=m_ffn_conv_b,
                ffn_w_out=m_ffn_w_out, ln_g=m_ln_g, ln_b=m_ln_b)
    mom2 = dict(meta=v_meta, pool_w=v_pool_w, pool_scale=v_pool_scale, w_kv=v_w_kv, w_f=v_w_f, b_f=v_b_f,
                w_q=v_w_q, w_o=v_w_o, ffn_w_in=v_ffn_w_in, ffn_conv_w=v_ffn_conv_w, ffn_conv_b=v_ffn_conv_b,
                ffn_w_out=v_ffn_w_out, ln_g=v_ln_g, ln_b=v_ln_b)
    axis_of = dict(PARAMS)

    S, D = x.shape[1], x.shape[2]
    H = b_f.shape[0]
    dh = D // H
    n_tok = N_META + S
    M = _round_up(n_tok, LANES)
    tm = _pick(M, (640, 128))
    nt = M // tm
    F2 = ffn_conv_b.shape[1]
    F = F2 // 2
    depth = ffn_conv_b.shape[0]

    small_sharded = [n for n in SMALL if axis_of[n] is not None]
    got = _all_gather([local[n].astype(BF16) for n in BIG] + [_pack([local[n] for n in small_sharded])],
                      "gather_weights")
    wb = {n: _from_blocks(blk, axis_of[n]) for n, blk in zip(BIG, got)}
    small_blocks = _unpack(got[-1], [local[n].shape for n in small_sharded], lead=(N_DEV,))
    wf32 = {n: _from_blocks(blk, axis_of[n]) for n, blk in zip(small_sharded, small_blocks)}

    pw = wb["pool_w"][0]
    wf_pad = jnp.pad(wf32["w_f"].astype(BF16), ((0, 0), (0, LANES - H)))
    w_qkv = jnp.concatenate([wb["w_kv"], wb["w_q"][0]], axis=1)
    w_att = jnp.concatenate([w_qkv, wf_pad], axis=1)
    wo = wb["w_o"][0]
    w_in = _glu_interleave(wb["ffn_w_in"])
    w_out = wb["ffn_w_out"]
    conv_w = _glu_interleave(wf32["ffn_conv_w"])
    conv_b = _glu_interleave(ffn_conv_b)[:, None, :]
    scale = wf32["pool_scale"]
    g_ln, b_ln = wf32["ln_g"], wf32["ln_b"]
    ln = lambda i, j: (g_ln[i, j][None, :], b_ln[i, j][None, :])
    bias_f = jnp.pad(b_f, (0, LANES - H))[None, :]

    pad_rows = M - n_tok
    h0 = jnp.concatenate([wf32["meta"], x[0], jnp.zeros((pad_rows, D), F32)], axis=0)
    target = jnp.concatenate([jnp.zeros((N_META, D), F32), loss_target[0], jnp.zeros((pad_rows, D), F32)], axis=0)

    z1, h1, h1b, diff = _pool_fwd(h0, pw, scale, *ln(0, 0), name="pool_fwd")
    u0 = _mm(h1b, w_in[0], name="ffn0_up", out_dtype=F32)
    act0 = _conv_glu_fwd(u0, conv_w[0], conv_b[0], name="ffn0_conv")
    z2, h2, h2b = _mm_ln(act0, w_out[0], h1, *ln(0, 1), name="ffn0_down_ln")

    qkv = _mm(h2b, w_qkv, name="attn_qkv", out_dtype=BF16)
    pre = _mm(h2b, wf_pad, name="attn_gate", out_dtype=F32)
    c = _forget_cumsum(pre, bias_f, name="attn_cumsum")
    qkv4 = qkv.reshape(M, 3, H, dh)
    rows_first = lambda t: jnp.transpose(t, (1, 0, 2))
    cols_first = lambda t: jnp.transpose(t, (1, 2, 0))
    key_blocks = lambda t: jnp.transpose(t.reshape(nt, tm, H, dh), (2, 0, 3, 1))
    k_h, v_h = rows_first(qkv4[:, 0]), rows_first(qkv4[:, 1])
    kT_h, vT_h, qT_h = key_blocks(qkv4[:, 0]), key_blocks(qkv4[:, 1]), cols_first(qkv4[:, 2])
    c_t = c[:, :H].T
    cq = c_t[:, None, :]
    ones_cols = jnp.ones((H, M, 3), BF16)
    to_bf16 = lambda t: lax.reduce_precision(t, exponent_bits=8, mantissa_bits=7)
    ck_hi = to_bf16(-c_t)
    ck_mid = to_bf16(-c_t - ck_hi)
    ck_lo = to_bf16(-c_t - ck_hi - ck_mid)
    minus_ck = jnp.stack([ck_hi, ck_mid, ck_lo], axis=-1).astype(BF16)
    widen = lambda t: jnp.pad(t, ((0, 0), (0, 0), (0, LANES - t.shape[-1])))
    k_aug = widen(jnp.concatenate([k_h, ones_cols, minus_ck], axis=-1))
    v_aug = widen(jnp.concatenate([v_h, ones_cols], axis=-1))
    one_row = lambda t: jnp.concatenate([t, jnp.ones((H, nt, 1, tm), BF16),
                                         jnp.zeros((H, nt, ATTN_EXTRA - 1, tm), BF16)], axis=2)
    kT_aug, vT_aug = one_row(kT_h), one_row(vT_h)
    oT_h, lse = _attn_fwd(qT_h, cq, k_aug, vT_aug, name="attn_fwd")
    o = jnp.transpose(oT_h, (2, 0, 1)).reshape(M, D)
    z3, h3, h3b = _mm_ln(o, wo, h2, *ln(1, 0), name="attn_out_ln")

    u1 = _mm(h3b, w_in[1], name="ffn1_up", out_dtype=F32)
    act1 = _conv_glu_fwd(u1, conv_w[1], conv_b[1], name="ffn1_conv")
    z4, dy, loss_part = _mm_ln(act1, w_out[1], h3, *ln(1, 1), name="ffn1_down_loss", target=target, n_tok=n_tok)
    loss = lax.psum(loss_part[0, 0], ("x", "y", "c"))

    grads = {}

    def ffn_bwd(layer, dz_b, u, act, h_in_b, tag):
        dact = _mm(dz_b, w_out[layer], name=tag + "_dact", out_dtype=F32, b_transposed=True)
        du, dcw, dcb = _conv_glu_bwd(u, dact, conv_w[layer], conv_b[layer], name=tag + "_conv_bwd")
        dh = _mm(du, w_in[layer], name=tag + "_dh", out_dtype=F32, b_transposed=True)
        d_w_out = _mm_tn(act, dz_b, name=tag + "_dw_out")
        d_w_in = _glu_deinterleave(_mm_tn(h_in_b, du, name=tag + "_dw_in"))
        return dh, d_w_in, d_w_out, _glu_deinterleave(dcw), _glu_deinterleave(dcb)[0]

    dz4, dz4b, dg11, db11 = _ln_bwd(dy, z4, g_ln[1, 1][None, :], name="ln4_bwd")
    dh3, dwin1, dwout1, dcw1, dcb1 = ffn_bwd(1, dz4b, u1, act1, h3b, "ffn1")
    dz3, dz3b, dg10, db10 = _ln_bwd(dh3, z3, g_ln[1, 0][None, :], name="ln3_bwd", dz_next=dz4)

    do = _mm(dz3b, wo, name="attn_do", out_dtype=BF16, b_transposed=True)
    grads["w_o"] = _mm_tn(o, dz3b, name="attn_dw_o")[None]
    doT_h = cols_first(do.reshape(M, H, dh))
    dqT_h, dcq, dkT_a, dvT_h = _attn_bwd(qT_h, oT_h, doT_h, lse, cq, k_aug, v_aug, kT_aug, name="attn_bwd")
    dkT_h = dkT_a[:, :, :dh, :]
    dck = -dkT_a[:, :, dh, :].reshape(H, M)
    dc = jnp.pad((dcq[:, 0, :] + dck).T, ((0, 0), (0, LANES - H)))
    dpre, dbias = _forget_cumsum_bwd(dc, pre, bias_f, name="attn_cumsum_bwd")
    from_blocks = lambda t: jnp.transpose(t, (1, 3, 0, 2)).reshape(M, D).astype(BF16)
    d_att = jnp.concatenate([from_blocks(dkT_h), from_blocks(dvT_h),
                             jnp.transpose(dqT_h, (2, 0, 1)).reshape(M, D), dpre], axis=1)
    dh2 = _mm(d_att, w_att, name="attn_dh", out_dtype=F32, b_transposed=True)
    d_w_att = _mm_tn(h2b, d_att, name="attn_dw_qkv")
    grads["w_kv"] = d_w_att[:, :2 * D]
    grads["w_q"] = d_w_att[:, 2 * D:3 * D][None]
    grads["w_f"] = d_w_att[:, 3 * D:3 * D + H]
    grads["b_f"] = dbias[0, :H]
    dz2, dz2b, dg01, db01 = _ln_bwd(dh2, z2, g_ln[0, 1][None, :], name="ln2_bwd", dz_next=dz3)

    dh1, dwin0, dwout0, dcw0, dcb0 = ffn_bwd(0, dz2b, u0, act0, h1b, "ffn0")
    dz1, _, dg00, db00 = _ln_bwd(dh1, z1, g_ln[0, 0][None, :], name="ln1_bwd", dz_next=dz2)
    dh0, dpw, dscale = _pool_bwd(dz1, diff, pw, scale, name="pool_bwd")

    grads["meta"] = dh0[:N_META]
    grads["pool_w"] = dpw[None]
    grads["pool_scale"] = dscale
    grads["ffn_w_in"] = jnp.stack([dwin0, dwin1])
    grads["ffn_w_out"] = jnp.stack([dwout0, dwout1])
    grads["ffn_conv_w"] = jnp.stack([dcw0, dcw1])
    grads["ffn_conv_b"] = jnp.stack([dcb0, dcb1])
    grads["ln_g"] = jnp.stack([jnp.stack([dg00[0], dg01[0]]), jnp.stack([dg10[0], dg11[0]])])
    grads["ln_b"] = jnp.stack([jnp.stack([db00[0], db01[0]]), jnp.stack([db10[0], db11[0]])])
    grad_x = dh0[N_META:n_tok][None]

    blocks = {n: _to_blocks(grads[n], axis_of[n]) for n, _ in PARAMS}
    me = _coords()
    core, chip = (jnp.reshape(v, (1,)).astype(jnp.int32) for v in (me[2], 2 * me[0] + me[1]))
    by_core = lambda t: jnp.swapaxes(t.reshape((N_CHIPS, 2) + t.shape[1:]), 0, 1).astype(BF16)
    segs = [by_core(blocks[n]) for n in BIG]
    pair = _exchange(segs, "grad_pair_exchange", (1,), lambda pos: pos[2])
    rows_of = lambda t, lead: t.reshape(lead + (-1, t.shape[-1]))
    partial = [_pair_sum(rows_of(s, (2,)), core, rows_of(p[0], ()), name="grad_pair_sum_" + n).reshape(s.shape[1:])
               for n, s, p in zip(BIG, segs, pair)]
    recv = _exchange(partial, "grad_chip_exchange", (4, 2, 6), lambda pos: 2 * pos[0] + pos[1])
    recv_small = _exchange([_pack([blocks[n] for n in SMALL], lead=(N_DEV,))], "grad_small_exchange",
                           tuple(range(1, N_DEV)), _index, by_sender=True)[0]
    results = {}
    for n, own, r in zip(BIG, partial, recv):
        shape = local[n].shape
        as2d = lambda t: t.reshape(-1, shape[-1])
        outs = _adamw(rows_of(own, (N_CHIPS,)), chip, rows_of(r, (N_CHIPS - 1,)), as2d(local[n]), as2d(mom1[n]),
                      as2d(mom2[n]), name="adamw_" + n)
        results[n] = [o_.reshape(shape) for o_ in outs]
    outs = _adamw(recv_small, jnp.zeros((1,), jnp.int32), recv_small[1:],
                  *[_pack([d[n] for n in SMALL]) for d in (local, mom1, mom2)], name="adamw_small")
    small_out = [_unpack(o_, [local[n].shape for n in SMALL]) for o_ in outs]
    for i, n in enumerate(SMALL):
        results[n] = [small_out[k][i] for k in range(4)]
    return (loss, grad_x, *[results[n][k] for k in range(4) for n, _ in PARAMS])
```

```python
import jax
import jax.numpy as jnp
from jax import lax
from jax.experimental import pallas as pl
from jax.experimental.pallas import tpu as pltpu

F32, BF16 = jnp.float32, jnp.bfloat16
MESH = pl.DeviceIdType.MESH

N_DEV = 8
N_CHIPS = 4
N_META = 16
POOL_WINDOWS = (2, 4, 8, 16)
POOL_HALO = 16
CONV_HALO = 8
CONV_STRIP = 16
ALPHA = 4.0 ** 0.25
LN_EPS = 1e-5
NEG_INF = -1e30
ADAM_LR, ADAM_B1, ADAM_B2, ADAM_EPS, ADAM_WD, ADAM_STEP = 0.001, 0.9, 0.999, 1e-08, 0.01, 10

LANES = 128
PACK_COLS = 1024
ADAM_TILE_BYTES = 4 << 20
MM_TN_WINDOW_BYTES = 28 << 20
GLU_CHUNK = 256
ATTN_EXTRA = 16
VMEM_LIMIT = 56 * 1024 * 1024

PARAMS = (("meta", 1), ("pool_w", 2), ("pool_scale", 1), ("w_kv", 1), ("w_f", 0), ("b_f", None),
          ("w_q", 1), ("w_o", 1), ("ffn_w_in", 2), ("ffn_conv_w", 2), ("ffn_conv_b", None),
          ("ffn_w_out", 1), ("ln_g", 2), ("ln_b", 2))
BIG = ("pool_w", "w_kv", "w_q", "w_o", "ffn_w_in", "ffn_w_out")
SMALL = ("meta", "pool_scale", "w_f", "b_f", "ffn_conv_w", "ffn_conv_b", "ln_g", "ln_b")


def _cparams(**kw):
    return pltpu.CompilerParams(vmem_limit_bytes=VMEM_LIMIT, **kw)


def _pick(n, cands):
    for c in cands:
        if n % c == 0:
            return c
    return n


def _round_up(n, m):
    return (n + m - 1) // m * m


def _pack(pieces, lead=()):
    flat = []
    for p in pieces:
        v = p.reshape(lead + (-1,))
        flat.append(jnp.pad(v, [(0, 0)] * len(lead) + [(0, _round_up(v.shape[-1], PACK_COLS) - v.shape[-1])]))
    v = jnp.concatenate(flat, axis=-1)
    rows = _round_up(v.shape[-1] // PACK_COLS, 8)
    v = jnp.pad(v, [(0, 0)] * len(lead) + [(0, rows * PACK_COLS - v.shape[-1])])
    return v.reshape(lead + (rows, PACK_COLS))


def _unpack(buf, shapes, lead=()):
    flat = buf.reshape(lead + (-1,))
    out, off = [], 0
    for s in shapes:
        n = 1
        for d in s:
            n *= d
        out.append(flat[..., off:off + n].reshape(lead + tuple(s)))
        off += _round_up(n, PACK_COLS)
    return out


def _to_blocks(full, axis):
    if axis is None:
        return jnp.broadcast_to(full[None], (N_DEV,) + full.shape)
    s = full.shape
    x = full.reshape(s[:axis] + (N_DEV, s[axis] // N_DEV) + s[axis + 1:])
    return jnp.moveaxis(x, axis, 0)


def _from_blocks(blocks, axis):
    x = jnp.moveaxis(blocks, 0, axis)
    s = x.shape
    return x.reshape(s[:axis] + (s[axis] * s[axis + 1],) + s[axis + 2:])


def _coords():
    return lax.axis_index("x"), lax.axis_index("y"), lax.axis_index("c")


def _flip(pos, mask):
    x, y, c = pos
    return (1 - x if mask & 4 else x, 1 - y if mask & 2 else y, 1 - c if mask & 1 else c)


def _index(pos):
    x, y, c = pos
    return 4 * x + 2 * y + c


def _comm_call(body, name, arrays, out_shapes):
    n = len(arrays)
    hbm = pl.BlockSpec(memory_space=pl.ANY)
    return pl.pallas_call(
        body, name=name, out_shape=out_shapes, in_specs=[hbm] * n, out_specs=[hbm] * n,
        scratch_shapes=[pltpu.SemaphoreType.DMA((7 * n,)), pltpu.SemaphoreType.DMA((7 * n,)),
                        pltpu.SemaphoreType.DMA((n,))],
    )(*arrays)


def _all_gather(blocks, name):
    chip_masks = (4, 2, 6)
    n = len(blocks)

    def body(*refs):
        x_refs, out_refs = refs[:n], refs[n:2 * n]
        send_sems, recv_sems, local_sems = refs[2 * n:]
        me = _coords()
        sibling = _flip(me, 1)

        def copy(a, k, owner, to, from_input=False):
            slot = out_refs[a].at[_index(owner)]
            return pltpu.make_async_remote_copy(
                src_ref=x_refs[a] if from_input else slot, dst_ref=slot,
                send_sem=send_sems.at[7 * a + k], recv_sem=recv_sems.at[7 * a + k],
                device_id=to, device_id_type=MESH)

        mine = [pltpu.make_async_copy(x_refs[a], out_refs[a].at[_index(me)], local_sems.at[a]) for a in range(n)]
        first = [copy(a, 0, me, sibling, True) for a in range(n)]
        first += [copy(a, 1 + j, me, _flip(me, m), True) for j, m in enumerate(chip_masks) for a in range(n)]
        for cp in mine + first:
            cp.start()
        passed = []
        for j, m in enumerate(chip_masks):
            for a in range(n):
                copy(a, 1 + j, _flip(me, m), me).wait_recv()
                passed.append(copy(a, 4 + j, _flip(me, m), sibling))
                passed[-1].start()
        for a in range(n):
            copy(a, 0, sibling, me).wait_recv()
            for j, m in enumerate(chip_masks):
                copy(a, 4 + j, _flip(sibling, m), me).wait_recv()
        for cp in first + passed:
            cp.wait_send()
        for cp in mine:
            cp.wait()

    return _comm_call(body, name, blocks, [jax.ShapeDtypeStruct((N_DEV,) + b.shape, b.dtype) for b in blocks])


def _exchange(segs, name, masks, slot_of, by_sender=False):
    n = len(segs)

    def body(*refs):
        seg_refs, out_refs = refs[:n], refs[n:2 * n]
        send_sems, recv_sems, local_sems = refs[2 * n:]
        me = _coords()

        def copy(a, k, sender):
            to = _flip(sender, masks[k])
            return pltpu.make_async_remote_copy(
                src_ref=seg_refs[a].at[slot_of(to)], dst_ref=out_refs[a].at[slot_of(sender) if by_sender else k],
                send_sem=send_sems.at[7 * a + k], recv_sem=recv_sems.at[7 * a + k],
                device_id=to, device_id_type=MESH)

        mine = [pltpu.make_async_copy(seg_refs[a].at[slot_of(me)], out_refs[a].at[slot_of(me)], local_sems.at[a])
                for a in range(n)] if by_sender else []
        sends = [copy(a, k, me) for k in range(len(masks)) for a in range(n)]
        for cp in mine + sends:
            cp.start()
        for k, mask in enumerate(masks):
            for a in range(n):
                copy(a, k, _flip(me, mask)).wait_recv()
        for cp in sends:
            cp.wait_send()
        for cp in mine:
            cp.wait()

    slots = lambda s: s.shape[0] if by_sender else len(masks)
    return _comm_call(body, name, segs, [jax.ShapeDtypeStruct((slots(s),) + s.shape[1:], s.dtype) for s in segs])


def _own_slot(tr, cols):
    return pl.BlockSpec((1, tr, cols), lambda i, slot: (slot[0], i, 0))


def _pair_sum(seg, slot, recv, name):
    _, rows, cols = seg.shape
    tr = max(t for t in range(16, rows + 1, 16) if rows % t == 0 and 2 * t * cols * 2 <= ADAM_TILE_BYTES)

    def body(slot_ref, own_ref, recv_ref, o_ref):
        o_ref[...] = (own_ref[0].astype(F32) + recv_ref[...].astype(F32)).astype(BF16)

    tile = pl.BlockSpec((tr, cols), lambda i, slot: (i, 0))
    return pl.pallas_call(
        body, name=name,
        grid_spec=pltpu.PrefetchScalarGridSpec(
            num_scalar_prefetch=1, grid=(rows // tr,), in_specs=[_own_slot(tr, cols), tile], out_specs=tile),
        out_shape=jax.ShapeDtypeStruct((rows, cols), BF16),
        compiler_params=_cparams(),
    )(slot, seg, recv)


def _adamw(seg, slot, recv, w, m, v, name):
    rows, cols = w.shape
    n_slots = recv.shape[0]
    sublanes = 8 * (4 // recv.dtype.itemsize)
    tr = max(t for t in range(sublanes, rows + 1, sublanes)
             if rows % t == 0 and N_DEV * t * cols * 4 <= ADAM_TILE_BYTES)
    c1 = 1.0 - ADAM_B1 ** ADAM_STEP
    c2 = 1.0 - ADAM_B2 ** ADAM_STEP

    def body(slot_ref, own_ref, r_ref, w_ref, m_ref, v_ref, g_out, d_out, m_out, v_out):
        g = own_ref[0].astype(F32)
        for s in range(n_slots):
            g = g + r_ref[s].astype(F32)
        m_new = ADAM_B1 * m_ref[...] + (1.0 - ADAM_B1) * g
        v_new = ADAM_B2 * v_ref[...] + (1.0 - ADAM_B2) * (g * g)
        m_hat = m_new / c1
        v_hat = v_new / c2
        g_out[...] = g
        d_out[...] = -ADAM_LR * (m_hat / (jnp.sqrt(v_hat) + ADAM_EPS) + ADAM_WD * w_ref[...])
        m_out[...] = m_new
        v_out[...] = v_new

    tile = pl.BlockSpec((tr, cols), lambda i, slot: (i, 0))
    return pl.pallas_call(
        body, name=name,
        grid_spec=pltpu.PrefetchScalarGridSpec(
            num_scalar_prefetch=1, grid=(rows // tr,),
            in_specs=[_own_slot(tr, cols), pl.BlockSpec((n_slots, tr, cols), lambda i, slot: (0, i, 0)),
                      tile, tile, tile],
            out_specs=[tile] * 4),
        out_shape=[jax.ShapeDtypeStruct(w.shape, F32)] * 4,
        compiler_params=_cparams(),
    )(slot, seg, recv, w, m, v)


def _mm(a, b, *, name, out_dtype, b_transposed=False):
    M, K = a.shape
    N = b.shape[0] if b_transposed else b.shape[1]
    tm = _pick(M, (640, 128))
    tn = _pick(N, (2816, 1024, 640, 512, 256, 128))
    tk = K if K <= 3200 else _pick(K, (2816, 1024, 640, 512, 256, 128))
    nk = K // tk

    def body(*refs):
        a_ref, b_ref, o_ref = refs[:3]
        acc_ref = refs[-1] if nk > 1 else None
        if b_transposed:
            prod = _dot_nt(a_ref[...], b_ref[...])
        else:
            prod = jnp.dot(a_ref[...], b_ref[...], preferred_element_type=F32)
        if nk == 1:
            o_ref[...] = prod.astype(out_dtype)
        else:
            k = pl.program_id(2)

            @pl.when(k == 0)
            def _():
                acc_ref[...] = prod

            @pl.when(k > 0)
            def _():
                acc_ref[...] += prod

            @pl.when(k == nk - 1)
            def _():
                o_ref[...] = acc_ref[...].astype(out_dtype)

    b_spec = (pl.BlockSpec((tn, tk), lambda i, j, k: (j, k)) if b_transposed
              else pl.BlockSpec((tk, tn), lambda i, j, k: (k, j)))
    return pl.pallas_call(
        body, name=name, grid=(M // tm, N // tn, nk),
        in_specs=[pl.BlockSpec((tm, tk), lambda i, j, k: (i, k)), b_spec],
        out_specs=pl.BlockSpec((tm, tn), lambda i, j, k: (i, j)),
        out_shape=jax.ShapeDtypeStruct((M, N), out_dtype),
        scratch_shapes=[pltpu.VMEM((tm, tn), F32)] if nk > 1 else [],
        compiler_params=_cparams(),
    )(a, b)


def _mm_tn(a, b, *, name):
    T, M = a.shape
    N = b.shape[1]
    tm = _pick(M, (1408, 1024, 512, 256, 128))
    tn = _pick(N, (1024, 640, 512, 256, 128))
    tt = _pick(T, [t for t in (4160, 1664, 640) if 2 * 2 * t * (tm + tn) <= MM_TN_WINDOW_BYTES] + [128])

    def body(a_ref, b_ref, o_ref):
        prod = lax.dot_general(a_ref[...], b_ref[...], (((0,), (0,)), ((), ())), preferred_element_type=F32)

        @pl.when(pl.program_id(2) == 0)
        def _():
            o_ref[...] = prod

        @pl.when(pl.program_id(2) > 0)
        def _():
            o_ref[...] += prod

    return pl.pallas_call(
        body, name=name, grid=(M // tm, N // tn, T // tt),
        in_specs=[pl.BlockSpec((tt, tm), lambda i, j, k: (k, i)), pl.BlockSpec((tt, tn), lambda i, j, k: (k, j))],
        out_specs=pl.BlockSpec((tm, tn), lambda i, j, k: (i, j)),
        out_shape=jax.ShapeDtypeStruct((M, N), F32),
        compiler_params=_cparams(),
    )(a, b)


def _layer_norm(z, g, b):
    mu = jnp.mean(z, axis=-1, keepdims=True)
    xc = z - mu
    var = jnp.mean(xc * xc, axis=-1, keepdims=True)
    return xc * lax.rsqrt(var + LN_EPS) * g + b


def _mm_ln(a, w, res, g, b, *, name, target=None, n_tok=None):
    M, K = a.shape
    D = w.shape[1]
    tm = _pick(M, (640, 128))
    final = target is not None

    def body(*refs):
        if final:
            a_ref, w_ref, r_ref, g_ref, b_ref, t_ref, z_ref, dy_ref, loss_ref = refs
        else:
            a_ref, w_ref, r_ref, g_ref, b_ref, z_ref, h_ref, hb_ref = refs
        z = ALPHA * r_ref[...] + jnp.dot(a_ref[...], w_ref[...], preferred_element_type=F32)
        z_ref[...] = z
        h = _layer_norm(z, g_ref[...], b_ref[...])
        if not final:
            h_ref[...] = h
            hb_ref[...] = h.astype(BF16)
            return
        i = pl.program_id(0)
        row = i * tm + lax.broadcasted_iota(jnp.int32, (tm, 1), 0)
        valid = (row >= N_META) & (row < n_tok)
        err = jnp.where(valid, h - t_ref[...], 0.0)
        dy_ref[...] = err / D

        @pl.when(i == 0)
        def _():
            loss_ref[...] = jnp.zeros_like(loss_ref)

        loss_ref[...] += 0.5 * jnp.sum(jnp.sum(err * err, axis=1, keepdims=True) / D, axis=0, keepdims=True)

    row_blk = lambda cols: pl.BlockSpec((tm, cols), lambda i: (i, 0))
    vec = pl.BlockSpec((1, D), lambda i: (0, 0))
    in_specs = [row_blk(K), pl.BlockSpec((K, D), lambda i: (0, 0)), row_blk(D), vec, vec]
    args = [a, w, res, g, b]
    if final:
        in_specs.append(row_blk(D))
        args.append(target)
        out_specs = [row_blk(D), row_blk(D), pl.BlockSpec((8, LANES), lambda i: (0, 0))]
        out_shape = [jax.ShapeDtypeStruct((M, D), F32)] * 2 + [jax.ShapeDtypeStruct((8, LANES), F32)]
    else:
        out_specs = [row_blk(D)] * 3
        out_shape = [jax.ShapeDtypeStruct((M, D), F32)] * 2 + [jax.ShapeDtypeStruct((M, D), BF16)]
    return pl.pallas_call(
        body, name=name, grid=(M // tm,), in_specs=in_specs, out_specs=out_specs, out_shape=out_shape,
        compiler_params=_cparams(),
    )(*args)


def _ln_bwd(dh, z, g, *, name, dz_next=None):
    M, D = z.shape
    tm = _pick(M, (640, 128))
    has_next = dz_next is not None

    def body(*refs):
        if has_next:
            dh_ref, nx_ref, z_ref, g_ref, dz_ref, dzb_ref, dg_ref, db_ref = refs
            dh_v = dh_ref[...] + ALPHA * nx_ref[...]
        else:
            dh_ref, z_ref, g_ref, dz_ref, dzb_ref, dg_ref, db_ref = refs
            dh_v = dh_ref[...]
        z_v = z_ref[...]
        mu = jnp.mean(z_v, axis=-1, keepdims=True)
        xc = z_v - mu
        rstd = lax.rsqrt(jnp.mean(xc * xc, axis=-1, keepdims=True) + LN_EPS)
        xhat = xc * rstd
        dxhat = dh_v * g_ref[...]
        dz = rstd * (dxhat - jnp.mean(dxhat, axis=-1, keepdims=True)
                     - xhat * jnp.mean(dxhat * xhat, axis=-1, keepdims=True))
        dz_ref[...] = dz
        dzb_ref[...] = dz.astype(BF16)

        @pl.when(pl.program_id(0) == 0)
        def _():
            dg_ref[...] = jnp.zeros_like(dg_ref)
            db_ref[...] = jnp.zeros_like(db_ref)

        dg_ref[...] += jnp.sum(dh_v * xhat, axis=0, keepdims=True)
        db_ref[...] += jnp.sum(dh_v, axis=0, keepdims=True)

    row_blk = pl.BlockSpec((tm, D), lambda i: (i, 0))
    vec = pl.BlockSpec((1, D), lambda i: (0, 0))
    args = [dh] + ([dz_next] if has_next else []) + [z, g]
    in_specs = [row_blk] * (len(args) - 1) + [vec]
    return pl.pallas_call(
        body, name=name, grid=(M // tm,), in_specs=in_specs,
        out_specs=[row_blk, row_blk, vec, vec],
        out_shape=[jax.ShapeDtypeStruct((M, D), F32), jax.ShapeDtypeStruct((M, D), BF16),
                   jax.ShapeDtypeStruct((1, D), F32), jax.ShapeDtypeStruct((1, D), F32)],
        compiler_params=_cparams(),
    )(*args)


def _pool_fwd(h0, pw, scale, g, b, *, name):
    M, D = h0.shape
    n_groups, G, _ = pw.shape
    tm = _pick(M, (640, 128))

    def body(x_ref, halo_ref, pw_ref, sc_ref, g_ref, b_ref, z_ref, h_ref, hb_ref, diff_ref, ext_ref, mix_ref):
        i = pl.program_id(0)
        x = x_ref[...]
        ext_ref[0:POOL_HALO, :] = jnp.where(i == 0, 0.0, halo_ref[...])
        ext_ref[POOL_HALO:, :] = x
        tok = i * tm + lax.broadcasted_iota(jnp.int32, (tm, 1), 0)
        for gi, win in enumerate(POOL_WINDOWS):
            cols = slice(gi * G, (gi + 1) * G)
            xs = x[:, cols]
            s = xs
            for k in range(1, win):
                s = s + ext_ref[pl.ds(POOL_HALO - k, tm), cols]
            count = jnp.minimum(tok + 1, win).astype(F32)
            d = (s / count - xs).astype(BF16)
            diff_ref[:, cols] = d
            mix_ref[:, cols] = jnp.dot(d, pw_ref[gi], preferred_element_type=F32)
        z = ALPHA * x + mix_ref[...] * sc_ref[...]
        z_ref[...] = z
        h = _layer_norm(z, g_ref[...], b_ref[...])
        h_ref[...] = h
        hb_ref[...] = h.astype(BF16)

    row_blk = pl.BlockSpec((tm, D), lambda i: (i, 0))
    vec = pl.BlockSpec((1, D), lambda i: (0, 0))
    halo = pl.BlockSpec((POOL_HALO, D), lambda i: (jnp.maximum(i * (tm // POOL_HALO) - 1, 0), 0))
    return pl.pallas_call(
        body, name=name, grid=(M // tm,),
        in_specs=[row_blk, halo, pl.BlockSpec((n_groups, G, G), lambda i: (0, 0, 0)), vec, vec, vec],
        out_specs=[row_blk] * 4,
        out_shape=[jax.ShapeDtypeStruct((M, D), F32)] * 2 + [jax.ShapeDtypeStruct((M, D), BF16)] * 2,
        scratch_shapes=[pltpu.VMEM((tm + POOL_HALO, D), F32), pltpu.VMEM((tm, D), F32)],
        compiler_params=_cparams(),
    )(h0, h0, pw, scale, g, b)


def _pool_bwd(dz, diff, pw, scale, *, name):
    M, D = dz.shape
    n_groups, G, _ = pw.shape
    tm = _pick(M, (640, 128))
    nt = M // tm

    def body(dz_ref, halo_ref, diff_ref, pw_ref, sc_ref, dh_ref, dpw_ref, dsc_ref, ext_ref, q_ref):
        i = pl.program_id(0)
        dz_v = dz_ref[...]
        ext_ref[0:tm, :] = dz_v
        ext_ref[tm:, :] = jnp.where(i == nt - 1, 0.0, halo_ref[...])
        tok = i * tm + lax.broadcasted_iota(jnp.int32, (tm + POOL_HALO, 1), 0)

        @pl.when(i == 0)
        def _():
            dpw_ref[...] = jnp.zeros_like(dpw_ref)
            dsc_ref[...] = jnp.zeros_like(dsc_ref)

        for gi, win in enumerate(POOL_WINDOWS):
            cols = slice(gi * G, (gi + 1) * G)
            dmix = (ext_ref[:, cols] * sc_ref[:, cols]).astype(BF16)
            ddiff = _dot_nt(dmix, pw_ref[gi])
            count = jnp.minimum(tok + 1, win).astype(F32)
            q_ref[:, cols] = ddiff / count
            acc = -ddiff[0:tm]
            for k in range(win):
                acc = acc + q_ref[pl.ds(k, tm), cols]
            dh_ref[:, cols] = ALPHA * dz_v[:, cols] + acc
            d = diff_ref[:, cols]
            dpw_ref[gi] += lax.dot_general(d, dmix[0:tm], (((0,), (0,)), ((), ())), preferred_element_type=F32)
            mixed = jnp.dot(d, pw_ref[gi], preferred_element_type=F32)
            dsc_ref[:, cols] += jnp.sum(dz_v[:, cols] * mixed, axis=0, keepdims=True)

    row_blk = pl.BlockSpec((tm, D), lambda i: (i, 0))
    vec = pl.BlockSpec((1, D), lambda i: (0, 0))
    per_tile = tm // POOL_HALO
    halo = pl.BlockSpec((POOL_HALO, D), lambda i: (jnp.minimum((i + 1) * per_tile, nt * per_tile - 1), 0))
    wblk = pl.BlockSpec((n_groups, G, G), lambda i: (0, 0, 0))
    return pl.pallas_call(
        body, name=name, grid=(nt,),
        in_specs=[row_blk, halo, row_blk, wblk, vec],
        out_specs=[row_blk, wblk, vec],
        out_shape=[jax.ShapeDtypeStruct((M, D), F32), jax.ShapeDtypeStruct((n_groups, G, G), F32),
                   jax.ShapeDtypeStruct((1, D), F32)],
        scratch_shapes=[pltpu.VMEM((tm + POOL_HALO, D), F32), pltpu.VMEM((tm + POOL_HALO, D), F32)],
        compiler_params=_cparams(),
    )(dz, dz, diff, pw, scale)


def _glu_interleave(x):
    s = x.shape
    n = s[-1] // (2 * GLU_CHUNK)
    return jnp.swapaxes(x.reshape(s[:-1] + (2, n, GLU_CHUNK)), -3, -2).reshape(s)


def _glu_deinterleave(x):
    s = x.shape
    n = s[-1] // (2 * GLU_CHUNK)
    return jnp.swapaxes(x.reshape(s[:-1] + (n, 2, GLU_CHUNK)), -3, -2).reshape(s)


W_PACK_ROWS = 256
W_PACK_PIECE = 64


def _w_in_interleaved(gathered, *, name):
    n_dev, depth, D, width = gathered.shape
    F = n_dev * width // 2
    rows = _pick(D, (W_PACK_ROWS, 128))
    assert width % W_PACK_PIECE == 0 and GLU_CHUNK % W_PACK_PIECE == 0 and F % GLU_CHUNK == 0

    def interleaved(col):
        half, within = divmod(col, F)
        chunk, off = divmod(within, GLU_CHUNK)
        return chunk * 2 * GLU_CHUNK + half * GLU_CHUNK + off

    def body(x_ref, o_ref):
        for j in range(n_dev):
            for q in range(0, width, W_PACK_PIECE):
                to = interleaved(width * j + q)
                o_ref[0, :, to:to + W_PACK_PIECE] = x_ref[j, 0, :, q:q + W_PACK_PIECE]

    return pl.pallas_call(
        body, name=name, grid=(depth, D // rows),
        in_specs=[pl.BlockSpec((n_dev, 1, rows, width), lambda l, r: (0, l, r, 0))],
        out_specs=pl.BlockSpec((1, rows, n_dev * width), lambda l, r: (l, r, 0)),
        out_shape=jax.ShapeDtypeStruct((depth, D, n_dev * width), gathered.dtype),
        compiler_params=_cparams(),
    )(gathered)


def _taps(u_ref, head_ref, r0):
    src, base = (head_ref, CONV_HALO) if r0 == 0 else (u_ref, r0)
    return tuple(src[pl.ds(base - k, CONV_STRIP), :] for k in range(3))


def _conv_glu_fwd(u, cw, cb, *, name):
    M, F2 = u.shape
    tm = _pick(M, (640, 128))
    tc = 2 * GLU_CHUNK

    def body(u_ref, halo_ref, w_ref, b_ref, o_ref, head_ref):
        i = pl.program_id(0)
        head_ref[0:CONV_HALO, :] = jnp.where(i == 0, 0.0, halo_ref[...])
        head_ref[CONV_HALO:, :] = u_ref[0:CONV_STRIP, :]
        w0, w1, w2, b = w_ref[0:1, :], w_ref[1:2, :], w_ref[2:3, :], b_ref[...]
        for r0 in range(0, tm, CONV_STRIP):
            u0, u1, u2 = _taps(u_ref, head_ref, r0)
            c = b + w0 * u2 + w1 * u1 + w2 * u0
            a, g = c[:, :GLU_CHUNK], c[:, GLU_CHUNK:]
            o_ref[pl.ds(r0, CONV_STRIP), :] = (a * jax.nn.sigmoid(a) * g).astype(BF16)

    per_tile = tm // CONV_HALO
    return pl.pallas_call(
        body, name=name, grid=(M // tm, F2 // tc),
        in_specs=[pl.BlockSpec((tm, tc), lambda i, j: (i, j)),
                  pl.BlockSpec((CONV_HALO, tc), lambda i, j: (jnp.maximum(i * per_tile - 1, 0), j)),
                  pl.BlockSpec((3, tc), lambda i, j: (0, j)), pl.BlockSpec((1, tc), lambda i, j: (0, j))],
        out_specs=pl.BlockSpec((tm, GLU_CHUNK), lambda i, j: (i, j)),
        out_shape=jax.ShapeDtypeStruct((M, F2 // 2), BF16),
        scratch_shapes=[pltpu.VMEM((CONV_HALO + CONV_STRIP, tc), F32)],
        compiler_params=_cparams(),
    )(u, u, cw, cb)


def _conv_glu_bwd(u, dact, cw, cb, *, name):
    M, F2 = u.shape
    tm = _pick(M, (640, 128))
    nt = M // tm
    tc = 2 * GLU_CHUNK

    def body(u_ref, halo_ref, da_ref, w_ref, b_ref, du_ref, dw_ref, db_ref, head_ref, dcx_ref, carry_ref):
        i = pl.program_id(1)
        head_ref[0:CONV_HALO, :] = jnp.where(i == nt - 1, 0.0, halo_ref[...])
        head_ref[CONV_HALO:, :] = u_ref[0:CONV_STRIP, :]
        w0, w1, w2, b = w_ref[0:1, :], w_ref[1:2, :], w_ref[2:3, :], b_ref[...]

        @pl.when(i == 0)
        def _():
            dw_ref[...] = jnp.zeros_like(dw_ref)
            db_ref[...] = jnp.zeros_like(db_ref)
            carry_ref[...] = jnp.zeros_like(carry_ref)

        def fold(t):
            return sum(t[r:r + 8] for r in range(0, CONV_STRIP, 8))

        dcx_ref[tm:, :] = carry_ref[...]
        s_b = s_0 = s_1 = s_2 = jnp.zeros((8, tc), F32)
        for r0 in reversed(range(0, tm, CONV_STRIP)):
            rows = pl.ds(r0, CONV_STRIP)
            u0, u1, u2 = _taps(u_ref, head_ref, r0)
            c = b + w0 * u2 + w1 * u1 + w2 * u0
            a, g = c[:, :GLU_CHUNK], c[:, GLU_CHUNK:]
            sig = jax.nn.sigmoid(a)
            dact_v = da_ref[rows, :]
            d_a = dact_v * g * (sig * (1.0 + a * (1.0 - sig)))
            d_g = dact_v * (a * sig)
            dc = jnp.concatenate([d_a, d_g], axis=1)
            dcx_ref[rows, :] = dc
            du = w2 * dc + w1 * dcx_ref[pl.ds(r0 + 1, CONV_STRIP), :] + w0 * dcx_ref[pl.ds(r0 + 2, CONV_STRIP), :]
            du_ref[rows, :] = du.astype(BF16)
            s_b, s_0, s_1, s_2 = s_b + fold(dc), s_0 + fold(dc * u2), s_1 + fold(dc * u1), s_2 + fold(dc * u0)
        carry_ref[...] = dcx_ref[0:CONV_HALO, :]
        db_ref[...] += jnp.sum(s_b, axis=0, keepdims=True)
        dw_ref[0:1, :] += jnp.sum(s_0, axis=0, keepdims=True)
        dw_ref[1:2, :] += jnp.sum(s_1, axis=0, keepdims=True)
        dw_ref[2:3, :] += jnp.sum(s_2, axis=0, keepdims=True)

    per_tile = tm // CONV_HALO
    rev = lambda i: nt - 1 - i
    return pl.pallas_call(
        body, name=name, grid=(F2 // tc, nt),
        in_specs=[pl.BlockSpec((tm, tc), lambda j, i: (rev(i), j)),
                  pl.BlockSpec((CONV_HALO, tc), lambda j, i: (jnp.maximum(rev(i) * per_tile - 1, 0), j)),
                  pl.BlockSpec((tm, GLU_CHUNK), lambda j, i: (rev(i), j)),
                  pl.BlockSpec((3, tc), lambda j, i: (0, j)), pl.BlockSpec((1, tc), lambda j, i: (0, j))],
        out_specs=[pl.BlockSpec((tm, tc), lambda j, i: (rev(i), j)),
                   pl.BlockSpec((3, tc), lambda j, i: (0, j)), pl.BlockSpec((1, tc), lambda j, i: (0, j))],
        out_shape=[jax.ShapeDtypeStruct((M, F2), BF16), jax.ShapeDtypeStruct((3, F2), F32),
                   jax.ShapeDtypeStruct((1, F2), F32)],
        scratch_shapes=[pltpu.VMEM((CONV_HALO + CONV_STRIP, tc), F32), pltpu.VMEM((tm + CONV_HALO, tc), F32),
                        pltpu.VMEM((CONV_HALO, tc), F32)],
        compiler_params=_cparams(),
    )(u, u, dact, cw, cb)


def _split3(x):
    hi = x.astype(BF16)
    r = x - hi.astype(F32)
    mid = r.astype(BF16)
    lo = (r - mid.astype(F32)).astype(BF16)
    return hi, mid, lo


def _tri_sum(tri, x):
    return sum(jnp.dot(tri, part, preferred_element_type=F32) for part in _split3(x))


def _log_sigmoid(x):
    return jnp.minimum(x, 0.0) - jnp.log1p(jnp.exp(-jnp.abs(x)))


def _forget_cumsum(pre, bias, *, name):
    M, C = pre.shape
    tm = _pick(M, (640, 128))

    def body(p_ref, b_ref, c_ref, carry_ref):
        i = pl.program_id(0)

        @pl.when(i == 0)
        def _():
            carry_ref[...] = jnp.zeros_like(carry_ref)

        logf = _log_sigmoid(p_ref[...] + b_ref[...])
        r = lax.broadcasted_iota(jnp.int32, (tm, tm), 0)
        s = lax.broadcasted_iota(jnp.int32, (tm, tm), 1)
        c_ref[...] = _tri_sum((s <= r).astype(BF16), logf) + carry_ref[...]
        carry_ref[...] = c_ref[pl.ds(tm - 1, 1), :]

    return pl.pallas_call(
        body, name=name, grid=(M // tm,),
        in_specs=[pl.BlockSpec((tm, C), lambda i: (i, 0)), pl.BlockSpec((1, C), lambda i: (0, 0))],
        out_specs=pl.BlockSpec((tm, C), lambda i: (i, 0)),
        out_shape=jax.ShapeDtypeStruct((M, C), F32),
        scratch_shapes=[pltpu.VMEM((1, C), F32)],
        compiler_params=_cparams(),
    )(pre, bias)


def _forget_cumsum_bwd(dc, pre, bias, *, name):
    M, C = pre.shape
    tm = _pick(M, (640, 128))
    nt = M // tm

    def body(dc_ref, p_ref, b_ref, dp_ref, db_ref, carry_ref, run_ref):
        i = pl.program_id(0)

        @pl.when(i == 0)
        def _():
            carry_ref[...] = jnp.zeros_like(carry_ref)
            db_ref[...] = jnp.zeros_like(db_ref)

        r = lax.broadcasted_iota(jnp.int32, (tm, tm), 0)
        s = lax.broadcasted_iota(jnp.int32, (tm, tm), 1)
        run_ref[...] = _tri_sum((s >= r).astype(BF16), dc_ref[...]) + carry_ref[...]
        carry_ref[...] = run_ref[pl.ds(0, 1), :]
        dpre = run_ref[...] * jax.nn.sigmoid(-(p_ref[...] + b_ref[...]))
        dp_ref[...] = dpre.astype(BF16)
        db_ref[...] += jnp.sum(dpre, axis=0, keepdims=True)

    rev_blk = pl.BlockSpec((tm, C), lambda i: (nt - 1 - i, 0))
    vec = pl.BlockSpec((1, C), lambda i: (0, 0))
    return pl.pallas_call(
        body, name=name, grid=(nt,),
        in_specs=[rev_blk, rev_blk, vec], out_specs=[rev_blk, vec],
        out_shape=[jax.ShapeDtypeStruct((M, C), BF16), jax.ShapeDtypeStruct((1, C), F32)],
        scratch_shapes=[pltpu.VMEM((1, C), F32), pltpu.VMEM((tm, C), F32)],
        compiler_params=_cparams(),
    )(dc, pre, bias)


def _causal_mask(tm):
    key = lax.broadcasted_iota(jnp.int32, (tm, tm), 0)
    query = lax.broadcasted_iota(jnp.int32, (tm, tm), 1)
    return key <= query


def _dot_nt(a, b):
    return lax.dot_general(a, b, (((1,), (1,)), ((), ())), preferred_element_type=F32)


def _loop_unrolled(n, step, init, unroll):
    def trip(p, carry):
        for r in range(unroll):
            carry = step(unroll * p + r, carry)
        return carry
    carry = lax.fori_loop(0, n // unroll, trip, init)
    return lax.fori_loop(unroll * (n // unroll), n, step, carry)


def _rows(main, extras, total):
    tm = main.shape[1]
    used = sum(e.shape[0] for e in extras)
    tile = jnp.concatenate([e.astype(BF16) for e in extras] + [jnp.zeros((ATTN_EXTRA - used, tm), BF16)], axis=0)
    rest = total - main.shape[0] - ATTN_EXTRA
    return jnp.concatenate([main, tile] + ([jnp.zeros((rest, tm), BF16)] if rest else []), axis=0)


def _attn_specs(H, M, dh, tm):
    nt = M // tm
    qT_blk = pl.BlockSpec((1, dh, tm), lambda h, i: (h, 0, i))
    row_blk = pl.BlockSpec((1, 1, tm), lambda h, i: (h, 0, i))
    key_blk = pl.BlockSpec((1, M, LANES), lambda h, i: (h, 0, 0))
    keyT_blk = pl.BlockSpec((1, nt, dh + ATTN_EXTRA, tm), lambda h, i: (h, 0, 0, 0))
    return qT_blk, row_blk, key_blk, keyT_blk


def _pair_specs(M, dh, tm):
    assert 2 * dh == LANES
    cols = lambda section: pl.BlockSpec((tm, LANES), lambda p, i: (i, section + p))
    headsT = pl.BlockSpec((2, dh, tm), lambda p, i: (p, 0, i))
    return cols, headsT


def _attn_pack(qkv, c, H, *, name):
    M, D3 = qkv.shape
    D = D3 // 3
    dh = D // H
    tm = _pick(M, (640, 128))
    nt = M // tm
    cols, headsT = _pair_specs(M, dh, tm)
    key_blk = pl.BlockSpec((2, tm, LANES), lambda p, i: (p, i, 0))
    keyT_blk = pl.BlockSpec((2, 1, dh + ATTN_EXTRA, tm), lambda p, i: (p, i, 0, 0))

    def body(k_ref, v_ref, q_ref, c_ref, ka_ref, va_ref, qT_ref, kT_ref, vT_ref):
        p = pl.program_id(0)
        lane = lax.broadcasted_iota(jnp.int32, (tm, LANES), 1)
        k_v, v_v = k_ref[...], v_ref[...]
        ones_v = jnp.where((lane >= dh) & (lane < dh + 3), 1.0, 0.0)
        for e in range(2):
            ck = jnp.sum(jnp.where(lane == 2 * p + e, c_ref[...], 0.0), axis=1, keepdims=True)
            hi, mid, lo = _split3(-ck)
            extra = jnp.where(lane == dh + 3, hi.astype(F32), jnp.where(lane == dh + 4, mid.astype(F32),
                              jnp.where(lane == dh + 5, lo.astype(F32), ones_v)))
            first = lambda t: t if e == 0 else pltpu.roll(t, dh, 1)
            ka_ref[e] = jnp.where(lane < dh, first(k_v), extra.astype(BF16))
            va_ref[e] = jnp.where(lane < dh, first(v_v), ones_v.astype(BF16))
        row = lax.broadcasted_iota(jnp.int32, (ATTN_EXTRA, tm), 0)
        tail = jnp.where(row == 0, 1.0, 0.0).astype(BF16)
        for src, dst in ((k_v, kT_ref), (v_v, vT_ref)):
            t = src.T
            for e in range(2):
                dst[e, 0, 0:dh, :] = t[e * dh:(e + 1) * dh]
                dst[e, 0, dh:, :] = tail
        qT_ref[...] = q_ref[...].T.reshape(2, dh, tm)

    n_sec = D // LANES
    return pl.pallas_call(
        body, name=name, grid=(H // 2, nt),
        in_specs=[cols(0), cols(n_sec), cols(2 * n_sec), pl.BlockSpec((tm, LANES), lambda p, i: (i, 0))],
        out_specs=[key_blk, key_blk, headsT, keyT_blk, keyT_blk],
        out_shape=[jax.ShapeDtypeStruct((H, M, LANES), BF16)] * 2 + [jax.ShapeDtypeStruct((H, dh, M), BF16)]
        + [jax.ShapeDtypeStruct((H, nt, dh + ATTN_EXTRA, tm), BF16)] * 2,
        compiler_params=_cparams(),
    )(qkv, qkv, qkv, c)


def _heads_split(x, H, *, name):
    M, D = x.shape
    dh = D // H
    tm = _pick(M, (640, 128))
    cols, headsT = _pair_specs(M, dh, tm)

    def body(x_ref, o_ref):
        o_ref[...] = x_ref[...].T.reshape(2, dh, tm)

    return pl.pallas_call(
        body, name=name, grid=(H // 2, M // tm), in_specs=[cols(0)], out_specs=headsT,
        out_shape=jax.ShapeDtypeStruct((H, dh, M), x.dtype), compiler_params=_cparams(),
    )(x)


def _heads_merge(parts, M, tm, *, name):
    H = parts[0].shape[0]
    dh = LANES // 2
    blocked = [t.ndim == 4 for t in parts]
    cols, headsT = _pair_specs(M, dh, tm)

    def body(*refs):
        for src, dst, b in zip(refs[:len(parts)], refs[len(parts):], blocked):
            t = jnp.concatenate([src[e, 0, 0:dh, :] for e in range(2)], axis=0) if b else src[...].reshape(2 * dh, tm)
            dst[...] = t.astype(F32).T.astype(BF16)

    in_specs = [pl.BlockSpec((2, 1, t.shape[2], tm), lambda p, i: (p, i, 0, 0)) if b else headsT
                for t, b in zip(parts, blocked)]
    return pl.pallas_call(
        body, name=name, grid=(H // 2, M // tm), in_specs=in_specs, out_specs=[cols(0)] * len(parts),
        out_shape=[jax.ShapeDtypeStruct((M, H * dh), BF16)] * len(parts), compiler_params=_cparams(),
    )(*parts)


def _attn_fwd(qT, cq, k_aug, vT_aug, *, name):
    H, dh, M = qT.shape
    tm = vT_aug.shape[-1]
    qT_blk, row_blk, key_blk, keyT_blk = _attn_specs(H, M, dh, tm)

    def body(qT_ref, cq_ref, k_ref, vT_ref, oT_ref, lse_ref):
        i = pl.program_id(1)
        ones = jnp.ones((3, tm), BF16)
        qa = _rows(qT_ref[0] * jnp.asarray(dh ** -0.5, BF16), [*_split3(cq_ref[0]), ones], LANES)

        def block(j, carry, masked):
            m, acc = carry
            keys = pl.ds(pl.multiple_of(j * tm, tm), tm)
            sT = jnp.dot(k_ref[0, keys, :], qa, preferred_element_type=F32)
            if masked:
                sT = jnp.where(_causal_mask(tm), sT, NEG_INF)
            m_new = jnp.maximum(m, jnp.max(sT, axis=0, keepdims=True))
            pT = jnp.exp(sT - m_new).astype(BF16)
            acc = jnp.exp(m - m_new) * acc + jnp.dot(vT_ref[0, j], pT, preferred_element_type=F32)
            return m_new, acc

        init = (jnp.full((1, tm), NEG_INF, F32), jnp.zeros((dh + ATTN_EXTRA, tm), F32))
        m, acc = block(i, _loop_unrolled(i, lambda j, c: block(j, c, False), init, 4), True)
        l = acc[dh:dh + 1, :]
        oT_ref[0] = (acc[0:dh, :] / l).astype(BF16)
        lse_ref[0] = m + jnp.log(l)

    return pl.pallas_call(
        body, name=name, grid=(H, M // tm),
        in_specs=[qT_blk, row_blk, key_blk, keyT_blk], out_specs=[qT_blk, row_blk],
        out_shape=[jax.ShapeDtypeStruct((H, dh, M), BF16), jax.ShapeDtypeStruct((H, 1, M), F32)],
        compiler_params=_cparams(),
    )(qT, cq, k_aug, vT_aug)


def _attn_bwd(qT, oT, doT, lse, cq, k_aug, v_aug, kT_aug, *, name):
    H, dh, M = qT.shape
    tm = kT_aug.shape[-1]
    qT_blk, row_blk, key_blk, keyT_blk = _attn_specs(H, M, dh, tm)
    dvT_blk = pl.BlockSpec((1, M // tm, dh, tm), lambda h, i: (h, 0, 0, 0))

    def body(qT_ref, oT_ref, doT_ref, lse_ref, cq_ref, k_ref, v_ref, kT_ref, dqT_ref, dcq_ref, dkT_ref, dvT_ref):
        i = pl.program_id(1)

        @pl.when(i == 0)
        def _():
            dkT_ref[...] = jnp.zeros_like(dkT_ref)
            dvT_ref[...] = jnp.zeros_like(dvT_ref)

        qsT = qT_ref[0] * jnp.asarray(dh ** -0.5, BF16)
        doT = doT_ref[0]
        delta = jnp.sum(doT.astype(F32) * oT_ref[0].astype(F32), axis=0, keepdims=True)
        ones = jnp.ones((3, tm), BF16)
        qa = _rows(qsT, [*_split3(cq_ref[0] - lse_ref[0]), ones], LANES)
        da = _rows(doT, _split3(-delta), LANES)
        q1 = _rows(qsT, [ones[0:1]], dh + ATTN_EXTRA)

        def block(j, dq, masked):
            keys = pl.ds(pl.multiple_of(j * tm, tm), tm)
            pT = jnp.exp(jnp.dot(k_ref[0, keys, :], qa, preferred_element_type=F32))
            if masked:
                pT = jnp.where(_causal_mask(tm), pT, 0.0)
            ds_b = (pT * jnp.dot(v_ref[0, keys, :], da, preferred_element_type=F32)).astype(BF16)
            dvT_ref[0, j] += _dot_nt(doT, pT.astype(BF16))
            dkT_ref[0, j] += _dot_nt(q1, ds_b)
            return dq + jnp.dot(kT_ref[0, j], ds_b, preferred_element_type=F32)

        init = jnp.zeros((dh + ATTN_EXTRA, tm), F32)
        dq = block(i, _loop_unrolled(i, lambda j, c: block(j, c, False), init, 2), True)
        dqT_ref[0] = (dq[0:dh, :] * dh ** -0.5).astype(BF16)
        dcq_ref[0] = dq[dh:dh + 1, :]

    return pl.pallas_call(
        body, name=name, grid=(H, M // tm),
        in_specs=[qT_blk, qT_blk, qT_blk, row_blk, row_blk, key_blk, key_blk, keyT_blk],
        out_specs=[qT_blk, row_blk, keyT_blk, dvT_blk],
        out_shape=[jax.ShapeDtypeStruct((H, dh, M), BF16), jax.ShapeDtypeStruct((H, 1, M), F32),
                   jax.ShapeDtypeStruct(kT_aug.shape, F32), jax.ShapeDtypeStruct((H, M // tm, dh, tm), F32)],
        compiler_params=_cparams(),
    )(qT, oT, doT, lse, cq, k_aug, v_aug, kT_aug)


def kernel(x, meta, pool_w, pool_scale, w_kv, w_f, b_f, w_q, w_o, ffn_w_in, ffn_conv_w, ffn_conv_b, ffn_w_out, ln_g, ln_b, loss_target, m_meta, m_pool_w, m_pool_scale, m_w_kv, m_w_f, m_b_f, m_w_q, m_w_o, m_ffn_w_in, m_ffn_conv_w, m_ffn_conv_b, m_ffn_w_out, m_ln_g, m_ln_b, v_meta, v_pool_w, v_pool_scale, v_w_kv, v_w_f, v_b_f, v_w_q, v_w_o, v_ffn_w_in, v_ffn_conv_w, v_ffn_conv_b, v_ffn_w_out, v_ln_g, v_ln_b):
    local = dict(meta=meta, pool_w=pool_w, pool_scale=pool_scale, w_kv=w_kv, w_f=w_f, b_f=b_f, w_q=w_q, w_o=w_o,
                 ffn_w_in=ffn_w_in, ffn_conv_w=ffn_conv_w, ffn_conv_b=ffn_conv_b, ffn_w_out=ffn_w_out,
                 ln_g=ln_g, ln_b=ln_b)
    mom1 = dict(meta=m_meta, pool_w=m_pool_w, pool_scale=m_pool_scale, w_kv=m_w_kv, w_f=m_w_f, b_f=m_b_f,
                w_q=m_w_q, w_o=m_w_o, ffn_w_in=m_ffn_w_in, ffn_conv_w=m_ffn_conv_w, ffn_conv_b=m_ffn_conv_b,
                ffn_w_out=m_ffn_w_out, ln_g=m_ln_g, ln_b=m_ln_b)
    mom2 = dict(meta=v_meta, pool_w=v_pool_w, pool_scale=v_pool_scale, w_kv=v_w_kv, w_f=v_w_f, b_f=v_b_f,
                w_q=v_w_q, w_o=v_w_o, ffn_w_in=v_ffn_w_in, ffn_conv_w=v_ffn_conv_w, ffn_conv_b=v_ffn_conv_b,
                ffn_w_out=v_ffn_w_out, ln_g=v_ln_g, ln_b=v_ln_b)
    axis_of = dict(PARAMS)

    S, D = x.shape[1], x.shape[2]
    H = b_f.shape[0]
    dh = D // H
    n_tok = N_META + S
    M = _round_up(n_tok, LANES)
    tm = _pick(M, (640, 128))
    nt = M // tm
    F2 = ffn_conv_b.shape[1]
    F = F2 // 2
    depth = ffn_conv_b.shape[0]

    small_sharded = [n for n in SMALL if axis_of[n] is not None]
    got = _all_gather([local[n].astype(BF16) for n in BIG] + [_pack([local[n] for n in small_sharded])],
                      "gather_weights")
    wb = {n: _from_blocks(blk, axis_of[n]) for n, blk in zip(BIG, got) if n != "ffn_w_in"}
    small_blocks = _unpack(got[-1], [local[n].shape for n in small_sharded], lead=(N_DEV,))
    wf32 = {n: _from_blocks(blk, axis_of[n]) for n, blk in zip(small_sharded, small_blocks)}

    pw = wb["pool_w"][0]
    wf_pad = jnp.pad(wf32["w_f"].astype(BF16), ((0, 0), (0, LANES - H)))
    w_qkv = jnp.concatenate([wb["w_kv"], wb["w_q"][0]], axis=1)
    w_att = jnp.concatenate([w_qkv, wf_pad], axis=1)
    wo = wb["w_o"][0]
    w_in = _w_in_interleaved(got[BIG.index("ffn_w_in")], name="w_in_interleave")
    w_out = wb["ffn_w_out"]
    conv_w = _glu_interleave(wf32["ffn_conv_w"])
    conv_b = _glu_interleave(ffn_conv_b)[:, None, :]
    scale = wf32["pool_scale"]
    g_ln, b_ln = wf32["ln_g"], wf32["ln_b"]
    ln = lambda i, j: (g_ln[i, j][None, :], b_ln[i, j][None, :])
    bias_f = jnp.pad(b_f, (0, LANES - H))[None, :]

    pad_rows = M - n_tok
    h0 = jnp.concatenate([wf32["meta"], x[0], jnp.zeros((pad_rows, D), F32)], axis=0)
    target = jnp.concatenate([jnp.zeros((N_META, D), F32), loss_target[0], jnp.zeros((pad_rows, D), F32)], axis=0)

    z1, h1, h1b, diff = _pool_fwd(h0, pw, scale, *ln(0, 0), name="pool_fwd")
    u0 = _mm(h1b, w_in[0], name="ffn0_up", out_dtype=F32)
    act0 = _conv_glu_fwd(u0, conv_w[0], conv_b[0], name="ffn0_conv")
    z2, h2, h2b = _mm_ln(act0, w_out[0], h1, *ln(0, 1), name="ffn0_down_ln")

    qkv = _mm(h2b, w_qkv, name="attn_qkv", out_dtype=BF16)
    pre = _mm(h2b, wf_pad, name="attn_gate", out_dtype=F32)
    c = _forget_cumsum(pre, bias_f, name="attn_cumsum")
    k_aug, v_aug, qT_h, kT_aug, vT_aug = _attn_pack(qkv, c, H, name="attn_pack")
    cq = c[:, :H].T[:, None, :]
    oT_h, lse = _attn_fwd(qT_h, cq, k_aug, vT_aug, name="attn_fwd")
    o, = _heads_merge([oT_h], M, tm, name="attn_o_merge")
    z3, h3, h3b = _mm_ln(o, wo, h2, *ln(1, 0), name="attn_out_ln")

    u1 = _mm(h3b, w_in[1], name="ffn1_up", out_dtype=F32)
    act1 = _conv_glu_fwd(u1, conv_w[1], conv_b[1], name="ffn1_conv")
    z4, dy, loss_part = _mm_ln(act1, w_out[1], h3, *ln(1, 1), name="ffn1_down_loss", target=target, n_tok=n_tok)
    loss = lax.psum(loss_part[0, 0], ("x", "y", "c"))

    grads = {}

    def ffn_bwd(layer, dz_b, u, act, h_in_b, tag):
        dact = _mm(dz_b, w_out[layer], name=tag + "_dact", out_dtype=F32, b_transposed=True)
        du, dcw, dcb = _conv_glu_bwd(u, dact, conv_w[layer], conv_b[layer], name=tag + "_conv_bwd")
        dh = _mm(du, w_in[layer], name=tag + "_dh", out_dtype=F32, b_transposed=True)
        d_w_out = _mm_tn(act, dz_b, name=tag + "_dw_out")
        d_w_in = _glu_deinterleave(_mm_tn(h_in_b, du, name=tag + "_dw_in"))
        return dh, d_w_in, d_w_out, _glu_deinterleave(dcw), _glu_deinterleave(dcb)[0]

    dz4, dz4b, dg11, db11 = _ln_bwd(dy, z4, g_ln[1, 1][None, :], name="ln4_bwd")
    dh3, dwin1, dwout1, dcw1, dcb1 = ffn_bwd(1, dz4b, u1, act1, h3b, "ffn1")
    dz3, dz3b, dg10, db10 = _ln_bwd(dh3, z3, g_ln[1, 0][None, :], name="ln3_bwd", dz_next=dz4)

    do = _mm(dz3b, wo, name="attn_do", out_dtype=BF16, b_transposed=True)
    grads["w_o"] = _mm_tn(o, dz3b, name="attn_dw_o")[None]
    doT_h = _heads_split(do, H, name="attn_do_split")
    dqT_h, dcq, dkT_a, dvT_h = _attn_bwd(qT_h, oT_h, doT_h, lse, cq, k_aug, v_aug, kT_aug, name="attn_bwd")
    dck = -dkT_a[:, :, dh, :].reshape(H, M)
    dc = jnp.pad((dcq[:, 0, :] + dck).T, ((0, 0), (0, LANES - H)))
    dpre, dbias = _forget_cumsum_bwd(dc, pre, bias_f, name="attn_cumsum_bwd")
    d_att = jnp.concatenate([*_heads_merge([dkT_a, dvT_h, dqT_h], M, tm, name="attn_dqkv_merge"), dpre], axis=1)
    dh2 = _mm(d_att, w_att, name="attn_dh", out_dtype=F32, b_transposed=True)
    d_w_att = _mm_tn(h2b, d_att, name="attn_dw_qkv")
    grads["w_kv"] = d_w_att[:, :2 * D]
    grads["w_q"] = d_w_att[:, 2 * D:3 * D][None]
    grads["w_f"] = d_w_att[:, 3 * D:3 * D + H]
    grads["b_f"] = dbias[0, :H]
    dz2, dz2b, dg01, db01 = _ln_bwd(dh2, z2, g_ln[0, 1][None, :], name="ln2_bwd", dz_next=dz3)

    dh1, dwin0, dwout0, dcw0, dcb0 = ffn_bwd(0, dz2b, u0, act0, h1b, "ffn0")
    dz1, _, dg00, db00 = _ln_bwd(dh1, z1, g_ln[0, 0][None, :], name="ln1_bwd", dz_next=dz2)
    dh0, dpw, dscale = _pool_bwd(dz1, diff, pw, scale, name="pool_bwd")

    grads["meta"] = dh0[:N_META]
    grads["pool_w"] = dpw[None]
    grads["pool_scale"] = dscale
    grads["ffn_w_in"] = jnp.stack([dwin0, dwin1])
    grads["ffn_w_out"] = jnp.stack([dwout0, dwout1])
    grads["ffn_conv_w"] = jnp.stack([dcw0, dcw1])
    grads["ffn_conv_b"] = jnp.stack([dcb0, dcb1])
    grads["ln_g"] = jnp.stack([jnp.stack([dg00[0], dg01[0]]), jnp.stack([dg10[0], dg11[0]])])
    grads["ln_b"] = jnp.stack([jnp.stack([db00[0], db01[0]]), jnp.stack([db10[0], db11[0]])])
    grad_x = dh0[N_META:n_tok][None]

    blocks = {n: _to_blocks(grads[n], axis_of[n]) for n, _ in PARAMS}
    me = _coords()
    core, chip = (jnp.reshape(v, (1,)).astype(jnp.int32) for v in (me[2], 2 * me[0] + me[1]))
    by_core = lambda t: jnp.swapaxes(t.reshape((N_CHIPS, 2) + t.shape[1:]), 0, 1).astype(BF16)
    segs = [by_core(blocks[n]) for n in BIG]
    pair = _exchange(segs, "grad_pair_exchange", (1,), lambda pos: pos[2])
    rows_of = lambda t, lead: t.reshape(lead + (-1, t.shape[-1]))
    partial = [_pair_sum(rows_of(s, (2,)), core, rows_of(p[0], ()), name="grad_pair_sum_" + n).reshape(s.shape[1:])
               for n, s, p in zip(BIG, segs, pair)]
    recv = _exchange(partial, "grad_chip_exchange", (4, 2, 6), lambda pos: 2 * pos[0] + pos[1])
    recv_small = _exchange([_pack([blocks[n] for n in SMALL], lead=(N_DEV,))], "grad_small_exchange",
                           tuple(range(1, N_DEV)), _index, by_sender=True)[0]
    results = {}
    for n, own, r in zip(BIG, partial, recv):
        shape = local[n].shape
        as2d = lambda t: t.reshape(-1, shape[-1])
        outs = _adamw(rows_of(own, (N_CHIPS,)), chip, rows_of(r, (N_CHIPS - 1,)), as2d(local[n]), as2d(mom1[n]),
                      as2d(mom2[n]), name="adamw_" + n)
        results[n] = [o_.reshape(shape) for o_ in outs]
    outs = _adamw(recv_small, jnp.zeros((1,), jnp.int32), recv_small[1:],
                  *[_pack([d[n] for n in SMALL]) for d in (local, mom1, mom2)], name="adamw_small")
    small_out = [_unpack(o_, [local[n].shape for n in SMALL]) for o_ in outs]
    for i, n in enumerate(SMALL):
        results[n] = [small_out[k][i] for k in range(4)]
    return (loss, grad_x, *[results[n][k] for k in range(4) for n, _ in PARAMS])
```

```python
import jax
import jax.numpy as jnp
from jax import lax
from jax.experimental import pallas as pl
from jax.experimental.pallas import tpu as pltpu

F32, BF16 = jnp.float32, jnp.bfloat16
MESH = pl.DeviceIdType.MESH

N_DEV = 8
N_CHIPS = 4
N_META = 16
POOL_WINDOWS = (2, 4, 8, 16)
POOL_HALO = 16
CONV_HALO = 8
CONV_STRIP = 16
ALPHA = 4.0 ** 0.25
LN_EPS = 1e-5
NEG_INF = -1e30
ADAM_LR, ADAM_B1, ADAM_B2, ADAM_EPS, ADAM_WD, ADAM_STEP = 0.001, 0.9, 0.999, 1e-08, 0.01, 10

LANES = 128
PACK_COLS = 1024
ADAM_TILE_BYTES = 4 << 20
MM_TN_WINDOW_BYTES = 28 << 20
GLU_CHUNK = 256
ATTN_EXTRA = 16
VMEM_LIMIT = 56 * 1024 * 1024

PARAMS = (("meta", 1), ("pool_w", 2), ("pool_scale", 1), ("w_kv", 1), ("w_f", 0), ("b_f", None),
          ("w_q", 1), ("w_o", 1), ("ffn_w_in", 2), ("ffn_conv_w", 2), ("ffn_conv_b", None),
          ("ffn_w_out", 1), ("ln_g", 2), ("ln_b", 2))
BIG = ("pool_w", "w_kv", "w_q", "w_o", "ffn_w_in", "ffn_w_out")
SMALL = ("meta", "pool_scale", "w_f", "b_f", "ffn_conv_w", "ffn_conv_b", "ln_g", "ln_b")


def _cparams(**kw):
    return pltpu.CompilerParams(vmem_limit_bytes=VMEM_LIMIT, **kw)


def _pick(n, cands):
    for c in cands:
        if n % c == 0:
            return c
    return n


def _round_up(n, m):
    return (n + m - 1) // m * m


def _pack(pieces, lead=()):
    flat = []
    for p in pieces:
        v = p.reshape(lead + (-1,))
        flat.append(jnp.pad(v, [(0, 0)] * len(lead) + [(0, _round_up(v.shape[-1], PACK_COLS) - v.shape[-1])]))
    v = jnp.concatenate(flat, axis=-1)
    rows = _round_up(v.shape[-1] // PACK_COLS, 8)
    v = jnp.pad(v, [(0, 0)] * len(lead) + [(0, rows * PACK_COLS - v.shape[-1])])
    return v.reshape(lead + (rows, PACK_COLS))


def _unpack(buf, shapes, lead=()):
    flat = buf.reshape(lead + (-1,))
    out, off = [], 0
    for s in shapes:
        n = 1
        for d in s:
            n *= d
        out.append(flat[..., off:off + n].reshape(lead + tuple(s)))
        off += _round_up(n, PACK_COLS)
    return out


def _to_blocks(full, axis):
    if axis is None:
        return jnp.broadcast_to(full[None], (N_DEV,) + full.shape)
    s = full.shape
    x = full.reshape(s[:axis] + (N_DEV, s[axis] // N_DEV) + s[axis + 1:])
    return jnp.moveaxis(x, axis, 0)


def _from_blocks(blocks, axis):
    x = jnp.moveaxis(blocks, 0, axis)
    s = x.shape
    return x.reshape(s[:axis] + (s[axis] * s[axis + 1],) + s[axis + 2:])


def _coords():
    return lax.axis_index("x"), lax.axis_index("y"), lax.axis_index("c")


def _flip(pos, mask):
    x, y, c = pos
    return (1 - x if mask & 4 else x, 1 - y if mask & 2 else y, 1 - c if mask & 1 else c)


def _index(pos):
    x, y, c = pos
    return 4 * x + 2 * y + c


def _comm_call(body, name, arrays, out_shapes):
    n = len(arrays)
    hbm = pl.BlockSpec(memory_space=pl.ANY)
    return pl.pallas_call(
        body, name=name, out_shape=out_shapes, in_specs=[hbm] * n, out_specs=[hbm] * n,
        scratch_shapes=[pltpu.SemaphoreType.DMA((7 * n,)), pltpu.SemaphoreType.DMA((7 * n,)),
                        pltpu.SemaphoreType.DMA((n,))],
    )(*arrays)


def _all_gather(blocks, name):
    chip_masks = (4, 2, 6)
    n = len(blocks)

    def body(*refs):
        x_refs, out_refs = refs[:n], refs[n:2 * n]
        send_sems, recv_sems, local_sems = refs[2 * n:]
        me = _coords()
        sibling = _flip(me, 1)

        def copy(a, k, owner, to, from_input=False):
            slot = out_refs[a].at[_index(owner)]
            return pltpu.make_async_remote_copy(
                src_ref=x_refs[a] if from_input else slot, dst_ref=slot,
                send_sem=send_sems.at[7 * a + k], recv_sem=recv_sems.at[7 * a + k],
                device_id=to, device_id_type=MESH)

        mine = [pltpu.make_async_copy(x_refs[a], out_refs[a].at[_index(me)], local_sems.at[a]) for a in range(n)]
        first = [copy(a, 0, me, sibling, True) for a in range(n)]
        first += [copy(a, 1 + j, me, _flip(me, m), True) for j, m in enumerate(chip_masks) for a in range(n)]
        for cp in mine + first:
            cp.start()
        passed = []
        for j, m in enumerate(chip_masks):
            for a in range(n):
                copy(a, 1 + j, _flip(me, m), me).wait_recv()
                passed.append(copy(a, 4 + j, _flip(me, m), sibling))
                passed[-1].start()
        for a in range(n):
            copy(a, 0, sibling, me).wait_recv()
            for j, m in enumerate(chip_masks):
                copy(a, 4 + j, _flip(sibling, m), me).wait_recv()
        for cp in first + passed:
            cp.wait_send()
        for cp in mine:
            cp.wait()

    return _comm_call(body, name, blocks, [jax.ShapeDtypeStruct((N_DEV,) + b.shape, b.dtype) for b in blocks])


def _exchange(segs, name, masks, slot_of, by_sender=False):
    n = len(segs)

    def body(*refs):
        seg_refs, out_refs = refs[:n], refs[n:2 * n]
        send_sems, recv_sems, local_sems = refs[2 * n:]
        me = _coords()

        def copy(a, k, sender):
            to = _flip(sender, masks[k])
            return pltpu.make_async_remote_copy(
                src_ref=seg_refs[a].at[slot_of(to)], dst_ref=out_refs[a].at[slot_of(sender) if by_sender else k],
                send_sem=send_sems.at[7 * a + k], recv_sem=recv_sems.at[7 * a + k],
                device_id=to, device_id_type=MESH)

        mine = [pltpu.make_async_copy(seg_refs[a].at[slot_of(me)], out_refs[a].at[slot_of(me)], local_sems.at[a])
                for a in range(n)] if by_sender else []
        sends = [copy(a, k, me) for k in range(len(masks)) for a in range(n)]
        for cp in mine + sends:
            cp.start()
        for k, mask in enumerate(masks):
            for a in range(n):
                copy(a, k, _flip(me, mask)).wait_recv()
        for cp in sends:
            cp.wait_send()
        for cp in mine:
            cp.wait()

    slots = lambda s: s.shape[0] if by_sender else len(masks)
    return _comm_call(body, name, segs, [jax.ShapeDtypeStruct((slots(s),) + s.shape[1:], s.dtype) for s in segs])


def _own_slot(tr, cols):
    return pl.BlockSpec((1, tr, cols), lambda i, slot: (slot[0], i, 0))


def _pair_sum(seg, slot, recv, name):
    _, rows, cols = seg.shape
    tr = max(t for t in range(16, rows + 1, 16) if rows % t == 0 and 2 * t * cols * 2 <= ADAM_TILE_BYTES)

    def body(slot_ref, own_ref, recv_ref, o_ref):
        o_ref[...] = (own_ref[0].astype(F32) + recv_ref[...].astype(F32)).astype(BF16)

    tile = pl.BlockSpec((tr, cols), lambda i, slot: (i, 0))
    return pl.pallas_call(
        body, name=name,
        grid_spec=pltpu.PrefetchScalarGridSpec(
            num_scalar_prefetch=1, grid=(rows // tr,), in_specs=[_own_slot(tr, cols), tile], out_specs=tile),
        out_shape=jax.ShapeDtypeStruct((rows, cols), BF16),
        compiler_params=_cparams(),
    )(slot, seg, recv)


def _adamw(seg, slot, recv, w, m, v, name):
    rows, cols = w.shape
    n_slots = recv.shape[0]
    sublanes = 8 * (4 // recv.dtype.itemsize)
    tr = max(t for t in range(sublanes, rows + 1, sublanes)
             if rows % t == 0 and N_DEV * t * cols * 4 <= ADAM_TILE_BYTES)
    c1 = 1.0 - ADAM_B1 ** ADAM_STEP
    c2 = 1.0 - ADAM_B2 ** ADAM_STEP

    def body(slot_ref, own_ref, r_ref, w_ref, m_ref, v_ref, g_out, d_out, m_out, v_out):
        g = own_ref[0].astype(F32)
        for s in range(n_slots):
            g = g + r_ref[s].astype(F32)
        m_new = ADAM_B1 * m_ref[...] + (1.0 - ADAM_B1) * g
        v_new = ADAM_B2 * v_ref[...] + (1.0 - ADAM_B2) * (g * g)
        m_hat = m_new / c1
        v_hat = v_new / c2
        g_out[...] = g
        d_out[...] = -ADAM_LR * (m_hat / (jnp.sqrt(v_hat) + ADAM_EPS) + ADAM_WD * w_ref[...])
        m_out[...] = m_new
        v_out[...] = v_new

    tile = pl.BlockSpec((tr, cols), lambda i, slot: (i, 0))
    return pl.pallas_call(
        body, name=name,
        grid_spec=pltpu.PrefetchScalarGridSpec(
            num_scalar_prefetch=1, grid=(rows // tr,),
            in_specs=[_own_slot(tr, cols), pl.BlockSpec((n_slots, tr, cols), lambda i, slot: (0, i, 0)),
                      tile, tile, tile],
            out_specs=[tile] * 4),
        out_shape=[jax.ShapeDtypeStruct(w.shape, F32)] * 4,
        compiler_params=_cparams(),
    )(slot, seg, recv, w, m, v)


def _mm(a, b, *, name, out_dtype, b_transposed=False):
    M, K = a.shape
    N = b.shape[0] if b_transposed else b.shape[1]
    tm = _pick(M, (640, 128))
    tn = _pick(N, (2816, 1024, 640, 512, 256, 128))
    tk = K if K <= 3200 else _pick(K, (2816, 1024, 640, 512, 256, 128))
    nk = K // tk

    def body(*refs):
        a_ref, b_ref, o_ref = refs[:3]
        acc_ref = refs[-1] if nk > 1 else None
        if b_transposed:
            prod = _dot_nt(a_ref[...], b_ref[...])
        else:
            prod = jnp.dot(a_ref[...], b_ref[...], preferred_element_type=F32)
        if nk == 1:
            o_ref[...] = prod.astype(out_dtype)
        else:
            k = pl.program_id(2)

            @pl.when(k == 0)
            def _():
                acc_ref[...] = prod

            @pl.when(k > 0)
            def _():
                acc_ref[...] += prod

            @pl.when(k == nk - 1)
            def _():
                o_ref[...] = acc_ref[...].astype(out_dtype)

    b_spec = (pl.BlockSpec((tn, tk), lambda i, j, k: (j, k)) if b_transposed
              else pl.BlockSpec((tk, tn), lambda i, j, k: (k, j)))
    return pl.pallas_call(
        body, name=name, grid=(M // tm, N // tn, nk),
        in_specs=[pl.BlockSpec((tm, tk), lambda i, j, k: (i, k)), b_spec],
        out_specs=pl.BlockSpec((tm, tn), lambda i, j, k: (i, j)),
        out_shape=jax.ShapeDtypeStruct((M, N), out_dtype),
        scratch_shapes=[pltpu.VMEM((tm, tn), F32)] if nk > 1 else [],
        compiler_params=_cparams(),
    )(a, b)


def _mm_tn(a, b, *, name):
    T, M = a.shape
    N = b.shape[1]
    tm = _pick(M, (1408, 1024, 512, 256, 128))
    tn = _pick(N, (1024, 640, 512, 256, 128))
    tt = _pick(T, [t for t in (4160, 1664, 640) if 2 * 2 * t * (tm + tn) <= MM_TN_WINDOW_BYTES] + [128])

    def body(a_ref, b_ref, o_ref):
        prod = lax.dot_general(a_ref[...], b_ref[...], (((0,), (0,)), ((), ())), preferred_element_type=F32)

        @pl.when(pl.program_id(2) == 0)
        def _():
            o_ref[...] = prod

        @pl.when(pl.program_id(2) > 0)
        def _():
            o_ref[...] += prod

    return pl.pallas_call(
        body, name=name, grid=(M // tm, N // tn, T // tt),
        in_specs=[pl.BlockSpec((tt, tm), lambda i, j, k: (k, i)), pl.BlockSpec((tt, tn), lambda i, j, k: (k, j))],
        out_specs=pl.BlockSpec((tm, tn), lambda i, j, k: (i, j)),
        out_shape=jax.ShapeDtypeStruct((M, N), F32),
        compiler_params=_cparams(),
    )(a, b)


def _layer_norm(z, g, b):
    mu = jnp.mean(z, axis=-1, keepdims=True)
    xc = z - mu
    var = jnp.mean(xc * xc, axis=-1, keepdims=True)
    return xc * lax.rsqrt(var + LN_EPS) * g + b


def _mm_ln(a, w, res, g, b, *, name, target=None, n_tok=None):
    M, K = a.shape
    D = w.shape[1]
    tm = _pick(M, (640, 128))
    final = target is not None

    def body(*refs):
        if final:
            a_ref, w_ref, r_ref, g_ref, b_ref, t_ref, z_ref, dy_ref, loss_ref = refs
        else:
            a_ref, w_ref, r_ref, g_ref, b_ref, z_ref, h_ref, hb_ref = refs
        z = ALPHA * r_ref[...] + jnp.dot(a_ref[...], w_ref[...], preferred_element_type=F32)
        z_ref[...] = z
        h = _layer_norm(z, g_ref[...], b_ref[...])
        if not final:
            h_ref[...] = h
            hb_ref[...] = h.astype(BF16)
            return
        i = pl.program_id(0)
        row = i * tm + lax.broadcasted_iota(jnp.int32, (tm, 1), 0)
        valid = (row >= N_META) & (row < n_tok)
        err = jnp.where(valid, h - t_ref[...], 0.0)
        dy_ref[...] = err / D

        @pl.when(i == 0)
        def _():
            loss_ref[...] = jnp.zeros_like(loss_ref)

        loss_ref[...] += 0.5 * jnp.sum(jnp.sum(err * err, axis=1, keepdims=True) / D, axis=0, keepdims=True)

    row_blk = lambda cols: pl.BlockSpec((tm, cols), lambda i: (i, 0))
    vec = pl.BlockSpec((1, D), lambda i: (0, 0))
    in_specs = [row_blk(K), pl.BlockSpec((K, D), lambda i: (0, 0)), row_blk(D), vec, vec]
    args = [a, w, res, g, b]
    if final:
        in_specs.append(row_blk(D))
        args.append(target)
        out_specs = [row_blk(D), row_blk(D), pl.BlockSpec((8, LANES), lambda i: (0, 0))]
        out_shape = [jax.ShapeDtypeStruct((M, D), F32)] * 2 + [jax.ShapeDtypeStruct((8, LANES), F32)]
    else:
        out_specs = [row_blk(D)] * 3
        out_shape = [jax.ShapeDtypeStruct((M, D), F32)] * 2 + [jax.ShapeDtypeStruct((M, D), BF16)]
    return pl.pallas_call(
        body, name=name, grid=(M // tm,), in_specs=in_specs, out_specs=out_specs, out_shape=out_shape,
        compiler_params=_cparams(),
    )(*args)


def _ln_bwd(dh, z, g, *, name, dz_next=None):
    M, D = z.shape
    tm = _pick(M, (640, 128))
    has_next = dz_next is not None

    def body(*refs):
        if has_next:
            dh_ref, nx_ref, z_ref, g_ref, dz_ref, dzb_ref, dg_ref, db_ref = refs
            dh_v = dh_ref[...] + ALPHA * nx_ref[...]
        else:
            dh_ref, z_ref, g_ref, dz_ref, dzb_ref, dg_ref, db_ref = refs
            dh_v = dh_ref[...]
        z_v = z_ref[...]
        mu = jnp.mean(z_v, axis=-1, keepdims=True)
        xc = z_v - mu
        rstd = lax.rsqrt(jnp.mean(xc * xc, axis=-1, keepdims=True) + LN_EPS)
        xhat = xc * rstd
        dxhat = dh_v * g_ref[...]
        dz = rstd * (dxhat - jnp.mean(dxhat, axis=-1, keepdims=True)
                     - xhat * jnp.mean(dxhat * xhat, axis=-1, keepdims=True))
        dz_ref[...] = dz
        dzb_ref[...] = dz.astype(BF16)

        @pl.when(pl.program_id(0) == 0)
        def _():
            dg_ref[...] = jnp.zeros_like(dg_ref)
            db_ref[...] = jnp.zeros_like(db_ref)

        dg_ref[...] += jnp.sum(dh_v * xhat, axis=0, keepdims=True)
        db_ref[...] += jnp.sum(dh_v, axis=0, keepdims=True)

    row_blk = pl.BlockSpec((tm, D), lambda i: (i, 0))
    vec = pl.BlockSpec((1, D), lambda i: (0, 0))
    args = [dh] + ([dz_next] if has_next else []) + [z, g]
    in_specs = [row_blk] * (len(args) - 1) + [vec]
    return pl.pallas_call(
        body, name=name, grid=(M // tm,), in_specs=in_specs,
        out_specs=[row_blk, row_blk, vec, vec],
        out_shape=[jax.ShapeDtypeStruct((M, D), F32), jax.ShapeDtypeStruct((M, D), BF16),
                   jax.ShapeDtypeStruct((1, D), F32), jax.ShapeDtypeStruct((1, D), F32)],
        compiler_params=_cparams(),
    )(*args)


def _pool_fwd(h0, pw, scale, g, b, *, name):
    M, D = h0.shape
    n_groups, G, _ = pw.shape
    tm = _pick(M, (640, 128))

    def body(x_ref, halo_ref, pw_ref, sc_ref, g_ref, b_ref, z_ref, h_ref, hb_ref, diff_ref, ext_ref, mix_ref):
        i = pl.program_id(0)
        x = x_ref[...]
        ext_ref[0:POOL_HALO, :] = jnp.where(i == 0, 0.0, halo_ref[...])
        ext_ref[POOL_HALO:, :] = x
        tok = i * tm + lax.broadcasted_iota(jnp.int32, (tm, 1), 0)
        for gi, win in enumerate(POOL_WINDOWS):
            cols = slice(gi * G, (gi + 1) * G)
            xs = x[:, cols]
            s = xs
            for k in range(1, win):
                s = s + ext_ref[pl.ds(POOL_HALO - k, tm), cols]
            count = jnp.minimum(tok + 1, win).astype(F32)
            d = (s / count - xs).astype(BF16)
            diff_ref[:, cols] = d
            mix_ref[:, cols] = jnp.dot(d, pw_ref[gi], preferred_element_type=F32)
        z = ALPHA * x + mix_ref[...] * sc_ref[...]
        z_ref[...] = z
        h = _layer_norm(z, g_ref[...], b_ref[...])
        h_ref[...] = h
        hb_ref[...] = h.astype(BF16)

    row_blk = pl.BlockSpec((tm, D), lambda i: (i, 0))
    vec = pl.BlockSpec((1, D), lambda i: (0, 0))
    halo = pl.BlockSpec((POOL_HALO, D), lambda i: (jnp.maximum(i * (tm // POOL_HALO) - 1, 0), 0))
    return pl.pallas_call(
        body, name=name, grid=(M // tm,),
        in_specs=[row_blk, halo, pl.BlockSpec((n_groups, G, G), lambda i: (0, 0, 0)), vec, vec, vec],
        out_specs=[row_blk] * 4,
        out_shape=[jax.ShapeDtypeStruct((M, D), F32)] * 2 + [jax.ShapeDtypeStruct((M, D), BF16)] * 2,
        scratch_shapes=[pltpu.VMEM((tm + POOL_HALO, D), F32), pltpu.VMEM((tm, D), F32)],
        compiler_params=_cparams(),
    )(h0, h0, pw, scale, g, b)


def _pool_bwd(dz, diff, pw, scale, *, name):
    M, D = dz.shape
    n_groups, G, _ = pw.shape
    tm = _pick(M, (640, 128))
    nt = M // tm

    def body(dz_ref, halo_ref, diff_ref, pw_ref, sc_ref, dh_ref, dpw_ref, dsc_ref, ext_ref, q_ref):
        i = pl.program_id(0)
        dz_v = dz_ref[...]
        ext_ref[0:tm, :] = dz_v
        ext_ref[tm:, :] = jnp.where(i == nt - 1, 0.0, halo_ref[...])
        tok = i * tm + lax.broadcasted_iota(jnp.int32, (tm + POOL_HALO, 1), 0)

        @pl.when(i == 0)
        def _():
            dpw_ref[...] = jnp.zeros_like(dpw_ref)
            dsc_ref[...] = jnp.zeros_like(dsc_ref)

        for gi, win in enumerate(POOL_WINDOWS):
            cols = slice(gi * G, (gi + 1) * G)
            dmix = (ext_ref[:, cols] * sc_ref[:, cols]).astype(BF16)
            ddiff = _dot_nt(dmix, pw_ref[gi])
            count = jnp.minimum(tok + 1, win).astype(F32)
            q_ref[:, cols] = ddiff / count
            acc = -ddiff[0:tm]
            for k in range(win):
                acc = acc + q_ref[pl.ds(k, tm), cols]
            dh_ref[:, cols] = ALPHA * dz_v[:, cols] + acc
            d = diff_ref[:, cols]
            dpw_ref[gi] += lax.dot_general(d, dmix[0:tm], (((0,), (0,)), ((), ())), preferred_element_type=F32)
            mixed = jnp.dot(d, pw_ref[gi], preferred_element_type=F32)
            dsc_ref[:, cols] += jnp.sum(dz_v[:, cols] * mixed, axis=0, keepdims=True)

    row_blk = pl.BlockSpec((tm, D), lambda i: (i, 0))
    vec = pl.BlockSpec((1, D), lambda i: (0, 0))
    per_tile = tm // POOL_HALO
    halo = pl.BlockSpec((POOL_HALO, D), lambda i: (jnp.minimum((i + 1) * per_tile, nt * per_tile - 1), 0))
    wblk = pl.BlockSpec((n_groups, G, G), lambda i: (0, 0, 0))
    return pl.pallas_call(
        body, name=name, grid=(nt,),
        in_specs=[row_blk, halo, row_blk, wblk, vec],
        out_specs=[row_blk, wblk, vec],
        out_shape=[jax.ShapeDtypeStruct((M, D), F32), jax.ShapeDtypeStruct((n_groups, G, G), F32),
                   jax.ShapeDtypeStruct((1, D), F32)],
        scratch_shapes=[pltpu.VMEM((tm + POOL_HALO, D), F32), pltpu.VMEM((tm + POOL_HALO, D), F32)],
        compiler_params=_cparams(),
    )(dz, dz, diff, pw, scale)


def _glu_interleave(x):
    s = x.shape
    n = s[-1] // (2 * GLU_CHUNK)
    return jnp.swapaxes(x.reshape(s[:-1] + (2, n, GLU_CHUNK)), -3, -2).reshape(s)


def _glu_deinterleave(x):
    s = x.shape
    n = s[-1] // (2 * GLU_CHUNK)
    return jnp.swapaxes(x.reshape(s[:-1] + (n, 2, GLU_CHUNK)), -3, -2).reshape(s)


W_PACK_ROWS = 256
W_PACK_PIECE = 64


def _interleaved_col(col, F):
    half, within = divmod(col, F)
    chunk, off = divmod(within, GLU_CHUNK)
    return chunk * 2 * GLU_CHUNK + half * GLU_CHUNK + off


def _w_in_grad_segments(dw, *, name):
    D, F2 = dw.shape
    width = F2 // N_DEV
    rows = _pick(D, (W_PACK_ROWS, 128))
    assert width % W_PACK_PIECE == 0

    def body(x_ref, o_ref):
        for j in range(N_DEV):
            for q in range(0, width, W_PACK_PIECE):
                at = _interleaved_col(width * j + q, F2 // 2)
                o_ref[j % 2, j // 2, :, q:q + W_PACK_PIECE] = x_ref[:, at:at + W_PACK_PIECE].astype(BF16)

    return pl.pallas_call(
        body, name=name, grid=(D // rows,),
        in_specs=[pl.BlockSpec((rows, F2), lambda r: (r, 0))],
        out_specs=pl.BlockSpec((2, N_CHIPS, rows, width), lambda r: (0, 0, r, 0)),
        out_shape=jax.ShapeDtypeStruct((2, N_CHIPS, D, width), BF16),
        compiler_params=_cparams(),
    )(dw)


def _w_in_interleaved(gathered, *, name):
    n_dev, depth, D, width = gathered.shape
    F = n_dev * width // 2
    rows = _pick(D, (W_PACK_ROWS, 128))
    assert width % W_PACK_PIECE == 0 and GLU_CHUNK % W_PACK_PIECE == 0 and F % GLU_CHUNK == 0
    interleaved = lambda col: _interleaved_col(col, F)

    def body(x_ref, o_ref):
        for j in range(n_dev):
            for q in range(0, width, W_PACK_PIECE):
                to = interleaved(width * j + q)
                o_ref[0, :, to:to + W_PACK_PIECE] = x_ref[j, 0, :, q:q + W_PACK_PIECE]

    return pl.pallas_call(
        body, name=name, grid=(depth, D // rows),
        in_specs=[pl.BlockSpec((n_dev, 1, rows, width), lambda l, r: (0, l, r, 0))],
        out_specs=pl.BlockSpec((1, rows, n_dev * width), lambda l, r: (l, r, 0)),
        out_shape=jax.ShapeDtypeStruct((depth, D, n_dev * width), gathered.dtype),
        compiler_params=_cparams(),
    )(gathered)


def _taps(u_ref, head_ref, r0):
    src, base = (head_ref, CONV_HALO) if r0 == 0 else (u_ref, r0)
    return tuple(src[pl.ds(base - k, CONV_STRIP), :] for k in range(3))


def _conv_glu_fwd(u, cw, cb, *, name):
    M, F2 = u.shape
    tm = _pick(M, (640, 128))
    tc = 2 * GLU_CHUNK

    def body(u_ref, halo_ref, w_ref, b_ref, o_ref, c_ref, head_ref):
        i = pl.program_id(0)
        head_ref[0:CONV_HALO, :] = jnp.where(i == 0, 0.0, halo_ref[...])
        head_ref[CONV_HALO:, :] = u_ref[0:CONV_STRIP, :]
        w0, w1, w2, b = w_ref[0:1, :], w_ref[1:2, :], w_ref[2:3, :], b_ref[...]
        for r0 in range(0, tm, CONV_STRIP):
            u0, u1, u2 = _taps(u_ref, head_ref, r0)
            c = b + w0 * u2 + w1 * u1 + w2 * u0
            c_ref[pl.ds(r0, CONV_STRIP), :] = c
            a, g = c[:, :GLU_CHUNK], c[:, GLU_CHUNK:]
            o_ref[pl.ds(r0, CONV_STRIP), :] = (a * jax.nn.sigmoid(a) * g).astype(BF16)

    per_tile = tm // CONV_HALO
    return pl.pallas_call(
        body, name=name, grid=(M // tm, F2 // tc),
        in_specs=[pl.BlockSpec((tm, tc), lambda i, j: (i, j)),
                  pl.BlockSpec((CONV_HALO, tc), lambda i, j: (jnp.maximum(i * per_tile - 1, 0), j)),
                  pl.BlockSpec((3, tc), lambda i, j: (0, j)), pl.BlockSpec((1, tc), lambda i, j: (0, j))],
        out_specs=[pl.BlockSpec((tm, GLU_CHUNK), lambda i, j: (i, j)), pl.BlockSpec((tm, tc), lambda i, j: (i, j))],
        out_shape=[jax.ShapeDtypeStruct((M, F2 // 2), BF16), jax.ShapeDtypeStruct((M, F2), F32)],
        scratch_shapes=[pltpu.VMEM((CONV_HALO + CONV_STRIP, tc), F32)],
        compiler_params=_cparams(),
    )(u, u, cw, cb)


def _conv_glu_bwd(u, c, dact, cw, *, name):
    M, F2 = u.shape
    tm = _pick(M, (640, 128))
    nt = M // tm
    tc = 2 * GLU_CHUNK

    def body(u_ref, c_ref, da_ref, w_ref, du_ref, dw_ref, db_ref, dcx_ref, carry_ref):
        i = pl.program_id(1)
        w0, w1, w2 = w_ref[0:1, :], w_ref[1:2, :], w_ref[2:3, :]

        @pl.when(i == 0)
        def _():
            dw_ref[...] = jnp.zeros_like(dw_ref)
            db_ref[...] = jnp.zeros_like(db_ref)
            carry_ref[...] = jnp.zeros_like(carry_ref)

        def fold(t):
            return sum(t[r:r + 8] for r in range(0, CONV_STRIP, 8))

        dcx_ref[tm:, :] = carry_ref[...]
        s_b = s_0 = s_1 = s_2 = jnp.zeros((8, tc), F32)
        for r0 in reversed(range(0, tm, CONV_STRIP)):
            rows = pl.ds(r0, CONV_STRIP)
            c_v = c_ref[rows, :]
            a, g = c_v[:, :GLU_CHUNK], c_v[:, GLU_CHUNK:]
            sig = jax.nn.sigmoid(a)
            dact_v = da_ref[rows, :]
            d_a = dact_v * g * (sig * (1.0 + a * (1.0 - sig)))
            d_g = dact_v * (a * sig)
            dc = jnp.concatenate([d_a, d_g], axis=1)
            dcx_ref[rows, :] = dc
            dc1, dc2 = dcx_ref[pl.ds(r0 + 1, CONV_STRIP), :], dcx_ref[pl.ds(r0 + 2, CONV_STRIP), :]
            du_ref[rows, :] = (w2 * dc + w1 * dc1 + w0 * dc2).astype(BF16)
            u_v = u_ref[rows, :]
            s_b, s_0, s_1, s_2 = s_b + fold(dc), s_0 + fold(dc2 * u_v), s_1 + fold(dc1 * u_v), s_2 + fold(dc * u_v)
        carry_ref[...] = dcx_ref[0:CONV_HALO, :]
        db_ref[...] += jnp.sum(s_b, axis=0, keepdims=True)
        dw_ref[0:1, :] += jnp.sum(s_0, axis=0, keepdims=True)
        dw_ref[1:2, :] += jnp.sum(s_1, axis=0, keepdims=True)
        dw_ref[2:3, :] += jnp.sum(s_2, axis=0, keepdims=True)

    per_tile = tm // CONV_HALO
    rev = lambda i: nt - 1 - i
    return pl.pallas_call(
        body, name=name, grid=(F2 // tc, nt),
        in_specs=[pl.BlockSpec((tm, tc), lambda j, i: (rev(i), j)), pl.BlockSpec((tm, tc), lambda j, i: (rev(i), j)),
                  pl.BlockSpec((tm, GLU_CHUNK), lambda j, i: (rev(i), j)),
                  pl.BlockSpec((3, tc), lambda j, i: (0, j))],
        out_specs=[pl.BlockSpec((tm, tc), lambda j, i: (rev(i), j)),
                   pl.BlockSpec((3, tc), lambda j, i: (0, j)), pl.BlockSpec((1, tc), lambda j, i: (0, j))],
        out_shape=[jax.ShapeDtypeStruct((M, F2), BF16), jax.ShapeDtypeStruct((3, F2), F32),
                   jax.ShapeDtypeStruct((1, F2), F32)],
        scratch_shapes=[pltpu.VMEM((tm + CONV_HALO, tc), F32), pltpu.VMEM((CONV_HALO, tc), F32)],
        compiler_params=_cparams(),
    )(u, c, dact, cw)


def _split3(x):
    hi = x.astype(BF16)
    r = x - hi.astype(F32)
    mid = r.astype(BF16)
    lo = (r - mid.astype(F32)).astype(BF16)
    return hi, mid, lo


def _tri_sum(tri, x):
    return sum(jnp.dot(tri, part, preferred_element_type=F32) for part in _split3(x))


def _log_sigmoid(x):
    return jnp.minimum(x, 0.0) - jnp.log1p(jnp.exp(-jnp.abs(x)))


def _forget_cumsum(pre, bias, *, name):
    M, C = pre.shape
    tm = _pick(M, (640, 128))

    def body(p_ref, b_ref, c_ref, carry_ref):
        i = pl.program_id(0)

        @pl.when(i == 0)
        def _():
            carry_ref[...] = jnp.zeros_like(carry_ref)

        logf = _log_sigmoid(p_ref[...] + b_ref[...])
        r = lax.broadcasted_iota(jnp.int32, (tm, tm), 0)
        s = lax.broadcasted_iota(jnp.int32, (tm, tm), 1)
        c_ref[...] = _tri_sum((s <= r).astype(BF16), logf) + carry_ref[...]
        carry_ref[...] = c_ref[pl.ds(tm - 1, 1), :]

    return pl.pallas_call(
        body, name=name, grid=(M // tm,),
        in_specs=[pl.BlockSpec((tm, C), lambda i: (i, 0)), pl.BlockSpec((1, C), lambda i: (0, 0))],
        out_specs=pl.BlockSpec((tm, C), lambda i: (i, 0)),
        out_shape=jax.ShapeDtypeStruct((M, C), F32),
        scratch_shapes=[pltpu.VMEM((1, C), F32)],
        compiler_params=_cparams(),
    )(pre, bias)


def _forget_cumsum_bwd(dc, pre, bias, *, name):
    M, C = pre.shape
    tm = _pick(M, (640, 128))
    nt = M // tm

    def body(dc_ref, p_ref, b_ref, dp_ref, db_ref, carry_ref, run_ref):
        i = pl.program_id(0)

        @pl.when(i == 0)
        def _():
            carry_ref[...] = jnp.zeros_like(carry_ref)
            db_ref[...] = jnp.zeros_like(db_ref)

        r = lax.broadcasted_iota(jnp.int32, (tm, tm), 0)
        s = lax.broadcasted_iota(jnp.int32, (tm, tm), 1)
        run_ref[...] = _tri_sum((s >= r).astype(BF16), dc_ref[...]) + carry_ref[...]
        carry_ref[...] = run_ref[pl.ds(0, 1), :]
        dpre = run_ref[...] * jax.nn.sigmoid(-(p_ref[...] + b_ref[...]))
        dp_ref[...] = dpre.astype(BF16)
        db_ref[...] += jnp.sum(dpre, axis=0, keepdims=True)

    rev_blk = pl.BlockSpec((tm, C), lambda i: (nt - 1 - i, 0))
    vec = pl.BlockSpec((1, C), lambda i: (0, 0))
    return pl.pallas_call(
        body, name=name, grid=(nt,),
        in_specs=[rev_blk, rev_blk, vec], out_specs=[rev_blk, vec],
        out_shape=[jax.ShapeDtypeStruct((M, C), BF16), jax.ShapeDtypeStruct((1, C), F32)],
        scratch_shapes=[pltpu.VMEM((1, C), F32), pltpu.VMEM((tm, C), F32)],
        compiler_params=_cparams(),
    )(dc, pre, bias)


def _causal_mask(tm):
    key = lax.broadcasted_iota(jnp.int32, (tm, tm), 0)
    query = lax.broadcasted_iota(jnp.int32, (tm, tm), 1)
    return key <= query


def _dot_nt(a, b):
    return lax.dot_general(a, b, (((1,), (1,)), ((), ())), preferred_element_type=F32)


def _loop_unrolled(n, step, init, unroll):
    def trip(p, carry):
        for r in range(unroll):
            carry = step(unroll * p + r, carry)
        return carry
    carry = lax.fori_loop(0, n // unroll, trip, init)
    return lax.fori_loop(unroll * (n // unroll), n, step, carry)


def _rows(main, extras, total):
    tm = main.shape[1]
    used = sum(e.shape[0] for e in extras)
    tile = jnp.concatenate([e.astype(BF16) for e in extras] + [jnp.zeros((ATTN_EXTRA - used, tm), BF16)], axis=0)
    rest = total - main.shape[0] - ATTN_EXTRA
    return jnp.concatenate([main, tile] + ([jnp.zeros((rest, tm), BF16)] if rest else []), axis=0)


def _attn_specs(H, M, dh, tm):
    nt = M // tm
    qT_blk = pl.BlockSpec((1, dh, tm), lambda h, i: (h, 0, i))
    row_blk = pl.BlockSpec((1, 1, tm), lambda h, i: (h, 0, i))
    key_blk = pl.BlockSpec((1, M, LANES), lambda h, i: (h, 0, 0))
    keyT_blk = pl.BlockSpec((1, nt, dh + ATTN_EXTRA, tm), lambda h, i: (h, 0, 0, 0))
    return qT_blk, row_blk, key_blk, keyT_blk


def _pair_specs(M, dh, tm):
    assert 2 * dh == LANES
    cols = lambda section: pl.BlockSpec((tm, LANES), lambda p, i: (i, section + p))
    headsT = pl.BlockSpec((2, dh, tm), lambda p, i: (p, 0, i))
    return cols, headsT


def _attn_pack(qkv, c, H, *, name):
    M, D3 = qkv.shape
    D = D3 // 3
    dh = D // H
    tm = _pick(M, (640, 128))
    nt = M // tm
    cols, headsT = _pair_specs(M, dh, tm)
    key_blk = pl.BlockSpec((2, tm, LANES), lambda p, i: (p, i, 0))
    keyT_blk = pl.BlockSpec((2, 1, dh + ATTN_EXTRA, tm), lambda p, i: (p, i, 0, 0))

    def body(k_ref, v_ref, q_ref, c_ref, ka_ref, va_ref, qT_ref, kT_ref, vT_ref):
        p = pl.program_id(0)
        lane = lax.broadcasted_iota(jnp.int32, (tm, LANES), 1)
        k_v, v_v = k_ref[...], v_ref[...]
        ones_v = jnp.where((lane >= dh) & (lane < dh + 3), 1.0, 0.0)
        for e in range(2):
            ck = jnp.sum(jnp.where(lane == 2 * p + e, c_ref[...], 0.0), axis=1, keepdims=True)
            hi, mid, lo = _split3(-ck)
            extra = jnp.where(lane == dh + 3, hi.astype(F32), jnp.where(lane == dh + 4, mid.astype(F32),
                              jnp.where(lane == dh + 5, lo.astype(F32), ones_v)))
            first = lambda t: t if e == 0 else pltpu.roll(t, dh, 1)
            ka_ref[e] = jnp.where(lane < dh, first(k_v), extra.astype(BF16))
            va_ref[e] = jnp.where(lane < dh, first(v_v), ones_v.astype(BF16))
        row = lax.broadcasted_iota(jnp.int32, (ATTN_EXTRA, tm), 0)
        tail = jnp.where(row == 0, 1.0, 0.0).astype(BF16)
        for src, dst in ((k_v, kT_ref), (v_v, vT_ref)):
            t = src.T
            for e in range(2):
                dst[e, 0, 0:dh, :] = t[e * dh:(e + 1) * dh]
                dst[e, 0, dh:, :] = tail
        qT_ref[...] = q_ref[...].T.reshape(2, dh, tm)

    n_sec = D // LANES
    return pl.pallas_call(
        body, name=name, grid=(H // 2, nt),
        in_specs=[cols(0), cols(n_sec), cols(2 * n_sec), pl.BlockSpec((tm, LANES), lambda p, i: (i, 0))],
        out_specs=[key_blk, key_blk, headsT, keyT_blk, keyT_blk],
        out_shape=[jax.ShapeDtypeStruct((H, M, LANES), BF16)] * 2 + [jax.ShapeDtypeStruct((H, dh, M), BF16)]
        + [jax.ShapeDtypeStruct((H, nt, dh + ATTN_EXTRA, tm), BF16)] * 2,
        compiler_params=_cparams(),
    )(qkv, qkv, qkv, c)


def _heads_split(x, H, *, name):
    M, D = x.shape
    dh = D // H
    tm = _pick(M, (640, 128))
    cols, headsT = _pair_specs(M, dh, tm)

    def body(x_ref, o_ref):
        o_ref[...] = x_ref[...].T.reshape(2, dh, tm)

    return pl.pallas_call(
        body, name=name, grid=(H // 2, M // tm), in_specs=[cols(0)], out_specs=headsT,
        out_shape=jax.ShapeDtypeStruct((H, dh, M), x.dtype), compiler_params=_cparams(),
    )(x)


def _heads_merge(parts, M, tm, *, name):
    H = parts[0].shape[0]
    dh = LANES // 2
    blocked = [t.ndim == 4 for t in parts]
    cols, headsT = _pair_specs(M, dh, tm)

    def body(*refs):
        for src, dst, b in zip(refs[:len(parts)], refs[len(parts):], blocked):
            t = jnp.concatenate([src[e, 0, 0:dh, :] for e in range(2)], axis=0) if b else src[...].reshape(2 * dh, tm)
            dst[...] = t.astype(F32).T.astype(BF16)

    in_specs = [pl.BlockSpec((2, 1, t.shape[2], tm), lambda p, i: (p, i, 0, 0)) if b else headsT
                for t, b in zip(parts, blocked)]
    return pl.pallas_call(
        body, name=name, grid=(H // 2, M // tm), in_specs=in_specs, out_specs=[cols(0)] * len(parts),
        out_shape=[jax.ShapeDtypeStruct((M, H * dh), BF16)] * len(parts), compiler_params=_cparams(),
    )(*parts)


def _attn_fwd(qT, cq, k_aug, vT_aug, *, name):
    H, dh, M = qT.shape
    tm = vT_aug.shape[-1]
    qT_blk, row_blk, key_blk, keyT_blk = _attn_specs(H, M, dh, tm)

    def body(qT_ref, cq_ref, k_ref, vT_ref, oT_ref, lse_ref):
        i = pl.program_id(1)
        ones = jnp.ones((3, tm), BF16)
        qa = _rows(qT_ref[0] * jnp.asarray(dh ** -0.5, BF16), [*_split3(cq_ref[0]), ones], LANES)

        def block(j, carry, masked):
            m, acc = carry
            keys = pl.ds(pl.multiple_of(j * tm, tm), tm)
            sT = jnp.dot(k_ref[0, keys, :], qa, preferred_element_type=F32)
            if masked:
                sT = jnp.where(_causal_mask(tm), sT, NEG_INF)
            m_new = jnp.maximum(m, jnp.max(sT, axis=0, keepdims=True))
            pT = jnp.exp(sT - m_new).astype(BF16)
            acc = jnp.exp(m - m_new) * acc + jnp.dot(vT_ref[0, j], pT, preferred_element_type=F32)
            return m_new, acc

        init = (jnp.full((1, tm), NEG_INF, F32), jnp.zeros((dh + ATTN_EXTRA, tm), F32))
        m, acc = block(i, _loop_unrolled(i, lambda j, c: block(j, c, False), init, 4), True)
        l = acc[dh:dh + 1, :]
        oT_ref[0] = (acc[0:dh, :] / l).astype(BF16)
        lse_ref[0] = m + jnp.log(l)

    return pl.pallas_call(
        body, name=name, grid=(H, M // tm),
        in_specs=[qT_blk, row_blk, key_blk, keyT_blk], out_specs=[qT_blk, row_blk],
        out_shape=[jax.ShapeDtypeStruct((H, dh, M), BF16), jax.ShapeDtypeStruct((H, 1, M), F32)],
        compiler_params=_cparams(),
    )(qT, cq, k_aug, vT_aug)


def _attn_bwd(qT, oT, doT, lse, cq, k_aug, v_aug, kT_aug, *, name):
    H, dh, M = qT.shape
    tm = kT_aug.shape[-1]
    qT_blk, row_blk, key_blk, keyT_blk = _attn_specs(H, M, dh, tm)
    dvT_blk = pl.BlockSpec((1, M // tm, dh, tm), lambda h, i: (h, 0, 0, 0))

    def body(qT_ref, oT_ref, doT_ref, lse_ref, cq_ref, k_ref, v_ref, kT_ref, dqT_ref, dcq_ref, dkT_ref, dvT_ref):
        i = pl.program_id(1)

        @pl.when(i == 0)
        def _():
            dkT_ref[...] = jnp.zeros_like(dkT_ref)
            dvT_ref[...] = jnp.zeros_like(dvT_ref)

        qsT = qT_ref[0] * jnp.asarray(dh ** -0.5, BF16)
        doT = doT_ref[0]
        delta = jnp.sum(doT.astype(F32) * oT_ref[0].astype(F32), axis=0, keepdims=True)
        ones = jnp.ones((3, tm), BF16)
        qa = _rows(qsT, [*_split3(cq_ref[0] - lse_ref[0]), ones], LANES)
        da = _rows(doT, _split3(-delta), LANES)
        q1 = _rows(qsT, [ones[0:1]], dh + ATTN_EXTRA)

        def block(j, dq, masked):
            keys = pl.ds(pl.multiple_of(j * tm, tm), tm)
            pT = jnp.exp(jnp.dot(k_ref[0, keys, :], qa, preferred_element_type=F32))
            if masked:
                pT = jnp.where(_causal_mask(tm), pT, 0.0)
            ds_b = (pT * jnp.dot(v_ref[0, keys, :], da, preferred_element_type=F32)).astype(BF16)
            dvT_ref[0, j] += _dot_nt(doT, pT.astype(BF16))
            dkT_ref[0, j] += _dot_nt(q1, ds_b)
            return dq + jnp.dot(kT_ref[0, j], ds_b, preferred_element_type=F32)

        init = jnp.zeros((dh + ATTN_EXTRA, tm), F32)
        dq = block(i, _loop_unrolled(i, lambda j, c: block(j, c, False), init, 2), True)
        dqT_ref[0] = (dq[0:dh, :] * dh ** -0.5).astype(BF16)
        dcq_ref[0] = dq[dh:dh + 1, :]

    return pl.pallas_call(
        body, name=name, grid=(H, M // tm),
        in_specs=[qT_blk, qT_blk, qT_blk, row_blk, row_blk, key_blk, key_blk, keyT_blk],
        out_specs=[qT_blk, row_blk, keyT_blk, dvT_blk],
        out_shape=[jax.ShapeDtypeStruct((H, dh, M), BF16), jax.ShapeDtypeStruct((H, 1, M), F32),
                   jax.ShapeDtypeStruct(kT_aug.shape, F32), jax.ShapeDtypeStruct((H, M // tm, dh, tm), F32)],
        compiler_params=_cparams(),
    )(qT, oT, doT, lse, cq, k_aug, v_aug, kT_aug)


def kernel(x, meta, pool_w, pool_scale, w_kv, w_f, b_f, w_q, w_o, ffn_w_in, ffn_conv_w, ffn_conv_b, ffn_w_out, ln_g, ln_b, loss_target, m_meta, m_pool_w, m_pool_scale, m_w_kv, m_w_f, m_b_f, m_w_q, m_w_o, m_ffn_w_in, m_ffn_conv_w, m_ffn_conv_b, m_ffn_w_out, m_ln_g, m_ln_b, v_meta, v_pool_w, v_pool_scale, v_w_kv, v_w_f, v_b_f, v_w_q, v_w_o, v_ffn_w_in, v_ffn_conv_w, v_ffn_conv_b, v_ffn_w_out, v_ln_g, v_ln_b):
    local = dict(meta=meta, pool_w=pool_w, pool_scale=pool_scale, w_kv=w_kv, w_f=w_f, b_f=b_f, w_q=w_q, w_o=w_o,
                 ffn_w_in=ffn_w_in, ffn_conv_w=ffn_conv_w, ffn_conv_b=ffn_conv_b, ffn_w_out=ffn_w_out,
                 ln_g=ln_g, ln_b=ln_b)
    mom1 = dict(meta=m_meta, pool_w=m_pool_w, pool_scale=m_pool_scale, w_kv=m_w_kv, w_f=m_w_f, b_f=m_b_f,
                w_q=m_w_q, w_o=m_w_o, ffn_w_in=m_ffn_w_in, ffn_conv_w=m_ffn_conv_w, ffn_conv_b=m_ffn_conv_b,
                ffn_w_out=m_ffn_w_out, ln_g=m_ln_g, ln_b=m_ln_b)
    mom2 = dict(meta=v_meta, pool_w=v_pool_w, pool_scale=v_pool_scale, w_kv=v_w_kv, w_f=v_w_f, b_f=v_b_f,
                w_q=v_w_q, w_o=v_w_o, ffn_w_in=v_ffn_w_in, ffn_conv_w=v_ffn_conv_w, ffn_conv_b=v_ffn_conv_b,
                ffn_w_out=v_ffn_w_out, ln_g=v_ln_g, ln_b=v_ln_b)
    axis_of = dict(PARAMS)

    S, D = x.shape[1], x.shape[2]
    H = b_f.shape[0]
    dh = D // H
    n_tok = N_META + S
    M = _round_up(n_tok, LANES)
    tm = _pick(M, (640, 128))
    nt = M // tm
    F2 = ffn_conv_b.shape[1]
    F = F2 // 2
    depth = ffn_conv_b.shape[0]

    small_sharded = [n for n in SMALL if axis_of[n] is not None]
    got = _all_gather([local[n].astype(BF16) for n in BIG] + [_pack([local[n] for n in small_sharded])],
                      "gather_weights")
    wb = {n: _from_blocks(blk, axis_of[n]) for n, blk in zip(BIG, got) if n != "ffn_w_in"}
    small_blocks = _unpack(got[-1], [local[n].shape for n in small_sharded], lead=(N_DEV,))
    wf32 = {n: _from_blocks(blk, axis_of[n]) for n, blk in zip(small_sharded, small_blocks)}

    pw = wb["pool_w"][0]
    wf_pad = jnp.pad(wf32["w_f"].astype(BF16), ((0, 0), (0, LANES - H)))
    w_qkv = jnp.concatenate([wb["w_kv"], wb["w_q"][0]], axis=1)
    w_att = jnp.concatenate([w_qkv, wf_pad], axis=1)
    wo = wb["w_o"][0]
    w_in = _w_in_interleaved(got[BIG.index("ffn_w_in")], name="w_in_interleave")
    w_out = wb["ffn_w_out"]
    conv_w = _glu_interleave(wf32["ffn_conv_w"])
    conv_b = _glu_interleave(ffn_conv_b)[:, None, :]
    scale = wf32["pool_scale"]
    g_ln, b_ln = wf32["ln_g"], wf32["ln_b"]
    ln = lambda i, j: (g_ln[i, j][None, :], b_ln[i, j][None, :])
    bias_f = jnp.pad(b_f, (0, LANES - H))[None, :]

    pad_rows = M - n_tok
    h0 = jnp.concatenate([wf32["meta"], x[0], jnp.zeros((pad_rows, D), F32)], axis=0)
    target = jnp.concatenate([jnp.zeros((N_META, D), F32), loss_target[0], jnp.zeros((pad_rows, D), F32)], axis=0)

    z1, h1, h1b, diff = _pool_fwd(h0, pw, scale, *ln(0, 0), name="pool_fwd")
    u0 = _mm(h1b, w_in[0], name="ffn0_up", out_dtype=F32)
    act0, c0 = _conv_glu_fwd(u0, conv_w[0], conv_b[0], name="ffn0_conv")
    z2, h2, h2b = _mm_ln(act0, w_out[0], h1, *ln(0, 1), name="ffn0_down_ln")

    qkv = _mm(h2b, w_qkv, name="attn_qkv", out_dtype=BF16)
    pre = _mm(h2b, wf_pad, name="attn_gate", out_dtype=F32)
    c = _forget_cumsum(pre, bias_f, name="attn_cumsum")
    k_aug, v_aug, qT_h, kT_aug, vT_aug = _attn_pack(qkv, c, H, name="attn_pack")
    cq = c[:, :H].T[:, None, :]
    oT_h, lse = _attn_fwd(qT_h, cq, k_aug, vT_aug, name="attn_fwd")
    o, = _heads_merge([oT_h], M, tm, name="attn_o_merge")
    z3, h3, h3b = _mm_ln(o, wo, h2, *ln(1, 0), name="attn_out_ln")

    u1 = _mm(h3b, w_in[1], name="ffn1_up", out_dtype=F32)
    act1, c1 = _conv_glu_fwd(u1, conv_w[1], conv_b[1], name="ffn1_conv")
    z4, dy, loss_part = _mm_ln(act1, w_out[1], h3, *ln(1, 1), name="ffn1_down_loss", target=target, n_tok=n_tok)
    loss = lax.psum(loss_part[0, 0], ("x", "y", "c"))

    grads = {}

    def ffn_bwd(layer, dz_b, u, c_conv, act, h_in_b, tag):
        dact = _mm(dz_b, w_out[layer], name=tag + "_dact", out_dtype=F32, b_transposed=True)
        du, dcw, dcb = _conv_glu_bwd(u, c_conv, dact, conv_w[layer], name=tag + "_conv_bwd")
        dh = _mm(du, w_in[layer], name=tag + "_dh", out_dtype=F32, b_transposed=True)
        d_w_out = _mm_tn(act, dz_b, name=tag + "_dw_out")
        d_w_in = _w_in_grad_segments(_mm_tn(h_in_b, du, name=tag + "_dw_in"), name=tag + "_dw_in_segments")
        return dh, d_w_in, d_w_out, _glu_deinterleave(dcw), _glu_deinterleave(dcb)[0]

    dz4, dz4b, dg11, db11 = _ln_bwd(dy, z4, g_ln[1, 1][None, :], name="ln4_bwd")
    dh3, dwin1, dwout1, dcw1, dcb1 = ffn_bwd(1, dz4b, u1, c1, act1, h3b, "ffn1")
    dz3, dz3b, dg10, db10 = _ln_bwd(dh3, z3, g_ln[1, 0][None, :], name="ln3_bwd", dz_next=dz4)

    do = _mm(dz3b, wo, name="attn_do", out_dtype=BF16, b_transposed=True)
    grads["w_o"] = _mm_tn(o, dz3b, name="attn_dw_o")[None]
    doT_h = _heads_split(do, H, name="attn_do_split")
    dqT_h, dcq, dkT_a, dvT_h = _attn_bwd(qT_h, oT_h, doT_h, lse, cq, k_aug, v_aug, kT_aug, name="attn_bwd")
    dck = -dkT_a[:, :, dh, :].reshape(H, M)
    dc = jnp.pad((dcq[:, 0, :] + dck).T, ((0, 0), (0, LANES - H)))
    dpre, dbias = _forget_cumsum_bwd(dc, pre, bias_f, name="attn_cumsum_bwd")
    d_att = jnp.concatenate([*_heads_merge([dkT_a, dvT_h, dqT_h], M, tm, name="attn_dqkv_merge"), dpre], axis=1)
    dh2 = _mm(d_att, w_att, name="attn_dh", out_dtype=F32, b_transposed=True)
    d_w_att = _mm_tn(h2b, d_att, name="attn_dw_qkv")
    grads["w_kv"] = d_w_att[:, :2 * D]
    grads["w_q"] = d_w_att[:, 2 * D:3 * D][None]
    grads["w_f"] = d_w_att[:, 3 * D:3 * D + H]
    grads["b_f"] = dbias[0, :H]
    dz2, dz2b, dg01, db01 = _ln_bwd(dh2, z2, g_ln[0, 1][None, :], name="ln2_bwd", dz_next=dz3)

    dh1, dwin0, dwout0, dcw0, dcb0 = ffn_bwd(0, dz2b, u0, c0, act0, h1b, "ffn0")
    dz1, _, dg00, db00 = _ln_bwd(dh1, z1, g_ln[0, 0][None, :], name="ln1_bwd", dz_next=dz2)
    dh0, dpw, dscale = _pool_bwd(dz1, diff, pw, scale, name="pool_bwd")

    grads["meta"] = dh0[:N_META]
    grads["pool_w"] = dpw[None]
    grads["pool_scale"] = dscale
    seg_w_in = jnp.stack([dwin0, dwin1], axis=2)
    grads["ffn_w_out"] = jnp.stack([dwout0, dwout1])
    grads["ffn_conv_w"] = jnp.stack([dcw0, dcw1])
    grads["ffn_conv_b"] = jnp.stack([dcb0, dcb1])
    grads["ln_g"] = jnp.stack([jnp.stack([dg00[0], dg01[0]]), jnp.stack([dg10[0], dg11[0]])])
    grads["ln_b"] = jnp.stack([jnp.stack([db00[0], db01[0]]), jnp.stack([db10[0], db11[0]])])
    grad_x = dh0[N_META:n_tok][None]

    blocks = {n: _to_blocks(grads[n], axis_of[n]) for n, _ in PARAMS if n != "ffn_w_in"}
    me = _coords()
    core, chip = (jnp.reshape(v, (1,)).astype(jnp.int32) for v in (me[2], 2 * me[0] + me[1]))
    by_core = lambda t: jnp.swapaxes(t.reshape((N_CHIPS, 2) + t.shape[1:]), 0, 1).astype(BF16)
    segs = [seg_w_in if n == "ffn_w_in" else by_core(blocks[n]) for n in BIG]
    pair = _exchange(segs, "grad_pair_exchange", (1,), lambda pos: pos[2])
    rows_of = lambda t, lead: t.reshape(lead + (-1, t.shape[-1]))
    partial = [_pair_sum(rows_of(s, (2,)), core, rows_of(p[0], ()), name="grad_pair_sum_" + n).reshape(s.shape[1:])
               for n, s, p in zip(BIG, segs, pair)]
    recv = _exchange(partial, "grad_chip_exchange", (4, 2, 6), lambda pos: 2 * pos[0] + pos[1])
    recv_small = _exchange([_pack([blocks[n] for n in SMALL], lead=(N_DEV,))], "grad_small_exchange",
                           tuple(range(1, N_DEV)), _index, by_sender=True)[0]
    results = {}
    for n, own, r in zip(BIG, partial, recv):
        shape = local[n].shape
        as2d = lambda t: t.reshape(-1, shape[-1])
        outs = _adamw(rows_of(own, (N_CHIPS,)), chip, rows_of(r, (N_CHIPS - 1,)), as2d(local[n]), as2d(mom1[n]),
                      as2d(mom2[n]), name="adamw_" + n)
        results[n] = [o_.reshape(shape) for o_ in outs]
    outs = _adamw(recv_small, jnp.zeros((1,), jnp.int32), recv_small[1:],
                  *[_pack([d[n] for n in SMALL]) for d in (local, mom1, mom2)], name="adamw_small")
    small_out = [_unpack(o_, [local[n].shape for n in SMALL]) for o_ in outs]
    for i, n in enumerate(SMALL):
        results[n] = [small_out[k][i] for k in range(4)]
    return (loss, grad_x, *[results[n][k] for k in range(4) for n, _ in PARAMS])
```

```python
import jax
import jax.numpy as jnp
from jax import lax
from jax.experimental import pallas as pl
from jax.experimental.pallas import tpu as pltpu

F32, BF16 = jnp.float32, jnp.bfloat16
MESH = pl.DeviceIdType.MESH

N_DEV = 8
N_CHIPS = 4
N_META = 16
POOL_WINDOWS = (2, 4, 8, 16)
POOL_HALO = 16
CONV_HALO = 8
CONV_STRIP = 16
ALPHA = 4.0 ** 0.25
LN_EPS = 1e-5
NEG_INF = -1e30
ADAM_LR, ADAM_B1, ADAM_B2, ADAM_EPS, ADAM_WD, ADAM_STEP = 0.001, 0.9, 0.999, 1e-08, 0.01, 10

LANES = 128
PACK_COLS = 1024
ADAM_TILE_BYTES = 4 << 20
MM_TN_WINDOW_BYTES = 28 << 20
GLU_CHUNK = 256
ATTN_EXTRA = 16
VMEM_LIMIT = 56 * 1024 * 1024

PARAMS = (("meta", 1), ("pool_w", 2), ("pool_scale", 1), ("w_kv", 1), ("w_f", 0), ("b_f", None),
          ("w_q", 1), ("w_o", 1), ("ffn_w_in", 2), ("ffn_conv_w", 2), ("ffn_conv_b", None),
          ("ffn_w_out", 1), ("ln_g", 2), ("ln_b", 2))
BIG = ("pool_w", "w_kv", "w_q", "w_o", "ffn_w_in", "ffn_w_out")
SMALL = ("meta", "pool_scale", "w_f", "b_f", "ffn_conv_w", "ffn_conv_b", "ln_g", "ln_b")


def _cparams(**kw):
    return pltpu.CompilerParams(vmem_limit_bytes=VMEM_LIMIT, **kw)


def _pick(n, cands):
    for c in cands:
        if n % c == 0:
            return c
    return n


def _round_up(n, m):
    return (n + m - 1) // m * m


def _pack(pieces, lead=()):
    flat = []
    for p in pieces:
        v = p.reshape(lead + (-1,))
        flat.append(jnp.pad(v, [(0, 0)] * len(lead) + [(0, _round_up(v.shape[-1], PACK_COLS) - v.shape[-1])]))
    v = jnp.concatenate(flat, axis=-1)
    rows = _round_up(v.shape[-1] // PACK_COLS, 8)
    v = jnp.pad(v, [(0, 0)] * len(lead) + [(0, rows * PACK_COLS - v.shape[-1])])
    return v.reshape(lead + (rows, PACK_COLS))


def _unpack(buf, shapes, lead=()):
    flat = buf.reshape(lead + (-1,))
    out, off = [], 0
    for s in shapes:
        n = 1
        for d in s:
            n *= d
        out.append(flat[..., off:off + n].reshape(lead + tuple(s)))
        off += _round_up(n, PACK_COLS)
    return out


def _to_blocks(full, axis):
    if axis is None:
        return jnp.broadcast_to(full[None], (N_DEV,) + full.shape)
    s = full.shape
    x = full.reshape(s[:axis] + (N_DEV, s[axis] // N_DEV) + s[axis + 1:])
    return jnp.moveaxis(x, axis, 0)


def _from_blocks(blocks, axis):
    x = jnp.moveaxis(blocks, 0, axis)
    s = x.shape
    return x.reshape(s[:axis] + (s[axis] * s[axis + 1],) + s[axis + 2:])


def _coords():
    return lax.axis_index("x"), lax.axis_index("y"), lax.axis_index("c")


def _flip(pos, mask):
    x, y, c = pos
    return (1 - x if mask & 4 else x, 1 - y if mask & 2 else y, 1 - c if mask & 1 else c)


def _index(pos):
    x, y, c = pos
    return 4 * x + 2 * y + c


def _comm_call(body, name, arrays, out_shapes):
    n = len(arrays)
    hbm = pl.BlockSpec(memory_space=pl.ANY)
    return pl.pallas_call(
        body, name=name, out_shape=out_shapes, in_specs=[hbm] * n, out_specs=[hbm] * n,
        scratch_shapes=[pltpu.SemaphoreType.DMA((7 * n,)), pltpu.SemaphoreType.DMA((7 * n,)),
                        pltpu.SemaphoreType.DMA((n,))],
    )(*arrays)


def _all_gather(blocks, name):
    chip_masks = (4, 2, 6)
    n = len(blocks)

    def body(*refs):
        x_refs, out_refs = refs[:n], refs[n:2 * n]
        send_sems, recv_sems, local_sems = refs[2 * n:]
        me = _coords()
        sibling = _flip(me, 1)

        def copy(a, k, owner, to, from_input=False):
            slot = out_refs[a].at[_index(owner)]
            return pltpu.make_async_remote_copy(
                src_ref=x_refs[a] if from_input else slot, dst_ref=slot,
                send_sem=send_sems.at[7 * a + k], recv_sem=recv_sems.at[7 * a + k],
                device_id=to, device_id_type=MESH)

        mine = [pltpu.make_async_copy(x_refs[a], out_refs[a].at[_index(me)], local_sems.at[a]) for a in range(n)]
        first = [copy(a, 0, me, sibling, True) for a in range(n)]
        first += [copy(a, 1 + j, me, _flip(me, m), True) for j, m in enumerate(chip_masks) for a in range(n)]
        for cp in mine + first:
            cp.start()
        passed = []
        for j, m in enumerate(chip_masks):
            for a in range(n):
                copy(a, 1 + j, _flip(me, m), me).wait_recv()
                passed.append(copy(a, 4 + j, _flip(me, m), sibling))
                passed[-1].start()
        for a in range(n):
            copy(a, 0, sibling, me).wait_recv()
            for j, m in enumerate(chip_masks):
                copy(a, 4 + j, _flip(sibling, m), me).wait_recv()
        for cp in first + passed:
            cp.wait_send()
        for cp in mine:
            cp.wait()

    return _comm_call(body, name, blocks, [jax.ShapeDtypeStruct((N_DEV,) + b.shape, b.dtype) for b in blocks])


def _exchange(segs, name, masks, slot_of, by_sender=False):
    n = len(segs)

    def body(*refs):
        seg_refs, out_refs = refs[:n], refs[n:2 * n]
        send_sems, recv_sems, local_sems = refs[2 * n:]
        me = _coords()

        def copy(a, k, sender):
            to = _flip(sender, masks[k])
            return pltpu.make_async_remote_copy(
                src_ref=seg_refs[a].at[slot_of(to)], dst_ref=out_refs[a].at[slot_of(sender) if by_sender else k],
                send_sem=send_sems.at[7 * a + k], recv_sem=recv_sems.at[7 * a + k],
                device_id=to, device_id_type=MESH)

        mine = [pltpu.make_async_copy(seg_refs[a].at[slot_of(me)], out_refs[a].at[slot_of(me)], local_sems.at[a])
                for a in range(n)] if by_sender else []
        sends = [copy(a, k, me) for k in range(len(masks)) for a in range(n)]
        for cp in mine + sends:
            cp.start()
        for k, mask in enumerate(masks):
            for a in range(n):
                copy(a, k, _flip(me, mask)).wait_recv()
        for cp in sends:
            cp.wait_send()
        for cp in mine:
            cp.wait()

    slots = lambda s: s.shape[0] if by_sender else len(masks)
    return _comm_call(body, name, segs, [jax.ShapeDtypeStruct((slots(s),) + s.shape[1:], s.dtype) for s in segs])


def _own_slot(tr, cols):
    return pl.BlockSpec((1, tr, cols), lambda i, slot: (slot[0], i, 0))


def _pair_sum(seg, slot, recv, name):
    _, rows, cols = seg.shape
    tr = max(t for t in range(16, rows + 1, 16) if rows % t == 0 and 2 * t * cols * 2 <= ADAM_TILE_BYTES)

    def body(slot_ref, own_ref, recv_ref, o_ref):
        o_ref[...] = (own_ref[0].astype(F32) + recv_ref[...].astype(F32)).astype(BF16)

    tile = pl.BlockSpec((tr, cols), lambda i, slot: (i, 0))
    return pl.pallas_call(
        body, name=name,
        grid_spec=pltpu.PrefetchScalarGridSpec(
            num_scalar_prefetch=1, grid=(rows // tr,), in_specs=[_own_slot(tr, cols), tile], out_specs=tile),
        out_shape=jax.ShapeDtypeStruct((rows, cols), BF16),
        compiler_params=_cparams(),
    )(slot, seg, recv)


def _adamw(seg, slot, recv, w, m, v, name):
    rows, cols = w.shape
    n_slots = recv.shape[0]
    sublanes = 8 * (4 // recv.dtype.itemsize)
    tr = max(t for t in range(sublanes, rows + 1, sublanes)
             if rows % t == 0 and N_DEV * t * cols * 4 <= ADAM_TILE_BYTES)
    c1 = 1.0 - ADAM_B1 ** ADAM_STEP
    c2 = 1.0 - ADAM_B2 ** ADAM_STEP

    def body(slot_ref, own_ref, r_ref, w_ref, m_ref, v_ref, g_out, d_out, m_out, v_out):
        g = own_ref[0].astype(F32)
        for s in range(n_slots):
            g = g + r_ref[s].astype(F32)
        m_new = ADAM_B1 * m_ref[...] + (1.0 - ADAM_B1) * g
        v_new = ADAM_B2 * v_ref[...] + (1.0 - ADAM_B2) * (g * g)
        m_hat = m_new / c1
        v_hat = v_new / c2
        g_out[...] = g
        d_out[...] = -ADAM_LR * (m_hat / (jnp.sqrt(v_hat) + ADAM_EPS) + ADAM_WD * w_ref[...])
        m_out[...] = m_new
        v_out[...] = v_new

    tile = pl.BlockSpec((tr, cols), lambda i, slot: (i, 0))
    return pl.pallas_call(
        body, name=name,
        grid_spec=pltpu.PrefetchScalarGridSpec(
            num_scalar_prefetch=1, grid=(rows // tr,),
            in_specs=[_own_slot(tr, cols), pl.BlockSpec((n_slots, tr, cols), lambda i, slot: (0, i, 0)),
                      tile, tile, tile],
            out_specs=[tile] * 4),
        out_shape=[jax.ShapeDtypeStruct(w.shape, F32)] * 4,
        compiler_params=_cparams(),
    )(slot, seg, recv, w, m, v)


def _mm(a, b, *, name, out_dtype, b_transposed=False):
    M, K = a.shape
    N = b.shape[0] if b_transposed else b.shape[1]
    tm = _pick(M, (640, 128))
    tn = _pick(N, (2816, 1024, 640, 512, 256, 128))
    tk = K if K <= 3200 else _pick(K, (2816, 1024, 640, 512, 256, 128))
    nk = K // tk

    def body(*refs):
        a_ref, b_ref, o_ref = refs[:3]
        acc_ref = refs[-1] if nk > 1 else None
        if b_transposed:
            prod = _dot_nt(a_ref[...], b_ref[...])
        else:
            prod = jnp.dot(a_ref[...], b_ref[...], preferred_element_type=F32)
        if nk == 1:
            o_ref[...] = prod.astype(out_dtype)
        else:
            k = pl.program_id(2)

            @pl.when(k == 0)
            def _():
                acc_ref[...] = prod

            @pl.when(k > 0)
            def _():
                acc_ref[...] += prod

            @pl.when(k == nk - 1)
            def _():
                o_ref[...] = acc_ref[...].astype(out_dtype)

    b_spec = (pl.BlockSpec((tn, tk), lambda i, j, k: (j, k)) if b_transposed
              else pl.BlockSpec((tk, tn), lambda i, j, k: (k, j)))
    return pl.pallas_call(
        body, name=name, grid=(M // tm, N // tn, nk),
        in_specs=[pl.BlockSpec((tm, tk), lambda i, j, k: (i, k)), b_spec],
        out_specs=pl.BlockSpec((tm, tn), lambda i, j, k: (i, j)),
        out_shape=jax.ShapeDtypeStruct((M, N), out_dtype),
        scratch_shapes=[pltpu.VMEM((tm, tn), F32)] if nk > 1 else [],
        compiler_params=_cparams(),
    )(a, b)


def _mm_tn(a, b, *, name):
    T, M = a.shape
    N = b.shape[1]
    tm = _pick(M, (1408, 1024, 512, 256, 128))
    tn = _pick(N, (1024, 640, 512, 256, 128))
    tt = _pick(T, [t for t in (4160, 1664, 640) if 2 * 2 * t * (tm + tn) <= MM_TN_WINDOW_BYTES] + [128])

    def body(a_ref, b_ref, o_ref):
        prod = lax.dot_general(a_ref[...], b_ref[...], (((0,), (0,)), ((), ())), preferred_element_type=F32)

        @pl.when(pl.program_id(2) == 0)
        def _():
            o_ref[...] = prod

        @pl.when(pl.program_id(2) > 0)
        def _():
            o_ref[...] += prod

    return pl.pallas_call(
        body, name=name, grid=(M // tm, N // tn, T // tt),
        in_specs=[pl.BlockSpec((tt, tm), lambda i, j, k: (k, i)), pl.BlockSpec((tt, tn), lambda i, j, k: (k, j))],
        out_specs=pl.BlockSpec((tm, tn), lambda i, j, k: (i, j)),
        out_shape=jax.ShapeDtypeStruct((M, N), F32),
        compiler_params=_cparams(),
    )(a, b)


def _layer_norm(z, g, b):
    mu = jnp.mean(z, axis=-1, keepdims=True)
    xc = z - mu
    var = jnp.mean(xc * xc, axis=-1, keepdims=True)
    return xc * lax.rsqrt(var + LN_EPS) * g + b


def _mm_ln(a, w, res, g, b, *, name, target=None, n_tok=None):
    M, K = a.shape
    D = w.shape[1]
    tm = _pick(M, (640, 128))
    final = target is not None

    def body(*refs):
        if final:
            a_ref, w_ref, r_ref, g_ref, b_ref, t_ref, z_ref, dy_ref, loss_ref = refs
        else:
            a_ref, w_ref, r_ref, g_ref, b_ref, z_ref, h_ref, hb_ref = refs
        z = ALPHA * r_ref[...] + jnp.dot(a_ref[...], w_ref[...], preferred_element_type=F32)
        z_ref[...] = z
        h = _layer_norm(z, g_ref[...], b_ref[...])
        if not final:
            h_ref[...] = h
            hb_ref[...] = h.astype(BF16)
            return
        i = pl.program_id(0)
        row = i * tm + lax.broadcasted_iota(jnp.int32, (tm, 1), 0)
        valid = (row >= N_META) & (row < n_tok)
        err = jnp.where(valid, h - t_ref[...], 0.0)
        dy_ref[...] = err / D

        @pl.when(i == 0)
        def _():
            loss_ref[...] = jnp.zeros_like(loss_ref)

        loss_ref[...] += 0.5 * jnp.sum(jnp.sum(err * err, axis=1, keepdims=True) / D, axis=0, keepdims=True)

    row_blk = lambda cols: pl.BlockSpec((tm, cols), lambda i: (i, 0))
    vec = pl.BlockSpec((1, D), lambda i: (0, 0))
    in_specs = [row_blk(K), pl.BlockSpec((K, D), lambda i: (0, 0)), row_blk(D), vec, vec]
    args = [a, w, res, g, b]
    if final:
        in_specs.append(row_blk(D))
        args.append(target)
        out_specs = [row_blk(D), row_blk(D), pl.BlockSpec((8, LANES), lambda i: (0, 0))]
        out_shape = [jax.ShapeDtypeStruct((M, D), F32)] * 2 + [jax.ShapeDtypeStruct((8, LANES), F32)]
    else:
        out_specs = [row_blk(D)] * 3
        out_shape = [jax.ShapeDtypeStruct((M, D), F32)] * 2 + [jax.ShapeDtypeStruct((M, D), BF16)]
    return pl.pallas_call(
        body, name=name, grid=(M // tm,), in_specs=in_specs, out_specs=out_specs, out_shape=out_shape,
        compiler_params=_cparams(),
    )(*args)


def _ln_bwd(dh, z, g, *, name, dz_next=None):
    M, D = z.shape
    tm = _pick(M, (640, 128))
    has_next = dz_next is not None

    def body(*refs):
        if has_next:
            dh_ref, nx_ref, z_ref, g_ref, dz_ref, dzb_ref, dg_ref, db_ref = refs
            dh_v = dh_ref[...] + ALPHA * nx_ref[...]
        else:
            dh_ref, z_ref, g_ref, dz_ref, dzb_ref, dg_ref, db_ref = refs
            dh_v = dh_ref[...]
        z_v = z_ref[...]
        mu = jnp.mean(z_v, axis=-1, keepdims=True)
        xc = z_v - mu
        rstd = lax.rsqrt(jnp.mean(xc * xc, axis=-1, keepdims=True) + LN_EPS)
        xhat = xc * rstd
        dxhat = dh_v * g_ref[...]
        dz = rstd * (dxhat - jnp.mean(dxhat, axis=-1, keepdims=True)
                     - xhat * jnp.mean(dxhat * xhat, axis=-1, keepdims=True))
        dz_ref[...] = dz
        dzb_ref[...] = dz.astype(BF16)

        @pl.when(pl.program_id(0) == 0)
        def _():
            dg_ref[...] = jnp.zeros_like(dg_ref)
            db_ref[...] = jnp.zeros_like(db_ref)

        dg_ref[...] += jnp.sum(dh_v * xhat, axis=0, keepdims=True)
        db_ref[...] += jnp.sum(dh_v, axis=0, keepdims=True)

    row_blk = pl.BlockSpec((tm, D), lambda i: (i, 0))
    vec = pl.BlockSpec((1, D), lambda i: (0, 0))
    args = [dh] + ([dz_next] if has_next else []) + [z, g]
    in_specs = [row_blk] * (len(args) - 1) + [vec]
    return pl.pallas_call(
        body, name=name, grid=(M // tm,), in_specs=in_specs,
        out_specs=[row_blk, row_blk, vec, vec],
        out_shape=[jax.ShapeDtypeStruct((M, D), F32), jax.ShapeDtypeStruct((M, D), BF16),
                   jax.ShapeDtypeStruct((1, D), F32), jax.ShapeDtypeStruct((1, D), F32)],
        compiler_params=_cparams(),
    )(*args)


def _pool_fwd(h0, pw, scale, g, b, *, name):
    M, D = h0.shape
    n_groups, G, _ = pw.shape
    tm = _pick(M, (640, 128))

    def body(x_ref, halo_ref, pw_ref, sc_ref, g_ref, b_ref, z_ref, h_ref, hb_ref, diff_ref, ext_ref, mix_ref):
        i = pl.program_id(0)
        x = x_ref[...]
        ext_ref[0:POOL_HALO, :] = jnp.where(i == 0, 0.0, halo_ref[...])
        ext_ref[POOL_HALO:, :] = x
        tok = i * tm + lax.broadcasted_iota(jnp.int32, (tm, 1), 0)
        for gi, win in enumerate(POOL_WINDOWS):
            cols = slice(gi * G, (gi + 1) * G)
            xs = x[:, cols]
            s = xs
            for k in range(1, win):
                s = s + ext_ref[pl.ds(POOL_HALO - k, tm), cols]
            count = jnp.minimum(tok + 1, win).astype(F32)
            d = (s / count - xs).astype(BF16)
            diff_ref[:, cols] = d
            mix_ref[:, cols] = jnp.dot(d, pw_ref[gi], preferred_element_type=F32)
        z = ALPHA * x + mix_ref[...] * sc_ref[...]
        z_ref[...] = z
        h = _layer_norm(z, g_ref[...], b_ref[...])
        h_ref[...] = h
        hb_ref[...] = h.astype(BF16)

    row_blk = pl.BlockSpec((tm, D), lambda i: (i, 0))
    vec = pl.BlockSpec((1, D), lambda i: (0, 0))
    halo = pl.BlockSpec((POOL_HALO, D), lambda i: (jnp.maximum(i * (tm // POOL_HALO) - 1, 0), 0))
    return pl.pallas_call(
        body, name=name, grid=(M // tm,),
        in_specs=[row_blk, halo, pl.BlockSpec((n_groups, G, G), lambda i: (0, 0, 0)), vec, vec, vec],
        out_specs=[row_blk] * 4,
        out_shape=[jax.ShapeDtypeStruct((M, D), F32)] * 2 + [jax.ShapeDtypeStruct((M, D), BF16)] * 2,
        scratch_shapes=[pltpu.VMEM((tm + POOL_HALO, D), F32), pltpu.VMEM((tm, D), F32)],
        compiler_params=_cparams(),
    )(h0, h0, pw, scale, g, b)


def _pool_bwd(dz, diff, pw, scale, *, name):
    M, D = dz.shape
    n_groups, G, _ = pw.shape
    tm = _pick(M, (640, 128))
    nt = M // tm

    def body(dz_ref, halo_ref, diff_ref, pw_ref, sc_ref, dh_ref, dpw_ref, dsc_ref, ext_ref, q_ref):
        i = pl.program_id(0)
        dz_v = dz_ref[...]
        ext_ref[0:tm, :] = dz_v
        ext_ref[tm:, :] = jnp.where(i == nt - 1, 0.0, halo_ref[...])
        tok = i * tm + lax.broadcasted_iota(jnp.int32, (tm + POOL_HALO, 1), 0)

        @pl.when(i == 0)
        def _():
            dpw_ref[...] = jnp.zeros_like(dpw_ref)
            dsc_ref[...] = jnp.zeros_like(dsc_ref)

        for gi, win in enumerate(POOL_WINDOWS):
            cols = slice(gi * G, (gi + 1) * G)
            dmix = (ext_ref[:, cols] * sc_ref[:, cols]).astype(BF16)
            ddiff = _dot_nt(dmix, pw_ref[gi])
            count = jnp.minimum(tok + 1, win).astype(F32)
            q_ref[:, cols] = ddiff / count
            acc = -ddiff[0:tm]
            for k in range(win):
                acc = acc + q_ref[pl.ds(k, tm), cols]
            dh_ref[:, cols] = ALPHA * dz_v[:, cols] + acc
            d = diff_ref[:, cols]
            dpw_ref[gi] += lax.dot_general(d, dmix[0:tm], (((0,), (0,)), ((), ())), preferred_element_type=F32)
            mixed = jnp.dot(d, pw_ref[gi], preferred_element_type=F32)
            dsc_ref[:, cols] += jnp.sum(dz_v[:, cols] * mixed, axis=0, keepdims=True)

    row_blk = pl.BlockSpec((tm, D), lambda i: (i, 0))
    vec = pl.BlockSpec((1, D), lambda i: (0, 0))
    per_tile = tm // POOL_HALO
    halo = pl.BlockSpec((POOL_HALO, D), lambda i: (jnp.minimum((i + 1) * per_tile, nt * per_tile - 1), 0))
    wblk = pl.BlockSpec((n_groups, G, G), lambda i: (0, 0, 0))
    return pl.pallas_call(
        body, name=name, grid=(nt,),
        in_specs=[row_blk, halo, row_blk, wblk, vec],
        out_specs=[row_blk, wblk, vec],
        out_shape=[jax.ShapeDtypeStruct((M, D), F32), jax.ShapeDtypeStruct((n_groups, G, G), F32),
                   jax.ShapeDtypeStruct((1, D), F32)],
        scratch_shapes=[pltpu.VMEM((tm + POOL_HALO, D), F32), pltpu.VMEM((tm + POOL_HALO, D), F32)],
        compiler_params=_cparams(),
    )(dz, dz, diff, pw, scale)


def _glu_interleave(x):
    s = x.shape
    n = s[-1] // (2 * GLU_CHUNK)
    return jnp.swapaxes(x.reshape(s[:-1] + (2, n, GLU_CHUNK)), -3, -2).reshape(s)


def _glu_deinterleave(x):
    s = x.shape
    n = s[-1] // (2 * GLU_CHUNK)
    return jnp.swapaxes(x.reshape(s[:-1] + (n, 2, GLU_CHUNK)), -3, -2).reshape(s)


W_PACK_ROWS = 256
W_PACK_PIECE = 64


def _interleaved_col(col, F):
    half, within = divmod(col, F)
    chunk, off = divmod(within, GLU_CHUNK)
    return chunk * 2 * GLU_CHUNK + half * GLU_CHUNK + off


def _w_in_grad_segments(dw, *, name):
    D, F2 = dw.shape
    width = F2 // N_DEV
    rows = _pick(D, (W_PACK_ROWS, 128))
    assert width % W_PACK_PIECE == 0

    def body(x_ref, o_ref):
        for j in range(N_DEV):
            for q in range(0, width, W_PACK_PIECE):
                at = _interleaved_col(width * j + q, F2 // 2)
                o_ref[j % 2, j // 2, :, q:q + W_PACK_PIECE] = x_ref[:, at:at + W_PACK_PIECE].astype(BF16)

    return pl.pallas_call(
        body, name=name, grid=(D // rows,),
        in_specs=[pl.BlockSpec((rows, F2), lambda r: (r, 0))],
        out_specs=pl.BlockSpec((2, N_CHIPS, rows, width), lambda r: (0, 0, r, 0)),
        out_shape=jax.ShapeDtypeStruct((2, N_CHIPS, D, width), BF16),
        compiler_params=_cparams(),
    )(dw)


def _w_in_interleaved(gathered, *, name):
    n_dev, depth, D, width = gathered.shape
    F = n_dev * width // 2
    rows = _pick(D, (W_PACK_ROWS, 128))
    assert width % W_PACK_PIECE == 0 and GLU_CHUNK % W_PACK_PIECE == 0 and F % GLU_CHUNK == 0
    interleaved = lambda col: _interleaved_col(col, F)

    def body(x_ref, o_ref):
        for j in range(n_dev):
            for q in range(0, width, W_PACK_PIECE):
                to = interleaved(width * j + q)
                o_ref[0, :, to:to + W_PACK_PIECE] = x_ref[j, 0, :, q:q + W_PACK_PIECE]

    return pl.pallas_call(
        body, name=name, grid=(depth, D // rows),
        in_specs=[pl.BlockSpec((n_dev, 1, rows, width), lambda l, r: (0, l, r, 0))],
        out_specs=pl.BlockSpec((1, rows, n_dev * width), lambda l, r: (l, r, 0)),
        out_shape=jax.ShapeDtypeStruct((depth, D, n_dev * width), gathered.dtype),
        compiler_params=_cparams(),
    )(gathered)


def _taps(u_ref, head_ref, r0):
    src, base = (head_ref, CONV_HALO) if r0 == 0 else (u_ref, r0)
    return tuple(src[pl.ds(base - k, CONV_STRIP), :] for k in range(3))


def _conv_glu_fwd(u, cw, cb, *, name):
    M, F2 = u.shape
    tm = _pick(M, (640, 128))
    tc = 2 * GLU_CHUNK

    def body(u_ref, halo_ref, w_ref, b_ref, o_ref, c_ref, head_ref):
        i = pl.program_id(0)
        head_ref[0:CONV_HALO, :] = jnp.where(i == 0, 0.0, halo_ref[...])
        head_ref[CONV_HALO:, :] = u_ref[0:CONV_STRIP, :]
        w0, w1, w2, b = w_ref[0:1, :], w_ref[1:2, :], w_ref[2:3, :], b_ref[...]
        for r0 in range(0, tm, CONV_STRIP):
            u0, u1, u2 = _taps(u_ref, head_ref, r0)
            c = b + w0 * u2 + w1 * u1 + w2 * u0
            c_ref[pl.ds(r0, CONV_STRIP), :] = c.astype(BF16)
            a, g = c[:, :GLU_CHUNK], c[:, GLU_CHUNK:]
            o_ref[pl.ds(r0, CONV_STRIP), :] = (a * jax.nn.sigmoid(a) * g).astype(BF16)

    per_tile = tm // CONV_HALO
    return pl.pallas_call(
        body, name=name, grid=(M // tm, F2 // tc),
        in_specs=[pl.BlockSpec((tm, tc), lambda i, j: (i, j)),
                  pl.BlockSpec((CONV_HALO, tc), lambda i, j: (jnp.maximum(i * per_tile - 1, 0), j)),
                  pl.BlockSpec((3, tc), lambda i, j: (0, j)), pl.BlockSpec((1, tc), lambda i, j: (0, j))],
        out_specs=[pl.BlockSpec((tm, GLU_CHUNK), lambda i, j: (i, j)), pl.BlockSpec((tm, tc), lambda i, j: (i, j))],
        out_shape=[jax.ShapeDtypeStruct((M, F2 // 2), BF16), jax.ShapeDtypeStruct((M, F2), BF16)],
        scratch_shapes=[pltpu.VMEM((CONV_HALO + CONV_STRIP, tc), F32)],
        compiler_params=_cparams(),
    )(u, u, cw, cb)


def _conv_glu_bwd(u, c, dact, cw, *, name):
    M, F2 = u.shape
    tm = _pick(M, (640, 128))
    nt = M // tm
    tc = 2 * GLU_CHUNK

    def body(u_ref, c_ref, da_ref, w_ref, du_ref, dw_ref, db_ref, dcx_ref, carry_ref):
        i = pl.program_id(1)
        w0, w1, w2 = w_ref[0:1, :], w_ref[1:2, :], w_ref[2:3, :]

        @pl.when(i == 0)
        def _():
            dw_ref[...] = jnp.zeros_like(dw_ref)
            db_ref[...] = jnp.zeros_like(db_ref)
            carry_ref[...] = jnp.zeros_like(carry_ref)

        def fold(t):
            return sum(t[r:r + 8] for r in range(0, CONV_STRIP, 8))

        dcx_ref[tm:, :] = carry_ref[...]
        s_b = s_0 = s_1 = s_2 = jnp.zeros((8, tc), F32)
        for r0 in reversed(range(0, tm, CONV_STRIP)):
            rows = pl.ds(r0, CONV_STRIP)
            c_v = c_ref[rows, :].astype(F32)
            a, g = c_v[:, :GLU_CHUNK], c_v[:, GLU_CHUNK:]
            sig = jax.nn.sigmoid(a)
            dact_v = da_ref[rows, :].astype(F32)
            d_a = dact_v * g * (sig * (1.0 + a * (1.0 - sig)))
            d_g = dact_v * (a * sig)
            dc = jnp.concatenate([d_a, d_g], axis=1)
            dcx_ref[rows, :] = dc
            dc1, dc2 = dcx_ref[pl.ds(r0 + 1, CONV_STRIP), :], dcx_ref[pl.ds(r0 + 2, CONV_STRIP), :]
            du_ref[rows, :] = (w2 * dc + w1 * dc1 + w0 * dc2).astype(BF16)
            u_v = u_ref[rows, :]
            s_b, s_0, s_1, s_2 = s_b + fold(dc), s_0 + fold(dc2 * u_v), s_1 + fold(dc1 * u_v), s_2 + fold(dc * u_v)
        carry_ref[...] = dcx_ref[0:CONV_HALO, :]
        db_ref[...] += jnp.sum(s_b, axis=0, keepdims=True)
        dw_ref[0:1, :] += jnp.sum(s_0, axis=0, keepdims=True)
        dw_ref[1:2, :] += jnp.sum(s_1, axis=0, keepdims=True)
        dw_ref[2:3, :] += jnp.sum(s_2, axis=0, keepdims=True)

    per_tile = tm // CONV_HALO
    rev = lambda i: nt - 1 - i
    return pl.pallas_call(
        body, name=name, grid=(F2 // tc, nt),
        in_specs=[pl.BlockSpec((tm, tc), lambda j, i: (rev(i), j)), pl.BlockSpec((tm, tc), lambda j, i: (rev(i), j)),
                  pl.BlockSpec((tm, GLU_CHUNK), lambda j, i: (rev(i), j)),
                  pl.BlockSpec((3, tc), lambda j, i: (0, j))],
        out_specs=[pl.BlockSpec((tm, tc), lambda j, i: (rev(i), j)),
                   pl.BlockSpec((3, tc), lambda j, i: (0, j)), pl.BlockSpec((1, tc), lambda j, i: (0, j))],
        out_shape=[jax.ShapeDtypeStruct((M, F2), BF16), jax.ShapeDtypeStruct((3, F2), F32),
                   jax.ShapeDtypeStruct((1, F2), F32)],
        scratch_shapes=[pltpu.VMEM((tm + CONV_HALO, tc), F32), pltpu.VMEM((CONV_HALO, tc), F32)],
        compiler_params=_cparams(),
    )(u, c, dact, cw)


def _split3(x):
    hi = x.astype(BF16)
    r = x - hi.astype(F32)
    mid = r.astype(BF16)
    lo = (r - mid.astype(F32)).astype(BF16)
    return hi, mid, lo


def _tri_sum(tri, x):
    return sum(jnp.dot(tri, part, preferred_element_type=F32) for part in _split3(x))


def _log_sigmoid(x):
    return jnp.minimum(x, 0.0) - jnp.log1p(jnp.exp(-jnp.abs(x)))


def _forget_cumsum(pre, bias, *, name):
    M, C = pre.shape
    tm = _pick(M, (640, 128))

    def body(p_ref, b_ref, c_ref, carry_ref):
        i = pl.program_id(0)

        @pl.when(i == 0)
        def _():
            carry_ref[...] = jnp.zeros_like(carry_ref)

        logf = _log_sigmoid(p_ref[...] + b_ref[...])
        r = lax.broadcasted_iota(jnp.int32, (tm, tm), 0)
        s = lax.broadcasted_iota(jnp.int32, (tm, tm), 1)
        c_ref[...] = _tri_sum((s <= r).astype(BF16), logf) + carry_ref[...]
        carry_ref[...] = c_ref[pl.ds(tm - 1, 1), :]

    return pl.pallas_call(
        body, name=name, grid=(M // tm,),
        in_specs=[pl.BlockSpec((tm, C), lambda i: (i, 0)), pl.BlockSpec((1, C), lambda i: (0, 0))],
        out_specs=pl.BlockSpec((tm, C), lambda i: (i, 0)),
        out_shape=jax.ShapeDtypeStruct((M, C), F32),
        scratch_shapes=[pltpu.VMEM((1, C), F32)],
        compiler_params=_cparams(),
    )(pre, bias)


def _forget_cumsum_bwd(dc, pre, bias, *, name):
    M, C = pre.shape
    tm = _pick(M, (640, 128))
    nt = M // tm

    def body(dc_ref, p_ref, b_ref, dp_ref, db_ref, carry_ref, run_ref):
        i = pl.program_id(0)

        @pl.when(i == 0)
        def _():
            carry_ref[...] = jnp.zeros_like(carry_ref)
            db_ref[...] = jnp.zeros_like(db_ref)

        r = lax.broadcasted_iota(jnp.int32, (tm, tm), 0)
        s = lax.broadcasted_iota(jnp.int32, (tm, tm), 1)
        run_ref[...] = _tri_sum((s >= r).astype(BF16), dc_ref[...]) + carry_ref[...]
        carry_ref[...] = run_ref[pl.ds(0, 1), :]
        dpre = run_ref[...] * jax.nn.sigmoid(-(p_ref[...] + b_ref[...]))
        dp_ref[...] = dpre.astype(BF16)
        db_ref[...] += jnp.sum(dpre, axis=0, keepdims=True)

    rev_blk = pl.BlockSpec((tm, C), lambda i: (nt - 1 - i, 0))
    vec = pl.BlockSpec((1, C), lambda i: (0, 0))
    return pl.pallas_call(
        body, name=name, grid=(nt,),
        in_specs=[rev_blk, rev_blk, vec], out_specs=[rev_blk, vec],
        out_shape=[jax.ShapeDtypeStruct((M, C), BF16), jax.ShapeDtypeStruct((1, C), F32)],
        scratch_shapes=[pltpu.VMEM((1, C), F32), pltpu.VMEM((tm, C), F32)],
        compiler_params=_cparams(),
    )(dc, pre, bias)


def _causal_mask(tm):
    key = lax.broadcasted_iota(jnp.int32, (tm, tm), 0)
    query = lax.broadcasted_iota(jnp.int32, (tm, tm), 1)
    return key <= query


def _dot_nt(a, b):
    return lax.dot_general(a, b, (((1,), (1,)), ((), ())), preferred_element_type=F32)


def _loop_unrolled(n, step, init, unroll):
    def trip(p, carry):
        for r in range(unroll):
            carry = step(unroll * p + r, carry)
        return carry
    carry = lax.fori_loop(0, n // unroll, trip, init)
    return lax.fori_loop(unroll * (n // unroll), n, step, carry)


def _rows(main, extras, total):
    tm = main.shape[1]
    used = sum(e.shape[0] for e in extras)
    tile = jnp.concatenate([e.astype(BF16) for e in extras] + [jnp.zeros((ATTN_EXTRA - used, tm), BF16)], axis=0)
    rest = total - main.shape[0] - ATTN_EXTRA
    return jnp.concatenate([main, tile] + ([jnp.zeros((rest, tm), BF16)] if rest else []), axis=0)


def _attn_specs(H, M, dh, tm):
    nt = M // tm
    qT_blk = pl.BlockSpec((1, dh, tm), lambda h, i: (h, 0, i))
    row_blk = pl.BlockSpec((1, 1, tm), lambda h, i: (h, 0, i))
    key_blk = pl.BlockSpec((1, M, LANES), lambda h, i: (h, 0, 0))
    keyT_blk = pl.BlockSpec((1, nt, dh + ATTN_EXTRA, tm), lambda h, i: (h, 0, 0, 0))
    return qT_blk, row_blk, key_blk, keyT_blk


def _pair_specs(M, dh, tm):
    assert 2 * dh == LANES
    cols = lambda section: pl.BlockSpec((tm, LANES), lambda p, i: (i, section + p))
    headsT = pl.BlockSpec((2, dh, tm), lambda p, i: (p, 0, i))
    return cols, headsT


def _attn_pack(qkv, c, H, *, name):
    M, D3 = qkv.shape
    D = D3 // 3
    dh = D // H
    tm = _pick(M, (640, 128))
    nt = M // tm
    cols, headsT = _pair_specs(M, dh, tm)
    key_blk = pl.BlockSpec((2, tm, LANES), lambda p, i: (p, i, 0))
    keyT_blk = pl.BlockSpec((2, 1, dh + ATTN_EXTRA, tm), lambda p, i: (p, i, 0, 0))

    def body(k_ref, v_ref, q_ref, c_ref, ka_ref, va_ref, qT_ref, kT_ref, vT_ref):
        p = pl.program_id(0)
        lane = lax.broadcasted_iota(jnp.int32, (tm, LANES), 1)
        k_v, v_v = k_ref[...], v_ref[...]
        ones_v = jnp.where((lane >= dh) & (lane < dh + 3), 1.0, 0.0)
        for e in range(2):
            ck = jnp.sum(jnp.where(lane == 2 * p + e, c_ref[...], 0.0), axis=1, keepdims=True)
            hi, mid, lo = _split3(-ck)
            extra = jnp.where(lane == dh + 3, hi.astype(F32), jnp.where(lane == dh + 4, mid.astype(F32),
                              jnp.where(lane == dh + 5, lo.astype(F32), ones_v)))
            first = lambda t: t if e == 0 else pltpu.roll(t, dh, 1)
            ka_ref[e] = jnp.where(lane < dh, first(k_v), extra.astype(BF16))
            va_ref[e] = jnp.where(lane < dh, first(v_v), ones_v.astype(BF16))
        row = lax.broadcasted_iota(jnp.int32, (ATTN_EXTRA, tm), 0)
        tail = jnp.where(row == 0, 1.0, 0.0).astype(BF16)
        for src, dst in ((k_v, kT_ref), (v_v, vT_ref)):
            t = src.T
            for e in range(2):
                dst[e, 0, 0:dh, :] = t[e * dh:(e + 1) * dh]
                dst[e, 0, dh:, :] = tail
        qT_ref[...] = q_ref[...].T.reshape(2, dh, tm)

    n_sec = D // LANES
    return pl.pallas_call(
        body, name=name, grid=(H // 2, nt),
        in_specs=[cols(0), cols(n_sec), cols(2 * n_sec), pl.BlockSpec((tm, LANES), lambda p, i: (i, 0))],
        out_specs=[key_blk, key_blk, headsT, keyT_blk, keyT_blk],
        out_shape=[jax.ShapeDtypeStruct((H, M, LANES), BF16)] * 2 + [jax.ShapeDtypeStruct((H, dh, M), BF16)]
        + [jax.ShapeDtypeStruct((H, nt, dh + ATTN_EXTRA, tm), BF16)] * 2,
        compiler_params=_cparams(),
    )(qkv, qkv, qkv, c)


def _heads_split(x, H, *, name):
    M, D = x.shape
    dh = D // H
    tm = _pick(M, (640, 128))
    cols, headsT = _pair_specs(M, dh, tm)

    def body(x_ref, o_ref):
        o_ref[...] = x_ref[...].T.reshape(2, dh, tm)

    return pl.pallas_call(
        body, name=name, grid=(H // 2, M // tm), in_specs=[cols(0)], out_specs=headsT,
        out_shape=jax.ShapeDtypeStruct((H, dh, M), x.dtype), compiler_params=_cparams(),
    )(x)


def _heads_merge(parts, M, tm, *, name):
    H = parts[0].shape[0]
    dh = LANES // 2
    blocked = [t.ndim == 4 for t in parts]
    cols, headsT = _pair_specs(M, dh, tm)

    def body(*refs):
        for src, dst, b in zip(refs[:len(parts)], refs[len(parts):], blocked):
            t = jnp.concatenate([src[e, 0, 0:dh, :] for e in range(2)], axis=0) if b else src[...].reshape(2 * dh, tm)
            dst[...] = t.astype(BF16).T

    in_specs = [pl.BlockSpec((2, 1, t.shape[2], tm), lambda p, i: (p, i, 0, 0)) if b else headsT
                for t, b in zip(parts, blocked)]
    return pl.pallas_call(
        body, name=name, grid=(H // 2, M // tm), in_specs=in_specs, out_specs=[cols(0)] * len(parts),
        out_shape=[jax.ShapeDtypeStruct((M, H * dh), BF16)] * len(parts), compiler_params=_cparams(),
    )(*parts)


def _attn_fwd(qT, cq, k_aug, vT_aug, *, name):
    H, dh, M = qT.shape
    tm = vT_aug.shape[-1]
    qT_blk, row_blk, key_blk, keyT_blk = _attn_specs(H, M, dh, tm)

    def body(qT_ref, cq_ref, k_ref, vT_ref, oT_ref, lse_ref):
        i = pl.program_id(1)
        ones = jnp.ones((3, tm), BF16)
        qa = _rows(qT_ref[0] * jnp.asarray(dh ** -0.5, BF16), [*_split3(cq_ref[0]), ones], LANES)

        def block(j, carry, masked):
            m, acc = carry
            keys = pl.ds(pl.multiple_of(j * tm, tm), tm)
            sT = jnp.dot(k_ref[0, keys, :], qa, preferred_element_type=F32)
            if masked:
                sT = jnp.where(_causal_mask(tm), sT, NEG_INF)
            m_new = jnp.maximum(m, jnp.max(sT, axis=0, keepdims=True))
            pT = jnp.exp(sT - m_new).astype(BF16)
            acc = jnp.exp(m - m_new) * acc + jnp.dot(vT_ref[0, j], pT, preferred_element_type=F32)
            return m_new, acc

        init = (jnp.full((1, tm), NEG_INF, F32), jnp.zeros((dh + ATTN_EXTRA, tm), F32))
        m, acc = block(i, _loop_unrolled(i, lambda j, c: block(j, c, False), init, 4), True)
        l = acc[dh:dh + 1, :]
        oT_ref[0] = (acc[0:dh, :] / l).astype(BF16)
        lse_ref[0] = m + jnp.log(l)

    return pl.pallas_call(
        body, name=name, grid=(H, M // tm),
        in_specs=[qT_blk, row_blk, key_blk, keyT_blk], out_specs=[qT_blk, row_blk],
        out_shape=[jax.ShapeDtypeStruct((H, dh, M), BF16), jax.ShapeDtypeStruct((H, 1, M), F32)],
        compiler_params=_cparams(),
    )(qT, cq, k_aug, vT_aug)


def _attn_bwd(qT, oT, doT, lse, cq, k_aug, v_aug, kT_aug, *, name):
    H, dh, M = qT.shape
    tm = kT_aug.shape[-1]
    qT_blk, row_blk, key_blk, keyT_blk = _attn_specs(H, M, dh, tm)
    dvT_blk = pl.BlockSpec((1, M // tm, dh, tm), lambda h, i: (h, 0, 0, 0))

    def body(qT_ref, oT_ref, doT_ref, lse_ref, cq_ref, k_ref, v_ref, kT_ref, dqT_ref, dcq_ref, dkT_ref, dvT_ref):
        i = pl.program_id(1)

        @pl.when(i == 0)
        def _():
            dkT_ref[...] = jnp.zeros_like(dkT_ref)
            dvT_ref[...] = jnp.zeros_like(dvT_ref)

        qsT = qT_ref[0] * jnp.asarray(dh ** -0.5, BF16)
        doT = doT_ref[0]
        delta = jnp.sum(doT.astype(F32) * oT_ref[0].astype(F32), axis=0, keepdims=True)
        ones = jnp.ones((3, tm), BF16)
        qa = _rows(qsT, [*_split3(cq_ref[0] - lse_ref[0]), ones], LANES)
        da = _rows(doT, _split3(-delta), LANES)
        q1 = _rows(qsT, [ones[0:1]], dh + ATTN_EXTRA)

        def block(j, dq, masked):
            keys = pl.ds(pl.multiple_of(j * tm, tm), tm)
            pT = jnp.exp(jnp.dot(k_ref[0, keys, :], qa, preferred_element_type=F32))
            if masked:
                pT = jnp.where(_causal_mask(tm), pT, 0.0)
            ds_b = (pT * jnp.dot(v_ref[0, keys, :], da, preferred_element_type=F32)).astype(BF16)
            dvT_ref[0, j] += _dot_nt(doT, pT.astype(BF16))
            dkT_ref[0, j] += _dot_nt(q1, ds_b)
            return dq + jnp.dot(kT_ref[0, j], ds_b, preferred_element_type=F32)

        init = jnp.zeros((dh + ATTN_EXTRA, tm), F32)
        dq = block(i, _loop_unrolled(i, lambda j, c: block(j, c, False), init, 4), True)
        dqT_ref[0] = (dq[0:dh, :] * dh ** -0.5).astype(BF16)
        dcq_ref[0] = dq[dh:dh + 1, :]

    return pl.pallas_call(
        body, name=name, grid=(H, M // tm),
        in_specs=[qT_blk, qT_blk, qT_blk, row_blk, row_blk, key_blk, key_blk, keyT_blk],
        out_specs=[qT_blk, row_blk, keyT_blk, dvT_blk],
        out_shape=[jax.ShapeDtypeStruct((H, dh, M), BF16), jax.ShapeDtypeStruct((H, 1, M), F32),
                   jax.ShapeDtypeStruct(kT_aug.shape, F32), jax.ShapeDtypeStruct((H, M // tm, dh, tm), F32)],
        compiler_params=_cparams(),
    )(qT, oT, doT, lse, cq, k_aug, v_aug, kT_aug)


def kernel(x, meta, pool_w, pool_scale, w_kv, w_f, b_f, w_q, w_o, ffn_w_in, ffn_conv_w, ffn_conv_b, ffn_w_out, ln_g, ln_b, loss_target, m_meta, m_pool_w, m_pool_scale, m_w_kv, m_w_f, m_b_f, m_w_q, m_w_o, m_ffn_w_in, m_ffn_conv_w, m_ffn_conv_b, m_ffn_w_out, m_ln_g, m_ln_b, v_meta, v_pool_w, v_pool_scale, v_w_kv, v_w_f, v_b_f, v_w_q, v_w_o, v_ffn_w_in, v_ffn_conv_w, v_ffn_conv_b, v_ffn_w_out, v_ln_g, v_ln_b):
    local = dict(meta=meta, pool_w=pool_w, pool_scale=pool_scale, w_kv=w_kv, w_f=w_f, b_f=b_f, w_q=w_q, w_o=w_o,
                 ffn_w_in=ffn_w_in, ffn_conv_w=ffn_conv_w, ffn_conv_b=ffn_conv_b, ffn_w_out=ffn_w_out,
                 ln_g=ln_g, ln_b=ln_b)
    mom1 = dict(meta=m_meta, pool_w=m_pool_w, pool_scale=m_pool_scale, w_kv=m_w_kv, w_f=m_w_f, b_f=m_b_f,
                w_q=m_w_q, w_o=m_w_o, ffn_w_in=m_ffn_w_in, ffn_conv_w=m_ffn_conv_w, ffn_conv_b=m_ffn_conv_b,
                ffn_w_out=m_ffn_w_out, ln_g=m_ln_g, ln_b=m_ln_b)
    mom2 = dict(meta=v_meta, pool_w=v_pool_w, pool_scale=v_pool_scale, w_kv=v_w_kv, w_f=v_w_f, b_f=v_b_f,
                w_q=v_w_q, w_o=v_w_o, ffn_w_in=v_ffn_w_in, ffn_conv_w=v_ffn_conv_w, ffn_conv_b=v_ffn_conv_b,
                ffn_w_out=v_ffn_w_out, ln_g=v_ln_g, ln_b=v_ln_b)
    axis_of = dict(PARAMS)

    S, D = x.shape[1], x.shape[2]
    H = b_f.shape[0]
    dh = D // H
    n_tok = N_META + S
    M = _round_up(n_tok, LANES)
    tm = _pick(M, (640, 128))
    nt = M // tm
    F2 = ffn_conv_b.shape[1]
    F = F2 // 2
    depth = ffn_conv_b.shape[0]

    small_sharded = [n for n in SMALL if axis_of[n] is not None]
    got = _all_gather([local[n].astype(BF16) for n in BIG] + [_pack([local[n] for n in small_sharded])],
                      "gather_weights")
    wb = {n: _from_blocks(blk, axis_of[n]) for n, blk in zip(BIG, got) if n != "ffn_w_in"}
    small_blocks = _unpack(got[-1], [local[n].shape for n in small_sharded], lead=(N_DEV,))
    wf32 = {n: _from_blocks(blk, axis_of[n]) for n, blk in zip(small_sharded, small_blocks)}

    pw = wb["pool_w"][0]
    wf_pad = jnp.pad(wf32["w_f"].astype(BF16), ((0, 0), (0, LANES - H)))
    w_qkv = jnp.concatenate([wb["w_kv"], wb["w_q"][0]], axis=1)
    w_att = jnp.concatenate([w_qkv, wf_pad], axis=1)
    wo = wb["w_o"][0]
    w_in = _w_in_interleaved(got[BIG.index("ffn_w_in")], name="w_in_interleave")
    w_out = wb["ffn_w_out"]
    conv_w = _glu_interleave(wf32["ffn_conv_w"])
    conv_b = _glu_interleave(ffn_conv_b)[:, None, :]
    scale = wf32["pool_scale"]
    g_ln, b_ln = wf32["ln_g"], wf32["ln_b"]
    ln = lambda i, j: (g_ln[i, j][None, :], b_ln[i, j][None, :])
    bias_f = jnp.pad(b_f, (0, LANES - H))[None, :]

    pad_rows = M - n_tok
    h0 = jnp.concatenate([wf32["meta"], x[0], jnp.zeros((pad_rows, D), F32)], axis=0)
    target = jnp.concatenate([jnp.zeros((N_META, D), F32), loss_target[0], jnp.zeros((pad_rows, D), F32)], axis=0)

    z1, h1, h1b, diff = _pool_fwd(h0, pw, scale, *ln(0, 0), name="pool_fwd")
    u0 = _mm(h1b, w_in[0], name="ffn0_up", out_dtype=F32)
    act0, c0 = _conv_glu_fwd(u0, conv_w[0], conv_b[0], name="ffn0_conv")
    z2, h2, h2b = _mm_ln(act0, w_out[0], h1, *ln(0, 1), name="ffn0_down_ln")

    qkv = _mm(h2b, w_qkv, name="attn_qkv", out_dtype=BF16)
    pre = _mm(h2b, wf_pad, name="attn_gate", out_dtype=F32)
    c = _forget_cumsum(pre, bias_f, name="attn_cumsum")
    k_aug, v_aug, qT_h, kT_aug, vT_aug = _attn_pack(qkv, c, H, name="attn_pack")
    cq = c[:, :H].T[:, None, :]
    oT_h, lse = _attn_fwd(qT_h, cq, k_aug, vT_aug, name="attn_fwd")
    o, = _heads_merge([oT_h], M, tm, name="attn_o_merge")
    z3, h3, h3b = _mm_ln(o, wo, h2, *ln(1, 0), name="attn_out_ln")

    u1 = _mm(h3b, w_in[1], name="ffn1_up", out_dtype=F32)
    act1, c1 = _conv_glu_fwd(u1, conv_w[1], conv_b[1], name="ffn1_conv")
    z4, dy, loss_part = _mm_ln(act1, w_out[1], h3, *ln(1, 1), name="ffn1_down_loss", target=target, n_tok=n_tok)
    loss = lax.psum(loss_part[0, 0], ("x", "y", "c"))

    grads = {}

    def ffn_bwd(layer, dz_b, u, c_conv, act, h_in_b, tag):
        dact = _mm(dz_b, w_out[layer], name=tag + "_dact", out_dtype=BF16, b_transposed=True)
        du, dcw, dcb = _conv_glu_bwd(u, c_conv, dact, conv_w[layer], name=tag + "_conv_bwd")
        dh = _mm(du, w_in[layer], name=tag + "_dh", out_dtype=F32, b_transposed=True)
        d_w_out = _mm_tn(act, dz_b, name=tag + "_dw_out")
        d_w_in = _w_in_grad_segments(_mm_tn(h_in_b, du, name=tag + "_dw_in"), name=tag + "_dw_in_segments")
        return dh, d_w_in, d_w_out, _glu_deinterleave(dcw), _glu_deinterleave(dcb)[0]

    dz4, dz4b, dg11, db11 = _ln_bwd(dy, z4, g_ln[1, 1][None, :], name="ln4_bwd")
    dh3, dwin1, dwout1, dcw1, dcb1 = ffn_bwd(1, dz4b, u1, c1, act1, h3b, "ffn1")
    dz3, dz3b, dg10, db10 = _ln_bwd(dh3, z3, g_ln[1, 0][None, :], name="ln3_bwd", dz_next=dz4)

    do = _mm(dz3b, wo, name="attn_do", out_dtype=BF16, b_transposed=True)
    grads["w_o"] = _mm_tn(o, dz3b, name="attn_dw_o")[None]
    doT_h = _heads_split(do, H, name="attn_do_split")
    dqT_h, dcq, dkT_a, dvT_h = _attn_bwd(qT_h, oT_h, doT_h, lse, cq, k_aug, v_aug, kT_aug, name="attn_bwd")
    dck = -dkT_a[:, :, dh, :].reshape(H, M)
    dc = jnp.pad((dcq[:, 0, :] + dck).T, ((0, 0), (0, LANES - H)))
    dpre, dbias = _forget_cumsum_bwd(dc, pre, bias_f, name="attn_cumsum_bwd")
    d_att = jnp.concatenate([*_heads_merge([dkT_a, dvT_h, dqT_h], M, tm, name="attn_dqkv_merge"), dpre], axis=1)
    dh2 = _mm(d_att, w_att, name="attn_dh", out_dtype=F32, b_transposed=True)
    d_w_att = _mm_tn(h2b, d_att, name="attn_dw_qkv")
    grads["w_kv"] = d_w_att[:, :2 * D]
    grads["w_q"] = d_w_att[:, 2 * D:3 * D][None]
    grads["w_f"] = d_w_att[:, 3 * D:3 * D + H]
    grads["b_f"] = dbias[0, :H]
    dz2, dz2b, dg01, db01 = _ln_bwd(dh2, z2, g_ln[0, 1][None, :], name="ln2_bwd", dz_next=dz3)

    dh1, dwin0, dwout0, dcw0, dcb0 = ffn_bwd(0, dz2b, u0, c0, act0, h1b, "ffn0")
    dz1, _, dg00, db00 = _ln_bwd(dh1, z1, g_ln[0, 0][None, :], name="ln1_bwd", dz_next=dz2)
    dh0, dpw, dscale = _pool_bwd(dz1, diff, pw, scale, name="pool_bwd")

    grads["meta"] = dh0[:N_META]
    grads["pool_w"] = dpw[None]
    grads["pool_scale"] = dscale
    seg_w_in = jnp.stack([dwin0, dwin1], axis=2)
    grads["ffn_w_out"] = jnp.stack([dwout0, dwout1])
    grads["ffn_conv_w"] = jnp.stack([dcw0, dcw1])
    grads["ffn_conv_b"] = jnp.stack([dcb0, dcb1])
    grads["ln_g"] = jnp.stack([jnp.stack([dg00[0], dg01[0]]), jnp.stack([dg10[0], dg11[0]])])
    grads["ln_b"] = jnp.stack([jnp.stack([db00[0], db01[0]]), jnp.stack([db10[0], db11[0]])])
    grad_x = dh0[N_META:n_tok][None]

    blocks = {n: _to_blocks(grads[n], axis_of[n]) for n, _ in PARAMS if n != "ffn_w_in"}
    me = _coords()
    core, chip = (jnp.reshape(v, (1,)).astype(jnp.int32) for v in (me[2], 2 * me[0] + me[1]))
    by_core = lambda t: jnp.swapaxes(t.reshape((N_CHIPS, 2) + t.shape[1:]), 0, 1).astype(BF16)
    segs = [seg_w_in if n == "ffn_w_in" else by_core(blocks[n]) for n in BIG]
    pair = _exchange(segs, "grad_pair_exchange", (1,), lambda pos: pos[2])
    rows_of = lambda t, lead: t.reshape(lead + (-1, t.shape[-1]))
    partial = [_pair_sum(rows_of(s, (2,)), core, rows_of(p[0], ()), name="grad_pair_sum_" + n).reshape(s.shape[1:])
               for n, s, p in zip(BIG, segs, pair)]
    recv = _exchange(partial, "grad_chip_exchange", (4, 2, 6), lambda pos: 2 * pos[0] + pos[1])
    recv_small = _exchange([_pack([blocks[n] for n in SMALL], lead=(N_DEV,))], "grad_small_exchange",
                           tuple(range(1, N_DEV)), _index, by_sender=True)[0]
    results = {}
    for n, own, r in zip(BIG, partial, recv):
        shape = local[n].shape
        as2d = lambda t: t.reshape(-1, shape[-1])
        outs = _adamw(rows_of(own, (N_CHIPS,)), chip, rows_of(r, (N_CHIPS - 1,)), as2d(local[n]), as2d(mom1[n]),
                      as2d(mom2[n]), name="adamw_" + n)
        results[n] = [o_.reshape(shape) for o_ in outs]
    outs = _adamw(recv_small, jnp.zeros((1,), jnp.int32), recv_small[1:],
                  *[_pack([d[n] for n in SMALL]) for d in (local, mom1, mom2)], name="adamw_small")
    small_out = [_unpack(o_, [local[n].shape for n in SMALL]) for o_ in outs]
    for i, n in enumerate(SMALL):
        results[n] = [small_out[k][i] for k in range(4)]
    return (loss, grad_x, *[results[n][k] for k in range(4) for n, _ in PARAMS])
```

```python
import jax
import jax.numpy as jnp
from jax import lax
from jax.experimental import pallas as pl
from jax.experimental.pallas import tpu as pltpu

F32, BF16 = jnp.float32, jnp.bfloat16
MESH = pl.DeviceIdType.MESH

N_DEV = 8
N_CHIPS = 4
N_META = 16
POOL_WINDOWS = (2, 4, 8, 16)
POOL_HALO = 16
CONV_HALO = 8
CONV_STRIP = 16
ALPHA = 4.0 ** 0.25
LN_EPS = 1e-5
NEG_INF = -1e30
ADAM_LR, ADAM_B1, ADAM_B2, ADAM_EPS, ADAM_WD, ADAM_STEP = 0.001, 0.9, 0.999, 1e-08, 0.01, 10

LANES = 128
PACK_COLS = 1024
ADAM_TILE_BYTES = 4 << 20
MM_TN_WINDOW_BYTES = 28 << 20
GLU_CHUNK = 256
ATTN_EXTRA = 16
VMEM_LIMIT = 56 * 1024 * 1024

PARAMS = (("meta", 1), ("pool_w", 2), ("pool_scale", 1), ("w_kv", 1), ("w_f", 0), ("b_f", None),
          ("w_q", 1), ("w_o", 1), ("ffn_w_in", 2), ("ffn_conv_w", 2), ("ffn_conv_b", None),
          ("ffn_w_out", 1), ("ln_g", 2), ("ln_b", 2))
BIG = ("pool_w", "w_kv", "w_q", "w_o", "ffn_w_in", "ffn_w_out")
SMALL = ("meta", "pool_scale", "w_f", "b_f", "ffn_conv_w", "ffn_conv_b", "ln_g", "ln_b")


def _cparams(**kw):
    return pltpu.CompilerParams(vmem_limit_bytes=VMEM_LIMIT, **kw)


def _pick(n, cands):
    for c in cands:
        if n % c == 0:
            return c
    return n


def _round_up(n, m):
    return (n + m - 1) // m * m


def _pack(pieces, lead=()):
    flat = []
    for p in pieces:
        v = p.reshape(lead + (-1,))
        flat.append(jnp.pad(v, [(0, 0)] * len(lead) + [(0, _round_up(v.shape[-1], PACK_COLS) - v.shape[-1])]))
    v = jnp.concatenate(flat, axis=-1)
    rows = _round_up(v.shape[-1] // PACK_COLS, 8)
    v = jnp.pad(v, [(0, 0)] * len(lead) + [(0, rows * PACK_COLS - v.shape[-1])])
    return v.reshape(lead + (rows, PACK_COLS))


def _unpack(buf, shapes, lead=()):
    flat = buf.reshape(lead + (-1,))
    out, off = [], 0
    for s in shapes:
        n = 1
        for d in s:
            n *= d
        out.append(flat[..., off:off + n].reshape(lead + tuple(s)))
        off += _round_up(n, PACK_COLS)
    return out


def _to_blocks(full, axis):
    if axis is None:
        return jnp.broadcast_to(full[None], (N_DEV,) + full.shape)
    s = full.shape
    x = full.reshape(s[:axis] + (N_DEV, s[axis] // N_DEV) + s[axis + 1:])
    return jnp.moveaxis(x, axis, 0)


def _from_blocks(blocks, axis):
    x = jnp.moveaxis(blocks, 0, axis)
    s = x.shape
    return x.reshape(s[:axis] + (s[axis] * s[axis + 1],) + s[axis + 2:])


def _coords():
    return lax.axis_index("x"), lax.axis_index("y"), lax.axis_index("c")


def _flip(pos, mask):
    x, y, c = pos
    return (1 - x if mask & 4 else x, 1 - y if mask & 2 else y, 1 - c if mask & 1 else c)


def _index(pos):
    x, y, c = pos
    return 4 * x + 2 * y + c


def _comm_call(body, name, arrays, out_shapes):
    n = len(arrays)
    hbm = pl.BlockSpec(memory_space=pl.ANY)
    return pl.pallas_call(
        body, name=name, out_shape=out_shapes, in_specs=[hbm] * n, out_specs=[hbm] * n,
        scratch_shapes=[pltpu.SemaphoreType.DMA((7 * n,)), pltpu.SemaphoreType.DMA((7 * n,)),
                        pltpu.SemaphoreType.DMA((n,))],
    )(*arrays)


def _all_gather(blocks, name):
    chip_masks = (4, 2, 6)
    n = len(blocks)

    def body(*refs):
        x_refs, out_refs = refs[:n], refs[n:2 * n]
        send_sems, recv_sems, local_sems = refs[2 * n:]
        me = _coords()
        sibling = _flip(me, 1)

        def copy(a, k, owner, to, from_input=False):
            slot = out_refs[a].at[_index(owner)]
            return pltpu.make_async_remote_copy(
                src_ref=x_refs[a] if from_input else slot, dst_ref=slot,
                send_sem=send_sems.at[7 * a + k], recv_sem=recv_sems.at[7 * a + k],
                device_id=to, device_id_type=MESH)

        mine = [pltpu.make_async_copy(x_refs[a], out_refs[a].at[_index(me)], local_sems.at[a]) for a in range(n)]
        first = [copy(a, 0, me, sibling, True) for a in range(n)]
        first += [copy(a, 1 + j, me, _flip(me, m), True) for j, m in enumerate(chip_masks) for a in range(n)]
        for cp in mine + first:
            cp.start()
        passed = []
        for j, m in enumerate(chip_masks):
            for a in range(n):
                copy(a, 1 + j, _flip(me, m), me).wait_recv()
                passed.append(copy(a, 4 + j, _flip(me, m), sibling))
                passed[-1].start()
        for a in range(n):
            copy(a, 0, sibling, me).wait_recv()
            for j, m in enumerate(chip_masks):
                copy(a, 4 + j, _flip(sibling, m), me).wait_recv()
        for cp in first + passed:
            cp.wait_send()
        for cp in mine:
            cp.wait()

    return _comm_call(body, name, blocks, [jax.ShapeDtypeStruct((N_DEV,) + b.shape, b.dtype) for b in blocks])


def _exchange(segs, name, masks, slot_of, by_sender=False):
    n = len(segs)

    def body(*refs):
        seg_refs, out_refs = refs[:n], refs[n:2 * n]
        send_sems, recv_sems, local_sems = refs[2 * n:]
        me = _coords()

        def copy(a, k, sender):
            to = _flip(sender, masks[k])
            return pltpu.make_async_remote_copy(
                src_ref=seg_refs[a].at[slot_of(to)], dst_ref=out_refs[a].at[slot_of(sender) if by_sender else k],
                send_sem=send_sems.at[7 * a + k], recv_sem=recv_sems.at[7 * a + k],
                device_id=to, device_id_type=MESH)

        mine = [pltpu.make_async_copy(seg_refs[a].at[slot_of(me)], out_refs[a].at[slot_of(me)], local_sems.at[a])
                for a in range(n)] if by_sender else []
        sends = [copy(a, k, me) for k in range(len(masks)) for a in range(n)]
        for cp in mine + sends:
            cp.start()
        for k, mask in enumerate(masks):
            for a in range(n):
                copy(a, k, _flip(me, mask)).wait_recv()
        for cp in sends:
            cp.wait_send()
        for cp in mine:
            cp.wait()

    slots = lambda s: s.shape[0] if by_sender else len(masks)
    return _comm_call(body, name, segs, [jax.ShapeDtypeStruct((slots(s),) + s.shape[1:], s.dtype) for s in segs])


def _own_slot(tr, cols):
    return pl.BlockSpec((1, tr, cols), lambda i, slot: (slot[0], i, 0))


def _pair_sum(seg, slot, recv, name):
    _, rows, cols = seg.shape
    tr = max(t for t in range(16, rows + 1, 16) if rows % t == 0 and 2 * t * cols * 2 <= ADAM_TILE_BYTES)

    def body(slot_ref, own_ref, recv_ref, o_ref):
        o_ref[...] = (own_ref[0].astype(F32) + recv_ref[...].astype(F32)).astype(BF16)

    tile = pl.BlockSpec((tr, cols), lambda i, slot: (i, 0))
    return pl.pallas_call(
        body, name=name,
        grid_spec=pltpu.PrefetchScalarGridSpec(
            num_scalar_prefetch=1, grid=(rows // tr,), in_specs=[_own_slot(tr, cols), tile], out_specs=tile),
        out_shape=jax.ShapeDtypeStruct((rows, cols), BF16),
        compiler_params=_cparams(),
    )(slot, seg, recv)


def _adamw(seg, slot, recv, w, m, v, name):
    rows, cols = w.shape
    n_slots = recv.shape[0]
    sublanes = 8 * (4 // recv.dtype.itemsize)
    tr = max(t for t in range(sublanes, rows + 1, sublanes)
             if rows % t == 0 and N_DEV * t * cols * 4 <= ADAM_TILE_BYTES)
    c1 = 1.0 - ADAM_B1 ** ADAM_STEP
    c2 = 1.0 - ADAM_B2 ** ADAM_STEP

    def body(slot_ref, own_ref, r_ref, w_ref, m_ref, v_ref, g_out, d_out, m_out, v_out):
        g = own_ref[0].astype(F32)
        for s in range(n_slots):
            g = g + r_ref[s].astype(F32)
        m_new = ADAM_B1 * m_ref[...] + (1.0 - ADAM_B1) * g
        v_new = ADAM_B2 * v_ref[...] + (1.0 - ADAM_B2) * (g * g)
        m_hat = m_new / c1
        v_hat = v_new / c2
        g_out[...] = g
        d_out[...] = -ADAM_LR * (m_hat / (jnp.sqrt(v_hat) + ADAM_EPS) + ADAM_WD * w_ref[...])
        m_out[...] = m_new
        v_out[...] = v_new

    tile = pl.BlockSpec((tr, cols), lambda i, slot: (i, 0))
    return pl.pallas_call(
        body, name=name,
        grid_spec=pltpu.PrefetchScalarGridSpec(
            num_scalar_prefetch=1, grid=(rows // tr,),
            in_specs=[_own_slot(tr, cols), pl.BlockSpec((n_slots, tr, cols), lambda i, slot: (0, i, 0)),
                      tile, tile, tile],
            out_specs=[tile] * 4),
        out_shape=[jax.ShapeDtypeStruct(w.shape, F32)] * 4,
        compiler_params=_cparams(),
    )(slot, seg, recv, w, m, v)


def _mm(a, b, *, name, out_dtype, b_transposed=False):
    M, K = a.shape
    N = b.shape[0] if b_transposed else b.shape[1]
    tm = _pick(M, (640, 512, 128))
    tn = _pick(N, (2816, 1024, 640, 512, 256, 128))
    tk = K if K <= 3200 else _pick(K, (2816, 1024, 640, 512, 256, 128))
    nk = K // tk

    def body(*refs):
        a_ref, b_ref, o_ref = refs[:3]
        acc_ref = refs[-1] if nk > 1 else None
        if b_transposed:
            prod = _dot_nt(a_ref[...], b_ref[...])
        else:
            prod = jnp.dot(a_ref[...], b_ref[...], preferred_element_type=F32)
        if nk == 1:
            o_ref[...] = prod.astype(out_dtype)
        else:
            k = pl.program_id(2)

            @pl.when(k == 0)
            def _():
                acc_ref[...] = prod

            @pl.when(k > 0)
            def _():
                acc_ref[...] += prod

            @pl.when(k == nk - 1)
            def _():
                o_ref[...] = acc_ref[...].astype(out_dtype)

    b_spec = (pl.BlockSpec((tn, tk), lambda i, j, k: (j, k)) if b_transposed
              else pl.BlockSpec((tk, tn), lambda i, j, k: (k, j)))
    return pl.pallas_call(
        body, name=name, grid=(M // tm, N // tn, nk),
        in_specs=[pl.BlockSpec((tm, tk), lambda i, j, k: (i, k)), b_spec],
        out_specs=pl.BlockSpec((tm, tn), lambda i, j, k: (i, j)),
        out_shape=jax.ShapeDtypeStruct((M, N), out_dtype),
        scratch_shapes=[pltpu.VMEM((tm, tn), F32)] if nk > 1 else [],
        compiler_params=_cparams(),
    )(a, b)


def _mm_tn(a, b, *, name):
    T, M = a.shape
    N = b.shape[1]
    tm = _pick(M, (1408, 1024, 512, 256, 128))
    tn = _pick(N, (1024, 640, 512, 256, 128))
    tt = _pick(T, [t for t in (4160, 1664, 640) if 2 * 2 * t * (tm + tn) <= MM_TN_WINDOW_BYTES] + [128])

    def body(a_ref, b_ref, o_ref):
        prod = lax.dot_general(a_ref[...], b_ref[...], (((0,), (0,)), ((), ())), preferred_element_type=F32)

        @pl.when(pl.program_id(2) == 0)
        def _():
            o_ref[...] = prod

        @pl.when(pl.program_id(2) > 0)
        def _():
            o_ref[...] += prod

    return pl.pallas_call(
        body, name=name, grid=(M // tm, N // tn, T // tt),
        in_specs=[pl.BlockSpec((tt, tm), lambda i, j, k: (k, i)), pl.BlockSpec((tt, tn), lambda i, j, k: (k, j))],
        out_specs=pl.BlockSpec((tm, tn), lambda i, j, k: (i, j)),
        out_shape=jax.ShapeDtypeStruct((M, N), F32),
        compiler_params=_cparams(),
    )(a, b)


def _layer_norm(z, g, b):
    mu = jnp.mean(z, axis=-1, keepdims=True)
    xc = z - mu
    var = jnp.mean(xc * xc, axis=-1, keepdims=True)
    return xc * lax.rsqrt(var + LN_EPS) * g + b


def _mm_ln(a, w, res, g, b, *, name, target=None, n_tok=None, a_transposed=False):
    K, M = a.shape if a_transposed else a.shape[::-1]
    D = w.shape[1]
    tm = _pick(M, (640, 128))
    final = target is not None

    def body(*refs):
        if final:
            a_ref, w_ref, r_ref, g_ref, b_ref, t_ref, z_ref, dy_ref, loss_ref = refs
        else:
            a_ref, w_ref, r_ref, g_ref, b_ref, z_ref, h_ref, hb_ref = refs
        contract = (((0,), (0,)), ((), ())) if a_transposed else (((1,), (0,)), ((), ()))
        z = ALPHA * r_ref[...] + lax.dot_general(a_ref[...], w_ref[...], contract, preferred_element_type=F32)
        z_ref[...] = z
        h = _layer_norm(z, g_ref[...], b_ref[...])
        if not final:
            h_ref[...] = h
            hb_ref[...] = h.astype(BF16)
            return
        i = pl.program_id(0)
        row = i * tm + lax.broadcasted_iota(jnp.int32, (tm, 1), 0)
        valid = (row >= N_META) & (row < n_tok)
        err = jnp.where(valid, h - t_ref[...], 0.0)
        dy_ref[...] = err / D

        @pl.when(i == 0)
        def _():
            loss_ref[...] = jnp.zeros_like(loss_ref)

        loss_ref[...] += 0.5 * jnp.sum(jnp.sum(err * err, axis=1, keepdims=True) / D, axis=0, keepdims=True)

    row_blk = lambda cols: pl.BlockSpec((tm, cols), lambda i: (i, 0))
    vec = pl.BlockSpec((1, D), lambda i: (0, 0))
    a_blk = pl.BlockSpec((K, tm), lambda i: (0, i)) if a_transposed else row_blk(K)
    in_specs = [a_blk, pl.BlockSpec((K, D), lambda i: (0, 0)), row_blk(D), vec, vec]
    args = [a, w, res, g, b]
    if final:
        in_specs.append(row_blk(D))
        args.append(target)
        out_specs = [row_blk(D), row_blk(D), pl.BlockSpec((8, LANES), lambda i: (0, 0))]
        out_shape = [jax.ShapeDtypeStruct((M, D), F32)] * 2 + [jax.ShapeDtypeStruct((8, LANES), F32)]
    else:
        out_specs = [row_blk(D)] * 3
        out_shape = [jax.ShapeDtypeStruct((M, D), F32)] * 2 + [jax.ShapeDtypeStruct((M, D), BF16)]
    return pl.pallas_call(
        body, name=name, grid=(M // tm,), in_specs=in_specs, out_specs=out_specs, out_shape=out_shape,
        compiler_params=_cparams(),
    )(*args)


def _ln_bwd(dh, z, g, *, name, dz_next=None):
    M, D = z.shape
    tm = _pick(M, (640, 128))
    has_next = dz_next is not None

    def body(*refs):
        if has_next:
            dh_ref, nx_ref, z_ref, g_ref, dz_ref, dzb_ref, dg_ref, db_ref = refs
            dh_v = dh_ref[...] + ALPHA * nx_ref[...]
        else:
            dh_ref, z_ref, g_ref, dz_ref, dzb_ref, dg_ref, db_ref = refs
            dh_v = dh_ref[...]
        z_v = z_ref[...]
        mu = jnp.mean(z_v, axis=-1, keepdims=True)
        xc = z_v - mu
        rstd = lax.rsqrt(jnp.mean(xc * xc, axis=-1, keepdims=True) + LN_EPS)
        xhat = xc * rstd
        dxhat = dh_v * g_ref[...]
        dz = rstd * (dxhat - jnp.mean(dxhat, axis=-1, keepdims=True)
                     - xhat * jnp.mean(dxhat * xhat, axis=-1, keepdims=True))
        dz_ref[...] = dz
        dzb_ref[...] = dz.astype(BF16)

        @pl.when(pl.program_id(0) == 0)
        def _():
            dg_ref[...] = jnp.zeros_like(dg_ref)
            db_ref[...] = jnp.zeros_like(db_ref)

        dg_ref[...] += jnp.sum(dh_v * xhat, axis=0, keepdims=True)
        db_ref[...] += jnp.sum(dh_v, axis=0, keepdims=True)

    row_blk = pl.BlockSpec((tm, D), lambda i: (i, 0))
    vec = pl.BlockSpec((1, D), lambda i: (0, 0))
    args = [dh] + ([dz_next] if has_next else []) + [z, g]
    in_specs = [row_blk] * (len(args) - 1) + [vec]
    return pl.pallas_call(
        body, name=name, grid=(M // tm,), in_specs=in_specs,
        out_specs=[row_blk, row_blk, vec, vec],
        out_shape=[jax.ShapeDtypeStruct((M, D), F32), jax.ShapeDtypeStruct((M, D), BF16),
                   jax.ShapeDtypeStruct((1, D), F32), jax.ShapeDtypeStruct((1, D), F32)],
        compiler_params=_cparams(),
    )(*args)


def _pool_fwd(h0, pw, scale, g, b, *, name):
    M, D = h0.shape
    n_groups, G, _ = pw.shape
    tm = _pick(M, (640, 128))

    def body(x_ref, halo_ref, pw_ref, sc_ref, g_ref, b_ref, z_ref, h_ref, hb_ref, diff_ref, ext_ref, mix_ref):
        i = pl.program_id(0)
        x = x_ref[...]
        ext_ref[0:POOL_HALO, :] = jnp.where(i == 0, 0.0, halo_ref[...])
        ext_ref[POOL_HALO:, :] = x
        tok = i * tm + lax.broadcasted_iota(jnp.int32, (tm, 1), 0)
        for gi, win in enumerate(POOL_WINDOWS):
            cols = slice(gi * G, (gi + 1) * G)
            xs = x[:, cols]
            s = xs
            for k in range(1, win):
                s = s + ext_ref[pl.ds(POOL_HALO - k, tm), cols]
            count = jnp.minimum(tok + 1, win).astype(F32)
            d = (s / count - xs).astype(BF16)
            diff_ref[:, cols] = d
            mix_ref[:, cols] = jnp.dot(d, pw_ref[gi], preferred_element_type=F32)
        z = ALPHA * x + mix_ref[...] * sc_ref[...]
        z_ref[...] = z
        h = _layer_norm(z, g_ref[...], b_ref[...])
        h_ref[...] = h
        hb_ref[...] = h.astype(BF16)

    row_blk = pl.BlockSpec((tm, D), lambda i: (i, 0))
    vec = pl.BlockSpec((1, D), lambda i: (0, 0))
    halo = pl.BlockSpec((POOL_HALO, D), lambda i: (jnp.maximum(i * (tm // POOL_HALO) - 1, 0), 0))
    return pl.pallas_call(
        body, name=name, grid=(M // tm,),
        in_specs=[row_blk, halo, pl.BlockSpec((n_groups, G, G), lambda i: (0, 0, 0)), vec, vec, vec],
        out_specs=[row_blk] * 4,
        out_shape=[jax.ShapeDtypeStruct((M, D), F32)] * 2 + [jax.ShapeDtypeStruct((M, D), BF16)] * 2,
        scratch_shapes=[pltpu.VMEM((tm + POOL_HALO, D), F32), pltpu.VMEM((tm, D), F32)],
        compiler_params=_cparams(),
    )(h0, h0, pw, scale, g, b)


def _pool_bwd(dz, diff, pw, scale, *, name):
    M, D = dz.shape
    n_groups, G, _ = pw.shape
    tm = _pick(M, (640, 128))
    nt = M // tm

    def body(dz_ref, halo_ref, diff_ref, pw_ref, sc_ref, dh_ref, dpw_ref, dsc_ref, ext_ref, q_ref):
        i = pl.program_id(0)
        dz_v = dz_ref[...]
        ext_ref[0:tm, :] = dz_v
        ext_ref[tm:, :] = jnp.where(i == nt - 1, 0.0, halo_ref[...])
        tok = i * tm + lax.broadcasted_iota(jnp.int32, (tm + POOL_HALO, 1), 0)

        @pl.when(i == 0)
        def _():
            dpw_ref[...] = jnp.zeros_like(dpw_ref)
            dsc_ref[...] = jnp.zeros_like(dsc_ref)

        for gi, win in enumerate(POOL_WINDOWS):
            cols = slice(gi * G, (gi + 1) * G)
            dmix = (ext_ref[:, cols] * sc_ref[:, cols]).astype(BF16)
            ddiff = _dot_nt(dmix, pw_ref[gi])
            count = jnp.minimum(tok + 1, win).astype(F32)
            q_ref[:, cols] = ddiff / count
            acc = -ddiff[0:tm]
            for k in range(win):
                acc = acc + q_ref[pl.ds(k, tm), cols]
            dh_ref[:, cols] = ALPHA * dz_v[:, cols] + acc
            d = diff_ref[:, cols]
            dpw_ref[gi] += lax.dot_general(d, dmix[0:tm], (((0,), (0,)), ((), ())), preferred_element_type=F32)
            mixed = jnp.dot(d, pw_ref[gi], preferred_element_type=F32)
            dsc_ref[:, cols] += jnp.sum(dz_v[:, cols] * mixed, axis=0, keepdims=True)

    row_blk = pl.BlockSpec((tm, D), lambda i: (i, 0))
    vec = pl.BlockSpec((1, D), lambda i: (0, 0))
    per_tile = tm // POOL_HALO
    halo = pl.BlockSpec((POOL_HALO, D), lambda i: (jnp.minimum((i + 1) * per_tile, nt * per_tile - 1), 0))
    wblk = pl.BlockSpec((n_groups, G, G), lambda i: (0, 0, 0))
    return pl.pallas_call(
        body, name=name, grid=(nt,),
        in_specs=[row_blk, halo, row_blk, wblk, vec],
        out_specs=[row_blk, wblk, vec],
        out_shape=[jax.ShapeDtypeStruct((M, D), F32), jax.ShapeDtypeStruct((n_groups, G, G), F32),
                   jax.ShapeDtypeStruct((1, D), F32)],
        scratch_shapes=[pltpu.VMEM((tm + POOL_HALO, D), F32), pltpu.VMEM((tm + POOL_HALO, D), F32)],
        compiler_params=_cparams(),
    )(dz, dz, diff, pw, scale)


def _glu_interleave(x):
    s = x.shape
    n = s[-1] // (2 * GLU_CHUNK)
    return jnp.swapaxes(x.reshape(s[:-1] + (2, n, GLU_CHUNK)), -3, -2).reshape(s)


def _glu_deinterleave(x):
    s = x.shape
    n = s[-1] // (2 * GLU_CHUNK)
    return jnp.swapaxes(x.reshape(s[:-1] + (n, 2, GLU_CHUNK)), -3, -2).reshape(s)


W_PACK_ROWS = 256
W_PACK_PIECE = 64


def _interleaved_col(col, F):
    half, within = divmod(col, F)
    chunk, off = divmod(within, GLU_CHUNK)
    return chunk * 2 * GLU_CHUNK + half * GLU_CHUNK + off


def _w_in_grad_segments(dw, *, name):
    D, F2 = dw.shape
    width = F2 // N_DEV
    rows = _pick(D, (W_PACK_ROWS, 128))
    assert width % W_PACK_PIECE == 0

    def body(x_ref, o_ref):
        for j in range(N_DEV):
            for q in range(0, width, W_PACK_PIECE):
                at = _interleaved_col(width * j + q, F2 // 2)
                o_ref[j % 2, j // 2, :, q:q + W_PACK_PIECE] = x_ref[:, at:at + W_PACK_PIECE].astype(BF16)

    return pl.pallas_call(
        body, name=name, grid=(D // rows,),
        in_specs=[pl.BlockSpec((rows, F2), lambda r: (r, 0))],
        out_specs=pl.BlockSpec((2, N_CHIPS, rows, width), lambda r: (0, 0, r, 0)),
        out_shape=jax.ShapeDtypeStruct((2, N_CHIPS, D, width), BF16),
        compiler_params=_cparams(),
    )(dw)


def _w_in_interleaved(gathered, *, name):
    n_dev, depth, D, width = gathered.shape
    F = n_dev * width // 2
    rows = _pick(D, (W_PACK_ROWS, 128))
    assert width % W_PACK_PIECE == 0 and GLU_CHUNK % W_PACK_PIECE == 0 and F % GLU_CHUNK == 0
    interleaved = lambda col: _interleaved_col(col, F)

    def body(x_ref, o_ref):
        for j in range(n_dev):
            for q in range(0, width, W_PACK_PIECE):
                to = interleaved(width * j + q)
                o_ref[0, :, to:to + W_PACK_PIECE] = x_ref[j, 0, :, q:q + W_PACK_PIECE]

    return pl.pallas_call(
        body, name=name, grid=(depth, D // rows),
        in_specs=[pl.BlockSpec((n_dev, 1, rows, width), lambda l, r: (0, l, r, 0))],
        out_specs=pl.BlockSpec((1, rows, n_dev * width), lambda l, r: (l, r, 0)),
        out_shape=jax.ShapeDtypeStruct((depth, D, n_dev * width), gathered.dtype),
        compiler_params=_cparams(),
    )(gathered)


def _taps(u_ref, head_ref, r0):
    src, base = (head_ref, CONV_HALO) if r0 == 0 else (u_ref, r0)
    return tuple(src[pl.ds(base - k, CONV_STRIP), :] for k in range(3))


def _conv_glu_fwd(u, cw, cb, *, name):
    M, F2 = u.shape
    tm = _pick(M, (640, 128))
    tc = 2 * GLU_CHUNK

    def body(u_ref, halo_ref, w_ref, b_ref, o_ref, c_ref, head_ref):
        i = pl.program_id(0)
        head_ref[0:CONV_HALO, :] = jnp.where(i == 0, 0.0, halo_ref[...])
        head_ref[CONV_HALO:, :] = u_ref[0:CONV_STRIP, :]
        w0, w1, w2, b = w_ref[0:1, :], w_ref[1:2, :], w_ref[2:3, :], b_ref[...]
        for r0 in range(0, tm, CONV_STRIP):
            u0, u1, u2 = _taps(u_ref, head_ref, r0)
            c = b + w0 * u2 + w1 * u1 + w2 * u0
            c_ref[pl.ds(r0, CONV_STRIP), :] = c.astype(BF16)
            a, g = c[:, :GLU_CHUNK], c[:, GLU_CHUNK:]
            o_ref[pl.ds(r0, CONV_STRIP), :] = (a * jax.nn.sigmoid(a) * g).astype(BF16)

    per_tile = tm // CONV_HALO
    return pl.pallas_call(
        body, name=name, grid=(M // tm, F2 // tc),
        in_specs=[pl.BlockSpec((tm, tc), lambda i, j: (i, j)),
                  pl.BlockSpec((CONV_HALO, tc), lambda i, j: (jnp.maximum(i * per_tile - 1, 0), j)),
                  pl.BlockSpec((3, tc), lambda i, j: (0, j)), pl.BlockSpec((1, tc), lambda i, j: (0, j))],
        out_specs=[pl.BlockSpec((tm, GLU_CHUNK), lambda i, j: (i, j)), pl.BlockSpec((tm, tc), lambda i, j: (i, j))],
        out_shape=[jax.ShapeDtypeStruct((M, F2 // 2), BF16), jax.ShapeDtypeStruct((M, F2), BF16)],
        scratch_shapes=[pltpu.VMEM((CONV_HALO + CONV_STRIP, tc), F32)],
        compiler_params=_cparams(),
    )(u, u, cw, cb)


def _conv_glu_bwd(u, c, dact, cw, *, name):
    M, F2 = u.shape
    tm = _pick(M, (640, 128))
    nt = M // tm
    tc = 2 * GLU_CHUNK

    def body(u_ref, c_ref, da_ref, w_ref, du_ref, dw_ref, db_ref, dcx_ref, carry_ref):
        i = pl.program_id(1)
        w0, w1, w2 = w_ref[0:1, :], w_ref[1:2, :], w_ref[2:3, :]

        @pl.when(i == 0)
        def _():
            dw_ref[...] = jnp.zeros_like(dw_ref)
            db_ref[...] = jnp.zeros_like(db_ref)
            carry_ref[...] = jnp.zeros_like(carry_ref)

        def fold(t):
            return sum(t[r:r + 8] for r in range(0, CONV_STRIP, 8))

        dcx_ref[tm:, :] = carry_ref[...]
        s_b = s_0 = s_1 = s_2 = jnp.zeros((8, tc), F32)
        for r0 in reversed(range(0, tm, CONV_STRIP)):
            rows = pl.ds(r0, CONV_STRIP)
            c_v = c_ref[rows, :].astype(F32)
            a, g = c_v[:, :GLU_CHUNK], c_v[:, GLU_CHUNK:]
            sig = jax.nn.sigmoid(a)
            dact_v = da_ref[rows, :].astype(F32)
            d_a = dact_v * g * (sig * (1.0 + a * (1.0 - sig)))
            d_g = dact_v * (a * sig)
            dc = jnp.concatenate([d_a, d_g], axis=1)
            dcx_ref[rows, :] = dc
            dc1, dc2 = dcx_ref[pl.ds(r0 + 1, CONV_STRIP), :], dcx_ref[pl.ds(r0 + 2, CONV_STRIP), :]
            du_ref[rows, :] = (w2 * dc + w1 * dc1 + w0 * dc2).astype(BF16)
            u_v = u_ref[rows, :]
            s_b, s_0, s_1, s_2 = s_b + fold(dc), s_0 + fold(dc2 * u_v), s_1 + fold(dc1 * u_v), s_2 + fold(dc * u_v)
        carry_ref[...] = dcx_ref[0:CONV_HALO, :]
        db_ref[...] += jnp.sum(s_b, axis=0, keepdims=True)
        dw_ref[0:1, :] += jnp.sum(s_0, axis=0, keepdims=True)
        dw_ref[1:2, :] += jnp.sum(s_1, axis=0, keepdims=True)
        dw_ref[2:3, :] += jnp.sum(s_2, axis=0, keepdims=True)

    per_tile = tm // CONV_HALO
    rev = lambda i: nt - 1 - i
    return pl.pallas_call(
        body, name=name, grid=(F2 // tc, nt),
        in_specs=[pl.BlockSpec((tm, tc), lambda j, i: (rev(i), j)), pl.BlockSpec((tm, tc), lambda j, i: (rev(i), j)),
                  pl.BlockSpec((tm, GLU_CHUNK), lambda j, i: (rev(i), j)),
                  pl.BlockSpec((3, tc), lambda j, i: (0, j))],
        out_specs=[pl.BlockSpec((tm, tc), lambda j, i: (rev(i), j)),
                   pl.BlockSpec((3, tc), lambda j, i: (0, j)), pl.BlockSpec((1, tc), lambda j, i: (0, j))],
        out_shape=[jax.ShapeDtypeStruct((M, F2), BF16), jax.ShapeDtypeStruct((3, F2), F32),
                   jax.ShapeDtypeStruct((1, F2), F32)],
        scratch_shapes=[pltpu.VMEM((tm + CONV_HALO, tc), F32), pltpu.VMEM((CONV_HALO, tc), F32)],
        compiler_params=_cparams(),
    )(u, c, dact, cw)


def _split3(x):
    hi = x.astype(BF16)
    r = x - hi.astype(F32)
    mid = r.astype(BF16)
    lo = (r - mid.astype(F32)).astype(BF16)
    return hi, mid, lo


def _tri_sum(tri, x):
    return sum(jnp.dot(tri, part, preferred_element_type=F32) for part in _split3(x))


def _log_sigmoid(x):
    return jnp.minimum(x, 0.0) - jnp.log1p(jnp.exp(-jnp.abs(x)))


def _forget_cumsum(pre, bias, *, name):
    M, C = pre.shape
    tm = _pick(M, (640, 128))

    def body(p_ref, b_ref, c_ref, carry_ref):
        i = pl.program_id(0)

        @pl.when(i == 0)
        def _():
            carry_ref[...] = jnp.zeros_like(carry_ref)

        logf = _log_sigmoid(p_ref[...] + b_ref[...])
        r = lax.broadcasted_iota(jnp.int32, (tm, tm), 0)
        s = lax.broadcasted_iota(jnp.int32, (tm, tm), 1)
        c_ref[...] = _tri_sum((s <= r).astype(BF16), logf) + carry_ref[...]
        carry_ref[...] = c_ref[pl.ds(tm - 1, 1), :]

    return pl.pallas_call(
        body, name=name, grid=(M // tm,),
        in_specs=[pl.BlockSpec((tm, C), lambda i: (i, 0)), pl.BlockSpec((1, C), lambda i: (0, 0))],
        out_specs=pl.BlockSpec((tm, C), lambda i: (i, 0)),
        out_shape=jax.ShapeDtypeStruct((M, C), F32),
        scratch_shapes=[pltpu.VMEM((1, C), F32)],
        compiler_params=_cparams(),
    )(pre, bias)


def _forget_cumsum_bwd(dc, pre, bias, *, name):
    M, C = pre.shape
    tm = _pick(M, (640, 128))
    nt = M // tm

    def body(dc_ref, p_ref, b_ref, dp_ref, db_ref, carry_ref, run_ref):
        i = pl.program_id(0)

        @pl.when(i == 0)
        def _():
            carry_ref[...] = jnp.zeros_like(carry_ref)
            db_ref[...] = jnp.zeros_like(db_ref)

        r = lax.broadcasted_iota(jnp.int32, (tm, tm), 0)
        s = lax.broadcasted_iota(jnp.int32, (tm, tm), 1)
        run_ref[...] = _tri_sum((s >= r).astype(BF16), dc_ref[...]) + carry_ref[...]
        carry_ref[...] = run_ref[pl.ds(0, 1), :]
        dpre = run_ref[...] * jax.nn.sigmoid(-(p_ref[...] + b_ref[...]))
        dp_ref[...] = dpre.astype(BF16)
        db_ref[...] += jnp.sum(dpre, axis=0, keepdims=True)

    rev_blk = pl.BlockSpec((tm, C), lambda i: (nt - 1 - i, 0))
    vec = pl.BlockSpec((1, C), lambda i: (0, 0))
    return pl.pallas_call(
        body, name=name, grid=(nt,),
        in_specs=[rev_blk, rev_blk, vec], out_specs=[rev_blk, vec],
        out_shape=[jax.ShapeDtypeStruct((M, C), BF16), jax.ShapeDtypeStruct((1, C), F32)],
        scratch_shapes=[pltpu.VMEM((1, C), F32), pltpu.VMEM((tm, C), F32)],
        compiler_params=_cparams(),
    )(dc, pre, bias)


def _causal_mask(tm):
    key = lax.broadcasted_iota(jnp.int32, (tm, tm), 0)
    query = lax.broadcasted_iota(jnp.int32, (tm, tm), 1)
    return key <= query


def _dot_nt(a, b):
    return lax.dot_general(a, b, (((1,), (1,)), ((), ())), preferred_element_type=F32)


def _loop_unrolled(n, step, init, unroll):
    def trip(p, carry):
        for r in range(unroll):
            carry = step(unroll * p + r, carry)
        return carry
    carry = lax.fori_loop(0, n // unroll, trip, init)
    return lax.fori_loop(unroll * (n // unroll), n, step, carry)


def _rows(main, extras, total):
    tm = main.shape[1]
    used = sum(e.shape[0] for e in extras)
    tile = jnp.concatenate([e.astype(BF16) for e in extras] + [jnp.zeros((ATTN_EXTRA - used, tm), BF16)], axis=0)
    rest = total - main.shape[0] - ATTN_EXTRA
    return jnp.concatenate([main, tile] + ([jnp.zeros((rest, tm), BF16)] if rest else []), axis=0)


def _attn_specs(H, M, dh, tm):
    nt = M // tm
    qT_blk = pl.BlockSpec((1, dh, tm), lambda h, i: (h, 0, i))
    row_blk = pl.BlockSpec((1, 1, tm), lambda h, i: (h, 0, i))
    key_blk = pl.BlockSpec((1, M, LANES), lambda h, i: (h, 0, 0))
    keyT_blk = pl.BlockSpec((1, nt, dh + ATTN_EXTRA, tm), lambda h, i: (h, 0, 0, 0))
    return qT_blk, row_blk, key_blk, keyT_blk


def _pair_specs(M, dh, tm):
    assert 2 * dh == LANES
    cols = lambda section: pl.BlockSpec((tm, LANES), lambda p, i: (i, section + p))
    headsT = pl.BlockSpec((2, dh, tm), lambda p, i: (p, 0, i))
    return cols, headsT


def _attn_pack(qkv, c, H, *, name):
    M, D3 = qkv.shape
    D = D3 // 3
    dh = D // H
    tm = _pick(M, (640, 128))
    nt = M // tm
    cols, headsT = _pair_specs(M, dh, tm)
    key_blk = pl.BlockSpec((2, tm, LANES), lambda p, i: (p, i, 0))
    keyT_blk = pl.BlockSpec((2, 1, dh + ATTN_EXTRA, tm), lambda p, i: (p, i, 0, 0))

    def body(k_ref, v_ref, q_ref, c_ref, ka_ref, va_ref, qT_ref, kT_ref, vT_ref):
        p = pl.program_id(0)
        lane = lax.broadcasted_iota(jnp.int32, (tm, LANES), 1)
        k_v, v_v = k_ref[...], v_ref[...]
        ones_v = jnp.where((lane >= dh) & (lane < dh + 3), 1.0, 0.0)
        for e in range(2):
            ck = jnp.sum(jnp.where(lane == 2 * p + e, c_ref[...], 0.0), axis=1, keepdims=True)
            hi, mid, lo = _split3(-ck)
            extra = jnp.where(lane == dh + 3, hi.astype(F32), jnp.where(lane == dh + 4, mid.astype(F32),
                              jnp.where(lane == dh + 5, lo.astype(F32), ones_v)))
            first = lambda t: t if e == 0 else pltpu.roll(t, dh, 1)
            ka_ref[e] = jnp.where(lane < dh, first(k_v), extra.astype(BF16))
            va_ref[e] = jnp.where(lane < dh, first(v_v), ones_v.astype(BF16))
        row = lax.broadcasted_iota(jnp.int32, (ATTN_EXTRA, tm), 0)
        tail = jnp.where(row == 0, 1.0, 0.0).astype(BF16)
        for src, dst in ((k_v, kT_ref), (v_v, vT_ref)):
            t = src.T
            for e in range(2):
                dst[e, 0, 0:dh, :] = t[e * dh:(e + 1) * dh]
                dst[e, 0, dh:, :] = tail
        qT_ref[...] = q_ref[...].T.reshape(2, dh, tm)

    n_sec = D // LANES
    return pl.pallas_call(
        body, name=name, grid=(H // 2, nt),
        in_specs=[cols(0), cols(n_sec), cols(2 * n_sec), pl.BlockSpec((tm, LANES), lambda p, i: (i, 0))],
        out_specs=[key_blk, key_blk, headsT, keyT_blk, keyT_blk],
        out_shape=[jax.ShapeDtypeStruct((H, M, LANES), BF16)] * 2 + [jax.ShapeDtypeStruct((H, dh, M), BF16)]
        + [jax.ShapeDtypeStruct((H, nt, dh + ATTN_EXTRA, tm), BF16)] * 2,
        compiler_params=_cparams(),
    )(qkv, qkv, qkv, c)


def _heads_merge(parts, M, tm, *, name):
    H = parts[0].shape[0]
    dh = LANES // 2
    blocked = [t.ndim == 4 for t in parts]
    cols, headsT = _pair_specs(M, dh, tm)

    def body(*refs):
        for src, dst, b in zip(refs[:len(parts)], refs[len(parts):], blocked):
            t = jnp.concatenate([src[e, 0, 0:dh, :] for e in range(2)], axis=0) if b else src[...].reshape(2 * dh, tm)
            dst[...] = t.astype(BF16).T

    in_specs = [pl.BlockSpec((2, 1, t.shape[2], tm), lambda p, i: (p, i, 0, 0)) if b else headsT
                for t, b in zip(parts, blocked)]
    return pl.pallas_call(
        body, name=name, grid=(H // 2, M // tm), in_specs=in_specs, out_specs=[cols(0)] * len(parts),
        out_shape=[jax.ShapeDtypeStruct((M, H * dh), BF16)] * len(parts), compiler_params=_cparams(),
    )(*parts)


def _attn_fwd(qT, cq, k_aug, vT_aug, *, name):
    H, dh, M = qT.shape
    tm = vT_aug.shape[-1]
    qT_blk, row_blk, key_blk, keyT_blk = _attn_specs(H, M, dh, tm)

    def body(qT_ref, cq_ref, k_ref, vT_ref, oT_ref, lse_ref):
        i = pl.program_id(1)
        ones = jnp.ones((3, tm), BF16)
        qa = _rows(qT_ref[0] * jnp.asarray(dh ** -0.5, BF16), [*_split3(cq_ref[0]), ones], LANES)

        def block(j, carry, masked):
            m, acc = carry
            keys = pl.ds(pl.multiple_of(j * tm, tm), tm)
            sT = jnp.dot(k_ref[0, keys, :], qa, preferred_element_type=F32)
            if masked:
                sT = jnp.where(_causal_mask(tm), sT, NEG_INF)
            m_new = jnp.maximum(m, jnp.max(sT, axis=0, keepdims=True))
            pT = jnp.exp(sT - m_new).astype(BF16)
            acc = jnp.exp(m - m_new) * acc + jnp.dot(vT_ref[0, j], pT, preferred_element_type=F32)
            return m_new, acc

        init = (jnp.full((1, tm), NEG_INF, F32), jnp.zeros((dh + ATTN_EXTRA, tm), F32))
        m, acc = block(i, _loop_unrolled(i, lambda j, c: block(j, c, False), init, 4), True)
        l = acc[dh:dh + 1, :]
        oT_ref[0] = (acc[0:dh, :] / l).astype(BF16)
        lse_ref[0] = m + jnp.log(l)

    return pl.pallas_call(
        body, name=name, grid=(H, M // tm),
        in_specs=[qT_blk, row_blk, key_blk, keyT_blk], out_specs=[qT_blk, row_blk],
        out_shape=[jax.ShapeDtypeStruct((H, dh, M), BF16), jax.ShapeDtypeStruct((H, 1, M), F32)],
        compiler_params=_cparams(),
    )(qT, cq, k_aug, vT_aug)


def _attn_bwd(qT, oT, doT, lse, cq, k_aug, v_aug, kT_aug, *, name):
    H, dh, M = qT.shape
    tm = kT_aug.shape[-1]
    qT_blk, row_blk, key_blk, keyT_blk = _attn_specs(H, M, dh, tm)
    dvT_blk = pl.BlockSpec((1, M // tm, dh, tm), lambda h, i: (h, 0, 0, 0))

    def body(qT_ref, oT_ref, doT_ref, lse_ref, cq_ref, k_ref, v_ref, kT_ref, dqT_ref, dcq_ref, dkT_ref, dvT_ref):
        i = pl.program_id(1)

        @pl.when(i == 0)
        def _():
            dkT_ref[...] = jnp.zeros_like(dkT_ref)
            dvT_ref[...] = jnp.zeros_like(dvT_ref)

        qsT = qT_ref[0] * jnp.asarray(dh ** -0.5, BF16)
        doT = doT_ref[0]
        delta = jnp.sum(doT.astype(F32) * oT_ref[0].astype(F32), axis=0, keepdims=True)
        ones = jnp.ones((3, tm), BF16)
        qa = _rows(qsT, [*_split3(cq_ref[0] - lse_ref[0]), ones], LANES)
        da = _rows(doT, _split3(-delta), LANES)
        q1 = _rows(qsT, [ones[0:1]], dh + ATTN_EXTRA)

        def block(j, dq, masked):
            keys = pl.ds(pl.multiple_of(j * tm, tm), tm)
            pT = jnp.exp(jnp.dot(k_ref[0, keys, :], qa, preferred_element_type=F32))
            if masked:
                pT = jnp.where(_causal_mask(tm), pT, 0.0)
            ds_b = (pT * jnp.dot(v_ref[0, keys, :], da, preferred_element_type=F32)).astype(BF16)
            dvT_ref[0, j] += _dot_nt(doT, pT.astype(BF16))
            dkT_ref[0, j] += _dot_nt(q1, ds_b)
            return dq + jnp.dot(kT_ref[0, j], ds_b, preferred_element_type=F32)

        init = jnp.zeros((dh + ATTN_EXTRA, tm), F32)
        dq = block(i, _loop_unrolled(i, lambda j, c: block(j, c, False), init, 4), True)
        dqT_ref[0] = (dq[0:dh, :] * dh ** -0.5).astype(BF16)
        dcq_ref[0] = dq[dh:dh + 1, :]

    return pl.pallas_call(
        body, name=name, grid=(H, M // tm),
        in_specs=[qT_blk, qT_blk, qT_blk, row_blk, row_blk, key_blk, key_blk, keyT_blk],
        out_specs=[qT_blk, row_blk, keyT_blk, dvT_blk],
        out_shape=[jax.ShapeDtypeStruct((H, dh, M), BF16), jax.ShapeDtypeStruct((H, 1, M), F32),
                   jax.ShapeDtypeStruct(kT_aug.shape, F32), jax.ShapeDtypeStruct((H, M // tm, dh, tm), F32)],
        compiler_params=_cparams(),
    )(qT, oT, doT, lse, cq, k_aug, v_aug, kT_aug)


def kernel(x, meta, pool_w, pool_scale, w_kv, w_f, b_f, w_q, w_o, ffn_w_in, ffn_conv_w, ffn_conv_b, ffn_w_out, ln_g, ln_b, loss_target, m_meta, m_pool_w, m_pool_scale, m_w_kv, m_w_f, m_b_f, m_w_q, m_w_o, m_ffn_w_in, m_ffn_conv_w, m_ffn_conv_b, m_ffn_w_out, m_ln_g, m_ln_b, v_meta, v_pool_w, v_pool_scale, v_w_kv, v_w_f, v_b_f, v_w_q, v_w_o, v_ffn_w_in, v_ffn_conv_w, v_ffn_conv_b, v_ffn_w_out, v_ln_g, v_ln_b):
    local = dict(meta=meta, pool_w=pool_w, pool_scale=pool_scale, w_kv=w_kv, w_f=w_f, b_f=b_f, w_q=w_q, w_o=w_o,
                 ffn_w_in=ffn_w_in, ffn_conv_w=ffn_conv_w, ffn_conv_b=ffn_conv_b, ffn_w_out=ffn_w_out,
                 ln_g=ln_g, ln_b=ln_b)
    mom1 = dict(meta=m_meta, pool_w=m_pool_w, pool_scale=m_pool_scale, w_kv=m_w_kv, w_f=m_w_f, b_f=m_b_f,
                w_q=m_w_q, w_o=m_w_o, ffn_w_in=m_ffn_w_in, ffn_conv_w=m_ffn_conv_w, ffn_conv_b=m_ffn_conv_b,
                ffn_w_out=m_ffn_w_out, ln_g=m_ln_g, ln_b=m_ln_b)
    mom2 = dict(meta=v_meta, pool_w=v_pool_w, pool_scale=v_pool_scale, w_kv=v_w_kv, w_f=v_w_f, b_f=v_b_f,
                w_q=v_w_q, w_o=v_w_o, ffn_w_in=v_ffn_w_in, ffn_conv_w=v_ffn_conv_w, ffn_conv_b=v_ffn_conv_b,
                ffn_w_out=v_ffn_w_out, ln_g=v_ln_g, ln_b=v_ln_b)
    axis_of = dict(PARAMS)

    S, D = x.shape[1], x.shape[2]
    H = b_f.shape[0]
    dh = D // H
    n_tok = N_META + S
    M = _round_up(n_tok, LANES)
    tm = _pick(M, (640, 128))
    nt = M // tm
    F2 = ffn_conv_b.shape[1]
    F = F2 // 2
    depth = ffn_conv_b.shape[0]

    small_sharded = [n for n in SMALL if axis_of[n] is not None]
    got = _all_gather([local[n].astype(BF16) for n in BIG] + [_pack([local[n] for n in small_sharded])],
                      "gather_weights")
    wb = {n: _from_blocks(blk, axis_of[n]) for n, blk in zip(BIG, got) if n != "ffn_w_in"}
    small_blocks = _unpack(got[-1], [local[n].shape for n in small_sharded], lead=(N_DEV,))
    wf32 = {n: _from_blocks(blk, axis_of[n]) for n, blk in zip(small_sharded, small_blocks)}

    pw = wb["pool_w"][0]
    wf_pad = jnp.pad(wf32["w_f"].astype(BF16), ((0, 0), (0, LANES - H)))
    w_qkv = jnp.concatenate([wb["w_kv"], wb["w_q"][0]], axis=1)
    w_att = jnp.concatenate([w_qkv, wf_pad], axis=1)
    wo = wb["w_o"][0]
    w_in = _w_in_interleaved(got[BIG.index("ffn_w_in")], name="w_in_interleave")
    w_out = wb["ffn_w_out"]
    conv_w = _glu_interleave(wf32["ffn_conv_w"])
    conv_b = _glu_interleave(ffn_conv_b)[:, None, :]
    scale = wf32["pool_scale"]
    g_ln, b_ln = wf32["ln_g"], wf32["ln_b"]
    ln = lambda i, j: (g_ln[i, j][None, :], b_ln[i, j][None, :])
    bias_f = jnp.pad(b_f, (0, LANES - H))[None, :]

    pad_rows = M - n_tok
    h0 = jnp.concatenate([wf32["meta"], x[0], jnp.zeros((pad_rows, D), F32)], axis=0)
    target = jnp.concatenate([jnp.zeros((N_META, D), F32), loss_target[0], jnp.zeros((pad_rows, D), F32)], axis=0)

    z1, h1, h1b, diff = _pool_fwd(h0, pw, scale, *ln(0, 0), name="pool_fwd")
    u0 = _mm(h1b, w_in[0], name="ffn0_up", out_dtype=F32)
    act0, c0 = _conv_glu_fwd(u0, conv_w[0], conv_b[0], name="ffn0_conv")
    z2, h2, h2b = _mm_ln(act0, w_out[0], h1, *ln(0, 1), name="ffn0_down_ln")

    qkv = _mm(h2b, w_qkv, name="attn_qkv", out_dtype=BF16)
    pre = _mm(h2b, wf_pad, name="attn_gate", out_dtype=F32)
    c = _forget_cumsum(pre, bias_f, name="attn_cumsum")
    k_aug, v_aug, qT_h, kT_aug, vT_aug = _attn_pack(qkv, c, H, name="attn_pack")
    cq = c[:, :H].T[:, None, :]
    oT_h, lse = _attn_fwd(qT_h, cq, k_aug, vT_aug, name="attn_fwd")
    oT = oT_h.reshape(D, M)
    z3, h3, h3b = _mm_ln(oT, wo, h2, *ln(1, 0), name="attn_out_ln", a_transposed=True)

    u1 = _mm(h3b, w_in[1], name="ffn1_up", out_dtype=F32)
    act1, c1 = _conv_glu_fwd(u1, conv_w[1], conv_b[1], name="ffn1_conv")
    z4, dy, loss_part = _mm_ln(act1, w_out[1], h3, *ln(1, 1), name="ffn1_down_loss", target=target, n_tok=n_tok)
    loss = lax.psum(loss_part[0, 0], ("x", "y", "c"))

    grads = {}

    def ffn_bwd(layer, dz_b, u, c_conv, act, h_in_b, tag):
        dact = _mm(dz_b, w_out[layer], name=tag + "_dact", out_dtype=BF16, b_transposed=True)
        du, dcw, dcb = _conv_glu_bwd(u, c_conv, dact, conv_w[layer], name=tag + "_conv_bwd")
        dh = _mm(du, w_in[layer], name=tag + "_dh", out_dtype=F32, b_transposed=True)
        d_w_out = _mm_tn(act, dz_b, name=tag + "_dw_out")
        d_w_in = _w_in_grad_segments(_mm_tn(h_in_b, du, name=tag + "_dw_in"), name=tag + "_dw_in_segments")
        return dh, d_w_in, d_w_out, _glu_deinterleave(dcw), _glu_deinterleave(dcb)[0]

    dz4, dz4b, dg11, db11 = _ln_bwd(dy, z4, g_ln[1, 1][None, :], name="ln4_bwd")
    dh3, dwin1, dwout1, dcw1, dcb1 = ffn_bwd(1, dz4b, u1, c1, act1, h3b, "ffn1")
    dz3, dz3b, dg10, db10 = _ln_bwd(dh3, z3, g_ln[1, 0][None, :], name="ln3_bwd", dz_next=dz4)

    doT_h = _mm(wo, dz3b, name="attn_do", out_dtype=BF16, b_transposed=True).reshape(H, dh, M)
    grads["w_o"] = _mm(oT, dz3b, name="attn_dw_o", out_dtype=F32)[None]
    dqT_h, dcq, dkT_a, dvT_h = _attn_bwd(qT_h, oT_h, doT_h, lse, cq, k_aug, v_aug, kT_aug, name="attn_bwd")
    dck = -dkT_a[:, :, dh, :].reshape(H, M)
    dc = jnp.pad((dcq[:, 0, :] + dck).T, ((0, 0), (0, LANES - H)))
    dpre, dbias = _forget_cumsum_bwd(dc, pre, bias_f, name="attn_cumsum_bwd")
    d_att = jnp.concatenate([*_heads_merge([dkT_a, dvT_h, dqT_h], M, tm, name="attn_dqkv_merge"), dpre], axis=1)
    dh2 = _mm(d_att, w_att, name="attn_dh", out_dtype=F32, b_transposed=True)
    d_w_att = _mm_tn(h2b, d_att, name="attn_dw_qkv")
    grads["w_kv"] = d_w_att[:, :2 * D]
    grads["w_q"] = d_w_att[:, 2 * D:3 * D][None]
    grads["w_f"] = d_w_att[:, 3 * D:3 * D + H]
    grads["b_f"] = dbias[0, :H]
    dz2, dz2b, dg01, db01 = _ln_bwd(dh2, z2, g_ln[0, 1][None, :], name="ln2_bwd", dz_next=dz3)

    dh1, dwin0, dwout0, dcw0, dcb0 = ffn_bwd(0, dz2b, u0, c0, act0, h1b, "ffn0")
    dz1, _, dg00, db00 = _ln_bwd(dh1, z1, g_ln[0, 0][None, :], name="ln1_bwd", dz_next=dz2)
    dh0, dpw, dscale = _pool_bwd(dz1, diff, pw, scale, name="pool_bwd")

    grads["meta"] = dh0[:N_META]
    grads["pool_w"] = dpw[None]
    grads["pool_scale"] = dscale
    seg_w_in = jnp.stack([dwin0, dwin1], axis=2)
    grads["ffn_w_out"] = jnp.stack([dwout0, dwout1])
    grads["ffn_conv_w"] = jnp.stack([dcw0, dcw1])
    grads["ffn_conv_b"] = jnp.stack([dcb0, dcb1])
    grads["ln_g"] = jnp.stack([jnp.stack([dg00[0], dg01[0]]), jnp.stack([dg10[0], dg11[0]])])
    grads["ln_b"] = jnp.stack([jnp.stack([db00[0], db01[0]]), jnp.stack([db10[0], db11[0]])])
    grad_x = dh0[N_META:n_tok][None]

    blocks = {n: _to_blocks(grads[n], axis_of[n]) for n, _ in PARAMS if n != "ffn_w_in"}
    me = _coords()
    core, chip = (jnp.reshape(v, (1,)).astype(jnp.int32) for v in (me[2], 2 * me[0] + me[1]))
    by_core = lambda t: jnp.swapaxes(t.reshape((N_CHIPS, 2) + t.shape[1:]), 0, 1).astype(BF16)
    segs = [seg_w_in if n == "ffn_w_in" else by_core(blocks[n]) for n in BIG]
    pair = _exchange(segs, "grad_pair_exchange", (1,), lambda pos: pos[2])
    rows_of = lambda t, lead: t.reshape(lead + (-1, t.shape[-1]))
    partial = [_pair_sum(rows_of(s, (2,)), core, rows_of(p[0], ()), name="grad_pair_sum_" + n).reshape(s.shape[1:])
               for n, s, p in zip(BIG, segs, pair)]
    recv = _exchange(partial, "grad_chip_exchange", (4, 2, 6), lambda pos: 2 * pos[0] + pos[1])
    recv_small = _exchange([_pack([blocks[n] for n in SMALL], lead=(N_DEV,))], "grad_small_exchange",
                           tuple(range(1, N_DEV)), _index, by_sender=True)[0]
    results = {}
    for n, own, r in zip(BIG, partial, recv):
        shape = local[n].shape
        as2d = lambda t: t.reshape(-1, shape[-1])
        outs = _adamw(rows_of(own, (N_CHIPS,)), chip, rows_of(r, (N_CHIPS - 1,)), as2d(local[n]), as2d(mom1[n]),
                      as2d(mom2[n]), name="adamw_" + n)
        results[n] = [o_.reshape(shape) for o_ in outs]
    outs = _adamw(recv_small, jnp.zeros((1,), jnp.int32), recv_small[1:],
                  *[_pack([d[n] for n in SMALL]) for d in (local, mom1, mom2)], name="adamw_small")
    small_out = [_unpack(o_, [local[n].shape for n in SMALL]) for o_ in outs]
    for i, n in enumerate(SMALL):
        results[n] = [small_out[k][i] for k in range(4)]
    return (loss, grad_x, *[results[n][k] for k in range(4) for n, _ in PARAMS])
```

```python
import jax
import jax.numpy as jnp
from jax import lax
from jax.experimental import pallas as pl
from jax.experimental.pallas import tpu as pltpu

F32, BF16 = jnp.float32, jnp.bfloat16
MESH = pl.DeviceIdType.MESH

N_DEV = 8
N_CHIPS = 4
N_META = 16
POOL_WINDOWS = (2, 4, 8, 16)
POOL_HALO = 16
CONV_HALO = 8
CONV_STRIP = 16
ALPHA = 4.0 ** 0.25
LN_EPS = 1e-5
NEG_INF = -1e30
ADAM_LR, ADAM_B1, ADAM_B2, ADAM_EPS, ADAM_WD, ADAM_STEP = 0.001, 0.9, 0.999, 1e-08, 0.01, 10

LANES = 128
ROW_ALIGN = 128
ROW_TILES = (640, 512, 128)
PACK_COLS = 1024
ADAM_TILE_BYTES = 4 << 20
MM_TN_WINDOW_BYTES = 28 << 20
GLU_CHUNK = 256
ATTN_EXTRA = 16
VMEM_LIMIT = 56 * 1024 * 1024

PARAMS = (("meta", 1), ("pool_w", 2), ("pool_scale", 1), ("w_kv", 1), ("w_f", 0), ("b_f", None),
          ("w_q", 1), ("w_o", 1), ("ffn_w_in", 2), ("ffn_conv_w", 2), ("ffn_conv_b", None),
          ("ffn_w_out", 1), ("ln_g", 2), ("ln_b", 2))
BIG = ("pool_w", "w_kv", "w_q", "w_o", "ffn_w_in", "ffn_w_out")
SMALL = ("meta", "pool_scale", "w_f", "b_f", "ffn_conv_w", "ffn_conv_b", "ln_g", "ln_b")


def _cparams(**kw):
    return pltpu.CompilerParams(vmem_limit_bytes=VMEM_LIMIT, **kw)


def _pick(n, cands):
    for c in cands:
        if n % c == 0:
            return c
    return n


def _round_up(n, m):
    return (n + m - 1) // m * m


def _pack(pieces, lead=()):
    flat = []
    for p in pieces:
        v = p.reshape(lead + (-1,))
        flat.append(jnp.pad(v, [(0, 0)] * len(lead) + [(0, _round_up(v.shape[-1], PACK_COLS) - v.shape[-1])]))
    v = jnp.concatenate(flat, axis=-1)
    rows = _round_up(v.shape[-1] // PACK_COLS, 8)
    v = jnp.pad(v, [(0, 0)] * len(lead) + [(0, rows * PACK_COLS - v.shape[-1])])
    return v.reshape(lead + (rows, PACK_COLS))


def _unpack(buf, shapes, lead=()):
    flat = buf.reshape(lead + (-1,))
    out, off = [], 0
    for s in shapes:
        n = 1
        for d in s:
            n *= d
        out.append(flat[..., off:off + n].reshape(lead + tuple(s)))
        off += _round_up(n, PACK_COLS)
    return out


def _to_blocks(full, axis):
    if axis is None:
        return jnp.broadcast_to(full[None], (N_DEV,) + full.shape)
    s = full.shape
    x = full.reshape(s[:axis] + (N_DEV, s[axis] // N_DEV) + s[axis + 1:])
    return jnp.moveaxis(x, axis, 0)


def _from_blocks(blocks, axis):
    x = jnp.moveaxis(blocks, 0, axis)
    s = x.shape
    return x.reshape(s[:axis] + (s[axis] * s[axis + 1],) + s[axis + 2:])


def _coords():
    return lax.axis_index("x"), lax.axis_index("y"), lax.axis_index("c")


def _flip(pos, mask):
    x, y, c = pos
    return (1 - x if mask & 4 else x, 1 - y if mask & 2 else y, 1 - c if mask & 1 else c)


def _index(pos):
    x, y, c = pos
    return 4 * x + 2 * y + c


def _comm_call(body, name, arrays, out_shapes):
    n = len(arrays)
    hbm = pl.BlockSpec(memory_space=pl.ANY)
    return pl.pallas_call(
        body, name=name, out_shape=out_shapes, in_specs=[hbm] * n, out_specs=[hbm] * n,
        scratch_shapes=[pltpu.SemaphoreType.DMA((7 * n,)), pltpu.SemaphoreType.DMA((7 * n,)),
                        pltpu.SemaphoreType.DMA((n,))],
    )(*arrays)


def _all_gather(blocks, name):
    chip_masks = (4, 2, 6)
    n = len(blocks)

    def body(*refs):
        x_refs, out_refs = refs[:n], refs[n:2 * n]
        send_sems, recv_sems, local_sems = refs[2 * n:]
        me = _coords()
        sibling = _flip(me, 1)

        def copy(a, k, owner, to, from_input=False):
            slot = out_refs[a].at[_index(owner)]
            return pltpu.make_async_remote_copy(
                src_ref=x_refs[a] if from_input else slot, dst_ref=slot,
                send_sem=send_sems.at[7 * a + k], recv_sem=recv_sems.at[7 * a + k],
                device_id=to, device_id_type=MESH)

        mine = [pltpu.make_async_copy(x_refs[a], out_refs[a].at[_index(me)], local_sems.at[a]) for a in range(n)]
        first = [copy(a, 0, me, sibling, True) for a in range(n)]
        first += [copy(a, 1 + j, me, _flip(me, m), True) for j, m in enumerate(chip_masks) for a in range(n)]
        for cp in mine + first:
            cp.start()
        passed = []
        for j, m in enumerate(chip_masks):
            for a in range(n):
                copy(a, 1 + j, _flip(me, m), me).wait_recv()
                passed.append(copy(a, 4 + j, _flip(me, m), sibling))
                passed[-1].start()
        for a in range(n):
            copy(a, 0, sibling, me).wait_recv()
            for j, m in enumerate(chip_masks):
                copy(a, 4 + j, _flip(sibling, m), me).wait_recv()
        for cp in first + passed:
            cp.wait_send()
        for cp in mine:
            cp.wait()

    return _comm_call(body, name, blocks, [jax.ShapeDtypeStruct((N_DEV,) + b.shape, b.dtype) for b in blocks])


def _exchange(segs, name, masks, slot_of, by_sender=False):
    n = len(segs)

    def body(*refs):
        seg_refs, out_refs = refs[:n], refs[n:2 * n]
        send_sems, recv_sems, local_sems = refs[2 * n:]
        me = _coords()

        def copy(a, k, sender):
            to = _flip(sender, masks[k])
            return pltpu.make_async_remote_copy(
                src_ref=seg_refs[a].at[slot_of(to)], dst_ref=out_refs[a].at[slot_of(sender) if by_sender else k],
                send_sem=send_sems.at[7 * a + k], recv_sem=recv_sems.at[7 * a + k],
                device_id=to, device_id_type=MESH)

        mine = [pltpu.make_async_copy(seg_refs[a].at[slot_of(me)], out_refs[a].at[slot_of(me)], local_sems.at[a])
                for a in range(n)] if by_sender else []
        sends = [copy(a, k, me) for k in range(len(masks)) for a in range(n)]
        for cp in mine + sends:
            cp.start()
        for k, mask in enumerate(masks):
            for a in range(n):
                copy(a, k, _flip(me, mask)).wait_recv()
        for cp in sends:
            cp.wait_send()
        for cp in mine:
            cp.wait()

    slots = lambda s: s.shape[0] if by_sender else len(masks)
    return _comm_call(body, name, segs, [jax.ShapeDtypeStruct((slots(s),) + s.shape[1:], s.dtype) for s in segs])


def _own_slot(tr, cols):
    return pl.BlockSpec((1, tr, cols), lambda i, slot: (slot[0], i, 0))


def _pair_sum(seg, slot, recv, name):
    _, rows, cols = seg.shape
    tr = max(t for t in range(16, rows + 1, 16) if rows % t == 0 and 2 * t * cols * 2 <= ADAM_TILE_BYTES)

    def body(slot_ref, own_ref, recv_ref, o_ref):
        o_ref[...] = (own_ref[0].astype(F32) + recv_ref[...].astype(F32)).astype(BF16)

    tile = pl.BlockSpec((tr, cols), lambda i, slot: (i, 0))
    return pl.pallas_call(
        body, name=name,
        grid_spec=pltpu.PrefetchScalarGridSpec(
            num_scalar_prefetch=1, grid=(rows // tr,), in_specs=[_own_slot(tr, cols), tile], out_specs=tile),
        out_shape=jax.ShapeDtypeStruct((rows, cols), BF16),
        compiler_params=_cparams(),
    )(slot, seg, recv)


def _adamw(seg, slot, recv, w, m, v, name):
    rows, cols = w.shape
    n_slots = recv.shape[0]
    sublanes = 8 * (4 // recv.dtype.itemsize)
    tr = max(t for t in range(sublanes, rows + 1, sublanes)
             if rows % t == 0 and N_DEV * t * cols * 4 <= ADAM_TILE_BYTES)
    c1 = 1.0 - ADAM_B1 ** ADAM_STEP
    c2 = 1.0 - ADAM_B2 ** ADAM_STEP

    def body(slot_ref, own_ref, r_ref, w_ref, m_ref, v_ref, g_out, d_out, m_out, v_out):
        g = own_ref[0].astype(F32)
        for s in range(n_slots):
            g = g + r_ref[s].astype(F32)
        m_new = ADAM_B1 * m_ref[...] + (1.0 - ADAM_B1) * g
        v_new = ADAM_B2 * v_ref[...] + (1.0 - ADAM_B2) * (g * g)
        m_hat = m_new / c1
        v_hat = v_new / c2
        g_out[...] = g
        d_out[...] = -ADAM_LR * (m_hat / (jnp.sqrt(v_hat) + ADAM_EPS) + ADAM_WD * w_ref[...])
        m_out[...] = m_new
        v_out[...] = v_new

    tile = pl.BlockSpec((tr, cols), lambda i, slot: (i, 0))
    return pl.pallas_call(
        body, name=name,
        grid_spec=pltpu.PrefetchScalarGridSpec(
            num_scalar_prefetch=1, grid=(rows // tr,),
            in_specs=[_own_slot(tr, cols), pl.BlockSpec((n_slots, tr, cols), lambda i, slot: (0, i, 0)),
                      tile, tile, tile],
            out_specs=[tile] * 4),
        out_shape=[jax.ShapeDtypeStruct(w.shape, F32)] * 4,
        compiler_params=_cparams(),
    )(slot, seg, recv, w, m, v)


def _mm(a, b, *, name, out_dtype, b_transposed=False):
    M, K = a.shape
    N = b.shape[0] if b_transposed else b.shape[1]
    tm = _pick(M, ROW_TILES)
    tn = _pick(N, (2816, 1024, 640, 512, 256, 128))
    tk = K if K <= 3200 else _pick(K, (2816, 1024, 640, 512, 256, 128))
    nk = K // tk

    def body(*refs):
        a_ref, b_ref, o_ref = refs[:3]
        acc_ref = refs[-1] if nk > 1 else None
        if b_transposed:
            prod = _dot_nt(a_ref[...], b_ref[...])
        else:
            prod = jnp.dot(a_ref[...], b_ref[...], preferred_element_type=F32)
        if nk == 1:
            o_ref[...] = prod.astype(out_dtype)
        else:
            k = pl.program_id(2)

            @pl.when(k == 0)
            def _():
                acc_ref[...] = prod

            @pl.when(k > 0)
            def _():
                acc_ref[...] += prod

            @pl.when(k == nk - 1)
            def _():
                o_ref[...] = acc_ref[...].astype(out_dtype)

    b_spec = (pl.BlockSpec((tn, tk), lambda i, j, k: (j, k)) if b_transposed
              else pl.BlockSpec((tk, tn), lambda i, j, k: (k, j)))
    return pl.pallas_call(
        body, name=name, grid=(M // tm, N // tn, nk),
        in_specs=[pl.BlockSpec((tm, tk), lambda i, j, k: (i, k)), b_spec],
        out_specs=pl.BlockSpec((tm, tn), lambda i, j, k: (i, j)),
        out_shape=jax.ShapeDtypeStruct((M, N), out_dtype),
        scratch_shapes=[pltpu.VMEM((tm, tn), F32)] if nk > 1 else [],
        compiler_params=_cparams(),
    )(a, b)


def _mm_tn(a, b, *, name):
    T, M = a.shape
    N = b.shape[1]
    tm = _pick(M, (1408, 1024, 512, 256, 128))
    tn = _pick(N, (1024, 640, 512, 256, 128))
    tt = _pick(T, [t for t in (4160, 1664, 640) if 2 * 2 * t * (tm + tn) <= MM_TN_WINDOW_BYTES] + [128])

    def body(a_ref, b_ref, o_ref):
        prod = lax.dot_general(a_ref[...], b_ref[...], (((0,), (0,)), ((), ())), preferred_element_type=F32)

        @pl.when(pl.program_id(2) == 0)
        def _():
            o_ref[...] = prod

        @pl.when(pl.program_id(2) > 0)
        def _():
            o_ref[...] += prod

    return pl.pallas_call(
        body, name=name, grid=(M // tm, N // tn, T // tt),
        in_specs=[pl.BlockSpec((tt, tm), lambda i, j, k: (k, i)), pl.BlockSpec((tt, tn), lambda i, j, k: (k, j))],
        out_specs=pl.BlockSpec((tm, tn), lambda i, j, k: (i, j)),
        out_shape=jax.ShapeDtypeStruct((M, N), F32),
        compiler_params=_cparams(),
    )(a, b)


def _layer_norm(z, g, b):
    mu = jnp.mean(z, axis=-1, keepdims=True)
    xc = z - mu
    var = jnp.mean(xc * xc, axis=-1, keepdims=True)
    return xc * lax.rsqrt(var + LN_EPS) * g + b


def _mm_ln(a, w, res, g, b, *, name, target=None, n_tok=None, a_transposed=False):
    K, M = a.shape if a_transposed else a.shape[::-1]
    D = w.shape[1]
    tm = _pick(M, ROW_TILES)
    final = target is not None

    def body(*refs):
        if final:
            a_ref, w_ref, r_ref, g_ref, b_ref, t_ref, z_ref, dy_ref, loss_ref = refs
        else:
            a_ref, w_ref, r_ref, g_ref, b_ref, z_ref, h_ref, hb_ref = refs
        contract = (((0,), (0,)), ((), ())) if a_transposed else (((1,), (0,)), ((), ()))
        z = ALPHA * r_ref[...] + lax.dot_general(a_ref[...], w_ref[...], contract, preferred_element_type=F32)
        z_ref[...] = z
        h = _layer_norm(z, g_ref[...], b_ref[...])
        if not final:
            h_ref[...] = h
            hb_ref[...] = h.astype(BF16)
            return
        i = pl.program_id(0)
        row = i * tm + lax.broadcasted_iota(jnp.int32, (tm, 1), 0)
        valid = (row >= N_META) & (row < n_tok)
        err = jnp.where(valid, h - t_ref[...], 0.0)
        dy_ref[...] = err / D

        @pl.when(i == 0)
        def _():
            loss_ref[...] = jnp.zeros_like(loss_ref)

        loss_ref[...] += 0.5 * jnp.sum(jnp.sum(err * err, axis=1, keepdims=True) / D, axis=0, keepdims=True)

    row_blk = lambda cols: pl.BlockSpec((tm, cols), lambda i: (i, 0))
    vec = pl.BlockSpec((1, D), lambda i: (0, 0))
    a_blk = pl.BlockSpec((K, tm), lambda i: (0, i)) if a_transposed else row_blk(K)
    in_specs = [a_blk, pl.BlockSpec((K, D), lambda i: (0, 0)), row_blk(D), vec, vec]
    args = [a, w, res, g, b]
    if final:
        in_specs.append(row_blk(D))
        args.append(target)
        out_specs = [row_blk(D), row_blk(D), pl.BlockSpec((8, LANES), lambda i: (0, 0))]
        out_shape = [jax.ShapeDtypeStruct((M, D), F32)] * 2 + [jax.ShapeDtypeStruct((8, LANES), F32)]
    else:
        out_specs = [row_blk(D)] * 3
        out_shape = [jax.ShapeDtypeStruct((M, D), F32)] * 2 + [jax.ShapeDtypeStruct((M, D), BF16)]
    return pl.pallas_call(
        body, name=name, grid=(M // tm,), in_specs=in_specs, out_specs=out_specs, out_shape=out_shape,
        compiler_params=_cparams(),
    )(*args)


def _ln_bwd_rows(dh_v, z_v, g_v):
    mu = jnp.mean(z_v, axis=-1, keepdims=True)
    xc = z_v - mu
    rstd = lax.rsqrt(jnp.mean(xc * xc, axis=-1, keepdims=True) + LN_EPS)
    xhat = xc * rstd
    dxhat = dh_v * g_v
    dz = rstd * (dxhat - jnp.mean(dxhat, axis=-1, keepdims=True)
                 - xhat * jnp.mean(dxhat * xhat, axis=-1, keepdims=True))
    return dz, xhat


def _ln_bwd(dh, z, g, *, name):
    M, D = z.shape
    tm = _pick(M, ROW_TILES)

    def body(dh_ref, z_ref, g_ref, dz_ref, dzb_ref, dg_ref, db_ref):
        dh_v = dh_ref[...]
        dz, xhat = _ln_bwd_rows(dh_v, z_ref[...], g_ref[...])
        dz_ref[...] = dz
        dzb_ref[...] = dz.astype(BF16)

        @pl.when(pl.program_id(0) == 0)
        def _():
            dg_ref[...] = jnp.zeros_like(dg_ref)
            db_ref[...] = jnp.zeros_like(db_ref)

        dg_ref[...] += jnp.sum(dh_v * xhat, axis=0, keepdims=True)
        db_ref[...] += jnp.sum(dh_v, axis=0, keepdims=True)

    row_blk = pl.BlockSpec((tm, D), lambda i: (i, 0))
    vec = pl.BlockSpec((1, D), lambda i: (0, 0))
    return pl.pallas_call(
        body, name=name, grid=(M // tm,), in_specs=[row_blk, row_blk, vec],
        out_specs=[row_blk, row_blk, vec, vec],
        out_shape=[jax.ShapeDtypeStruct((M, D), F32), jax.ShapeDtypeStruct((M, D), BF16),
                   jax.ShapeDtypeStruct((1, D), F32), jax.ShapeDtypeStruct((1, D), F32)],
        compiler_params=_cparams(),
    )(dh, z, g)


def _mm_ln_bwd(a, b, dz_next, z, g, *, name):
    M, K = a.shape
    D = b.shape[0]
    tm = _pick(M, ROW_TILES)
    tk = K if K <= 3200 else _pick(K, (2816, 1024, 640, 512, 256, 128))
    nk = K // tk

    def body(a_ref, b_ref, nx_ref, z_ref, g_ref, dz_ref, dzb_ref, dg_ref, db_ref, *scratch):
        i, k = pl.program_id(0), pl.program_id(1)
        prod = _dot_nt(a_ref[...], b_ref[...])

        @pl.when((i == 0) & (k == 0))
        def _():
            dg_ref[...] = jnp.zeros_like(dg_ref)
            db_ref[...] = jnp.zeros_like(db_ref)

        def finish(acc):
            dh_v = acc + ALPHA * nx_ref[...]
            dz, xhat = _ln_bwd_rows(dh_v, z_ref[...], g_ref[...])
            dz_ref[...] = dz
            dzb_ref[...] = dz.astype(BF16)
            dg_ref[...] += jnp.sum(dh_v * xhat, axis=0, keepdims=True)
            db_ref[...] += jnp.sum(dh_v, axis=0, keepdims=True)

        if nk == 1:
            finish(prod)
        else:
            acc_ref, = scratch

            @pl.when(k == 0)
            def _():
                acc_ref[...] = prod

            @pl.when(k > 0)
            def _():
                acc_ref[...] += prod

            @pl.when(k == nk - 1)
            def _():
                finish(acc_ref[...])

    row_blk = pl.BlockSpec((tm, D), lambda i, k: (i, 0))
    vec = pl.BlockSpec((1, D), lambda i, k: (0, 0))
    return pl.pallas_call(
        body, name=name, grid=(M // tm, nk),
        in_specs=[pl.BlockSpec((tm, tk), lambda i, k: (i, k)), pl.BlockSpec((D, tk), lambda i, k: (0, k)),
                  row_blk, row_blk, vec],
        out_specs=[row_blk, row_blk, vec, vec],
        out_shape=[jax.ShapeDtypeStruct((M, D), F32), jax.ShapeDtypeStruct((M, D), BF16),
                   jax.ShapeDtypeStruct((1, D), F32), jax.ShapeDtypeStruct((1, D), F32)],
        scratch_shapes=[pltpu.VMEM((tm, D), F32)] if nk > 1 else [],
        compiler_params=_cparams(),
    )(a, b, dz_next, z, g)


def _pool_fwd(h0, pw, scale, g, b, *, name):
    M, D = h0.shape
    n_groups, G, _ = pw.shape
    tm = _pick(M, ROW_TILES)

    def body(x_ref, halo_ref, pw_ref, sc_ref, g_ref, b_ref, z_ref, h_ref, hb_ref, diff_ref, ext_ref, mix_ref):
        i = pl.program_id(0)
        x = x_ref[...]
        ext_ref[0:POOL_HALO, :] = jnp.where(i == 0, 0.0, halo_ref[...])
        ext_ref[POOL_HALO:, :] = x
        tok = i * tm + lax.broadcasted_iota(jnp.int32, (tm, 1), 0)
        for gi, win in enumerate(POOL_WINDOWS):
            cols = slice(gi * G, (gi + 1) * G)
            xs = x[:, cols]
            s = xs
            for k in range(1, win):
                s = s + ext_ref[pl.ds(POOL_HALO - k, tm), cols]
            count = jnp.minimum(tok + 1, win).astype(F32)
            d = (s / count - xs).astype(BF16)
            diff_ref[:, cols] = d
            mix_ref[:, cols] = jnp.dot(d, pw_ref[gi], preferred_element_type=F32)
        z = ALPHA * x + mix_ref[...] * sc_ref[...]
        z_ref[...] = z
        h = _layer_norm(z, g_ref[...], b_ref[...])
        h_ref[...] = h
        hb_ref[...] = h.astype(BF16)

    row_blk = pl.BlockSpec((tm, D), lambda i: (i, 0))
    vec = pl.BlockSpec((1, D), lambda i: (0, 0))
    halo = pl.BlockSpec((POOL_HALO, D), lambda i: (jnp.maximum(i * (tm // POOL_HALO) - 1, 0), 0))
    return pl.pallas_call(
        body, name=name, grid=(M // tm,),
        in_specs=[row_blk, halo, pl.BlockSpec((n_groups, G, G), lambda i: (0, 0, 0)), vec, vec, vec],
        out_specs=[row_blk] * 4,
        out_shape=[jax.ShapeDtypeStruct((M, D), F32)] * 2 + [jax.ShapeDtypeStruct((M, D), BF16)] * 2,
        scratch_shapes=[pltpu.VMEM((tm + POOL_HALO, D), F32), pltpu.VMEM((tm, D), F32)],
        compiler_params=_cparams(),
    )(h0, h0, pw, scale, g, b)


def _pool_bwd(dz, diff, pw, scale, *, name):
    M, D = dz.shape
    n_groups, G, _ = pw.shape
    tm = _pick(M, ROW_TILES)
    nt = M // tm

    def body(dz_ref, halo_ref, diff_ref, pw_ref, sc_ref, dh_ref, dpw_ref, dsc_ref, ext_ref, q_ref):
        i = pl.program_id(0)
        dz_v = dz_ref[...]
        ext_ref[0:tm, :] = dz_v
        ext_ref[tm:, :] = jnp.where(i == nt - 1, 0.0, halo_ref[...])
        tok = i * tm + lax.broadcasted_iota(jnp.int32, (tm + POOL_HALO, 1), 0)

        @pl.when(i == 0)
        def _():
            dpw_ref[...] = jnp.zeros_like(dpw_ref)
            dsc_ref[...] = jnp.zeros_like(dsc_ref)

        for gi, win in enumerate(POOL_WINDOWS):
            cols = slice(gi * G, (gi + 1) * G)
            dmix = (ext_ref[:, cols] * sc_ref[:, cols]).astype(BF16)
            ddiff = _dot_nt(dmix, pw_ref[gi])
            count = jnp.minimum(tok + 1, win).astype(F32)
            q_ref[:, cols] = ddiff / count
            acc = -ddiff[0:tm]
            for k in range(win):
                acc = acc + q_ref[pl.ds(k, tm), cols]
            dh_ref[:, cols] = ALPHA * dz_v[:, cols] + acc
            d = diff_ref[:, cols]
            dpw_ref[gi] += lax.dot_general(d, dmix[0:tm], (((0,), (0,)), ((), ())), preferred_element_type=F32)
            mixed = jnp.dot(d, pw_ref[gi], preferred_element_type=F32)
            dsc_ref[:, cols] += jnp.sum(dz_v[:, cols] * mixed, axis=0, keepdims=True)

    row_blk = pl.BlockSpec((tm, D), lambda i: (i, 0))
    vec = pl.BlockSpec((1, D), lambda i: (0, 0))
    per_tile = tm // POOL_HALO
    halo = pl.BlockSpec((POOL_HALO, D), lambda i: (jnp.minimum((i + 1) * per_tile, nt * per_tile - 1), 0))
    wblk = pl.BlockSpec((n_groups, G, G), lambda i: (0, 0, 0))
    return pl.pallas_call(
        body, name=name, grid=(nt,),
        in_specs=[row_blk, halo, row_blk, wblk, vec],
        out_specs=[row_blk, wblk, vec],
        out_shape=[jax.ShapeDtypeStruct((M, D), F32), jax.ShapeDtypeStruct((n_groups, G, G), F32),
                   jax.ShapeDtypeStruct((1, D), F32)],
        scratch_shapes=[pltpu.VMEM((tm + POOL_HALO, D), F32), pltpu.VMEM((tm + POOL_HALO, D), F32)],
        compiler_params=_cparams(),
    )(dz, dz, diff, pw, scale)


def _glu_interleave(x):
    s = x.shape
    n = s[-1] // (2 * GLU_CHUNK)
    return jnp.swapaxes(x.reshape(s[:-1] + (2, n, GLU_CHUNK)), -3, -2).reshape(s)


def _glu_deinterleave(x):
    s = x.shape
    n = s[-1] // (2 * GLU_CHUNK)
    return jnp.swapaxes(x.reshape(s[:-1] + (n, 2, GLU_CHUNK)), -3, -2).reshape(s)


W_PACK_ROWS = 256
W_PACK_PIECE = 64


def _interleaved_col(col, F):
    half, within = divmod(col, F)
    chunk, off = divmod(within, GLU_CHUNK)
    return chunk * 2 * GLU_CHUNK + half * GLU_CHUNK + off


def _w_in_grad_segments(dw, *, name):
    D, F2 = dw.shape
    width = F2 // N_DEV
    rows = _pick(D, (W_PACK_ROWS, 128))
    assert width % W_PACK_PIECE == 0

    def body(x_ref, o_ref):
        for j in range(N_DEV):
            for q in range(0, width, W_PACK_PIECE):
                at = _interleaved_col(width * j + q, F2 // 2)
                o_ref[j % 2, j // 2, :, q:q + W_PACK_PIECE] = x_ref[:, at:at + W_PACK_PIECE].astype(BF16)

    return pl.pallas_call(
        body, name=name, grid=(D // rows,),
        in_specs=[pl.BlockSpec((rows, F2), lambda r: (r, 0))],
        out_specs=pl.BlockSpec((2, N_CHIPS, rows, width), lambda r: (0, 0, r, 0)),
        out_shape=jax.ShapeDtypeStruct((2, N_CHIPS, D, width), BF16),
        compiler_params=_cparams(),
    )(dw)


def _w_in_interleaved(gathered, *, name):
    n_dev, depth, D, width = gathered.shape
    F = n_dev * width // 2
    rows = _pick(D, (W_PACK_ROWS, 128))
    assert width % W_PACK_PIECE == 0 and GLU_CHUNK % W_PACK_PIECE == 0 and F % GLU_CHUNK == 0
    interleaved = lambda col: _interleaved_col(col, F)

    def body(x_ref, o_ref):
        for j in range(n_dev):
            for q in range(0, width, W_PACK_PIECE):
                to = interleaved(width * j + q)
                o_ref[0, :, to:to + W_PACK_PIECE] = x_ref[j, 0, :, q:q + W_PACK_PIECE]

    return pl.pallas_call(
        body, name=name, grid=(depth, D // rows),
        in_specs=[pl.BlockSpec((n_dev, 1, rows, width), lambda l, r: (0, l, r, 0))],
        out_specs=pl.BlockSpec((1, rows, n_dev * width), lambda l, r: (l, r, 0)),
        out_shape=jax.ShapeDtypeStruct((depth, D, n_dev * width), gathered.dtype),
        compiler_params=_cparams(),
    )(gathered)


def _taps(u_ref, head_ref, r0):
    src, base = (head_ref, CONV_HALO) if r0 == 0 else (u_ref, r0)
    return tuple(src[pl.ds(base - k, CONV_STRIP), :] for k in range(3))


def _conv_glu_fwd(u, cw, cb, *, name):
    M, F2 = u.shape
    tm = _pick(M, ROW_TILES)
    tc = 2 * GLU_CHUNK

    def body(u_ref, halo_ref, w_ref, b_ref, o_ref, c_ref, head_ref):
        i = pl.program_id(0)
        head_ref[0:CONV_HALO, :] = jnp.where(i == 0, 0.0, halo_ref[...])
        head_ref[CONV_HALO:, :] = u_ref[0:CONV_STRIP, :]
        w0, w1, w2, b = w_ref[0:1, :], w_ref[1:2, :], w_ref[2:3, :], b_ref[...]
        for r0 in range(0, tm, CONV_STRIP):
            u0, u1, u2 = _taps(u_ref, head_ref, r0)
            c = b + w0 * u2 + w1 * u1 + w2 * u0
            c_ref[pl.ds(r0, CONV_STRIP), :] = c.astype(BF16)
            a, g = c[:, :GLU_CHUNK], c[:, GLU_CHUNK:]
            o_ref[pl.ds(r0, CONV_STRIP), :] = (a * jax.nn.sigmoid(a) * g).astype(BF16)

    per_tile = tm // CONV_HALO
    return pl.pallas_call(
        body, name=name, grid=(M // tm, F2 // tc),
        in_specs=[pl.BlockSpec((tm, tc), lambda i, j: (i, j)),
                  pl.BlockSpec((CONV_HALO, tc), lambda i, j: (jnp.maximum(i * per_tile - 1, 0), j)),
                  pl.BlockSpec((3, tc), lambda i, j: (0, j)), pl.BlockSpec((1, tc), lambda i, j: (0, j))],
        out_specs=[pl.BlockSpec((tm, GLU_CHUNK), lambda i, j: (i, j)), pl.BlockSpec((tm, tc), lambda i, j: (i, j))],
        out_shape=[jax.ShapeDtypeStruct((M, F2 // 2), BF16), jax.ShapeDtypeStruct((M, F2), BF16)],
        scratch_shapes=[pltpu.VMEM((CONV_HALO + CONV_STRIP, tc), F32)],
        compiler_params=_cparams(),
    )(u, u, cw, cb)


def _conv_glu_bwd(u, c, dact, cw, *, name):
    M, F2 = u.shape
    tm = _pick(M, ROW_TILES)
    nt = M // tm
    tc = 2 * GLU_CHUNK

    def body(u_ref, c_ref, da_ref, w_ref, du_ref, dw_ref, db_ref, dcx_ref, carry_ref):
        i = pl.program_id(1)
        w0, w1, w2 = w_ref[0:1, :], w_ref[1:2, :], w_ref[2:3, :]

        @pl.when(i == 0)
        def _():
            dw_ref[...] = jnp.zeros_like(dw_ref)
            db_ref[...] = jnp.zeros_like(db_ref)
            carry_ref[...] = jnp.zeros_like(carry_ref)

        def fold(t):
            return sum(t[r:r + 8] for r in range(0, CONV_STRIP, 8))

        dcx_ref[tm:, :] = carry_ref[...]
        s_b = s_0 = s_1 = s_2 = jnp.zeros((8, tc), F32)
        for r0 in reversed(range(0, tm, CONV_STRIP)):
            rows = pl.ds(r0, CONV_STRIP)
            c_v = c_ref[rows, :].astype(F32)
            a, g = c_v[:, :GLU_CHUNK], c_v[:, GLU_CHUNK:]
            sig = jax.nn.sigmoid(a)
            dact_v = da_ref[rows, :].astype(F32)
            d_a = dact_v * g * (sig * (1.0 + a * (1.0 - sig)))
            d_g = dact_v * (a * sig)
            dc = jnp.concatenate([d_a, d_g], axis=1)
            dcx_ref[rows, :] = dc
            dc1, dc2 = dcx_ref[pl.ds(r0 + 1, CONV_STRIP), :], dcx_ref[pl.ds(r0 + 2, CONV_STRIP), :]
            du_ref[rows, :] = (w2 * dc + w1 * dc1 + w0 * dc2).astype(BF16)
            u_v = u_ref[rows, :]
            s_b, s_0, s_1, s_2 = s_b + fold(dc), s_0 + fold(dc2 * u_v), s_1 + fold(dc1 * u_v), s_2 + fold(dc * u_v)
        carry_ref[...] = dcx_ref[0:CONV_HALO, :]
        db_ref[...] += jnp.sum(s_b, axis=0, keepdims=True)
        dw_ref[0:1, :] += jnp.sum(s_0, axis=0, keepdims=True)
        dw_ref[1:2, :] += jnp.sum(s_1, axis=0, keepdims=True)
        dw_ref[2:3, :] += jnp.sum(s_2, axis=0, keepdims=True)

    per_tile = tm // CONV_HALO
    rev = lambda i: nt - 1 - i
    return pl.pallas_call(
        body, name=name, grid=(F2 // tc, nt),
        in_specs=[pl.BlockSpec((tm, tc), lambda j, i: (rev(i), j)), pl.BlockSpec((tm, tc), lambda j, i: (rev(i), j)),
                  pl.BlockSpec((tm, GLU_CHUNK), lambda j, i: (rev(i), j)),
                  pl.BlockSpec((3, tc), lambda j, i: (0, j))],
        out_specs=[pl.BlockSpec((tm, tc), lambda j, i: (rev(i), j)),
                   pl.BlockSpec((3, tc), lambda j, i: (0, j)), pl.BlockSpec((1, tc), lambda j, i: (0, j))],
        out_shape=[jax.ShapeDtypeStruct((M, F2), BF16), jax.ShapeDtypeStruct((3, F2), F32),
                   jax.ShapeDtypeStruct((1, F2), F32)],
        scratch_shapes=[pltpu.VMEM((tm + CONV_HALO, tc), F32), pltpu.VMEM((CONV_HALO, tc), F32)],
        compiler_params=_cparams(),
    )(u, c, dact, cw)


def _split3(x):
    hi = x.astype(BF16)
    r = x - hi.astype(F32)
    mid = r.astype(BF16)
    lo = (r - mid.astype(F32)).astype(BF16)
    return hi, mid, lo


def _tri_sum(tri, x):
    return sum(jnp.dot(tri, part, preferred_element_type=F32) for part in _split3(x))


def _log_sigmoid(x):
    return jnp.minimum(x, 0.0) - jnp.log1p(jnp.exp(-jnp.abs(x)))


def _forget_cumsum(pre, bias, *, name):
    M, C = pre.shape
    tm = _pick(M, ROW_TILES)

    def body(p_ref, b_ref, c_ref, carry_ref):
        i = pl.program_id(0)

        @pl.when(i == 0)
        def _():
            carry_ref[...] = jnp.zeros_like(carry_ref)

        logf = _log_sigmoid(p_ref[...] + b_ref[...])
        r = lax.broadcasted_iota(jnp.int32, (tm, tm), 0)
        s = lax.broadcasted_iota(jnp.int32, (tm, tm), 1)
        c_ref[...] = _tri_sum((s <= r).astype(BF16), logf) + carry_ref[...]
        carry_ref[...] = c_ref[pl.ds(tm - 1, 1), :]

    return pl.pallas_call(
        body, name=name, grid=(M // tm,),
        in_specs=[pl.BlockSpec((tm, C), lambda i: (i, 0)), pl.BlockSpec((1, C), lambda i: (0, 0))],
        out_specs=pl.BlockSpec((tm, C), lambda i: (i, 0)),
        out_shape=jax.ShapeDtypeStruct((M, C), F32),
        scratch_shapes=[pltpu.VMEM((1, C), F32)],
        compiler_params=_cparams(),
    )(pre, bias)


def _forget_cumsum_bwd(dc, pre, bias, *, name):
    M, C = pre.shape
    tm = _pick(M, ROW_TILES)
    nt = M // tm

    def body(dc_ref, p_ref, b_ref, dp_ref, db_ref, carry_ref, run_ref):
        i = pl.program_id(0)

        @pl.when(i == 0)
        def _():
            carry_ref[...] = jnp.zeros_like(carry_ref)
            db_ref[...] = jnp.zeros_like(db_ref)

        r = lax.broadcasted_iota(jnp.int32, (tm, tm), 0)
        s = lax.broadcasted_iota(jnp.int32, (tm, tm), 1)
        run_ref[...] = _tri_sum((s >= r).astype(BF16), dc_ref[...]) + carry_ref[...]
        carry_ref[...] = run_ref[pl.ds(0, 1), :]
        dpre = run_ref[...] * jax.nn.sigmoid(-(p_ref[...] + b_ref[...]))
        dp_ref[...] = dpre.astype(BF16)
        db_ref[...] += jnp.sum(dpre, axis=0, keepdims=True)

    rev_blk = pl.BlockSpec((tm, C), lambda i: (nt - 1 - i, 0))
    vec = pl.BlockSpec((1, C), lambda i: (0, 0))
    return pl.pallas_call(
        body, name=name, grid=(nt,),
        in_specs=[rev_blk, rev_blk, vec], out_specs=[rev_blk, vec],
        out_shape=[jax.ShapeDtypeStruct((M, C), BF16), jax.ShapeDtypeStruct((1, C), F32)],
        scratch_shapes=[pltpu.VMEM((1, C), F32), pltpu.VMEM((tm, C), F32)],
        compiler_params=_cparams(),
    )(dc, pre, bias)


def _causal_mask(tm):
    key = lax.broadcasted_iota(jnp.int32, (tm, tm), 0)
    query = lax.broadcasted_iota(jnp.int32, (tm, tm), 1)
    return key <= query


def _dot_nt(a, b):
    return lax.dot_general(a, b, (((1,), (1,)), ((), ())), preferred_element_type=F32)


def _loop_unrolled(n, step, init, unroll):
    def trip(p, carry):
        for r in range(unroll):
            carry = step(unroll * p + r, carry)
        return carry
    carry = lax.fori_loop(0, n // unroll, trip, init)
    return lax.fori_loop(unroll * (n // unroll), n, step, carry)


def _rows(main, extras, total):
    tm = main.shape[1]
    used = sum(e.shape[0] for e in extras)
    tile = jnp.concatenate([e.astype(BF16) for e in extras] + [jnp.zeros((ATTN_EXTRA - used, tm), BF16)], axis=0)
    rest = total - main.shape[0] - ATTN_EXTRA
    return jnp.concatenate([main, tile] + ([jnp.zeros((rest, tm), BF16)] if rest else []), axis=0)


def _attn_specs(H, M, dh, tm):
    nt = M // tm
    qT_blk = pl.BlockSpec((1, dh, tm), lambda h, i: (h, 0, i))
    row_blk = pl.BlockSpec((1, 1, tm), lambda h, i: (h, 0, i))
    key_blk = pl.BlockSpec((1, M, LANES), lambda h, i: (h, 0, 0))
    keyT_blk = pl.BlockSpec((1, nt, dh + ATTN_EXTRA, tm), lambda h, i: (h, 0, 0, 0))
    return qT_blk, row_blk, key_blk, keyT_blk


def _pair_specs(M, dh, tm):
    assert 2 * dh == LANES
    cols = lambda section: pl.BlockSpec((tm, LANES), lambda p, i: (i, section + p))
    headsT = pl.BlockSpec((2, dh, tm), lambda p, i: (p, 0, i))
    return cols, headsT


def _attn_pack(qkv, c, H, *, name):
    M, D3 = qkv.shape
    D = D3 // 3
    dh = D // H
    tm = _pick(M, ROW_TILES)
    nt = M // tm
    cols, headsT = _pair_specs(M, dh, tm)
    key_blk = pl.BlockSpec((2, tm, LANES), lambda p, i: (p, i, 0))
    keyT_blk = pl.BlockSpec((2, 1, dh + ATTN_EXTRA, tm), lambda p, i: (p, i, 0, 0))

    def body(k_ref, v_ref, q_ref, c_ref, ka_ref, va_ref, qT_ref, kT_ref, vT_ref):
        p = pl.program_id(0)
        lane = lax.broadcasted_iota(jnp.int32, (tm, LANES), 1)
        k_v, v_v = k_ref[...], v_ref[...]
        ones_v = jnp.where((lane >= dh) & (lane < dh + 3), 1.0, 0.0)
        for e in range(2):
            ck = jnp.sum(jnp.where(lane == 2 * p + e, c_ref[...], 0.0), axis=1, keepdims=True)
            hi, mid, lo = _split3(-ck)
            extra = jnp.where(lane == dh + 3, hi.astype(F32), jnp.where(lane == dh + 4, mid.astype(F32),
                              jnp.where(lane == dh + 5, lo.astype(F32), ones_v)))
            first = lambda t: t if e == 0 else pltpu.roll(t, dh, 1)
            ka_ref[e] = jnp.where(lane < dh, first(k_v), extra.astype(BF16))
            va_ref[e] = jnp.where(lane < dh, first(v_v), ones_v.astype(BF16))
        row = lax.broadcasted_iota(jnp.int32, (ATTN_EXTRA, tm), 0)
        tail = jnp.where(row == 0, 1.0, 0.0).astype(BF16)
        for src, dst in ((k_v, kT_ref), (v_v, vT_ref)):
            t = src.T
            for e in range(2):
                dst[e, 0, 0:dh, :] = t[e * dh:(e + 1) * dh]
                dst[e, 0, dh:, :] = tail
        qT_ref[...] = q_ref[...].T.reshape(2, dh, tm)

    n_sec = D // LANES
    return pl.pallas_call(
        body, name=name, grid=(H // 2, nt),
        in_specs=[cols(0), cols(n_sec), cols(2 * n_sec), pl.BlockSpec((tm, LANES), lambda p, i: (i, 0))],
        out_specs=[key_blk, key_blk, headsT, keyT_blk, keyT_blk],
        out_shape=[jax.ShapeDtypeStruct((H, M, LANES), BF16)] * 2 + [jax.ShapeDtypeStruct((H, dh, M), BF16)]
        + [jax.ShapeDtypeStruct((H, nt, dh + ATTN_EXTRA, tm), BF16)] * 2,
        compiler_params=_cparams(),
    )(qkv, qkv, qkv, c)


def _heads_merge(parts, M, tm, *, name):
    H = parts[0].shape[0]
    dh = LANES // 2
    blocked = [t.ndim == 4 for t in parts]
    cols, headsT = _pair_specs(M, dh, tm)

    def body(*refs):
        for src, dst, b in zip(refs[:len(parts)], refs[len(parts):], blocked):
            t = jnp.concatenate([src[e, 0, 0:dh, :] for e in range(2)], axis=0) if b else src[...].reshape(2 * dh, tm)
            dst[...] = t.astype(BF16).T

    in_specs = [pl.BlockSpec((2, 1, t.shape[2], tm), lambda p, i: (p, i, 0, 0)) if b else headsT
                for t, b in zip(parts, blocked)]
    return pl.pallas_call(
        body, name=name, grid=(H // 2, M // tm), in_specs=in_specs, out_specs=[cols(0)] * len(parts),
        out_shape=[jax.ShapeDtypeStruct((M, H * dh), BF16)] * len(parts), compiler_params=_cparams(),
    )(*parts)


def _attn_fwd(qT, cq, k_aug, vT_aug, *, name):
    H, dh, M = qT.shape
    tm = vT_aug.shape[-1]
    qT_blk, row_blk, key_blk, keyT_blk = _attn_specs(H, M, dh, tm)

    def body(qT_ref, cq_ref, k_ref, vT_ref, oT_ref, lse_ref):
        i = pl.program_id(1)
        ones = jnp.ones((3, tm), BF16)
        qa = _rows(qT_ref[0] * jnp.asarray(dh ** -0.5, BF16), [*_split3(cq_ref[0]), ones], LANES)

        def block(j, carry, masked):
            m, acc = carry
            keys = pl.ds(pl.multiple_of(j * tm, tm), tm)
            sT = jnp.dot(k_ref[0, keys, :], qa, preferred_element_type=F32)
            if masked:
                sT = jnp.where(_causal_mask(tm), sT, NEG_INF)
            m_new = jnp.maximum(m, jnp.max(sT, axis=0, keepdims=True))
            pT = jnp.exp(sT - m_new).astype(BF16)
            acc = jnp.exp(m - m_new) * acc + jnp.dot(vT_ref[0, j], pT, preferred_element_type=F32)
            return m_new, acc

        init = (jnp.full((1, tm), NEG_INF, F32), jnp.zeros((dh + ATTN_EXTRA, tm), F32))
        m, acc = block(i, _loop_unrolled(i, lambda j, c: block(j, c, False), init, 4), True)
        l = acc[dh:dh + 1, :]
        oT_ref[0] = (acc[0:dh, :] / l).astype(BF16)
        lse_ref[0] = m + jnp.log(l)

    return pl.pallas_call(
        body, name=name, grid=(H, M // tm),
        in_specs=[qT_blk, row_blk, key_blk, keyT_blk], out_specs=[qT_blk, row_blk],
        out_shape=[jax.ShapeDtypeStruct((H, dh, M), BF16), jax.ShapeDtypeStruct((H, 1, M), F32)],
        compiler_params=_cparams(),
    )(qT, cq, k_aug, vT_aug)


def _attn_bwd(qT, oT, doT, lse, cq, k_aug, v_aug, kT_aug, *, name):
    H, dh, M = qT.shape
    tm = kT_aug.shape[-1]
    qT_blk, row_blk, key_blk, keyT_blk = _attn_specs(H, M, dh, tm)
    dvT_blk = pl.BlockSpec((1, M // tm, dh, tm), lambda h, i: (h, 0, 0, 0))

    def body(qT_ref, oT_ref, doT_ref, lse_ref, cq_ref, k_ref, v_ref, kT_ref, dqT_ref, dcq_ref, dkT_ref, dvT_ref):
        i = pl.program_id(1)

        @pl.when(i == 0)
        def _():
            dkT_ref[...] = jnp.zeros_like(dkT_ref)
            dvT_ref[...] = jnp.zeros_like(dvT_ref)

        qsT = qT_ref[0] * jnp.asarray(dh ** -0.5, BF16)
        doT = doT_ref[0]
        delta = jnp.sum(doT.astype(F32) * oT_ref[0].astype(F32), axis=0, keepdims=True)
        ones = jnp.ones((3, tm), BF16)
        qa = _rows(qsT, [*_split3(cq_ref[0] - lse_ref[0]), ones], LANES)
        da = _rows(doT, _split3(-delta), LANES)
        q1 = _rows(qsT, [ones[0:1]], dh + ATTN_EXTRA)

        def block(j, dq, masked):
            keys = pl.ds(pl.multiple_of(j * tm, tm), tm)
            pT = jnp.exp(jnp.dot(k_ref[0, keys, :], qa, preferred_element_type=F32))
            if masked:
                pT = jnp.where(_causal_mask(tm), pT, 0.0)
            ds_b = (pT * jnp.dot(v_ref[0, keys, :], da, preferred_element_type=F32)).astype(BF16)
            dvT_ref[0, j] += _dot_nt(doT, pT.astype(BF16))
            dkT_ref[0, j] += _dot_nt(q1, ds_b)
            return dq + jnp.dot(kT_ref[0, j], ds_b, preferred_element_type=F32)

        init = jnp.zeros((dh + ATTN_EXTRA, tm), F32)
        dq = block(i, _loop_unrolled(i, lambda j, c: block(j, c, False), init, 4), True)
        dqT_ref[0] = (dq[0:dh, :] * dh ** -0.5).astype(BF16)
        dcq_ref[0] = dq[dh:dh + 1, :]

    return pl.pallas_call(
        body, name=name, grid=(H, M // tm),
        in_specs=[qT_blk, qT_blk, qT_blk, row_blk, row_blk, key_blk, key_blk, keyT_blk],
        out_specs=[qT_blk, row_blk, keyT_blk, dvT_blk],
        out_shape=[jax.ShapeDtypeStruct((H, dh, M), BF16), jax.ShapeDtypeStruct((H, 1, M), F32),
                   jax.ShapeDtypeStruct(kT_aug.shape, F32), jax.ShapeDtypeStruct((H, M // tm, dh, tm), F32)],
        compiler_params=_cparams(),
    )(qT, oT, doT, lse, cq, k_aug, v_aug, kT_aug)


def kernel(x, meta, pool_w, pool_scale, w_kv, w_f, b_f, w_q, w_o, ffn_w_in, ffn_conv_w, ffn_conv_b, ffn_w_out, ln_g, ln_b, loss_target, m_meta, m_pool_w, m_pool_scale, m_w_kv, m_w_f, m_b_f, m_w_q, m_w_o, m_ffn_w_in, m_ffn_conv_w, m_ffn_conv_b, m_ffn_w_out, m_ln_g, m_ln_b, v_meta, v_pool_w, v_pool_scale, v_w_kv, v_w_f, v_b_f, v_w_q, v_w_o, v_ffn_w_in, v_ffn_conv_w, v_ffn_conv_b, v_ffn_w_out, v_ln_g, v_ln_b):
    local = dict(meta=meta, pool_w=pool_w, pool_scale=pool_scale, w_kv=w_kv, w_f=w_f, b_f=b_f, w_q=w_q, w_o=w_o,
                 ffn_w_in=ffn_w_in, ffn_conv_w=ffn_conv_w, ffn_conv_b=ffn_conv_b, ffn_w_out=ffn_w_out,
                 ln_g=ln_g, ln_b=ln_b)
    mom1 = dict(meta=m_meta, pool_w=m_pool_w, pool_scale=m_pool_scale, w_kv=m_w_kv, w_f=m_w_f, b_f=m_b_f,
                w_q=m_w_q, w_o=m_w_o, ffn_w_in=m_ffn_w_in, ffn_conv_w=m_ffn_conv_w, ffn_conv_b=m_ffn_conv_b,
                ffn_w_out=m_ffn_w_out, ln_g=m_ln_g, ln_b=m_ln_b)
    mom2 = dict(meta=v_meta, pool_w=v_pool_w, pool_scale=v_pool_scale, w_kv=v_w_kv, w_f=v_w_f, b_f=v_b_f,
                w_q=v_w_q, w_o=v_w_o, ffn_w_in=v_ffn_w_in, ffn_conv_w=v_ffn_conv_w, ffn_conv_b=v_ffn_conv_b,
                ffn_w_out=v_ffn_w_out, ln_g=v_ln_g, ln_b=v_ln_b)
    axis_of = dict(PARAMS)

    S, D = x.shape[1], x.shape[2]
    H = b_f.shape[0]
    dh = D // H
    n_tok = N_META + S
    M = _round_up(n_tok, ROW_ALIGN)
    tm = _pick(M, ROW_TILES)
    nt = M // tm
    F2 = ffn_conv_b.shape[1]
    F = F2 // 2
    depth = ffn_conv_b.shape[0]

    small_sharded = [n for n in SMALL if axis_of[n] is not None]
    got = _all_gather([local[n].astype(BF16) for n in BIG] + [_pack([local[n] for n in small_sharded])],
                      "gather_weights")
    wb = {n: _from_blocks(blk, axis_of[n]) for n, blk in zip(BIG, got) if n != "ffn_w_in"}
    small_blocks = _unpack(got[-1], [local[n].shape for n in small_sharded], lead=(N_DEV,))
    wf32 = {n: _from_blocks(blk, axis_of[n]) for n, blk in zip(small_sharded, small_blocks)}

    pw = wb["pool_w"][0]
    wf_pad = jnp.pad(wf32["w_f"].astype(BF16), ((0, 0), (0, LANES - H)))
    w_qkv = jnp.concatenate([wb["w_kv"], wb["w_q"][0]], axis=1)
    w_att = jnp.concatenate([w_qkv, wf_pad], axis=1)
    wo = wb["w_o"][0]
    w_in = _w_in_interleaved(got[BIG.index("ffn_w_in")], name="w_in_interleave")
    w_out = wb["ffn_w_out"]
    conv_w = _glu_interleave(wf32["ffn_conv_w"])
    conv_b = _glu_interleave(ffn_conv_b)[:, None, :]
    scale = wf32["pool_scale"]
    g_ln, b_ln = wf32["ln_g"], wf32["ln_b"]
    ln = lambda i, j: (g_ln[i, j][None, :], b_ln[i, j][None, :])
    bias_f = jnp.pad(b_f, (0, LANES - H))[None, :]

    pad_rows = M - n_tok
    h0 = jnp.concatenate([wf32["meta"], x[0], jnp.zeros((pad_rows, D), F32)], axis=0)
    target = jnp.concatenate([jnp.zeros((N_META, D), F32), loss_target[0], jnp.zeros((pad_rows, D), F32)], axis=0)

    z1, h1, h1b, diff = _pool_fwd(h0, pw, scale, *ln(0, 0), name="pool_fwd")
    u0 = _mm(h1b, w_in[0], name="ffn0_up", out_dtype=F32)
    act0, c0 = _conv_glu_fwd(u0, conv_w[0], conv_b[0], name="ffn0_conv")
    z2, h2, h2b = _mm_ln(act0, w_out[0], h1, *ln(0, 1), name="ffn0_down_ln")

    qkv = _mm(h2b, w_qkv, name="attn_qkv", out_dtype=BF16)
    pre = _mm(h2b, wf_pad, name="attn_gate", out_dtype=F32)
    c = _forget_cumsum(pre, bias_f, name="attn_cumsum")
    k_aug, v_aug, qT_h, kT_aug, vT_aug = _attn_pack(qkv, c, H, name="attn_pack")
    cq = c[:, :H].T[:, None, :]
    oT_h, lse = _attn_fwd(qT_h, cq, k_aug, vT_aug, name="attn_fwd")
    oT = oT_h.reshape(D, M)
    z3, h3, h3b = _mm_ln(oT, wo, h2, *ln(1, 0), name="attn_out_ln", a_transposed=True)

    u1 = _mm(h3b, w_in[1], name="ffn1_up", out_dtype=F32)
    act1, c1 = _conv_glu_fwd(u1, conv_w[1], conv_b[1], name="ffn1_conv")
    z4, dy, loss_part = _mm_ln(act1, w_out[1], h3, *ln(1, 1), name="ffn1_down_loss", target=target, n_tok=n_tok)
    loss = lax.psum(loss_part[0, 0], ("x", "y", "c"))

    grads = {}

    def ffn_bwd(layer, dz, dz_b, u, c_conv, act, h_in_b, z_in, g_in, tag):
        dact = _mm(dz_b, w_out[layer], name=tag + "_dact", out_dtype=BF16, b_transposed=True)
        du, dcw, dcb = _conv_glu_bwd(u, c_conv, dact, conv_w[layer], name=tag + "_conv_bwd")
        dz_in = _mm_ln_bwd(du, w_in[layer], dz, z_in, g_in[None, :], name=tag + "_dh_ln_bwd")
        d_w_out = _mm_tn(act, dz_b, name=tag + "_dw_out")
        d_w_in = _w_in_grad_segments(_mm_tn(h_in_b, du, name=tag + "_dw_in"), name=tag + "_dw_in_segments")
        return dz_in, d_w_in, d_w_out, _glu_deinterleave(dcw), _glu_deinterleave(dcb)[0]

    dz4, dz4b, dg11, db11 = _ln_bwd(dy, z4, g_ln[1, 1][None, :], name="ln4_bwd")
    (dz3, dz3b, dg10, db10), dwin1, dwout1, dcw1, dcb1 = ffn_bwd(1, dz4, dz4b, u1, c1, act1, h3b, z3, g_ln[1, 0], "ffn1")

    doT_h = _mm(wo, dz3b, name="attn_do", out_dtype=BF16, b_transposed=True).reshape(H, dh, M)
    grads["w_o"] = _mm(oT, dz3b, name="attn_dw_o", out_dtype=F32)[None]
    dqT_h, dcq, dkT_a, dvT_h = _attn_bwd(qT_h, oT_h, doT_h, lse, cq, k_aug, v_aug, kT_aug, name="attn_bwd")
    dck = -dkT_a[:, :, dh, :].reshape(H, M)
    dc = jnp.pad((dcq[:, 0, :] + dck).T, ((0, 0), (0, LANES - H)))
    dpre, dbias = _forget_cumsum_bwd(dc, pre, bias_f, name="attn_cumsum_bwd")
    d_att = jnp.concatenate([*_heads_merge([dkT_a, dvT_h, dqT_h], M, tm, name="attn_dqkv_merge"), dpre], axis=1)
    dz2, dz2b, dg01, db01 = _mm_ln_bwd(d_att, w_att, dz3, z2, g_ln[0, 1][None, :], name="attn_dh_ln_bwd")
    d_w_att = _mm_tn(h2b, d_att, name="attn_dw_qkv")
    grads["w_kv"] = d_w_att[:, :2 * D]
    grads["w_q"] = d_w_att[:, 2 * D:3 * D][None]
    grads["w_f"] = d_w_att[:, 3 * D:3 * D + H]
    grads["b_f"] = dbias[0, :H]

    (dz1, _, dg00, db00), dwin0, dwout0, dcw0, dcb0 = ffn_bwd(0, dz2, dz2b, u0, c0, act0, h1b, z1, g_ln[0, 0], "ffn0")
    dh0, dpw, dscale = _pool_bwd(dz1, diff, pw, scale, name="pool_bwd")

    grads["meta"] = dh0[:N_META]
    grads["pool_w"] = dpw[None]
    grads["pool_scale"] = dscale
    seg_w_in = jnp.stack([dwin0, dwin1], axis=2)
    grads["ffn_w_out"] = jnp.stack([dwout0, dwout1])
    grads["ffn_conv_w"] = jnp.stack([dcw0, dcw1])
    grads["ffn_conv_b"] = jnp.stack([dcb0, dcb1])
    grads["ln_g"] = jnp.stack([jnp.stack([dg00[0], dg01[0]]), jnp.stack([dg10[0], dg11[0]])])
    grads["ln_b"] = jnp.stack([jnp.stack([db00[0], db01[0]]), jnp.stack([db10[0], db11[0]])])
    grad_x = dh0[N_META:n_tok][None]

    blocks = {n: _to_blocks(grads[n], axis_of[n]) for n, _ in PARAMS if n != "ffn_w_in"}
    me = _coords()
    core, chip = (jnp.reshape(v, (1,)).astype(jnp.int32) for v in (me[2], 2 * me[0] + me[1]))
    by_core = lambda t: jnp.swapaxes(t.reshape((N_CHIPS, 2) + t.shape[1:]), 0, 1).astype(BF16)
    segs = [seg_w_in if n == "ffn_w_in" else by_core(blocks[n]) for n in BIG]
    pair = _exchange(segs, "grad_pair_exchange", (1,), lambda pos: pos[2])
    rows_of = lambda t, lead: t.reshape(lead + (-1, t.shape[-1]))
    partial = [_pair_sum(rows_of(s, (2,)), core, rows_of(p[0], ()), name="grad_pair_sum_" + n).reshape(s.shape[1:])
               for n, s, p in zip(BIG, segs, pair)]
    recv = _exchange(partial, "grad_chip_exchange", (4, 2, 6), lambda pos: 2 * pos[0] + pos[1])
    recv_small = _exchange([_pack([blocks[n] for n in SMALL], lead=(N_DEV,))], "grad_small_exchange",
                           tuple(range(1, N_DEV)), _index, by_sender=True)[0]
    results = {}
    for n, own, r in zip(BIG, partial, recv):
        shape = local[n].shape
        as2d = lambda t: t.reshape(-1, shape[-1])
        outs = _adamw(rows_of(own, (N_CHIPS,)), chip, rows_of(r, (N_CHIPS - 1,)), as2d(local[n]), as2d(mom1[n]),
                      as2d(mom2[n]), name="adamw_" + n)
        results[n] = [o_.reshape(shape) for o_ in outs]
    outs = _adamw(recv_small, jnp.zeros((1,), jnp.int32), recv_small[1:],
                  *[_pack([d[n] for n in SMALL]) for d in (local, mom1, mom2)], name="adamw_small")
    small_out = [_unpack(o_, [local[n].shape for n in SMALL]) for o_ in outs]
    for i, n in enumerate(SMALL):
        results[n] = [small_out[k][i] for k in range(4)]
    return (loss, grad_x, *[results[n][k] for k in range(4) for n, _ in PARAMS])
```

```python
import jax
import jax.numpy as jnp
from jax import lax
from jax.experimental import pallas as pl
from jax.experimental.pallas import tpu as pltpu

F32, BF16 = jnp.float32, jnp.bfloat16
MESH = pl.DeviceIdType.MESH

N_DEV = 8
N_CHIPS = 4
N_META = 16
POOL_WINDOWS = (2, 4, 8, 16)
POOL_HALO = 16
CONV_HALO = 8
CONV_STRIP = 16
ALPHA = 4.0 ** 0.25
LN_EPS = 1e-5
NEG_INF = -1e30
ADAM_LR, ADAM_B1, ADAM_B2, ADAM_EPS, ADAM_WD, ADAM_STEP = 0.001, 0.9, 0.999, 1e-08, 0.01, 10

LANES = 128
ROW_ALIGN = 128
ROW_TILES = (640, 512, 128)
PACK_COLS = 1024
ADAM_TILE_BYTES = 4 << 20
MM_TN_WINDOW_BYTES = 28 << 20
GLU_CHUNK = 256
ATTN_EXTRA = 16
VMEM_LIMIT = 56 * 1024 * 1024

PARAMS = (("meta", 1), ("pool_w", 2), ("pool_scale", 1), ("w_kv", 1), ("w_f", 0), ("b_f", None),
          ("w_q", 1), ("w_o", 1), ("ffn_w_in", 2), ("ffn_conv_w", 2), ("ffn_conv_b", None),
          ("ffn_w_out", 1), ("ln_g", 2), ("ln_b", 2))
BIG = ("pool_w", "w_kv", "w_q", "w_o", "ffn_w_in", "ffn_w_out")
SMALL = ("meta", "pool_scale", "w_f", "b_f", "ffn_conv_w", "ffn_conv_b", "ln_g", "ln_b")


def _cparams(**kw):
    return pltpu.CompilerParams(vmem_limit_bytes=VMEM_LIMIT, **kw)


def _pick(n, cands):
    for c in cands:
        if n % c == 0:
            return c
    return n


def _round_up(n, m):
    return (n + m - 1) // m * m


def _pack(pieces, lead=()):
    flat = []
    for p in pieces:
        v = p.reshape(lead + (-1,))
        flat.append(jnp.pad(v, [(0, 0)] * len(lead) + [(0, _round_up(v.shape[-1], PACK_COLS) - v.shape[-1])]))
    v = jnp.concatenate(flat, axis=-1)
    rows = _round_up(v.shape[-1] // PACK_COLS, 8)
    v = jnp.pad(v, [(0, 0)] * len(lead) + [(0, rows * PACK_COLS - v.shape[-1])])
    return v.reshape(lead + (rows, PACK_COLS))


def _unpack(buf, shapes, lead=()):
    flat = buf.reshape(lead + (-1,))
    out, off = [], 0
    for s in shapes:
        n = 1
        for d in s:
            n *= d
        out.append(flat[..., off:off + n].reshape(lead + tuple(s)))
        off += _round_up(n, PACK_COLS)
    return out


def _to_blocks(full, axis):
    if axis is None:
        return jnp.broadcast_to(full[None], (N_DEV,) + full.shape)
    s = full.shape
    x = full.reshape(s[:axis] + (N_DEV, s[axis] // N_DEV) + s[axis + 1:])
    return jnp.moveaxis(x, axis, 0)


def _from_blocks(blocks, axis):
    x = jnp.moveaxis(blocks, 0, axis)
    s = x.shape
    return x.reshape(s[:axis] + (s[axis] * s[axis + 1],) + s[axis + 2:])


def _coords():
    return lax.axis_index("x"), lax.axis_index("y"), lax.axis_index("c")


def _flip(pos, mask):
    x, y, c = pos
    return (1 - x if mask & 4 else x, 1 - y if mask & 2 else y, 1 - c if mask & 1 else c)


def _index(pos):
    x, y, c = pos
    return 4 * x + 2 * y + c


def _comm_call(body, name, arrays, out_shapes):
    n = len(arrays)
    hbm = pl.BlockSpec(memory_space=pl.ANY)
    return pl.pallas_call(
        body, name=name, out_shape=out_shapes, in_specs=[hbm] * n, out_specs=[hbm] * n,
        scratch_shapes=[pltpu.SemaphoreType.DMA((7 * n,)), pltpu.SemaphoreType.DMA((7 * n,)),
                        pltpu.SemaphoreType.DMA((n,))],
    )(*arrays)


def _all_gather(blocks, name):
    chip_masks = (4, 2, 6)
    n = len(blocks)

    def body(*refs):
        x_refs, out_refs = refs[:n], refs[n:2 * n]
        send_sems, recv_sems, local_sems = refs[2 * n:]
        me = _coords()
        sibling = _flip(me, 1)

        def copy(a, k, owner, to, from_input=False):
            slot = out_refs[a].at[_index(owner)]
            return pltpu.make_async_remote_copy(
                src_ref=x_refs[a] if from_input else slot, dst_ref=slot,
                send_sem=send_sems.at[7 * a + k], recv_sem=recv_sems.at[7 * a + k],
                device_id=to, device_id_type=MESH)

        mine = [pltpu.make_async_copy(x_refs[a], out_refs[a].at[_index(me)], local_sems.at[a]) for a in range(n)]
        first = [copy(a, 0, me, sibling, True) for a in range(n)]
        first += [copy(a, 1 + j, me, _flip(me, m), True) for j, m in enumerate(chip_masks) for a in range(n)]
        for cp in mine + first:
            cp.start()
        passed = []
        for j, m in enumerate(chip_masks):
            for a in range(n):
                copy(a, 1 + j, _flip(me, m), me).wait_recv()
                passed.append(copy(a, 4 + j, _flip(me, m), sibling))
                passed[-1].start()
        for a in range(n):
            copy(a, 0, sibling, me).wait_recv()
            for j, m in enumerate(chip_masks):
                copy(a, 4 + j, _flip(sibling, m), me).wait_recv()
        for cp in first + passed:
            cp.wait_send()
        for cp in mine:
            cp.wait()

    return _comm_call(body, name, blocks, [jax.ShapeDtypeStruct((N_DEV,) + b.shape, b.dtype) for b in blocks])


def _exchange(segs, name, masks, slot_of, by_sender=False):
    n = len(segs)

    def body(*refs):
        seg_refs, out_refs = refs[:n], refs[n:2 * n]
        send_sems, recv_sems, local_sems = refs[2 * n:]
        me = _coords()

        def copy(a, k, sender):
            to = _flip(sender, masks[k])
            return pltpu.make_async_remote_copy(
                src_ref=seg_refs[a].at[slot_of(to)], dst_ref=out_refs[a].at[slot_of(sender) if by_sender else k],
                send_sem=send_sems.at[7 * a + k], recv_sem=recv_sems.at[7 * a + k],
                device_id=to, device_id_type=MESH)

        mine = [pltpu.make_async_copy(seg_refs[a].at[slot_of(me)], out_refs[a].at[slot_of(me)], local_sems.at[a])
                for a in range(n)] if by_sender else []
        sends = [copy(a, k, me) for k in range(len(masks)) for a in range(n)]
        for cp in mine + sends:
            cp.start()
        for k, mask in enumerate(masks):
            for a in range(n):
                copy(a, k, _flip(me, mask)).wait_recv()
        for cp in sends:
            cp.wait_send()
        for cp in mine:
            cp.wait()

    slots = lambda s: s.shape[0] if by_sender else len(masks)
    return _comm_call(body, name, segs, [jax.ShapeDtypeStruct((slots(s),) + s.shape[1:], s.dtype) for s in segs])


def _own_slot(tr, cols):
    return pl.BlockSpec((1, tr, cols), lambda i, slot: (slot[0], i, 0))


def _pair_sum(seg, slot, recv, name):
    _, rows, cols = seg.shape
    tr = max(t for t in range(16, rows + 1, 16) if rows % t == 0 and 2 * t * cols * 2 <= ADAM_TILE_BYTES)

    def body(slot_ref, own_ref, recv_ref, o_ref):
        o_ref[...] = (own_ref[0].astype(F32) + recv_ref[...].astype(F32)).astype(BF16)

    tile = pl.BlockSpec((tr, cols), lambda i, slot: (i, 0))
    return pl.pallas_call(
        body, name=name,
        grid_spec=pltpu.PrefetchScalarGridSpec(
            num_scalar_prefetch=1, grid=(rows // tr,), in_specs=[_own_slot(tr, cols), tile], out_specs=tile),
        out_shape=jax.ShapeDtypeStruct((rows, cols), BF16),
        compiler_params=_cparams(),
    )(slot, seg, recv)


def _adamw(seg, slot, recv, w, m, v, name):
    rows, cols = w.shape
    n_slots = recv.shape[0]
    sublanes = 8 * (4 // recv.dtype.itemsize)
    tr = max(t for t in range(sublanes, rows + 1, sublanes)
             if rows % t == 0 and N_DEV * t * cols * 4 <= ADAM_TILE_BYTES)
    c1 = 1.0 - ADAM_B1 ** ADAM_STEP
    c2 = 1.0 - ADAM_B2 ** ADAM_STEP

    def body(slot_ref, own_ref, r_ref, w_ref, m_ref, v_ref, g_out, d_out, m_out, v_out):
        g = own_ref[0].astype(F32)
        for s in range(n_slots):
            g = g + r_ref[s].astype(F32)
        m_new = ADAM_B1 * m_ref[...] + (1.0 - ADAM_B1) * g
        v_new = ADAM_B2 * v_ref[...] + (1.0 - ADAM_B2) * (g * g)
        m_hat = m_new / c1
        v_hat = v_new / c2
        g_out[...] = g
        d_out[...] = -ADAM_LR * (m_hat / (jnp.sqrt(v_hat) + ADAM_EPS) + ADAM_WD * w_ref[...])
        m_out[...] = m_new
        v_out[...] = v_new

    tile = pl.BlockSpec((tr, cols), lambda i, slot: (i, 0))
    return pl.pallas_call(
        body, name=name,
        grid_spec=pltpu.PrefetchScalarGridSpec(
            num_scalar_prefetch=1, grid=(rows // tr,),
            in_specs=[_own_slot(tr, cols), pl.BlockSpec((n_slots, tr, cols), lambda i, slot: (0, i, 0)),
                      tile, tile, tile],
            out_specs=[tile] * 4),
        out_shape=[jax.ShapeDtypeStruct(w.shape, F32)] * 4,
        compiler_params=_cparams(),
    )(slot, seg, recv, w, m, v)


def _mm(a, b, *, name, out_dtype, b_transposed=False):
    M, K = a.shape
    N = b.shape[0] if b_transposed else b.shape[1]
    tm = _pick(M, ROW_TILES)
    tn = _pick(N, (2816, 1024, 640, 512, 256, 128))
    tk = K if K <= 3200 else _pick(K, (2816, 1024, 640, 512, 256, 128))
    nk = K // tk

    def body(*refs):
        a_ref, b_ref, o_ref = refs[:3]
        acc_ref = refs[-1] if nk > 1 else None
        if b_transposed:
            prod = _dot_nt(a_ref[...], b_ref[...])
        else:
            prod = jnp.dot(a_ref[...], b_ref[...], preferred_element_type=F32)
        if nk == 1:
            o_ref[...] = prod.astype(out_dtype)
        else:
            k = pl.program_id(2)

            @pl.when(k == 0)
            def _():
                acc_ref[...] = prod

            @pl.when(k > 0)
            def _():
                acc_ref[...] += prod

            @pl.when(k == nk - 1)
            def _():
                o_ref[...] = acc_ref[...].astype(out_dtype)

    b_spec = (pl.BlockSpec((tn, tk), lambda i, j, k: (j, k)) if b_transposed
              else pl.BlockSpec((tk, tn), lambda i, j, k: (k, j)))
    return pl.pallas_call(
        body, name=name, grid=(M // tm, N // tn, nk),
        in_specs=[pl.BlockSpec((tm, tk), lambda i, j, k: (i, k)), b_spec],
        out_specs=pl.BlockSpec((tm, tn), lambda i, j, k: (i, j)),
        out_shape=jax.ShapeDtypeStruct((M, N), out_dtype),
        scratch_shapes=[pltpu.VMEM((tm, tn), F32)] if nk > 1 else [],
        compiler_params=_cparams(),
    )(a, b)


def _mm_tn(a, b, *, name):
    T, M = a.shape
    N = b.shape[1]
    tm = _pick(M, (1408, 1024, 512, 256, 128))
    tn = _pick(N, (1024, 640, 512, 256, 128))
    tt = _pick(T, [t for t in (4160, 1664, 640) if 2 * 2 * t * (tm + tn) <= MM_TN_WINDOW_BYTES] + [128])

    def body(a_ref, b_ref, o_ref):
        prod = lax.dot_general(a_ref[...], b_ref[...], (((0,), (0,)), ((), ())), preferred_element_type=F32)

        @pl.when(pl.program_id(2) == 0)
        def _():
            o_ref[...] = prod

        @pl.when(pl.program_id(2) > 0)
        def _():
            o_ref[...] += prod

    return pl.pallas_call(
        body, name=name, grid=(M // tm, N // tn, T // tt),
        in_specs=[pl.BlockSpec((tt, tm), lambda i, j, k: (k, i)), pl.BlockSpec((tt, tn), lambda i, j, k: (k, j))],
        out_specs=pl.BlockSpec((tm, tn), lambda i, j, k: (i, j)),
        out_shape=jax.ShapeDtypeStruct((M, N), F32),
        compiler_params=_cparams(),
    )(a, b)


def _layer_norm(z, g, b):
    mu = jnp.mean(z, axis=-1, keepdims=True)
    xc = z - mu
    var = jnp.mean(xc * xc, axis=-1, keepdims=True)
    return xc * lax.rsqrt(var + LN_EPS) * g + b


def _mm_ln(a, w, res, g, b, *, name, target=None, n_tok=None, a_transposed=False):
    K, M = a.shape if a_transposed else a.shape[::-1]
    D = w.shape[1]
    tm = _pick(M, ROW_TILES)
    final = target is not None

    def body(*refs):
        if final:
            a_ref, w_ref, r_ref, g_ref, b_ref, t_ref, dz_ref, dzb_ref, loss_ref, dg_ref, db_ref = refs
        else:
            a_ref, w_ref, r_ref, g_ref, b_ref, z_ref, h_ref, hb_ref = refs
        contract = (((0,), (0,)), ((), ())) if a_transposed else (((1,), (0,)), ((), ()))
        z = ALPHA * r_ref[...] + lax.dot_general(a_ref[...], w_ref[...], contract, preferred_element_type=F32)
        h = _layer_norm(z, g_ref[...], b_ref[...])
        if not final:
            z_ref[...] = z
            h_ref[...] = h
            hb_ref[...] = h.astype(BF16)
            return
        i = pl.program_id(0)
        row = i * tm + lax.broadcasted_iota(jnp.int32, (tm, 1), 0)
        valid = (row >= N_META) & (row < n_tok)
        err = jnp.where(valid, h - t_ref[...], 0.0)
        dy = err / D
        dz, xhat = _ln_bwd_rows(dy, z, g_ref[...])
        dz_ref[...] = dz
        dzb_ref[...] = dz.astype(BF16)

        @pl.when(i == 0)
        def _():
            loss_ref[...] = jnp.zeros_like(loss_ref)
            dg_ref[...] = jnp.zeros_like(dg_ref)
            db_ref[...] = jnp.zeros_like(db_ref)

        loss_ref[...] += 0.5 * jnp.sum(jnp.sum(err * err, axis=1, keepdims=True) / D, axis=0, keepdims=True)
        dg_ref[...] += jnp.sum(dy * xhat, axis=0, keepdims=True)
        db_ref[...] += jnp.sum(dy, axis=0, keepdims=True)

    row_blk = lambda cols: pl.BlockSpec((tm, cols), lambda i: (i, 0))
    vec = pl.BlockSpec((1, D), lambda i: (0, 0))
    a_blk = pl.BlockSpec((K, tm), lambda i: (0, i)) if a_transposed else row_blk(K)
    in_specs = [a_blk, pl.BlockSpec((K, D), lambda i: (0, 0)), row_blk(D), vec, vec]
    args = [a, w, res, g, b]
    if final:
        in_specs.append(row_blk(D))
        args.append(target)
        out_specs = [row_blk(D), row_blk(D), pl.BlockSpec((8, LANES), lambda i: (0, 0)), vec, vec]
        out_shape = [jax.ShapeDtypeStruct((M, D), F32), jax.ShapeDtypeStruct((M, D), BF16),
                     jax.ShapeDtypeStruct((8, LANES), F32), jax.ShapeDtypeStruct((1, D), F32),
                     jax.ShapeDtypeStruct((1, D), F32)]
    else:
        out_specs = [row_blk(D)] * 3
        out_shape = [jax.ShapeDtypeStruct((M, D), F32)] * 2 + [jax.ShapeDtypeStruct((M, D), BF16)]
    return pl.pallas_call(
        body, name=name, grid=(M // tm,), in_specs=in_specs, out_specs=out_specs, out_shape=out_shape,
        compiler_params=_cparams(),
    )(*args)


def _ln_bwd_rows(dh_v, z_v, g_v):
    mu = jnp.mean(z_v, axis=-1, keepdims=True)
    xc = z_v - mu
    rstd = lax.rsqrt(jnp.mean(xc * xc, axis=-1, keepdims=True) + LN_EPS)
    xhat = xc * rstd
    dxhat = dh_v * g_v
    dz = rstd * (dxhat - jnp.mean(dxhat, axis=-1, keepdims=True)
                 - xhat * jnp.mean(dxhat * xhat, axis=-1, keepdims=True))
    return dz, xhat


def _mm_ln_bwd(a, b, dz_next, z, g, *, name):
    M, K = a.shape
    D = b.shape[0]
    tm = _pick(M, ROW_TILES)
    tk = K if K <= 3200 else _pick(K, (2816, 1024, 640, 512, 256, 128))
    nk = K // tk

    def body(a_ref, b_ref, nx_ref, z_ref, g_ref, dz_ref, dzb_ref, dg_ref, db_ref, *scratch):
        i, k = pl.program_id(0), pl.program_id(1)
        prod = _dot_nt(a_ref[...], b_ref[...])

        @pl.when((i == 0) & (k == 0))
        def _():
            dg_ref[...] = jnp.zeros_like(dg_ref)
            db_ref[...] = jnp.zeros_like(db_ref)

        def finish(acc):
            dh_v = acc + ALPHA * nx_ref[...]
            dz, xhat = _ln_bwd_rows(dh_v, z_ref[...], g_ref[...])
            dz_ref[...] = dz
            dzb_ref[...] = dz.astype(BF16)
            dg_ref[...] += jnp.sum(dh_v * xhat, axis=0, keepdims=True)
            db_ref[...] += jnp.sum(dh_v, axis=0, keepdims=True)

        if nk == 1:
            finish(prod)
        else:
            acc_ref, = scratch

            @pl.when(k == 0)
            def _():
                acc_ref[...] = prod

            @pl.when(k > 0)
            def _():
                acc_ref[...] += prod

            @pl.when(k == nk - 1)
            def _():
                finish(acc_ref[...])

    row_blk = pl.BlockSpec((tm, D), lambda i, k: (i, 0))
    vec = pl.BlockSpec((1, D), lambda i, k: (0, 0))
    return pl.pallas_call(
        body, name=name, grid=(M // tm, nk),
        in_specs=[pl.BlockSpec((tm, tk), lambda i, k: (i, k)), pl.BlockSpec((D, tk), lambda i, k: (0, k)),
                  row_blk, row_blk, vec],
        out_specs=[row_blk, row_blk, vec, vec],
        out_shape=[jax.ShapeDtypeStruct((M, D), F32), jax.ShapeDtypeStruct((M, D), BF16),
                   jax.ShapeDtypeStruct((1, D), F32), jax.ShapeDtypeStruct((1, D), F32)],
        scratch_shapes=[pltpu.VMEM((tm, D), F32)] if nk > 1 else [],
        compiler_params=_cparams(),
    )(a, b, dz_next, z, g)


def _pool_fwd(h0, pw, scale, g, b, *, name):
    M, D = h0.shape
    n_groups, G, _ = pw.shape
    tm = _pick(M, ROW_TILES)

    def body(x_ref, halo_ref, pw_ref, sc_ref, g_ref, b_ref, z_ref, h_ref, hb_ref, diff_ref, ext_ref, mix_ref):
        i = pl.program_id(0)
        x = x_ref[...]
        ext_ref[0:POOL_HALO, :] = jnp.where(i == 0, 0.0, halo_ref[...])
        ext_ref[POOL_HALO:, :] = x
        tok = i * tm + lax.broadcasted_iota(jnp.int32, (tm, 1), 0)
        for gi, win in enumerate(POOL_WINDOWS):
            cols = slice(gi * G, (gi + 1) * G)
            xs = x[:, cols]
            s = xs
            for k in range(1, win):
                s = s + ext_ref[pl.ds(POOL_HALO - k, tm), cols]
            count = jnp.minimum(tok + 1, win).astype(F32)
            d = (s / count - xs).astype(BF16)
            diff_ref[:, cols] = d
            mix_ref[:, cols] = jnp.dot(d, pw_ref[gi], preferred_element_type=F32)
        z = ALPHA * x + mix_ref[...] * sc_ref[...]
        z_ref[...] = z
        h = _layer_norm(z, g_ref[...], b_ref[...])
        h_ref[...] = h
        hb_ref[...] = h.astype(BF16)

    row_blk = pl.BlockSpec((tm, D), lambda i: (i, 0))
    vec = pl.BlockSpec((1, D), lambda i: (0, 0))
    halo = pl.BlockSpec((POOL_HALO, D), lambda i: (jnp.maximum(i * (tm // POOL_HALO) - 1, 0), 0))
    return pl.pallas_call(
        body, name=name, grid=(M // tm,),
        in_specs=[row_blk, halo, pl.BlockSpec((n_groups, G, G), lambda i: (0, 0, 0)), vec, vec, vec],
        out_specs=[row_blk] * 4,
        out_shape=[jax.ShapeDtypeStruct((M, D), F32)] * 2 + [jax.ShapeDtypeStruct((M, D), BF16)] * 2,
        scratch_shapes=[pltpu.VMEM((tm + POOL_HALO, D), F32), pltpu.VMEM((tm, D), F32)],
        compiler_params=_cparams(),
    )(h0, h0, pw, scale, g, b)


def _pool_bwd(dz, diff, pw, scale, *, name):
    M, D = dz.shape
    n_groups, G, _ = pw.shape
    tm = _pick(M, ROW_TILES)
    nt = M // tm

    def body(dz_ref, halo_ref, diff_ref, pw_ref, sc_ref, dh_ref, dpw_ref, dsc_ref, ext_ref, q_ref):
        i = pl.program_id(0)
        dz_v = dz_ref[...]
        ext_ref[0:tm, :] = dz_v
        ext_ref[tm:, :] = jnp.where(i == nt - 1, 0.0, halo_ref[...])
        tok = i * tm + lax.broadcasted_iota(jnp.int32, (tm + POOL_HALO, 1), 0)

        @pl.when(i == 0)
        def _():
            dpw_ref[...] = jnp.zeros_like(dpw_ref)
            dsc_ref[...] = jnp.zeros_like(dsc_ref)

        for gi, win in enumerate(POOL_WINDOWS):
            cols = slice(gi * G, (gi + 1) * G)
            dmix = (ext_ref[:, cols] * sc_ref[:, cols]).astype(BF16)
            ddiff = _dot_nt(dmix, pw_ref[gi])
            count = jnp.minimum(tok + 1, win).astype(F32)
            q_ref[:, cols] = ddiff / count
            acc = -ddiff[0:tm]
            for k in range(win):
                acc = acc + q_ref[pl.ds(k, tm), cols]
            dh_ref[:, cols] = ALPHA * dz_v[:, cols] + acc
            d = diff_ref[:, cols]
            dpw_ref[gi] += lax.dot_general(d, dmix[0:tm], (((0,), (0,)), ((), ())), preferred_element_type=F32)
            mixed = jnp.dot(d, pw_ref[gi], preferred_element_type=F32)
            dsc_ref[:, cols] += jnp.sum(dz_v[:, cols] * mixed, axis=0, keepdims=True)

    row_blk = pl.BlockSpec((tm, D), lambda i: (i, 0))
    vec = pl.BlockSpec((1, D), lambda i: (0, 0))
    per_tile = tm // POOL_HALO
    halo = pl.BlockSpec((POOL_HALO, D), lambda i: (jnp.minimum((i + 1) * per_tile, nt * per_tile - 1), 0))
    wblk = pl.BlockSpec((n_groups, G, G), lambda i: (0, 0, 0))
    return pl.pallas_call(
        body, name=name, grid=(nt,),
        in_specs=[row_blk, halo, row_blk, wblk, vec],
        out_specs=[row_blk, wblk, vec],
        out_shape=[jax.ShapeDtypeStruct((M, D), F32), jax.ShapeDtypeStruct((n_groups, G, G), F32),
                   jax.ShapeDtypeStruct((1, D), F32)],
        scratch_shapes=[pltpu.VMEM((tm + POOL_HALO, D), F32), pltpu.VMEM((tm + POOL_HALO, D), F32)],
        compiler_params=_cparams(),
    )(dz, dz, diff, pw, scale)


def _glu_interleave(x):
    s = x.shape
    n = s[-1] // (2 * GLU_CHUNK)
    return jnp.swapaxes(x.reshape(s[:-1] + (2, n, GLU_CHUNK)), -3, -2).reshape(s)


def _glu_deinterleave(x):
    s = x.shape
    n = s[-1] // (2 * GLU_CHUNK)
    return jnp.swapaxes(x.reshape(s[:-1] + (n, 2, GLU_CHUNK)), -3, -2).reshape(s)


W_PACK_ROWS = 256
W_PACK_PIECE = 64


def _interleaved_col(col, F):
    half, within = divmod(col, F)
    chunk, off = divmod(within, GLU_CHUNK)
    return chunk * 2 * GLU_CHUNK + half * GLU_CHUNK + off


def _w_in_grad_segments(dw, *, name):
    D, F2 = dw.shape
    width = F2 // N_DEV
    rows = _pick(D, (W_PACK_ROWS, 128))
    assert width % W_PACK_PIECE == 0

    def body(x_ref, o_ref):
        for j in range(N_DEV):
            for q in range(0, width, W_PACK_PIECE):
                at = _interleaved_col(width * j + q, F2 // 2)
                o_ref[j % 2, j // 2, :, q:q + W_PACK_PIECE] = x_ref[:, at:at + W_PACK_PIECE].astype(BF16)

    return pl.pallas_call(
        body, name=name, grid=(D // rows,),
        in_specs=[pl.BlockSpec((rows, F2), lambda r: (r, 0))],
        out_specs=pl.BlockSpec((2, N_CHIPS, rows, width), lambda r: (0, 0, r, 0)),
        out_shape=jax.ShapeDtypeStruct((2, N_CHIPS, D, width), BF16),
        compiler_params=_cparams(),
    )(dw)


def _w_in_interleaved(gathered, *, name):
    n_dev, depth, D, width = gathered.shape
    F = n_dev * width // 2
    rows = _pick(D, (W_PACK_ROWS, 128))
    assert width % W_PACK_PIECE == 0 and GLU_CHUNK % W_PACK_PIECE == 0 and F % GLU_CHUNK == 0
    interleaved = lambda col: _interleaved_col(col, F)

    def body(x_ref, o_ref):
        for j in range(n_dev):
            for q in range(0, width, W_PACK_PIECE):
                to = interleaved(width * j + q)
                o_ref[0, :, to:to + W_PACK_PIECE] = x_ref[j, 0, :, q:q + W_PACK_PIECE]

    return pl.pallas_call(
        body, name=name, grid=(depth, D // rows),
        in_specs=[pl.BlockSpec((n_dev, 1, rows, width), lambda l, r: (0, l, r, 0))],
        out_specs=pl.BlockSpec((1, rows, n_dev * width), lambda l, r: (l, r, 0)),
        out_shape=jax.ShapeDtypeStruct((depth, D, n_dev * width), gathered.dtype),
        compiler_params=_cparams(),
    )(gathered)


def _taps(u_ref, head_ref, r0):
    src, base = (head_ref, CONV_HALO) if r0 == 0 else (u_ref, r0)
    return tuple(src[pl.ds(base - k, CONV_STRIP), :] for k in range(3))


def _conv_glu_fwd(u, cw, cb, *, name):
    M, F2 = u.shape
    tm = _pick(M, ROW_TILES)
    tc = 2 * GLU_CHUNK

    def body(u_ref, halo_ref, w_ref, b_ref, o_ref, c_ref, head_ref):
        i = pl.program_id(0)
        head_ref[0:CONV_HALO, :] = jnp.where(i == 0, 0.0, halo_ref[...])
        head_ref[CONV_HALO:, :] = u_ref[0:CONV_STRIP, :]
        w0, w1, w2, b = w_ref[0:1, :], w_ref[1:2, :], w_ref[2:3, :], b_ref[...]
        for r0 in range(0, tm, CONV_STRIP):
            u0, u1, u2 = _taps(u_ref, head_ref, r0)
            c = b + w0 * u2 + w1 * u1 + w2 * u0
            c_ref[pl.ds(r0, CONV_STRIP), :] = c.astype(BF16)
            a, g = c[:, :GLU_CHUNK], c[:, GLU_CHUNK:]
            o_ref[pl.ds(r0, CONV_STRIP), :] = (a * jax.nn.sigmoid(a) * g).astype(BF16)

    per_tile = tm // CONV_HALO
    return pl.pallas_call(
        body, name=name, grid=(M // tm, F2 // tc),
        in_specs=[pl.BlockSpec((tm, tc), lambda i, j: (i, j)),
                  pl.BlockSpec((CONV_HALO, tc), lambda i, j: (jnp.maximum(i * per_tile - 1, 0), j)),
                  pl.BlockSpec((3, tc), lambda i, j: (0, j)), pl.BlockSpec((1, tc), lambda i, j: (0, j))],
        out_specs=[pl.BlockSpec((tm, GLU_CHUNK), lambda i, j: (i, j)), pl.BlockSpec((tm, tc), lambda i, j: (i, j))],
        out_shape=[jax.ShapeDtypeStruct((M, F2 // 2), BF16), jax.ShapeDtypeStruct((M, F2), BF16)],
        scratch_shapes=[pltpu.VMEM((CONV_HALO + CONV_STRIP, tc), F32)],
        compiler_params=_cparams(),
    )(u, u, cw, cb)


def _conv_glu_bwd(u, c, dact, cw, *, name):
    M, F2 = u.shape
    tm = _pick(M, ROW_TILES)
    nt = M // tm
    tc = 2 * GLU_CHUNK

    def body(u_ref, c_ref, da_ref, w_ref, du_ref, dw_ref, db_ref, dcx_ref, carry_ref):
        i = pl.program_id(1)
        w0, w1, w2 = w_ref[0:1, :], w_ref[1:2, :], w_ref[2:3, :]

        @pl.when(i == 0)
        def _():
            dw_ref[...] = jnp.zeros_like(dw_ref)
            db_ref[...] = jnp.zeros_like(db_ref)
            carry_ref[...] = jnp.zeros_like(carry_ref)

        def fold(t):
            return sum(t[r:r + 8] for r in range(0, CONV_STRIP, 8))

        dcx_ref[tm:, :] = carry_ref[...]
        s_b = s_0 = s_1 = s_2 = jnp.zeros((8, tc), F32)
        for r0 in reversed(range(0, tm, CONV_STRIP)):
            rows = pl.ds(r0, CONV_STRIP)
            c_v = c_ref[rows, :].astype(F32)
            a, g = c_v[:, :GLU_CHUNK], c_v[:, GLU_CHUNK:]
            sig = jax.nn.sigmoid(a)
            dact_v = da_ref[rows, :].astype(F32)
            d_a = dact_v * g * (sig * (1.0 + a * (1.0 - sig)))
            d_g = dact_v * (a * sig)
            dc = jnp.concatenate([d_a, d_g], axis=1)
            dcx_ref[rows, :] = dc
            dc1, dc2 = dcx_ref[pl.ds(r0 + 1, CONV_STRIP), :], dcx_ref[pl.ds(r0 + 2, CONV_STRIP), :]
            du_ref[rows, :] = (w2 * dc + w1 * dc1 + w0 * dc2).astype(BF16)
            u_v = u_ref[rows, :]
            s_b, s_0, s_1, s_2 = s_b + fold(dc), s_0 + fold(dc2 * u_v), s_1 + fold(dc1 * u_v), s_2 + fold(dc * u_v)
        carry_ref[...] = dcx_ref[0:CONV_HALO, :]
        db_ref[...] += jnp.sum(s_b, axis=0, keepdims=True)
        dw_ref[0:1, :] += jnp.sum(s_0, axis=0, keepdims=True)
        dw_ref[1:2, :] += jnp.sum(s_1, axis=0, keepdims=True)
        dw_ref[2:3, :] += jnp.sum(s_2, axis=0, keepdims=True)

    per_tile = tm // CONV_HALO
    rev = lambda i: nt - 1 - i
    return pl.pallas_call(
        body, name=name, grid=(F2 // tc, nt),
        in_specs=[pl.BlockSpec((tm, tc), lambda j, i: (rev(i), j)), pl.BlockSpec((tm, tc), lambda j, i: (rev(i), j)),
                  pl.BlockSpec((tm, GLU_CHUNK), lambda j, i: (rev(i), j)),
                  pl.BlockSpec((3, tc), lambda j, i: (0, j))],
        out_specs=[pl.BlockSpec((tm, tc), lambda j, i: (rev(i), j)),
                   pl.BlockSpec((3, tc), lambda j, i: (0, j)), pl.BlockSpec((1, tc), lambda j, i: (0, j))],
        out_shape=[jax.ShapeDtypeStruct((M, F2), BF16), jax.ShapeDtypeStruct((3, F2), F32),
                   jax.ShapeDtypeStruct((1, F2), F32)],
        scratch_shapes=[pltpu.VMEM((tm + CONV_HALO, tc), F32), pltpu.VMEM((CONV_HALO, tc), F32)],
        compiler_params=_cparams(),
    )(u, c, dact, cw)


def _split3(x):
    hi = x.astype(BF16)
    r = x - hi.astype(F32)
    mid = r.astype(BF16)
    lo = (r - mid.astype(F32)).astype(BF16)
    return hi, mid, lo


def _tri_sum(tri, x):
    return sum(jnp.dot(tri, part, preferred_element_type=F32) for part in _split3(x))


def _log_sigmoid(x):
    return jnp.minimum(x, 0.0) - jnp.log1p(jnp.exp(-jnp.abs(x)))


def _forget_cumsum(pre, bias, *, name):
    M, C = pre.shape
    tm = _pick(M, ROW_TILES)

    def body(p_ref, b_ref, c_ref, carry_ref):
        i = pl.program_id(0)

        @pl.when(i == 0)
        def _():
            carry_ref[...] = jnp.zeros_like(carry_ref)

        logf = _log_sigmoid(p_ref[...] + b_ref[...])
        r = lax.broadcasted_iota(jnp.int32, (tm, tm), 0)
        s = lax.broadcasted_iota(jnp.int32, (tm, tm), 1)
        c_ref[...] = _tri_sum((s <= r).astype(BF16), logf) + carry_ref[...]
        carry_ref[...] = c_ref[pl.ds(tm - 1, 1), :]

    return pl.pallas_call(
        body, name=name, grid=(M // tm,),
        in_specs=[pl.BlockSpec((tm, C), lambda i: (i, 0)), pl.BlockSpec((1, C), lambda i: (0, 0))],
        out_specs=pl.BlockSpec((tm, C), lambda i: (i, 0)),
        out_shape=jax.ShapeDtypeStruct((M, C), F32),
        scratch_shapes=[pltpu.VMEM((1, C), F32)],
        compiler_params=_cparams(),
    )(pre, bias)


def _forget_cumsum_bwd(dc, pre, bias, *, name):
    M, C = pre.shape
    tm = _pick(M, ROW_TILES)
    nt = M // tm

    def body(dc_ref, p_ref, b_ref, dp_ref, db_ref, carry_ref, run_ref):
        i = pl.program_id(0)

        @pl.when(i == 0)
        def _():
            carry_ref[...] = jnp.zeros_like(carry_ref)
            db_ref[...] = jnp.zeros_like(db_ref)

        r = lax.broadcasted_iota(jnp.int32, (tm, tm), 0)
        s = lax.broadcasted_iota(jnp.int32, (tm, tm), 1)
        run_ref[...] = _tri_sum((s >= r).astype(BF16), dc_ref[...]) + carry_ref[...]
        carry_ref[...] = run_ref[pl.ds(0, 1), :]
        dpre = run_ref[...] * jax.nn.sigmoid(-(p_ref[...] + b_ref[...]))
        dp_ref[...] = dpre.astype(BF16)
        db_ref[...] += jnp.sum(dpre, axis=0, keepdims=True)

    rev_blk = pl.BlockSpec((tm, C), lambda i: (nt - 1 - i, 0))
    vec = pl.BlockSpec((1, C), lambda i: (0, 0))
    return pl.pallas_call(
        body, name=name, grid=(nt,),
        in_specs=[rev_blk, rev_blk, vec], out_specs=[rev_blk, vec],
        out_shape=[jax.ShapeDtypeStruct((M, C), BF16), jax.ShapeDtypeStruct((1, C), F32)],
        scratch_shapes=[pltpu.VMEM((1, C), F32), pltpu.VMEM((tm, C), F32)],
        compiler_params=_cparams(),
    )(dc, pre, bias)


def _causal_mask(tm):
    key = lax.broadcasted_iota(jnp.int32, (tm, tm), 0)
    query = lax.broadcasted_iota(jnp.int32, (tm, tm), 1)
    return key <= query


def _dot_nt(a, b):
    return lax.dot_general(a, b, (((1,), (1,)), ((), ())), preferred_element_type=F32)


def _loop_unrolled(n, step, init, unroll):
    def trip(p, carry):
        for r in range(unroll):
            carry = step(unroll * p + r, carry)
        return carry
    carry = lax.fori_loop(0, n // unroll, trip, init)
    return lax.fori_loop(unroll * (n // unroll), n, step, carry)


def _rows(main, extras, total):
    tm = main.shape[1]
    used = sum(e.shape[0] for e in extras)
    tile = jnp.concatenate([e.astype(BF16) for e in extras] + [jnp.zeros((ATTN_EXTRA - used, tm), BF16)], axis=0)
    rest = total - main.shape[0] - ATTN_EXTRA
    return jnp.concatenate([main, tile] + ([jnp.zeros((rest, tm), BF16)] if rest else []), axis=0)


def _attn_specs(H, M, dh, tm):
    nt = M // tm
    qT_blk = pl.BlockSpec((1, dh, tm), lambda h, i: (h, 0, i))
    row_blk = pl.BlockSpec((1, 1, tm), lambda h, i: (h, 0, i))
    key_blk = pl.BlockSpec((1, M, LANES), lambda h, i: (h, 0, 0))
    keyT_blk = pl.BlockSpec((1, nt, dh + ATTN_EXTRA, tm), lambda h, i: (h, 0, 0, 0))
    return qT_blk, row_blk, key_blk, keyT_blk


def _pair_specs(M, dh, tm):
    assert 2 * dh == LANES
    cols = lambda section: pl.BlockSpec((tm, LANES), lambda p, i: (i, section + p))
    headsT = pl.BlockSpec((2, dh, tm), lambda p, i: (p, 0, i))
    return cols, headsT


def _attn_pack(qkv, c, H, *, name):
    M, D3 = qkv.shape
    D = D3 // 3
    dh = D // H
    tm = _pick(M, ROW_TILES)
    nt = M // tm
    cols, headsT = _pair_specs(M, dh, tm)
    key_blk = pl.BlockSpec((2, tm, LANES), lambda p, i: (p, i, 0))
    keyT_blk = pl.BlockSpec((2, 1, dh + ATTN_EXTRA, tm), lambda p, i: (p, i, 0, 0))

    def body(k_ref, v_ref, q_ref, c_ref, ka_ref, va_ref, qT_ref, kT_ref, vT_ref):
        p = pl.program_id(0)
        lane = lax.broadcasted_iota(jnp.int32, (tm, LANES), 1)
        k_v, v_v = k_ref[...], v_ref[...]
        ones_v = jnp.where((lane >= dh) & (lane < dh + 3), 1.0, 0.0)
        for e in range(2):
            ck = jnp.sum(jnp.where(lane == 2 * p + e, c_ref[...], 0.0), axis=1, keepdims=True)
            hi, mid, lo = _split3(-ck)
            extra = jnp.where(lane == dh + 3, hi.astype(F32), jnp.where(lane == dh + 4, mid.astype(F32),
                              jnp.where(lane == dh + 5, lo.astype(F32), ones_v)))
            first = lambda t: t if e == 0 else pltpu.roll(t, dh, 1)
            ka_ref[e] = jnp.where(lane < dh, first(k_v), extra.astype(BF16))
            va_ref[e] = jnp.where(lane < dh, first(v_v), ones_v.astype(BF16))
        row = lax.broadcasted_iota(jnp.int32, (ATTN_EXTRA, tm), 0)
        tail = jnp.where(row == 0, 1.0, 0.0).astype(BF16)
        for src, dst in ((k_v, kT_ref), (v_v, vT_ref)):
            t = src.T
            for e in range(2):
                dst[e, 0, 0:dh, :] = t[e * dh:(e + 1) * dh]
                dst[e, 0, dh:, :] = tail
        qT_ref[...] = q_ref[...].T.reshape(2, dh, tm)

    n_sec = D // LANES
    return pl.pallas_call(
        body, name=name, grid=(H // 2, nt),
        in_specs=[cols(0), cols(n_sec), cols(2 * n_sec), pl.BlockSpec((tm, LANES), lambda p, i: (i, 0))],
        out_specs=[key_blk, key_blk, headsT, keyT_blk, keyT_blk],
        out_shape=[jax.ShapeDtypeStruct((H, M, LANES), BF16)] * 2 + [jax.ShapeDtypeStruct((H, dh, M), BF16)]
        + [jax.ShapeDtypeStruct((H, nt, dh + ATTN_EXTRA, tm), BF16)] * 2,
        compiler_params=_cparams(),
    )(qkv, qkv, qkv, c)


def _heads_merge(parts, M, tm, *, name):
    H = parts[0].shape[0]
    dh = LANES // 2
    blocked = [t.ndim == 4 for t in parts]
    cols, headsT = _pair_specs(M, dh, tm)

    def body(*refs):
        for src, dst, b in zip(refs[:len(parts)], refs[len(parts):], blocked):
            t = jnp.concatenate([src[e, 0, 0:dh, :] for e in range(2)], axis=0) if b else src[...].reshape(2 * dh, tm)
            dst[...] = t.astype(BF16).T

    in_specs = [pl.BlockSpec((2, 1, t.shape[2], tm), lambda p, i: (p, i, 0, 0)) if b else headsT
                for t, b in zip(parts, blocked)]
    return pl.pallas_call(
        body, name=name, grid=(H // 2, M // tm), in_specs=in_specs, out_specs=[cols(0)] * len(parts),
        out_shape=[jax.ShapeDtypeStruct((M, H * dh), BF16)] * len(parts), compiler_params=_cparams(),
    )(*parts)


def _attn_fwd(qT, cq, k_aug, vT_aug, *, name):
    H, dh, M = qT.shape
    tm = vT_aug.shape[-1]
    qT_blk, row_blk, key_blk, keyT_blk = _attn_specs(H, M, dh, tm)

    def body(qT_ref, cq_ref, k_ref, vT_ref, oT_ref, lse_ref):
        i = pl.program_id(1)
        ones = jnp.ones((3, tm), BF16)
        qa = _rows(qT_ref[0] * jnp.asarray(dh ** -0.5, BF16), [*_split3(cq_ref[0]), ones], LANES)

        def block(j, carry, masked):
            m, acc = carry
            keys = pl.ds(pl.multiple_of(j * tm, tm), tm)
            sT = jnp.dot(k_ref[0, keys, :], qa, preferred_element_type=F32)
            if masked:
                sT = jnp.where(_causal_mask(tm), sT, NEG_INF)
            m_new = jnp.maximum(m, jnp.max(sT, axis=0, keepdims=True))
            pT = jnp.exp(sT - m_new).astype(BF16)
            acc = jnp.exp(m - m_new) * acc + jnp.dot(vT_ref[0, j], pT, preferred_element_type=F32)
            return m_new, acc

        init = (jnp.full((1, tm), NEG_INF, F32), jnp.zeros((dh + ATTN_EXTRA, tm), F32))
        m, acc = block(i, _loop_unrolled(i, lambda j, c: block(j, c, False), init, 4), True)
        l = acc[dh:dh + 1, :]
        oT_ref[0] = (acc[0:dh, :] / l).astype(BF16)
        lse_ref[0] = m + jnp.log(l)

    return pl.pallas_call(
        body, name=name, grid=(H, M // tm),
        in_specs=[qT_blk, row_blk, key_blk, keyT_blk], out_specs=[qT_blk, row_blk],
        out_shape=[jax.ShapeDtypeStruct((H, dh, M), BF16), jax.ShapeDtypeStruct((H, 1, M), F32)],
        compiler_params=_cparams(),
    )(qT, cq, k_aug, vT_aug)


def _attn_bwd(qT, oT, doT, lse, cq, k_aug, v_aug, kT_aug, *, name):
    H, dh, M = qT.shape
    tm = kT_aug.shape[-1]
    qT_blk, row_blk, key_blk, keyT_blk = _attn_specs(H, M, dh, tm)
    dvT_blk = pl.BlockSpec((1, M // tm, dh, tm), lambda h, i: (h, 0, 0, 0))

    def body(qT_ref, oT_ref, doT_ref, lse_ref, cq_ref, k_ref, v_ref, kT_ref, dqT_ref, dcq_ref, dkT_ref, dvT_ref):
        i = pl.program_id(1)

        @pl.when(i == 0)
        def _():
            dkT_ref[...] = jnp.zeros_like(dkT_ref)
            dvT_ref[...] = jnp.zeros_like(dvT_ref)

        qsT = qT_ref[0] * jnp.asarray(dh ** -0.5, BF16)
        doT = doT_ref[0]
        delta = jnp.sum(doT.astype(F32) * oT_ref[0].astype(F32), axis=0, keepdims=True)
        ones = jnp.ones((3, tm), BF16)
        qa = _rows(qsT, [*_split3(cq_ref[0] - lse_ref[0]), ones], LANES)
        da = _rows(doT, _split3(-delta), LANES)
        q1 = _rows(qsT, [ones[0:1]], dh + ATTN_EXTRA)

        def block(j, dq, masked):
            keys = pl.ds(pl.multiple_of(j * tm, tm), tm)
            pT = jnp.exp(jnp.dot(k_ref[0, keys, :], qa, preferred_element_type=F32))
            if masked:
                pT = jnp.where(_causal_mask(tm), pT, 0.0)
            ds_b = (pT * jnp.dot(v_ref[0, keys, :], da, preferred_element_type=F32)).astype(BF16)
            dvT_ref[0, j] += _dot_nt(doT, pT.astype(BF16))
            dkT_ref[0, j] += _dot_nt(q1, ds_b)
            return dq + jnp.dot(kT_ref[0, j], ds_b, preferred_element_type=F32)

        init = jnp.zeros((dh + ATTN_EXTRA, tm), F32)
        dq = block(i, _loop_unrolled(i, lambda j, c: block(j, c, False), init, 4), True)
        dqT_ref[0] = (dq[0:dh, :] * dh ** -0.5).astype(BF16)
        dcq_ref[0] = dq[dh:dh + 1, :]

    return pl.pallas_call(
        body, name=name, grid=(H, M // tm),
        in_specs=[qT_blk, qT_blk, qT_blk, row_blk, row_blk, key_blk, key_blk, keyT_blk],
        out_specs=[qT_blk, row_blk, keyT_blk, dvT_blk],
        out_shape=[jax.ShapeDtypeStruct((H, dh, M), BF16), jax.ShapeDtypeStruct((H, 1, M), F32),
                   jax.ShapeDtypeStruct(kT_aug.shape, F32), jax.ShapeDtypeStruct((H, M // tm, dh, tm), F32)],
        compiler_params=_cparams(),
    )(qT, oT, doT, lse, cq, k_aug, v_aug, kT_aug)


def kernel(x, meta, pool_w, pool_scale, w_kv, w_f, b_f, w_q, w_o, ffn_w_in, ffn_conv_w, ffn_conv_b, ffn_w_out, ln_g, ln_b, loss_target, m_meta, m_pool_w, m_pool_scale, m_w_kv, m_w_f, m_b_f, m_w_q, m_w_o, m_ffn_w_in, m_ffn_conv_w, m_ffn_conv_b, m_ffn_w_out, m_ln_g, m_ln_b, v_meta, v_pool_w, v_pool_scale, v_w_kv, v_w_f, v_b_f, v_w_q, v_w_o, v_ffn_w_in, v_ffn_conv_w, v_ffn_conv_b, v_ffn_w_out, v_ln_g, v_ln_b):
    local = dict(meta=meta, pool_w=pool_w, pool_scale=pool_scale, w_kv=w_kv, w_f=w_f, b_f=b_f, w_q=w_q, w_o=w_o,
                 ffn_w_in=ffn_w_in, ffn_conv_w=ffn_conv_w, ffn_conv_b=ffn_conv_b, ffn_w_out=ffn_w_out,
                 ln_g=ln_g, ln_b=ln_b)
    mom1 = dict(meta=m_meta, pool_w=m_pool_w, pool_scale=m_pool_scale, w_kv=m_w_kv, w_f=m_w_f, b_f=m_b_f,
                w_q=m_w_q, w_o=m_w_o, ffn_w_in=m_ffn_w_in, ffn_conv_w=m_ffn_conv_w, ffn_conv_b=m_ffn_conv_b,
                ffn_w_out=m_ffn_w_out, ln_g=m_ln_g, ln_b=m_ln_b)
    mom2 = dict(meta=v_meta, pool_w=v_pool_w, pool_scale=v_pool_scale, w_kv=v_w_kv, w_f=v_w_f, b_f=v_b_f,
                w_q=v_w_q, w_o=v_w_o, ffn_w_in=v_ffn_w_in, ffn_conv_w=v_ffn_conv_w, ffn_conv_b=v_ffn_conv_b,
                ffn_w_out=v_ffn_w_out, ln_g=v_ln_g, ln_b=v_ln_b)
    axis_of = dict(PARAMS)

    S, D = x.shape[1], x.shape[2]
    H = b_f.shape[0]
    dh = D // H
    n_tok = N_META + S
    M = _round_up(n_tok, ROW_ALIGN)
    tm = _pick(M, ROW_TILES)
    nt = M // tm
    F2 = ffn_conv_b.shape[1]
    F = F2 // 2
    depth = ffn_conv_b.shape[0]

    small_sharded = [n for n in SMALL if axis_of[n] is not None]
    got = _all_gather([local[n].astype(BF16) for n in BIG] + [_pack([local[n] for n in small_sharded])],
                      "gather_weights")
    wb = {n: _from_blocks(blk, axis_of[n]) for n, blk in zip(BIG, got) if n != "ffn_w_in"}
    small_blocks = _unpack(got[-1], [local[n].shape for n in small_sharded], lead=(N_DEV,))
    wf32 = {n: _from_blocks(blk, axis_of[n]) for n, blk in zip(small_sharded, small_blocks)}

    pw = wb["pool_w"][0]
    wf_pad = jnp.pad(wf32["w_f"].astype(BF16), ((0, 0), (0, LANES - H)))
    w_qkv = jnp.concatenate([wb["w_kv"], wb["w_q"][0]], axis=1)
    w_att = jnp.concatenate([w_qkv, wf_pad], axis=1)
    wo = wb["w_o"][0]
    w_in = _w_in_interleaved(got[BIG.index("ffn_w_in")], name="w_in_interleave")
    w_out = wb["ffn_w_out"]
    conv_w = _glu_interleave(wf32["ffn_conv_w"])
    conv_b = _glu_interleave(ffn_conv_b)[:, None, :]
    scale = wf32["pool_scale"]
    g_ln, b_ln = wf32["ln_g"], wf32["ln_b"]
    ln = lambda i, j: (g_ln[i, j][None, :], b_ln[i, j][None, :])
    bias_f = jnp.pad(b_f, (0, LANES - H))[None, :]

    pad_rows = M - n_tok
    h0 = jnp.concatenate([wf32["meta"], x[0], jnp.zeros((pad_rows, D), F32)], axis=0)
    target = jnp.concatenate([jnp.zeros((N_META, D), F32), loss_target[0], jnp.zeros((pad_rows, D), F32)], axis=0)

    z1, h1, h1b, diff = _pool_fwd(h0, pw, scale, *ln(0, 0), name="pool_fwd")
    u0 = _mm(h1b, w_in[0], name="ffn0_up", out_dtype=F32)
    act0, c0 = _conv_glu_fwd(u0, conv_w[0], conv_b[0], name="ffn0_conv")
    z2, h2, h2b = _mm_ln(act0, w_out[0], h1, *ln(0, 1), name="ffn0_down_ln")

    qkv = _mm(h2b, w_qkv, name="attn_qkv", out_dtype=BF16)
    pre = _mm(h2b, wf_pad, name="attn_gate", out_dtype=F32)
    c = _forget_cumsum(pre, bias_f, name="attn_cumsum")
    k_aug, v_aug, qT_h, kT_aug, vT_aug = _attn_pack(qkv, c, H, name="attn_pack")
    cq = c[:, :H].T[:, None, :]
    oT_h, lse = _attn_fwd(qT_h, cq, k_aug, vT_aug, name="attn_fwd")
    oT = oT_h.reshape(D, M)
    z3, h3, h3b = _mm_ln(oT, wo, h2, *ln(1, 0), name="attn_out_ln", a_transposed=True)

    u1 = _mm(h3b, w_in[1], name="ffn1_up", out_dtype=F32)
    act1, c1 = _conv_glu_fwd(u1, conv_w[1], conv_b[1], name="ffn1_conv")
    dz4, dz4b, loss_part, dg11, db11 = _mm_ln(act1, w_out[1], h3, *ln(1, 1), name="ffn1_down_loss", target=target,
                                              n_tok=n_tok)
    loss = lax.psum(loss_part[0, 0], ("x", "y", "c"))

    grads = {}

    def ffn_bwd(layer, dz, dz_b, u, c_conv, act, h_in_b, z_in, g_in, tag):
        dact = _mm(dz_b, w_out[layer], name=tag + "_dact", out_dtype=BF16, b_transposed=True)
        du, dcw, dcb = _conv_glu_bwd(u, c_conv, dact, conv_w[layer], name=tag + "_conv_bwd")
        dz_in = _mm_ln_bwd(du, w_in[layer], dz, z_in, g_in[None, :], name=tag + "_dh_ln_bwd")
        d_w_out = _mm_tn(act, dz_b, name=tag + "_dw_out")
        d_w_in = _w_in_grad_segments(_mm_tn(h_in_b, du, name=tag + "_dw_in"), name=tag + "_dw_in_segments")
        return dz_in, d_w_in, d_w_out, _glu_deinterleave(dcw), _glu_deinterleave(dcb)[0]

    (dz3, dz3b, dg10, db10), dwin1, dwout1, dcw1, dcb1 = ffn_bwd(1, dz4, dz4b, u1, c1, act1, h3b, z3, g_ln[1, 0], "ffn1")

    doT_h = _mm(wo, dz3b, name="attn_do", out_dtype=BF16, b_transposed=True).reshape(H, dh, M)
    grads["w_o"] = _mm(oT, dz3b, name="attn_dw_o", out_dtype=F32)[None]
    dqT_h, dcq, dkT_a, dvT_h = _attn_bwd(qT_h, oT_h, doT_h, lse, cq, k_aug, v_aug, kT_aug, name="attn_bwd")
    dck = -dkT_a[:, :, dh, :].reshape(H, M)
    dc = jnp.pad((dcq[:, 0, :] + dck).T, ((0, 0), (0, LANES - H)))
    dpre, dbias = _forget_cumsum_bwd(dc, pre, bias_f, name="attn_cumsum_bwd")
    d_att = jnp.concatenate([*_heads_merge([dkT_a, dvT_h, dqT_h], M, tm, name="attn_dqkv_merge"), dpre], axis=1)
    dz2, dz2b, dg01, db01 = _mm_ln_bwd(d_att, w_att, dz3, z2, g_ln[0, 1][None, :], name="attn_dh_ln_bwd")
    d_w_att = _mm_tn(h2b, d_att, name="attn_dw_qkv")
    grads["w_kv"] = d_w_att[:, :2 * D]
    grads["w_q"] = d_w_att[:, 2 * D:3 * D][None]
    grads["w_f"] = d_w_att[:, 3 * D:3 * D + H]
    grads["b_f"] = dbias[0, :H]

    (dz1, _, dg00, db00), dwin0, dwout0, dcw0, dcb0 = ffn_bwd(0, dz2, dz2b, u0, c0, act0, h1b, z1, g_ln[0, 0], "ffn0")
    dh0, dpw, dscale = _pool_bwd(dz1, diff, pw, scale, name="pool_bwd")

    grads["meta"] = dh0[:N_META]
    grads["pool_w"] = dpw[None]
    grads["pool_scale"] = dscale
    seg_w_in = jnp.stack([dwin0, dwin1], axis=2)
    grads["ffn_w_out"] = jnp.stack([dwout0, dwout1])
    grads["ffn_conv_w"] = jnp.stack([dcw0, dcw1])
    grads["ffn_conv_b"] = jnp.stack([dcb0, dcb1])
    grads["ln_g"] = jnp.stack([jnp.stack([dg00[0], dg01[0]]), jnp.stack([dg10[0], dg11[0]])])
    grads["ln_b"] = jnp.stack([jnp.stack([db00[0], db01[0]]), jnp.stack([db10[0], db11[0]])])
    grad_x = dh0[N_META:n_tok][None]

    blocks = {n: _to_blocks(grads[n], axis_of[n]) for n, _ in PARAMS if n != "ffn_w_in"}
    me = _coords()
    core, chip = (jnp.reshape(v, (1,)).astype(jnp.int32) for v in (me[2], 2 * me[0] + me[1]))
    by_core = lambda t: jnp.swapaxes(t.reshape((N_CHIPS, 2) + t.shape[1:]), 0, 1).astype(BF16)
    segs = [seg_w_in if n == "ffn_w_in" else by_core(blocks[n]) for n in BIG]
    pair = _exchange(segs, "grad_pair_exchange", (1,), lambda pos: pos[2])
    rows_of = lambda t, lead: t.reshape(lead + (-1, t.shape[-1]))
    partial = [_pair_sum(rows_of(s, (2,)), core, rows_of(p[0], ()), name="grad_pair_sum_" + n).reshape(s.shape[1:])
               for n, s, p in zip(BIG, segs, pair)]
    recv = _exchange(partial, "grad_chip_exchange", (4, 2, 6), lambda pos: 2 * pos[0] + pos[1])
    recv_small = _exchange([_pack([blocks[n] for n in SMALL], lead=(N_DEV,))], "grad_small_exchange",
                           tuple(range(1, N_DEV)), _index, by_sender=True)[0]
    results = {}
    for n, own, r in zip(BIG, partial, recv):
        shape = local[n].shape
        as2d = lambda t: t.reshape(-1, shape[-1])
        outs = _adamw(rows_of(own, (N_CHIPS,)), chip, rows_of(r, (N_CHIPS - 1,)), as2d(local[n]), as2d(mom1[n]),
                      as2d(mom2[n]), name="adamw_" + n)
        results[n] = [o_.reshape(shape) for o_ in outs]
    outs = _adamw(recv_small, jnp.zeros((1,), jnp.int32), recv_small[1:],
                  *[_pack([d[n] for n in SMALL]) for d in (local, mom1, mom2)], name="adamw_small")
    small_out = [_unpack(o_, [local[n].shape for n in SMALL]) for o_ in outs]
    for i, n in enumerate(SMALL):
        results[n] = [small_out[k][i] for k in range(4)]
    return (loss, grad_x, *[results[n][k] for k in range(4) for n, _ in PARAMS])
```

```python
import jax
import jax.numpy as jnp
from jax import lax
from jax.experimental import pallas as pl
from jax.experimental.pallas import tpu as pltpu

F32, BF16 = jnp.float32, jnp.bfloat16
MESH = pl.DeviceIdType.MESH

N_DEV = 8
N_CHIPS = 4
N_META = 16
POOL_WINDOWS = (2, 4, 8, 16)
POOL_HALO = 16
CONV_HALO = 16
CONV_STRIP = 16
ALPHA = 4.0 ** 0.25
LN_EPS = 1e-5
NEG_INF = -1e30
ADAM_LR, ADAM_B1, ADAM_B2, ADAM_EPS, ADAM_WD, ADAM_STEP = 0.001, 0.9, 0.999, 1e-08, 0.01, 10

LANES = 128
ROW_ALIGN = 128
ROW_TILES = (640, 512, 128)
PACK_COLS = 1024
ADAM_TILE_BYTES = 4 << 20
MM_TN_WINDOW_BYTES = 28 << 20
GLU_CHUNK = 256
ATTN_EXTRA = 16
VMEM_LIMIT = 56 * 1024 * 1024

PARAMS = (("meta", 1), ("pool_w", 2), ("pool_scale", 1), ("w_kv", 1), ("w_f", 0), ("b_f", None),
          ("w_q", 1), ("w_o", 1), ("ffn_w_in", 2), ("ffn_conv_w", 2), ("ffn_conv_b", None),
          ("ffn_w_out", 1), ("ln_g", 2), ("ln_b", 2))
BIG = ("pool_w", "w_kv", "w_q", "w_o", "ffn_w_in", "ffn_w_out")
SMALL = ("meta", "pool_scale", "w_f", "b_f", "ffn_conv_w", "ffn_conv_b", "ln_g", "ln_b")


def _cparams(**kw):
    return pltpu.CompilerParams(vmem_limit_bytes=VMEM_LIMIT, **kw)


def _pick(n, cands):
    for c in cands:
        if n % c == 0:
            return c
    return n


def _round_up(n, m):
    return (n + m - 1) // m * m


def _pack(pieces, lead=()):
    flat = []
    for p in pieces:
        v = p.reshape(lead + (-1,))
        flat.append(jnp.pad(v, [(0, 0)] * len(lead) + [(0, _round_up(v.shape[-1], PACK_COLS) - v.shape[-1])]))
    v = jnp.concatenate(flat, axis=-1)
    rows = _round_up(v.shape[-1] // PACK_COLS, 8)
    v = jnp.pad(v, [(0, 0)] * len(lead) + [(0, rows * PACK_COLS - v.shape[-1])])
    return v.reshape(lead + (rows, PACK_COLS))


def _unpack(buf, shapes, lead=()):
    flat = buf.reshape(lead + (-1,))
    out, off = [], 0
    for s in shapes:
        n = 1
        for d in s:
            n *= d
        out.append(flat[..., off:off + n].reshape(lead + tuple(s)))
        off += _round_up(n, PACK_COLS)
    return out


def _to_blocks(full, axis):
    if axis is None:
        return jnp.broadcast_to(full[None], (N_DEV,) + full.shape)
    s = full.shape
    x = full.reshape(s[:axis] + (N_DEV, s[axis] // N_DEV) + s[axis + 1:])
    return jnp.moveaxis(x, axis, 0)


def _from_blocks(blocks, axis):
    x = jnp.moveaxis(blocks, 0, axis)
    s = x.shape
    return x.reshape(s[:axis] + (s[axis] * s[axis + 1],) + s[axis + 2:])


def _coords():
    return lax.axis_index("x"), lax.axis_index("y"), lax.axis_index("c")


def _flip(pos, mask):
    x, y, c = pos
    return (1 - x if mask & 4 else x, 1 - y if mask & 2 else y, 1 - c if mask & 1 else c)


def _index(pos):
    x, y, c = pos
    return 4 * x + 2 * y + c


def _comm_call(body, name, arrays, out_shapes):
    n = len(arrays)
    hbm = pl.BlockSpec(memory_space=pl.ANY)
    return pl.pallas_call(
        body, name=name, out_shape=out_shapes, in_specs=[hbm] * n, out_specs=[hbm] * n,
        scratch_shapes=[pltpu.SemaphoreType.DMA((7 * n,)), pltpu.SemaphoreType.DMA((7 * n,)),
                        pltpu.SemaphoreType.DMA((n,))],
    )(*arrays)


def _all_gather(blocks, name):
    chip_masks = (4, 2, 6)
    n = len(blocks)

    def body(*refs):
        x_refs, out_refs = refs[:n], refs[n:2 * n]
        send_sems, recv_sems, local_sems = refs[2 * n:]
        me = _coords()
        sibling = _flip(me, 1)

        def copy(a, k, owner, to, from_input=False):
            slot = out_refs[a].at[_index(owner)]
            return pltpu.make_async_remote_copy(
                src_ref=x_refs[a] if from_input else slot, dst_ref=slot,
                send_sem=send_sems.at[7 * a + k], recv_sem=recv_sems.at[7 * a + k],
                device_id=to, device_id_type=MESH)

        mine = [pltpu.make_async_copy(x_refs[a], out_refs[a].at[_index(me)], local_sems.at[a]) for a in range(n)]
        first = [copy(a, 0, me, sibling, True) for a in range(n)]
        first += [copy(a, 1 + j, me, _flip(me, m), True) for j, m in enumerate(chip_masks) for a in range(n)]
        for cp in mine + first:
            cp.start()
        passed = []
        for j, m in enumerate(chip_masks):
            for a in range(n):
                copy(a, 1 + j, _flip(me, m), me).wait_recv()
                passed.append(copy(a, 4 + j, _flip(me, m), sibling))
                passed[-1].start()
        for a in range(n):
            copy(a, 0, sibling, me).wait_recv()
            for j, m in enumerate(chip_masks):
                copy(a, 4 + j, _flip(sibling, m), me).wait_recv()
        for cp in first + passed:
            cp.wait_send()
        for cp in mine:
            cp.wait()

    return _comm_call(body, name, blocks, [jax.ShapeDtypeStruct((N_DEV,) + b.shape, b.dtype) for b in blocks])


def _exchange(segs, name, masks, slot_of, by_sender=False):
    n = len(segs)

    def body(*refs):
        seg_refs, out_refs = refs[:n], refs[n:2 * n]
        send_sems, recv_sems, local_sems = refs[2 * n:]
        me = _coords()

        def copy(a, k, sender):
            to = _flip(sender, masks[k])
            return pltpu.make_async_remote_copy(
                src_ref=seg_refs[a].at[slot_of(to)], dst_ref=out_refs[a].at[slot_of(sender) if by_sender else k],
                send_sem=send_sems.at[7 * a + k], recv_sem=recv_sems.at[7 * a + k],
                device_id=to, device_id_type=MESH)

        mine = [pltpu.make_async_copy(seg_refs[a].at[slot_of(me)], out_refs[a].at[slot_of(me)], local_sems.at[a])
                for a in range(n)] if by_sender else []
        sends = [copy(a, k, me) for k in range(len(masks)) for a in range(n)]
        for cp in mine + sends:
            cp.start()
        for k, mask in enumerate(masks):
            for a in range(n):
                copy(a, k, _flip(me, mask)).wait_recv()
        for cp in sends:
            cp.wait_send()
        for cp in mine:
            cp.wait()

    slots = lambda s: s.shape[0] if by_sender else len(masks)
    return _comm_call(body, name, segs, [jax.ShapeDtypeStruct((slots(s),) + s.shape[1:], s.dtype) for s in segs])


def _own_slot(tr, cols):
    return pl.BlockSpec((1, tr, cols), lambda i, slot: (slot[0], i, 0))


def _pair_sum(seg, slot, recv, name):
    _, rows, cols = seg.shape
    tr = max(t for t in range(16, rows + 1, 16) if rows % t == 0 and 2 * t * cols * 2 <= ADAM_TILE_BYTES)

    def body(slot_ref, own_ref, recv_ref, o_ref):
        o_ref[...] = (own_ref[0].astype(F32) + recv_ref[...].astype(F32)).astype(BF16)

    tile = pl.BlockSpec((tr, cols), lambda i, slot: (i, 0))
    return pl.pallas_call(
        body, name=name,
        grid_spec=pltpu.PrefetchScalarGridSpec(
            num_scalar_prefetch=1, grid=(rows // tr,), in_specs=[_own_slot(tr, cols), tile], out_specs=tile),
        out_shape=jax.ShapeDtypeStruct((rows, cols), BF16),
        compiler_params=_cparams(),
    )(slot, seg, recv)


def _adamw(seg, slot, recv, w, m, v, name):
    rows, cols = w.shape
    n_slots = recv.shape[0]
    sublanes = 8 * (4 // recv.dtype.itemsize)
    tr = max(t for t in range(sublanes, rows + 1, sublanes)
             if rows % t == 0 and N_DEV * t * cols * 4 <= ADAM_TILE_BYTES)
    c1 = 1.0 - ADAM_B1 ** ADAM_STEP
    c2 = 1.0 - ADAM_B2 ** ADAM_STEP

    def body(slot_ref, own_ref, r_ref, w_ref, m_ref, v_ref, g_out, d_out, m_out, v_out):
        g = own_ref[0].astype(F32)
        for s in range(n_slots):
            g = g + r_ref[s].astype(F32)
        m_new = ADAM_B1 * m_ref[...] + (1.0 - ADAM_B1) * g
        v_new = ADAM_B2 * v_ref[...] + (1.0 - ADAM_B2) * (g * g)
        m_hat = m_new / c1
        v_hat = v_new / c2
        g_out[...] = g
        d_out[...] = -ADAM_LR * (m_hat / (jnp.sqrt(v_hat) + ADAM_EPS) + ADAM_WD * w_ref[...])
        m_out[...] = m_new
        v_out[...] = v_new

    tile = pl.BlockSpec((tr, cols), lambda i, slot: (i, 0))
    return pl.pallas_call(
        body, name=name,
        grid_spec=pltpu.PrefetchScalarGridSpec(
            num_scalar_prefetch=1, grid=(rows // tr,),
            in_specs=[_own_slot(tr, cols), pl.BlockSpec((n_slots, tr, cols), lambda i, slot: (0, i, 0)),
                      tile, tile, tile],
            out_specs=[tile] * 4),
        out_shape=[jax.ShapeDtypeStruct(w.shape, F32)] * 4,
        compiler_params=_cparams(),
    )(slot, seg, recv, w, m, v)


def _mm(a, b, *, name, out_dtype, b_transposed=False):
    M, K = a.shape
    N = b.shape[0] if b_transposed else b.shape[1]
    tm = _pick(M, ROW_TILES)
    tn = _pick(N, (2816, 1024, 640, 512, 256, 128))
    tk = K if K <= 3200 else _pick(K, (2816, 1024, 640, 512, 256, 128))
    nk = K // tk

    def body(*refs):
        a_ref, b_ref, o_ref = refs[:3]
        acc_ref = refs[-1] if nk > 1 else None
        if b_transposed:
            prod = _dot_nt(a_ref[...], b_ref[...])
        else:
            prod = jnp.dot(a_ref[...], b_ref[...], preferred_element_type=F32)
        if nk == 1:
            o_ref[...] = prod.astype(out_dtype)
        else:
            k = pl.program_id(2)

            @pl.when(k == 0)
            def _():
                acc_ref[...] = prod

            @pl.when(k > 0)
            def _():
                acc_ref[...] += prod

            @pl.when(k == nk - 1)
            def _():
                o_ref[...] = acc_ref[...].astype(out_dtype)

    b_spec = (pl.BlockSpec((tn, tk), lambda i, j, k: (j, k)) if b_transposed
              else pl.BlockSpec((tk, tn), lambda i, j, k: (k, j)))
    return pl.pallas_call(
        body, name=name, grid=(M // tm, N // tn, nk),
        in_specs=[pl.BlockSpec((tm, tk), lambda i, j, k: (i, k)), b_spec],
        out_specs=pl.BlockSpec((tm, tn), lambda i, j, k: (i, j)),
        out_shape=jax.ShapeDtypeStruct((M, N), out_dtype),
        scratch_shapes=[pltpu.VMEM((tm, tn), F32)] if nk > 1 else [],
        compiler_params=_cparams(),
    )(a, b)


def _mm_tn(a, b, *, name):
    T, M = a.shape
    N = b.shape[1]
    tm = _pick(M, (1408, 1024, 512, 256, 128))
    tn = _pick(N, (1024, 640, 512, 256, 128))
    tt = _pick(T, [t for t in (4160, 1664, 640) if 2 * 2 * t * (tm + tn) <= MM_TN_WINDOW_BYTES] + [128])

    def body(a_ref, b_ref, o_ref):
        prod = lax.dot_general(a_ref[...], b_ref[...], (((0,), (0,)), ((), ())), preferred_element_type=F32)

        @pl.when(pl.program_id(2) == 0)
        def _():
            o_ref[...] = prod

        @pl.when(pl.program_id(2) > 0)
        def _():
            o_ref[...] += prod

    return pl.pallas_call(
        body, name=name, grid=(M // tm, N // tn, T // tt),
        in_specs=[pl.BlockSpec((tt, tm), lambda i, j, k: (k, i)), pl.BlockSpec((tt, tn), lambda i, j, k: (k, j))],
        out_specs=pl.BlockSpec((tm, tn), lambda i, j, k: (i, j)),
        out_shape=jax.ShapeDtypeStruct((M, N), F32),
        compiler_params=_cparams(),
    )(a, b)


def _layer_norm(z, g, b):
    mu = jnp.mean(z, axis=-1, keepdims=True)
    xc = z - mu
    var = jnp.mean(xc * xc, axis=-1, keepdims=True)
    return xc * lax.rsqrt(var + LN_EPS) * g + b


def _mm_ln(a, w, res, g, b, *, name, target=None, n_tok=None, a_transposed=False):
    K, M = a.shape if a_transposed else a.shape[::-1]
    D = w.shape[1]
    tm = _pick(M, ROW_TILES)
    final = target is not None

    def body(*refs):
        if final:
            a_ref, w_ref, r_ref, g_ref, b_ref, t_ref, dz_ref, dzb_ref, loss_ref, dg_ref, db_ref = refs
        else:
            a_ref, w_ref, r_ref, g_ref, b_ref, z_ref, h_ref, hb_ref = refs
        contract = (((0,), (0,)), ((), ())) if a_transposed else (((1,), (0,)), ((), ()))
        z = ALPHA * r_ref[...] + lax.dot_general(a_ref[...], w_ref[...], contract, preferred_element_type=F32)
        h = _layer_norm(z, g_ref[...], b_ref[...])
        if not final:
            z_ref[...] = z
            h_ref[...] = h
            hb_ref[...] = h.astype(BF16)
            return
        i = pl.program_id(0)
        row = i * tm + lax.broadcasted_iota(jnp.int32, (tm, 1), 0)
        valid = (row >= N_META) & (row < n_tok)
        err = jnp.where(valid, h - t_ref[...], 0.0)
        dy = err / D
        dz, xhat = _ln_bwd_rows(dy, z, g_ref[...])
        dz_ref[...] = dz
        dzb_ref[...] = dz.astype(BF16)

        @pl.when(i == 0)
        def _():
            loss_ref[...] = jnp.zeros_like(loss_ref)
            dg_ref[...] = jnp.zeros_like(dg_ref)
            db_ref[...] = jnp.zeros_like(db_ref)

        loss_ref[...] += 0.5 * jnp.sum(jnp.sum(err * err, axis=1, keepdims=True) / D, axis=0, keepdims=True)
        dg_ref[...] += jnp.sum(dy * xhat, axis=0, keepdims=True)
        db_ref[...] += jnp.sum(dy, axis=0, keepdims=True)

    row_blk = lambda cols: pl.BlockSpec((tm, cols), lambda i: (i, 0))
    vec = pl.BlockSpec((1, D), lambda i: (0, 0))
    a_blk = pl.BlockSpec((K, tm), lambda i: (0, i)) if a_transposed else row_blk(K)
    in_specs = [a_blk, pl.BlockSpec((K, D), lambda i: (0, 0)), row_blk(D), vec, vec]
    args = [a, w, res, g, b]
    if final:
        in_specs.append(row_blk(D))
        args.append(target)
        out_specs = [row_blk(D), row_blk(D), pl.BlockSpec((8, LANES), lambda i: (0, 0)), vec, vec]
        out_shape = [jax.ShapeDtypeStruct((M, D), F32), jax.ShapeDtypeStruct((M, D), BF16),
                     jax.ShapeDtypeStruct((8, LANES), F32), jax.ShapeDtypeStruct((1, D), F32),
                     jax.ShapeDtypeStruct((1, D), F32)]
    else:
        out_specs = [row_blk(D)] * 3
        out_shape = [jax.ShapeDtypeStruct((M, D), F32)] * 2 + [jax.ShapeDtypeStruct((M, D), BF16)]
    return pl.pallas_call(
        body, name=name, grid=(M // tm,), in_specs=in_specs, out_specs=out_specs, out_shape=out_shape,
        compiler_params=_cparams(),
    )(*args)


def _ln_bwd_rows(dh_v, z_v, g_v):
    mu = jnp.mean(z_v, axis=-1, keepdims=True)
    xc = z_v - mu
    rstd = lax.rsqrt(jnp.mean(xc * xc, axis=-1, keepdims=True) + LN_EPS)
    xhat = xc * rstd
    dxhat = dh_v * g_v
    dz = rstd * (dxhat - jnp.mean(dxhat, axis=-1, keepdims=True)
                 - xhat * jnp.mean(dxhat * xhat, axis=-1, keepdims=True))
    return dz, xhat


def _mm_ln_bwd(a, b, dz_next, z, g, *, name):
    M, K = a.shape
    D = b.shape[0]
    tm = _pick(M, ROW_TILES)
    tk = K if K <= 3200 else _pick(K, (2816, 1024, 640, 512, 256, 128))
    nk = K // tk

    def body(a_ref, b_ref, nx_ref, z_ref, g_ref, dz_ref, dzb_ref, dg_ref, db_ref, *scratch):
        i, k = pl.program_id(0), pl.program_id(1)
        prod = _dot_nt(a_ref[...], b_ref[...])

        @pl.when((i == 0) & (k == 0))
        def _():
            dg_ref[...] = jnp.zeros_like(dg_ref)
            db_ref[...] = jnp.zeros_like(db_ref)

        def finish(acc):
            dh_v = acc + ALPHA * nx_ref[...]
            dz, xhat = _ln_bwd_rows(dh_v, z_ref[...], g_ref[...])
            dz_ref[...] = dz
            dzb_ref[...] = dz.astype(BF16)
            dg_ref[...] += jnp.sum(dh_v * xhat, axis=0, keepdims=True)
            db_ref[...] += jnp.sum(dh_v, axis=0, keepdims=True)

        if nk == 1:
            finish(prod)
        else:
            acc_ref, = scratch

            @pl.when(k == 0)
            def _():
                acc_ref[...] = prod

            @pl.when(k > 0)
            def _():
                acc_ref[...] += prod

            @pl.when(k == nk - 1)
            def _():
                finish(acc_ref[...])

    row_blk = pl.BlockSpec((tm, D), lambda i, k: (i, 0))
    vec = pl.BlockSpec((1, D), lambda i, k: (0, 0))
    return pl.pallas_call(
        body, name=name, grid=(M // tm, nk),
        in_specs=[pl.BlockSpec((tm, tk), lambda i, k: (i, k)), pl.BlockSpec((D, tk), lambda i, k: (0, k)),
                  row_blk, row_blk, vec],
        out_specs=[row_blk, row_blk, vec, vec],
        out_shape=[jax.ShapeDtypeStruct((M, D), F32), jax.ShapeDtypeStruct((M, D), BF16),
                   jax.ShapeDtypeStruct((1, D), F32), jax.ShapeDtypeStruct((1, D), F32)],
        scratch_shapes=[pltpu.VMEM((tm, D), F32)] if nk > 1 else [],
        compiler_params=_cparams(),
    )(a, b, dz_next, z, g)


def _pool_fwd(h0, pw, scale, g, b, *, name):
    M, D = h0.shape
    n_groups, G, _ = pw.shape
    tm = _pick(M, ROW_TILES)

    def body(x_ref, halo_ref, pw_ref, sc_ref, g_ref, b_ref, z_ref, h_ref, hb_ref, diff_ref, ext_ref, mix_ref):
        i = pl.program_id(0)
        x = x_ref[...]
        ext_ref[0:POOL_HALO, :] = jnp.where(i == 0, 0.0, halo_ref[...])
        ext_ref[POOL_HALO:, :] = x
        tok = i * tm + lax.broadcasted_iota(jnp.int32, (tm, 1), 0)
        for gi, win in enumerate(POOL_WINDOWS):
            cols = slice(gi * G, (gi + 1) * G)
            xs = x[:, cols]
            s = xs
            for k in range(1, win):
                s = s + ext_ref[pl.ds(POOL_HALO - k, tm), cols]
            count = jnp.minimum(tok + 1, win).astype(F32)
            d = (s / count - xs).astype(BF16)
            diff_ref[:, cols] = d
            mix_ref[:, cols] = jnp.dot(d, pw_ref[gi], preferred_element_type=F32)
        z = ALPHA * x + mix_ref[...] * sc_ref[...]
        z_ref[...] = z
        h = _layer_norm(z, g_ref[...], b_ref[...])
        h_ref[...] = h
        hb_ref[...] = h.astype(BF16)

    row_blk = pl.BlockSpec((tm, D), lambda i: (i, 0))
    vec = pl.BlockSpec((1, D), lambda i: (0, 0))
    halo = pl.BlockSpec((POOL_HALO, D), lambda i: (jnp.maximum(i * (tm // POOL_HALO) - 1, 0), 0))
    return pl.pallas_call(
        body, name=name, grid=(M // tm,),
        in_specs=[row_blk, halo, pl.BlockSpec((n_groups, G, G), lambda i: (0, 0, 0)), vec, vec, vec],
        out_specs=[row_blk] * 4,
        out_shape=[jax.ShapeDtypeStruct((M, D), F32)] * 2 + [jax.ShapeDtypeStruct((M, D), BF16)] * 2,
        scratch_shapes=[pltpu.VMEM((tm + POOL_HALO, D), F32), pltpu.VMEM((tm, D), F32)],
        compiler_params=_cparams(),
    )(h0, h0, pw, scale, g, b)


def _pool_bwd(dz, diff, pw, scale, *, name):
    M, D = dz.shape
    n_groups, G, _ = pw.shape
    tm = _pick(M, ROW_TILES)
    nt = M // tm

    def body(dz_ref, halo_ref, diff_ref, pw_ref, sc_ref, dh_ref, dpw_ref, dsc_ref, ext_ref, q_ref):
        i = pl.program_id(0)
        dz_v = dz_ref[...]
        ext_ref[0:tm, :] = dz_v
        ext_ref[tm:, :] = jnp.where(i == nt - 1, 0.0, halo_ref[...])
        tok = i * tm + lax.broadcasted_iota(jnp.int32, (tm + POOL_HALO, 1), 0)

        @pl.when(i == 0)
        def _():
            dpw_ref[...] = jnp.zeros_like(dpw_ref)
            dsc_ref[...] = jnp.zeros_like(dsc_ref)

        for gi, win in enumerate(POOL_WINDOWS):
            cols = slice(gi * G, (gi + 1) * G)
            dmix = (ext_ref[:, cols] * sc_ref[:, cols]).astype(BF16)
            ddiff = _dot_nt(dmix, pw_ref[gi])
            count = jnp.minimum(tok + 1, win).astype(F32)
            q_ref[:, cols] = ddiff / count
            acc = -ddiff[0:tm]
            for k in range(win):
                acc = acc + q_ref[pl.ds(k, tm), cols]
            dh_ref[:, cols] = ALPHA * dz_v[:, cols] + acc
            d = diff_ref[:, cols]
            dpw_ref[gi] += lax.dot_general(d, dmix[0:tm], (((0,), (0,)), ((), ())), preferred_element_type=F32)
            mixed = jnp.dot(d, pw_ref[gi], preferred_element_type=F32)
            dsc_ref[:, cols] += jnp.sum(dz_v[:, cols] * mixed, axis=0, keepdims=True)

    row_blk = pl.BlockSpec((tm, D), lambda i: (i, 0))
    vec = pl.BlockSpec((1, D), lambda i: (0, 0))
    per_tile = tm // POOL_HALO
    halo = pl.BlockSpec((POOL_HALO, D), lambda i: (jnp.minimum((i + 1) * per_tile, nt * per_tile - 1), 0))
    wblk = pl.BlockSpec((n_groups, G, G), lambda i: (0, 0, 0))
    return pl.pallas_call(
        body, name=name, grid=(nt,),
        in_specs=[row_blk, halo, row_blk, wblk, vec],
        out_specs=[row_blk, wblk, vec],
        out_shape=[jax.ShapeDtypeStruct((M, D), F32), jax.ShapeDtypeStruct((n_groups, G, G), F32),
                   jax.ShapeDtypeStruct((1, D), F32)],
        scratch_shapes=[pltpu.VMEM((tm + POOL_HALO, D), F32), pltpu.VMEM((tm + POOL_HALO, D), F32)],
        compiler_params=_cparams(),
    )(dz, dz, diff, pw, scale)


def _glu_interleave(x):
    s = x.shape
    n = s[-1] // (2 * GLU_CHUNK)
    return jnp.swapaxes(x.reshape(s[:-1] + (2, n, GLU_CHUNK)), -3, -2).reshape(s)


def _glu_deinterleave(x):
    s = x.shape
    n = s[-1] // (2 * GLU_CHUNK)
    return jnp.swapaxes(x.reshape(s[:-1] + (n, 2, GLU_CHUNK)), -3, -2).reshape(s)


W_PACK_ROWS = 256
W_PACK_PIECE = 64


def _interleaved_col(col, F):
    half, within = divmod(col, F)
    chunk, off = divmod(within, GLU_CHUNK)
    return chunk * 2 * GLU_CHUNK + half * GLU_CHUNK + off


def _w_in_grad_segments(dw, *, name):
    D, F2 = dw.shape
    width = F2 // N_DEV
    rows = _pick(D, (W_PACK_ROWS, 128))
    assert width % W_PACK_PIECE == 0

    def body(x_ref, o_ref):
        for j in range(N_DEV):
            for q in range(0, width, W_PACK_PIECE):
                at = _interleaved_col(width * j + q, F2 // 2)
                o_ref[j % 2, j // 2, :, q:q + W_PACK_PIECE] = x_ref[:, at:at + W_PACK_PIECE].astype(BF16)

    return pl.pallas_call(
        body, name=name, grid=(D // rows,),
        in_specs=[pl.BlockSpec((rows, F2), lambda r: (r, 0))],
        out_specs=pl.BlockSpec((2, N_CHIPS, rows, width), lambda r: (0, 0, r, 0)),
        out_shape=jax.ShapeDtypeStruct((2, N_CHIPS, D, width), BF16),
        compiler_params=_cparams(),
    )(dw)


def _w_in_interleaved(gathered, *, name):
    n_dev, depth, D, width = gathered.shape
    F = n_dev * width // 2
    rows = _pick(D, (W_PACK_ROWS, 128))
    assert width % W_PACK_PIECE == 0 and GLU_CHUNK % W_PACK_PIECE == 0 and F % GLU_CHUNK == 0
    interleaved = lambda col: _interleaved_col(col, F)

    def body(x_ref, o_ref):
        for j in range(n_dev):
            for q in range(0, width, W_PACK_PIECE):
                to = interleaved(width * j + q)
                o_ref[0, :, to:to + W_PACK_PIECE] = x_ref[j, 0, :, q:q + W_PACK_PIECE]

    return pl.pallas_call(
        body, name=name, grid=(depth, D // rows),
        in_specs=[pl.BlockSpec((n_dev, 1, rows, width), lambda l, r: (0, l, r, 0))],
        out_specs=pl.BlockSpec((1, rows, n_dev * width), lambda l, r: (l, r, 0)),
        out_shape=jax.ShapeDtypeStruct((depth, D, n_dev * width), gathered.dtype),
        compiler_params=_cparams(),
    )(gathered)


def _conv_glu_fwd(u, cw, cb, *, name):
    M, F2 = u.shape
    tm = _pick(M, ROW_TILES)
    tc = 2 * GLU_CHUNK

    def body(u_ref, halo_ref, w_ref, b_ref, o_ref, c_ref):
        i = pl.program_id(0)
        halo = jnp.where(i == 0, 0.0, halo_ref[...].astype(F32))
        w0, w1, w2, b = w_ref[0:1, :], w_ref[1:2, :], w_ref[2:3, :], b_ref[...]
        for r0 in range(0, tm, CONV_STRIP):
            before = halo if r0 == 0 else u_ref[pl.ds(r0 - CONV_HALO, CONV_HALO), :].astype(F32)
            win = jnp.concatenate([before, u_ref[pl.ds(r0, CONV_STRIP), :].astype(F32)], axis=0)
            u0, u1, u2 = (win[CONV_HALO - k:CONV_HALO - k + CONV_STRIP] for k in range(3))
            c = b + w0 * u2 + w1 * u1 + w2 * u0
            c_ref[pl.ds(r0, CONV_STRIP), :] = c.astype(BF16)
            a, g = c[:, :GLU_CHUNK], c[:, GLU_CHUNK:]
            o_ref[pl.ds(r0, CONV_STRIP), :] = (a * jax.nn.sigmoid(a) * g).astype(BF16)

    per_tile = tm // CONV_HALO
    return pl.pallas_call(
        body, name=name, grid=(M // tm, F2 // tc),
        in_specs=[pl.BlockSpec((tm, tc), lambda i, j: (i, j)),
                  pl.BlockSpec((CONV_HALO, tc), lambda i, j: (jnp.maximum(i * per_tile - 1, 0), j)),
                  pl.BlockSpec((3, tc), lambda i, j: (0, j)), pl.BlockSpec((1, tc), lambda i, j: (0, j))],
        out_specs=[pl.BlockSpec((tm, GLU_CHUNK), lambda i, j: (i, j)), pl.BlockSpec((tm, tc), lambda i, j: (i, j))],
        out_shape=[jax.ShapeDtypeStruct((M, F2 // 2), BF16), jax.ShapeDtypeStruct((M, F2), BF16)],
        compiler_params=_cparams(),
    )(u, u, cw, cb)


def _conv_glu_bwd(u, c, dact, cw, *, name):
    M, F2 = u.shape
    tm = _pick(M, ROW_TILES)
    nt = M // tm
    tc = 2 * GLU_CHUNK

    def body(u_ref, c_ref, da_ref, w_ref, du_ref, dw_ref, db_ref, dcx_ref, carry_ref):
        i = pl.program_id(1)
        w0, w1, w2 = w_ref[0:1, :], w_ref[1:2, :], w_ref[2:3, :]

        @pl.when(i == 0)
        def _():
            dw_ref[...] = jnp.zeros_like(dw_ref)
            db_ref[...] = jnp.zeros_like(db_ref)
            carry_ref[...] = jnp.zeros_like(carry_ref)

        def fold(t):
            return sum(t[r:r + 8] for r in range(0, CONV_STRIP, 8))

        dcx_ref[tm:, :] = carry_ref[...]
        s_b = s_0 = s_1 = s_2 = jnp.zeros((8, tc), F32)
        for r0 in reversed(range(0, tm, CONV_STRIP)):
            rows = pl.ds(r0, CONV_STRIP)
            c_v = c_ref[rows, :].astype(F32)
            a, g = c_v[:, :GLU_CHUNK], c_v[:, GLU_CHUNK:]
            sig = jax.nn.sigmoid(a)
            dact_v = da_ref[rows, :].astype(F32)
            d_a = dact_v * g * (sig * (1.0 + a * (1.0 - sig)))
            d_g = dact_v * (a * sig)
            dc = jnp.concatenate([d_a, d_g], axis=1)
            dcx_ref[rows, :] = dc
            dc1, dc2 = dcx_ref[pl.ds(r0 + 1, CONV_STRIP), :], dcx_ref[pl.ds(r0 + 2, CONV_STRIP), :]
            du_ref[rows, :] = (w2 * dc + w1 * dc1 + w0 * dc2).astype(BF16)
            u_v = u_ref[rows, :].astype(F32)
            s_b, s_0, s_1, s_2 = s_b + fold(dc), s_0 + fold(dc2 * u_v), s_1 + fold(dc1 * u_v), s_2 + fold(dc * u_v)
        carry_ref[...] = dcx_ref[0:CONV_HALO, :]
        db_ref[...] += jnp.sum(s_b, axis=0, keepdims=True)
        dw_ref[0:1, :] += jnp.sum(s_0, axis=0, keepdims=True)
        dw_ref[1:2, :] += jnp.sum(s_1, axis=0, keepdims=True)
        dw_ref[2:3, :] += jnp.sum(s_2, axis=0, keepdims=True)

    per_tile = tm // CONV_HALO
    rev = lambda i: nt - 1 - i
    return pl.pallas_call(
        body, name=name, grid=(F2 // tc, nt),
        in_specs=[pl.BlockSpec((tm, tc), lambda j, i: (rev(i), j)), pl.BlockSpec((tm, tc), lambda j, i: (rev(i), j)),
                  pl.BlockSpec((tm, GLU_CHUNK), lambda j, i: (rev(i), j)),
                  pl.BlockSpec((3, tc), lambda j, i: (0, j))],
        out_specs=[pl.BlockSpec((tm, tc), lambda j, i: (rev(i), j)),
                   pl.BlockSpec((3, tc), lambda j, i: (0, j)), pl.BlockSpec((1, tc), lambda j, i: (0, j))],
        out_shape=[jax.ShapeDtypeStruct((M, F2), BF16), jax.ShapeDtypeStruct((3, F2), F32),
                   jax.ShapeDtypeStruct((1, F2), F32)],
        scratch_shapes=[pltpu.VMEM((tm + CONV_HALO, tc), F32), pltpu.VMEM((CONV_HALO, tc), F32)],
        compiler_params=_cparams(),
    )(u, c, dact, cw)


def _split3(x):
    hi = x.astype(BF16)
    r = x - hi.astype(F32)
    mid = r.astype(BF16)
    lo = (r - mid.astype(F32)).astype(BF16)
    return hi, mid, lo


def _tri_sum(tri, x):
    return sum(jnp.dot(tri, part, preferred_element_type=F32) for part in _split3(x))


def _log_sigmoid(x):
    return jnp.minimum(x, 0.0) - jnp.log1p(jnp.exp(-jnp.abs(x)))


def _forget_cumsum(pre, bias, *, name):
    M, C = pre.shape
    tm = _pick(M, ROW_TILES)

    def body(p_ref, b_ref, c_ref, carry_ref):
        i = pl.program_id(0)

        @pl.when(i == 0)
        def _():
            carry_ref[...] = jnp.zeros_like(carry_ref)

        logf = _log_sigmoid(p_ref[...] + b_ref[...])
        r = lax.broadcasted_iota(jnp.int32, (tm, tm), 0)
        s = lax.broadcasted_iota(jnp.int32, (tm, tm), 1)
        c_ref[...] = _tri_sum((s <= r).astype(BF16), logf) + carry_ref[...]
        carry_ref[...] = c_ref[pl.ds(tm - 1, 1), :]

    return pl.pallas_call(
        body, name=name, grid=(M // tm,),
        in_specs=[pl.BlockSpec((tm, C), lambda i: (i, 0)), pl.BlockSpec((1, C), lambda i: (0, 0))],
        out_specs=pl.BlockSpec((tm, C), lambda i: (i, 0)),
        out_shape=jax.ShapeDtypeStruct((M, C), F32),
        scratch_shapes=[pltpu.VMEM((1, C), F32)],
        compiler_params=_cparams(),
    )(pre, bias)


def _forget_cumsum_bwd(dc, pre, bias, *, name):
    M, C = pre.shape
    tm = _pick(M, ROW_TILES)
    nt = M // tm

    def body(dc_ref, p_ref, b_ref, dp_ref, db_ref, carry_ref, run_ref):
        i = pl.program_id(0)

        @pl.when(i == 0)
        def _():
            carry_ref[...] = jnp.zeros_like(carry_ref)
            db_ref[...] = jnp.zeros_like(db_ref)

        r = lax.broadcasted_iota(jnp.int32, (tm, tm), 0)
        s = lax.broadcasted_iota(jnp.int32, (tm, tm), 1)
        run_ref[...] = _tri_sum((s >= r).astype(BF16), dc_ref[...]) + carry_ref[...]
        carry_ref[...] = run_ref[pl.ds(0, 1), :]
        dpre = run_ref[...] * jax.nn.sigmoid(-(p_ref[...] + b_ref[...]))
        dp_ref[...] = dpre.astype(BF16)
        db_ref[...] += jnp.sum(dpre, axis=0, keepdims=True)

    rev_blk = pl.BlockSpec((tm, C), lambda i: (nt - 1 - i, 0))
    vec = pl.BlockSpec((1, C), lambda i: (0, 0))
    return pl.pallas_call(
        body, name=name, grid=(nt,),
        in_specs=[rev_blk, rev_blk, vec], out_specs=[rev_blk, vec],
        out_shape=[jax.ShapeDtypeStruct((M, C), BF16), jax.ShapeDtypeStruct((1, C), F32)],
        scratch_shapes=[pltpu.VMEM((1, C), F32), pltpu.VMEM((tm, C), F32)],
        compiler_params=_cparams(),
    )(dc, pre, bias)


def _causal_mask(tm):
    key = lax.broadcasted_iota(jnp.int32, (tm, tm), 0)
    query = lax.broadcasted_iota(jnp.int32, (tm, tm), 1)
    return key <= query


def _dot_nt(a, b):
    return lax.dot_general(a, b, (((1,), (1,)), ((), ())), preferred_element_type=F32)


def _loop_unrolled(n, step, init, unroll):
    def trip(p, carry):
        for r in range(unroll):
            carry = step(unroll * p + r, carry)
        return carry
    carry = lax.fori_loop(0, n // unroll, trip, init)
    return lax.fori_loop(unroll * (n // unroll), n, step, carry)


def _rows(main, extras, total):
    tm = main.shape[1]
    used = sum(e.shape[0] for e in extras)
    tile = jnp.concatenate([e.astype(BF16) for e in extras] + [jnp.zeros((ATTN_EXTRA - used, tm), BF16)], axis=0)
    rest = total - main.shape[0] - ATTN_EXTRA
    return jnp.concatenate([main, tile] + ([jnp.zeros((rest, tm), BF16)] if rest else []), axis=0)


def _attn_specs(H, M, dh, tm):
    nt = M // tm
    qT_blk = pl.BlockSpec((1, dh, tm), lambda h, i: (h, 0, i))
    row_blk = pl.BlockSpec((1, 1, tm), lambda h, i: (h, 0, i))
    key_blk = pl.BlockSpec((1, M, LANES), lambda h, i: (h, 0, 0))
    keyT_blk = pl.BlockSpec((1, nt, dh + ATTN_EXTRA, tm), lambda h, i: (h, 0, 0, 0))
    return qT_blk, row_blk, key_blk, keyT_blk


def _pair_specs(M, dh, tm):
    assert 2 * dh == LANES
    cols = lambda section: pl.BlockSpec((tm, LANES), lambda p, i: (i, section + p))
    headsT = pl.BlockSpec((2, dh, tm), lambda p, i: (p, 0, i))
    return cols, headsT


def _attn_pack(qkv, c, H, *, name):
    M, D3 = qkv.shape
    D = D3 // 3
    dh = D // H
    tm = _pick(M, ROW_TILES)
    nt = M // tm
    cols, headsT = _pair_specs(M, dh, tm)
    key_blk = pl.BlockSpec((2, tm, LANES), lambda p, i: (p, i, 0))
    keyT_blk = pl.BlockSpec((2, 1, dh + ATTN_EXTRA, tm), lambda p, i: (p, i, 0, 0))

    def body(k_ref, v_ref, q_ref, c_ref, ka_ref, va_ref, qT_ref, kT_ref, vT_ref):
        p = pl.program_id(0)
        lane = lax.broadcasted_iota(jnp.int32, (tm, LANES), 1)
        k_v, v_v = k_ref[...], v_ref[...]
        ones_v = jnp.where((lane >= dh) & (lane < dh + 3), 1.0, 0.0)
        for e in range(2):
            ck = jnp.sum(jnp.where(lane == 2 * p + e, c_ref[...], 0.0), axis=1, keepdims=True)
            hi, mid, lo = _split3(-ck)
            extra = jnp.where(lane == dh + 3, hi.astype(F32), jnp.where(lane == dh + 4, mid.astype(F32),
                              jnp.where(lane == dh + 5, lo.astype(F32), ones_v)))
            first = lambda t: t if e == 0 else pltpu.roll(t, dh, 1)
            ka_ref[e] = jnp.where(lane < dh, first(k_v), extra.astype(BF16))
            va_ref[e] = jnp.where(lane < dh, first(v_v), ones_v.astype(BF16))
        row = lax.broadcasted_iota(jnp.int32, (ATTN_EXTRA, tm), 0)
        tail = jnp.where(row == 0, 1.0, 0.0).astype(BF16)
        for src, dst in ((k_v, kT_ref), (v_v, vT_ref)):
            t = src.T
            for e in range(2):
                dst[e, 0, 0:dh, :] = t[e * dh:(e + 1) * dh]
                dst[e, 0, dh:, :] = tail
        qT_ref[...] = q_ref[...].T.reshape(2, dh, tm)

    n_sec = D // LANES
    return pl.pallas_call(
        body, name=name, grid=(H // 2, nt),
        in_specs=[cols(0), cols(n_sec), cols(2 * n_sec), pl.BlockSpec((tm, LANES), lambda p, i: (i, 0))],
        out_specs=[key_blk, key_blk, headsT, keyT_blk, keyT_blk],
        out_shape=[jax.ShapeDtypeStruct((H, M, LANES), BF16)] * 2 + [jax.ShapeDtypeStruct((H, dh, M), BF16)]
        + [jax.ShapeDtypeStruct((H, nt, dh + ATTN_EXTRA, tm), BF16)] * 2,
        compiler_params=_cparams(),
    )(qkv, qkv, qkv, c)


def _heads_merge(parts, M, tm, *, name):
    H = parts[0].shape[0]
    dh = LANES // 2
    blocked = [t.ndim == 4 for t in parts]
    cols, headsT = _pair_specs(M, dh, tm)

    def body(*refs):
        for src, dst, b in zip(refs[:len(parts)], refs[len(parts):], blocked):
            t = jnp.concatenate([src[e, 0, 0:dh, :] for e in range(2)], axis=0) if b else src[...].reshape(2 * dh, tm)
            dst[...] = t.astype(BF16).T

    in_specs = [pl.BlockSpec((2, 1, t.shape[2], tm), lambda p, i: (p, i, 0, 0)) if b else headsT
                for t, b in zip(parts, blocked)]
    return pl.pallas_call(
        body, name=name, grid=(H // 2, M // tm), in_specs=in_specs, out_specs=[cols(0)] * len(parts),
        out_shape=[jax.ShapeDtypeStruct((M, H * dh), BF16)] * len(parts), compiler_params=_cparams(),
    )(*parts)


def _attn_fwd(qT, cq, k_aug, vT_aug, *, name):
    H, dh, M = qT.shape
    tm = vT_aug.shape[-1]
    qT_blk, row_blk, key_blk, keyT_blk = _attn_specs(H, M, dh, tm)

    def body(qT_ref, cq_ref, k_ref, vT_ref, oT_ref, lse_ref):
        i = pl.program_id(1)
        ones = jnp.ones((3, tm), BF16)
        qa = _rows(qT_ref[0] * jnp.asarray(dh ** -0.5, BF16), [*_split3(cq_ref[0]), ones], LANES)

        def block(j, carry, masked):
            m, acc = carry
            keys = pl.ds(pl.multiple_of(j * tm, tm), tm)
            sT = jnp.dot(k_ref[0, keys, :], qa, preferred_element_type=F32)
            if masked:
                sT = jnp.where(_causal_mask(tm), sT, NEG_INF)
            m_new = jnp.maximum(m, jnp.max(sT, axis=0, keepdims=True))
            pT = jnp.exp(sT - m_new).astype(BF16)
            acc = jnp.exp(m - m_new) * acc + jnp.dot(vT_ref[0, j], pT, preferred_element_type=F32)
            return m_new, acc

        init = (jnp.full((1, tm), NEG_INF, F32), jnp.zeros((dh + ATTN_EXTRA, tm), F32))
        m, acc = block(i, _loop_unrolled(i, lambda j, c: block(j, c, False), init, 4), True)
        l = acc[dh:dh + 1, :]
        oT_ref[0] = (acc[0:dh, :] / l).astype(BF16)
        lse_ref[0] = m + jnp.log(l)

    return pl.pallas_call(
        body, name=name, grid=(H, M // tm),
        in_specs=[qT_blk, row_blk, key_blk, keyT_blk], out_specs=[qT_blk, row_blk],
        out_shape=[jax.ShapeDtypeStruct((H, dh, M), BF16), jax.ShapeDtypeStruct((H, 1, M), F32)],
        compiler_params=_cparams(),
    )(qT, cq, k_aug, vT_aug)


def _attn_bwd(qT, oT, doT, lse, cq, k_aug, v_aug, kT_aug, *, name):
    H, dh, M = qT.shape
    tm = kT_aug.shape[-1]
    qT_blk, row_blk, key_blk, keyT_blk = _attn_specs(H, M, dh, tm)
    dvT_blk = pl.BlockSpec((1, M // tm, dh, tm), lambda h, i: (h, 0, 0, 0))

    def body(qT_ref, oT_ref, doT_ref, lse_ref, cq_ref, k_ref, v_ref, kT_ref, dqT_ref, dcq_ref, dkT_ref, dvT_ref):
        i = pl.program_id(1)

        @pl.when(i == 0)
        def _():
            dkT_ref[...] = jnp.zeros_like(dkT_ref)
            dvT_ref[...] = jnp.zeros_like(dvT_ref)

        qsT = qT_ref[0] * jnp.asarray(dh ** -0.5, BF16)
        doT = doT_ref[0]
        delta = jnp.sum(doT.astype(F32) * oT_ref[0].astype(F32), axis=0, keepdims=True)
        ones = jnp.ones((3, tm), BF16)
        qa = _rows(qsT, [*_split3(cq_ref[0] - lse_ref[0]), ones], LANES)
        da = _rows(doT, _split3(-delta), LANES)
        q1 = _rows(qsT, [ones[0:1]], dh + ATTN_EXTRA)

        def block(j, dq, masked):
            keys = pl.ds(pl.multiple_of(j * tm, tm), tm)
            pT = jnp.exp(jnp.dot(k_ref[0, keys, :], qa, preferred_element_type=F32))
            if masked:
                pT = jnp.where(_causal_mask(tm), pT, 0.0)
            ds_b = (pT * jnp.dot(v_ref[0, keys, :], da, preferred_element_type=F32)).astype(BF16)
            dvT_ref[0, j] += _dot_nt(doT, pT.astype(BF16))
            dkT_ref[0, j] += _dot_nt(q1, ds_b)
            return dq + jnp.dot(kT_ref[0, j], ds_b, preferred_element_type=F32)

        init = jnp.zeros((dh + ATTN_EXTRA, tm), F32)
        dq = block(i, _loop_unrolled(i, lambda j, c: block(j, c, False), init, 4), True)
        dqT_ref[0] = (dq[0:dh, :] * dh ** -0.5).astype(BF16)
        dcq_ref[0] = dq[dh:dh + 1, :]

    return pl.pallas_call(
        body, name=name, grid=(H, M // tm),
        in_specs=[qT_blk, qT_blk, qT_blk, row_blk, row_blk, key_blk, key_blk, keyT_blk],
        out_specs=[qT_blk, row_blk, keyT_blk, dvT_blk],
        out_shape=[jax.ShapeDtypeStruct((H, dh, M), BF16), jax.ShapeDtypeStruct((H, 1, M), F32),
                   jax.ShapeDtypeStruct(kT_aug.shape, F32), jax.ShapeDtypeStruct((H, M // tm, dh, tm), F32)],
        compiler_params=_cparams(),
    )(qT, oT, doT, lse, cq, k_aug, v_aug, kT_aug)


def kernel(x, meta, pool_w, pool_scale, w_kv, w_f, b_f, w_q, w_o, ffn_w_in, ffn_conv_w, ffn_conv_b, ffn_w_out, ln_g, ln_b, loss_target, m_meta, m_pool_w, m_pool_scale, m_w_kv, m_w_f, m_b_f, m_w_q, m_w_o, m_ffn_w_in, m_ffn_conv_w, m_ffn_conv_b, m_ffn_w_out, m_ln_g, m_ln_b, v_meta, v_pool_w, v_pool_scale, v_w_kv, v_w_f, v_b_f, v_w_q, v_w_o, v_ffn_w_in, v_ffn_conv_w, v_ffn_conv_b, v_ffn_w_out, v_ln_g, v_ln_b):
    local = dict(meta=meta, pool_w=pool_w, pool_scale=pool_scale, w_kv=w_kv, w_f=w_f, b_f=b_f, w_q=w_q, w_o=w_o,
                 ffn_w_in=ffn_w_in, ffn_conv_w=ffn_conv_w, ffn_conv_b=ffn_conv_b, ffn_w_out=ffn_w_out,
                 ln_g=ln_g, ln_b=ln_b)
    mom1 = dict(meta=m_meta, pool_w=m_pool_w, pool_scale=m_pool_scale, w_kv=m_w_kv, w_f=m_w_f, b_f=m_b_f,
                w_q=m_w_q, w_o=m_w_o, ffn_w_in=m_ffn_w_in, ffn_conv_w=m_ffn_conv_w, ffn_conv_b=m_ffn_conv_b,
                ffn_w_out=m_ffn_w_out, ln_g=m_ln_g, ln_b=m_ln_b)
    mom2 = dict(meta=v_meta, pool_w=v_pool_w, pool_scale=v_pool_scale, w_kv=v_w_kv, w_f=v_w_f, b_f=v_b_f,
                w_q=v_w_q, w_o=v_w_o, ffn_w_in=v_ffn_w_in, ffn_conv_w=v_ffn_conv_w, ffn_conv_b=v_ffn_conv_b,
                ffn_w_out=v_ffn_w_out, ln_g=v_ln_g, ln_b=v_ln_b)
    axis_of = dict(PARAMS)

    S, D = x.shape[1], x.shape[2]
    H = b_f.shape[0]
    dh = D // H
    n_tok = N_META + S
    M = _round_up(n_tok, ROW_ALIGN)
    tm = _pick(M, ROW_TILES)
    nt = M // tm
    F2 = ffn_conv_b.shape[1]
    F = F2 // 2
    depth = ffn_conv_b.shape[0]

    small_sharded = [n for n in SMALL if axis_of[n] is not None]
    got = _all_gather([local[n].astype(BF16) for n in BIG] + [_pack([local[n] for n in small_sharded])],
                      "gather_weights")
    wb = {n: _from_blocks(blk, axis_of[n]) for n, blk in zip(BIG, got) if n != "ffn_w_in"}
    small_blocks = _unpack(got[-1], [local[n].shape for n in small_sharded], lead=(N_DEV,))
    wf32 = {n: _from_blocks(blk, axis_of[n]) for n, blk in zip(small_sharded, small_blocks)}

    pw = wb["pool_w"][0]
    wf_pad = jnp.pad(wf32["w_f"].astype(BF16), ((0, 0), (0, LANES - H)))
    w_qkv = jnp.concatenate([wb["w_kv"], wb["w_q"][0]], axis=1)
    w_att = jnp.concatenate([w_qkv, wf_pad], axis=1)
    wo = wb["w_o"][0]
    w_in = _w_in_interleaved(got[BIG.index("ffn_w_in")], name="w_in_interleave")
    w_out = wb["ffn_w_out"]
    conv_w = _glu_interleave(wf32["ffn_conv_w"])
    conv_b = _glu_interleave(ffn_conv_b)[:, None, :]
    scale = wf32["pool_scale"]
    g_ln, b_ln = wf32["ln_g"], wf32["ln_b"]
    ln = lambda i, j: (g_ln[i, j][None, :], b_ln[i, j][None, :])
    bias_f = jnp.pad(b_f, (0, LANES - H))[None, :]

    pad_rows = M - n_tok
    h0 = jnp.concatenate([wf32["meta"], x[0], jnp.zeros((pad_rows, D), F32)], axis=0)
    target = jnp.concatenate([jnp.zeros((N_META, D), F32), loss_target[0], jnp.zeros((pad_rows, D), F32)], axis=0)

    z1, h1, h1b, diff = _pool_fwd(h0, pw, scale, *ln(0, 0), name="pool_fwd")
    u0 = _mm(h1b, w_in[0], name="ffn0_up", out_dtype=BF16)
    act0, c0 = _conv_glu_fwd(u0, conv_w[0], conv_b[0], name="ffn0_conv")
    z2, h2, h2b = _mm_ln(act0, w_out[0], h1, *ln(0, 1), name="ffn0_down_ln")

    qkv = _mm(h2b, w_qkv, name="attn_qkv", out_dtype=BF16)
    pre = _mm(h2b, wf_pad, name="attn_gate", out_dtype=F32)
    c = _forget_cumsum(pre, bias_f, name="attn_cumsum")
    k_aug, v_aug, qT_h, kT_aug, vT_aug = _attn_pack(qkv, c, H, name="attn_pack")
    cq = c[:, :H].T[:, None, :]
    oT_h, lse = _attn_fwd(qT_h, cq, k_aug, vT_aug, name="attn_fwd")
    oT = oT_h.reshape(D, M)
    z3, h3, h3b = _mm_ln(oT, wo, h2, *ln(1, 0), name="attn_out_ln", a_transposed=True)

    u1 = _mm(h3b, w_in[1], name="ffn1_up", out_dtype=BF16)
    act1, c1 = _conv_glu_fwd(u1, conv_w[1], conv_b[1], name="ffn1_conv")
    dz4, dz4b, loss_part, dg11, db11 = _mm_ln(act1, w_out[1], h3, *ln(1, 1), name="ffn1_down_loss", target=target,
                                              n_tok=n_tok)
    loss = lax.psum(loss_part[0, 0], ("x", "y", "c"))

    grads = {}

    def ffn_bwd(layer, dz, dz_b, u, c_conv, act, h_in_b, z_in, g_in, tag):
        dact = _mm(dz_b, w_out[layer], name=tag + "_dact", out_dtype=BF16, b_transposed=True)
        du, dcw, dcb = _conv_glu_bwd(u, c_conv, dact, conv_w[layer], name=tag + "_conv_bwd")
        dz_in = _mm_ln_bwd(du, w_in[layer], dz, z_in, g_in[None, :], name=tag + "_dh_ln_bwd")
        d_w_out = _mm_tn(act, dz_b, name=tag + "_dw_out")
        d_w_in = _w_in_grad_segments(_mm_tn(h_in_b, du, name=tag + "_dw_in"), name=tag + "_dw_in_segments")
        return dz_in, d_w_in, d_w_out, _glu_deinterleave(dcw), _glu_deinterleave(dcb)[0]

    (dz3, dz3b, dg10, db10), dwin1, dwout1, dcw1, dcb1 = ffn_bwd(1, dz4, dz4b, u1, c1, act1, h3b, z3, g_ln[1, 0], "ffn1")

    doT_h = _mm(wo, dz3b, name="attn_do", out_dtype=BF16, b_transposed=True).reshape(H, dh, M)
    grads["w_o"] = _mm(oT, dz3b, name="attn_dw_o", out_dtype=F32)[None]
    dqT_h, dcq, dkT_a, dvT_h = _attn_bwd(qT_h, oT_h, doT_h, lse, cq, k_aug, v_aug, kT_aug, name="attn_bwd")
    dck = -dkT_a[:, :, dh, :].reshape(H, M)
    dc = jnp.pad((dcq[:, 0, :] + dck).T, ((0, 0), (0, LANES - H)))
    dpre, dbias = _forget_cumsum_bwd(dc, pre, bias_f, name="attn_cumsum_bwd")
    d_att = jnp.concatenate([*_heads_merge([dkT_a, dvT_h, dqT_h], M, tm, name="attn_dqkv_merge"), dpre], axis=1)
    dz2, dz2b, dg01, db01 = _mm_ln_bwd(d_att, w_att, dz3, z2, g_ln[0, 1][None, :], name="attn_dh_ln_bwd")
    d_w_att = _mm_tn(h2b, d_att, name="attn_dw_qkv")
    grads["w_kv"] = d_w_att[:, :2 * D]
    grads["w_q"] = d_w_att[:, 2 * D:3 * D][None]
    grads["w_f"] = d_w_att[:, 3 * D:3 * D + H]
    grads["b_f"] = dbias[0, :H]

    (dz1, _, dg00, db00), dwin0, dwout0, dcw0, dcb0 = ffn_bwd(0, dz2, dz2b, u0, c0, act0, h1b, z1, g_ln[0, 0], "ffn0")
    dh0, dpw, dscale = _pool_bwd(dz1, diff, pw, scale, name="pool_bwd")

    grads["meta"] = dh0[:N_META]
    grads["pool_w"] = dpw[None]
    grads["pool_scale"] = dscale
    seg_w_in = jnp.stack([dwin0, dwin1], axis=2)
    grads["ffn_w_out"] = jnp.stack([dwout0, dwout1])
    grads["ffn_conv_w"] = jnp.stack([dcw0, dcw1])
    grads["ffn_conv_b"] = jnp.stack([dcb0, dcb1])
    grads["ln_g"] = jnp.stack([jnp.stack([dg00[0], dg01[0]]), jnp.stack([dg10[0], dg11[0]])])
    grads["ln_b"] = jnp.stack([jnp.stack([db00[0], db01[0]]), jnp.stack([db10[0], db11[0]])])
    grad_x = dh0[N_META:n_tok][None]

    blocks = {n: _to_blocks(grads[n], axis_of[n]) for n, _ in PARAMS if n != "ffn_w_in"}
    me = _coords()
    core, chip = (jnp.reshape(v, (1,)).astype(jnp.int32) for v in (me[2], 2 * me[0] + me[1]))
    by_core = lambda t: jnp.swapaxes(t.reshape((N_CHIPS, 2) + t.shape[1:]), 0, 1).astype(BF16)
    segs = [seg_w_in if n == "ffn_w_in" else by_core(blocks[n]) for n in BIG]
    pair = _exchange(segs, "grad_pair_exchange", (1,), lambda pos: pos[2])
    rows_of = lambda t, lead: t.reshape(lead + (-1, t.shape[-1]))
    partial = [_pair_sum(rows_of(s, (2,)), core, rows_of(p[0], ()), name="grad_pair_sum_" + n).reshape(s.shape[1:])
               for n, s, p in zip(BIG, segs, pair)]
    recv = _exchange(partial, "grad_chip_exchange", (4, 2, 6), lambda pos: 2 * pos[0] + pos[1])
    recv_small = _exchange([_pack([blocks[n] for n in SMALL], lead=(N_DEV,))], "grad_small_exchange",
                           tuple(range(1, N_DEV)), _index, by_sender=True)[0]
    results = {}
    for n, own, r in zip(BIG, partial, recv):
        shape = local[n].shape
        as2d = lambda t: t.reshape(-1, shape[-1])
        outs = _adamw(rows_of(own, (N_CHIPS,)), chip, rows_of(r, (N_CHIPS - 1,)), as2d(local[n]), as2d(mom1[n]),
                      as2d(mom2[n]), name="adamw_" + n)
        results[n] = [o_.reshape(shape) for o_ in outs]
    outs = _adamw(recv_small, jnp.zeros((1,), jnp.int32), recv_small[1:],
                  *[_pack([d[n] for n in SMALL]) for d in (local, mom1, mom2)], name="adamw_small")
    small_out = [_unpack(o_, [local[n].shape for n in SMALL]) for o_ in outs]
    for i, n in enumerate(SMALL):
        results[n] = [small_out[k][i] for k in range(4)]
    return (loss, grad_x, *[results[n][k] for k in range(4) for n, _ in PARAMS])
```

```python
import jax
import jax.numpy as jnp
from jax import lax
from jax.experimental import pallas as pl
from jax.experimental.pallas import tpu as pltpu

F32, BF16 = jnp.float32, jnp.bfloat16
MESH = pl.DeviceIdType.MESH

N_DEV = 8
N_CHIPS = 4
N_META = 16
POOL_WINDOWS = (2, 4, 8, 16)
POOL_HALO = 16
CONV_HALO = 16
CONV_STRIP = 32
ALPHA = 4.0 ** 0.25
LN_EPS = 1e-5
NEG_INF = -1e30
ADAM_LR, ADAM_B1, ADAM_B2, ADAM_EPS, ADAM_WD, ADAM_STEP = 0.001, 0.9, 0.999, 1e-08, 0.01, 10

LANES = 128
ROW_ALIGN = 128
ROW_TILES = (640, 512, 128)
CONV_ROW_TILES = (1664,) + ROW_TILES
PACK_COLS = 1024
ADAM_TILE_BYTES = 4 << 20
MM_TN_WINDOW_BYTES = 28 << 20
GLU_CHUNK = 256
ATTN_EXTRA = 16
VMEM_LIMIT = 56 * 1024 * 1024

PARAMS = (("meta", 1), ("pool_w", 2), ("pool_scale", 1), ("w_kv", 1), ("w_f", 0), ("b_f", None),
          ("w_q", 1), ("w_o", 1), ("ffn_w_in", 2), ("ffn_conv_w", 2), ("ffn_conv_b", None),
          ("ffn_w_out", 1), ("ln_g", 2), ("ln_b", 2))
BIG = ("pool_w", "w_kv", "w_q", "w_o", "ffn_w_in", "ffn_w_out")
SMALL = ("meta", "pool_scale", "w_f", "b_f", "ffn_conv_w", "ffn_conv_b", "ln_g", "ln_b")


def _cparams(**kw):
    return pltpu.CompilerParams(vmem_limit_bytes=VMEM_LIMIT, **kw)


def _pick(n, cands):
    for c in cands:
        if n % c == 0:
            return c
    return n


def _round_up(n, m):
    return (n + m - 1) // m * m


def _pack(pieces, lead=()):
    flat = []
    for p in pieces:
        v = p.reshape(lead + (-1,))
        flat.append(jnp.pad(v, [(0, 0)] * len(lead) + [(0, _round_up(v.shape[-1], PACK_COLS) - v.shape[-1])]))
    v = jnp.concatenate(flat, axis=-1)
    rows = _round_up(v.shape[-1] // PACK_COLS, 8)
    v = jnp.pad(v, [(0, 0)] * len(lead) + [(0, rows * PACK_COLS - v.shape[-1])])
    return v.reshape(lead + (rows, PACK_COLS))


def _unpack(buf, shapes, lead=()):
    flat = buf.reshape(lead + (-1,))
    out, off = [], 0
    for s in shapes:
        n = 1
        for d in s:
            n *= d
        out.append(flat[..., off:off + n].reshape(lead + tuple(s)))
        off += _round_up(n, PACK_COLS)
    return out


def _to_blocks(full, axis):
    if axis is None:
        return jnp.broadcast_to(full[None], (N_DEV,) + full.shape)
    s = full.shape
    x = full.reshape(s[:axis] + (N_DEV, s[axis] // N_DEV) + s[axis + 1:])
    return jnp.moveaxis(x, axis, 0)


def _from_blocks(blocks, axis):
    x = jnp.moveaxis(blocks, 0, axis)
    s = x.shape
    return x.reshape(s[:axis] + (s[axis] * s[axis + 1],) + s[axis + 2:])


def _coords():
    return lax.axis_index("x"), lax.axis_index("y"), lax.axis_index("c")


def _flip(pos, mask):
    x, y, c = pos
    return (1 - x if mask & 4 else x, 1 - y if mask & 2 else y, 1 - c if mask & 1 else c)


def _index(pos):
    x, y, c = pos
    return 4 * x + 2 * y + c


def _comm_call(body, name, arrays, out_shapes):
    n = len(arrays)
    hbm = pl.BlockSpec(memory_space=pl.ANY)
    return pl.pallas_call(
        body, name=name, out_shape=out_shapes, in_specs=[hbm] * n, out_specs=[hbm] * n,
        scratch_shapes=[pltpu.SemaphoreType.DMA((7 * n,)), pltpu.SemaphoreType.DMA((7 * n,)),
                        pltpu.SemaphoreType.DMA((n,))],
    )(*arrays)


def _all_gather(blocks, name):
    chip_masks = (4, 2, 6)
    n = len(blocks)

    def body(*refs):
        x_refs, out_refs = refs[:n], refs[n:2 * n]
        send_sems, recv_sems, local_sems = refs[2 * n:]
        me = _coords()
        sibling = _flip(me, 1)

        def copy(a, k, owner, to, from_input=False):
            slot = out_refs[a].at[_index(owner)]
            return pltpu.make_async_remote_copy(
                src_ref=x_refs[a] if from_input else slot, dst_ref=slot,
                send_sem=send_sems.at[7 * a + k], recv_sem=recv_sems.at[7 * a + k],
                device_id=to, device_id_type=MESH)

        mine = [pltpu.make_async_copy(x_refs[a], out_refs[a].at[_index(me)], local_sems.at[a]) for a in range(n)]
        first = [copy(a, 0, me, sibling, True) for a in range(n)]
        first += [copy(a, 1 + j, me, _flip(me, m), True) for j, m in enumerate(chip_masks) for a in range(n)]
        for cp in mine + first:
            cp.start()
        passed = []
        for j, m in enumerate(chip_masks):
            for a in range(n):
                copy(a, 1 + j, _flip(me, m), me).wait_recv()
                passed.append(copy(a, 4 + j, _flip(me, m), sibling))
                passed[-1].start()
        for a in range(n):
            copy(a, 0, sibling, me).wait_recv()
            for j, m in enumerate(chip_masks):
                copy(a, 4 + j, _flip(sibling, m), me).wait_recv()
        for cp in first + passed:
            cp.wait_send()
        for cp in mine:
            cp.wait()

    return _comm_call(body, name, blocks, [jax.ShapeDtypeStruct((N_DEV,) + b.shape, b.dtype) for b in blocks])


def _exchange(segs, name, masks, slot_of, by_sender=False):
    n = len(segs)

    def body(*refs):
        seg_refs, out_refs = refs[:n], refs[n:2 * n]
        send_sems, recv_sems, local_sems = refs[2 * n:]
        me = _coords()

        def copy(a, k, sender):
            to = _flip(sender, masks[k])
            return pltpu.make_async_remote_copy(
                src_ref=seg_refs[a].at[slot_of(to)], dst_ref=out_refs[a].at[slot_of(sender) if by_sender else k],
                send_sem=send_sems.at[7 * a + k], recv_sem=recv_sems.at[7 * a + k],
                device_id=to, device_id_type=MESH)

        mine = [pltpu.make_async_copy(seg_refs[a].at[slot_of(me)], out_refs[a].at[slot_of(me)], local_sems.at[a])
                for a in range(n)] if by_sender else []
        sends = [copy(a, k, me) for k in range(len(masks)) for a in range(n)]
        for cp in mine + sends:
            cp.start()
        for k, mask in enumerate(masks):
            for a in range(n):
                copy(a, k, _flip(me, mask)).wait_recv()
        for cp in sends:
            cp.wait_send()
        for cp in mine:
            cp.wait()

    slots = lambda s: s.shape[0] if by_sender else len(masks)
    return _comm_call(body, name, segs, [jax.ShapeDtypeStruct((slots(s),) + s.shape[1:], s.dtype) for s in segs])


def _own_slot(tr, cols):
    return pl.BlockSpec((1, tr, cols), lambda i, slot: (slot[0], i, 0))


def _pair_sum(seg, slot, recv, name):
    _, rows, cols = seg.shape
    tr = max(t for t in range(16, rows + 1, 16) if rows % t == 0 and 2 * t * cols * 2 <= ADAM_TILE_BYTES)

    def body(slot_ref, own_ref, recv_ref, o_ref):
        o_ref[...] = (own_ref[0].astype(F32) + recv_ref[...].astype(F32)).astype(BF16)

    tile = pl.BlockSpec((tr, cols), lambda i, slot: (i, 0))
    return pl.pallas_call(
        body, name=name,
        grid_spec=pltpu.PrefetchScalarGridSpec(
            num_scalar_prefetch=1, grid=(rows // tr,), in_specs=[_own_slot(tr, cols), tile], out_specs=tile),
        out_shape=jax.ShapeDtypeStruct((rows, cols), BF16),
        compiler_params=_cparams(),
    )(slot, seg, recv)


def _adamw(seg, slot, recv, w, m, v, name):
    rows, cols = w.shape
    n_slots = recv.shape[0]
    sublanes = 8 * (4 // recv.dtype.itemsize)
    tr = max(t for t in range(sublanes, rows + 1, sublanes)
             if rows % t == 0 and N_DEV * t * cols * 4 <= ADAM_TILE_BYTES)
    c1 = 1.0 - ADAM_B1 ** ADAM_STEP
    c2 = 1.0 - ADAM_B2 ** ADAM_STEP

    def body(slot_ref, own_ref, r_ref, w_ref, m_ref, v_ref, g_out, d_out, m_out, v_out):
        g = own_ref[0].astype(F32)
        for s in range(n_slots):
            g = g + r_ref[s].astype(F32)
        m_new = ADAM_B1 * m_ref[...] + (1.0 - ADAM_B1) * g
        v_new = ADAM_B2 * v_ref[...] + (1.0 - ADAM_B2) * (g * g)
        m_hat = m_new / c1
        v_hat = v_new / c2
        g_out[...] = g
        d_out[...] = -ADAM_LR * (m_hat / (jnp.sqrt(v_hat) + ADAM_EPS) + ADAM_WD * w_ref[...])
        m_out[...] = m_new
        v_out[...] = v_new

    tile = pl.BlockSpec((tr, cols), lambda i, slot: (i, 0))
    return pl.pallas_call(
        body, name=name,
        grid_spec=pltpu.PrefetchScalarGridSpec(
            num_scalar_prefetch=1, grid=(rows // tr,),
            in_specs=[_own_slot(tr, cols), pl.BlockSpec((n_slots, tr, cols), lambda i, slot: (0, i, 0)),
                      tile, tile, tile],
            out_specs=[tile] * 4),
        out_shape=[jax.ShapeDtypeStruct(w.shape, F32)] * 4,
        compiler_params=_cparams(),
    )(slot, seg, recv, w, m, v)


def _mm(a, b, *, name, out_dtype, b_transposed=False):
    M, K = a.shape
    N = b.shape[0] if b_transposed else b.shape[1]
    tm = _pick(M, ROW_TILES)
    tn = _pick(N, (2816, 1024, 640, 512, 256, 128))
    tk = K if K <= 3200 else _pick(K, (2816, 1024, 640, 512, 256, 128))
    nk = K // tk

    def body(*refs):
        a_ref, b_ref, o_ref = refs[:3]
        acc_ref = refs[-1] if nk > 1 else None
        if b_transposed:
            prod = _dot_nt(a_ref[...], b_ref[...])
        else:
            prod = jnp.dot(a_ref[...], b_ref[...], preferred_element_type=F32)
        if nk == 1:
            o_ref[...] = prod.astype(out_dtype)
        else:
            k = pl.program_id(2)

            @pl.when(k == 0)
            def _():
                acc_ref[...] = prod

            @pl.when(k > 0)
            def _():
                acc_ref[...] += prod

            @pl.when(k == nk - 1)
            def _():
                o_ref[...] = acc_ref[...].astype(out_dtype)

    b_spec = (pl.BlockSpec((tn, tk), lambda i, j, k: (j, k)) if b_transposed
              else pl.BlockSpec((tk, tn), lambda i, j, k: (k, j)))
    return pl.pallas_call(
        body, name=name, grid=(M // tm, N // tn, nk),
        in_specs=[pl.BlockSpec((tm, tk), lambda i, j, k: (i, k)), b_spec],
        out_specs=pl.BlockSpec((tm, tn), lambda i, j, k: (i, j)),
        out_shape=jax.ShapeDtypeStruct((M, N), out_dtype),
        scratch_shapes=[pltpu.VMEM((tm, tn), F32)] if nk > 1 else [],
        compiler_params=_cparams(),
    )(a, b)


def _mm_tn(a, b, *, name):
    T, M = a.shape
    N = b.shape[1]
    tm = _pick(M, (1408, 1024, 512, 256, 128))
    tn = _pick(N, (1024, 640, 512, 256, 128))
    tt = _pick(T, [t for t in (4160, 1664, 640) if 2 * 2 * t * (tm + tn) <= MM_TN_WINDOW_BYTES] + [128])

    def body(a_ref, b_ref, o_ref):
        prod = lax.dot_general(a_ref[...], b_ref[...], (((0,), (0,)), ((), ())), preferred_element_type=F32)

        @pl.when(pl.program_id(2) == 0)
        def _():
            o_ref[...] = prod

        @pl.when(pl.program_id(2) > 0)
        def _():
            o_ref[...] += prod

    return pl.pallas_call(
        body, name=name, grid=(M // tm, N // tn, T // tt),
        in_specs=[pl.BlockSpec((tt, tm), lambda i, j, k: (k, i)), pl.BlockSpec((tt, tn), lambda i, j, k: (k, j))],
        out_specs=pl.BlockSpec((tm, tn), lambda i, j, k: (i, j)),
        out_shape=jax.ShapeDtypeStruct((M, N), F32),
        compiler_params=_cparams(),
    )(a, b)


def _layer_norm(z, g, b):
    mu = jnp.mean(z, axis=-1, keepdims=True)
    xc = z - mu
    var = jnp.mean(xc * xc, axis=-1, keepdims=True)
    return xc * lax.rsqrt(var + LN_EPS) * g + b


def _mm_ln(a, w, res, g, b, *, name, target=None, n_tok=None, a_transposed=False):
    K, M = a.shape if a_transposed else a.shape[::-1]
    D = w.shape[1]
    tm = _pick(M, ROW_TILES)
    final = target is not None

    def body(*refs):
        if final:
            a_ref, w_ref, r_ref, g_ref, b_ref, t_ref, dz_ref, dzb_ref, loss_ref, dg_ref, db_ref = refs
        else:
            a_ref, w_ref, r_ref, g_ref, b_ref, z_ref, h_ref, hb_ref = refs
        contract = (((0,), (0,)), ((), ())) if a_transposed else (((1,), (0,)), ((), ()))
        z = ALPHA * r_ref[...] + lax.dot_general(a_ref[...], w_ref[...], contract, preferred_element_type=F32)
        h = _layer_norm(z, g_ref[...], b_ref[...])
        if not final:
            z_ref[...] = z
            h_ref[...] = h
            hb_ref[...] = h.astype(BF16)
            return
        i = pl.program_id(0)
        row = i * tm + lax.broadcasted_iota(jnp.int32, (tm, 1), 0)
        valid = (row >= N_META) & (row < n_tok)
        err = jnp.where(valid, h - t_ref[...], 0.0)
        dy = err / D
        dz, xhat = _ln_bwd_rows(dy, z, g_ref[...])
        dz_ref[...] = dz
        dzb_ref[...] = dz.astype(BF16)

        @pl.when(i == 0)
        def _():
            loss_ref[...] = jnp.zeros_like(loss_ref)
            dg_ref[...] = jnp.zeros_like(dg_ref)
            db_ref[...] = jnp.zeros_like(db_ref)

        loss_ref[...] += 0.5 * jnp.sum(jnp.sum(err * err, axis=1, keepdims=True) / D, axis=0, keepdims=True)
        dg_ref[...] += jnp.sum(dy * xhat, axis=0, keepdims=True)
        db_ref[...] += jnp.sum(dy, axis=0, keepdims=True)

    row_blk = lambda cols: pl.BlockSpec((tm, cols), lambda i: (i, 0))
    vec = pl.BlockSpec((1, D), lambda i: (0, 0))
    a_blk = pl.BlockSpec((K, tm), lambda i: (0, i)) if a_transposed else row_blk(K)
    in_specs = [a_blk, pl.BlockSpec((K, D), lambda i: (0, 0)), row_blk(D), vec, vec]
    args = [a, w, res, g, b]
    if final:
        in_specs.append(row_blk(D))
        args.append(target)
        out_specs = [row_blk(D), row_blk(D), pl.BlockSpec((8, LANES), lambda i: (0, 0)), vec, vec]
        out_shape = [jax.ShapeDtypeStruct((M, D), F32), jax.ShapeDtypeStruct((M, D), BF16),
                     jax.ShapeDtypeStruct((8, LANES), F32), jax.ShapeDtypeStruct((1, D), F32),
                     jax.ShapeDtypeStruct((1, D), F32)]
    else:
        out_specs = [row_blk(D)] * 3
        out_shape = [jax.ShapeDtypeStruct((M, D), F32)] * 2 + [jax.ShapeDtypeStruct((M, D), BF16)]
    return pl.pallas_call(
        body, name=name, grid=(M // tm,), in_specs=in_specs, out_specs=out_specs, out_shape=out_shape,
        compiler_params=_cparams(),
    )(*args)


def _ln_bwd_rows(dh_v, z_v, g_v):
    mu = jnp.mean(z_v, axis=-1, keepdims=True)
    xc = z_v - mu
    rstd = lax.rsqrt(jnp.mean(xc * xc, axis=-1, keepdims=True) + LN_EPS)
    xhat = xc * rstd
    dxhat = dh_v * g_v
    dz = rstd * (dxhat - jnp.mean(dxhat, axis=-1, keepdims=True)
                 - xhat * jnp.mean(dxhat * xhat, axis=-1, keepdims=True))
    return dz, xhat


def _mm_ln_bwd(a, b, dz_next, z, g, *, name):
    M, K = a.shape
    D = b.shape[0]
    tm = _pick(M, ROW_TILES)
    tk = K if K <= 3200 else _pick(K, (2816, 1024, 640, 512, 256, 128))
    nk = K // tk

    def body(a_ref, b_ref, nx_ref, z_ref, g_ref, dz_ref, dzb_ref, dg_ref, db_ref, *scratch):
        i, k = pl.program_id(0), pl.program_id(1)
        prod = _dot_nt(a_ref[...], b_ref[...])

        @pl.when((i == 0) & (k == 0))
        def _():
            dg_ref[...] = jnp.zeros_like(dg_ref)
            db_ref[...] = jnp.zeros_like(db_ref)

        def finish(acc):
            dh_v = acc + ALPHA * nx_ref[...]
            dz, xhat = _ln_bwd_rows(dh_v, z_ref[...], g_ref[...])
            dz_ref[...] = dz
            dzb_ref[...] = dz.astype(BF16)
            dg_ref[...] += jnp.sum(dh_v * xhat, axis=0, keepdims=True)
            db_ref[...] += jnp.sum(dh_v, axis=0, keepdims=True)

        if nk == 1:
            finish(prod)
        else:
            acc_ref, = scratch

            @pl.when(k == 0)
            def _():
                acc_ref[...] = prod

            @pl.when(k > 0)
            def _():
                acc_ref[...] += prod

            @pl.when(k == nk - 1)
            def _():
                finish(acc_ref[...])

    row_blk = pl.BlockSpec((tm, D), lambda i, k: (i, 0))
    vec = pl.BlockSpec((1, D), lambda i, k: (0, 0))
    return pl.pallas_call(
        body, name=name, grid=(M // tm, nk),
        in_specs=[pl.BlockSpec((tm, tk), lambda i, k: (i, k)), pl.BlockSpec((D, tk), lambda i, k: (0, k)),
                  row_blk, row_blk, vec],
        out_specs=[row_blk, row_blk, vec, vec],
        out_shape=[jax.ShapeDtypeStruct((M, D), F32), jax.ShapeDtypeStruct((M, D), BF16),
                   jax.ShapeDtypeStruct((1, D), F32), jax.ShapeDtypeStruct((1, D), F32)],
        scratch_shapes=[pltpu.VMEM((tm, D), F32)] if nk > 1 else [],
        compiler_params=_cparams(),
    )(a, b, dz_next, z, g)


def _pool_fwd(h0, pw, scale, g, b, *, name):
    M, D = h0.shape
    n_groups, G, _ = pw.shape
    tm = _pick(M, ROW_TILES)

    def body(x_ref, halo_ref, pw_ref, sc_ref, g_ref, b_ref, z_ref, h_ref, hb_ref, diff_ref, ext_ref, mix_ref):
        i = pl.program_id(0)
        x = x_ref[...]
        ext_ref[0:POOL_HALO, :] = jnp.where(i == 0, 0.0, halo_ref[...])
        ext_ref[POOL_HALO:, :] = x
        tok = i * tm + lax.broadcasted_iota(jnp.int32, (tm, 1), 0)
        for gi, win in enumerate(POOL_WINDOWS):
            cols = slice(gi * G, (gi + 1) * G)
            xs = x[:, cols]
            s = xs
            for k in range(1, win):
                s = s + ext_ref[pl.ds(POOL_HALO - k, tm), cols]
            count = jnp.minimum(tok + 1, win).astype(F32)
            d = (s / count - xs).astype(BF16)
            diff_ref[:, cols] = d
            mix_ref[:, cols] = jnp.dot(d, pw_ref[gi], preferred_element_type=F32)
        z = ALPHA * x + mix_ref[...] * sc_ref[...]
        z_ref[...] = z
        h = _layer_norm(z, g_ref[...], b_ref[...])
        h_ref[...] = h
        hb_ref[...] = h.astype(BF16)

    row_blk = pl.BlockSpec((tm, D), lambda i: (i, 0))
    vec = pl.BlockSpec((1, D), lambda i: (0, 0))
    halo = pl.BlockSpec((POOL_HALO, D), lambda i: (jnp.maximum(i * (tm // POOL_HALO) - 1, 0), 0))
    return pl.pallas_call(
        body, name=name, grid=(M // tm,),
        in_specs=[row_blk, halo, pl.BlockSpec((n_groups, G, G), lambda i: (0, 0, 0)), vec, vec, vec],
        out_specs=[row_blk] * 4,
        out_shape=[jax.ShapeDtypeStruct((M, D), F32)] * 2 + [jax.ShapeDtypeStruct((M, D), BF16)] * 2,
        scratch_shapes=[pltpu.VMEM((tm + POOL_HALO, D), F32), pltpu.VMEM((tm, D), F32)],
        compiler_params=_cparams(),
    )(h0, h0, pw, scale, g, b)


def _pool_bwd(dz, diff, pw, scale, *, name):
    M, D = dz.shape
    n_groups, G, _ = pw.shape
    tm = _pick(M, ROW_TILES)
    nt = M // tm

    def body(dz_ref, halo_ref, diff_ref, pw_ref, sc_ref, dh_ref, dpw_ref, dsc_ref, ext_ref, q_ref):
        i = pl.program_id(0)
        dz_v = dz_ref[...]
        ext_ref[0:tm, :] = dz_v
        ext_ref[tm:, :] = jnp.where(i == nt - 1, 0.0, halo_ref[...])
        tok = i * tm + lax.broadcasted_iota(jnp.int32, (tm + POOL_HALO, 1), 0)

        @pl.when(i == 0)
        def _():
            dpw_ref[...] = jnp.zeros_like(dpw_ref)
            dsc_ref[...] = jnp.zeros_like(dsc_ref)

        for gi, win in enumerate(POOL_WINDOWS):
            cols = slice(gi * G, (gi + 1) * G)
            dmix = (ext_ref[:, cols] * sc_ref[:, cols]).astype(BF16)
            ddiff = _dot_nt(dmix, pw_ref[gi])
            count = jnp.minimum(tok + 1, win).astype(F32)
            q_ref[:, cols] = ddiff / count
            acc = -ddiff[0:tm]
            for k in range(win):
                acc = acc + q_ref[pl.ds(k, tm), cols]
            dh_ref[:, cols] = ALPHA * dz_v[:, cols] + acc
            d = diff_ref[:, cols]
            dpw_ref[gi] += lax.dot_general(d, dmix[0:tm], (((0,), (0,)), ((), ())), preferred_element_type=F32)
            mixed = jnp.dot(d, pw_ref[gi], preferred_element_type=F32)
            dsc_ref[:, cols] += jnp.sum(dz_v[:, cols] * mixed, axis=0, keepdims=True)

    row_blk = pl.BlockSpec((tm, D), lambda i: (i, 0))
    vec = pl.BlockSpec((1, D), lambda i: (0, 0))
    per_tile = tm // POOL_HALO
    halo = pl.BlockSpec((POOL_HALO, D), lambda i: (jnp.minimum((i + 1) * per_tile, nt * per_tile - 1), 0))
    wblk = pl.BlockSpec((n_groups, G, G), lambda i: (0, 0, 0))
    return pl.pallas_call(
        body, name=name, grid=(nt,),
        in_specs=[row_blk, halo, row_blk, wblk, vec],
        out_specs=[row_blk, wblk, vec],
        out_shape=[jax.ShapeDtypeStruct((M, D), F32), jax.ShapeDtypeStruct((n_groups, G, G), F32),
                   jax.ShapeDtypeStruct((1, D), F32)],
        scratch_shapes=[pltpu.VMEM((tm + POOL_HALO, D), F32), pltpu.VMEM((tm + POOL_HALO, D), F32)],
        compiler_params=_cparams(),
    )(dz, dz, diff, pw, scale)


def _glu_interleave(x):
    s = x.shape
    n = s[-1] // (2 * GLU_CHUNK)
    return jnp.swapaxes(x.reshape(s[:-1] + (2, n, GLU_CHUNK)), -3, -2).reshape(s)


def _glu_deinterleave(x):
    s = x.shape
    n = s[-1] // (2 * GLU_CHUNK)
    return jnp.swapaxes(x.reshape(s[:-1] + (n, 2, GLU_CHUNK)), -3, -2).reshape(s)


W_PACK_ROWS = 256
W_PACK_PIECE = 64


def _interleaved_col(col, F):
    half, within = divmod(col, F)
    chunk, off = divmod(within, GLU_CHUNK)
    return chunk * 2 * GLU_CHUNK + half * GLU_CHUNK + off


def _w_in_grad_segments(dw, *, name):
    D, F2 = dw.shape
    width = F2 // N_DEV
    rows = _pick(D, (W_PACK_ROWS, 128))
    assert width % W_PACK_PIECE == 0

    def body(x_ref, o_ref):
        for j in range(N_DEV):
            for q in range(0, width, W_PACK_PIECE):
                at = _interleaved_col(width * j + q, F2 // 2)
                o_ref[j % 2, j // 2, :, q:q + W_PACK_PIECE] = x_ref[:, at:at + W_PACK_PIECE].astype(BF16)

    return pl.pallas_call(
        body, name=name, grid=(D // rows,),
        in_specs=[pl.BlockSpec((rows, F2), lambda r: (r, 0))],
        out_specs=pl.BlockSpec((2, N_CHIPS, rows, width), lambda r: (0, 0, r, 0)),
        out_shape=jax.ShapeDtypeStruct((2, N_CHIPS, D, width), BF16),
        compiler_params=_cparams(),
    )(dw)


def _w_in_interleaved(gathered, *, name):
    n_dev, depth, D, width = gathered.shape
    F = n_dev * width // 2
    rows = _pick(D, (W_PACK_ROWS, 128))
    assert width % W_PACK_PIECE == 0 and GLU_CHUNK % W_PACK_PIECE == 0 and F % GLU_CHUNK == 0
    interleaved = lambda col: _interleaved_col(col, F)

    def body(x_ref, o_ref):
        for j in range(n_dev):
            for q in range(0, width, W_PACK_PIECE):
                to = interleaved(width * j + q)
                o_ref[0, :, to:to + W_PACK_PIECE] = x_ref[j, 0, :, q:q + W_PACK_PIECE]

    return pl.pallas_call(
        body, name=name, grid=(depth, D // rows),
        in_specs=[pl.BlockSpec((n_dev, 1, rows, width), lambda l, r: (0, l, r, 0))],
        out_specs=pl.BlockSpec((1, rows, n_dev * width), lambda l, r: (l, r, 0)),
        out_shape=jax.ShapeDtypeStruct((depth, D, n_dev * width), gathered.dtype),
        compiler_params=_cparams(),
    )(gathered)


def _conv_glu_fwd(u, cw, cb, *, name):
    M, F2 = u.shape
    tm = _pick(M, CONV_ROW_TILES)
    tc = 2 * GLU_CHUNK

    def body(u_ref, halo_ref, w_ref, b_ref, o_ref, c_ref):
        i = pl.program_id(0)
        halo = jnp.where(i == 0, 0.0, halo_ref[...].astype(F32))
        w0, w1, w2, b = w_ref[0:1, :], w_ref[1:2, :], w_ref[2:3, :], b_ref[...]
        for r0 in range(0, tm, CONV_STRIP):
            before = halo if r0 == 0 else u_ref[pl.ds(r0 - CONV_HALO, CONV_HALO), :].astype(F32)
            win = jnp.concatenate([before, u_ref[pl.ds(r0, CONV_STRIP), :].astype(F32)], axis=0)
            u0, u1, u2 = (win[CONV_HALO - k:CONV_HALO - k + CONV_STRIP] for k in range(3))
            c = b + w0 * u2 + w1 * u1 + w2 * u0
            c_ref[pl.ds(r0, CONV_STRIP), :] = c.astype(BF16)
            a, g = c[:, :GLU_CHUNK], c[:, GLU_CHUNK:]
            o_ref[pl.ds(r0, CONV_STRIP), :] = (a * jax.nn.sigmoid(a) * g).astype(BF16)

    per_tile = tm // CONV_HALO
    return pl.pallas_call(
        body, name=name, grid=(M // tm, F2 // tc),
        in_specs=[pl.BlockSpec((tm, tc), lambda i, j: (i, j)),
                  pl.BlockSpec((CONV_HALO, tc), lambda i, j: (jnp.maximum(i * per_tile - 1, 0), j)),
                  pl.BlockSpec((3, tc), lambda i, j: (0, j)), pl.BlockSpec((1, tc), lambda i, j: (0, j))],
        out_specs=[pl.BlockSpec((tm, GLU_CHUNK), lambda i, j: (i, j)), pl.BlockSpec((tm, tc), lambda i, j: (i, j))],
        out_shape=[jax.ShapeDtypeStruct((M, F2 // 2), BF16), jax.ShapeDtypeStruct((M, F2), BF16)],
        compiler_params=_cparams(),
    )(u, u, cw, cb)


def _conv_glu_bwd(u, c, dact, cw, *, name):
    M, F2 = u.shape
    tm = _pick(M, CONV_ROW_TILES)
    nt = M // tm
    tc = 2 * GLU_CHUNK

    def body(u_ref, c_ref, da_ref, w_ref, du_ref, dw_ref, db_ref, dcx_ref, carry_ref):
        i = pl.program_id(1)
        w0, w1, w2 = w_ref[0:1, :], w_ref[1:2, :], w_ref[2:3, :]

        @pl.when(i == 0)
        def _():
            dw_ref[...] = jnp.zeros_like(dw_ref)
            db_ref[...] = jnp.zeros_like(db_ref)
            carry_ref[...] = jnp.zeros_like(carry_ref)

        def fold(t):
            return sum(t[r:r + 8] for r in range(0, CONV_STRIP, 8))

        dcx_ref[tm:, :] = carry_ref[...]
        s_b = s_0 = s_1 = s_2 = jnp.zeros((8, tc), F32)
        for r0 in reversed(range(0, tm, CONV_STRIP)):
            rows = pl.ds(r0, CONV_STRIP)
            c_v = c_ref[rows, :].astype(F32)
            a, g = c_v[:, :GLU_CHUNK], c_v[:, GLU_CHUNK:]
            sig = jax.nn.sigmoid(a)
            dact_v = da_ref[rows, :].astype(F32)
            d_a = dact_v * g * (sig * (1.0 + a * (1.0 - sig)))
            d_g = dact_v * (a * sig)
            dc = jnp.concatenate([d_a, d_g], axis=1)
            dcx_ref[rows, :] = dc
            dc1, dc2 = dcx_ref[pl.ds(r0 + 1, CONV_STRIP), :], dcx_ref[pl.ds(r0 + 2, CONV_STRIP), :]
            du_ref[rows, :] = (w2 * dc + w1 * dc1 + w0 * dc2).astype(BF16)
            u_v = u_ref[rows, :].astype(F32)
            s_b, s_0, s_1, s_2 = s_b + fold(dc), s_0 + fold(dc2 * u_v), s_1 + fold(dc1 * u_v), s_2 + fold(dc * u_v)
        carry_ref[...] = dcx_ref[0:CONV_HALO, :]
        db_ref[...] += jnp.sum(s_b, axis=0, keepdims=True)
        dw_ref[0:1, :] += jnp.sum(s_0, axis=0, keepdims=True)
        dw_ref[1:2, :] += jnp.sum(s_1, axis=0, keepdims=True)
        dw_ref[2:3, :] += jnp.sum(s_2, axis=0, keepdims=True)

    per_tile = tm // CONV_HALO
    rev = lambda i: nt - 1 - i
    return pl.pallas_call(
        body, name=name, grid=(F2 // tc, nt),
        in_specs=[pl.BlockSpec((tm, tc), lambda j, i: (rev(i), j)), pl.BlockSpec((tm, tc), lambda j, i: (rev(i), j)),
                  pl.BlockSpec((tm, GLU_CHUNK), lambda j, i: (rev(i), j)),
                  pl.BlockSpec((3, tc), lambda j, i: (0, j))],
        out_specs=[pl.BlockSpec((tm, tc), lambda j, i: (rev(i), j)),
                   pl.BlockSpec((3, tc), lambda j, i: (0, j)), pl.BlockSpec((1, tc), lambda j, i: (0, j))],
        out_shape=[jax.ShapeDtypeStruct((M, F2), BF16), jax.ShapeDtypeStruct((3, F2), F32),
                   jax.ShapeDtypeStruct((1, F2), F32)],
        scratch_shapes=[pltpu.VMEM((tm + CONV_HALO, tc), F32), pltpu.VMEM((CONV_HALO, tc), F32)],
        compiler_params=_cparams(),
    )(u, c, dact, cw)


def _split3(x):
    hi = x.astype(BF16)
    r = x - hi.astype(F32)
    mid = r.astype(BF16)
    lo = (r - mid.astype(F32)).astype(BF16)
    return hi, mid, lo


def _tri_sum(tri, x):
    return sum(jnp.dot(tri, part, preferred_element_type=F32) for part in _split3(x))


def _log_sigmoid(x):
    return jnp.minimum(x, 0.0) - jnp.log1p(jnp.exp(-jnp.abs(x)))


def _forget_cumsum(pre, bias, *, name):
    M, C = pre.shape
    tm = _pick(M, ROW_TILES)

    def body(p_ref, b_ref, c_ref, carry_ref):
        i = pl.program_id(0)

        @pl.when(i == 0)
        def _():
            carry_ref[...] = jnp.zeros_like(carry_ref)

        logf = _log_sigmoid(p_ref[...] + b_ref[...])
        r = lax.broadcasted_iota(jnp.int32, (tm, tm), 0)
        s = lax.broadcasted_iota(jnp.int32, (tm, tm), 1)
        c_ref[...] = _tri_sum((s <= r).astype(BF16), logf) + carry_ref[...]
        carry_ref[...] = c_ref[pl.ds(tm - 1, 1), :]

    return pl.pallas_call(
        body, name=name, grid=(M // tm,),
        in_specs=[pl.BlockSpec((tm, C), lambda i: (i, 0)), pl.BlockSpec((1, C), lambda i: (0, 0))],
        out_specs=pl.BlockSpec((tm, C), lambda i: (i, 0)),
        out_shape=jax.ShapeDtypeStruct((M, C), F32),
        scratch_shapes=[pltpu.VMEM((1, C), F32)],
        compiler_params=_cparams(),
    )(pre, bias)


def _forget_cumsum_bwd(dc, pre, bias, *, name):
    M, C = pre.shape
    tm = _pick(M, ROW_TILES)
    nt = M // tm

    def body(dc_ref, p_ref, b_ref, dp_ref, db_ref, carry_ref, run_ref):
        i = pl.program_id(0)

        @pl.when(i == 0)
        def _():
            carry_ref[...] = jnp.zeros_like(carry_ref)
            db_ref[...] = jnp.zeros_like(db_ref)

        r = lax.broadcasted_iota(jnp.int32, (tm, tm), 0)
        s = lax.broadcasted_iota(jnp.int32, (tm, tm), 1)
        run_ref[...] = _tri_sum((s >= r).astype(BF16), dc_ref[...]) + carry_ref[...]
        carry_ref[...] = run_ref[pl.ds(0, 1), :]
        dpre = run_ref[...] * jax.nn.sigmoid(-(p_ref[...] + b_ref[...]))
        dp_ref[...] = dpre.astype(BF16)
        db_ref[...] += jnp.sum(dpre, axis=0, keepdims=True)

    rev_blk = pl.BlockSpec((tm, C), lambda i: (nt - 1 - i, 0))
    vec = pl.BlockSpec((1, C), lambda i: (0, 0))
    return pl.pallas_call(
        body, name=name, grid=(nt,),
        in_specs=[rev_blk, rev_blk, vec], out_specs=[rev_blk, vec],
        out_shape=[jax.ShapeDtypeStruct((M, C), BF16), jax.ShapeDtypeStruct((1, C), F32)],
        scratch_shapes=[pltpu.VMEM((1, C), F32), pltpu.VMEM((tm, C), F32)],
        compiler_params=_cparams(),
    )(dc, pre, bias)


def _causal_mask(tm):
    key = lax.broadcasted_iota(jnp.int32, (tm, tm), 0)
    query = lax.broadcasted_iota(jnp.int32, (tm, tm), 1)
    return key <= query


def _dot_nt(a, b):
    return lax.dot_general(a, b, (((1,), (1,)), ((), ())), preferred_element_type=F32)


def _loop_unrolled(n, step, init, unroll):
    def trip(p, carry):
        for r in range(unroll):
            carry = step(unroll * p + r, carry)
        return carry
    carry = lax.fori_loop(0, n // unroll, trip, init)
    return lax.fori_loop(unroll * (n // unroll), n, step, carry)


def _rows(main, extras, total):
    tm = main.shape[1]
    used = sum(e.shape[0] for e in extras)
    tile = jnp.concatenate([e.astype(BF16) for e in extras] + [jnp.zeros((ATTN_EXTRA - used, tm), BF16)], axis=0)
    rest = total - main.shape[0] - ATTN_EXTRA
    return jnp.concatenate([main, tile] + ([jnp.zeros((rest, tm), BF16)] if rest else []), axis=0)


def _attn_specs(H, M, dh, tm):
    nt = M // tm
    qT_blk = pl.BlockSpec((1, dh, tm), lambda h, i: (h, 0, i))
    row_blk = pl.BlockSpec((1, 1, tm), lambda h, i: (h, 0, i))
    key_blk = pl.BlockSpec((1, M, LANES), lambda h, i: (h, 0, 0))
    keyT_blk = pl.BlockSpec((1, nt, dh + ATTN_EXTRA, tm), lambda h, i: (h, 0, 0, 0))
    return qT_blk, row_blk, key_blk, keyT_blk


def _pair_specs(M, dh, tm):
    assert 2 * dh == LANES
    cols = lambda section: pl.BlockSpec((tm, LANES), lambda p, i: (i, section + p))
    headsT = pl.BlockSpec((2, dh, tm), lambda p, i: (p, 0, i))
    return cols, headsT


def _attn_pack(qkv, c, H, *, name):
    M, D3 = qkv.shape
    D = D3 // 3
    dh = D // H
    tm = _pick(M, ROW_TILES)
    nt = M // tm
    cols, headsT = _pair_specs(M, dh, tm)
    key_blk = pl.BlockSpec((2, tm, LANES), lambda p, i: (p, i, 0))
    keyT_blk = pl.BlockSpec((2, 1, dh + ATTN_EXTRA, tm), lambda p, i: (p, i, 0, 0))

    def body(k_ref, v_ref, q_ref, c_ref, ka_ref, va_ref, qT_ref, kT_ref, vT_ref):
        p = pl.program_id(0)
        lane = lax.broadcasted_iota(jnp.int32, (tm, LANES), 1)
        k_v, v_v = k_ref[...], v_ref[...]
        ones_v = jnp.where((lane >= dh) & (lane < dh + 3), 1.0, 0.0)
        for e in range(2):
            ck = jnp.sum(jnp.where(lane == 2 * p + e, c_ref[...], 0.0), axis=1, keepdims=True)
            hi, mid, lo = _split3(-ck)
            extra = jnp.where(lane == dh + 3, hi.astype(F32), jnp.where(lane == dh + 4, mid.astype(F32),
                              jnp.where(lane == dh + 5, lo.astype(F32), ones_v)))
            first = lambda t: t if e == 0 else pltpu.roll(t, dh, 1)
            ka_ref[e] = jnp.where(lane < dh, first(k_v), extra.astype(BF16))
            va_ref[e] = jnp.where(lane < dh, first(v_v), ones_v.astype(BF16))
        row = lax.broadcasted_iota(jnp.int32, (ATTN_EXTRA, tm), 0)
        tail = jnp.where(row == 0, 1.0, 0.0).astype(BF16)
        for src, dst in ((k_v, kT_ref), (v_v, vT_ref)):
            t = src.T
            for e in range(2):
                dst[e, 0, 0:dh, :] = t[e * dh:(e + 1) * dh]
                dst[e, 0, dh:, :] = tail
        qT_ref[...] = q_ref[...].T.reshape(2, dh, tm)

    n_sec = D // LANES
    return pl.pallas_call(
        body, name=name, grid=(H // 2, nt),
        in_specs=[cols(0), cols(n_sec), cols(2 * n_sec), pl.BlockSpec((tm, LANES), lambda p, i: (i, 0))],
        out_specs=[key_blk, key_blk, headsT, keyT_blk, keyT_blk],
        out_shape=[jax.ShapeDtypeStruct((H, M, LANES), BF16)] * 2 + [jax.ShapeDtypeStruct((H, dh, M), BF16)]
        + [jax.ShapeDtypeStruct((H, nt, dh + ATTN_EXTRA, tm), BF16)] * 2,
        compiler_params=_cparams(),
    )(qkv, qkv, qkv, c)


def _heads_merge(parts, M, tm, *, name):
    H = parts[0].shape[0]
    dh = LANES // 2
    blocked = [t.ndim == 4 for t in parts]
    cols, headsT = _pair_specs(M, dh, tm)

    def body(*refs):
        for src, dst, b in zip(refs[:len(parts)], refs[len(parts):], blocked):
            t = jnp.concatenate([src[e, 0, 0:dh, :] for e in range(2)], axis=0) if b else src[...].reshape(2 * dh, tm)
            dst[...] = t.astype(BF16).T

    in_specs = [pl.BlockSpec((2, 1, t.shape[2], tm), lambda p, i: (p, i, 0, 0)) if b else headsT
                for t, b in zip(parts, blocked)]
    return pl.pallas_call(
        body, name=name, grid=(H // 2, M // tm), in_specs=in_specs, out_specs=[cols(0)] * len(parts),
        out_shape=[jax.ShapeDtypeStruct((M, H * dh), BF16)] * len(parts), compiler_params=_cparams(),
    )(*parts)


def _attn_fwd(qT, cq, k_aug, vT_aug, *, name):
    H, dh, M = qT.shape
    tm = vT_aug.shape[-1]
    qT_blk, row_blk, key_blk, keyT_blk = _attn_specs(H, M, dh, tm)

    def body(qT_ref, cq_ref, k_ref, vT_ref, oT_ref, lse_ref):
        i = pl.program_id(1)
        ones = jnp.ones((3, tm), BF16)
        qa = _rows(qT_ref[0] * jnp.asarray(dh ** -0.5, BF16), [*_split3(cq_ref[0]), ones], LANES)

        def block(j, carry, masked):
            m, acc = carry
            keys = pl.ds(pl.multiple_of(j * tm, tm), tm)
            sT = jnp.dot(k_ref[0, keys, :], qa, preferred_element_type=F32)
            if masked:
                sT = jnp.where(_causal_mask(tm), sT, NEG_INF)
            m_new = jnp.maximum(m, jnp.max(sT, axis=0, keepdims=True))
            pT = jnp.exp(sT - m_new).astype(BF16)
            acc = jnp.exp(m - m_new) * acc + jnp.dot(vT_ref[0, j], pT, preferred_element_type=F32)
            return m_new, acc

        init = (jnp.full((1, tm), NEG_INF, F32), jnp.zeros((dh + ATTN_EXTRA, tm), F32))
        m, acc = block(i, _loop_unrolled(i, lambda j, c: block(j, c, False), init, 4), True)
        l = acc[dh:dh + 1, :]
        oT_ref[0] = (acc[0:dh, :] / l).astype(BF16)
        lse_ref[0] = m + jnp.log(l)

    return pl.pallas_call(
        body, name=name, grid=(H, M // tm),
        in_specs=[qT_blk, row_blk, key_blk, keyT_blk], out_specs=[qT_blk, row_blk],
        out_shape=[jax.ShapeDtypeStruct((H, dh, M), BF16), jax.ShapeDtypeStruct((H, 1, M), F32)],
        compiler_params=_cparams(),
    )(qT, cq, k_aug, vT_aug)


def _attn_bwd(qT, oT, doT, lse, cq, k_aug, v_aug, kT_aug, *, name):
    H, dh, M = qT.shape
    tm = kT_aug.shape[-1]
    qT_blk, row_blk, key_blk, keyT_blk = _attn_specs(H, M, dh, tm)
    dvT_blk = pl.BlockSpec((1, M // tm, dh, tm), lambda h, i: (h, 0, 0, 0))

    def body(qT_ref, oT_ref, doT_ref, lse_ref, cq_ref, k_ref, v_ref, kT_ref, dqT_ref, dcq_ref, dkT_ref, dvT_ref):
        i = pl.program_id(1)

        @pl.when(i == 0)
        def _():
            dkT_ref[...] = jnp.zeros_like(dkT_ref)
            dvT_ref[...] = jnp.zeros_like(dvT_ref)

        qsT = qT_ref[0] * jnp.asarray(dh ** -0.5, BF16)
        doT = doT_ref[0]
        delta = jnp.sum(doT.astype(F32) * oT_ref[0].astype(F32), axis=0, keepdims=True)
        ones = jnp.ones((3, tm), BF16)
        qa = _rows(qsT, [*_split3(cq_ref[0] - lse_ref[0]), ones], LANES)
        da = _rows(doT, _split3(-delta), LANES)
        q1 = _rows(qsT, [ones[0:1]], dh + ATTN_EXTRA)

        def block(j, dq, masked):
            keys = pl.ds(pl.multiple_of(j * tm, tm), tm)
            pT = jnp.exp(jnp.dot(k_ref[0, keys, :], qa, preferred_element_type=F32))
            if masked:
                pT = jnp.where(_causal_mask(tm), pT, 0.0)
            ds_b = (pT * jnp.dot(v_ref[0, keys, :], da, preferred_element_type=F32)).astype(BF16)
            dvT_ref[0, j] += _dot_nt(doT, pT.astype(BF16))
            dkT_ref[0, j] += _dot_nt(q1, ds_b)
            return dq + jnp.dot(kT_ref[0, j], ds_b, preferred_element_type=F32)

        init = jnp.zeros((dh + ATTN_EXTRA, tm), F32)
        dq = block(i, _loop_unrolled(i, lambda j, c: block(j, c, False), init, 4), True)
        dqT_ref[0] = (dq[0:dh, :] * dh ** -0.5).astype(BF16)
        dcq_ref[0] = dq[dh:dh + 1, :]

    return pl.pallas_call(
        body, name=name, grid=(H, M // tm),
        in_specs=[qT_blk, qT_blk, qT_blk, row_blk, row_blk, key_blk, key_blk, keyT_blk],
        out_specs=[qT_blk, row_blk, keyT_blk, dvT_blk],
        out_shape=[jax.ShapeDtypeStruct((H, dh, M), BF16), jax.ShapeDtypeStruct((H, 1, M), F32),
                   jax.ShapeDtypeStruct(kT_aug.shape, F32), jax.ShapeDtypeStruct((H, M // tm, dh, tm), F32)],
        compiler_params=_cparams(),
    )(qT, oT, doT, lse, cq, k_aug, v_aug, kT_aug)


def kernel(x, meta, pool_w, pool_scale, w_kv, w_f, b_f, w_q, w_o, ffn_w_in, ffn_conv_w, ffn_conv_b, ffn_w_out, ln_g, ln_b, loss_target, m_meta, m_pool_w, m_pool_scale, m_w_kv, m_w_f, m_b_f, m_w_q, m_w_o, m_ffn_w_in, m_ffn_conv_w, m_ffn_conv_b, m_ffn_w_out, m_ln_g, m_ln_b, v_meta, v_pool_w, v_pool_scale, v_w_kv, v_w_f, v_b_f, v_w_q, v_w_o, v_ffn_w_in, v_ffn_conv_w, v_ffn_conv_b, v_ffn_w_out, v_ln_g, v_ln_b):
    local = dict(meta=meta, pool_w=pool_w, pool_scale=pool_scale, w_kv=w_kv, w_f=w_f, b_f=b_f, w_q=w_q, w_o=w_o,
                 ffn_w_in=ffn_w_in, ffn_conv_w=ffn_conv_w, ffn_conv_b=ffn_conv_b, ffn_w_out=ffn_w_out,
                 ln_g=ln_g, ln_b=ln_b)
    mom1 = dict(meta=m_meta, pool_w=m_pool_w, pool_scale=m_pool_scale, w_kv=m_w_kv, w_f=m_w_f, b_f=m_b_f,
                w_q=m_w_q, w_o=m_w_o, ffn_w_in=m_ffn_w_in, ffn_conv_w=m_ffn_conv_w, ffn_conv_b=m_ffn_conv_b,
                ffn_w_out=m_ffn_w_out, ln_g=m_ln_g, ln_b=m_ln_b)
    mom2 = dict(meta=v_meta, pool_w=v_pool_w, pool_scale=v_pool_scale, w_kv=v_w_kv, w_f=v_w_f, b_f=v_b_f,
                w_q=v_w_q, w_o=v_w_o, ffn_w_in=v_ffn_w_in, ffn_conv_w=v_ffn_conv_w, ffn_conv_b=v_ffn_conv_b,
                ffn_w_out=v_ffn_w_out, ln_g=v_ln_g, ln_b=v_ln_b)
    axis_of = dict(PARAMS)

    S, D = x.shape[1], x.shape[2]
    H = b_f.shape[0]
    dh = D // H
    n_tok = N_META + S
    M = _round_up(n_tok, ROW_ALIGN)
    tm = _pick(M, ROW_TILES)
    nt = M // tm
    F2 = ffn_conv_b.shape[1]
    F = F2 // 2
    depth = ffn_conv_b.shape[0]

    small_sharded = [n for n in SMALL if axis_of[n] is not None]
    got = _all_gather([local[n].astype(BF16) for n in BIG] + [_pack([local[n] for n in small_sharded])],
                      "gather_weights")
    wb = {n: _from_blocks(blk, axis_of[n]) for n, blk in zip(BIG, got) if n != "ffn_w_in"}
    small_blocks = _unpack(got[-1], [local[n].shape for n in small_sharded], lead=(N_DEV,))
    wf32 = {n: _from_blocks(blk, axis_of[n]) for n, blk in zip(small_sharded, small_blocks)}

    pw = wb["pool_w"][0]
    wf_pad = jnp.pad(wf32["w_f"].astype(BF16), ((0, 0), (0, LANES - H)))
    w_qkv = jnp.concatenate([wb["w_kv"], wb["w_q"][0]], axis=1)
    w_att = jnp.concatenate([w_qkv, wf_pad], axis=1)
    wo = wb["w_o"][0]
    w_in = _w_in_interleaved(got[BIG.index("ffn_w_in")], name="w_in_interleave")
    w_out = wb["ffn_w_out"]
    conv_w = _glu_interleave(wf32["ffn_conv_w"])
    conv_b = _glu_interleave(ffn_conv_b)[:, None, :]
    scale = wf32["pool_scale"]
    g_ln, b_ln = wf32["ln_g"], wf32["ln_b"]
    ln = lambda i, j: (g_ln[i, j][None, :], b_ln[i, j][None, :])
    bias_f = jnp.pad(b_f, (0, LANES - H))[None, :]

    pad_rows = M - n_tok
    h0 = jnp.concatenate([wf32["meta"], x[0], jnp.zeros((pad_rows, D), F32)], axis=0)
    target = jnp.concatenate([jnp.zeros((N_META, D), F32), loss_target[0], jnp.zeros((pad_rows, D), F32)], axis=0)

    z1, h1, h1b, diff = _pool_fwd(h0, pw, scale, *ln(0, 0), name="pool_fwd")
    u0 = _mm(h1b, w_in[0], name="ffn0_up", out_dtype=BF16)
    act0, c0 = _conv_glu_fwd(u0, conv_w[0], conv_b[0], name="ffn0_conv")
    z2, h2, h2b = _mm_ln(act0, w_out[0], h1, *ln(0, 1), name="ffn0_down_ln")

    qkv = _mm(h2b, w_qkv, name="attn_qkv", out_dtype=BF16)
    pre = _mm(h2b, wf_pad, name="attn_gate", out_dtype=F32)
    c = _forget_cumsum(pre, bias_f, name="attn_cumsum")
    k_aug, v_aug, qT_h, kT_aug, vT_aug = _attn_pack(qkv, c, H, name="attn_pack")
    cq = c[:, :H].T[:, None, :]
    oT_h, lse = _attn_fwd(qT_h, cq, k_aug, vT_aug, name="attn_fwd")
    oT = oT_h.reshape(D, M)
    z3, h3, h3b = _mm_ln(oT, wo, h2, *ln(1, 0), name="attn_out_ln", a_transposed=True)

    u1 = _mm(h3b, w_in[1], name="ffn1_up", out_dtype=BF16)
    act1, c1 = _conv_glu_fwd(u1, conv_w[1], conv_b[1], name="ffn1_conv")
    dz4, dz4b, loss_part, dg11, db11 = _mm_ln(act1, w_out[1], h3, *ln(1, 1), name="ffn1_down_loss", target=target,
                                              n_tok=n_tok)
    loss = lax.psum(loss_part[0, 0], ("x", "y", "c"))

    grads = {}

    def ffn_bwd(layer, dz, dz_b, u, c_conv, act, h_in_b, z_in, g_in, tag):
        dact = _mm(dz_b, w_out[layer], name=tag + "_dact", out_dtype=BF16, b_transposed=True)
        du, dcw, dcb = _conv_glu_bwd(u, c_conv, dact, conv_w[layer], name=tag + "_conv_bwd")
        dz_in = _mm_ln_bwd(du, w_in[layer], dz, z_in, g_in[None, :], name=tag + "_dh_ln_bwd")
        d_w_out = _mm_tn(act, dz_b, name=tag + "_dw_out")
        d_w_in = _w_in_grad_segments(_mm_tn(h_in_b, du, name=tag + "_dw_in"), name=tag + "_dw_in_segments")
        return dz_in, d_w_in, d_w_out, _glu_deinterleave(dcw), _glu_deinterleave(dcb)[0]

    (dz3, dz3b, dg10, db10), dwin1, dwout1, dcw1, dcb1 = ffn_bwd(1, dz4, dz4b, u1, c1, act1, h3b, z3, g_ln[1, 0], "ffn1")

    doT_h = _mm(wo, dz3b, name="attn_do", out_dtype=BF16, b_transposed=True).reshape(H, dh, M)
    grads["w_o"] = _mm(oT, dz3b, name="attn_dw_o", out_dtype=F32)[None]
    dqT_h, dcq, dkT_a, dvT_h = _attn_bwd(qT_h, oT_h, doT_h, lse, cq, k_aug, v_aug, kT_aug, name="attn_bwd")
    dck = -dkT_a[:, :, dh, :].reshape(H, M)
    dc = jnp.pad((dcq[:, 0, :] + dck).T, ((0, 0), (0, LANES - H)))
    dpre, dbias = _forget_cumsum_bwd(dc, pre, bias_f, name="attn_cumsum_bwd")
    d_att = jnp.concatenate([*_heads_merge([dkT_a, dvT_h, dqT_h], M, tm, name="attn_dqkv_merge"), dpre], axis=1)
    dz2, dz2b, dg01, db01 = _mm_ln_bwd(d_att, w_att, dz3, z2, g_ln[0, 1][None, :], name="attn_dh_ln_bwd")
    d_w_att = _mm_tn(h2b, d_att, name="attn_dw_qkv")
    grads["w_kv"] = d_w_att[:, :2 * D]
    grads["w_q"] = d_w_att[:, 2 * D:3 * D][None]
    grads["w_f"] = d_w_att[:, 3 * D:3 * D + H]
    grads["b_f"] = dbias[0, :H]

    (dz1, _, dg00, db00), dwin0, dwout0, dcw0, dcb0 = ffn_bwd(0, dz2, dz2b, u0, c0, act0, h1b, z1, g_ln[0, 0], "ffn0")
    dh0, dpw, dscale = _pool_bwd(dz1, diff, pw, scale, name="pool_bwd")

    grads["meta"] = dh0[:N_META]
    grads["pool_w"] = dpw[None]
    grads["pool_scale"] = dscale
    seg_w_in = jnp.stack([dwin0, dwin1], axis=2)
    grads["ffn_w_out"] = jnp.stack([dwout0, dwout1])
    grads["ffn_conv_w"] = jnp.stack([dcw0, dcw1])
    grads["ffn_conv_b"] = jnp.stack([dcb0, dcb1])
    grads["ln_g"] = jnp.stack([jnp.stack([dg00[0], dg01[0]]), jnp.stack([dg10[0], dg11[0]])])
    grads["ln_b"] = jnp.stack([jnp.stack([db00[0], db01[0]]), jnp.stack([db10[0], db11[0]])])
    grad_x = dh0[N_META:n_tok][None]

    blocks = {n: _to_blocks(grads[n], axis_of[n]) for n, _ in PARAMS if n != "ffn_w_in"}
    me = _coords()
    core, chip = (jnp.reshape(v, (1,)).astype(jnp.int32) for v in (me[2], 2 * me[0] + me[1]))
    by_core = lambda t: jnp.swapaxes(t.reshape((N_CHIPS, 2) + t.shape[1:]), 0, 1).astype(BF16)
    segs = [seg_w_in if n == "ffn_w_in" else by_core(blocks[n]) for n in BIG]
    pair = _exchange(segs, "grad_pair_exchange", (1,), lambda pos: pos[2])
    rows_of = lambda t, lead: t.reshape(lead + (-1, t.shape[-1]))
    partial = [_pair_sum(rows_of(s, (2,)), core, rows_of(p[0], ()), name="grad_pair_sum_" + n).reshape(s.shape[1:])
               for n, s, p in zip(BIG, segs, pair)]
    recv = _exchange(partial, "grad_chip_exchange", (4, 2, 6), lambda pos: 2 * pos[0] + pos[1])
    recv_small = _exchange([_pack([blocks[n] for n in SMALL], lead=(N_DEV,))], "grad_small_exchange",
                           tuple(range(1, N_DEV)), _index, by_sender=True)[0]
    results = {}
    for n, own, r in zip(BIG, partial, recv):
        shape = local[n].shape
        as2d = lambda t: t.reshape(-1, shape[-1])
        outs = _adamw(rows_of(own, (N_CHIPS,)), chip, rows_of(r, (N_CHIPS - 1,)), as2d(local[n]), as2d(mom1[n]),
                      as2d(mom2[n]), name="adamw_" + n)
        results[n] = [o_.reshape(shape) for o_ in outs]
    outs = _adamw(recv_small, jnp.zeros((1,), jnp.int32), recv_small[1:],
                  *[_pack([d[n] for n in SMALL]) for d in (local, mom1, mom2)], name="adamw_small")
    small_out = [_unpack(o_, [local[n].shape for n in SMALL]) for o_ in outs]
    for i, n in enumerate(SMALL):
        results[n] = [small_out[k][i] for k in range(4)]
    return (loss, grad_x, *[results[n][k] for k in range(4) for n, _ in PARAMS])
```

```python
import jax
import jax.numpy as jnp
from jax import lax
from jax.experimental import pallas as pl
from jax.experimental.pallas import tpu as pltpu

F32, BF16 = jnp.float32, jnp.bfloat16
MESH = pl.DeviceIdType.MESH

N_DEV = 8
N_CHIPS = 4
N_META = 16
POOL_WINDOWS = (2, 4, 8, 16)
POOL_HALO = 16
CONV_HALO = 16
CONV_STRIP = 32
ALPHA = 4.0 ** 0.25
LN_EPS = 1e-5
NEG_INF = -1e30
ADAM_LR, ADAM_B1, ADAM_B2, ADAM_EPS, ADAM_WD, ADAM_STEP = 0.001, 0.9, 0.999, 1e-08, 0.01, 10

LANES = 128
ROW_ALIGN = 128
ROW_TILES = (640, 512, 128)
CONV_ROW_TILES = (1664,) + ROW_TILES
PACK_COLS = 1024
ADAM_TILE_BYTES = 4 << 20
MM_TN_WINDOW_BYTES = 28 << 20
GLU_CHUNK = 256
ATTN_EXTRA = 16
ATTN_HEADS = 2
VMEM_LIMIT = 56 * 1024 * 1024

PARAMS = (("meta", 1), ("pool_w", 2), ("pool_scale", 1), ("w_kv", 1), ("w_f", 0), ("b_f", None),
          ("w_q", 1), ("w_o", 1), ("ffn_w_in", 2), ("ffn_conv_w", 2), ("ffn_conv_b", None),
          ("ffn_w_out", 1), ("ln_g", 2), ("ln_b", 2))
BIG = ("pool_w", "w_kv", "w_q", "w_o", "ffn_w_in", "ffn_w_out")
SMALL = ("meta", "pool_scale", "w_f", "b_f", "ffn_conv_w", "ffn_conv_b", "ln_g", "ln_b")


def _cparams(**kw):
    return pltpu.CompilerParams(vmem_limit_bytes=VMEM_LIMIT, **kw)


def _pick(n, cands):
    for c in cands:
        if n % c == 0:
            return c
    return n


def _round_up(n, m):
    return (n + m - 1) // m * m


def _pack(pieces, lead=()):
    flat = []
    for p in pieces:
        v = p.reshape(lead + (-1,))
        flat.append(jnp.pad(v, [(0, 0)] * len(lead) + [(0, _round_up(v.shape[-1], PACK_COLS) - v.shape[-1])]))
    v = jnp.concatenate(flat, axis=-1)
    rows = _round_up(v.shape[-1] // PACK_COLS, 8)
    v = jnp.pad(v, [(0, 0)] * len(lead) + [(0, rows * PACK_COLS - v.shape[-1])])
    return v.reshape(lead + (rows, PACK_COLS))


def _unpack(buf, shapes, lead=()):
    flat = buf.reshape(lead + (-1,))
    out, off = [], 0
    for s in shapes:
        n = 1
        for d in s:
            n *= d
        out.append(flat[..., off:off + n].reshape(lead + tuple(s)))
        off += _round_up(n, PACK_COLS)
    return out


def _to_blocks(full, axis):
    if axis is None:
        return jnp.broadcast_to(full[None], (N_DEV,) + full.shape)
    s = full.shape
    x = full.reshape(s[:axis] + (N_DEV, s[axis] // N_DEV) + s[axis + 1:])
    return jnp.moveaxis(x, axis, 0)


def _from_blocks(blocks, axis):
    x = jnp.moveaxis(blocks, 0, axis)
    s = x.shape
    return x.reshape(s[:axis] + (s[axis] * s[axis + 1],) + s[axis + 2:])


def _coords():
    return lax.axis_index("x"), lax.axis_index("y"), lax.axis_index("c")


def _flip(pos, mask):
    x, y, c = pos
    return (1 - x if mask & 4 else x, 1 - y if mask & 2 else y, 1 - c if mask & 1 else c)


def _index(pos):
    x, y, c = pos
    return 4 * x + 2 * y + c


def _comm_call(body, name, arrays, out_shapes):
    n = len(arrays)
    hbm = pl.BlockSpec(memory_space=pl.ANY)
    return pl.pallas_call(
        body, name=name, out_shape=out_shapes, in_specs=[hbm] * n, out_specs=[hbm] * n,
        scratch_shapes=[pltpu.SemaphoreType.DMA((7 * n,)), pltpu.SemaphoreType.DMA((7 * n,)),
                        pltpu.SemaphoreType.DMA((n,))],
    )(*arrays)


def _all_gather(blocks, name):
    chip_masks = (4, 2, 6)
    n = len(blocks)

    def body(*refs):
        x_refs, out_refs = refs[:n], refs[n:2 * n]
        send_sems, recv_sems, local_sems = refs[2 * n:]
        me = _coords()
        sibling = _flip(me, 1)

        def copy(a, k, owner, to, from_input=False):
            slot = out_refs[a].at[_index(owner)]
            return pltpu.make_async_remote_copy(
                src_ref=x_refs[a] if from_input else slot, dst_ref=slot,
                send_sem=send_sems.at[7 * a + k], recv_sem=recv_sems.at[7 * a + k],
                device_id=to, device_id_type=MESH)

        mine = [pltpu.make_async_copy(x_refs[a], out_refs[a].at[_index(me)], local_sems.at[a]) for a in range(n)]
        first = [copy(a, 0, me, sibling, True) for a in range(n)]
        first += [copy(a, 1 + j, me, _flip(me, m), True) for j, m in enumerate(chip_masks) for a in range(n)]
        for cp in mine + first:
            cp.start()
        passed = []
        for j, m in enumerate(chip_masks):
            for a in range(n):
                copy(a, 1 + j, _flip(me, m), me).wait_recv()
                passed.append(copy(a, 4 + j, _flip(me, m), sibling))
                passed[-1].start()
        for a in range(n):
            copy(a, 0, sibling, me).wait_recv()
            for j, m in enumerate(chip_masks):
                copy(a, 4 + j, _flip(sibling, m), me).wait_recv()
        for cp in first + passed:
            cp.wait_send()
        for cp in mine:
            cp.wait()

    return _comm_call(body, name, blocks, [jax.ShapeDtypeStruct((N_DEV,) + b.shape, b.dtype) for b in blocks])


def _exchange(segs, name, masks, slot_of, by_sender=False):
    n = len(segs)

    def body(*refs):
        seg_refs, out_refs = refs[:n], refs[n:2 * n]
        send_sems, recv_sems, local_sems = refs[2 * n:]
        me = _coords()

        def copy(a, k, sender):
            to = _flip(sender, masks[k])
            return pltpu.make_async_remote_copy(
                src_ref=seg_refs[a].at[slot_of(to)], dst_ref=out_refs[a].at[slot_of(sender) if by_sender else k],
                send_sem=send_sems.at[7 * a + k], recv_sem=recv_sems.at[7 * a + k],
                device_id=to, device_id_type=MESH)

        mine = [pltpu.make_async_copy(seg_refs[a].at[slot_of(me)], out_refs[a].at[slot_of(me)], local_sems.at[a])
                for a in range(n)] if by_sender else []
        sends = [copy(a, k, me) for k in range(len(masks)) for a in range(n)]
        for cp in mine + sends:
            cp.start()
        for k, mask in enumerate(masks):
            for a in range(n):
                copy(a, k, _flip(me, mask)).wait_recv()
        for cp in sends:
            cp.wait_send()
        for cp in mine:
            cp.wait()

    slots = lambda s: s.shape[0] if by_sender else len(masks)
    return _comm_call(body, name, segs, [jax.ShapeDtypeStruct((slots(s),) + s.shape[1:], s.dtype) for s in segs])


def _own_slot(tr, cols):
    return pl.BlockSpec((1, tr, cols), lambda i, slot: (slot[0], i, 0))


def _pair_sum(seg, slot, recv, name):
    _, rows, cols = seg.shape
    tr = max(t for t in range(16, rows + 1, 16) if rows % t == 0 and 2 * t * cols * 2 <= ADAM_TILE_BYTES)

    def body(slot_ref, own_ref, recv_ref, o_ref):
        o_ref[...] = (own_ref[0].astype(F32) + recv_ref[...].astype(F32)).astype(BF16)

    tile = pl.BlockSpec((tr, cols), lambda i, slot: (i, 0))
    return pl.pallas_call(
        body, name=name,
        grid_spec=pltpu.PrefetchScalarGridSpec(
            num_scalar_prefetch=1, grid=(rows // tr,), in_specs=[_own_slot(tr, cols), tile], out_specs=tile),
        out_shape=jax.ShapeDtypeStruct((rows, cols), BF16),
        compiler_params=_cparams(),
    )(slot, seg, recv)


def _adamw(seg, slot, recv, w, m, v, name):
    rows, cols = w.shape
    n_slots = recv.shape[0]
    sublanes = 8 * (4 // recv.dtype.itemsize)
    tr = max(t for t in range(sublanes, rows + 1, sublanes)
             if rows % t == 0 and N_DEV * t * cols * 4 <= ADAM_TILE_BYTES)
    c1 = 1.0 - ADAM_B1 ** ADAM_STEP
    c2 = 1.0 - ADAM_B2 ** ADAM_STEP

    def body(slot_ref, own_ref, r_ref, w_ref, m_ref, v_ref, g_out, d_out, m_out, v_out):
        g = own_ref[0].astype(F32)
        for s in range(n_slots):
            g = g + r_ref[s].astype(F32)
        m_new = ADAM_B1 * m_ref[...] + (1.0 - ADAM_B1) * g
        v_new = ADAM_B2 * v_ref[...] + (1.0 - ADAM_B2) * (g * g)
        m_hat = m_new / c1
        v_hat = v_new / c2
        g_out[...] = g
        d_out[...] = -ADAM_LR * (m_hat / (jnp.sqrt(v_hat) + ADAM_EPS) + ADAM_WD * w_ref[...])
        m_out[...] = m_new
        v_out[...] = v_new

    tile = pl.BlockSpec((tr, cols), lambda i, slot: (i, 0))
    return pl.pallas_call(
        body, name=name,
        grid_spec=pltpu.PrefetchScalarGridSpec(
            num_scalar_prefetch=1, grid=(rows // tr,),
            in_specs=[_own_slot(tr, cols), pl.BlockSpec((n_slots, tr, cols), lambda i, slot: (0, i, 0)),
                      tile, tile, tile],
            out_specs=[tile] * 4),
        out_shape=[jax.ShapeDtypeStruct(w.shape, F32)] * 4,
        compiler_params=_cparams(),
    )(slot, seg, recv, w, m, v)


def _mm(a, b, *, name, out_dtype, b_transposed=False):
    M, K = a.shape
    N = b.shape[0] if b_transposed else b.shape[1]
    tm = _pick(M, ROW_TILES)
    tn = _pick(N, (2816, 1024, 640, 512, 256, 128))
    tk = K if K <= 3200 else _pick(K, (2816, 1024, 640, 512, 256, 128))
    nk = K // tk

    def body(*refs):
        a_ref, b_ref, o_ref = refs[:3]
        acc_ref = refs[-1] if nk > 1 else None
        if b_transposed:
            prod = _dot_nt(a_ref[...], b_ref[...])
        else:
            prod = jnp.dot(a_ref[...], b_ref[...], preferred_element_type=F32)
        if nk == 1:
            o_ref[...] = prod.astype(out_dtype)
        else:
            k = pl.program_id(2)

            @pl.when(k == 0)
            def _():
                acc_ref[...] = prod

            @pl.when(k > 0)
            def _():
                acc_ref[...] += prod

            @pl.when(k == nk - 1)
            def _():
                o_ref[...] = acc_ref[...].astype(out_dtype)

    b_spec = (pl.BlockSpec((tn, tk), lambda i, j, k: (j, k)) if b_transposed
              else pl.BlockSpec((tk, tn), lambda i, j, k: (k, j)))
    return pl.pallas_call(
        body, name=name, grid=(M // tm, N // tn, nk),
        in_specs=[pl.BlockSpec((tm, tk), lambda i, j, k: (i, k)), b_spec],
        out_specs=pl.BlockSpec((tm, tn), lambda i, j, k: (i, j)),
        out_shape=jax.ShapeDtypeStruct((M, N), out_dtype),
        scratch_shapes=[pltpu.VMEM((tm, tn), F32)] if nk > 1 else [],
        compiler_params=_cparams(),
    )(a, b)


def _mm_tn(a, b, *, name):
    T, M = a.shape
    N = b.shape[1]
    tm = _pick(M, (1408, 1024, 512, 256, 128))
    tn = _pick(N, (1024, 640, 512, 256, 128))
    tt = _pick(T, [t for t in (4160, 1664, 640) if 2 * 2 * t * (tm + tn) <= MM_TN_WINDOW_BYTES] + [128])

    def body(a_ref, b_ref, o_ref):
        prod = lax.dot_general(a_ref[...], b_ref[...], (((0,), (0,)), ((), ())), preferred_element_type=F32)

        @pl.when(pl.program_id(2) == 0)
        def _():
            o_ref[...] = prod

        @pl.when(pl.program_id(2) > 0)
        def _():
            o_ref[...] += prod

    return pl.pallas_call(
        body, name=name, grid=(M // tm, N // tn, T // tt),
        in_specs=[pl.BlockSpec((tt, tm), lambda i, j, k: (k, i)), pl.BlockSpec((tt, tn), lambda i, j, k: (k, j))],
        out_specs=pl.BlockSpec((tm, tn), lambda i, j, k: (i, j)),
        out_shape=jax.ShapeDtypeStruct((M, N), F32),
        compiler_params=_cparams(),
    )(a, b)


def _layer_norm(z, g, b):
    mu = jnp.mean(z, axis=-1, keepdims=True)
    xc = z - mu
    var = jnp.mean(xc * xc, axis=-1, keepdims=True)
    return xc * lax.rsqrt(var + LN_EPS) * g + b


def _mm_ln(a, w, res, g, b, *, name, target=None, n_tok=None, a_transposed=False):
    K, M = a.shape if a_transposed else a.shape[::-1]
    D = w.shape[1]
    tm = _pick(M, ROW_TILES)
    final = target is not None

    def body(*refs):
        if final:
            a_ref, w_ref, r_ref, g_ref, b_ref, t_ref, dz_ref, dzb_ref, loss_ref, dg_ref, db_ref = refs
        else:
            a_ref, w_ref, r_ref, g_ref, b_ref, z_ref, h_ref, hb_ref = refs
        contract = (((0,), (0,)), ((), ())) if a_transposed else (((1,), (0,)), ((), ()))
        z = ALPHA * r_ref[...] + lax.dot_general(a_ref[...], w_ref[...], contract, preferred_element_type=F32)
        h = _layer_norm(z, g_ref[...], b_ref[...])
        if not final:
            z_ref[...] = z
            h_ref[...] = h
            hb_ref[...] = h.astype(BF16)
            return
        i = pl.program_id(0)
        row = i * tm + lax.broadcasted_iota(jnp.int32, (tm, 1), 0)
        valid = (row >= N_META) & (row < n_tok)
        err = jnp.where(valid, h - t_ref[...], 0.0)
        dy = err / D
        dz, xhat = _ln_bwd_rows(dy, z, g_ref[...])
        dz_ref[...] = dz
        dzb_ref[...] = dz.astype(BF16)

        @pl.when(i == 0)
        def _():
            loss_ref[...] = jnp.zeros_like(loss_ref)
            dg_ref[...] = jnp.zeros_like(dg_ref)
            db_ref[...] = jnp.zeros_like(db_ref)

        loss_ref[...] += 0.5 * jnp.sum(jnp.sum(err * err, axis=1, keepdims=True) / D, axis=0, keepdims=True)
        dg_ref[...] += jnp.sum(dy * xhat, axis=0, keepdims=True)
        db_ref[...] += jnp.sum(dy, axis=0, keepdims=True)

    row_blk = lambda cols: pl.BlockSpec((tm, cols), lambda i: (i, 0))
    vec = pl.BlockSpec((1, D), lambda i: (0, 0))
    a_blk = pl.BlockSpec((K, tm), lambda i: (0, i)) if a_transposed else row_blk(K)
    in_specs = [a_blk, pl.BlockSpec((K, D), lambda i: (0, 0)), row_blk(D), vec, vec]
    args = [a, w, res, g, b]
    if final:
        in_specs.append(row_blk(D))
        args.append(target)
        out_specs = [row_blk(D), row_blk(D), pl.BlockSpec((8, LANES), lambda i: (0, 0)), vec, vec]
        out_shape = [jax.ShapeDtypeStruct((M, D), F32), jax.ShapeDtypeStruct((M, D), BF16),
                     jax.ShapeDtypeStruct((8, LANES), F32), jax.ShapeDtypeStruct((1, D), F32),
                     jax.ShapeDtypeStruct((1, D), F32)]
    else:
        out_specs = [row_blk(D)] * 3
        out_shape = [jax.ShapeDtypeStruct((M, D), F32)] * 2 + [jax.ShapeDtypeStruct((M, D), BF16)]
    return pl.pallas_call(
        body, name=name, grid=(M // tm,), in_specs=in_specs, out_specs=out_specs, out_shape=out_shape,
        compiler_params=_cparams(),
    )(*args)


def _ln_bwd_rows(dh_v, z_v, g_v):
    mu = jnp.mean(z_v, axis=-1, keepdims=True)
    xc = z_v - mu
    rstd = lax.rsqrt(jnp.mean(xc * xc, axis=-1, keepdims=True) + LN_EPS)
    xhat = xc * rstd
    dxhat = dh_v * g_v
    dz = rstd * (dxhat - jnp.mean(dxhat, axis=-1, keepdims=True)
                 - xhat * jnp.mean(dxhat * xhat, axis=-1, keepdims=True))
    return dz, xhat


def _mm_ln_bwd(a, b, dz_next, z, g, *, name):
    M, K = a.shape
    D = b.shape[0]
    tm = _pick(M, ROW_TILES)
    tk = K if K <= 3200 else _pick(K, (2816, 1024, 640, 512, 256, 128))
    nk = K // tk

    def body(a_ref, b_ref, nx_ref, z_ref, g_ref, dz_ref, dzb_ref, dg_ref, db_ref, *scratch):
        i, k = pl.program_id(0), pl.program_id(1)
        prod = _dot_nt(a_ref[...], b_ref[...])

        @pl.when((i == 0) & (k == 0))
        def _():
            dg_ref[...] = jnp.zeros_like(dg_ref)
            db_ref[...] = jnp.zeros_like(db_ref)

        def finish(acc):
            dh_v = acc + ALPHA * nx_ref[...]
            dz, xhat = _ln_bwd_rows(dh_v, z_ref[...], g_ref[...])
            dz_ref[...] = dz
            dzb_ref[...] = dz.astype(BF16)
            dg_ref[...] += jnp.sum(dh_v * xhat, axis=0, keepdims=True)
            db_ref[...] += jnp.sum(dh_v, axis=0, keepdims=True)

        if nk == 1:
            finish(prod)
        else:
            acc_ref, = scratch

            @pl.when(k == 0)
            def _():
                acc_ref[...] = prod

            @pl.when(k > 0)
            def _():
                acc_ref[...] += prod

            @pl.when(k == nk - 1)
            def _():
                finish(acc_ref[...])

    row_blk = pl.BlockSpec((tm, D), lambda i, k: (i, 0))
    vec = pl.BlockSpec((1, D), lambda i, k: (0, 0))
    return pl.pallas_call(
        body, name=name, grid=(M // tm, nk),
        in_specs=[pl.BlockSpec((tm, tk), lambda i, k: (i, k)), pl.BlockSpec((D, tk), lambda i, k: (0, k)),
                  row_blk, row_blk, vec],
        out_specs=[row_blk, row_blk, vec, vec],
        out_shape=[jax.ShapeDtypeStruct((M, D), F32), jax.ShapeDtypeStruct((M, D), BF16),
                   jax.ShapeDtypeStruct((1, D), F32), jax.ShapeDtypeStruct((1, D), F32)],
        scratch_shapes=[pltpu.VMEM((tm, D), F32)] if nk > 1 else [],
        compiler_params=_cparams(),
    )(a, b, dz_next, z, g)


def _pool_fwd(h0, pw, scale, g, b, *, name):
    M, D = h0.shape
    n_groups, G, _ = pw.shape
    tm = _pick(M, ROW_TILES)

    def body(x_ref, halo_ref, pw_ref, sc_ref, g_ref, b_ref, z_ref, h_ref, hb_ref, diff_ref, ext_ref, mix_ref):
        i = pl.program_id(0)
        x = x_ref[...]
        ext_ref[0:POOL_HALO, :] = jnp.where(i == 0, 0.0, halo_ref[...])
        ext_ref[POOL_HALO:, :] = x
        tok = i * tm + lax.broadcasted_iota(jnp.int32, (tm, 1), 0)
        for gi, win in enumerate(POOL_WINDOWS):
            cols = slice(gi * G, (gi + 1) * G)
            xs = x[:, cols]
            s = xs
            for k in range(1, win):
                s = s + ext_ref[pl.ds(POOL_HALO - k, tm), cols]
            count = jnp.minimum(tok + 1, win).astype(F32)
            d = (s / count - xs).astype(BF16)
            diff_ref[:, cols] = d
            mix_ref[:, cols] = jnp.dot(d, pw_ref[gi], preferred_element_type=F32)
        z = ALPHA * x + mix_ref[...] * sc_ref[...]
        z_ref[...] = z
        h = _layer_norm(z, g_ref[...], b_ref[...])
        h_ref[...] = h
        hb_ref[...] = h.astype(BF16)

    row_blk = pl.BlockSpec((tm, D), lambda i: (i, 0))
    vec = pl.BlockSpec((1, D), lambda i: (0, 0))
    halo = pl.BlockSpec((POOL_HALO, D), lambda i: (jnp.maximum(i * (tm // POOL_HALO) - 1, 0), 0))
    return pl.pallas_call(
        body, name=name, grid=(M // tm,),
        in_specs=[row_blk, halo, pl.BlockSpec((n_groups, G, G), lambda i: (0, 0, 0)), vec, vec, vec],
        out_specs=[row_blk] * 4,
        out_shape=[jax.ShapeDtypeStruct((M, D), F32)] * 2 + [jax.ShapeDtypeStruct((M, D), BF16)] * 2,
        scratch_shapes=[pltpu.VMEM((tm + POOL_HALO, D), F32), pltpu.VMEM((tm, D), F32)],
        compiler_params=_cparams(),
    )(h0, h0, pw, scale, g, b)


def _pool_bwd(dz, diff, pw, scale, *, name):
    M, D = dz.shape
    n_groups, G, _ = pw.shape
    tm = _pick(M, ROW_TILES)
    nt = M // tm

    def body(dz_ref, halo_ref, diff_ref, pw_ref, sc_ref, dh_ref, dpw_ref, dsc_ref, ext_ref, q_ref):
        i = pl.program_id(0)
        dz_v = dz_ref[...]
        ext_ref[0:tm, :] = dz_v
        ext_ref[tm:, :] = jnp.where(i == nt - 1, 0.0, halo_ref[...])
        tok = i * tm + lax.broadcasted_iota(jnp.int32, (tm + POOL_HALO, 1), 0)

        @pl.when(i == 0)
        def _():
            dpw_ref[...] = jnp.zeros_like(dpw_ref)
            dsc_ref[...] = jnp.zeros_like(dsc_ref)

        for gi, win in enumerate(POOL_WINDOWS):
            cols = slice(gi * G, (gi + 1) * G)
            dmix = (ext_ref[:, cols] * sc_ref[:, cols]).astype(BF16)
            ddiff = _dot_nt(dmix, pw_ref[gi])
            count = jnp.minimum(tok + 1, win).astype(F32)
            q_ref[:, cols] = ddiff / count
            acc = -ddiff[0:tm]
            for k in range(win):
                acc = acc + q_ref[pl.ds(k, tm), cols]
            dh_ref[:, cols] = ALPHA * dz_v[:, cols] + acc
            d = diff_ref[:, cols]
            dpw_ref[gi] += lax.dot_general(d, dmix[0:tm], (((0,), (0,)), ((), ())), preferred_element_type=F32)
            mixed = jnp.dot(d, pw_ref[gi], preferred_element_type=F32)
            dsc_ref[:, cols] += jnp.sum(dz_v[:, cols] * mixed, axis=0, keepdims=True)

    row_blk = pl.BlockSpec((tm, D), lambda i: (i, 0))
    vec = pl.BlockSpec((1, D), lambda i: (0, 0))
    per_tile = tm // POOL_HALO
    halo = pl.BlockSpec((POOL_HALO, D), lambda i: (jnp.minimum((i + 1) * per_tile, nt * per_tile - 1), 0))
    wblk = pl.BlockSpec((n_groups, G, G), lambda i: (0, 0, 0))
    return pl.pallas_call(
        body, name=name, grid=(nt,),
        in_specs=[row_blk, halo, row_blk, wblk, vec],
        out_specs=[row_blk, wblk, vec],
        out_shape=[jax.ShapeDtypeStruct((M, D), F32), jax.ShapeDtypeStruct((n_groups, G, G), F32),
                   jax.ShapeDtypeStruct((1, D), F32)],
        scratch_shapes=[pltpu.VMEM((tm + POOL_HALO, D), F32), pltpu.VMEM((tm + POOL_HALO, D), F32)],
        compiler_params=_cparams(),
    )(dz, dz, diff, pw, scale)


def _glu_interleave(x):
    s = x.shape
    n = s[-1] // (2 * GLU_CHUNK)
    return jnp.swapaxes(x.reshape(s[:-1] + (2, n, GLU_CHUNK)), -3, -2).reshape(s)


def _glu_deinterleave(x):
    s = x.shape
    n = s[-1] // (2 * GLU_CHUNK)
    return jnp.swapaxes(x.reshape(s[:-1] + (n, 2, GLU_CHUNK)), -3, -2).reshape(s)


W_PACK_ROWS = 256
W_PACK_PIECE = 64


def _interleaved_col(col, F):
    half, within = divmod(col, F)
    chunk, off = divmod(within, GLU_CHUNK)
    return chunk * 2 * GLU_CHUNK + half * GLU_CHUNK + off


def _w_in_grad_segments(dw, *, name):
    D, F2 = dw.shape
    width = F2 // N_DEV
    rows = _pick(D, (W_PACK_ROWS, 128))
    assert width % W_PACK_PIECE == 0

    def body(x_ref, o_ref):
        for j in range(N_DEV):
            for q in range(0, width, W_PACK_PIECE):
                at = _interleaved_col(width * j + q, F2 // 2)
                o_ref[j % 2, j // 2, :, q:q + W_PACK_PIECE] = x_ref[:, at:at + W_PACK_PIECE].astype(BF16)

    return pl.pallas_call(
        body, name=name, grid=(D // rows,),
        in_specs=[pl.BlockSpec((rows, F2), lambda r: (r, 0))],
        out_specs=pl.BlockSpec((2, N_CHIPS, rows, width), lambda r: (0, 0, r, 0)),
        out_shape=jax.ShapeDtypeStruct((2, N_CHIPS, D, width), BF16),
        compiler_params=_cparams(),
    )(dw)


def _w_in_interleaved(gathered, *, name):
    n_dev, depth, D, width = gathered.shape
    F = n_dev * width // 2
    rows = _pick(D, (W_PACK_ROWS, 128))
    assert width % W_PACK_PIECE == 0 and GLU_CHUNK % W_PACK_PIECE == 0 and F % GLU_CHUNK == 0
    interleaved = lambda col: _interleaved_col(col, F)

    def body(x_ref, o_ref):
        for j in range(n_dev):
            for q in range(0, width, W_PACK_PIECE):
                to = interleaved(width * j + q)
                o_ref[0, :, to:to + W_PACK_PIECE] = x_ref[j, 0, :, q:q + W_PACK_PIECE]

    return pl.pallas_call(
        body, name=name, grid=(depth, D // rows),
        in_specs=[pl.BlockSpec((n_dev, 1, rows, width), lambda l, r: (0, l, r, 0))],
        out_specs=pl.BlockSpec((1, rows, n_dev * width), lambda l, r: (l, r, 0)),
        out_shape=jax.ShapeDtypeStruct((depth, D, n_dev * width), gathered.dtype),
        compiler_params=_cparams(),
    )(gathered)


def _conv_glu_fwd(u, cw, cb, *, name):
    M, F2 = u.shape
    tm = _pick(M, CONV_ROW_TILES)
    tc = 2 * GLU_CHUNK

    def body(u_ref, halo_ref, w_ref, b_ref, o_ref, c_ref):
        i = pl.program_id(0)
        halo = jnp.where(i == 0, 0.0, halo_ref[...].astype(F32))
        w0, w1, w2, b = w_ref[0:1, :], w_ref[1:2, :], w_ref[2:3, :], b_ref[...]
        for r0 in range(0, tm, CONV_STRIP):
            before = halo if r0 == 0 else u_ref[pl.ds(r0 - CONV_HALO, CONV_HALO), :].astype(F32)
            win = jnp.concatenate([before, u_ref[pl.ds(r0, CONV_STRIP), :].astype(F32)], axis=0)
            u0, u1, u2 = (win[CONV_HALO - k:CONV_HALO - k + CONV_STRIP] for k in range(3))
            c = b + w0 * u2 + w1 * u1 + w2 * u0
            c_ref[pl.ds(r0, CONV_STRIP), :] = c.astype(BF16)
            a, g = c[:, :GLU_CHUNK], c[:, GLU_CHUNK:]
            o_ref[pl.ds(r0, CONV_STRIP), :] = (a * jax.nn.sigmoid(a) * g).astype(BF16)

    per_tile = tm // CONV_HALO
    return pl.pallas_call(
        body, name=name, grid=(M // tm, F2 // tc),
        in_specs=[pl.BlockSpec((tm, tc), lambda i, j: (i, j)),
                  pl.BlockSpec((CONV_HALO, tc), lambda i, j: (jnp.maximum(i * per_tile - 1, 0), j)),
                  pl.BlockSpec((3, tc), lambda i, j: (0, j)), pl.BlockSpec((1, tc), lambda i, j: (0, j))],
        out_specs=[pl.BlockSpec((tm, GLU_CHUNK), lambda i, j: (i, j)), pl.BlockSpec((tm, tc), lambda i, j: (i, j))],
        out_shape=[jax.ShapeDtypeStruct((M, F2 // 2), BF16), jax.ShapeDtypeStruct((M, F2), BF16)],
        compiler_params=_cparams(),
    )(u, u, cw, cb)


def _conv_glu_bwd(u, c, dact, cw, *, name):
    M, F2 = u.shape
    tm = _pick(M, CONV_ROW_TILES)
    nt = M // tm
    tc = 2 * GLU_CHUNK

    def body(u_ref, c_ref, da_ref, w_ref, du_ref, dw_ref, db_ref, dcx_ref, carry_ref):
        i = pl.program_id(1)
        w0, w1, w2 = w_ref[0:1, :], w_ref[1:2, :], w_ref[2:3, :]

        @pl.when(i == 0)
        def _():
            dw_ref[...] = jnp.zeros_like(dw_ref)
            db_ref[...] = jnp.zeros_like(db_ref)
            carry_ref[...] = jnp.zeros_like(carry_ref)

        def fold(t):
            return sum(t[r:r + 8] for r in range(0, CONV_STRIP, 8))

        dcx_ref[tm:, :] = carry_ref[...]
        s_b = s_0 = s_1 = s_2 = jnp.zeros((8, tc), F32)
        for r0 in reversed(range(0, tm, CONV_STRIP)):
            rows = pl.ds(r0, CONV_STRIP)
            c_v = c_ref[rows, :].astype(F32)
            a, g = c_v[:, :GLU_CHUNK], c_v[:, GLU_CHUNK:]
            sig = jax.nn.sigmoid(a)
            dact_v = da_ref[rows, :].astype(F32)
            d_a = dact_v * g * (sig * (1.0 + a * (1.0 - sig)))
            d_g = dact_v * (a * sig)
            dc = jnp.concatenate([d_a, d_g], axis=1)
            dcx_ref[rows, :] = dc
            dc1, dc2 = dcx_ref[pl.ds(r0 + 1, CONV_STRIP), :], dcx_ref[pl.ds(r0 + 2, CONV_STRIP), :]
            du_ref[rows, :] = (w2 * dc + w1 * dc1 + w0 * dc2).astype(BF16)
            u_v = u_ref[rows, :].astype(F32)
            s_b, s_0, s_1, s_2 = s_b + fold(dc), s_0 + fold(dc2 * u_v), s_1 + fold(dc1 * u_v), s_2 + fold(dc * u_v)
        carry_ref[...] = dcx_ref[0:CONV_HALO, :]
        db_ref[...] += jnp.sum(s_b, axis=0, keepdims=True)
        dw_ref[0:1, :] += jnp.sum(s_0, axis=0, keepdims=True)
        dw_ref[1:2, :] += jnp.sum(s_1, axis=0, keepdims=True)
        dw_ref[2:3, :] += jnp.sum(s_2, axis=0, keepdims=True)

    per_tile = tm // CONV_HALO
    rev = lambda i: nt - 1 - i
    return pl.pallas_call(
        body, name=name, grid=(F2 // tc, nt),
        in_specs=[pl.BlockSpec((tm, tc), lambda j, i: (rev(i), j)), pl.BlockSpec((tm, tc), lambda j, i: (rev(i), j)),
                  pl.BlockSpec((tm, GLU_CHUNK), lambda j, i: (rev(i), j)),
                  pl.BlockSpec((3, tc), lambda j, i: (0, j))],
        out_specs=[pl.BlockSpec((tm, tc), lambda j, i: (rev(i), j)),
                   pl.BlockSpec((3, tc), lambda j, i: (0, j)), pl.BlockSpec((1, tc), lambda j, i: (0, j))],
        out_shape=[jax.ShapeDtypeStruct((M, F2), BF16), jax.ShapeDtypeStruct((3, F2), F32),
                   jax.ShapeDtypeStruct((1, F2), F32)],
        scratch_shapes=[pltpu.VMEM((tm + CONV_HALO, tc), F32), pltpu.VMEM((CONV_HALO, tc), F32)],
        compiler_params=_cparams(),
    )(u, c, dact, cw)


def _split3(x):
    hi = x.astype(BF16)
    r = x - hi.astype(F32)
    mid = r.astype(BF16)
    lo = (r - mid.astype(F32)).astype(BF16)
    return hi, mid, lo


def _tri_sum(tri, x):
    return sum(jnp.dot(tri, part, preferred_element_type=F32) for part in _split3(x))


def _log_sigmoid(x):
    return jnp.minimum(x, 0.0) - jnp.log1p(jnp.exp(-jnp.abs(x)))


def _forget_cumsum(pre, bias, *, name):
    M, C = pre.shape
    tm = _pick(M, ROW_TILES)

    def body(p_ref, b_ref, c_ref, carry_ref):
        i = pl.program_id(0)

        @pl.when(i == 0)
        def _():
            carry_ref[...] = jnp.zeros_like(carry_ref)

        logf = _log_sigmoid(p_ref[...] + b_ref[...])
        r = lax.broadcasted_iota(jnp.int32, (tm, tm), 0)
        s = lax.broadcasted_iota(jnp.int32, (tm, tm), 1)
        c_ref[...] = _tri_sum((s <= r).astype(BF16), logf) + carry_ref[...]
        carry_ref[...] = c_ref[pl.ds(tm - 1, 1), :]

    return pl.pallas_call(
        body, name=name, grid=(M // tm,),
        in_specs=[pl.BlockSpec((tm, C), lambda i: (i, 0)), pl.BlockSpec((1, C), lambda i: (0, 0))],
        out_specs=pl.BlockSpec((tm, C), lambda i: (i, 0)),
        out_shape=jax.ShapeDtypeStruct((M, C), F32),
        scratch_shapes=[pltpu.VMEM((1, C), F32)],
        compiler_params=_cparams(),
    )(pre, bias)


def _forget_cumsum_bwd(dc, pre, bias, *, name):
    M, C = pre.shape
    tm = _pick(M, ROW_TILES)
    nt = M // tm

    def body(dc_ref, p_ref, b_ref, dp_ref, db_ref, carry_ref, run_ref):
        i = pl.program_id(0)

        @pl.when(i == 0)
        def _():
            carry_ref[...] = jnp.zeros_like(carry_ref)
            db_ref[...] = jnp.zeros_like(db_ref)

        r = lax.broadcasted_iota(jnp.int32, (tm, tm), 0)
        s = lax.broadcasted_iota(jnp.int32, (tm, tm), 1)
        run_ref[...] = _tri_sum((s >= r).astype(BF16), dc_ref[...]) + carry_ref[...]
        carry_ref[...] = run_ref[pl.ds(0, 1), :]
        dpre = run_ref[...] * jax.nn.sigmoid(-(p_ref[...] + b_ref[...]))
        dp_ref[...] = dpre.astype(BF16)
        db_ref[...] += jnp.sum(dpre, axis=0, keepdims=True)

    rev_blk = pl.BlockSpec((tm, C), lambda i: (nt - 1 - i, 0))
    vec = pl.BlockSpec((1, C), lambda i: (0, 0))
    return pl.pallas_call(
        body, name=name, grid=(nt,),
        in_specs=[rev_blk, rev_blk, vec], out_specs=[rev_blk, vec],
        out_shape=[jax.ShapeDtypeStruct((M, C), BF16), jax.ShapeDtypeStruct((1, C), F32)],
        scratch_shapes=[pltpu.VMEM((1, C), F32), pltpu.VMEM((tm, C), F32)],
        compiler_params=_cparams(),
    )(dc, pre, bias)


def _causal_mask(tm):
    key = lax.broadcasted_iota(jnp.int32, (tm, tm), 0)
    query = lax.broadcasted_iota(jnp.int32, (tm, tm), 1)
    return key <= query


def _dot_nt(a, b):
    return lax.dot_general(a, b, (((1,), (1,)), ((), ())), preferred_element_type=F32)


def _loop_unrolled(n, step, init, unroll):
    def trip(p, carry):
        for r in range(unroll):
            carry = step(unroll * p + r, carry)
        return carry
    carry = lax.fori_loop(0, n // unroll, trip, init)
    return lax.fori_loop(unroll * (n // unroll), n, step, carry)


def _rows(main, extras, total):
    tm = main.shape[1]
    used = sum(e.shape[0] for e in extras)
    tile = jnp.concatenate([e.astype(BF16) for e in extras] + [jnp.zeros((ATTN_EXTRA - used, tm), BF16)], axis=0)
    rest = total - main.shape[0] - ATTN_EXTRA
    return jnp.concatenate([main, tile] + ([jnp.zeros((rest, tm), BF16)] if rest else []), axis=0)


def _attn_specs(H, M, dh, tm):
    nt = M // tm
    qT_blk = pl.BlockSpec((1, dh, tm), lambda h, i: (h, 0, i))
    row_blk = pl.BlockSpec((1, 1, tm), lambda h, i: (h, 0, i))
    key_blk = pl.BlockSpec((1, M, LANES), lambda h, i: (h, 0, 0))
    keyT_blk = pl.BlockSpec((1, nt, dh + ATTN_EXTRA, tm), lambda h, i: (h, 0, 0, 0))
    return qT_blk, row_blk, key_blk, keyT_blk


def _pair_specs(M, dh, tm):
    assert 2 * dh == LANES
    cols = lambda section: pl.BlockSpec((tm, LANES), lambda p, i: (i, section + p))
    headsT = pl.BlockSpec((2, dh, tm), lambda p, i: (p, 0, i))
    return cols, headsT


def _attn_pack(qkv, c, H, *, name):
    M, D3 = qkv.shape
    D = D3 // 3
    dh = D // H
    tm = _pick(M, ROW_TILES)
    nt = M // tm
    cols, headsT = _pair_specs(M, dh, tm)
    key_blk = pl.BlockSpec((2, tm, LANES), lambda p, i: (p, i, 0))
    keyT_blk = pl.BlockSpec((2, 1, dh + ATTN_EXTRA, tm), lambda p, i: (p, i, 0, 0))

    def body(k_ref, v_ref, q_ref, c_ref, ka_ref, va_ref, qT_ref, kT_ref, vT_ref):
        p = pl.program_id(0)
        lane = lax.broadcasted_iota(jnp.int32, (tm, LANES), 1)
        k_v, v_v = k_ref[...], v_ref[...]
        ones_v = jnp.where((lane >= dh) & (lane < dh + 3), 1.0, 0.0)
        for e in range(2):
            ck = jnp.sum(jnp.where(lane == 2 * p + e, c_ref[...], 0.0), axis=1, keepdims=True)
            hi, mid, lo = _split3(-ck)
            extra = jnp.where(lane == dh + 3, hi.astype(F32), jnp.where(lane == dh + 4, mid.astype(F32),
                              jnp.where(lane == dh + 5, lo.astype(F32), ones_v)))
            first = lambda t: t if e == 0 else pltpu.roll(t, dh, 1)
            ka_ref[e] = jnp.where(lane < dh, first(k_v), extra.astype(BF16))
            va_ref[e] = jnp.where(lane < dh, first(v_v), ones_v.astype(BF16))
        row = lax.broadcasted_iota(jnp.int32, (ATTN_EXTRA, tm), 0)
        tail = jnp.where(row == 0, 1.0, 0.0).astype(BF16)
        for src, dst in ((k_v, kT_ref), (v_v, vT_ref)):
            t = src.T
            for e in range(2):
                dst[e, 0, 0:dh, :] = t[e * dh:(e + 1) * dh]
                dst[e, 0, dh:, :] = tail
        qT_ref[...] = q_ref[...].T.reshape(2, dh, tm)

    n_sec = D // LANES
    return pl.pallas_call(
        body, name=name, grid=(H // 2, nt),
        in_specs=[cols(0), cols(n_sec), cols(2 * n_sec), pl.BlockSpec((tm, LANES), lambda p, i: (i, 0))],
        out_specs=[key_blk, key_blk, headsT, keyT_blk, keyT_blk],
        out_shape=[jax.ShapeDtypeStruct((H, M, LANES), BF16)] * 2 + [jax.ShapeDtypeStruct((H, dh, M), BF16)]
        + [jax.ShapeDtypeStruct((H, nt, dh + ATTN_EXTRA, tm), BF16)] * 2,
        compiler_params=_cparams(),
    )(qkv, qkv, qkv, c)


def _heads_merge(parts, M, tm, *, name):
    H = parts[0].shape[0]
    dh = LANES // 2
    blocked = [t.ndim == 4 for t in parts]
    cols, headsT = _pair_specs(M, dh, tm)

    def body(*refs):
        for src, dst, b in zip(refs[:len(parts)], refs[len(parts):], blocked):
            t = jnp.concatenate([src[e, 0, 0:dh, :] for e in range(2)], axis=0) if b else src[...].reshape(2 * dh, tm)
            dst[...] = t.astype(BF16).T

    in_specs = [pl.BlockSpec((2, 1, t.shape[2], tm), lambda p, i: (p, i, 0, 0)) if b else headsT
                for t, b in zip(parts, blocked)]
    return pl.pallas_call(
        body, name=name, grid=(H // 2, M // tm), in_specs=in_specs, out_specs=[cols(0)] * len(parts),
        out_shape=[jax.ShapeDtypeStruct((M, H * dh), BF16)] * len(parts), compiler_params=_cparams(),
    )(*parts)


def _attn_fwd(qT, cq, k_aug, vT_aug, *, name):
    H, dh, M = qT.shape
    tm = vT_aug.shape[-1]
    nt = M // tm
    qT_blk = pl.BlockSpec((ATTN_HEADS, dh, tm), lambda p, i: (p, 0, i))
    row_blk = pl.BlockSpec((ATTN_HEADS, 1, tm), lambda p, i: (p, 0, i))
    key_blk = pl.BlockSpec((ATTN_HEADS, M, LANES), lambda p, i: (p, 0, 0))
    keyT_blk = pl.BlockSpec((ATTN_HEADS, nt, dh + ATTN_EXTRA, tm), lambda p, i: (p, 0, 0, 0))

    def body(qT_ref, cq_ref, k_ref, vT_ref, oT_ref, lse_ref):
        i = pl.program_id(1)
        ones = jnp.ones((3, tm), BF16)
        qa = [_rows(qT_ref[e] * jnp.asarray(dh ** -0.5, BF16), [*_split3(cq_ref[e]), ones], LANES)
              for e in range(ATTN_HEADS)]

        def block(j, carry, masked):
            keys = pl.ds(pl.multiple_of(j * tm, tm), tm)
            out = []
            for e, (m, acc) in enumerate(carry):
                sT = jnp.dot(k_ref[e, keys, :], qa[e], preferred_element_type=F32)
                if masked:
                    sT = jnp.where(_causal_mask(tm), sT, NEG_INF)
                m_new = jnp.maximum(m, jnp.max(sT, axis=0, keepdims=True))
                pT = jnp.exp(sT - m_new).astype(BF16)
                out.append((m_new, jnp.exp(m - m_new) * acc + jnp.dot(vT_ref[e, j], pT, preferred_element_type=F32)))
            return tuple(out)

        init = tuple((jnp.full((1, tm), NEG_INF, F32), jnp.zeros((dh + ATTN_EXTRA, tm), F32)) for _ in range(ATTN_HEADS))
        done = block(i, _loop_unrolled(i, lambda j, c: block(j, c, False), init, 4), True)
        for e, (m, acc) in enumerate(done):
            l = acc[dh:dh + 1, :]
            oT_ref[e] = (acc[0:dh, :] / l).astype(BF16)
            lse_ref[e] = m + jnp.log(l)

    return pl.pallas_call(
        body, name=name, grid=(H // ATTN_HEADS, nt),
        in_specs=[qT_blk, row_blk, key_blk, keyT_blk], out_specs=[qT_blk, row_blk],
        out_shape=[jax.ShapeDtypeStruct((H, dh, M), BF16), jax.ShapeDtypeStruct((H, 1, M), F32)],
        compiler_params=_cparams(),
    )(qT, cq, k_aug, vT_aug)


def _attn_bwd(qT, oT, doT, lse, cq, k_aug, v_aug, kT_aug, *, name):
    H, dh, M = qT.shape
    tm = kT_aug.shape[-1]
    qT_blk, row_blk, key_blk, keyT_blk = _attn_specs(H, M, dh, tm)
    dvT_blk = pl.BlockSpec((1, M // tm, dh, tm), lambda h, i: (h, 0, 0, 0))

    def body(qT_ref, oT_ref, doT_ref, lse_ref, cq_ref, k_ref, v_ref, kT_ref, dqT_ref, dcq_ref, dkT_ref, dvT_ref):
        i = pl.program_id(1)

        @pl.when(i == 0)
        def _():
            dkT_ref[...] = jnp.zeros_like(dkT_ref)
            dvT_ref[...] = jnp.zeros_like(dvT_ref)

        qsT = qT_ref[0] * jnp.asarray(dh ** -0.5, BF16)
        doT = doT_ref[0]
        delta = jnp.sum(doT.astype(F32) * oT_ref[0].astype(F32), axis=0, keepdims=True)
        ones = jnp.ones((3, tm), BF16)
        qa = _rows(qsT, [*_split3(cq_ref[0] - lse_ref[0]), ones], LANES)
        da = _rows(doT, _split3(-delta), LANES)
        q1 = _rows(qsT, [ones[0:1]], dh + ATTN_EXTRA)

        def block(j, dq, masked):
            keys = pl.ds(pl.multiple_of(j * tm, tm), tm)
            pT = jnp.exp(jnp.dot(k_ref[0, keys, :], qa, preferred_element_type=F32))
            if masked:
                pT = jnp.where(_causal_mask(tm), pT, 0.0)
            ds_b = (pT * jnp.dot(v_ref[0, keys, :], da, preferred_element_type=F32)).astype(BF16)
            dvT_ref[0, j] += _dot_nt(doT, pT.astype(BF16))
            dkT_ref[0, j] += _dot_nt(q1, ds_b)
            return dq + jnp.dot(kT_ref[0, j], ds_b, preferred_element_type=F32)

        init = jnp.zeros((dh + ATTN_EXTRA, tm), F32)
        dq = block(i, _loop_unrolled(i, lambda j, c: block(j, c, False), init, 4), True)
        dqT_ref[0] = (dq[0:dh, :] * dh ** -0.5).astype(BF16)
        dcq_ref[0] = dq[dh:dh + 1, :]

    return pl.pallas_call(
        body, name=name, grid=(H, M // tm),
        in_specs=[qT_blk, qT_blk, qT_blk, row_blk, row_blk, key_blk, key_blk, keyT_blk],
        out_specs=[qT_blk, row_blk, keyT_blk, dvT_blk],
        out_shape=[jax.ShapeDtypeStruct((H, dh, M), BF16), jax.ShapeDtypeStruct((H, 1, M), F32),
                   jax.ShapeDtypeStruct(kT_aug.shape, F32), jax.ShapeDtypeStruct((H, M // tm, dh, tm), F32)],
        compiler_params=_cparams(),
    )(qT, oT, doT, lse, cq, k_aug, v_aug, kT_aug)


def kernel(x, meta, pool_w, pool_scale, w_kv, w_f, b_f, w_q, w_o, ffn_w_in, ffn_conv_w, ffn_conv_b, ffn_w_out, ln_g, ln_b, loss_target, m_meta, m_pool_w, m_pool_scale, m_w_kv, m_w_f, m_b_f, m_w_q, m_w_o, m_ffn_w_in, m_ffn_conv_w, m_ffn_conv_b, m_ffn_w_out, m_ln_g, m_ln_b, v_meta, v_pool_w, v_pool_scale, v_w_kv, v_w_f, v_b_f, v_w_q, v_w_o, v_ffn_w_in, v_ffn_conv_w, v_ffn_conv_b, v_ffn_w_out, v_ln_g, v_ln_b):
    local = dict(meta=meta, pool_w=pool_w, pool_scale=pool_scale, w_kv=w_kv, w_f=w_f, b_f=b_f, w_q=w_q, w_o=w_o,
                 ffn_w_in=ffn_w_in, ffn_conv_w=ffn_conv_w, ffn_conv_b=ffn_conv_b, ffn_w_out=ffn_w_out,
                 ln_g=ln_g, ln_b=ln_b)
    mom1 = dict(meta=m_meta, pool_w=m_pool_w, pool_scale=m_pool_scale, w_kv=m_w_kv, w_f=m_w_f, b_f=m_b_f,
                w_q=m_w_q, w_o=m_w_o, ffn_w_in=m_ffn_w_in, ffn_conv_w=m_ffn_conv_w, ffn_conv_b=m_ffn_conv_b,
                ffn_w_out=m_ffn_w_out, ln_g=m_ln_g, ln_b=m_ln_b)
    mom2 = dict(meta=v_meta, pool_w=v_pool_w, pool_scale=v_pool_scale, w_kv=v_w_kv, w_f=v_w_f, b_f=v_b_f,
                w_q=v_w_q, w_o=v_w_o, ffn_w_in=v_ffn_w_in, ffn_conv_w=v_ffn_conv_w, ffn_conv_b=v_ffn_conv_b,
                ffn_w_out=v_ffn_w_out, ln_g=v_ln_g, ln_b=v_ln_b)
    axis_of = dict(PARAMS)

    S, D = x.shape[1], x.shape[2]
    H = b_f.shape[0]
    dh = D // H
    n_tok = N_META + S
    M = _round_up(n_tok, ROW_ALIGN)
    tm = _pick(M, ROW_TILES)
    nt = M // tm
    F2 = ffn_conv_b.shape[1]
    F = F2 // 2
    depth = ffn_conv_b.shape[0]

    small_sharded = [n for n in SMALL if axis_of[n] is not None]
    got = _all_gather([local[n].astype(BF16) for n in BIG] + [_pack([local[n] for n in small_sharded])],
                      "gather_weights")
    wb = {n: _from_blocks(blk, axis_of[n]) for n, blk in zip(BIG, got) if n != "ffn_w_in"}
    small_blocks = _unpack(got[-1], [local[n].shape for n in small_sharded], lead=(N_DEV,))
    wf32 = {n: _from_blocks(blk, axis_of[n]) for n, blk in zip(small_sharded, small_blocks)}

    pw = wb["pool_w"][0]
    wf_pad = jnp.pad(wf32["w_f"].astype(BF16), ((0, 0), (0, LANES - H)))
    w_qkv = jnp.concatenate([wb["w_kv"], wb["w_q"][0]], axis=1)
    w_att = jnp.concatenate([w_qkv, wf_pad], axis=1)
    wo = wb["w_o"][0]
    w_in = _w_in_interleaved(got[BIG.index("ffn_w_in")], name="w_in_interleave")
    w_out = wb["ffn_w_out"]
    conv_w = _glu_interleave(wf32["ffn_conv_w"])
    conv_b = _glu_interleave(ffn_conv_b)[:, None, :]
    scale = wf32["pool_scale"]
    g_ln, b_ln = wf32["ln_g"], wf32["ln_b"]
    ln = lambda i, j: (g_ln[i, j][None, :], b_ln[i, j][None, :])
    bias_f = jnp.pad(b_f, (0, LANES - H))[None, :]

    pad_rows = M - n_tok
    h0 = jnp.concatenate([wf32["meta"], x[0], jnp.zeros((pad_rows, D), F32)], axis=0)
    target = jnp.concatenate([jnp.zeros((N_META, D), F32), loss_target[0], jnp.zeros((pad_rows, D), F32)], axis=0)

    z1, h1, h1b, diff = _pool_fwd(h0, pw, scale, *ln(0, 0), name="pool_fwd")
    u0 = _mm(h1b, w_in[0], name="ffn0_up", out_dtype=BF16)
    act0, c0 = _conv_glu_fwd(u0, conv_w[0], conv_b[0], name="ffn0_conv")
    z2, h2, h2b = _mm_ln(act0, w_out[0], h1, *ln(0, 1), name="ffn0_down_ln")

    qkv = _mm(h2b, w_qkv, name="attn_qkv", out_dtype=BF16)
    pre = _mm(h2b, wf_pad, name="attn_gate", out_dtype=F32)
    c = _forget_cumsum(pre, bias_f, name="attn_cumsum")
    k_aug, v_aug, qT_h, kT_aug, vT_aug = _attn_pack(qkv, c, H, name="attn_pack")
    cq = c[:, :H].T[:, None, :]
    oT_h, lse = _attn_fwd(qT_h, cq, k_aug, vT_aug, name="attn_fwd")
    oT = oT_h.reshape(D, M)
    z3, h3, h3b = _mm_ln(oT, wo, h2, *ln(1, 0), name="attn_out_ln", a_transposed=True)

    u1 = _mm(h3b, w_in[1], name="ffn1_up", out_dtype=BF16)
    act1, c1 = _conv_glu_fwd(u1, conv_w[1], conv_b[1], name="ffn1_conv")
    dz4, dz4b, loss_part, dg11, db11 = _mm_ln(act1, w_out[1], h3, *ln(1, 1), name="ffn1_down_loss", target=target,
                                              n_tok=n_tok)
    loss = lax.psum(loss_part[0, 0], ("x", "y", "c"))

    grads = {}

    def ffn_bwd(layer, dz, dz_b, u, c_conv, act, h_in_b, z_in, g_in, tag):
        dact = _mm(dz_b, w_out[layer], name=tag + "_dact", out_dtype=BF16, b_transposed=True)
        du, dcw, dcb = _conv_glu_bwd(u, c_conv, dact, conv_w[layer], name=tag + "_conv_bwd")
        dz_in = _mm_ln_bwd(du, w_in[layer], dz, z_in, g_in[None, :], name=tag + "_dh_ln_bwd")
        d_w_out = _mm_tn(act, dz_b, name=tag + "_dw_out")
        d_w_in = _w_in_grad_segments(_mm_tn(h_in_b, du, name=tag + "_dw_in"), name=tag + "_dw_in_segments")
        return dz_in, d_w_in, d_w_out, _glu_deinterleave(dcw), _glu_deinterleave(dcb)[0]

    (dz3, dz3b, dg10, db10), dwin1, dwout1, dcw1, dcb1 = ffn_bwd(1, dz4, dz4b, u1, c1, act1, h3b, z3, g_ln[1, 0], "ffn1")

    doT_h = _mm(wo, dz3b, name="attn_do", out_dtype=BF16, b_transposed=True).reshape(H, dh, M)
    grads["w_o"] = _mm(oT, dz3b, name="attn_dw_o", out_dtype=F32)[None]
    dqT_h, dcq, dkT_a, dvT_h = _attn_bwd(qT_h, oT_h, doT_h, lse, cq, k_aug, v_aug, kT_aug, name="attn_bwd")
    dck = -dkT_a[:, :, dh, :].reshape(H, M)
    dc = jnp.pad((dcq[:, 0, :] + dck).T, ((0, 0), (0, LANES - H)))
    dpre, dbias = _forget_cumsum_bwd(dc, pre, bias_f, name="attn_cumsum_bwd")
    d_att = jnp.concatenate([*_heads_merge([dkT_a, dvT_h, dqT_h], M, tm, name="attn_dqkv_merge"), dpre], axis=1)
    dz2, dz2b, dg01, db01 = _mm_ln_bwd(d_att, w_att, dz3, z2, g_ln[0, 1][None, :], name="attn_dh_ln_bwd")
    d_w_att = _mm_tn(h2b, d_att, name="attn_dw_qkv")
    grads["w_kv"] = d_w_att[:, :2 * D]
    grads["w_q"] = d_w_att[:, 2 * D:3 * D][None]
    grads["w_f"] = d_w_att[:, 3 * D:3 * D + H]
    grads["b_f"] = dbias[0, :H]

    (dz1, _, dg00, db00), dwin0, dwout0, dcw0, dcb0 = ffn_bwd(0, dz2, dz2b, u0, c0, act0, h1b, z1, g_ln[0, 0], "ffn0")
    dh0, dpw, dscale = _pool_bwd(dz1, diff, pw, scale, name="pool_bwd")

    grads["meta"] = dh0[:N_META]
    grads["pool_w"] = dpw[None]
    grads["pool_scale"] = dscale
    seg_w_in = jnp.stack([dwin0, dwin1], axis=2)
    grads["ffn_w_out"] = jnp.stack([dwout0, dwout1])
    grads["ffn_conv_w"] = jnp.stack([dcw0, dcw1])
    grads["ffn_conv_b"] = jnp.stack([dcb0, dcb1])
    grads["ln_g"] = jnp.stack([jnp.stack([dg00[0], dg01[0]]), jnp.stack([dg10[0], dg11[0]])])
    grads["ln_b"] = jnp.stack([jnp.stack([db00[0], db01[0]]), jnp.stack([db10[0], db11[0]])])
    grad_x = dh0[N_META:n_tok][None]

    blocks = {n: _to_blocks(grads[n], axis_of[n]) for n, _ in PARAMS if n != "ffn_w_in"}
    me = _coords()
    core, chip = (jnp.reshape(v, (1,)).astype(jnp.int32) for v in (me[2], 2 * me[0] + me[1]))
    by_core = lambda t: jnp.swapaxes(t.reshape((N_CHIPS, 2) + t.shape[1:]), 0, 1).astype(BF16)
    segs = [seg_w_in if n == "ffn_w_in" else by_core(blocks[n]) for n in BIG]
    pair = _exchange(segs, "grad_pair_exchange", (1,), lambda pos: pos[2])
    rows_of = lambda t, lead: t.reshape(lead + (-1, t.shape[-1]))
    partial = [_pair_sum(rows_of(s, (2,)), core, rows_of(p[0], ()), name="grad_pair_sum_" + n).reshape(s.shape[1:])
               for n, s, p in zip(BIG, segs, pair)]
    recv = _exchange(partial, "grad_chip_exchange", (4, 2, 6), lambda pos: 2 * pos[0] + pos[1])
    recv_small = _exchange([_pack([blocks[n] for n in SMALL], lead=(N_DEV,))], "grad_small_exchange",
                           tuple(range(1, N_DEV)), _index, by_sender=True)[0]
    results = {}
    for n, own, r in zip(BIG, partial, recv):
        shape = local[n].shape
        as2d = lambda t: t.reshape(-1, shape[-1])
        outs = _adamw(rows_of(own, (N_CHIPS,)), chip, rows_of(r, (N_CHIPS - 1,)), as2d(local[n]), as2d(mom1[n]),
                      as2d(mom2[n]), name="adamw_" + n)
        results[n] = [o_.reshape(shape) for o_ in outs]
    outs = _adamw(recv_small, jnp.zeros((1,), jnp.int32), recv_small[1:],
                  *[_pack([d[n] for n in SMALL]) for d in (local, mom1, mom2)], name="adamw_small")
    small_out = [_unpack(o_, [local[n].shape for n in SMALL]) for o_ in outs]
    for i, n in enumerate(SMALL):
        results[n] = [small_out[k][i] for k in range(4)]
    return (loss, grad_x, *[results[n][k] for k in range(4) for n, _ in PARAMS])
```
